```python
import jax
import jax.numpy as jnp
from jax import lax
import numpy as np

D_MODEL = 2048
BATCH = 4
SEQ = 4096
DEPTH = 2

GRID_W = 64
CTX_LEN = 256
HEAD_DIM = 128
BLOCK = 128
WINDOW = 128
ROPE_BASE = 10000.0
EPS = 1e-6
ATTN_SCALE = HEAD_DIM ** -0.5

A_Q_HEADS = 8
A_KV_HEADS = 2
A_WIDTH = A_Q_HEADS * HEAD_DIM
A_KV_WIDTH = A_KV_HEADS * HEAD_DIM
LRU_WIDTH = D_MODEL // 2
LRU_BLOCKS = 8
LRU_BLOCK_SIZE = LRU_WIDTH // LRU_BLOCKS
LRU_C = 8.0
CONV_W = 4
CONV_LEFT = 2
AB_IN = A_WIDTH + 2 * A_KV_WIDTH + 2 * LRU_WIDTH
AB_MIX = A_WIDTH + LRU_WIDTH
C_Q_HEADS = 16
C_KV_HEADS = 4
C_WIDTH = C_Q_HEADS * HEAD_DIM
C_KV_WIDTH = C_KV_HEADS * HEAD_DIM
GQA_IN = C_WIDTH + 2 * C_KV_WIDTH
N_EXPERTS = 16
N_GROUPS = 4
EXPERTS_PER_GROUP = N_EXPERTS // N_GROUPS
GROUP_SCORE_K = 2
TOP_K = 2
D_FF = 1024

N_EVEN = (DEPTH + 1) // 2
N_ODD = DEPTH // 2

kernel_name = 'hybrid_swa_rglru_gqa_grouped_moe_dit'


def rmsnorm(x, g):
    xf = x.astype(jnp.float32)
    xf = xf * lax.rsqrt(jnp.mean(xf * xf, axis=-1, keepdims=True) + EPS)
    return xf.astype(x.dtype) * g


def modulate(h, shift, scale):
    return h * (1 + scale) + shift


def axial_rope_tables(n_tokens):
    rows = n_tokens // GRID_W
    row = jnp.repeat(jnp.arange(rows, dtype=jnp.float32), GRID_W)
    col = jnp.tile(jnp.arange(GRID_W, dtype=jnp.float32), rows)
    n_freq = HEAD_DIM // 4
    inv_freq = ROPE_BASE ** (-jnp.arange(n_freq, dtype=jnp.float32) / n_freq)
    ang = jnp.concatenate([row[:, None] * inv_freq, col[:, None] * inv_freq], axis=-1)
    return jnp.cos(ang), jnp.sin(ang)


def apply_rope(x, cos, sin):
    x1, x2 = jnp.split(x, 2, axis=-1)
    c = cos[None, :, None, :].astype(x.dtype)
    s = sin[None, :, None, :].astype(x.dtype)
    return jnp.concatenate([x1 * c - x2 * s, x1 * s + x2 * c], axis=-1)


def context_attention(qc, kc, vc, sink):
    B, C, Hq, dh = qc.shape
    Hkv = kc.shape[2]
    G = Hq // Hkv
    qg = qc.reshape(B, C, Hkv, G, dh)
    s = jnp.einsum('bqhgd,bkhd->bhgqk', qg, kc).astype(jnp.float32) * ATTN_SCALE
    if sink is not None:
        s_sink = jnp.broadcast_to(sink.astype(jnp.float32).reshape(1, Hkv, G, 1, 1), s.shape[:-1] + (1,))
        s = jnp.concatenate([s, s_sink], axis=-1)
    p = jax.nn.softmax(s, axis=-1)[..., :C].astype(vc.dtype)
    out = jnp.einsum('bhgqk,bkhd->bqhgd', p, vc)
    return out.reshape(B, C, Hq * dh)


def window_attention(q, k, v, kc, vc, sink):
    B, S, Hq, dh = q.shape
    Hkv = k.shape[2]
    G = Hq // Hkv
    C = kc.shape[1]
    nb = S // BLOCK
    qb = q.reshape(B, nb, BLOCK, Hkv, G, dh)
    pad = ((0, 0), (BLOCK, BLOCK), (0, 0), (0, 0))
    kp = jnp.pad(k, pad).reshape(B, nb + 2, BLOCK, Hkv, dh)
    vp = jnp.pad(v, pad).reshape(B, nb + 2, BLOCK, Hkv, dh)
    kb = jnp.concatenate([kp[:, :-2], kp[:, 1:-1], kp[:, 2:]], axis=2)
    vb = jnp.concatenate([vp[:, :-2], vp[:, 1:-1], vp[:, 2:]], axis=2)
    s_band = jnp.einsum('bnqhgd,bnkhd->bnhgqk', qb, kb).astype(jnp.float32) * ATTN_SCALE
    qi = jnp.arange(BLOCK)[:, None]
    kj = jnp.arange(3 * BLOCK)[None, :]
    near = jnp.abs(kj - BLOCK - qi) <= WINDOW
    kpos = jnp.arange(nb)[:, None] * BLOCK - BLOCK + jnp.arange(3 * BLOCK)[None, :]
    inside = (kpos >= 0) & (kpos < S)
    valid = near[None] & inside[:, None, :]
    s_band = jnp.where(valid[None, :, None, None], s_band, -jnp.inf)
    s_ctx = jnp.einsum('bnqhgd,bchd->bnhgqc', qb, kc).astype(jnp.float32) * ATTN_SCALE
    s_sink = jnp.broadcast_to(sink.astype(jnp.float32).reshape(1, 1, Hkv, G, 1, 1), s_band.shape[:-1] + (1,))
    p = jax.nn.softmax(jnp.concatenate([s_band, s_ctx, s_sink], axis=-1), axis=-1)
    p_band = p[..., :3 * BLOCK].astype(v.dtype)
    p_ctx = p[..., 3 * BLOCK:3 * BLOCK + C].astype(v.dtype)
    out = (jnp.einsum('bnhgqk,bnkhd->bnqhgd', p_band, vb)
           + jnp.einsum('bnhgqc,bchd->bnqhgd', p_ctx, vc))
    return out.reshape(B, S, Hq * dh)


def block_attention(q, k, v, kc, vc):
    B, S, Hq, dh = q.shape
    Hkv = k.shape[2]
    G = Hq // Hkv
    nb = S // BLOCK
    k_all = jnp.concatenate([k, kc], axis=1)
    v_all = jnp.concatenate([v, vc], axis=1)
    qb = jnp.moveaxis(q.reshape(B, nb, BLOCK, Hkv, G, dh), 1, 0)

    def one_block(q_blk):
        s = jnp.einsum('bqhgd,bkhd->bhgqk', q_blk, k_all).astype(jnp.float32) * ATTN_SCALE
        p = jax.nn.softmax(s, axis=-1).astype(v_all.dtype)
        return jnp.einsum('bhgqk,bkhd->bqhgd', p, v_all)

    out = lax.map(one_block, qb)
    return jnp.moveaxis(out, 0, 1).reshape(B, S, Hq * dh)


def centred_dwconv(x, w, b):
    y = lax.conv_general_dilated(x, w[:, None, :], window_strides=(1,),
                                 padding=[(CONV_LEFT, CONV_W - 1 - CONV_LEFT)],
                                 dimension_numbers=('NWC', 'WIO', 'NWC'),
                                 feature_group_count=x.shape[-1])
    return y + b


def block_diag(x, w, b):
    B, L, _ = x.shape
    xb = x.reshape(B, L, LRU_BLOCKS, LRU_BLOCK_SIZE)
    y = jnp.einsum('blnc,ncd->blnd', xb, w.astype(jnp.float32))
    return y.reshape(B, L, LRU_WIDTH) + b.astype(jnp.float32)


def rglru_coeffs(u, wa, ba, wx, bx, lam):
    uf = u.astype(jnp.float32)
    r = jax.nn.sigmoid(block_diag(uf, wa, ba))
    i = jax.nn.sigmoid(block_diag(uf, wx, bx))
    log_a = LRU_C * r * jax.nn.log_sigmoid(lam.astype(jnp.float32))
    a = jnp.exp(log_a)
    b = jnp.sqrt(-jnp.expm1(2.0 * log_a)) * (i * uf)
    return a, b


def linear_scan(a, b, h0):
    def combine(left, right):
        return left[0] * right[0], right[0] * left[1] + right[1]
    a_cum, b_cum = lax.associative_scan(combine, (a, b), axis=1)
    return a_cum * h0[:, None, :] + b_cum


def bidir_rglru(u, uc, ga_w, ga_b, gx_w, gx_b, lam, ctx_out):
    B = u.shape[0]
    y = jnp.zeros(u.shape, jnp.float32)
    yc = jnp.zeros(uc.shape, jnp.float32)
    for d in range(2):
        a, b = rglru_coeffs(u, ga_w[d], ga_b[d], gx_w[d], gx_b[d], lam[d])
        ac, bc = rglru_coeffs(uc, ga_w[d], ga_b[d], gx_w[d], gx_b[d], lam[d])
        if d == 1:
            a, b, ac, bc = (jnp.flip(t, axis=1) for t in (a, b, ac, bc))
        hc = linear_scan(ac, bc, jnp.zeros((B, LRU_WIDTH), jnp.float32))
        hl = linear_scan(a, b, hc[:, -1])
        if d == 1:
            hc, hl = jnp.flip(hc, axis=1), jnp.flip(hl, axis=1)
        y = y + hl
        yc = yc + hc
    return y.astype(u.dtype), (yc.astype(uc.dtype) if ctx_out else None)


def mixer_swa_rglru(h, hc, w_in, q_gain, k_gain, sink, conv_w, conv_b,
                    ga_w, ga_b, gx_w, gx_b, lam, w_out, cos, sin, ctx_out):
    B, S, _ = h.shape
    C = hc.shape[1]
    cuts = [A_WIDTH, A_WIDTH + A_KV_WIDTH, A_WIDTH + 2 * A_KV_WIDTH, A_WIDTH + 2 * A_KV_WIDTH + LRU_WIDTH]
    q, k, v, xr, xg = jnp.split(h @ w_in, cuts, axis=-1)
    kc, vc, xrc = jnp.split(hc @ w_in[:, A_WIDTH:cuts[3]], [A_KV_WIDTH, 2 * A_KV_WIDTH], axis=-1)
    q = apply_rope(rmsnorm(q.reshape(B, S, A_Q_HEADS, HEAD_DIM), q_gain), cos, sin)
    k = apply_rope(rmsnorm(k.reshape(B, S, A_KV_HEADS, HEAD_DIM), k_gain), cos, sin)
    v = v.reshape(B, S, A_KV_HEADS, HEAD_DIM)
    kc = rmsnorm(kc.reshape(B, C, A_KV_HEADS, HEAD_DIM), k_gain)
    vc = vc.reshape(B, C, A_KV_HEADS, HEAD_DIM)
    att = window_attention(q, k, v, kc, vc, sink)
    u = centred_dwconv(xr, conv_w, conv_b)
    uc = centred_dwconv(xrc, conv_w, conv_b)
    yr, yrc = bidir_rglru(u, uc, ga_w, ga_b, gx_w, gx_b, lam, ctx_out)
    y = jnp.concatenate([att, yr * jax.nn.gelu(xg)], axis=-1) @ w_out
    if not ctx_out:
        return y, None
    qc = rmsnorm((hc @ w_in[:, :A_WIDTH]).reshape(B, C, A_Q_HEADS, HEAD_DIM), q_gain)
    xgc = hc @ w_in[:, cuts[3]:]
    attc = context_attention(qc, kc, vc, sink)
    yc = jnp.concatenate([attc, yrc * jax.nn.gelu(xgc)], axis=-1) @ w_out
    return y, yc


def mixer_gqa(h, hc, w_in, q_gain, k_gain, w_out, cos, sin, ctx_out):
    B, S, _ = h.shape
    C = hc.shape[1]
    q, k, v = jnp.split(h @ w_in, [C_WIDTH, C_WIDTH + C_KV_WIDTH], axis=-1)
    kc, vc = jnp.split(hc @ w_in[:, C_WIDTH:], 2, axis=-1)
    q = apply_rope(rmsnorm(q.reshape(B, S, C_Q_HEADS, HEAD_DIM), q_gain), cos, sin)
    k = apply_rope(rmsnorm(k.reshape(B, S, C_KV_HEADS, HEAD_DIM), k_gain), cos, sin)
    v = v.reshape(B, S, C_KV_HEADS, HEAD_DIM)
    kc = rmsnorm(kc.reshape(B, C, C_KV_HEADS, HEAD_DIM), k_gain)
    vc = vc.reshape(B, C, C_KV_HEADS, HEAD_DIM)
    y = block_attention(q, k, v, kc, vc) @ w_out
    if not ctx_out:
        return y, None
    qc = rmsnorm((hc @ w_in[:, :C_WIDTH]).reshape(B, C, C_Q_HEADS, HEAD_DIM), q_gain)
    yc = context_attention(qc, kc, vc, None) @ w_out
    return y, yc


def grouped_moe(h, router_w, router_bias, w_gate, w_up, w_down):
    B, L, D = h.shape
    t = h.reshape(B * L, D)
    scores = jax.nn.sigmoid((t @ router_w).astype(jnp.float32))
    biased = scores + router_bias.astype(jnp.float32)
    grp_top, _ = lax.top_k(biased.reshape(-1, N_GROUPS, EXPERTS_PER_GROUP), GROUP_SCORE_K)
    grp_sel = jnp.argmax(jnp.sum(grp_top, axis=-1), axis=-1)
    in_grp = (jnp.arange(N_EXPERTS) // EXPERTS_PER_GROUP)[None, :] == grp_sel[:, None]
    _, idx = lax.top_k(jnp.where(in_grp, biased, -jnp.inf), TOP_K)
    w = jnp.take_along_axis(scores, idx, axis=-1)
    w = w / jnp.sum(w, axis=-1, keepdims=True)
    combine = jnp.sum(jax.nn.one_hot(idx, N_EXPERTS, dtype=jnp.float32) * w[..., None], axis=1).astype(h.dtype)
    out = jnp.zeros_like(t)
    for e in range(N_EXPERTS):
        y = (jax.nn.silu(t @ w_gate[e]) * (t @ w_up[e])) @ w_down[e]
        out = out + combine[:, e:e + 1] * y
    return out.reshape(B, L, D)


def setup_inputs(seed: int = 0) -> dict:
    key = jax.random.key(seed)
    ks = jax.random.split(key, 32)
    f32 = jnp.float32
    D = D_MODEL

    def nrm(k, shape, scale):
        return jax.random.normal(k, shape, f32) * scale

    u = jax.random.uniform(ks[18], (N_EVEN, 2, LRU_WIDTH), f32, minval=0.9, maxval=0.999)
    sig = u ** (1.0 / LRU_C)
    lam = jnp.log(sig) - jnp.log1p(-sig)
    return {
        'x': nrm(ks[0], (BATCH, SEQ, D), 1.0),
        'c': nrm(ks[1], (BATCH, D), 1.0),
        'ctx': nrm(ks[2], (BATCH, CTX_LEN, D), 1.0),
        'c_ctx': nrm(ks[3], (D,), 1.0),
        'ada_w': nrm(ks[4], (DEPTH, D, 6 * D), D ** -0.5),
        'ada_b': nrm(ks[5], (DEPTH, 6 * D), 0.02),
        'norm_mix': 1.0 + nrm(ks[6], (DEPTH, D), 0.1),
        'norm_ffn': 1.0 + nrm(ks[7], (DEPTH, D), 0.1),
        'ab_w_in': nrm(ks[8], (N_EVEN, D, AB_IN), D ** -0.5),
        'ab_q_gain': 1.0 + nrm(ks[9], (N_EVEN, HEAD_DIM), 0.1),
        'ab_k_gain': 1.0 + nrm(ks[10], (N_EVEN, HEAD_DIM), 0.1),
        'ab_sink': nrm(ks[11], (N_EVEN, A_Q_HEADS), 1.0),
        'ab_conv_w': nrm(ks[12], (N_EVEN, CONV_W, LRU_WIDTH), CONV_W ** -0.5),
        'ab_conv_b': nrm(ks[13], (N_EVEN, LRU_WIDTH), 0.02),
        'ab_gate_a_w': nrm(ks[14], (N_EVEN, 2, LRU_BLOCKS, LRU_BLOCK_SIZE, LRU_BLOCK_SIZE), LRU_BLOCK_SIZE ** -0.5),
        'ab_gate_a_b': nrm(ks[15], (N_EVEN, 2, LRU_WIDTH), 0.02),
        'ab_gate_x_w': nrm(ks[16], (N_EVEN, 2, LRU_BLOCKS, LRU_BLOCK_SIZE, LRU_BLOCK_SIZE), LRU_BLOCK_SIZE ** -0.5),
        'ab_gate_x_b': nrm(ks[17], (N_EVEN, 2, LRU_WIDTH), 0.02),
        'ab_lru_lambda': lam,
        'ab_w_out': nrm(ks[19], (N_EVEN, AB_MIX, D), AB_MIX ** -0.5),
        'gqa_w_in': nrm(ks[20], (N_ODD, D, GQA_IN), D ** -0.5),
        'gqa_q_gain': 1.0 + nrm(ks[21], (N_ODD, HEAD_DIM), 0.1),
        'gqa_k_gain': 1.0 + nrm(ks[22], (N_ODD, HEAD_DIM), 0.1),
        'gqa_w_out': nrm(ks[23], (N_ODD, C_WIDTH, D), C_WIDTH ** -0.5),
        'router_w': nrm(ks[24], (D, N_EXPERTS), D ** -0.5),
        'router_bias': nrm(ks[25], (N_EXPERTS,), 0.01),
        'moe_w_gate': nrm(ks[26], (DEPTH, N_EXPERTS, D, D_FF), D ** -0.5),
        'moe_w_up': nrm(ks[27], (DEPTH, N_EXPERTS, D, D_FF), D ** -0.5),
        'moe_w_down': nrm(ks[28], (DEPTH, N_EXPERTS, D_FF, D), D_FF ** -0.5),
    }


def reference(x, c, ctx, c_ctx, ada_w, ada_b, norm_mix, norm_ffn,
              ab_w_in, ab_q_gain, ab_k_gain, ab_sink, ab_conv_w, ab_conv_b,
              ab_gate_a_w, ab_gate_a_b, ab_gate_x_w, ab_gate_x_b, ab_lru_lambda, ab_w_out,
              gqa_w_in, gqa_q_gain, gqa_k_gain, gqa_w_out,
              router_w, router_bias, moe_w_gate, moe_w_up, moe_w_down):
    n_lat = x.shape[1]
    cos, sin = axial_rope_tables(n_lat)
    s_c = jax.nn.silu(c)
    s_cc = jax.nn.silu(c_ctx)
    cx = ctx
    for l in range(DEPTH):
        last = l == DEPTH - 1
        i = l // 2
        mod = (s_c @ ada_w[l] + ada_b[l])[:, None, :]
        mod_c = s_cc @ ada_w[l] + ada_b[l]
        sh_m, sc_m, g_m, sh_f, sc_f, g_f = jnp.split(mod, 6, axis=-1)
        csh_m, csc_m, cg_m, csh_f, csc_f, cg_f = jnp.split(mod_c, 6, axis=-1)
        h = modulate(rmsnorm(x, norm_mix[l]), sh_m, sc_m)
        hc = modulate(rmsnorm(cx, norm_mix[l]), csh_m, csc_m)
        if l % 2 == 0:
            y, yc = mixer_swa_rglru(h, hc, ab_w_in[i], ab_q_gain[i], ab_k_gain[i], ab_sink[i],
                                    ab_conv_w[i], ab_conv_b[i], ab_gate_a_w[i], ab_gate_a_b[i],
                                    ab_gate_x_w[i], ab_gate_x_b[i], ab_lru_lambda[i], ab_w_out[i],
                                    cos, sin, not last)
        else:
            y, yc = mixer_gqa(h, hc, gqa_w_in[i], gqa_q_gain[i], gqa_k_gain[i], gqa_w_out[i],
                              cos, sin, not last)
        x = x + g_m * y
        h = modulate(rmsnorm(x, norm_ffn[l]), sh_f, sc_f)
        if last:
            x = x + g_f * grouped_moe(h, router_w, router_bias, moe_w_gate[l], moe_w_up[l], moe_w_down[l])
        else:
            cx = cx + cg_m * yc
            hc = modulate(rmsnorm(cx, norm_ffn[l]), csh_f, csc_f)
            f = grouped_moe(jnp.concatenate([h, hc], axis=1), router_w, router_bias,
                            moe_w_gate[l], moe_w_up[l], moe_w_down[l])
            x = x + g_f * f[:, :n_lat]
            cx = cx + cg_f * f[:, n_lat:]
    return x
```

```python
import functools

import jax
import jax.numpy as jnp
from jax import lax
from jax.experimental import pallas as pl
from jax.experimental.pallas import tpu as pltpu

F32 = jnp.float32
BF16 = jnp.bfloat16

LANES = 128
SUBLANES = 8
VMEM_LIMIT = 56 * 1024 * 1024

HEAD_DIM = 128
GRID_W = 64
WINDOW = 128
BLOCK = 128
ROPE_BASE = 10000.0
EPS = 1e-6
ATTN_SCALE = HEAD_DIM ** -0.5
A_Q_HEADS, A_KV_HEADS = 8, 2
C_Q_HEADS, C_KV_HEADS = 16, 4
GQA_GROUP = 4
LRU_C = 8.0
CONV_W = 4
CONV_LEFT = 2
N_EXPERTS = 16
N_GROUPS = 4
EXPERTS_PER_GROUP = 4
NEG_BIG = -1e30

LRU_CHUNKS = SUBLANES
LRU_JB = 16
MOE_TM = 256


def _cparams(sem, vmem=VMEM_LIMIT):
    return pltpu.CompilerParams(dimension_semantics=sem, vmem_limit_bytes=vmem)


def _dot(a, b):
    return jnp.dot(a, b, preferred_element_type=F32)


def _dot_nt(a, b):
    return lax.dot_general(a, b, (((1,), (1,)), ((), ())), preferred_element_type=F32)


def _ada_kernel(c_ref, w_ref, b_ref, o_ref):
    c = c_ref[...]
    s = (c * jax.nn.sigmoid(c)).astype(BF16)
    o_ref[0] = _dot(s, w_ref[0].astype(BF16)) + b_ref[0]


def _ada(cc, ada_w, ada_b):
    depth, d, n = ada_w.shape
    tn = 1024
    return pl.pallas_call(
        _ada_kernel,
        out_shape=jax.ShapeDtypeStruct((depth, SUBLANES, n), F32),
        grid=(depth, n // tn),
        in_specs=[pl.BlockSpec((SUBLANES, d), lambda l, j: (0, 0)),
                  pl.BlockSpec((1, d, tn), lambda l, j: (l, 0, j)),
                  pl.BlockSpec((1, 1, tn), lambda l, j: (l, 0, j))],
        out_specs=pl.BlockSpec((1, SUBLANES, tn), lambda l, j: (l, 0, j)),
        compiler_params=_cparams(("arbitrary", "arbitrary")),
        name="ada",
    )(cc, ada_w, ada_b.reshape(depth, 1, n))


def _norm_mod_rows(x_ref, g_ref, sh_ref, sc_ref, dst_ref, tm, rc=128):
    g = g_ref[...]
    sc1 = 1.0 + sc_ref[...]
    sh = sh_ref[...]

    def body(r, carry):
        rows = pl.ds(pl.multiple_of(r * rc, rc), rc)
        xf = x_ref[rows, :]
        ms = jnp.mean(xf * xf, axis=-1, keepdims=True)
        xn = (xf * lax.rsqrt(ms + EPS)) * g
        dst_ref[rows, :] = (xn * sc1 + sh).astype(dst_ref.dtype)
        return carry

    lax.fori_loop(0, tm // rc, body, 0)


def _nm_mm_kernel(x_ref, g_ref, sh_ref, sc_ref, w_ref, o_ref, hn_ref, *, tm):
    @pl.when(pl.program_id(1) == 0)
    def _():
        _norm_mod_rows(x_ref, g_ref, sh_ref, sc_ref, hn_ref, tm)

    o_ref[...] = _dot(hn_ref[...], w_ref[...]).astype(o_ref.dtype)


def _norm_mod_matmul(x, g, mod3, shift_blk, scale_blk, w, rows_per_mod, mod_base, tm=1024, tn=512):
    n, d = x.shape
    nout = w.shape[1]
    tm = min(tm, n)
    tn = min(tn, nout)
    per = rows_per_mod // tm

    def mod_row(i):
        return mod_base + i // per

    return pl.pallas_call(
        functools.partial(_nm_mm_kernel, tm=tm),
        out_shape=jax.ShapeDtypeStruct((n, nout), BF16),
        grid=(n // tm, nout // tn),
        in_specs=[pl.BlockSpec((tm, d), lambda i, j: (i, 0)),
                  pl.BlockSpec((1, d), lambda i, j: (0, 0)),
                  pl.BlockSpec((None, 1, d), lambda i, j: (mod_row(i), 0, shift_blk)),
                  pl.BlockSpec((None, 1, d), lambda i, j: (mod_row(i), 0, scale_blk)),
                  pl.BlockSpec((d, tn), lambda i, j: (0, j))],
        out_specs=pl.BlockSpec((tm, tn), lambda i, j: (i, j)),
        scratch_shapes=[pltpu.VMEM((tm, d), BF16)],
        compiler_params=_cparams(("parallel", "arbitrary")),
        name="norm_mod_matmul",
    )(x, g.reshape(1, d), mod3, mod3, w)


def _qknorm_kernel(*refs, n_heads, rope):
    if rope:
        x_ref, g_ref, cos_ref, sin_ref, o_ref = refs
        cos2 = cos_ref[...]
        sin2 = sin_ref[...]
    else:
        x_ref, g_ref, o_ref = refs
    g = g_ref[...]
    for h in range(n_heads):
        cols = slice(h * HEAD_DIM, (h + 1) * HEAD_DIM)
        xf = x_ref[:, cols].astype(F32)
        ms = jnp.mean(xf * xf, axis=-1, keepdims=True)
        xn = (xf * lax.rsqrt(ms + EPS)) * g
        if rope:
            xn = xn * cos2 + pltpu.roll(xn, HEAD_DIM // 2, 1) * sin2
        o_ref[:, cols] = xn.astype(o_ref.dtype)


def _qknorm(proj, col_blk, n_heads, gain, cos2=None, sin2=None, tr=512):
    n = proj.shape[0]
    tr = min(tr, n)
    w = n_heads * HEAD_DIM
    rope = cos2 is not None
    in_specs = [pl.BlockSpec((tr, w), lambda i: (i, col_blk)),
                pl.BlockSpec((1, HEAD_DIM), lambda i: (0, 0))]
    args = [proj, gain.reshape(1, HEAD_DIM)]
    if rope:
        per = cos2.shape[0] // tr
        in_specs += [pl.BlockSpec((tr, HEAD_DIM), lambda i: (i % per, 0)),
                     pl.BlockSpec((tr, HEAD_DIM), lambda i: (i % per, 0))]
        args += [cos2, sin2]
    return pl.pallas_call(
        functools.partial(_qknorm_kernel, n_heads=n_heads, rope=rope),
        out_shape=jax.ShapeDtypeStruct((n, w), BF16),
        grid=(n // tr,),
        in_specs=in_specs,
        out_specs=pl.BlockSpec((tr, w), lambda i: (i, 0)),
        compiler_params=_cparams(("parallel",)),
        name="qknorm",
    )(*args)


def _head_cols(h):
    return slice(h * HEAD_DIM, (h + 1) * HEAD_DIM)


def _stack_group(q_ref, kvh):
    return jnp.concatenate([q_ref[:, _head_cols(kvh * GQA_GROUP + g)] for g in range(GQA_GROUP)], axis=0)


def _sink_col(sink_ref, kvh, rows):
    return jnp.concatenate([jnp.full((rows, 1), sink_ref[kvh * GQA_GROUP + g], F32)
                            for g in range(GQA_GROUP)], axis=0)


def _win_attn_kernel(sink_ref, q_ref, kp_ref, kc_ref, kn_ref, vp_ref, vc_ref, vn_ref,
                     kx_ref, vx_ref, o_ref, *, seq):
    n = pl.program_id(1)
    rows = GQA_GROUP * BLOCK
    qi = lax.broadcasted_iota(jnp.int32, (rows, 3 * BLOCK), 0) % BLOCK
    kj = lax.broadcasted_iota(jnp.int32, (rows, 3 * BLOCK), 1)
    kpos = n * BLOCK - BLOCK + kj
    valid = (jnp.abs(kj - BLOCK - qi) <= WINDOW) & (kpos >= 0) & (kpos < seq)
    for kvh in range(A_KV_HEADS):
        cols = _head_cols(kvh)
        q4 = _stack_group(q_ref, kvh)
        kb = jnp.concatenate([kp_ref[:, cols], kc_ref[:, cols], kn_ref[:, cols]], axis=0)
        vb = jnp.concatenate([vp_ref[:, cols], vc_ref[:, cols], vn_ref[:, cols]], axis=0)
        sb = jnp.where(valid, _dot_nt(q4, kb) * ATTN_SCALE, NEG_BIG)
        sx = _dot_nt(q4, kx_ref[:, cols]) * ATTN_SCALE
        sk = _sink_col(sink_ref, kvh, BLOCK)
        m = jnp.maximum(jnp.maximum(jnp.max(sb, axis=-1, keepdims=True),
                                    jnp.max(sx, axis=-1, keepdims=True)), sk)
        pb = jnp.exp(sb - m)
        px = jnp.exp(sx - m)
        den = (jnp.sum(pb, axis=-1, keepdims=True) + jnp.sum(px, axis=-1, keepdims=True)
               + jnp.exp(sk - m))
        o = (_dot(pb.astype(BF16), vb) + _dot(px.astype(BF16), vx_ref[:, cols])) / den
        for g in range(GQA_GROUP):
            o_ref[:, _head_cols(kvh * GQA_GROUP + g)] = o[g * BLOCK:(g + 1) * BLOCK].astype(o_ref.dtype)


def _win_attn(sink, q_r, k_r, proj, kx_r, proj_c, batch, seq, ctx_len):
    nb = seq // BLOCK
    kvw = A_KV_HEADS * HEAD_DIM
    v_blk = (A_Q_HEADS * HEAD_DIM + kvw) // kvw

    def prev(b, n):
        return b * nb + jnp.maximum(n - 1, 0)

    def cur(b, n):
        return b * nb + n

    def nxt(b, n):
        return b * nb + jnp.minimum(n + 1, nb - 1)

    return pl.pallas_call(
        functools.partial(_win_attn_kernel, seq=seq),
        out_shape=jax.ShapeDtypeStruct((batch * seq, A_Q_HEADS * HEAD_DIM), BF16),
        grid=(batch, nb),
        in_specs=[pl.BlockSpec(memory_space=pltpu.SMEM),
                  pl.BlockSpec((BLOCK, A_Q_HEADS * HEAD_DIM), lambda b, n: (cur(b, n), 0)),
                  pl.BlockSpec((BLOCK, kvw), lambda b, n: (prev(b, n), 0)),
                  pl.BlockSpec((BLOCK, kvw), lambda b, n: (cur(b, n), 0)),
                  pl.BlockSpec((BLOCK, kvw), lambda b, n: (nxt(b, n), 0)),
                  pl.BlockSpec((BLOCK, kvw), lambda b, n: (prev(b, n), v_blk)),
                  pl.BlockSpec((BLOCK, kvw), lambda b, n: (cur(b, n), v_blk)),
                  pl.BlockSpec((BLOCK, kvw), lambda b, n: (nxt(b, n), v_blk)),
                  pl.BlockSpec((ctx_len, kvw), lambda b, n: (b, 0)),
                  pl.BlockSpec((ctx_len, kvw), lambda b, n: (b, v_blk))],
        out_specs=pl.BlockSpec((BLOCK, A_Q_HEADS * HEAD_DIM), lambda b, n: (cur(b, n), 0)),
        compiler_params=_cparams(("parallel", "parallel")),
        name="win_attn",
    )(sink, q_r, k_r, k_r, k_r, proj, proj, proj, kx_r, proj_c)


def _ctx_attn_kernel(sink_ref, q_ref, k_ref, v_ref, o_ref, *, ctx_len):
    kvh = pl.program_id(1)
    q4 = jnp.concatenate([q_ref[:, _head_cols(g)] for g in range(GQA_GROUP)], axis=0)
    s = _dot_nt(q4, k_ref[...]) * ATTN_SCALE
    sk = jnp.concatenate([jnp.full((ctx_len, 1), sink_ref[kvh * GQA_GROUP + g], F32)
                          for g in range(GQA_GROUP)], axis=0)
    m = jnp.maximum(jnp.max(s, axis=-1, keepdims=True), sk)
    p = jnp.exp(s - m)
    den = jnp.sum(p, axis=-1, keepdims=True) + jnp.exp(sk - m)
    o = _dot(p.astype(BF16), v_ref[...]) / den
    for g in range(GQA_GROUP):
        o_ref[:, _head_cols(g)] = o[g * ctx_len:(g + 1) * ctx_len].astype(o_ref.dtype)


def _ctx_attn(sink, qx_r, kx_r, proj_c, batch, ctx_len):
    v_blk = A_Q_HEADS + A_KV_HEADS
    gw = GQA_GROUP * HEAD_DIM
    return pl.pallas_call(
        functools.partial(_ctx_attn_kernel, ctx_len=ctx_len),
        out_shape=jax.ShapeDtypeStruct((batch * ctx_len, A_Q_HEADS * HEAD_DIM), BF16),
        grid=(batch, A_KV_HEADS),
        in_specs=[pl.BlockSpec(memory_space=pltpu.SMEM),
                  pl.BlockSpec((ctx_len, gw), lambda b, h: (b, h)),
                  pl.BlockSpec((ctx_len, HEAD_DIM), lambda b, h: (b, h)),
                  pl.BlockSpec((ctx_len, HEAD_DIM), lambda b, h: (b, v_blk + h))],
        out_specs=pl.BlockSpec((ctx_len, gw), lambda b, h: (b, h)),
        compiler_params=_cparams(("parallel", "parallel")),
        name="ctx_attn",
    )(sink, qx_r, kx_r, proj_c)


def _gelu_tanh(x):
    return 0.5 * x * (1.0 + jnp.tanh(0.7978845608028654 * (x + 0.044715 * (x * x * x))))


def _lru_sequence(x_ref, xg_ref, y_ref, xp_ref, a_ref, b_ref, w, init, rows, jb):
    conv_w, conv_b, wa, ba, wx, bx, c_logsig = w
    sub = lax.broadcasted_iota(jnp.int32, (1, SUBLANES, LANES), 1)

    def fill(r, carry):
        rr = pl.ds(pl.multiple_of(r * jb, jb), jb)
        xp_ref[pl.ds(pl.multiple_of(r * jb, jb) + CONV_LEFT, jb)] = x_ref[rr].astype(F32)
        return carry

    lax.fori_loop(0, rows // jb, fill, 0)
    tail = x_ref[rows - CONV_LEFT:rows].astype(F32)
    xp_ref[0:CONV_LEFT] = jnp.where(sub == 0, 0.0, pltpu.roll(tail, 1, 1))
    head = x_ref[0:1].astype(F32)
    xp_ref[rows + CONV_LEFT:rows + CONV_LEFT + 1] = jnp.where(
        sub == SUBLANES - 1, 0.0, pltpu.roll(head, SUBLANES - 1, 1))

    def gates(r, carry):
        j0 = pl.multiple_of(r * jb, jb)
        u = conv_b
        for k in range(CONV_W):
            u = u + conv_w[k] * xp_ref[pl.ds(j0 + k, jb)]
        u2 = u.reshape(jb * SUBLANES, LANES)
        ub = u2.astype(BF16)
        for d in range(2):
            r_gate = jax.nn.sigmoid(_dot(ub, wa[d]) + ba[d])
            i_gate = jax.nn.sigmoid(_dot(ub, wx[d]) + bx[d])
            log_a = c_logsig[d] * r_gate
            a = jnp.exp(log_a)
            b = jnp.sqrt(1.0 - a * a) * (i_gate * u2)
            a_ref[d, pl.ds(j0, jb)] = a.reshape(jb, SUBLANES, LANES)
            b_ref[d, pl.ds(j0, jb)] = b.reshape(jb, SUBLANES, LANES)
        return carry

    lax.fori_loop(0, rows // jb, gates, 0)

    def scan(j, carry):
        hf, pf, hb, pb = carry
        jr = rows - 1 - j
        af = a_ref[0, j]
        hf = af * hf + b_ref[0, j]
        pf = pf * af
        b_ref[0, j] = hf
        a_ref[0, j] = pf
        ab = a_ref[1, jr]
        hb = ab * hb + b_ref[1, jr]
        pb = pb * ab
        b_ref[1, jr] = hb
        a_ref[1, jr] = pb
        return hf, pf, hb, pb

    z = jnp.zeros((SUBLANES, LANES), F32)
    o = jnp.ones((SUBLANES, LANES), F32)
    lax.fori_loop(0, rows, scan, (z, o, z, o), unroll=8)

    hf_last, pf_last = b_ref[0, rows - 1], a_ref[0, rows - 1]
    hb_last, pb_last = b_ref[1, 0], a_ref[1, 0]
    s = init[0]
    carry_f = []
    for c in range(SUBLANES):
        carry_f.append(s)
        s = hf_last[c:c + 1] + pf_last[c:c + 1] * s
    out_f = s
    s = init[1]
    carry_b = [None] * SUBLANES
    for c in reversed(range(SUBLANES)):
        carry_b[c] = s
        s = hb_last[c:c + 1] + pb_last[c:c + 1] * s
    out_b = s
    cf = jnp.concatenate(carry_f, axis=0)
    cb = jnp.concatenate(carry_b, axis=0)

    def emit(r, carry):
        rr = pl.ds(pl.multiple_of(r * jb, jb), jb)
        h = (b_ref[0, rr] + a_ref[0, rr] * cf) + (b_ref[1, rr] + a_ref[1, rr] * cb)
        y_ref[rr] = (h * _gelu_tanh(xg_ref[rr].astype(F32))).astype(y_ref.dtype)
        return carry

    lax.fori_loop(0, rows // jb, emit, 0)
    return out_f, out_b


def _lru_kernel(xr_ref, xg_ref, xrc_ref, xgc_ref, cw_ref, cb_ref, wa_ref, ba_ref, wx_ref, bx_ref,
                lam_ref, y_ref, yc_ref, xp_ref, a_ref, b_ref, *, rows, rows_c):
    c_logsig = [LRU_C * jax.nn.log_sigmoid(lam_ref[d]) for d in range(2)]
    w = ([cw_ref[k] for k in range(CONV_W)], cb_ref[0],
         [wa_ref[d] for d in range(2)], [ba_ref[d] for d in range(2)],
         [wx_ref[d] for d in range(2)], [bx_ref[d] for d in range(2)], c_logsig)
    zero = jnp.zeros((1, LANES), F32)
    sf, sb = _lru_sequence(xrc_ref, xgc_ref, yc_ref, xp_ref, a_ref, b_ref, w, (zero, zero),
                           rows_c, min(LRU_JB, rows_c))
    _lru_sequence(xr_ref, xg_ref, y_ref, xp_ref, a_ref, b_ref, w, (sf, sb), rows, LRU_JB)


def _lru(xr, xg, xrc, xgc, conv_w, conv_b, wa, ba, wx, bx, lam):
    batch, rows, _, width = xr.shape
    rows_c = xrc.shape[1]
    nblk = width // LANES
    seq_spec = pl.BlockSpec((None, rows, SUBLANES, LANES), lambda b, n: (b, 0, 0, n))
    ctx_spec = pl.BlockSpec((None, rows_c, SUBLANES, LANES), lambda b, n: (b, 0, 0, n))
    vec2 = pl.BlockSpec((2, 1, LANES), lambda b, n: (0, 0, n))
    mat2 = pl.BlockSpec((2, None, LANES, LANES), lambda b, n: (0, n, 0, 0))
    return pl.pallas_call(
        functools.partial(_lru_kernel, rows=rows, rows_c=rows_c),
        out_shape=(jax.ShapeDtypeStruct(xr.shape, BF16), jax.ShapeDtypeStruct(xrc.shape, BF16)),
        grid=(batch, nblk),
        in_specs=[seq_spec, seq_spec, ctx_spec, ctx_spec,
                  pl.BlockSpec((CONV_W, 1, LANES), lambda b, n: (0, 0, n)),
                  pl.BlockSpec((1, 1, LANES), lambda b, n: (0, 0, n)),
                  mat2, vec2, mat2, vec2, vec2],
        out_specs=(seq_spec, ctx_spec),
        scratch_shapes=[pltpu.VMEM((rows + CONV_W - 1, SUBLANES, LANES), F32),
                        pltpu.VMEM((2, rows, SUBLANES, LANES), F32),
                        pltpu.VMEM((2, rows, SUBLANES, LANES), F32)],
        compiler_params=_cparams(("parallel", "parallel")),
        name="rglru",
    )(xr, xg, xrc, xgc, conv_w.reshape(CONV_W, 1, width), conv_b.reshape(1, 1, width),
      wa.astype(BF16), ba.reshape(2, 1, width), wx.astype(BF16), bx.reshape(2, 1, width),
      lam.reshape(2, 1, width))


def _to_chunked(a, batch):
    t = a.shape[0] // batch
    return a.reshape(batch, LRU_CHUNKS, t // LRU_CHUNKS, a.shape[1]).transpose(0, 2, 1, 3)


def _from_chunked(a):
    b, r, c, w = a.shape
    return a.transpose(0, 2, 1, 3).reshape(b * r * c, w)


def _out_proj_kernel(a1_ref, a2_ref, w1_ref, w2_ref, x_ref, g_ref, o_ref):
    y = _dot(a1_ref[...], w1_ref[...]) + _dot(a2_ref[...], w2_ref[...])
    o_ref[...] = x_ref[...] + g_ref[...] * y


def _out_proj(a1, a1_blk, a2, a2_blk, w, x, mod3, gate_blk, rows_per_mod, mod_base, tm=1024, tn=512):
    n, d = x.shape
    kh = w.shape[0] // 2
    tm = min(tm, n)
    per = rows_per_mod // tm
    gpb = d // tn
    return pl.pallas_call(
        _out_proj_kernel,
        out_shape=jax.ShapeDtypeStruct((n, d), F32),
        grid=(n // tm, d // tn),
        in_specs=[pl.BlockSpec((tm, kh), lambda i, j: (i, a1_blk)),
                  pl.BlockSpec((tm, kh), lambda i, j: (i, a2_blk)),
                  pl.BlockSpec((kh, tn), lambda i, j: (0, j)),
                  pl.BlockSpec((kh, tn), lambda i, j: (1, j)),
                  pl.BlockSpec((tm, tn), lambda i, j: (i, j)),
                  pl.BlockSpec((None, 1, tn), lambda i, j: (mod_base + i // per, 0, gate_blk * gpb + j))],
        out_specs=pl.BlockSpec((tm, tn), lambda i, j: (i, j)),
        compiler_params=_cparams(("parallel", "parallel")),
        name="out_proj",
    )(a1, a2, w, w, x, mod3)


def _flash_kernel(q_ref, k_ref, v_ref, kx_ref, vx_ref, o_ref, m_ref, l_ref, acc_ref, *, tq, seq, kc):
    q4 = jnp.concatenate([q_ref[:, _head_cols(g)] for g in range(GQA_GROUP)], axis=0)
    m_ref[...] = jnp.full(m_ref.shape, NEG_BIG, F32)
    l_ref[...] = jnp.zeros(l_ref.shape, F32)
    acc_ref[...] = jnp.zeros(acc_ref.shape, F32)

    def update(kb, vb):
        s = _dot_nt(q4, kb) * ATTN_SCALE
        m_old = m_ref[...]
        m_new = jnp.maximum(m_old, jnp.max(s, axis=-1, keepdims=True))
        alpha = jnp.exp(m_old - m_new)
        p = jnp.exp(s - m_new)
        l_ref[...] = alpha * l_ref[...] + jnp.sum(p, axis=-1, keepdims=True)
        acc_ref[...] = alpha * acc_ref[...] + _dot(p.astype(BF16), vb)
        m_ref[...] = m_new

    def body(c, carry):
        rows = pl.ds(pl.multiple_of(c * kc, kc), kc)
        update(k_ref[rows, :], v_ref[rows, :])
        return carry

    lax.fori_loop(0, seq // kc, body, 0)
    update(kx_ref[...], vx_ref[...])
    o = acc_ref[...] / l_ref[...]
    for g in range(GQA_GROUP):
        o_ref[:, _head_cols(g)] = o[g * tq:(g + 1) * tq].astype(o_ref.dtype)


def _flash_attn(q_r, k_r, proj, kx_r, proj_c, batch, seq, ctx_len, tq=256, kc=512):
    gw = GQA_GROUP * HEAD_DIM
    nq = seq // tq
    v_blk = C_Q_HEADS + C_KV_HEADS
    vx_blk = C_KV_HEADS
    return pl.pallas_call(
        functools.partial(_flash_kernel, tq=tq, seq=seq, kc=kc),
        out_shape=jax.ShapeDtypeStruct((batch * seq, C_Q_HEADS * HEAD_DIM), BF16),
        grid=(batch, C_KV_HEADS, nq),
        in_specs=[pl.BlockSpec((tq, gw), lambda b, h, i: (b * nq + i, h)),
                  pl.BlockSpec((seq, HEAD_DIM), lambda b, h, i: (b, h)),
                  pl.BlockSpec((seq, HEAD_DIM), lambda b, h, i: (b, v_blk + h)),
                  pl.BlockSpec((ctx_len, HEAD_DIM), lambda b, h, i: (b, h)),
                  pl.BlockSpec((ctx_len, HEAD_DIM), lambda b, h, i: (b, vx_blk + h))],
        out_specs=pl.BlockSpec((tq, gw), lambda b, h, i: (b * nq + i, h)),
        scratch_shapes=[pltpu.VMEM((GQA_GROUP * tq, 1), F32),
                        pltpu.VMEM((GQA_GROUP * tq, 1), F32),
                        pltpu.VMEM((GQA_GROUP * tq, HEAD_DIM), F32)],
        compiler_params=_cparams(("parallel", "parallel", "parallel")),
        name="flash_attn",
    )(q_r, k_r, proj, kx_r, proj_c)


def _router_rows(biased, scores):
    v = [biased[e:e + 1, :] for e in range(N_EXPERTS)]
    s = [scores[e:e + 1, :] for e in range(N_EXPERTS)]

    def top2_sum(vals):
        best = vals[0] + vals[1]
        for i in range(len(vals)):
            for j in range(i + 1, len(vals)):
                if (i, j) != (0, 1):
                    best = jnp.maximum(best, vals[i] + vals[j])
        return best

    gsum = [top2_sum(v[g * EXPERTS_PER_GROUP:(g + 1) * EXPERTS_PER_GROUP]) for g in range(N_GROUPS)]
    sel = jnp.zeros_like(gsum[0], dtype=jnp.int32)
    best = gsum[0]
    for g in range(1, N_GROUPS):
        take = gsum[g] > best
        sel = jnp.where(take, g, sel)
        best = jnp.where(take, gsum[g], best)

    def pick_group(rows, i):
        out = rows[i]
        for g in range(1, N_GROUPS):
            out = jnp.where(sel == g, rows[g * EXPERTS_PER_GROUP + i], out)
        return out

    cand = [pick_group(v, i) for i in range(EXPERTS_PER_GROUP)]
    cand_s = [pick_group(s, i) for i in range(EXPERTS_PER_GROUP)]
    i1 = jnp.zeros_like(sel)
    b1 = cand[0]
    for i in range(1, EXPERTS_PER_GROUP):
        take = cand[i] > b1
        i1 = jnp.where(take, i, i1)
        b1 = jnp.where(take, cand[i], b1)
    i2 = jnp.full_like(sel, -1)
    b2 = jnp.full_like(b1, -jnp.inf)
    for i in range(EXPERTS_PER_GROUP):
        take = (i1 != i) & ((cand[i] > b2) | (i2 < 0))
        i2 = jnp.where(take, i, i2)
        b2 = jnp.where(take, cand[i], b2)

    def pick_idx(rows, idx):
        out = rows[0]
        for i in range(1, EXPERTS_PER_GROUP):
            out = jnp.where(idx == i, rows[i], out)
        return out

    s0 = pick_idx(cand_s, i1)
    s1 = pick_idx(cand_s, i2)
    tot = s0 + s1
    e0 = (sel * EXPERTS_PER_GROUP + i1).astype(F32)
    e1 = (sel * EXPERTS_PER_GROUP + i2).astype(F32)
    return e0, e1, s0 / tot, s1 / tot


def _norm_router_kernel(*refs, tm, n_lat_tiles, has_ctx):
    if has_ctx:
        x_ref, xc_ref, g_ref, sh_ref, sc_ref, rw_ref, rb_ref, h_ref, r_ref = refs
        i = pl.program_id(0)

        @pl.when(i < n_lat_tiles)
        def _():
            _norm_mod_rows(x_ref, g_ref, sh_ref, sc_ref, h_ref, tm)

        @pl.when(i >= n_lat_tiles)
        def _():
            _norm_mod_rows(xc_ref, g_ref, sh_ref, sc_ref, h_ref, tm)
    else:
        x_ref, g_ref, sh_ref, sc_ref, rw_ref, rb_ref, h_ref, r_ref = refs
        _norm_mod_rows(x_ref, g_ref, sh_ref, sc_ref, h_ref, tm)

    logits = _dot_nt(rw_ref[...], h_ref[...])
    scores = jax.nn.sigmoid(logits)
    e0, e1, w0, w1 = _router_rows(scores + rb_ref[...], scores)
    zero = jnp.zeros_like(w0)
    r_ref[...] = jnp.concatenate([e0, e1, w0, w1, zero, zero, zero, zero], axis=0)


def _lat_ctx_maps(n_lat, per, ctx_mod_row):
    def mod_row(i):
        return jnp.where(i < n_lat, i // per, ctx_mod_row)

    def lat(i):
        return (jnp.minimum(i, n_lat - 1), 0)

    def ctx(i):
        return (jnp.maximum(i - n_lat, 0), 0)

    return mod_row, lat, ctx


def _norm_router(x, cx, g, mod3, shift_blk, scale_blk, rw_t, rb, rows_per_mod, ctx_mod_row, tm=512):
    n, d = x.shape
    n_lat = n // tm
    has_ctx = cx is not None
    ntot = n_lat + (cx.shape[0] // tm if has_ctx else 0)
    mod_row, lat, ctx = _lat_ctx_maps(n_lat, rows_per_mod // tm, ctx_mod_row)
    row_specs = [pl.BlockSpec((tm, d), lat)] + ([pl.BlockSpec((tm, d), ctx)] if has_ctx else [])
    row_args = [x] + ([cx] if has_ctx else [])
    return pl.pallas_call(
        functools.partial(_norm_router_kernel, tm=tm, n_lat_tiles=n_lat, has_ctx=has_ctx),
        out_shape=(jax.ShapeDtypeStruct((ntot * tm, d), BF16),
                   jax.ShapeDtypeStruct((SUBLANES, ntot * tm), F32)),
        grid=(ntot,),
        in_specs=row_specs + [
            pl.BlockSpec((1, d), lambda i: (0, 0)),
            pl.BlockSpec((None, 1, d), lambda i: (mod_row(i), 0, shift_blk)),
            pl.BlockSpec((None, 1, d), lambda i: (mod_row(i), 0, scale_blk)),
            pl.BlockSpec((N_EXPERTS, d), lambda i: (0, 0)),
            pl.BlockSpec((N_EXPERTS, 1), lambda i: (0, 0))],
        out_specs=(pl.BlockSpec((tm, d), lambda i: (i, 0)),
                   pl.BlockSpec((SUBLANES, tm), lambda i: (0, i))),
        compiler_params=_cparams(("arbitrary",)),
        name="norm_router",
    )(*row_args, g.reshape(1, d), mod3, mod3, rw_t, rb)


def _gmm_kernel(te_ref, tv_ref, xs_ref, wg_ref, wu_ref, wd_ref, o_ref):
    i = pl.program_id(0)

    @pl.when(tv_ref[i] > 0)
    def _():
        x = xs_ref[...]
        gate = _dot(x, wg_ref[0])
        up = _dot(x, wu_ref[0])
        h1 = ((gate * jax.nn.sigmoid(gate)) * up).astype(BF16)
        o_ref[...] = _dot(h1, wd_ref[0]).astype(o_ref.dtype)

    @pl.when(tv_ref[i] == 0)
    def _():
        o_ref[...] = jnp.zeros(o_ref.shape, o_ref.dtype)


def _grouped_mlp(tile_expert, tile_valid, xs, wg, wu, wd, tm=MOE_TM):
    p, d = xs.shape
    dff = wg.shape[2]
    grid_spec = pltpu.PrefetchScalarGridSpec(
        num_scalar_prefetch=2,
        grid=(p // tm,),
        in_specs=[pl.BlockSpec((tm, d), lambda i, te, tv: (i, 0)),
                  pl.BlockSpec((1, d, dff), lambda i, te, tv: (te[i], 0, 0)),
                  pl.BlockSpec((1, d, dff), lambda i, te, tv: (te[i], 0, 0)),
                  pl.BlockSpec((1, dff, d), lambda i, te, tv: (te[i], 0, 0))],
        out_specs=pl.BlockSpec((tm, d), lambda i, te, tv: (i, 0)),
    )
    return pl.pallas_call(
        _gmm_kernel,
        out_shape=jax.ShapeDtypeStruct((p, d), BF16),
        grid_spec=grid_spec,
        compiler_params=_cparams(("arbitrary",)),
        name="grouped_mlp",
    )(tile_expert, tile_valid, xs, wg, wu, wd)


def _combine_kernel(*refs, n_lat_tiles, has_ctx):
    if has_ctx:
        x_ref, xc_ref, y0_ref, y1_ref, r_ref, g_ref, o_ref, oc_ref = refs
    else:
        x_ref, y0_ref, y1_ref, r_ref, g_ref, o_ref = refs
    w0 = r_ref[:, 2:3]
    w1 = r_ref[:, 3:4]
    f = g_ref[...] * (w0 * y0_ref[...].astype(F32) + w1 * y1_ref[...].astype(F32))
    if not has_ctx:
        o_ref[...] = x_ref[...] + f
        return
    i = pl.program_id(0)

    @pl.when(i < n_lat_tiles)
    def _():
        o_ref[...] = x_ref[...] + f

    @pl.when(i >= n_lat_tiles)
    def _():
        oc_ref[...] = xc_ref[...] + f


def _combine(x, cx, yg, route_cols, mod3, gate_blk, rows_per_mod, ctx_mod_row, tm=512):
    n, d = x.shape
    n_lat = n // tm
    has_ctx = cx is not None
    ntot = n_lat + (cx.shape[0] // tm if has_ctx else 0)
    mod_row, lat, ctx = _lat_ctx_maps(n_lat, rows_per_mod // tm, ctx_mod_row)
    row_specs = [pl.BlockSpec((tm, d), lat)] + ([pl.BlockSpec((tm, d), ctx)] if has_ctx else [])
    row_args = [x] + ([cx] if has_ctx else [])
    out_shape = [jax.ShapeDtypeStruct(x.shape, F32)] + ([jax.ShapeDtypeStruct(cx.shape, F32)] if has_ctx else [])
    out = pl.pallas_call(
        functools.partial(_combine_kernel, n_lat_tiles=n_lat, has_ctx=has_ctx),
        out_shape=tuple(out_shape),
        grid=(ntot,),
        in_specs=row_specs + [
            pl.BlockSpec((tm, d), lambda i: (i, 0)),
            pl.BlockSpec((tm, d), lambda i: (ntot + i, 0)),
            pl.BlockSpec((tm, SUBLANES), lambda i: (i, 0)),
            pl.BlockSpec((None, 1, d), lambda i: (mod_row(i), 0, gate_blk))],
        out_specs=tuple(row_specs),
        compiler_params=_cparams(("arbitrary",)),
        name="moe_combine",
    )(*row_args, yg, yg, route_cols, mod3)
    return out if has_ctx else (out[0], None)


def _dispatch_plan(route, tm):
    n = route.shape[1]
    e_flat = jnp.concatenate([route[0], route[1]]).astype(jnp.int32)
    n_assign = 2 * n
    n_tiles = n_assign // tm + N_EXPERTS
    order = jnp.argsort(e_flat, stable=True).astype(jnp.int32)
    counts = jnp.zeros((N_EXPERTS,), jnp.int32).at[e_flat].add(1)
    padded = ((counts + tm - 1) // tm) * tm
    ends_p = jnp.cumsum(padded)
    starts_p = ends_p - padded
    starts_c = jnp.cumsum(counts) - counts
    e_sorted = e_flat[order]
    dest_sorted = starts_p[e_sorted] + (jnp.arange(n_assign, dtype=jnp.int32) - starts_c[e_sorted])
    src_tok = jnp.zeros((n_tiles * tm,), jnp.int32).at[dest_sorted].set(order % n)
    dest = jnp.zeros((n_assign,), jnp.int32).at[order].set(dest_sorted)
    tile_start = jnp.arange(n_tiles, dtype=jnp.int32) * tm
    tile_valid = (tile_start < ends_p[-1]).astype(jnp.int32)
    last_tile = jnp.maximum(ends_p[-1] // tm - 1, 0) * tm
    tile_expert = jnp.searchsorted(ends_p, jnp.minimum(tile_start, last_tile), side="right").astype(jnp.int32)
    tile_expert = jnp.minimum(tile_expert, N_EXPERTS - 1)
    return src_tok, dest, tile_expert, tile_valid


def _moe(x, cx, g, mod3, rw_t, rb, wg, wu, wd, rows_per_mod, ctx_mod_row):
    h, route = _norm_router(x, cx, g, mod3, 3, 4, rw_t, rb, rows_per_mod, ctx_mod_row)
    src_tok, dest, tile_expert, tile_valid = _dispatch_plan(route, MOE_TM)
    xs = jnp.take(h, src_tok, axis=0)
    ys = _grouped_mlp(tile_expert, tile_valid, xs, wg, wu, wd)
    yg = jnp.take(ys, dest, axis=0)
    route_cols = route.T
    return _combine(x, cx, yg, route_cols, mod3, 5, rows_per_mod, ctx_mod_row)


def _rope_tables(seq):
    rows = seq // GRID_W
    row = jnp.repeat(jnp.arange(rows, dtype=F32), GRID_W)
    col = jnp.tile(jnp.arange(GRID_W, dtype=F32), rows)
    n_freq = HEAD_DIM // 4
    inv_freq = ROPE_BASE ** (-jnp.arange(n_freq, dtype=F32) / n_freq)
    ang = jnp.concatenate([row[:, None] * inv_freq, col[:, None] * inv_freq], axis=-1)
    cos, sin = jnp.cos(ang), jnp.sin(ang)
    return jnp.concatenate([cos, cos], axis=-1), jnp.concatenate([-sin, sin], axis=-1)


def kernel(x, c, ctx, c_ctx, ada_w, ada_b, norm_mix, norm_ffn, ab_w_in, ab_q_gain, ab_k_gain, ab_sink, ab_conv_w, ab_conv_b, ab_gate_a_w, ab_gate_a_b, ab_gate_x_w, ab_gate_x_b, ab_lru_lambda, ab_w_out, gqa_w_in, gqa_q_gain, gqa_k_gain, gqa_w_out, router_w, router_bias, moe_w_gate, moe_w_up, moe_w_down):
    batch, seq, d = x.shape
    ctx_len = ctx.shape[1]
    depth = ada_w.shape[0]
    assert depth == 2 and batch < SUBLANES
    n_lat = batch * seq
    n_ctx = batch * ctx_len
    ctx_row = batch

    xl = x.reshape(n_lat, d)
    xc = ctx.reshape(n_ctx, d)
    cc = jnp.zeros((SUBLANES, d), F32).at[:batch].set(c).at[ctx_row].set(c_ctx)
    mod = _ada(cc, ada_w, ada_b)
    cos2, sin2 = _rope_tables(seq)
    rw_t = router_w.T.astype(BF16)
    rb = router_bias.reshape(N_EXPERTS, 1).astype(F32)

    mod3 = mod[0].reshape(SUBLANES, 1, 6 * d)
    w_in = ab_w_in[0].astype(BF16)
    proj = _norm_mod_matmul(xl, norm_mix[0], mod3, 0, 1, w_in, seq, 0)
    proj_c = _norm_mod_matmul(xc, norm_mix[0], mod3, 0, 1, w_in, n_ctx, ctx_row)
    q_r = _qknorm(proj, 0, A_Q_HEADS, ab_q_gain[0], cos2, sin2)
    k_r = _qknorm(proj, A_Q_HEADS // A_KV_HEADS, A_KV_HEADS, ab_k_gain[0], cos2, sin2)
    qx_r = _qknorm(proj_c, 0, A_Q_HEADS, ab_q_gain[0])
    kx_r = _qknorm(proj_c, A_Q_HEADS // A_KV_HEADS, A_KV_HEADS, ab_k_gain[0])
    att = _win_attn(ab_sink[0], q_r, k_r, proj, kx_r, proj_c, batch, seq, ctx_len)
    att_c = _ctx_attn(ab_sink[0], qx_r, kx_r, proj_c, batch, ctx_len)

    lru_w = ab_conv_w.shape[2]
    c0 = (A_Q_HEADS + 2 * A_KV_HEADS) * HEAD_DIM
    y_p, yc_p = _lru(_to_chunked(proj[:, c0:c0 + lru_w], batch),
                     _to_chunked(proj[:, c0 + lru_w:c0 + 2 * lru_w], batch),
                     _to_chunked(proj_c[:, c0:c0 + lru_w], batch),
                     _to_chunked(proj_c[:, c0 + lru_w:c0 + 2 * lru_w], batch),
                     ab_conv_w[0], ab_conv_b[0], ab_gate_a_w[0], ab_gate_a_b[0],
                     ab_gate_x_w[0], ab_gate_x_b[0], ab_lru_lambda[0])
    w_out = ab_w_out[0].astype(BF16)
    xl = _out_proj(att, 0, _from_chunked(y_p), 0, w_out, xl, mod3, 2, seq, 0)
    xc = _out_proj(att_c, 0, _from_chunked(yc_p), 0, w_out, xc, mod3, 2, n_ctx, ctx_row)
    xl, xc = _moe(xl, xc, norm_ffn[0], mod3, rw_t, rb, moe_w_gate[0].astype(BF16),
                  moe_w_up[0].astype(BF16), moe_w_down[0].astype(BF16), seq, ctx_row)

    mod3 = mod[1].reshape(SUBLANES, 1, 6 * d)
    w_in = gqa_w_in[0].astype(BF16)
    cw = C_Q_HEADS * HEAD_DIM
    proj = _norm_mod_matmul(xl, norm_mix[1], mod3, 0, 1, w_in, seq, 0)
    proj_c = _norm_mod_matmul(xc, norm_mix[1], mod3, 0, 1, w_in[:, cw:], n_ctx, ctx_row)
    q_r = _qknorm(proj, 0, C_Q_HEADS, gqa_q_gain[0], cos2, sin2)
    k_r = _qknorm(proj, C_Q_HEADS // C_KV_HEADS, C_KV_HEADS, gqa_k_gain[0], cos2, sin2)
    kx_r = _qknorm(proj_c, 0, C_KV_HEADS, gqa_k_gain[0])
    att = _flash_attn(q_r, k_r, proj, kx_r, proj_c, batch, seq, ctx_len)
    xl = _out_proj(att, 0, att, 1, gqa_w_out[0].astype(BF16), xl, mod3, 2, seq, 0)
    xl, _ = _moe(xl, None, norm_ffn[1], mod3, rw_t, rb, moe_w_gate[1].astype(BF16),
                 moe_w_up[1].astype(BF16), moe_w_down[1].astype(BF16), seq, ctx_row)
    return xl.reshape(batch, seq, d)
```

```python
import functools

import jax
import jax.numpy as jnp
from jax import lax
from jax.experimental import pallas as pl
from jax.experimental.pallas import tpu as pltpu

F32 = jnp.float32
BF16 = jnp.bfloat16

LANES = 128
SUBLANES = 8
VMEM_LIMIT = 56 * 1024 * 1024

HEAD_DIM = 128
GRID_W = 64
WINDOW = 128
BLOCK = 128
ROPE_BASE = 10000.0
EPS = 1e-6
ATTN_SCALE = HEAD_DIM ** -0.5
A_Q_HEADS, A_KV_HEADS = 8, 2
C_Q_HEADS, C_KV_HEADS = 16, 4
GQA_GROUP = 4
LRU_C = 8.0
CONV_W = 4
CONV_LEFT = 2
N_EXPERTS = 16
N_GROUPS = 4
EXPERTS_PER_GROUP = 4
NEG_BIG = -1e30

LRU_CHUNKS = SUBLANES
LRU_JB = 16
MOE_TM = 256


def _cparams(sem, vmem=VMEM_LIMIT):
    return pltpu.CompilerParams(dimension_semantics=sem, vmem_limit_bytes=vmem)


def _dot(a, b):
    return jnp.dot(a, b, preferred_element_type=F32)


def _dot_nt(a, b):
    return lax.dot_general(a, b, (((1,), (1,)), ((), ())), preferred_element_type=F32)


def _ada_kernel(c_ref, w_ref, b_ref, o_ref):
    c = c_ref[...]
    s = (c * jax.nn.sigmoid(c)).astype(BF16)
    o_ref[0] = _dot(s, w_ref[0].astype(BF16)) + b_ref[0]


def _ada(cc, ada_w, ada_b):
    depth, d, n = ada_w.shape
    tn = 1024
    return pl.pallas_call(
        _ada_kernel,
        out_shape=jax.ShapeDtypeStruct((depth, SUBLANES, n), F32),
        grid=(depth, n // tn),
        in_specs=[pl.BlockSpec((SUBLANES, d), lambda l, j: (0, 0)),
                  pl.BlockSpec((1, d, tn), lambda l, j: (l, 0, j)),
                  pl.BlockSpec((1, 1, tn), lambda l, j: (l, 0, j))],
        out_specs=pl.BlockSpec((1, SUBLANES, tn), lambda l, j: (l, 0, j)),
        compiler_params=_cparams(("arbitrary", "arbitrary")),
        name="ada",
    )(cc, ada_w, ada_b.reshape(depth, 1, n))


def _norm_mod_rows(x_ref, g_ref, sh_ref, sc_ref, dst_ref, tm, rc=128):
    g = g_ref[...]
    sc1 = 1.0 + sc_ref[...]
    sh = sh_ref[...]

    def body(r, carry):
        rows = pl.ds(pl.multiple_of(r * rc, rc), rc)
        xf = x_ref[rows, :]
        ms = jnp.mean(xf * xf, axis=-1, keepdims=True)
        xn = (xf * lax.rsqrt(ms + EPS)) * g
        dst_ref[rows, :] = (xn * sc1 + sh).astype(dst_ref.dtype)
        return carry

    lax.fori_loop(0, tm // rc, body, 0)


def _nm_mm_kernel(x_ref, g_ref, sh_ref, sc_ref, w_ref, o_ref, hn_ref, *, tm):
    @pl.when(pl.program_id(1) == 0)
    def _():
        _norm_mod_rows(x_ref, g_ref, sh_ref, sc_ref, hn_ref, tm)

    o_ref[...] = _dot(hn_ref[...], w_ref[...]).astype(o_ref.dtype)


def _norm_mod_matmul(x, g, mod3, shift_blk, scale_blk, w, rows_per_mod, mod_base, tm=1024, tn=512):
    n, d = x.shape
    nout = w.shape[1]
    tm = min(tm, n)
    tn = min(tn, nout)
    per = rows_per_mod // tm

    def mod_row(i):
        return mod_base + i // per

    return pl.pallas_call(
        functools.partial(_nm_mm_kernel, tm=tm),
        out_shape=jax.ShapeDtypeStruct((n, nout), BF16),
        grid=(n // tm, nout // tn),
        in_specs=[pl.BlockSpec((tm, d), lambda i, j: (i, 0)),
                  pl.BlockSpec((1, d), lambda i, j: (0, 0)),
                  pl.BlockSpec((None, 1, d), lambda i, j: (mod_row(i), 0, shift_blk)),
                  pl.BlockSpec((None, 1, d), lambda i, j: (mod_row(i), 0, scale_blk)),
                  pl.BlockSpec((d, tn), lambda i, j: (0, j))],
        out_specs=pl.BlockSpec((tm, tn), lambda i, j: (i, j)),
        scratch_shapes=[pltpu.VMEM((tm, d), BF16)],
        compiler_params=_cparams(("parallel", "arbitrary")),
        name="norm_mod_matmul",
    )(x, g.reshape(1, d), mod3, mod3, w)


def _qknorm_kernel(*refs, n_heads, rope, out_scale):
    if rope:
        x_ref, g_ref, cos_ref, sin_ref, o_ref = refs
        cos2 = cos_ref[...]
        sin2 = sin_ref[...]
    else:
        x_ref, g_ref, o_ref = refs
    g = g_ref[...]
    for h in range(n_heads):
        cols = slice(h * HEAD_DIM, (h + 1) * HEAD_DIM)
        xf = x_ref[:, cols].astype(F32)
        ms = jnp.mean(xf * xf, axis=-1, keepdims=True)
        xn = (xf * lax.rsqrt(ms + EPS)) * g
        if rope:
            xn = xn * cos2 + pltpu.roll(xn, HEAD_DIM // 2, 1) * sin2
        if out_scale != 1.0:
            xn = xn * out_scale
        o_ref[:, cols] = xn.astype(o_ref.dtype)


def _qknorm(proj, col_blk, n_heads, gain, cos2=None, sin2=None, out_scale=1.0, tr=512):
    n = proj.shape[0]
    tr = min(tr, n)
    w = n_heads * HEAD_DIM
    rope = cos2 is not None
    in_specs = [pl.BlockSpec((tr, w), lambda i: (i, col_blk)),
                pl.BlockSpec((1, HEAD_DIM), lambda i: (0, 0))]
    args = [proj, gain.reshape(1, HEAD_DIM)]
    if rope:
        per = cos2.shape[0] // tr
        in_specs += [pl.BlockSpec((tr, HEAD_DIM), lambda i: (i % per, 0)),
                     pl.BlockSpec((tr, HEAD_DIM), lambda i: (i % per, 0))]
        args += [cos2, sin2]
    return pl.pallas_call(
        functools.partial(_qknorm_kernel, n_heads=n_heads, rope=rope, out_scale=out_scale),
        out_shape=jax.ShapeDtypeStruct((n, w), BF16),
        grid=(n // tr,),
        in_specs=in_specs,
        out_specs=pl.BlockSpec((tr, w), lambda i: (i, 0)),
        compiler_params=_cparams(("parallel",)),
        name="qknorm",
    )(*args)


def _head_cols(h):
    return slice(h * HEAD_DIM, (h + 1) * HEAD_DIM)


def _stack_group(q_ref, kvh):
    return jnp.concatenate([q_ref[:, _head_cols(kvh * GQA_GROUP + g)] for g in range(GQA_GROUP)], axis=0)


def _sink_col(sink_ref, kvh, rows):
    return jnp.concatenate([jnp.full((rows, 1), sink_ref[kvh * GQA_GROUP + g], F32)
                            for g in range(GQA_GROUP)], axis=0)


def _win_attn_kernel(sink_ref, q_ref, kp_ref, kc_ref, kn_ref, vp_ref, vc_ref, vn_ref,
                     kx_ref, vx_ref, o_ref, *, seq, ctx_len):
    n = pl.program_id(1)
    rows = GQA_GROUP * BLOCK
    nk = 3 * BLOCK + ctx_len
    qi = lax.broadcasted_iota(jnp.int32, (rows, nk), 0) % BLOCK
    kj = lax.broadcasted_iota(jnp.int32, (rows, nk), 1)
    kpos = n * BLOCK - BLOCK + kj
    in_band = (jnp.abs(kj - BLOCK - qi) <= WINDOW) & (kpos >= 0) & (kpos < seq)
    valid = (kj >= 3 * BLOCK) | in_band
    ones = jnp.ones((nk, HEAD_DIM), BF16)
    for kvh in range(A_KV_HEADS):
        cols = _head_cols(kvh)
        q4 = _stack_group(q_ref, kvh)
        ka = jnp.concatenate([kp_ref[:, cols], kc_ref[:, cols], kn_ref[:, cols], kx_ref[:, cols]], axis=0)
        va = jnp.concatenate([vp_ref[:, cols], vc_ref[:, cols], vn_ref[:, cols], vx_ref[:, cols]], axis=0)
        s = jnp.where(valid, _dot_nt(q4, ka), NEG_BIG)
        sk = _sink_col(sink_ref, kvh, BLOCK)
        m = jnp.maximum(jnp.max(s, axis=-1, keepdims=True), sk)
        p = jnp.exp((s - m).astype(BF16))
        acc = _dot(p, jnp.concatenate([va, ones], axis=1))
        o = acc[:, 0:HEAD_DIM] / (acc[:, HEAD_DIM:] + jnp.exp(sk - m))
        for g in range(GQA_GROUP):
            o_ref[:, _head_cols(kvh * GQA_GROUP + g)] = o[g * BLOCK:(g + 1) * BLOCK].astype(o_ref.dtype)


def _win_attn(sink, q_r, k_r, proj, kx_r, proj_c, batch, seq, ctx_len):
    nb = seq // BLOCK
    kvw = A_KV_HEADS * HEAD_DIM
    v_blk = (A_Q_HEADS * HEAD_DIM + kvw) // kvw

    def prev(b, n):
        return b * nb + jnp.maximum(n - 1, 0)

    def cur(b, n):
        return b * nb + n

    def nxt(b, n):
        return b * nb + jnp.minimum(n + 1, nb - 1)

    return pl.pallas_call(
        functools.partial(_win_attn_kernel, seq=seq, ctx_len=ctx_len),
        out_shape=jax.ShapeDtypeStruct((batch * seq, A_Q_HEADS * HEAD_DIM), BF16),
        grid=(batch, nb),
        in_specs=[pl.BlockSpec(memory_space=pltpu.SMEM),
                  pl.BlockSpec((BLOCK, A_Q_HEADS * HEAD_DIM), lambda b, n: (cur(b, n), 0)),
                  pl.BlockSpec((BLOCK, kvw), lambda b, n: (prev(b, n), 0)),
                  pl.BlockSpec((BLOCK, kvw), lambda b, n: (cur(b, n), 0)),
                  pl.BlockSpec((BLOCK, kvw), lambda b, n: (nxt(b, n), 0)),
                  pl.BlockSpec((BLOCK, kvw), lambda b, n: (prev(b, n), v_blk)),
                  pl.BlockSpec((BLOCK, kvw), lambda b, n: (cur(b, n), v_blk)),
                  pl.BlockSpec((BLOCK, kvw), lambda b, n: (nxt(b, n), v_blk)),
                  pl.BlockSpec((ctx_len, kvw), lambda b, n: (b, 0)),
                  pl.BlockSpec((ctx_len, kvw), lambda b, n: (b, v_blk))],
        out_specs=pl.BlockSpec((BLOCK, A_Q_HEADS * HEAD_DIM), lambda b, n: (cur(b, n), 0)),
        compiler_params=_cparams(("parallel", "parallel")),
        name="win_attn",
    )(sink, q_r, k_r, k_r, k_r, proj, proj, proj, kx_r, proj_c)


def _ctx_attn_kernel(sink_ref, q_ref, k_ref, v_ref, o_ref, *, ctx_len):
    kvh = pl.program_id(1)
    q4 = jnp.concatenate([q_ref[:, _head_cols(g)] for g in range(GQA_GROUP)], axis=0)
    s = _dot_nt(q4, k_ref[...])
    sk = jnp.concatenate([jnp.full((ctx_len, 1), sink_ref[kvh * GQA_GROUP + g], F32)
                          for g in range(GQA_GROUP)], axis=0)
    m = jnp.maximum(jnp.max(s, axis=-1, keepdims=True), sk)
    p = jnp.exp(s - m)
    den = jnp.sum(p, axis=-1, keepdims=True) + jnp.exp(sk - m)
    o = _dot(p.astype(BF16), v_ref[...]) / den
    for g in range(GQA_GROUP):
        o_ref[:, _head_cols(g)] = o[g * ctx_len:(g + 1) * ctx_len].astype(o_ref.dtype)


def _ctx_attn(sink, qx_r, kx_r, proj_c, batch, ctx_len):
    v_blk = A_Q_HEADS + A_KV_HEADS
    gw = GQA_GROUP * HEAD_DIM
    return pl.pallas_call(
        functools.partial(_ctx_attn_kernel, ctx_len=ctx_len),
        out_shape=jax.ShapeDtypeStruct((batch * ctx_len, A_Q_HEADS * HEAD_DIM), BF16),
        grid=(batch, A_KV_HEADS),
        in_specs=[pl.BlockSpec(memory_space=pltpu.SMEM),
                  pl.BlockSpec((ctx_len, gw), lambda b, h: (b, h)),
                  pl.BlockSpec((ctx_len, HEAD_DIM), lambda b, h: (b, h)),
                  pl.BlockSpec((ctx_len, HEAD_DIM), lambda b, h: (b, v_blk + h))],
        out_specs=pl.BlockSpec((ctx_len, gw), lambda b, h: (b, h)),
        compiler_params=_cparams(("parallel", "parallel")),
        name="ctx_attn",
    )(sink, qx_r, kx_r, proj_c)


def _gelu_tanh(x):
    return 0.5 * x * (1.0 + jnp.tanh(0.7978845608028654 * (x + 0.044715 * (x * x * x))))


def _lru_sequence(x_ref, xg_ref, y_ref, xp_ref, a_ref, b_ref, w, init, rows, jb):
    conv_w, conv_b, wa, ba, wx, bx, c_logsig = w
    sub = lax.broadcasted_iota(jnp.int32, (1, SUBLANES, LANES), 1)

    def fill(r, carry):
        rr = pl.ds(pl.multiple_of(r * jb, jb), jb)
        xp_ref[pl.ds(pl.multiple_of(r * jb, jb) + CONV_LEFT, jb)] = x_ref[rr].astype(F32)
        return carry

    lax.fori_loop(0, rows // jb, fill, 0)
    tail = x_ref[rows - CONV_LEFT:rows].astype(F32)
    xp_ref[0:CONV_LEFT] = jnp.where(sub == 0, 0.0, pltpu.roll(tail, 1, 1))
    head = x_ref[0:1].astype(F32)
    xp_ref[rows + CONV_LEFT:rows + CONV_LEFT + 1] = jnp.where(
        sub == SUBLANES - 1, 0.0, pltpu.roll(head, SUBLANES - 1, 1))

    def gates(r, carry):
        j0 = pl.multiple_of(r * jb, jb)
        u = conv_b
        for k in range(CONV_W):
            u = u + conv_w[k] * xp_ref[pl.ds(j0 + k, jb)]
        u2 = u.reshape(jb * SUBLANES, LANES)
        ub = u2.astype(BF16)
        for d in range(2):
            r_gate = jax.nn.sigmoid(_dot(ub, wa[d]) + ba[d])
            i_gate = jax.nn.sigmoid(_dot(ub, wx[d]) + bx[d])
            log_a = c_logsig[d] * r_gate
            a = jnp.exp(log_a)
            b = jnp.sqrt(1.0 - a * a) * (i_gate * u2)
            a_ref[d, pl.ds(j0, jb)] = a.reshape(jb, SUBLANES, LANES)
            b_ref[d, pl.ds(j0, jb)] = b.reshape(jb, SUBLANES, LANES)
        return carry

    lax.fori_loop(0, rows // jb, gates, 0)

    def scan(j, carry):
        hf, pf, hb, pb = carry
        jr = rows - 1 - j
        af = a_ref[0, j]
        hf = af * hf + b_ref[0, j]
        pf = pf * af
        b_ref[0, j] = hf
        a_ref[0, j] = pf
        ab = a_ref[1, jr]
        hb = ab * hb + b_ref[1, jr]
        pb = pb * ab
        b_ref[1, jr] = hb
        a_ref[1, jr] = pb
        return hf, pf, hb, pb

    z = jnp.zeros((SUBLANES, LANES), F32)
    o = jnp.ones((SUBLANES, LANES), F32)
    lax.fori_loop(0, rows, scan, (z, o, z, o), unroll=8)

    hf_last, pf_last = b_ref[0, rows - 1], a_ref[0, rows - 1]
    hb_last, pb_last = b_ref[1, 0], a_ref[1, 0]
    s = init[0]
    carry_f = []
    for c in range(SUBLANES):
        carry_f.append(s)
        s = hf_last[c:c + 1] + pf_last[c:c + 1] * s
    out_f = s
    s = init[1]
    carry_b = [None] * SUBLANES
    for c in reversed(range(SUBLANES)):
        carry_b[c] = s
        s = hb_last[c:c + 1] + pb_last[c:c + 1] * s
    out_b = s
    cf = jnp.concatenate(carry_f, axis=0)
    cb = jnp.concatenate(carry_b, axis=0)

    def emit(r, carry):
        rr = pl.ds(pl.multiple_of(r * jb, jb), jb)
        h = (b_ref[0, rr] + a_ref[0, rr] * cf) + (b_ref[1, rr] + a_ref[1, rr] * cb)
        y_ref[rr] = (h * _gelu_tanh(xg_ref[rr].astype(F32))).astype(y_ref.dtype)
        return carry

    lax.fori_loop(0, rows // jb, emit, 0)
    return out_f, out_b


def _lru_kernel(xr_ref, xg_ref, xrc_ref, xgc_ref, cw_ref, cb_ref, wa_ref, ba_ref, wx_ref, bx_ref,
                lam_ref, y_ref, yc_ref, xp_ref, a_ref, b_ref, *, rows, rows_c):
    c_logsig = [LRU_C * jax.nn.log_sigmoid(lam_ref[d]) for d in range(2)]
    w = ([cw_ref[k] for k in range(CONV_W)], cb_ref[0],
         [wa_ref[d] for d in range(2)], [ba_ref[d] for d in range(2)],
         [wx_ref[d] for d in range(2)], [bx_ref[d] for d in range(2)], c_logsig)
    zero = jnp.zeros((1, LANES), F32)
    sf, sb = _lru_sequence(xrc_ref, xgc_ref, yc_ref, xp_ref, a_ref, b_ref, w, (zero, zero),
                           rows_c, min(LRU_JB, rows_c))
    _lru_sequence(xr_ref, xg_ref, y_ref, xp_ref, a_ref, b_ref, w, (sf, sb), rows, LRU_JB)


def _lru(xr, xg, xrc, xgc, conv_w, conv_b, wa, ba, wx, bx, lam):
    batch, rows, _, width = xr.shape
    rows_c = xrc.shape[1]
    nblk = width // LANES
    seq_spec = pl.BlockSpec((None, rows, SUBLANES, LANES), lambda b, n: (b, 0, 0, n))
    ctx_spec = pl.BlockSpec((None, rows_c, SUBLANES, LANES), lambda b, n: (b, 0, 0, n))
    vec2 = pl.BlockSpec((2, 1, LANES), lambda b, n: (0, 0, n))
    mat2 = pl.BlockSpec((2, None, LANES, LANES), lambda b, n: (0, n, 0, 0))
    return pl.pallas_call(
        functools.partial(_lru_kernel, rows=rows, rows_c=rows_c),
        out_shape=(jax.ShapeDtypeStruct(xr.shape, BF16), jax.ShapeDtypeStruct(xrc.shape, BF16)),
        grid=(batch, nblk),
        in_specs=[seq_spec, seq_spec, ctx_spec, ctx_spec,
                  pl.BlockSpec((CONV_W, 1, LANES), lambda b, n: (0, 0, n)),
                  pl.BlockSpec((1, 1, LANES), lambda b, n: (0, 0, n)),
                  mat2, vec2, mat2, vec2, vec2],
        out_specs=(seq_spec, ctx_spec),
        scratch_shapes=[pltpu.VMEM((rows + CONV_W - 1, SUBLANES, LANES), F32),
                        pltpu.VMEM((2, rows, SUBLANES, LANES), F32),
                        pltpu.VMEM((2, rows, SUBLANES, LANES), F32)],
        compiler_params=_cparams(("parallel", "parallel")),
        name="rglru",
    )(xr, xg, xrc, xgc, conv_w.reshape(CONV_W, 1, width), conv_b.reshape(1, 1, width),
      wa.astype(BF16), ba.reshape(2, 1, width), wx.astype(BF16), bx.reshape(2, 1, width),
      lam.reshape(2, 1, width))


def _to_chunked(a, batch):
    t = a.shape[0] // batch
    return a.reshape(batch, LRU_CHUNKS, t // LRU_CHUNKS, a.shape[1]).transpose(0, 2, 1, 3)


def _from_chunked(a):
    b, r, c, w = a.shape
    return a.transpose(0, 2, 1, 3).reshape(b * r * c, w)


def _out_proj_kernel(a1_ref, a2_ref, w1_ref, w2_ref, x_ref, g_ref, o_ref):
    y = _dot(a1_ref[...], w1_ref[...]) + _dot(a2_ref[...], w2_ref[...])
    o_ref[...] = x_ref[...] + g_ref[...] * y


def _out_proj(a1, a1_blk, a2, a2_blk, w, x, mod3, gate_blk, rows_per_mod, mod_base, tm=1024, tn=512):
    n, d = x.shape
    kh = w.shape[0] // 2
    tm = min(tm, n)
    per = rows_per_mod // tm
    gpb = d // tn
    return pl.pallas_call(
        _out_proj_kernel,
        out_shape=jax.ShapeDtypeStruct((n, d), F32),
        grid=(n // tm, d // tn),
        in_specs=[pl.BlockSpec((tm, kh), lambda i, j: (i, a1_blk)),
                  pl.BlockSpec((tm, kh), lambda i, j: (i, a2_blk)),
                  pl.BlockSpec((kh, tn), lambda i, j: (0, j)),
                  pl.BlockSpec((kh, tn), lambda i, j: (1, j)),
                  pl.BlockSpec((tm, tn), lambda i, j: (i, j)),
                  pl.BlockSpec((None, 1, tn), lambda i, j: (mod_base + i // per, 0, gate_blk * gpb + j))],
        out_specs=pl.BlockSpec((tm, tn), lambda i, j: (i, j)),
        compiler_params=_cparams(("parallel", "parallel")),
        name="out_proj",
    )(a1, a2, w, w, x, mod3)


def _lane_block_max(s):
    mm = s[:, 0:LANES]
    for t in range(1, s.shape[1] // LANES):
        mm = jnp.maximum(mm, s[:, t * LANES:(t + 1) * LANES])
    return mm


def _exp_blocks(s, mrep):
    return jnp.concatenate(
        [jnp.exp((s[:, t * LANES:(t + 1) * LANES] - mrep).astype(BF16)) for t in range(s.shape[1] // LANES)],
        axis=1)


def _dense_attn_kernel(q_ref, k_ref, v_ref, kx_ref, vx_ref, o_ref,
                       s_ref, sx_ref, m_ref, acc_ref, va_ref, vax_ref, *, tq, seq, kc):
    n_chunks = seq // kc

    @pl.when(pl.program_id(2) == 0)
    def _():
        for c in range(n_chunks):
            va_ref[c, :, 0:HEAD_DIM] = v_ref[c * kc:(c + 1) * kc, :]
            va_ref[c, :, HEAD_DIM:] = jnp.ones((kc, HEAD_DIM), BF16)
        vax_ref[:, 0:HEAD_DIM] = vx_ref[...]
        vax_ref[:, HEAD_DIM:] = jnp.ones((vx_ref.shape[0], HEAD_DIM), BF16)

    q4 = jnp.concatenate([q_ref[:, _head_cols(g)] for g in range(GQA_GROUP)], axis=0)
    sx = _dot_nt(q4, kx_ref[...])
    sx_ref[...] = sx
    m_ref[...] = _lane_block_max(sx)

    def sweep1(c, carry):
        s = _dot_nt(q4, k_ref[pl.ds(pl.multiple_of(c * kc, kc), kc), :])
        s_ref[c] = s
        m_ref[...] = jnp.maximum(m_ref[...], _lane_block_max(s))
        return carry

    lax.fori_loop(0, n_chunks, sweep1, 0)
    mrep = jnp.broadcast_to(jnp.max(m_ref[...], axis=-1, keepdims=True), m_ref.shape)
    m_ref[...] = mrep
    acc_ref[...] = _dot(_exp_blocks(sx_ref[...], mrep), vax_ref[...])

    def sweep2(c, carry):
        acc_ref[...] += _dot(_exp_blocks(s_ref[c], m_ref[...]), va_ref[c])
        return carry

    lax.fori_loop(0, n_chunks, sweep2, 0)
    o = acc_ref[:, 0:HEAD_DIM] / acc_ref[:, HEAD_DIM:]
    for g in range(GQA_GROUP):
        o_ref[:, _head_cols(g)] = o[g * tq:(g + 1) * tq].astype(o_ref.dtype)


def _dense_attn(q_r, k_r, proj, kx_r, proj_c, batch, seq, ctx_len, tq=256, kc=1024):
    gw = GQA_GROUP * HEAD_DIM
    nq = seq // tq
    rows = GQA_GROUP * tq
    v_blk = C_Q_HEADS + C_KV_HEADS
    vx_blk = C_KV_HEADS
    return pl.pallas_call(
        functools.partial(_dense_attn_kernel, tq=tq, seq=seq, kc=kc),
        out_shape=jax.ShapeDtypeStruct((batch * seq, C_Q_HEADS * HEAD_DIM), BF16),
        grid=(batch, C_KV_HEADS, nq),
        in_specs=[pl.BlockSpec((tq, gw), lambda b, h, i: (b * nq + i, h)),
                  pl.BlockSpec((seq, HEAD_DIM), lambda b, h, i: (b, h)),
                  pl.BlockSpec((seq, HEAD_DIM), lambda b, h, i: (b, v_blk + h)),
                  pl.BlockSpec((ctx_len, HEAD_DIM), lambda b, h, i: (b, h)),
                  pl.BlockSpec((ctx_len, HEAD_DIM), lambda b, h, i: (b, vx_blk + h))],
        out_specs=pl.BlockSpec((tq, gw), lambda b, h, i: (b * nq + i, h)),
        scratch_shapes=[pltpu.VMEM((seq // kc, rows, kc), F32),
                        pltpu.VMEM((rows, ctx_len), F32),
                        pltpu.VMEM((rows, LANES), F32),
                        pltpu.VMEM((rows, 2 * HEAD_DIM), F32),
                        pltpu.VMEM((seq // kc, kc, 2 * HEAD_DIM), BF16),
                        pltpu.VMEM((ctx_len, 2 * HEAD_DIM), BF16)],
        compiler_params=_cparams(("parallel", "parallel", "arbitrary")),
        name="dense_attn",
    )(q_r, k_r, proj, kx_r, proj_c)


def _router_rows(biased, scores):
    v = [biased[e:e + 1, :] for e in range(N_EXPERTS)]
    s = [scores[e:e + 1, :] for e in range(N_EXPERTS)]

    def top2_sum(vals):
        best = vals[0] + vals[1]
        for i in range(len(vals)):
            for j in range(i + 1, len(vals)):
                if (i, j) != (0, 1):
                    best = jnp.maximum(best, vals[i] + vals[j])
        return best

    gsum = [top2_sum(v[g * EXPERTS_PER_GROUP:(g + 1) * EXPERTS_PER_GROUP]) for g in range(N_GROUPS)]
    sel = jnp.zeros_like(gsum[0], dtype=jnp.int32)
    best = gsum[0]
    for g in range(1, N_GROUPS):
        take = gsum[g] > best
        sel = jnp.where(take, g, sel)
        best = jnp.where(take, gsum[g], best)

    def pick_group(rows, i):
        out = rows[i]
        for g in range(1, N_GROUPS):
            out = jnp.where(sel == g, rows[g * EXPERTS_PER_GROUP + i], out)
        return out

    cand = [pick_group(v, i) for i in range(EXPERTS_PER_GROUP)]
    cand_s = [pick_group(s, i) for i in range(EXPERTS_PER_GROUP)]
    i1 = jnp.zeros_like(sel)
    b1 = cand[0]
    for i in range(1, EXPERTS_PER_GROUP):
        take = cand[i] > b1
        i1 = jnp.where(take, i, i1)
        b1 = jnp.where(take, cand[i], b1)
    i2 = jnp.full_like(sel, -1)
    b2 = jnp.full_like(b1, -jnp.inf)
    for i in range(EXPERTS_PER_GROUP):
        take = (i1 != i) & ((cand[i] > b2) | (i2 < 0))
        i2 = jnp.where(take, i, i2)
        b2 = jnp.where(take, cand[i], b2)

    def pick_idx(rows, idx):
        out = rows[0]
        for i in range(1, EXPERTS_PER_GROUP):
            out = jnp.where(idx == i, rows[i], out)
        return out

    s0 = pick_idx(cand_s, i1)
    s1 = pick_idx(cand_s, i2)
    tot = s0 + s1
    e0 = (sel * EXPERTS_PER_GROUP + i1).astype(F32)
    e1 = (sel * EXPERTS_PER_GROUP + i2).astype(F32)
    return e0, e1, s0 / tot, s1 / tot


def _norm_router_kernel(*refs, tm, n_lat_tiles, has_ctx):
    if has_ctx:
        x_ref, xc_ref, g_ref, sh_ref, sc_ref, rw_ref, rb_ref, h_ref, r_ref = refs
        i = pl.program_id(0)

        @pl.when(i < n_lat_tiles)
        def _():
            _norm_mod_rows(x_ref, g_ref, sh_ref, sc_ref, h_ref, tm)

        @pl.when(i >= n_lat_tiles)
        def _():
            _norm_mod_rows(xc_ref, g_ref, sh_ref, sc_ref, h_ref, tm)
    else:
        x_ref, g_ref, sh_ref, sc_ref, rw_ref, rb_ref, h_ref, r_ref = refs
        _norm_mod_rows(x_ref, g_ref, sh_ref, sc_ref, h_ref, tm)

    logits = _dot_nt(rw_ref[...], h_ref[...])
    scores = jax.nn.sigmoid(logits)
    e0, e1, w0, w1 = _router_rows(scores + rb_ref[...], scores)
    zero = jnp.zeros_like(w0)
    r_ref[...] = jnp.concatenate([e0, e1, w0, w1, zero, zero, zero, zero], axis=0)


def _lat_ctx_maps(n_lat, per, ctx_mod_row):
    def mod_row(i):
        return jnp.where(i < n_lat, i // per, ctx_mod_row)

    def lat(i):
        return (jnp.minimum(i, n_lat - 1), 0)

    def ctx(i):
        return (jnp.maximum(i - n_lat, 0), 0)

    return mod_row, lat, ctx


def _norm_router(x, cx, g, mod3, shift_blk, scale_blk, rw_t, rb, rows_per_mod, ctx_mod_row, tm=512):
    n, d = x.shape
    n_lat = n // tm
    has_ctx = cx is not None
    ntot = n_lat + (cx.shape[0] // tm if has_ctx else 0)
    mod_row, lat, ctx = _lat_ctx_maps(n_lat, rows_per_mod // tm, ctx_mod_row)
    row_specs = [pl.BlockSpec((tm, d), lat)] + ([pl.BlockSpec((tm, d), ctx)] if has_ctx else [])
    row_args = [x] + ([cx] if has_ctx else [])
    return pl.pallas_call(
        functools.partial(_norm_router_kernel, tm=tm, n_lat_tiles=n_lat, has_ctx=has_ctx),
        out_shape=(jax.ShapeDtypeStruct((ntot * tm, d), BF16),
                   jax.ShapeDtypeStruct((SUBLANES, ntot * tm), F32)),
        grid=(ntot,),
        in_specs=row_specs + [
            pl.BlockSpec((1, d), lambda i: (0, 0)),
            pl.BlockSpec((None, 1, d), lambda i: (mod_row(i), 0, shift_blk)),
            pl.BlockSpec((None, 1, d), lambda i: (mod_row(i), 0, scale_blk)),
            pl.BlockSpec((N_EXPERTS, d), lambda i: (0, 0)),
            pl.BlockSpec((N_EXPERTS, 1), lambda i: (0, 0))],
        out_specs=(pl.BlockSpec((tm, d), lambda i: (i, 0)),
                   pl.BlockSpec((SUBLANES, tm), lambda i: (0, i))),
        compiler_params=_cparams(("arbitrary",)),
        name="norm_router",
    )(*row_args, g.reshape(1, d), mod3, mod3, rw_t, rb)


def _gmm_kernel(te_ref, tv_ref, xs_ref, wg_ref, wu_ref, wd_ref, o_ref):
    i = pl.program_id(0)

    @pl.when(tv_ref[i] > 0)
    def _():
        x = xs_ref[...]
        gate = _dot(x, wg_ref[0])
        up = _dot(x, wu_ref[0])
        h1 = ((gate * jax.nn.sigmoid(gate)) * up).astype(BF16)
        o_ref[...] = _dot(h1, wd_ref[0]).astype(o_ref.dtype)

    @pl.when(tv_ref[i] == 0)
    def _():
        o_ref[...] = jnp.zeros(o_ref.shape, o_ref.dtype)


def _grouped_mlp(tile_expert, tile_valid, xs, wg, wu, wd, tm=MOE_TM):
    p, d = xs.shape
    dff = wg.shape[2]
    grid_spec = pltpu.PrefetchScalarGridSpec(
        num_scalar_prefetch=2,
        grid=(p // tm,),
        in_specs=[pl.BlockSpec((tm, d), lambda i, te, tv: (i, 0)),
                  pl.BlockSpec((1, d, dff), lambda i, te, tv: (te[i], 0, 0)),
                  pl.BlockSpec((1, d, dff), lambda i, te, tv: (te[i], 0, 0)),
                  pl.BlockSpec((1, dff, d), lambda i, te, tv: (te[i], 0, 0))],
        out_specs=pl.BlockSpec((tm, d), lambda i, te, tv: (i, 0)),
    )
    return pl.pallas_call(
        _gmm_kernel,
        out_shape=jax.ShapeDtypeStruct((p, d), BF16),
        grid_spec=grid_spec,
        compiler_params=_cparams(("arbitrary",)),
        name="grouped_mlp",
    )(tile_expert, tile_valid, xs, wg, wu, wd)


def _combine_kernel(*refs, n_lat_tiles, has_ctx):
    if has_ctx:
        x_ref, xc_ref, y0_ref, y1_ref, r_ref, g_ref, o_ref, oc_ref = refs
    else:
        x_ref, y0_ref, y1_ref, r_ref, g_ref, o_ref = refs
    w0 = r_ref[:, 2:3]
    w1 = r_ref[:, 3:4]
    f = g_ref[...] * (w0 * y0_ref[...].astype(F32) + w1 * y1_ref[...].astype(F32))
    if not has_ctx:
        o_ref[...] = x_ref[...] + f
        return
    i = pl.program_id(0)

    @pl.when(i < n_lat_tiles)
    def _():
        o_ref[...] = x_ref[...] + f

    @pl.when(i >= n_lat_tiles)
    def _():
        oc_ref[...] = xc_ref[...] + f


def _combine(x, cx, yg, route_cols, mod3, gate_blk, rows_per_mod, ctx_mod_row, tm=512):
    n, d = x.shape
    n_lat = n // tm
    has_ctx = cx is not None
    ntot = n_lat + (cx.shape[0] // tm if has_ctx else 0)
    mod_row, lat, ctx = _lat_ctx_maps(n_lat, rows_per_mod // tm, ctx_mod_row)
    row_specs = [pl.BlockSpec((tm, d), lat)] + ([pl.BlockSpec((tm, d), ctx)] if has_ctx else [])
    row_args = [x] + ([cx] if has_ctx else [])
    out_shape = [jax.ShapeDtypeStruct(x.shape, F32)] + ([jax.ShapeDtypeStruct(cx.shape, F32)] if has_ctx else [])
    out = pl.pallas_call(
        functools.partial(_combine_kernel, n_lat_tiles=n_lat, has_ctx=has_ctx),
        out_shape=tuple(out_shape),
        grid=(ntot,),
        in_specs=row_specs + [
            pl.BlockSpec((tm, d), lambda i: (i, 0)),
            pl.BlockSpec((tm, d), lambda i: (ntot + i, 0)),
            pl.BlockSpec((tm, SUBLANES), lambda i: (i, 0)),
            pl.BlockSpec((None, 1, d), lambda i: (mod_row(i), 0, gate_blk))],
        out_specs=tuple(row_specs),
        compiler_params=_cparams(("arbitrary",)),
        name="moe_combine",
    )(*row_args, yg, yg, route_cols, mod3)
    return out if has_ctx else (out[0], None)


def _dispatch_plan(route, tm):
    n = route.shape[1]
    e_flat = jnp.concatenate([route[0], route[1]]).astype(jnp.int32)
    n_assign = 2 * n
    n_tiles = n_assign // tm + N_EXPERTS
    experts = jnp.arange(N_EXPERTS, dtype=jnp.int32)[:, None]
    onehot = (experts == e_flat[None, :]).astype(jnp.int32)
    csum = jnp.cumsum(onehot, axis=1)
    counts = csum[:, -1]
    padded = ((counts + tm - 1) // tm) * tm
    ends_p = jnp.cumsum(padded)
    starts_p = ends_p - padded
    starts_c = jnp.cumsum(counts) - counts
    dest = jnp.sum(onehot * (csum - 1 + starts_p[:, None]), axis=0)
    order = jnp.argsort(e_flat, stable=True).astype(jnp.int32)
    p = jnp.arange(n_tiles * tm, dtype=jnp.int32)[None, :]
    owner = ((p >= starts_p[:, None]) & (p < ends_p[:, None])).astype(jnp.int32)
    within = jnp.sum(owner * (p - starts_p[:, None]), axis=0)
    live = jnp.sum(owner * (p - starts_p[:, None] < counts[:, None]), axis=0) > 0
    compact = jnp.sum(owner * starts_c[:, None], axis=0) + within
    src_tok = jnp.where(live, order[jnp.clip(compact, 0, n_assign - 1)] % n, 0)
    tile_start = jnp.arange(n_tiles, dtype=jnp.int32) * tm
    tile_valid = (tile_start < ends_p[-1]).astype(jnp.int32)
    last_tile = jnp.maximum(ends_p[-1] // tm - 1, 0) * tm
    tile_expert = jnp.searchsorted(ends_p, jnp.minimum(tile_start, last_tile), side="right").astype(jnp.int32)
    tile_expert = jnp.minimum(tile_expert, N_EXPERTS - 1)
    return src_tok, dest, tile_expert, tile_valid


def _moe(x, cx, g, mod3, rw_t, rb, wg, wu, wd, rows_per_mod, ctx_mod_row):
    h, route = _norm_router(x, cx, g, mod3, 3, 4, rw_t, rb, rows_per_mod, ctx_mod_row)
    src_tok, dest, tile_expert, tile_valid = _dispatch_plan(route, MOE_TM)
    xs = h.at[src_tok].get(mode="promise_in_bounds")
    ys = _grouped_mlp(tile_expert, tile_valid, xs, wg, wu, wd)
    yg = ys.at[dest].get(mode="promise_in_bounds")
    route_cols = route.T
    return _combine(x, cx, yg, route_cols, mod3, 5, rows_per_mod, ctx_mod_row)


def _rope_tables(seq):
    rows = seq // GRID_W
    row = jnp.repeat(jnp.arange(rows, dtype=F32), GRID_W)
    col = jnp.tile(jnp.arange(GRID_W, dtype=F32), rows)
    n_freq = HEAD_DIM // 4
    inv_freq = ROPE_BASE ** (-jnp.arange(n_freq, dtype=F32) / n_freq)
    ang = jnp.concatenate([row[:, None] * inv_freq, col[:, None] * inv_freq], axis=-1)
    cos, sin = jnp.cos(ang), jnp.sin(ang)
    return jnp.concatenate([cos, cos], axis=-1), jnp.concatenate([-sin, sin], axis=-1)


def kernel(x, c, ctx, c_ctx, ada_w, ada_b, norm_mix, norm_ffn, ab_w_in, ab_q_gain, ab_k_gain, ab_sink, ab_conv_w, ab_conv_b, ab_gate_a_w, ab_gate_a_b, ab_gate_x_w, ab_gate_x_b, ab_lru_lambda, ab_w_out, gqa_w_in, gqa_q_gain, gqa_k_gain, gqa_w_out, router_w, router_bias, moe_w_gate, moe_w_up, moe_w_down):
    batch, seq, d = x.shape
    ctx_len = ctx.shape[1]
    depth = ada_w.shape[0]
    assert depth == 2 and batch < SUBLANES
    n_lat = batch * seq
    n_ctx = batch * ctx_len
    ctx_row = batch

    xl = x.reshape(n_lat, d)
    xc = ctx.reshape(n_ctx, d)
    cc = jnp.zeros((SUBLANES, d), F32).at[:batch].set(c).at[ctx_row].set(c_ctx)
    mod = _ada(cc, ada_w, ada_b)
    cos2, sin2 = _rope_tables(seq)
    rw_t = router_w.T.astype(BF16)
    rb = router_bias.reshape(N_EXPERTS, 1).astype(F32)

    mod3 = mod[0].reshape(SUBLANES, 1, 6 * d)
    w_in = ab_w_in[0].astype(BF16)
    proj = _norm_mod_matmul(xl, norm_mix[0], mod3, 0, 1, w_in, seq, 0)
    proj_c = _norm_mod_matmul(xc, norm_mix[0], mod3, 0, 1, w_in, n_ctx, ctx_row)
    q_r = _qknorm(proj, 0, A_Q_HEADS, ab_q_gain[0], cos2, sin2, out_scale=ATTN_SCALE)
    k_r = _qknorm(proj, A_Q_HEADS // A_KV_HEADS, A_KV_HEADS, ab_k_gain[0], cos2, sin2)
    qx_r = _qknorm(proj_c, 0, A_Q_HEADS, ab_q_gain[0], out_scale=ATTN_SCALE)
    kx_r = _qknorm(proj_c, A_Q_HEADS // A_KV_HEADS, A_KV_HEADS, ab_k_gain[0])
    att = _win_attn(ab_sink[0], q_r, k_r, proj, kx_r, proj_c, batch, seq, ctx_len)
    att_c = _ctx_attn(ab_sink[0], qx_r, kx_r, proj_c, batch, ctx_len)

    lru_w = ab_conv_w.shape[2]
    c0 = (A_Q_HEADS + 2 * A_KV_HEADS) * HEAD_DIM
    y_p, yc_p = _lru(_to_chunked(proj[:, c0:c0 + lru_w], batch),
                     _to_chunked(proj[:, c0 + lru_w:c0 + 2 * lru_w], batch),
                     _to_chunked(proj_c[:, c0:c0 + lru_w], batch),
                     _to_chunked(proj_c[:, c0 + lru_w:c0 + 2 * lru_w], batch),
                     ab_conv_w[0], ab_conv_b[0], ab_gate_a_w[0], ab_gate_a_b[0],
                     ab_gate_x_w[0], ab_gate_x_b[0], ab_lru_lambda[0])
    w_out = ab_w_out[0].astype(BF16)
    xl = _out_proj(att, 0, _from_chunked(y_p), 0, w_out, xl, mod3, 2, seq, 0)
    xc = _out_proj(att_c, 0, _from_chunked(yc_p), 0, w_out, xc, mod3, 2, n_ctx, ctx_row)
    xl, xc = _moe(xl, xc, norm_ffn[0], mod3, rw_t, rb, moe_w_gate[0].astype(BF16),
                  moe_w_up[0].astype(BF16), moe_w_down[0].astype(BF16), seq, ctx_row)

    mod3 = mod[1].reshape(SUBLANES, 1, 6 * d)
    w_in = gqa_w_in[0].astype(BF16)
    cw = C_Q_HEADS * HEAD_DIM
    proj = _norm_mod_matmul(xl, norm_mix[1], mod3, 0, 1, w_in, seq, 0)
    proj_c = _norm_mod_matmul(xc, norm_mix[1], mod3, 0, 1, w_in[:, cw:], n_ctx, ctx_row)
    q_r = _qknorm(proj, 0, C_Q_HEADS, gqa_q_gain[0], cos2, sin2, out_scale=ATTN_SCALE)
    k_r = _qknorm(proj, C_Q_HEADS // C_KV_HEADS, C_KV_HEADS, gqa_k_gain[0], cos2, sin2)
    kx_r = _qknorm(proj_c, 0, C_KV_HEADS, gqa_k_gain[0])
    att = _dense_attn(q_r, k_r, proj, kx_r, proj_c, batch, seq, ctx_len)
    xl = _out_proj(att, 0, att, 1, gqa_w_out[0].astype(BF16), xl, mod3, 2, seq, 0)
    xl, _ = _moe(xl, None, norm_ffn[1], mod3, rw_t, rb, moe_w_gate[1].astype(BF16),
                 moe_w_up[1].astype(BF16), moe_w_down[1].astype(BF16), seq, ctx_row)
    return xl.reshape(batch, seq, d)
```

```python
import functools

import jax
import jax.numpy as jnp
from jax import lax
from jax.experimental import pallas as pl
from jax.experimental.pallas import tpu as pltpu

F32 = jnp.float32
BF16 = jnp.bfloat16

LANES = 128
SUBLANES = 8
VMEM_LIMIT = 56 * 1024 * 1024

HEAD_DIM = 128
GRID_W = 64
WINDOW = 128
BLOCK = 128
ROPE_BASE = 10000.0
EPS = 1e-6
ATTN_SCALE = HEAD_DIM ** -0.5
A_Q_HEADS, A_KV_HEADS = 8, 2
C_Q_HEADS, C_KV_HEADS = 16, 4
GQA_GROUP = 4
LRU_C = 8.0
CONV_W = 4
CONV_LEFT = 2
N_EXPERTS = 16
N_GROUPS = 4
EXPERTS_PER_GROUP = 4
NEG_BIG = -1e30

LRU_CHUNKS = SUBLANES
LRU_JB = 8
LRU_LANE_BLOCKS = 2
MOE_TM = 256


def _cparams(sem, vmem=VMEM_LIMIT):
    return pltpu.CompilerParams(dimension_semantics=sem, vmem_limit_bytes=vmem)


def _dot(a, b):
    return jnp.dot(a, b, preferred_element_type=F32)


def _dot_nt(a, b):
    return lax.dot_general(a, b, (((1,), (1,)), ((), ())), preferred_element_type=F32)


def _cast_kernel(x_ref, o_ref):
    o_ref[...] = x_ref[...].astype(o_ref.dtype)


def _cast_bf16(w, block_elems=2 * 1024 * 1024):
    shape = w.shape
    cols = shape[-1]
    w2 = w.reshape(-1, cols)
    tr = min(block_elems // cols, w2.shape[0])
    out = pl.pallas_call(
        _cast_kernel,
        out_shape=jax.ShapeDtypeStruct(w2.shape, BF16),
        grid=(w2.shape[0] // tr,),
        in_specs=[pl.BlockSpec((tr, cols), lambda i: (i, 0))],
        out_specs=pl.BlockSpec((tr, cols), lambda i: (i, 0)),
        compiler_params=_cparams(("parallel",)),
        name="cast_bf16",
    )(w2)
    return out.reshape(shape)


def _ada_kernel(c_ref, w_ref, b_ref, o_ref):
    c = c_ref[...]
    s = (c * jax.nn.sigmoid(c)).astype(BF16)
    o_ref[0] = _dot(s, w_ref[0].astype(BF16)) + b_ref[0]


def _ada(cc, ada_w, ada_b):
    depth, d, n = ada_w.shape
    tn = 1024
    return pl.pallas_call(
        _ada_kernel,
        out_shape=jax.ShapeDtypeStruct((depth, SUBLANES, n), F32),
        grid=(depth, n // tn),
        in_specs=[pl.BlockSpec((SUBLANES, d), lambda l, j: (0, 0)),
                  pl.BlockSpec((1, d, tn), lambda l, j: (l, 0, j)),
                  pl.BlockSpec((1, 1, tn), lambda l, j: (l, 0, j))],
        out_specs=pl.BlockSpec((1, SUBLANES, tn), lambda l, j: (l, 0, j)),
        compiler_params=_cparams(("arbitrary", "arbitrary")),
        name="ada",
    )(cc, ada_w, ada_b.reshape(depth, 1, n))


def _norm_mod_rows(x_ref, g_ref, sh_ref, sc_ref, dst_ref, tm, rc=128):
    g = g_ref[...]
    sc1 = 1.0 + sc_ref[...]
    sh = sh_ref[...]

    def body(r, carry):
        rows = pl.ds(pl.multiple_of(r * rc, rc), rc)
        xf = x_ref[rows, :]
        ms = jnp.mean(xf * xf, axis=-1, keepdims=True)
        xn = (xf * lax.rsqrt(ms + EPS)) * g
        dst_ref[rows, :] = (xn * sc1 + sh).astype(dst_ref.dtype)
        return carry

    lax.fori_loop(0, tm // rc, body, 0)


def _nm_mm_kernel(x_ref, g_ref, sh_ref, sc_ref, w_ref, o_ref, hn_ref, *, tm):
    @pl.when(pl.program_id(1) == 0)
    def _():
        _norm_mod_rows(x_ref, g_ref, sh_ref, sc_ref, hn_ref, tm)

    o_ref[...] = _dot(hn_ref[...], w_ref[...]).astype(o_ref.dtype)


def _norm_mod_matmul(x, g, mod3, shift_blk, scale_blk, w, rows_per_mod, mod_base, tm=1024, tn=512):
    n, d = x.shape
    nout = w.shape[1]
    tm = min(tm, n)
    tn = min(tn, nout)
    per = rows_per_mod // tm

    def mod_row(i):
        return mod_base + i // per

    return pl.pallas_call(
        functools.partial(_nm_mm_kernel, tm=tm),
        out_shape=jax.ShapeDtypeStruct((n, nout), BF16),
        grid=(n // tm, nout // tn),
        in_specs=[pl.BlockSpec((tm, d), lambda i, j: (i, 0)),
                  pl.BlockSpec((1, d), lambda i, j: (0, 0)),
                  pl.BlockSpec((None, 1, d), lambda i, j: (mod_row(i), 0, shift_blk)),
                  pl.BlockSpec((None, 1, d), lambda i, j: (mod_row(i), 0, scale_blk)),
                  pl.BlockSpec((d, tn), lambda i, j: (0, j))],
        out_specs=pl.BlockSpec((tm, tn), lambda i, j: (i, j)),
        scratch_shapes=[pltpu.VMEM((tm, d), BF16)],
        compiler_params=_cparams(("parallel", "arbitrary")),
        name="norm_mod_matmul",
    )(x, g.reshape(1, d), mod3, mod3, w)


def _qknorm_kernel(*refs, n_heads, rope, out_scale):
    if rope:
        x_ref, g_ref, cos_ref, sin_ref, o_ref = refs
        cos2 = cos_ref[...]
        sin2 = sin_ref[...]
    else:
        x_ref, g_ref, o_ref = refs
    g = g_ref[...]
    for h in range(n_heads):
        cols = slice(h * HEAD_DIM, (h + 1) * HEAD_DIM)
        xf = x_ref[:, cols].astype(F32)
        ms = jnp.mean(xf * xf, axis=-1, keepdims=True)
        xn = (xf * lax.rsqrt(ms + EPS)) * g
        if rope:
            xn = xn * cos2 + pltpu.roll(xn, HEAD_DIM // 2, 1) * sin2
        if out_scale != 1.0:
            xn = xn * out_scale
        o_ref[:, cols] = xn.astype(o_ref.dtype)


def _qknorm(proj, col_blk, n_heads, gain, cos2=None, sin2=None, out_scale=1.0, tr=512):
    n = proj.shape[0]
    tr = min(tr, n)
    w = n_heads * HEAD_DIM
    rope = cos2 is not None
    in_specs = [pl.BlockSpec((tr, w), lambda i: (i, col_blk)),
                pl.BlockSpec((1, HEAD_DIM), lambda i: (0, 0))]
    args = [proj, gain.reshape(1, HEAD_DIM)]
    if rope:
        per = cos2.shape[0] // tr
        in_specs += [pl.BlockSpec((tr, HEAD_DIM), lambda i: (i % per, 0)),
                     pl.BlockSpec((tr, HEAD_DIM), lambda i: (i % per, 0))]
        args += [cos2, sin2]
    return pl.pallas_call(
        functools.partial(_qknorm_kernel, n_heads=n_heads, rope=rope, out_scale=out_scale),
        out_shape=jax.ShapeDtypeStruct((n, w), BF16),
        grid=(n // tr,),
        in_specs=in_specs,
        out_specs=pl.BlockSpec((tr, w), lambda i: (i, 0)),
        compiler_params=_cparams(("parallel",)),
        name="qknorm",
    )(*args)


def _head_cols(h):
    return slice(h * HEAD_DIM, (h + 1) * HEAD_DIM)


def _stack_group(q_ref, kvh):
    return jnp.concatenate([q_ref[:, _head_cols(kvh * GQA_GROUP + g)] for g in range(GQA_GROUP)], axis=0)


def _sink_col(sink_ref, kvh, rows):
    return jnp.concatenate([jnp.full((rows, 1), sink_ref[kvh * GQA_GROUP + g], F32)
                            for g in range(GQA_GROUP)], axis=0)


def _win_attn_kernel(sink_ref, q_ref, kp_ref, kc_ref, kn_ref, vp_ref, vc_ref, vn_ref,
                     kx_ref, vx_ref, o_ref, *, seq, ctx_len):
    n = pl.program_id(1)
    rows = GQA_GROUP * BLOCK
    nk = 3 * BLOCK + ctx_len
    qi = lax.broadcasted_iota(jnp.int32, (rows, nk), 0) % BLOCK
    kj = lax.broadcasted_iota(jnp.int32, (rows, nk), 1)
    kpos = n * BLOCK - BLOCK + kj
    in_band = (jnp.abs(kj - BLOCK - qi) <= WINDOW) & (kpos >= 0) & (kpos < seq)
    valid = (kj >= 3 * BLOCK) | in_band
    ones = jnp.ones((nk, HEAD_DIM), BF16)
    for kvh in range(A_KV_HEADS):
        cols = _head_cols(kvh)
        q4 = _stack_group(q_ref, kvh)
        ka = jnp.concatenate([kp_ref[:, cols], kc_ref[:, cols], kn_ref[:, cols], kx_ref[:, cols]], axis=0)
        va = jnp.concatenate([vp_ref[:, cols], vc_ref[:, cols], vn_ref[:, cols], vx_ref[:, cols]], axis=0)
        s = jnp.where(valid, _dot_nt(q4, ka), NEG_BIG)
        sk = _sink_col(sink_ref, kvh, BLOCK)
        m = jnp.maximum(jnp.max(s, axis=-1, keepdims=True), sk)
        p = jnp.exp((s - m).astype(BF16))
        acc = _dot(p, jnp.concatenate([va, ones], axis=1))
        o = acc[:, 0:HEAD_DIM] / (acc[:, HEAD_DIM:] + jnp.exp(sk - m))
        for g in range(GQA_GROUP):
            o_ref[:, _head_cols(kvh * GQA_GROUP + g)] = o[g * BLOCK:(g + 1) * BLOCK].astype(o_ref.dtype)


def _win_attn(sink, q_r, k_r, proj, kx_r, proj_c, batch, seq, ctx_len):
    nb = seq // BLOCK
    kvw = A_KV_HEADS * HEAD_DIM
    v_blk = (A_Q_HEADS * HEAD_DIM + kvw) // kvw

    def prev(b, n):
        return b * nb + jnp.maximum(n - 1, 0)

    def cur(b, n):
        return b * nb + n

    def nxt(b, n):
        return b * nb + jnp.minimum(n + 1, nb - 1)

    return pl.pallas_call(
        functools.partial(_win_attn_kernel, seq=seq, ctx_len=ctx_len),
        out_shape=jax.ShapeDtypeStruct((batch * seq, A_Q_HEADS * HEAD_DIM), BF16),
        grid=(batch, nb),
        in_specs=[pl.BlockSpec(memory_space=pltpu.SMEM),
                  pl.BlockSpec((BLOCK, A_Q_HEADS * HEAD_DIM), lambda b, n: (cur(b, n), 0)),
                  pl.BlockSpec((BLOCK, kvw), lambda b, n: (prev(b, n), 0)),
                  pl.BlockSpec((BLOCK, kvw), lambda b, n: (cur(b, n), 0)),
                  pl.BlockSpec((BLOCK, kvw), lambda b, n: (nxt(b, n), 0)),
                  pl.BlockSpec((BLOCK, kvw), lambda b, n: (prev(b, n), v_blk)),
                  pl.BlockSpec((BLOCK, kvw), lambda b, n: (cur(b, n), v_blk)),
                  pl.BlockSpec((BLOCK, kvw), lambda b, n: (nxt(b, n), v_blk)),
                  pl.BlockSpec((ctx_len, kvw), lambda b, n: (b, 0)),
                  pl.BlockSpec((ctx_len, kvw), lambda b, n: (b, v_blk))],
        out_specs=pl.BlockSpec((BLOCK, A_Q_HEADS * HEAD_DIM), lambda b, n: (cur(b, n), 0)),
        compiler_params=_cparams(("parallel", "parallel")),
        name="win_attn",
    )(sink, q_r, k_r, k_r, k_r, proj, proj, proj, kx_r, proj_c)


def _ctx_attn_kernel(sink_ref, q_ref, k_ref, v_ref, o_ref, *, ctx_len):
    kvh = pl.program_id(1)
    q4 = jnp.concatenate([q_ref[:, _head_cols(g)] for g in range(GQA_GROUP)], axis=0)
    s = _dot_nt(q4, k_ref[...])
    sk = jnp.concatenate([jnp.full((ctx_len, 1), sink_ref[kvh * GQA_GROUP + g], F32)
                          for g in range(GQA_GROUP)], axis=0)
    m = jnp.maximum(jnp.max(s, axis=-1, keepdims=True), sk)
    p = jnp.exp(s - m)
    den = jnp.sum(p, axis=-1, keepdims=True) + jnp.exp(sk - m)
    o = _dot(p.astype(BF16), v_ref[...]) / den
    for g in range(GQA_GROUP):
        o_ref[:, _head_cols(g)] = o[g * ctx_len:(g + 1) * ctx_len].astype(o_ref.dtype)


def _ctx_attn(sink, qx_r, kx_r, proj_c, batch, ctx_len):
    v_blk = A_Q_HEADS + A_KV_HEADS
    gw = GQA_GROUP * HEAD_DIM
    return pl.pallas_call(
        functools.partial(_ctx_attn_kernel, ctx_len=ctx_len),
        out_shape=jax.ShapeDtypeStruct((batch * ctx_len, A_Q_HEADS * HEAD_DIM), BF16),
        grid=(batch, A_KV_HEADS),
        in_specs=[pl.BlockSpec(memory_space=pltpu.SMEM),
                  pl.BlockSpec((ctx_len, gw), lambda b, h: (b, h)),
                  pl.BlockSpec((ctx_len, HEAD_DIM), lambda b, h: (b, h)),
                  pl.BlockSpec((ctx_len, HEAD_DIM), lambda b, h: (b, v_blk + h))],
        out_specs=pl.BlockSpec((ctx_len, gw), lambda b, h: (b, h)),
        compiler_params=_cparams(("parallel", "parallel")),
        name="ctx_attn",
    )(sink, qx_r, kx_r, proj_c)


def _sigmoid(x):
    return 0.5 * jnp.tanh(0.5 * x) + 0.5


def _gelu_tanh(x):
    return 0.5 * x * (1.0 + jnp.tanh(0.7978845608028654 * (x + 0.044715 * (x * x * x))))


def _lru_sequence(x_ref, xg_ref, y_ref, xp_ref, a_ref, b_ref, w, init, rows, jb):
    conv_w, conv_b, wa, ba, wx, bx, c_logsig = w
    width = x_ref.shape[-1]
    sub = lax.broadcasted_iota(jnp.int32, (1, SUBLANES, width), 1)

    def block_diag(ub, wd):
        return jnp.concatenate([_dot(ub[:, _head_cols(n)], wd[n]) for n in range(width // LANES)], axis=1)

    def fill(r, carry):
        rr = pl.ds(pl.multiple_of(r * jb, jb), jb)
        xp_ref[pl.ds(pl.multiple_of(r * jb, jb) + CONV_LEFT, jb)] = x_ref[rr].astype(F32)
        return carry

    lax.fori_loop(0, rows // jb, fill, 0)
    tail = x_ref[rows - CONV_LEFT:rows].astype(F32)
    xp_ref[0:CONV_LEFT] = jnp.where(sub == 0, 0.0, pltpu.roll(tail, 1, 1))
    head = x_ref[0:1].astype(F32)
    xp_ref[rows + CONV_LEFT:rows + CONV_LEFT + 1] = jnp.where(
        sub == SUBLANES - 1, 0.0, pltpu.roll(head, SUBLANES - 1, 1))

    def gates(r, carry):
        j0 = pl.multiple_of(r * jb, jb)
        u = conv_b
        for k in range(CONV_W):
            u = u + conv_w[k] * xp_ref[pl.ds(j0 + k, jb)]
        u2 = u.reshape(jb * SUBLANES, width)
        ub = u2.astype(BF16)
        for d in range(2):
            r_gate = _sigmoid(block_diag(ub, wa[d]) + ba[d])
            i_gate = _sigmoid(block_diag(ub, wx[d]) + bx[d])
            log_a = c_logsig[d] * r_gate
            a = jnp.exp(log_a)
            b = jnp.sqrt(1.0 - a * a) * (i_gate * u2)
            a_ref[d, pl.ds(j0, jb)] = a.reshape(jb, SUBLANES, width)
            b_ref[d, pl.ds(j0, jb)] = b.reshape(jb, SUBLANES, width)
        return carry

    lax.fori_loop(0, rows // jb, gates, 0)

    def scan(j, carry):
        hf, pf, hb, pb = carry
        jr = rows - 1 - j
        af = a_ref[0, j]
        hf = af * hf + b_ref[0, j]
        pf = pf * af
        b_ref[0, j] = hf
        a_ref[0, j] = pf
        ab = a_ref[1, jr]
        hb = ab * hb + b_ref[1, jr]
        pb = pb * ab
        b_ref[1, jr] = hb
        a_ref[1, jr] = pb
        return hf, pf, hb, pb

    z = jnp.zeros((SUBLANES, width), F32)
    o = jnp.ones((SUBLANES, width), F32)
    lax.fori_loop(0, rows, scan, (z, o, z, o), unroll=8)

    hf_last, pf_last = b_ref[0, rows - 1], a_ref[0, rows - 1]
    hb_last, pb_last = b_ref[1, 0], a_ref[1, 0]
    s = init[0]
    carry_f = []
    for c in range(SUBLANES):
        carry_f.append(s)
        s = hf_last[c:c + 1] + pf_last[c:c + 1] * s
    out_f = s
    s = init[1]
    carry_b = [None] * SUBLANES
    for c in reversed(range(SUBLANES)):
        carry_b[c] = s
        s = hb_last[c:c + 1] + pb_last[c:c + 1] * s
    out_b = s
    cf = jnp.concatenate(carry_f, axis=0)
    cb = jnp.concatenate(carry_b, axis=0)

    def emit(r, carry):
        rr = pl.ds(pl.multiple_of(r * jb, jb), jb)
        h = (b_ref[0, rr] + a_ref[0, rr] * cf) + (b_ref[1, rr] + a_ref[1, rr] * cb)
        y_ref[rr] = (h * _gelu_tanh(xg_ref[rr].astype(F32))).astype(y_ref.dtype)
        return carry

    lax.fori_loop(0, rows // jb, emit, 0)
    return out_f, out_b


def _lru_kernel(xr_ref, xg_ref, xrc_ref, xgc_ref, cw_ref, cb_ref, wa_ref, ba_ref, wx_ref, bx_ref,
                lam_ref, y_ref, yc_ref, xp_ref, a_ref, b_ref, *, rows, rows_c):
    c_logsig = [LRU_C * jax.nn.log_sigmoid(lam_ref[d]) for d in range(2)]
    w = ([cw_ref[k] for k in range(CONV_W)], cb_ref[0],
         [wa_ref[d] for d in range(2)], [ba_ref[d] for d in range(2)],
         [wx_ref[d] for d in range(2)], [bx_ref[d] for d in range(2)], c_logsig)
    zero = jnp.zeros((1, xr_ref.shape[-1]), F32)
    sf, sb = _lru_sequence(xrc_ref, xgc_ref, yc_ref, xp_ref, a_ref, b_ref, w, (zero, zero),
                           rows_c, min(LRU_JB, rows_c))
    _lru_sequence(xr_ref, xg_ref, y_ref, xp_ref, a_ref, b_ref, w, (sf, sb), rows, LRU_JB)


def _lru(xr, xg, xrc, xgc, conv_w, conv_b, wa, ba, wx, bx, lam):
    batch, rows, _, width = xr.shape
    rows_c = xrc.shape[1]
    cw = LRU_LANE_BLOCKS * LANES
    seq_spec = pl.BlockSpec((None, rows, SUBLANES, cw), lambda b, n: (b, 0, 0, n))
    ctx_spec = pl.BlockSpec((None, rows_c, SUBLANES, cw), lambda b, n: (b, 0, 0, n))
    vec2 = pl.BlockSpec((2, 1, cw), lambda b, n: (0, 0, n))
    mat2 = pl.BlockSpec((2, LRU_LANE_BLOCKS, LANES, LANES), lambda b, n: (0, n, 0, 0))
    return pl.pallas_call(
        functools.partial(_lru_kernel, rows=rows, rows_c=rows_c),
        out_shape=(jax.ShapeDtypeStruct(xr.shape, BF16), jax.ShapeDtypeStruct(xrc.shape, BF16)),
        grid=(batch, width // cw),
        in_specs=[seq_spec, seq_spec, ctx_spec, ctx_spec,
                  pl.BlockSpec((CONV_W, 1, cw), lambda b, n: (0, 0, n)),
                  pl.BlockSpec((1, 1, cw), lambda b, n: (0, 0, n)),
                  mat2, vec2, mat2, vec2, vec2],
        out_specs=(seq_spec, ctx_spec),
        scratch_shapes=[pltpu.VMEM((rows + CONV_W - 1, SUBLANES, cw), F32),
                        pltpu.VMEM((2, rows, SUBLANES, cw), F32),
                        pltpu.VMEM((2, rows, SUBLANES, cw), F32)],
        compiler_params=_cparams(("parallel", "parallel")),
        name="rglru",
    )(xr, xg, xrc, xgc, conv_w.reshape(CONV_W, 1, width), conv_b.reshape(1, 1, width),
      wa.astype(BF16), ba.reshape(2, 1, width), wx.astype(BF16), bx.reshape(2, 1, width),
      lam.reshape(2, 1, width))


def _to_chunked(a, batch):
    t = a.shape[0] // batch
    return a.reshape(batch, LRU_CHUNKS, t // LRU_CHUNKS, a.shape[1]).transpose(0, 2, 1, 3)


def _from_chunked(a):
    b, r, c, w = a.shape
    return a.transpose(0, 2, 1, 3).reshape(b * r * c, w)


def _out_proj_kernel(a1_ref, a2_ref, w1_ref, w2_ref, x_ref, g_ref, o_ref):
    y = _dot(a1_ref[...], w1_ref[...]) + _dot(a2_ref[...], w2_ref[...])
    o_ref[...] = x_ref[...] + g_ref[...] * y


def _out_proj(a1, a1_blk, a2, a2_blk, w, x, mod3, gate_blk, rows_per_mod, mod_base, tm=2048, tn=512):
    n, d = x.shape
    kh = w.shape[0] // 2
    tm = min(tm, n)
    per = rows_per_mod // tm
    gpb = d // tn
    return pl.pallas_call(
        _out_proj_kernel,
        out_shape=jax.ShapeDtypeStruct((n, d), F32),
        grid=(n // tm, d // tn),
        in_specs=[pl.BlockSpec((tm, kh), lambda i, j: (i, a1_blk)),
                  pl.BlockSpec((tm, kh), lambda i, j: (i, a2_blk)),
                  pl.BlockSpec((kh, tn), lambda i, j: (0, j)),
                  pl.BlockSpec((kh, tn), lambda i, j: (1, j)),
                  pl.BlockSpec((tm, tn), lambda i, j: (i, j)),
                  pl.BlockSpec((None, 1, tn), lambda i, j: (mod_base + i // per, 0, gate_blk * gpb + j))],
        out_specs=pl.BlockSpec((tm, tn), lambda i, j: (i, j)),
        compiler_params=_cparams(("parallel", "parallel")),
        name="out_proj",
    )(a1, a2, w, w, x, mod3)


def _lane_block_max(s):
    mm = s[:, 0:LANES]
    for t in range(1, s.shape[1] // LANES):
        mm = jnp.maximum(mm, s[:, t * LANES:(t + 1) * LANES])
    return mm


def _exp_blocks(s, mrep):
    return jnp.concatenate(
        [jnp.exp((s[:, t * LANES:(t + 1) * LANES] - mrep).astype(BF16)) for t in range(s.shape[1] // LANES)],
        axis=1)


def _dense_attn_kernel(q_ref, k_ref, v_ref, kx_ref, vx_ref, o_ref,
                       s_ref, sx_ref, m_ref, acc_ref, va_ref, vax_ref, *, tq, seq, kc):
    n_chunks = seq // kc

    @pl.when(pl.program_id(2) == 0)
    def _():
        for c in range(n_chunks):
            va_ref[c, :, 0:HEAD_DIM] = v_ref[c * kc:(c + 1) * kc, :]
            va_ref[c, :, HEAD_DIM:] = jnp.ones((kc, HEAD_DIM), BF16)
        vax_ref[:, 0:HEAD_DIM] = vx_ref[...]
        vax_ref[:, HEAD_DIM:] = jnp.ones((vx_ref.shape[0], HEAD_DIM), BF16)

    q4 = jnp.concatenate([q_ref[:, _head_cols(g)] for g in range(GQA_GROUP)], axis=0)
    sx = _dot_nt(q4, kx_ref[...])
    sx_ref[...] = sx
    m_ref[...] = _lane_block_max(sx)

    def sweep1(c, carry):
        s = _dot_nt(q4, k_ref[pl.ds(pl.multiple_of(c * kc, kc), kc), :])
        s_ref[c] = s
        m_ref[...] = jnp.maximum(m_ref[...], _lane_block_max(s))
        return carry

    lax.fori_loop(0, n_chunks, sweep1, 0)
    mrep = jnp.broadcast_to(jnp.max(m_ref[...], axis=-1, keepdims=True), m_ref.shape)
    m_ref[...] = mrep
    acc_ref[...] = _dot(_exp_blocks(sx_ref[...], mrep), vax_ref[...])

    def sweep2(c, carry):
        acc_ref[...] += _dot(_exp_blocks(s_ref[c], m_ref[...]), va_ref[c])
        return carry

    lax.fori_loop(0, n_chunks, sweep2, 0)
    o = acc_ref[:, 0:HEAD_DIM] / acc_ref[:, HEAD_DIM:]
    for g in range(GQA_GROUP):
        o_ref[:, _head_cols(g)] = o[g * tq:(g + 1) * tq].astype(o_ref.dtype)


def _dense_attn(q_r, k_r, proj, kx_r, proj_c, batch, seq, ctx_len, tq=256, kc=1024):
    gw = GQA_GROUP * HEAD_DIM
    nq = seq // tq
    rows = GQA_GROUP * tq
    v_blk = C_Q_HEADS + C_KV_HEADS
    vx_blk = C_KV_HEADS
    return pl.pallas_call(
        functools.partial(_dense_attn_kernel, tq=tq, seq=seq, kc=kc),
        out_shape=jax.ShapeDtypeStruct((batch * seq, C_Q_HEADS * HEAD_DIM), BF16),
        grid=(batch, C_KV_HEADS, nq),
        in_specs=[pl.BlockSpec((tq, gw), lambda b, h, i: (b * nq + i, h)),
                  pl.BlockSpec((seq, HEAD_DIM), lambda b, h, i: (b, h)),
                  pl.BlockSpec((seq, HEAD_DIM), lambda b, h, i: (b, v_blk + h)),
                  pl.BlockSpec((ctx_len, HEAD_DIM), lambda b, h, i: (b, h)),
                  pl.BlockSpec((ctx_len, HEAD_DIM), lambda b, h, i: (b, vx_blk + h))],
        out_specs=pl.BlockSpec((tq, gw), lambda b, h, i: (b * nq + i, h)),
        scratch_shapes=[pltpu.VMEM((seq // kc, rows, kc), F32),
                        pltpu.VMEM((rows, ctx_len), F32),
                        pltpu.VMEM((rows, LANES), F32),
                        pltpu.VMEM((rows, 2 * HEAD_DIM), F32),
                        pltpu.VMEM((seq // kc, kc, 2 * HEAD_DIM), BF16),
                        pltpu.VMEM((ctx_len, 2 * HEAD_DIM), BF16)],
        compiler_params=_cparams(("parallel", "parallel", "arbitrary")),
        name="dense_attn",
    )(q_r, k_r, proj, kx_r, proj_c)


def _router_rows(biased, scores):
    v = [biased[e:e + 1, :] for e in range(N_EXPERTS)]
    s = [scores[e:e + 1, :] for e in range(N_EXPERTS)]

    def top2_sum(vals):
        best = vals[0] + vals[1]
        for i in range(len(vals)):
            for j in range(i + 1, len(vals)):
                if (i, j) != (0, 1):
                    best = jnp.maximum(best, vals[i] + vals[j])
        return best

    gsum = [top2_sum(v[g * EXPERTS_PER_GROUP:(g + 1) * EXPERTS_PER_GROUP]) for g in range(N_GROUPS)]
    sel = jnp.zeros_like(gsum[0], dtype=jnp.int32)
    best = gsum[0]
    for g in range(1, N_GROUPS):
        take = gsum[g] > best
        sel = jnp.where(take, g, sel)
        best = jnp.where(take, gsum[g], best)

    def pick_group(rows, i):
        out = rows[i]
        for g in range(1, N_GROUPS):
            out = jnp.where(sel == g, rows[g * EXPERTS_PER_GROUP + i], out)
        return out

    cand = [pick_group(v, i) for i in range(EXPERTS_PER_GROUP)]
    cand_s = [pick_group(s, i) for i in range(EXPERTS_PER_GROUP)]
    i1 = jnp.zeros_like(sel)
    b1 = cand[0]
    for i in range(1, EXPERTS_PER_GROUP):
        take = cand[i] > b1
        i1 = jnp.where(take, i, i1)
        b1 = jnp.where(take, cand[i], b1)
    i2 = jnp.full_like(sel, -1)
    b2 = jnp.full_like(b1, -jnp.inf)
    for i in range(EXPERTS_PER_GROUP):
        take = (i1 != i) & ((cand[i] > b2) | (i2 < 0))
        i2 = jnp.where(take, i, i2)
        b2 = jnp.where(take, cand[i], b2)

    def pick_idx(rows, idx):
        out = rows[0]
        for i in range(1, EXPERTS_PER_GROUP):
            out = jnp.where(idx == i, rows[i], out)
        return out

    s0 = pick_idx(cand_s, i1)
    s1 = pick_idx(cand_s, i2)
    tot = s0 + s1
    e0 = (sel * EXPERTS_PER_GROUP + i1).astype(F32)
    e1 = (sel * EXPERTS_PER_GROUP + i2).astype(F32)
    return e0, e1, s0 / tot, s1 / tot


def _norm_router_kernel(*refs, tm, n_lat_tiles, has_ctx):
    if has_ctx:
        x_ref, xc_ref, g_ref, sh_ref, sc_ref, rw_ref, rb_ref, h_ref, r_ref = refs
        i = pl.program_id(0)

        @pl.when(i < n_lat_tiles)
        def _():
            _norm_mod_rows(x_ref, g_ref, sh_ref, sc_ref, h_ref, tm)

        @pl.when(i >= n_lat_tiles)
        def _():
            _norm_mod_rows(xc_ref, g_ref, sh_ref, sc_ref, h_ref, tm)
    else:
        x_ref, g_ref, sh_ref, sc_ref, rw_ref, rb_ref, h_ref, r_ref = refs
        _norm_mod_rows(x_ref, g_ref, sh_ref, sc_ref, h_ref, tm)

    logits = _dot_nt(rw_ref[...], h_ref[...])
    scores = jax.nn.sigmoid(logits)
    e0, e1, w0, w1 = _router_rows(scores + rb_ref[...], scores)
    zero = jnp.zeros_like(w0)
    r_ref[...] = jnp.concatenate([e0, e1, w0, w1, zero, zero, zero, zero], axis=0)


def _lat_ctx_maps(n_lat, per, ctx_mod_row):
    def mod_row(i):
        return jnp.where(i < n_lat, i // per, ctx_mod_row)

    def lat(i):
        return (jnp.minimum(i, n_lat - 1), 0)

    def ctx(i):
        return (jnp.maximum(i - n_lat, 0), 0)

    return mod_row, lat, ctx


def _norm_router(x, cx, g, mod3, shift_blk, scale_blk, rw_t, rb, rows_per_mod, ctx_mod_row, tm=512):
    n, d = x.shape
    n_lat = n // tm
    has_ctx = cx is not None
    ntot = n_lat + (cx.shape[0] // tm if has_ctx else 0)
    mod_row, lat, ctx = _lat_ctx_maps(n_lat, rows_per_mod // tm, ctx_mod_row)
    row_specs = [pl.BlockSpec((tm, d), lat)] + ([pl.BlockSpec((tm, d), ctx)] if has_ctx else [])
    row_args = [x] + ([cx] if has_ctx else [])
    return pl.pallas_call(
        functools.partial(_norm_router_kernel, tm=tm, n_lat_tiles=n_lat, has_ctx=has_ctx),
        out_shape=(jax.ShapeDtypeStruct((ntot * tm, d), BF16),
                   jax.ShapeDtypeStruct((SUBLANES, ntot * tm), F32)),
        grid=(ntot,),
        in_specs=row_specs + [
            pl.BlockSpec((1, d), lambda i: (0, 0)),
            pl.BlockSpec((None, 1, d), lambda i: (mod_row(i), 0, shift_blk)),
            pl.BlockSpec((None, 1, d), lambda i: (mod_row(i), 0, scale_blk)),
            pl.BlockSpec((N_EXPERTS, d), lambda i: (0, 0)),
            pl.BlockSpec((N_EXPERTS, 1), lambda i: (0, 0))],
        out_specs=(pl.BlockSpec((tm, d), lambda i: (i, 0)),
                   pl.BlockSpec((SUBLANES, tm), lambda i: (0, i))),
        compiler_params=_cparams(("arbitrary",)),
        name="norm_router",
    )(*row_args, g.reshape(1, d), mod3, mod3, rw_t, rb)


def _gmm_kernel(te_ref, tv_ref, xs_ref, wg_ref, wu_ref, wd_ref, o_ref):
    i = pl.program_id(0)

    @pl.when(tv_ref[i] > 0)
    def _():
        x = xs_ref[...]
        gate = _dot(x, wg_ref[0])
        up = _dot(x, wu_ref[0])
        h1 = ((gate * jax.nn.sigmoid(gate)) * up).astype(BF16)
        o_ref[...] = _dot(h1, wd_ref[0]).astype(o_ref.dtype)

    @pl.when(tv_ref[i] == 0)
    def _():
        o_ref[...] = jnp.zeros(o_ref.shape, o_ref.dtype)


def _grouped_mlp(tile_expert, tile_valid, xs, wg, wu, wd, layer, tm=MOE_TM):
    p, d = xs.shape
    dff = wg.shape[3]
    grid_spec = pltpu.PrefetchScalarGridSpec(
        num_scalar_prefetch=2,
        grid=(p // tm,),
        in_specs=[pl.BlockSpec((tm, d), lambda i, te, tv: (i, 0)),
                  pl.BlockSpec((None, 1, d, dff), lambda i, te, tv: (layer, te[i], 0, 0)),
                  pl.BlockSpec((None, 1, d, dff), lambda i, te, tv: (layer, te[i], 0, 0)),
                  pl.BlockSpec((None, 1, dff, d), lambda i, te, tv: (layer, te[i], 0, 0))],
        out_specs=pl.BlockSpec((tm, d), lambda i, te, tv: (i, 0)),
    )
    return pl.pallas_call(
        _gmm_kernel,
        out_shape=jax.ShapeDtypeStruct((p, d), BF16),
        grid_spec=grid_spec,
        compiler_params=_cparams(("arbitrary",)),
        name="grouped_mlp",
    )(tile_expert, tile_valid, xs, wg, wu, wd)


def _combine_kernel(*refs, n_lat_tiles, has_ctx):
    if has_ctx:
        x_ref, xc_ref, y0_ref, y1_ref, r_ref, g_ref, o_ref, oc_ref = refs
    else:
        x_ref, y0_ref, y1_ref, r_ref, g_ref, o_ref = refs
    w0 = r_ref[:, 2:3]
    w1 = r_ref[:, 3:4]
    f = g_ref[...] * (w0 * y0_ref[...].astype(F32) + w1 * y1_ref[...].astype(F32))
    if not has_ctx:
        o_ref[...] = x_ref[...] + f
        return
    i = pl.program_id(0)

    @pl.when(i < n_lat_tiles)
    def _():
        o_ref[...] = x_ref[...] + f

    @pl.when(i >= n_lat_tiles)
    def _():
        oc_ref[...] = xc_ref[...] + f


def _combine(x, cx, yg, route_cols, mod3, gate_blk, rows_per_mod, ctx_mod_row, tm=512):
    n, d = x.shape
    n_lat = n // tm
    has_ctx = cx is not None
    ntot = n_lat + (cx.shape[0] // tm if has_ctx else 0)
    mod_row, lat, ctx = _lat_ctx_maps(n_lat, rows_per_mod // tm, ctx_mod_row)
    row_specs = [pl.BlockSpec((tm, d), lat)] + ([pl.BlockSpec((tm, d), ctx)] if has_ctx else [])
    row_args = [x] + ([cx] if has_ctx else [])
    out_shape = [jax.ShapeDtypeStruct(x.shape, F32)] + ([jax.ShapeDtypeStruct(cx.shape, F32)] if has_ctx else [])
    out = pl.pallas_call(
        functools.partial(_combine_kernel, n_lat_tiles=n_lat, has_ctx=has_ctx),
        out_shape=tuple(out_shape),
        grid=(ntot,),
        in_specs=row_specs + [
            pl.BlockSpec((tm, d), lambda i: (i, 0)),
            pl.BlockSpec((tm, d), lambda i: (ntot + i, 0)),
            pl.BlockSpec((tm, SUBLANES), lambda i: (i, 0)),
            pl.BlockSpec((None, 1, d), lambda i: (mod_row(i), 0, gate_blk))],
        out_specs=tuple(row_specs),
        compiler_params=_cparams(("arbitrary",)),
        name="moe_combine",
    )(*row_args, yg, yg, route_cols, mod3)
    return out if has_ctx else (out[0], None)


def _dispatch_plan(route, tm):
    n = route.shape[1]
    e_flat = jnp.concatenate([route[0], route[1]]).astype(jnp.int32)
    n_assign = 2 * n
    n_tiles = n_assign // tm + N_EXPERTS
    experts = jnp.arange(N_EXPERTS, dtype=jnp.int32)[:, None]
    onehot = (experts == e_flat[None, :]).astype(jnp.int32)
    csum = jnp.cumsum(onehot, axis=1)
    counts = csum[:, -1]
    padded = ((counts + tm - 1) // tm) * tm
    ends_p = jnp.cumsum(padded)
    starts_p = ends_p - padded
    starts_c = jnp.cumsum(counts) - counts
    dest = jnp.sum(onehot * (csum - 1 + starts_p[:, None]), axis=0)
    order = jnp.argsort(e_flat, stable=True).astype(jnp.int32)
    p = jnp.arange(n_tiles * tm, dtype=jnp.int32)[None, :]
    owner = ((p >= starts_p[:, None]) & (p < ends_p[:, None])).astype(jnp.int32)
    within = jnp.sum(owner * (p - starts_p[:, None]), axis=0)
    live = jnp.sum(owner * (p - starts_p[:, None] < counts[:, None]), axis=0) > 0
    compact = jnp.sum(owner * starts_c[:, None], axis=0) + within
    src_tok = jnp.where(live, order[jnp.clip(compact, 0, n_assign - 1)] % n, 0)
    tile_start = jnp.arange(n_tiles, dtype=jnp.int32) * tm
    tile_valid = (tile_start < ends_p[-1]).astype(jnp.int32)
    last_tile = jnp.maximum(ends_p[-1] // tm - 1, 0) * tm
    tile_expert = jnp.searchsorted(ends_p, jnp.minimum(tile_start, last_tile), side="right").astype(jnp.int32)
    tile_expert = jnp.minimum(tile_expert, N_EXPERTS - 1)
    return src_tok, dest, tile_expert, tile_valid


def _moe(x, cx, g, mod3, rw_t, rb, wg, wu, wd, layer, rows_per_mod, ctx_mod_row):
    h, route = _norm_router(x, cx, g, mod3, 3, 4, rw_t, rb, rows_per_mod, ctx_mod_row)
    src_tok, dest, tile_expert, tile_valid = _dispatch_plan(route, MOE_TM)
    xs = h.at[src_tok].get(mode="promise_in_bounds")
    ys = _grouped_mlp(tile_expert, tile_valid, xs, wg, wu, wd, layer)
    yg = ys.at[dest].get(mode="promise_in_bounds")
    route_cols = route.T
    return _combine(x, cx, yg, route_cols, mod3, 5, rows_per_mod, ctx_mod_row)


def _rope_tables(seq):
    rows = seq // GRID_W
    row = jnp.repeat(jnp.arange(rows, dtype=F32), GRID_W)
    col = jnp.tile(jnp.arange(GRID_W, dtype=F32), rows)
    n_freq = HEAD_DIM // 4
    inv_freq = ROPE_BASE ** (-jnp.arange(n_freq, dtype=F32) / n_freq)
    ang = jnp.concatenate([row[:, None] * inv_freq, col[:, None] * inv_freq], axis=-1)
    cos, sin = jnp.cos(ang), jnp.sin(ang)
    return jnp.concatenate([cos, cos], axis=-1), jnp.concatenate([-sin, sin], axis=-1)


def kernel(x, c, ctx, c_ctx, ada_w, ada_b, norm_mix, norm_ffn, ab_w_in, ab_q_gain, ab_k_gain, ab_sink, ab_conv_w, ab_conv_b, ab_gate_a_w, ab_gate_a_b, ab_gate_x_w, ab_gate_x_b, ab_lru_lambda, ab_w_out, gqa_w_in, gqa_q_gain, gqa_k_gain, gqa_w_out, router_w, router_bias, moe_w_gate, moe_w_up, moe_w_down):
    batch, seq, d = x.shape
    ctx_len = ctx.shape[1]
    depth = ada_w.shape[0]
    assert depth == 2 and batch < SUBLANES
    n_lat = batch * seq
    n_ctx = batch * ctx_len
    ctx_row = batch

    xl = x.reshape(n_lat, d)
    xc = ctx.reshape(n_ctx, d)
    cc = jnp.zeros((SUBLANES, d), F32).at[:batch].set(c).at[ctx_row].set(c_ctx)
    mod = _ada(cc, ada_w, ada_b)
    cos2, sin2 = _rope_tables(seq)
    rw_t = router_w.T.astype(BF16)
    rb = router_bias.reshape(N_EXPERTS, 1).astype(F32)
    w_gate, w_up, w_down = _cast_bf16(moe_w_gate), _cast_bf16(moe_w_up), _cast_bf16(moe_w_down)

    mod3 = mod[0].reshape(SUBLANES, 1, 6 * d)
    w_in = ab_w_in[0].astype(BF16)
    proj = _norm_mod_matmul(xl, norm_mix[0], mod3, 0, 1, w_in, seq, 0)
    proj_c = _norm_mod_matmul(xc, norm_mix[0], mod3, 0, 1, w_in, n_ctx, ctx_row)
    q_r = _qknorm(proj, 0, A_Q_HEADS, ab_q_gain[0], cos2, sin2, out_scale=ATTN_SCALE)
    k_r = _qknorm(proj, A_Q_HEADS // A_KV_HEADS, A_KV_HEADS, ab_k_gain[0], cos2, sin2)
    qx_r = _qknorm(proj_c, 0, A_Q_HEADS, ab_q_gain[0], out_scale=ATTN_SCALE)
    kx_r = _qknorm(proj_c, A_Q_HEADS // A_KV_HEADS, A_KV_HEADS, ab_k_gain[0])
    att = _win_attn(ab_sink[0], q_r, k_r, proj, kx_r, proj_c, batch, seq, ctx_len)
    att_c = _ctx_attn(ab_sink[0], qx_r, kx_r, proj_c, batch, ctx_len)

    lru_w = ab_conv_w.shape[2]
    c0 = (A_Q_HEADS + 2 * A_KV_HEADS) * HEAD_DIM
    y_p, yc_p = _lru(_to_chunked(proj[:, c0:c0 + lru_w], batch),
                     _to_chunked(proj[:, c0 + lru_w:c0 + 2 * lru_w], batch),
                     _to_chunked(proj_c[:, c0:c0 + lru_w], batch),
                     _to_chunked(proj_c[:, c0 + lru_w:c0 + 2 * lru_w], batch),
                     ab_conv_w[0], ab_conv_b[0], ab_gate_a_w[0], ab_gate_a_b[0],
                     ab_gate_x_w[0], ab_gate_x_b[0], ab_lru_lambda[0])
    w_out = ab_w_out[0].astype(BF16)
    xl = _out_proj(att, 0, _from_chunked(y_p), 0, w_out, xl, mod3, 2, seq, 0)
    xc = _out_proj(att_c, 0, _from_chunked(yc_p), 0, w_out, xc, mod3, 2, n_ctx, ctx_row)
    xl, xc = _moe(xl, xc, norm_ffn[0], mod3, rw_t, rb, w_gate, w_up, w_down, 0, seq, ctx_row)

    mod3 = mod[1].reshape(SUBLANES, 1, 6 * d)
    w_in = gqa_w_in[0].astype(BF16)
    cw = C_Q_HEADS * HEAD_DIM
    proj = _norm_mod_matmul(xl, norm_mix[1], mod3, 0, 1, w_in, seq, 0)
    proj_c = _norm_mod_matmul(xc, norm_mix[1], mod3, 0, 1, w_in[:, cw:], n_ctx, ctx_row)
    q_r = _qknorm(proj, 0, C_Q_HEADS, gqa_q_gain[0], cos2, sin2, out_scale=ATTN_SCALE)
    k_r = _qknorm(proj, C_Q_HEADS // C_KV_HEADS, C_KV_HEADS, gqa_k_gain[0], cos2, sin2)
    kx_r = _qknorm(proj_c, 0, C_KV_HEADS, gqa_k_gain[0])
    att = _dense_attn(q_r, k_r, proj, kx_r, proj_c, batch, seq, ctx_len)
    xl = _out_proj(att, 0, att, 1, gqa_w_out[0].astype(BF16), xl, mod3, 2, seq, 0)
    xl, _ = _moe(xl, None, norm_ffn[1], mod3, rw_t, rb, w_gate, w_up, w_down, 1, seq, ctx_row)
    return xl.reshape(batch, seq, d)
```

```python
import functools

import jax
import jax.numpy as jnp
from jax import lax
from jax.experimental import pallas as pl
from jax.experimental.pallas import tpu as pltpu

F32 = jnp.float32
BF16 = jnp.bfloat16

LANES = 128
SUBLANES = 8
VMEM_LIMIT = 56 * 1024 * 1024

HEAD_DIM = 128
GRID_W = 64
WINDOW = 128
BLOCK = 128
ROPE_BASE = 10000.0
EPS = 1e-6
ATTN_SCALE = HEAD_DIM ** -0.5
A_Q_HEADS, A_KV_HEADS = 8, 2
C_Q_HEADS, C_KV_HEADS = 16, 4
GQA_GROUP = 4
LRU_C = 8.0
CONV_W = 4
CONV_LEFT = 2
N_EXPERTS = 16
N_GROUPS = 4
EXPERTS_PER_GROUP = 4
NEG_BIG = -1e30

LRU_CHUNKS = SUBLANES
LRU_JB = 8
LRU_LANE_BLOCKS = 2
MOE_TM = 256


def _cparams(sem, vmem=VMEM_LIMIT):
    return pltpu.CompilerParams(dimension_semantics=sem, vmem_limit_bytes=vmem)


def _dot(a, b):
    return jnp.dot(a, b, preferred_element_type=F32)


def _dot_nt(a, b):
    return lax.dot_general(a, b, (((1,), (1,)), ((), ())), preferred_element_type=F32)


def _cast_kernel(x_ref, o_ref):
    o_ref[...] = x_ref[...].astype(o_ref.dtype)


def _cast_bf16(w, layer, block_elems=2 * 1024 * 1024):
    shape = w.shape[1:]
    cols = shape[-1]
    w2 = w.reshape(-1, cols)
    rows = w2.shape[0] // w.shape[0]
    tr = min(block_elems // cols, rows)
    nblk = rows // tr
    out = pl.pallas_call(
        _cast_kernel,
        out_shape=jax.ShapeDtypeStruct((rows, cols), BF16),
        grid=(nblk,),
        in_specs=[pl.BlockSpec((tr, cols), lambda i: (layer * nblk + i, 0))],
        out_specs=pl.BlockSpec((tr, cols), lambda i: (i, 0)),
        compiler_params=_cparams(("parallel",)),
        name="cast_bf16",
    )(w2)
    return out.reshape(shape)


def _ada_kernel(c_ref, w_ref, b_ref, o_ref):
    c = c_ref[...]
    s = (c * jax.nn.sigmoid(c)).astype(BF16)
    o_ref[0] = _dot(s, w_ref[0].astype(BF16)) + b_ref[0]


def _ada(cc, ada_w, ada_b):
    depth, d, n = ada_w.shape
    tn = 1024
    return pl.pallas_call(
        _ada_kernel,
        out_shape=jax.ShapeDtypeStruct((depth, SUBLANES, n), F32),
        grid=(depth, n // tn),
        in_specs=[pl.BlockSpec((SUBLANES, d), lambda l, j: (0, 0)),
                  pl.BlockSpec((1, d, tn), lambda l, j: (l, 0, j)),
                  pl.BlockSpec((1, 1, tn), lambda l, j: (l, 0, j))],
        out_specs=pl.BlockSpec((1, SUBLANES, tn), lambda l, j: (l, 0, j)),
        compiler_params=_cparams(("arbitrary", "arbitrary")),
        name="ada",
    )(cc, ada_w, ada_b.reshape(depth, 1, n))


def _norm_mod_rows(x_ref, g_ref, sh_ref, sc_ref, dst_ref, tm, rc=128):
    g = g_ref[...]
    sc1 = 1.0 + sc_ref[...]
    sh = sh_ref[...]

    def body(r, carry):
        rows = pl.ds(pl.multiple_of(r * rc, rc), rc)
        xf = x_ref[rows, :]
        ms = jnp.mean(xf * xf, axis=-1, keepdims=True)
        xn = (xf * lax.rsqrt(ms + EPS)) * g
        dst_ref[rows, :] = (xn * sc1 + sh).astype(dst_ref.dtype)
        return carry

    lax.fori_loop(0, tm // rc, body, 0)


def _nm_mm_kernel(x_ref, g_ref, sh_ref, sc_ref, w_ref, o_ref, hn_ref, *, tm, tn):
    _norm_mod_rows(x_ref, g_ref, sh_ref, sc_ref, hn_ref, tm)
    h = hn_ref[...]
    for j in range(w_ref.shape[1] // tn):
        cols = slice(j * tn, (j + 1) * tn)
        o_ref[:, cols] = _dot(h, w_ref[:, cols]).astype(o_ref.dtype)


def _norm_mod_matmul(x, g, mod3, shift_blk, scale_blk, w, rows_per_mod, mod_base, tm=512, tn=512):
    n, d = x.shape
    nout = w.shape[1]
    tm = min(tm, n)
    tn = min(tn, nout)
    per = rows_per_mod // tm

    def mod_row(i):
        return mod_base + i // per

    return pl.pallas_call(
        functools.partial(_nm_mm_kernel, tm=tm, tn=tn),
        out_shape=jax.ShapeDtypeStruct((n, nout), BF16),
        grid=(n // tm,),
        in_specs=[pl.BlockSpec((tm, d), lambda i: (i, 0)),
                  pl.BlockSpec((1, d), lambda i: (0, 0)),
                  pl.BlockSpec((None, 1, d), lambda i: (mod_row(i), 0, shift_blk)),
                  pl.BlockSpec((None, 1, d), lambda i: (mod_row(i), 0, scale_blk)),
                  pl.BlockSpec((d, nout), lambda i: (0, 0), pipeline_mode=pl.Buffered(1))],
        out_specs=pl.BlockSpec((tm, nout), lambda i: (i, 0)),
        scratch_shapes=[pltpu.VMEM((tm, d), BF16)],
        compiler_params=_cparams(("parallel",)),
        name="norm_mod_matmul",
    )(x, g.reshape(1, d), mod3, mod3, w)


def _qknorm_kernel(*refs, n_heads, rope, out_scale):
    if rope:
        x_ref, g_ref, cos_ref, sin_ref, o_ref = refs
        cos2 = cos_ref[...]
        sin2 = sin_ref[...]
    else:
        x_ref, g_ref, o_ref = refs
    g = g_ref[...]
    for h in range(n_heads):
        cols = slice(h * HEAD_DIM, (h + 1) * HEAD_DIM)
        xf = x_ref[:, cols].astype(F32)
        ms = jnp.mean(xf * xf, axis=-1, keepdims=True)
        xn = (xf * lax.rsqrt(ms + EPS)) * g
        if rope:
            xn = xn * cos2 + pltpu.roll(xn, HEAD_DIM // 2, 1) * sin2
        if out_scale != 1.0:
            xn = xn * out_scale
        o_ref[:, cols] = xn.astype(o_ref.dtype)


def _qknorm(proj, col_blk, n_heads, gain, cos2=None, sin2=None, out_scale=1.0, tr=512):
    n = proj.shape[0]
    tr = min(tr, n)
    w = n_heads * HEAD_DIM
    rope = cos2 is not None
    in_specs = [pl.BlockSpec((tr, w), lambda i: (i, col_blk)),
                pl.BlockSpec((1, HEAD_DIM), lambda i: (0, 0))]
    args = [proj, gain.reshape(1, HEAD_DIM)]
    if rope:
        per = cos2.shape[0] // tr
        in_specs += [pl.BlockSpec((tr, HEAD_DIM), lambda i: (i % per, 0)),
                     pl.BlockSpec((tr, HEAD_DIM), lambda i: (i % per, 0))]
        args += [cos2, sin2]
    return pl.pallas_call(
        functools.partial(_qknorm_kernel, n_heads=n_heads, rope=rope, out_scale=out_scale),
        out_shape=jax.ShapeDtypeStruct((n, w), BF16),
        grid=(n // tr,),
        in_specs=in_specs,
        out_specs=pl.BlockSpec((tr, w), lambda i: (i, 0)),
        compiler_params=_cparams(("parallel",)),
        name="qknorm",
    )(*args)


def _head_cols(h):
    return slice(h * HEAD_DIM, (h + 1) * HEAD_DIM)


def _stack_group(q_ref, kvh):
    return jnp.concatenate([q_ref[:, _head_cols(kvh * GQA_GROUP + g)] for g in range(GQA_GROUP)], axis=0)


def _sink_col(sink_ref, kvh, rows):
    return jnp.concatenate([jnp.full((rows, 1), sink_ref[kvh * GQA_GROUP + g], F32)
                            for g in range(GQA_GROUP)], axis=0)


def _win_attn_kernel(sink_ref, q_ref, kp_ref, kc_ref, kn_ref, vp_ref, vc_ref, vn_ref,
                     kx_ref, vx_ref, o_ref, *, seq, ctx_len):
    n = pl.program_id(1)
    rows = GQA_GROUP * BLOCK
    nk = 3 * BLOCK + ctx_len
    qi = lax.broadcasted_iota(jnp.int32, (rows, nk), 0) % BLOCK
    kj = lax.broadcasted_iota(jnp.int32, (rows, nk), 1)
    kpos = n * BLOCK - BLOCK + kj
    in_band = (jnp.abs(kj - BLOCK - qi) <= WINDOW) & (kpos >= 0) & (kpos < seq)
    valid = (kj >= 3 * BLOCK) | in_band
    ones = jnp.ones((nk, HEAD_DIM), BF16)
    for kvh in range(A_KV_HEADS):
        cols = _head_cols(kvh)
        q4 = _stack_group(q_ref, kvh)
        ka = jnp.concatenate([kp_ref[:, cols], kc_ref[:, cols], kn_ref[:, cols], kx_ref[:, cols]], axis=0)
        va = jnp.concatenate([vp_ref[:, cols], vc_ref[:, cols], vn_ref[:, cols], vx_ref[:, cols]], axis=0)
        s = jnp.where(valid, _dot_nt(q4, ka), NEG_BIG)
        sk = _sink_col(sink_ref, kvh, BLOCK)
        m = jnp.maximum(jnp.max(s, axis=-1, keepdims=True), sk)
        p = jnp.exp((s - m).astype(BF16))
        acc = _dot(p, jnp.concatenate([va, ones], axis=1))
        o = acc[:, 0:HEAD_DIM] / (acc[:, HEAD_DIM:] + jnp.exp(sk - m))
        for g in range(GQA_GROUP):
            o_ref[:, _head_cols(kvh * GQA_GROUP + g)] = o[g * BLOCK:(g + 1) * BLOCK].astype(o_ref.dtype)


def _win_attn(sink, q_r, k_r, proj, kx_r, proj_c, batch, seq, ctx_len):
    nb = seq // BLOCK
    kvw = A_KV_HEADS * HEAD_DIM
    v_blk = (A_Q_HEADS * HEAD_DIM + kvw) // kvw

    def prev(b, n):
        return b * nb + jnp.maximum(n - 1, 0)

    def cur(b, n):
        return b * nb + n

    def nxt(b, n):
        return b * nb + jnp.minimum(n + 1, nb - 1)

    return pl.pallas_call(
        functools.partial(_win_attn_kernel, seq=seq, ctx_len=ctx_len),
        out_shape=jax.ShapeDtypeStruct((batch * seq, A_Q_HEADS * HEAD_DIM), BF16),
        grid=(batch, nb),
        in_specs=[pl.BlockSpec(memory_space=pltpu.SMEM),
                  pl.BlockSpec((BLOCK, A_Q_HEADS * HEAD_DIM), lambda b, n: (cur(b, n), 0)),
                  pl.BlockSpec((BLOCK, kvw), lambda b, n: (prev(b, n), 0)),
                  pl.BlockSpec((BLOCK, kvw), lambda b, n: (cur(b, n), 0)),
                  pl.BlockSpec((BLOCK, kvw), lambda b, n: (nxt(b, n), 0)),
                  pl.BlockSpec((BLOCK, kvw), lambda b, n: (prev(b, n), v_blk)),
                  pl.BlockSpec((BLOCK, kvw), lambda b, n: (cur(b, n), v_blk)),
                  pl.BlockSpec((BLOCK, kvw), lambda b, n: (nxt(b, n), v_blk)),
                  pl.BlockSpec((ctx_len, kvw), lambda b, n: (b, 0)),
                  pl.BlockSpec((ctx_len, kvw), lambda b, n: (b, v_blk))],
        out_specs=pl.BlockSpec((BLOCK, A_Q_HEADS * HEAD_DIM), lambda b, n: (cur(b, n), 0)),
        compiler_params=_cparams(("parallel", "parallel")),
        name="win_attn",
    )(sink, q_r, k_r, k_r, k_r, proj, proj, proj, kx_r, proj_c)


def _ctx_attn_kernel(sink_ref, q_ref, k_ref, v_ref, o_ref, *, ctx_len):
    kvh = pl.program_id(1)
    q4 = jnp.concatenate([q_ref[:, _head_cols(g)] for g in range(GQA_GROUP)], axis=0)
    s = _dot_nt(q4, k_ref[...])
    sk = jnp.concatenate([jnp.full((ctx_len, 1), sink_ref[kvh * GQA_GROUP + g], F32)
                          for g in range(GQA_GROUP)], axis=0)
    m = jnp.maximum(jnp.max(s, axis=-1, keepdims=True), sk)
    p = jnp.exp(s - m)
    den = jnp.sum(p, axis=-1, keepdims=True) + jnp.exp(sk - m)
    o = _dot(p.astype(BF16), v_ref[...]) / den
    for g in range(GQA_GROUP):
        o_ref[:, _head_cols(g)] = o[g * ctx_len:(g + 1) * ctx_len].astype(o_ref.dtype)


def _ctx_attn(sink, qx_r, kx_r, proj_c, batch, ctx_len):
    v_blk = A_Q_HEADS + A_KV_HEADS
    gw = GQA_GROUP * HEAD_DIM
    return pl.pallas_call(
        functools.partial(_ctx_attn_kernel, ctx_len=ctx_len),
        out_shape=jax.ShapeDtypeStruct((batch * ctx_len, A_Q_HEADS * HEAD_DIM), BF16),
        grid=(batch, A_KV_HEADS),
        in_specs=[pl.BlockSpec(memory_space=pltpu.SMEM),
                  pl.BlockSpec((ctx_len, gw), lambda b, h: (b, h)),
                  pl.BlockSpec((ctx_len, HEAD_DIM), lambda b, h: (b, h)),
                  pl.BlockSpec((ctx_len, HEAD_DIM), lambda b, h: (b, v_blk + h))],
        out_specs=pl.BlockSpec((ctx_len, gw), lambda b, h: (b, h)),
        compiler_params=_cparams(("parallel", "parallel")),
        name="ctx_attn",
    )(sink, qx_r, kx_r, proj_c)


def _sigmoid(x):
    return 0.5 * jnp.tanh(0.5 * x) + 0.5


def _gelu_tanh(x):
    return 0.5 * x * (1.0 + jnp.tanh(0.7978845608028654 * (x + 0.044715 * (x * x * x))))


def _lru_sequence(x_ref, xg_ref, y_ref, xp_ref, a_ref, b_ref, w, init, rows, jb):
    conv_w, conv_b, wa, ba, wx, bx, c_logsig = w
    width = x_ref.shape[-1]
    sub = lax.broadcasted_iota(jnp.int32, (1, SUBLANES, width), 1)

    def block_diag(ub, wd):
        return jnp.concatenate([_dot(ub[:, _head_cols(n)], wd[n]) for n in range(width // LANES)], axis=1)

    def fill(r, carry):
        rr = pl.ds(pl.multiple_of(r * jb, jb), jb)
        xp_ref[pl.ds(pl.multiple_of(r * jb, jb) + CONV_LEFT, jb)] = x_ref[rr].astype(F32)
        return carry

    lax.fori_loop(0, rows // jb, fill, 0)
    tail = x_ref[rows - CONV_LEFT:rows].astype(F32)
    xp_ref[0:CONV_LEFT] = jnp.where(sub == 0, 0.0, pltpu.roll(tail, 1, 1))
    head = x_ref[0:1].astype(F32)
    xp_ref[rows + CONV_LEFT:rows + CONV_LEFT + 1] = jnp.where(
        sub == SUBLANES - 1, 0.0, pltpu.roll(head, SUBLANES - 1, 1))

    def gates(r, carry):
        j0 = pl.multiple_of(r * jb, jb)
        u = conv_b
        for k in range(CONV_W):
            u = u + conv_w[k] * xp_ref[pl.ds(j0 + k, jb)]
        u2 = u.reshape(jb * SUBLANES, width)
        ub = u2.astype(BF16)
        for d in range(2):
            r_gate = _sigmoid(block_diag(ub, wa[d]) + ba[d])
            i_gate = _sigmoid(block_diag(ub, wx[d]) + bx[d])
            log_a = c_logsig[d] * r_gate
            a = jnp.exp(log_a)
            b = jnp.sqrt(1.0 - a * a) * (i_gate * u2)
            a_ref[d, pl.ds(j0, jb)] = a.reshape(jb, SUBLANES, width)
            b_ref[d, pl.ds(j0, jb)] = b.reshape(jb, SUBLANES, width)
        return carry

    lax.fori_loop(0, rows // jb, gates, 0)

    def scan(j, carry):
        hf, pf, hb, pb = carry
        jr = rows - 1 - j
        af = a_ref[0, j]
        hf = af * hf + b_ref[0, j]
        pf = pf * af
        b_ref[0, j] = hf
        a_ref[0, j] = pf
        ab = a_ref[1, jr]
        hb = ab * hb + b_ref[1, jr]
        pb = pb * ab
        b_ref[1, jr] = hb
        a_ref[1, jr] = pb
        return hf, pf, hb, pb

    z = jnp.zeros((SUBLANES, width), F32)
    o = jnp.ones((SUBLANES, width), F32)
    lax.fori_loop(0, rows, scan, (z, o, z, o), unroll=8)

    hf_last, pf_last = b_ref[0, rows - 1], a_ref[0, rows - 1]
    hb_last, pb_last = b_ref[1, 0], a_ref[1, 0]
    s = init[0]
    carry_f = []
    for c in range(SUBLANES):
        carry_f.append(s)
        s = hf_last[c:c + 1] + pf_last[c:c + 1] * s
    out_f = s
    s = init[1]
    carry_b = [None] * SUBLANES
    for c in reversed(range(SUBLANES)):
        carry_b[c] = s
        s = hb_last[c:c + 1] + pb_last[c:c + 1] * s
    out_b = s
    cf = jnp.concatenate(carry_f, axis=0)
    cb = jnp.concatenate(carry_b, axis=0)

    def emit(r, carry):
        rr = pl.ds(pl.multiple_of(r * jb, jb), jb)
        h = (b_ref[0, rr] + a_ref[0, rr] * cf) + (b_ref[1, rr] + a_ref[1, rr] * cb)
        y_ref[rr] = (h * _gelu_tanh(xg_ref[rr].astype(F32))).astype(y_ref.dtype)
        return carry

    lax.fori_loop(0, rows // jb, emit, 0)
    return out_f, out_b


def _lru_kernel(xr_ref, xg_ref, xrc_ref, xgc_ref, cw_ref, cb_ref, wa_ref, ba_ref, wx_ref, bx_ref,
                lam_ref, y_ref, yc_ref, xp_ref, a_ref, b_ref, *, rows, rows_c):
    c_logsig = [LRU_C * jax.nn.log_sigmoid(lam_ref[d]) for d in range(2)]
    w = ([cw_ref[k] for k in range(CONV_W)], cb_ref[0],
         [wa_ref[d] for d in range(2)], [ba_ref[d] for d in range(2)],
         [wx_ref[d] for d in range(2)], [bx_ref[d] for d in range(2)], c_logsig)
    zero = jnp.zeros((1, xr_ref.shape[-1]), F32)
    sf, sb = _lru_sequence(xrc_ref, xgc_ref, yc_ref, xp_ref, a_ref, b_ref, w, (zero, zero),
                           rows_c, min(LRU_JB, rows_c))
    _lru_sequence(xr_ref, xg_ref, y_ref, xp_ref, a_ref, b_ref, w, (sf, sb), rows, LRU_JB)


def _lru(xr, xg, xrc, xgc, conv_w, conv_b, wa, ba, wx, bx, lam):
    batch, rows, _, width = xr.shape
    rows_c = xrc.shape[1]
    cw = LRU_LANE_BLOCKS * LANES
    seq_spec = pl.BlockSpec((None, rows, SUBLANES, cw), lambda b, n: (b, 0, 0, n))
    ctx_spec = pl.BlockSpec((None, rows_c, SUBLANES, cw), lambda b, n: (b, 0, 0, n))
    vec2 = pl.BlockSpec((2, 1, cw), lambda b, n: (0, 0, n))
    mat2 = pl.BlockSpec((2, LRU_LANE_BLOCKS, LANES, LANES), lambda b, n: (0, n, 0, 0))
    return pl.pallas_call(
        functools.partial(_lru_kernel, rows=rows, rows_c=rows_c),
        out_shape=(jax.ShapeDtypeStruct(xr.shape, BF16), jax.ShapeDtypeStruct(xrc.shape, BF16)),
        grid=(batch, width // cw),
        in_specs=[seq_spec, seq_spec, ctx_spec, ctx_spec,
                  pl.BlockSpec((CONV_W, 1, cw), lambda b, n: (0, 0, n)),
                  pl.BlockSpec((1, 1, cw), lambda b, n: (0, 0, n)),
                  mat2, vec2, mat2, vec2, vec2],
        out_specs=(seq_spec, ctx_spec),
        scratch_shapes=[pltpu.VMEM((rows + CONV_W - 1, SUBLANES, cw), F32),
                        pltpu.VMEM((2, rows, SUBLANES, cw), F32),
                        pltpu.VMEM((2, rows, SUBLANES, cw), F32)],
        compiler_params=_cparams(("parallel", "parallel")),
        name="rglru",
    )(xr, xg, xrc, xgc, conv_w.reshape(CONV_W, 1, width), conv_b.reshape(1, 1, width),
      wa.astype(BF16), ba.reshape(2, 1, width), wx.astype(BF16), bx.reshape(2, 1, width),
      lam.reshape(2, 1, width))


def _to_chunked(a, batch):
    t = a.shape[0] // batch
    return a.reshape(batch, LRU_CHUNKS, t // LRU_CHUNKS, a.shape[1]).transpose(0, 2, 1, 3)


def _from_chunked(a):
    b, r, c, w = a.shape
    return a.transpose(0, 2, 1, 3).reshape(b * r * c, w)


def _out_proj_kernel(a1_ref, a2_ref, w1_ref, w2_ref, x_ref, g_ref, o_ref):
    y = _dot(a1_ref[...], w1_ref[...]) + _dot(a2_ref[...], w2_ref[...])
    o_ref[...] = x_ref[...] + g_ref[...] * y


def _out_proj(a1, a1_blk, a2, a2_blk, w, x, mod3, gate_blk, rows_per_mod, mod_base, tm=2048, tn=512):
    n, d = x.shape
    kh = w.shape[0] // 2
    tm = min(tm, n)
    per = rows_per_mod // tm
    gpb = d // tn
    return pl.pallas_call(
        _out_proj_kernel,
        out_shape=jax.ShapeDtypeStruct((n, d), F32),
        grid=(n // tm, d // tn),
        in_specs=[pl.BlockSpec((tm, kh), lambda i, j: (i, a1_blk)),
                  pl.BlockSpec((tm, kh), lambda i, j: (i, a2_blk)),
                  pl.BlockSpec((kh, tn), lambda i, j: (0, j)),
                  pl.BlockSpec((kh, tn), lambda i, j: (1, j)),
                  pl.BlockSpec((tm, tn), lambda i, j: (i, j)),
                  pl.BlockSpec((None, 1, tn), lambda i, j: (mod_base + i // per, 0, gate_blk * gpb + j))],
        out_specs=pl.BlockSpec((tm, tn), lambda i, j: (i, j)),
        compiler_params=_cparams(("parallel", "parallel")),
        name="out_proj",
    )(a1, a2, w, w, x, mod3)


def _lane_block_max(s):
    mm = s[:, 0:LANES]
    for t in range(1, s.shape[1] // LANES):
        mm = jnp.maximum(mm, s[:, t * LANES:(t + 1) * LANES])
    return mm


def _exp_blocks(s, mrep):
    return jnp.concatenate(
        [jnp.exp((s[:, t * LANES:(t + 1) * LANES] - mrep).astype(BF16)) for t in range(s.shape[1] // LANES)],
        axis=1)


def _dense_attn_kernel(q_ref, k_ref, v_ref, kx_ref, vx_ref, o_ref,
                       s_ref, sx_ref, m_ref, acc_ref, va_ref, vax_ref, *, tq, seq, kc):
    n_chunks = seq // kc

    @pl.when(pl.program_id(2) == 0)
    def _():
        for c in range(n_chunks):
            va_ref[c, :, 0:HEAD_DIM] = v_ref[c * kc:(c + 1) * kc, :]
            va_ref[c, :, HEAD_DIM:] = jnp.ones((kc, HEAD_DIM), BF16)
        vax_ref[:, 0:HEAD_DIM] = vx_ref[...]
        vax_ref[:, HEAD_DIM:] = jnp.ones((vx_ref.shape[0], HEAD_DIM), BF16)

    q4 = jnp.concatenate([q_ref[:, _head_cols(g)] for g in range(GQA_GROUP)], axis=0)
    sx = _dot_nt(q4, kx_ref[...])
    sx_ref[...] = sx
    m_ref[...] = _lane_block_max(sx)

    def sweep1(c, carry):
        s = _dot_nt(q4, k_ref[pl.ds(pl.multiple_of(c * kc, kc), kc), :])
        s_ref[c] = s
        m_ref[...] = jnp.maximum(m_ref[...], _lane_block_max(s))
        return carry

    lax.fori_loop(0, n_chunks, sweep1, 0)
    mrep = jnp.broadcast_to(jnp.max(m_ref[...], axis=-1, keepdims=True), m_ref.shape)
    m_ref[...] = mrep
    acc_ref[...] = _dot(_exp_blocks(sx_ref[...], mrep), vax_ref[...])

    def sweep2(c, carry):
        acc_ref[...] += _dot(_exp_blocks(s_ref[c], m_ref[...]), va_ref[c])
        return carry

    lax.fori_loop(0, n_chunks, sweep2, 0)
    o = acc_ref[:, 0:HEAD_DIM] / acc_ref[:, HEAD_DIM:]
    for g in range(GQA_GROUP):
        o_ref[:, _head_cols(g)] = o[g * tq:(g + 1) * tq].astype(o_ref.dtype)


def _dense_attn(q_r, k_r, proj, kx_r, proj_c, batch, seq, ctx_len, tq=256, kc=1024):
    gw = GQA_GROUP * HEAD_DIM
    nq = seq // tq
    rows = GQA_GROUP * tq
    v_blk = C_Q_HEADS + C_KV_HEADS
    vx_blk = C_KV_HEADS
    return pl.pallas_call(
        functools.partial(_dense_attn_kernel, tq=tq, seq=seq, kc=kc),
        out_shape=jax.ShapeDtypeStruct((batch * seq, C_Q_HEADS * HEAD_DIM), BF16),
        grid=(batch, C_KV_HEADS, nq),
        in_specs=[pl.BlockSpec((tq, gw), lambda b, h, i: (b * nq + i, h)),
                  pl.BlockSpec((seq, HEAD_DIM), lambda b, h, i: (b, h)),
                  pl.BlockSpec((seq, HEAD_DIM), lambda b, h, i: (b, v_blk + h)),
                  pl.BlockSpec((ctx_len, HEAD_DIM), lambda b, h, i: (b, h)),
                  pl.BlockSpec((ctx_len, HEAD_DIM), lambda b, h, i: (b, vx_blk + h))],
        out_specs=pl.BlockSpec((tq, gw), lambda b, h, i: (b * nq + i, h)),
        scratch_shapes=[pltpu.VMEM((seq // kc, rows, kc), F32),
                        pltpu.VMEM((rows, ctx_len), F32),
                        pltpu.VMEM((rows, LANES), F32),
                        pltpu.VMEM((rows, 2 * HEAD_DIM), F32),
                        pltpu.VMEM((seq // kc, kc, 2 * HEAD_DIM), BF16),
                        pltpu.VMEM((ctx_len, 2 * HEAD_DIM), BF16)],
        compiler_params=_cparams(("parallel", "parallel", "arbitrary")),
        name="dense_attn",
    )(q_r, k_r, proj, kx_r, proj_c)


def _router_rows(biased, scores):
    v = [biased[e:e + 1, :] for e in range(N_EXPERTS)]
    s = [scores[e:e + 1, :] for e in range(N_EXPERTS)]

    def top2_sum(vals):
        best = vals[0] + vals[1]
        for i in range(len(vals)):
            for j in range(i + 1, len(vals)):
                if (i, j) != (0, 1):
                    best = jnp.maximum(best, vals[i] + vals[j])
        return best

    gsum = [top2_sum(v[g * EXPERTS_PER_GROUP:(g + 1) * EXPERTS_PER_GROUP]) for g in range(N_GROUPS)]
    sel = jnp.zeros_like(gsum[0], dtype=jnp.int32)
    best = gsum[0]
    for g in range(1, N_GROUPS):
        take = gsum[g] > best
        sel = jnp.where(take, g, sel)
        best = jnp.where(take, gsum[g], best)

    def pick_group(rows, i):
        out = rows[i]
        for g in range(1, N_GROUPS):
            out = jnp.where(sel == g, rows[g * EXPERTS_PER_GROUP + i], out)
        return out

    cand = [pick_group(v, i) for i in range(EXPERTS_PER_GROUP)]
    cand_s = [pick_group(s, i) for i in range(EXPERTS_PER_GROUP)]
    i1 = jnp.zeros_like(sel)
    b1 = cand[0]
    for i in range(1, EXPERTS_PER_GROUP):
        take = cand[i] > b1
        i1 = jnp.where(take, i, i1)
        b1 = jnp.where(take, cand[i], b1)
    i2 = jnp.full_like(sel, -1)
    b2 = jnp.full_like(b1, -jnp.inf)
    for i in range(EXPERTS_PER_GROUP):
        take = (i1 != i) & ((cand[i] > b2) | (i2 < 0))
        i2 = jnp.where(take, i, i2)
        b2 = jnp.where(take, cand[i], b2)

    def pick_idx(rows, idx):
        out = rows[0]
        for i in range(1, EXPERTS_PER_GROUP):
            out = jnp.where(idx == i, rows[i], out)
        return out

    s0 = pick_idx(cand_s, i1)
    s1 = pick_idx(cand_s, i2)
    tot = s0 + s1
    e0 = (sel * EXPERTS_PER_GROUP + i1).astype(F32)
    e1 = (sel * EXPERTS_PER_GROUP + i2).astype(F32)
    return e0, e1, s0 / tot, s1 / tot


def _norm_router_kernel(*refs, tm, n_lat_tiles, has_ctx):
    if has_ctx:
        x_ref, xc_ref, g_ref, sh_ref, sc_ref, rw_ref, rb_ref, h_ref, r_ref = refs
        i = pl.program_id(0)

        @pl.when(i < n_lat_tiles)
        def _():
            _norm_mod_rows(x_ref, g_ref, sh_ref, sc_ref, h_ref, tm)

        @pl.when(i >= n_lat_tiles)
        def _():
            _norm_mod_rows(xc_ref, g_ref, sh_ref, sc_ref, h_ref, tm)
    else:
        x_ref, g_ref, sh_ref, sc_ref, rw_ref, rb_ref, h_ref, r_ref = refs
        _norm_mod_rows(x_ref, g_ref, sh_ref, sc_ref, h_ref, tm)

    logits = _dot_nt(rw_ref[...], h_ref[...])
    scores = jax.nn.sigmoid(logits)
    e0, e1, w0, w1 = _router_rows(scores + rb_ref[...], scores)
    zero = jnp.zeros_like(w0)
    r_ref[...] = jnp.concatenate([e0, e1, w0, w1, zero, zero, zero, zero], axis=0)


def _lat_ctx_maps(n_lat, per, ctx_mod_row):
    def mod_row(i):
        return jnp.where(i < n_lat, i // per, ctx_mod_row)

    def lat(i):
        return (jnp.minimum(i, n_lat - 1), 0)

    def ctx(i):
        return (jnp.maximum(i - n_lat, 0), 0)

    return mod_row, lat, ctx


def _norm_router(x, cx, g, mod3, shift_blk, scale_blk, rw_t, rb, rows_per_mod, ctx_mod_row, tm=512):
    n, d = x.shape
    n_lat = n // tm
    has_ctx = cx is not None
    ntot = n_lat + (cx.shape[0] // tm if has_ctx else 0)
    mod_row, lat, ctx = _lat_ctx_maps(n_lat, rows_per_mod // tm, ctx_mod_row)
    row_specs = [pl.BlockSpec((tm, d), lat)] + ([pl.BlockSpec((tm, d), ctx)] if has_ctx else [])
    row_args = [x] + ([cx] if has_ctx else [])
    return pl.pallas_call(
        functools.partial(_norm_router_kernel, tm=tm, n_lat_tiles=n_lat, has_ctx=has_ctx),
        out_shape=(jax.ShapeDtypeStruct((ntot * tm, d), BF16),
                   jax.ShapeDtypeStruct((SUBLANES, ntot * tm), F32)),
        grid=(ntot,),
        in_specs=row_specs + [
            pl.BlockSpec((1, d), lambda i: (0, 0)),
            pl.BlockSpec((None, 1, d), lambda i: (mod_row(i), 0, shift_blk)),
            pl.BlockSpec((None, 1, d), lambda i: (mod_row(i), 0, scale_blk)),
            pl.BlockSpec((N_EXPERTS, d), lambda i: (0, 0)),
            pl.BlockSpec((N_EXPERTS, 1), lambda i: (0, 0))],
        out_specs=(pl.BlockSpec((tm, d), lambda i: (i, 0)),
                   pl.BlockSpec((SUBLANES, tm), lambda i: (0, i))),
        compiler_params=_cparams(("arbitrary",)),
        name="norm_router",
    )(*row_args, g.reshape(1, d), mod3, mod3, rw_t, rb)


def _gmm_kernel(te_ref, tv_ref, xs_ref, wg_ref, wu_ref, wd_ref, *rest, tile0):
    o_ref = rest[-1]
    t = tile0 + pl.program_id(0)

    @pl.when(tv_ref[t] > 0)
    def _():
        x = xs_ref[...]
        gate = _dot(x, wg_ref[0])
        up = _dot(x, wu_ref[0])
        h1 = ((gate * jax.nn.sigmoid(gate)) * up).astype(BF16)
        o_ref[...] = _dot(h1, wd_ref[0]).astype(o_ref.dtype)

    @pl.when(tv_ref[t] == 0)
    def _():
        o_ref[...] = jnp.zeros(o_ref.shape, o_ref.dtype)


def _grouped_mlp(tile_expert, tile_valid, xs_part, tile0, ys_prev, wg, wu, wd, tm=MOE_TM):
    d = xs_part.shape[1]
    dff = wg.shape[2]
    n_tiles = tile_expert.shape[0]
    in_specs = [pl.BlockSpec((tm, d), lambda i, te, tv: (i, 0)),
                pl.BlockSpec((1, d, dff), lambda i, te, tv: (te[tile0 + i], 0, 0)),
                pl.BlockSpec((1, d, dff), lambda i, te, tv: (te[tile0 + i], 0, 0)),
                pl.BlockSpec((1, dff, d), lambda i, te, tv: (te[tile0 + i], 0, 0))]
    args = [tile_expert, tile_valid, xs_part, wg, wu, wd]
    aliases = {}
    if ys_prev is not None:
        in_specs.append(pl.BlockSpec(memory_space=pl.ANY))
        args.append(ys_prev)
        aliases = {len(args) - 1: 0}
    grid_spec = pltpu.PrefetchScalarGridSpec(
        num_scalar_prefetch=2,
        grid=(xs_part.shape[0] // tm,),
        in_specs=in_specs,
        out_specs=pl.BlockSpec((tm, d), lambda i, te, tv: (tile0 + i, 0)),
    )
    return pl.pallas_call(
        functools.partial(_gmm_kernel, tile0=tile0),
        out_shape=jax.ShapeDtypeStruct((n_tiles * tm, d), BF16),
        grid_spec=grid_spec,
        input_output_aliases=aliases,
        compiler_params=_cparams(("arbitrary",)),
        name="grouped_mlp",
    )(*args)


def _combine_kernel(*refs, n_lat_tiles, has_ctx):
    if has_ctx:
        x_ref, xc_ref, y0_ref, y1_ref, r_ref, g_ref, o_ref, oc_ref = refs
    else:
        x_ref, y0_ref, y1_ref, r_ref, g_ref, o_ref = refs
    w0 = r_ref[:, 2:3]
    w1 = r_ref[:, 3:4]
    f = g_ref[...] * (w0 * y0_ref[...].astype(F32) + w1 * y1_ref[...].astype(F32))
    if not has_ctx:
        o_ref[...] = x_ref[...] + f
        return
    i = pl.program_id(0)

    @pl.when(i < n_lat_tiles)
    def _():
        o_ref[...] = x_ref[...] + f

    @pl.when(i >= n_lat_tiles)
    def _():
        oc_ref[...] = xc_ref[...] + f


def _combine(x, cx, yg, route_cols, mod3, gate_blk, rows_per_mod, ctx_mod_row, tm=512):
    n, d = x.shape
    n_lat = n // tm
    has_ctx = cx is not None
    ntot = n_lat + (cx.shape[0] // tm if has_ctx else 0)
    mod_row, lat, ctx = _lat_ctx_maps(n_lat, rows_per_mod // tm, ctx_mod_row)
    row_specs = [pl.BlockSpec((tm, d), lat)] + ([pl.BlockSpec((tm, d), ctx)] if has_ctx else [])
    row_args = [x] + ([cx] if has_ctx else [])
    out_shape = [jax.ShapeDtypeStruct(x.shape, F32)] + ([jax.ShapeDtypeStruct(cx.shape, F32)] if has_ctx else [])
    out = pl.pallas_call(
        functools.partial(_combine_kernel, n_lat_tiles=n_lat, has_ctx=has_ctx),
        out_shape=tuple(out_shape),
        grid=(ntot,),
        in_specs=row_specs + [
            pl.BlockSpec((tm, d), lambda i: (i, 0)),
            pl.BlockSpec((tm, d), lambda i: (ntot + i, 0)),
            pl.BlockSpec((tm, SUBLANES), lambda i: (i, 0)),
            pl.BlockSpec((None, 1, d), lambda i: (mod_row(i), 0, gate_blk))],
        out_specs=tuple(row_specs),
        compiler_params=_cparams(("arbitrary",)),
        name="moe_combine",
    )(*row_args, yg, yg, route_cols, mod3)
    return out if has_ctx else (out[0], None)


def _dispatch_plan(route, tm):
    n = route.shape[1]
    e_flat = jnp.concatenate([route[0], route[1]]).astype(jnp.int32)
    n_assign = 2 * n
    n_tiles = n_assign // tm + N_EXPERTS
    experts = jnp.arange(N_EXPERTS, dtype=jnp.int32)[:, None]
    onehot = (experts == e_flat[None, :]).astype(jnp.int32)
    csum = jnp.cumsum(onehot, axis=1)
    counts = csum[:, -1]
    padded = ((counts + tm - 1) // tm) * tm
    ends_p = jnp.cumsum(padded)
    starts_p = ends_p - padded
    starts_c = jnp.cumsum(counts) - counts
    dest = jnp.sum(onehot * (csum - 1 + starts_p[:, None]), axis=0)
    order = jnp.argsort(e_flat, stable=True).astype(jnp.int32)
    p = jnp.arange(n_tiles * tm, dtype=jnp.int32)[None, :]
    owner = ((p >= starts_p[:, None]) & (p < ends_p[:, None])).astype(jnp.int32)
    within = jnp.sum(owner * (p - starts_p[:, None]), axis=0)
    live = jnp.sum(owner * (p - starts_p[:, None] < counts[:, None]), axis=0) > 0
    compact = jnp.sum(owner * starts_c[:, None], axis=0) + within
    src_tok = jnp.where(live, order[jnp.clip(compact, 0, n_assign - 1)] % n, 0)
    tile_start = jnp.arange(n_tiles, dtype=jnp.int32) * tm
    tile_valid = (tile_start < ends_p[-1]).astype(jnp.int32)
    last_tile = jnp.maximum(ends_p[-1] // tm - 1, 0) * tm
    tile_expert = jnp.searchsorted(ends_p, jnp.minimum(tile_start, last_tile), side="right").astype(jnp.int32)
    tile_expert = jnp.minimum(tile_expert, N_EXPERTS - 1)
    return src_tok, dest, tile_expert, tile_valid


def _moe(x, cx, g, mod3, rw_t, rb, wg, wu, wd, rows_per_mod, ctx_mod_row):
    h, route = _norm_router(x, cx, g, mod3, 3, 4, rw_t, rb, rows_per_mod, ctx_mod_row)
    src_tok, dest, tile_expert, tile_valid = _dispatch_plan(route, MOE_TM)
    n_tiles = tile_expert.shape[0]
    ys = None
    for t0, t1 in ((0, n_tiles // 2), (n_tiles // 2, n_tiles)):
        xs = h.at[src_tok[t0 * MOE_TM:t1 * MOE_TM]].get(mode="promise_in_bounds")
        ys = _grouped_mlp(tile_expert, tile_valid, xs, t0, ys, wg, wu, wd)
    yg = ys.at[dest].get(mode="promise_in_bounds")
    route_cols = route.T
    return _combine(x, cx, yg, route_cols, mod3, 5, rows_per_mod, ctx_mod_row)


def _rope_tables(seq):
    rows = seq // GRID_W
    row = jnp.repeat(jnp.arange(rows, dtype=F32), GRID_W)
    col = jnp.tile(jnp.arange(GRID_W, dtype=F32), rows)
    n_freq = HEAD_DIM // 4
    inv_freq = ROPE_BASE ** (-jnp.arange(n_freq, dtype=F32) / n_freq)
    ang = jnp.concatenate([row[:, None] * inv_freq, col[:, None] * inv_freq], axis=-1)
    cos, sin = jnp.cos(ang), jnp.sin(ang)
    return jnp.concatenate([cos, cos], axis=-1), jnp.concatenate([-sin, sin], axis=-1)


def kernel(x, c, ctx, c_ctx, ada_w, ada_b, norm_mix, norm_ffn, ab_w_in, ab_q_gain, ab_k_gain, ab_sink, ab_conv_w, ab_conv_b, ab_gate_a_w, ab_gate_a_b, ab_gate_x_w, ab_gate_x_b, ab_lru_lambda, ab_w_out, gqa_w_in, gqa_q_gain, gqa_k_gain, gqa_w_out, router_w, router_bias, moe_w_gate, moe_w_up, moe_w_down):
    batch, seq, d = x.shape
    ctx_len = ctx.shape[1]
    depth = ada_w.shape[0]
    assert depth == 2 and batch < SUBLANES
    n_lat = batch * seq
    n_ctx = batch * ctx_len
    ctx_row = batch

    xl = x.reshape(n_lat, d)
    xc = ctx.reshape(n_ctx, d)
    cc = jnp.zeros((SUBLANES, d), F32).at[:batch].set(c).at[ctx_row].set(c_ctx)
    mod = _ada(cc, ada_w, ada_b)
    cos2, sin2 = _rope_tables(seq)
    rw_t = router_w.T.astype(BF16)
    rb = router_bias.reshape(N_EXPERTS, 1).astype(F32)
    experts = [tuple(_cast_bf16(w, l) for w in (moe_w_gate, moe_w_up, moe_w_down)) for l in range(depth)]

    mod3 = mod[0].reshape(SUBLANES, 1, 6 * d)
    w_in = ab_w_in[0].astype(BF16)
    proj = _norm_mod_matmul(xl, norm_mix[0], mod3, 0, 1, w_in, seq, 0)
    proj_c = _norm_mod_matmul(xc, norm_mix[0], mod3, 0, 1, w_in, n_ctx, ctx_row)
    q_r = _qknorm(proj, 0, A_Q_HEADS, ab_q_gain[0], cos2, sin2, out_scale=ATTN_SCALE)
    k_r = _qknorm(proj, A_Q_HEADS // A_KV_HEADS, A_KV_HEADS, ab_k_gain[0], cos2, sin2)
    qx_r = _qknorm(proj_c, 0, A_Q_HEADS, ab_q_gain[0], out_scale=ATTN_SCALE)
    kx_r = _qknorm(proj_c, A_Q_HEADS // A_KV_HEADS, A_KV_HEADS, ab_k_gain[0])
    att = _win_attn(ab_sink[0], q_r, k_r, proj, kx_r, proj_c, batch, seq, ctx_len)
    att_c = _ctx_attn(ab_sink[0], qx_r, kx_r, proj_c, batch, ctx_len)

    lru_w = ab_conv_w.shape[2]
    c0 = (A_Q_HEADS + 2 * A_KV_HEADS) * HEAD_DIM
    y_p, yc_p = _lru(_to_chunked(proj[:, c0:c0 + lru_w], batch),
                     _to_chunked(proj[:, c0 + lru_w:c0 + 2 * lru_w], batch),
                     _to_chunked(proj_c[:, c0:c0 + lru_w], batch),
                     _to_chunked(proj_c[:, c0 + lru_w:c0 + 2 * lru_w], batch),
                     ab_conv_w[0], ab_conv_b[0], ab_gate_a_w[0], ab_gate_a_b[0],
                     ab_gate_x_w[0], ab_gate_x_b[0], ab_lru_lambda[0])
    w_out = ab_w_out[0].astype(BF16)
    xl = _out_proj(att, 0, _from_chunked(y_p), 0, w_out, xl, mod3, 2, seq, 0)
    xc = _out_proj(att_c, 0, _from_chunked(yc_p), 0, w_out, xc, mod3, 2, n_ctx, ctx_row)
    xl, xc = _moe(xl, xc, norm_ffn[0], mod3, rw_t, rb, *experts[0], seq, ctx_row)

    mod3 = mod[1].reshape(SUBLANES, 1, 6 * d)
    w_in = gqa_w_in[0].astype(BF16)
    cw = C_Q_HEADS * HEAD_DIM
    proj = _norm_mod_matmul(xl, norm_mix[1], mod3, 0, 1, w_in, seq, 0)
    proj_c = _norm_mod_matmul(xc, norm_mix[1], mod3, 0, 1, w_in[:, cw:], n_ctx, ctx_row)
    q_r = _qknorm(proj, 0, C_Q_HEADS, gqa_q_gain[0], cos2, sin2, out_scale=ATTN_SCALE)
    k_r = _qknorm(proj, C_Q_HEADS // C_KV_HEADS, C_KV_HEADS, gqa_k_gain[0], cos2, sin2)
    kx_r = _qknorm(proj_c, 0, C_KV_HEADS, gqa_k_gain[0])
    att = _dense_attn(q_r, k_r, proj, kx_r, proj_c, batch, seq, ctx_len)
    xl = _out_proj(att, 0, att, 1, gqa_w_out[0].astype(BF16), xl, mod3, 2, seq, 0)
    xl, _ = _moe(xl, None, norm_ffn[1], mod3, rw_t, rb, *experts[1], seq, ctx_row)
    return xl.reshape(batch, seq, d)
```

```python
import functools

import jax
import jax.numpy as jnp
from jax import lax
from jax.experimental import pallas as pl
from jax.experimental.pallas import tpu as pltpu
from jax.experimental.pallas import tpu_sc as plsc

F32 = jnp.float32
BF16 = jnp.bfloat16

LANES = 128
SUBLANES = 8
VMEM_LIMIT = 56 * 1024 * 1024

HEAD_DIM = 128
GRID_W = 64
WINDOW = 128
BLOCK = 128
ROPE_BASE = 10000.0
EPS = 1e-6
ATTN_SCALE = HEAD_DIM ** -0.5
A_Q_HEADS, A_KV_HEADS = 8, 2
C_Q_HEADS, C_KV_HEADS = 16, 4
GQA_GROUP = 4
LRU_C = 8.0
CONV_W = 4
CONV_LEFT = 2
N_EXPERTS = 16
N_GROUPS = 4
EXPERTS_PER_GROUP = 4
NEG_BIG = -1e30

LRU_CHUNKS = SUBLANES
LRU_JB = 8
LRU_LANE_BLOCKS = 2
MOE_TM = 256
SC_CORES = 2
SC_WORKERS = 32
SC_GATHER_WINDOW = 32


def _cparams(sem, vmem=VMEM_LIMIT):
    return pltpu.CompilerParams(dimension_semantics=sem, vmem_limit_bytes=vmem)


def _dot(a, b):
    return jnp.dot(a, b, preferred_element_type=F32)


def _dot_nt(a, b):
    return lax.dot_general(a, b, (((1,), (1,)), ((), ())), preferred_element_type=F32)


def _cast_kernel(x_ref, o_ref):
    o_ref[...] = x_ref[...].astype(o_ref.dtype)


def _cast_bf16(w, layer, block_elems=2 * 1024 * 1024):
    shape = w.shape[1:]
    cols = shape[-1]
    w2 = w.reshape(-1, cols)
    rows = w2.shape[0] // w.shape[0]
    tr = min(block_elems // cols, rows)
    nblk = rows // tr
    out = pl.pallas_call(
        _cast_kernel,
        out_shape=jax.ShapeDtypeStruct((rows, cols), BF16),
        grid=(nblk,),
        in_specs=[pl.BlockSpec((tr, cols), lambda i: (layer * nblk + i, 0))],
        out_specs=pl.BlockSpec((tr, cols), lambda i: (i, 0)),
        compiler_params=_cparams(("parallel",)),
        name="cast_bf16",
    )(w2)
    return out.reshape(shape)


def _ada_kernel(c_ref, w_ref, b_ref, o_ref):
    c = c_ref[...]
    s = (c * jax.nn.sigmoid(c)).astype(BF16)
    o_ref[0] = _dot(s, w_ref[0].astype(BF16)) + b_ref[0]


def _ada(cc, ada_w, ada_b):
    depth, d, n = ada_w.shape
    tn = 1024
    return pl.pallas_call(
        _ada_kernel,
        out_shape=jax.ShapeDtypeStruct((depth, SUBLANES, n), F32),
        grid=(depth, n // tn),
        in_specs=[pl.BlockSpec((SUBLANES, d), lambda l, j: (0, 0)),
                  pl.BlockSpec((1, d, tn), lambda l, j: (l, 0, j)),
                  pl.BlockSpec((1, 1, tn), lambda l, j: (l, 0, j))],
        out_specs=pl.BlockSpec((1, SUBLANES, tn), lambda l, j: (l, 0, j)),
        compiler_params=_cparams(("arbitrary", "arbitrary")),
        name="ada",
    )(cc, ada_w, ada_b.reshape(depth, 1, n))


def _norm_mod_rows(x_ref, g_ref, sh_ref, sc_ref, dst_ref, tm, rc=128):
    g = g_ref[...]
    sc1 = 1.0 + sc_ref[...]
    sh = sh_ref[...]

    def body(r, carry):
        rows = pl.ds(pl.multiple_of(r * rc, rc), rc)
        xf = x_ref[rows, :]
        ms = jnp.mean(xf * xf, axis=-1, keepdims=True)
        xn = (xf * lax.rsqrt(ms + EPS)) * g
        dst_ref[rows, :] = (xn * sc1 + sh).astype(dst_ref.dtype)
        return carry

    lax.fori_loop(0, tm // rc, body, 0)


def _nm_mm_kernel(x_ref, g_ref, sh_ref, sc_ref, w_ref, o_ref, hn_ref, *, tm, tn):
    _norm_mod_rows(x_ref, g_ref, sh_ref, sc_ref, hn_ref, tm)
    h = hn_ref[...]
    for j in range(w_ref.shape[1] // tn):
        cols = slice(j * tn, (j + 1) * tn)
        o_ref[:, cols] = _dot(h, w_ref[:, cols]).astype(o_ref.dtype)


def _norm_mod_matmul(x, g, mod3, shift_blk, scale_blk, w, rows_per_mod, mod_base, tm=512, tn=512):
    n, d = x.shape
    nout = w.shape[1]
    tm = min(tm, n)
    tn = min(tn, nout)
    per = rows_per_mod // tm

    def mod_row(i):
        return mod_base + i // per

    return pl.pallas_call(
        functools.partial(_nm_mm_kernel, tm=tm, tn=tn),
        out_shape=jax.ShapeDtypeStruct((n, nout), BF16),
        grid=(n // tm,),
        in_specs=[pl.BlockSpec((tm, d), lambda i: (i, 0)),
                  pl.BlockSpec((1, d), lambda i: (0, 0)),
                  pl.BlockSpec((None, 1, d), lambda i: (mod_row(i), 0, shift_blk)),
                  pl.BlockSpec((None, 1, d), lambda i: (mod_row(i), 0, scale_blk)),
                  pl.BlockSpec((d, nout), lambda i: (0, 0), pipeline_mode=pl.Buffered(1))],
        out_specs=pl.BlockSpec((tm, nout), lambda i: (i, 0)),
        scratch_shapes=[pltpu.VMEM((tm, d), BF16)],
        compiler_params=_cparams(("parallel",)),
        name="norm_mod_matmul",
    )(x, g.reshape(1, d), mod3, mod3, w)


def _qknorm_kernel(*refs, n_heads, rope, out_scale):
    if rope:
        x_ref, g_ref, cos_ref, sin_ref, o_ref = refs
        cos2 = cos_ref[...]
        sin2 = sin_ref[...]
    else:
        x_ref, g_ref, o_ref = refs
    g = g_ref[...]
    for h in range(n_heads):
        cols = slice(h * HEAD_DIM, (h + 1) * HEAD_DIM)
        xf = x_ref[:, cols].astype(F32)
        ms = jnp.mean(xf * xf, axis=-1, keepdims=True)
        xn = (xf * lax.rsqrt(ms + EPS)) * g
        if rope:
            xn = xn * cos2 + pltpu.roll(xn, HEAD_DIM // 2, 1) * sin2
        if out_scale != 1.0:
            xn = xn * out_scale
        o_ref[:, cols] = xn.astype(o_ref.dtype)


def _qknorm(proj, col_blk, n_heads, gain, cos2=None, sin2=None, out_scale=1.0, tr=512):
    n = proj.shape[0]
    tr = min(tr, n)
    w = n_heads * HEAD_DIM
    rope = cos2 is not None
    in_specs = [pl.BlockSpec((tr, w), lambda i: (i, col_blk)),
                pl.BlockSpec((1, HEAD_DIM), lambda i: (0, 0))]
    args = [proj, gain.reshape(1, HEAD_DIM)]
    if rope:
        per = cos2.shape[0] // tr
        in_specs += [pl.BlockSpec((tr, HEAD_DIM), lambda i: (i % per, 0)),
                     pl.BlockSpec((tr, HEAD_DIM), lambda i: (i % per, 0))]
        args += [cos2, sin2]
    return pl.pallas_call(
        functools.partial(_qknorm_kernel, n_heads=n_heads, rope=rope, out_scale=out_scale),
        out_shape=jax.ShapeDtypeStruct((n, w), BF16),
        grid=(n // tr,),
        in_specs=in_specs,
        out_specs=pl.BlockSpec((tr, w), lambda i: (i, 0)),
        compiler_params=_cparams(("parallel",)),
        name="qknorm",
    )(*args)


def _head_cols(h):
    return slice(h * HEAD_DIM, (h + 1) * HEAD_DIM)


def _stack_group(q_ref, kvh):
    return jnp.concatenate([q_ref[:, _head_cols(kvh * GQA_GROUP + g)] for g in range(GQA_GROUP)], axis=0)


def _sink_col(sink_ref, kvh, rows):
    return jnp.concatenate([jnp.full((rows, 1), sink_ref[kvh * GQA_GROUP + g], F32)
                            for g in range(GQA_GROUP)], axis=0)


def _win_attn_kernel(sink_ref, q_ref, kp_ref, kc_ref, kn_ref, vp_ref, vc_ref, vn_ref,
                     kx_ref, vx_ref, o_ref, *, seq, ctx_len):
    n = pl.program_id(1)
    rows = GQA_GROUP * BLOCK
    nk = 3 * BLOCK + ctx_len
    qi = lax.broadcasted_iota(jnp.int32, (rows, nk), 0) % BLOCK
    kj = lax.broadcasted_iota(jnp.int32, (rows, nk), 1)
    kpos = n * BLOCK - BLOCK + kj
    in_band = (jnp.abs(kj - BLOCK - qi) <= WINDOW) & (kpos >= 0) & (kpos < seq)
    valid = (kj >= 3 * BLOCK) | in_band
    ones = jnp.ones((nk, HEAD_DIM), BF16)
    for kvh in range(A_KV_HEADS):
        cols = _head_cols(kvh)
        q4 = _stack_group(q_ref, kvh)
        ka = jnp.concatenate([kp_ref[:, cols], kc_ref[:, cols], kn_ref[:, cols], kx_ref[:, cols]], axis=0)
        va = jnp.concatenate([vp_ref[:, cols], vc_ref[:, cols], vn_ref[:, cols], vx_ref[:, cols]], axis=0)
        s = jnp.where(valid, _dot_nt(q4, ka), NEG_BIG)
        sk = _sink_col(sink_ref, kvh, BLOCK)
        m = jnp.maximum(jnp.max(s, axis=-1, keepdims=True), sk)
        p = jnp.exp((s - m).astype(BF16))
        acc = _dot(p, jnp.concatenate([va, ones], axis=1))
        o = acc[:, 0:HEAD_DIM] / (acc[:, HEAD_DIM:] + jnp.exp(sk - m))
        for g in range(GQA_GROUP):
            o_ref[:, _head_cols(kvh * GQA_GROUP + g)] = o[g * BLOCK:(g + 1) * BLOCK].astype(o_ref.dtype)


def _win_attn(sink, q_r, k_r, proj, kx_r, proj_c, batch, seq, ctx_len):
    nb = seq // BLOCK
    kvw = A_KV_HEADS * HEAD_DIM
    v_blk = (A_Q_HEADS * HEAD_DIM + kvw) // kvw

    def prev(b, n):
        return b * nb + jnp.maximum(n - 1, 0)

    def cur(b, n):
        return b * nb + n

    def nxt(b, n):
        return b * nb + jnp.minimum(n + 1, nb - 1)

    return pl.pallas_call(
        functools.partial(_win_attn_kernel, seq=seq, ctx_len=ctx_len),
        out_shape=jax.ShapeDtypeStruct((batch * seq, A_Q_HEADS * HEAD_DIM), BF16),
        grid=(batch, nb),
        in_specs=[pl.BlockSpec(memory_space=pltpu.SMEM),
                  pl.BlockSpec((BLOCK, A_Q_HEADS * HEAD_DIM), lambda b, n: (cur(b, n), 0)),
                  pl.BlockSpec((BLOCK, kvw), lambda b, n: (prev(b, n), 0)),
                  pl.BlockSpec((BLOCK, kvw), lambda b, n: (cur(b, n), 0)),
                  pl.BlockSpec((BLOCK, kvw), lambda b, n: (nxt(b, n), 0)),
                  pl.BlockSpec((BLOCK, kvw), lambda b, n: (prev(b, n), v_blk)),
                  pl.BlockSpec((BLOCK, kvw), lambda b, n: (cur(b, n), v_blk)),
                  pl.BlockSpec((BLOCK, kvw), lambda b, n: (nxt(b, n), v_blk)),
                  pl.BlockSpec((ctx_len, kvw), lambda b, n: (b, 0)),
                  pl.BlockSpec((ctx_len, kvw), lambda b, n: (b, v_blk))],
        out_specs=pl.BlockSpec((BLOCK, A_Q_HEADS * HEAD_DIM), lambda b, n: (cur(b, n), 0)),
        compiler_params=_cparams(("parallel", "parallel")),
        name="win_attn",
    )(sink, q_r, k_r, k_r, k_r, proj, proj, proj, kx_r, proj_c)


def _ctx_attn_kernel(sink_ref, q_ref, k_ref, v_ref, o_ref, *, ctx_len):
    kvh = pl.program_id(1)
    q4 = jnp.concatenate([q_ref[:, _head_cols(g)] for g in range(GQA_GROUP)], axis=0)
    s = _dot_nt(q4, k_ref[...])
    sk = jnp.concatenate([jnp.full((ctx_len, 1), sink_ref[kvh * GQA_GROUP + g], F32)
                          for g in range(GQA_GROUP)], axis=0)
    m = jnp.maximum(jnp.max(s, axis=-1, keepdims=True), sk)
    p = jnp.exp(s - m)
    den = jnp.sum(p, axis=-1, keepdims=True) + jnp.exp(sk - m)
    o = _dot(p.astype(BF16), v_ref[...]) / den
    for g in range(GQA_GROUP):
        o_ref[:, _head_cols(g)] = o[g * ctx_len:(g + 1) * ctx_len].astype(o_ref.dtype)


def _ctx_attn(sink, qx_r, kx_r, proj_c, batch, ctx_len):
    v_blk = A_Q_HEADS + A_KV_HEADS
    gw = GQA_GROUP * HEAD_DIM
    return pl.pallas_call(
        functools.partial(_ctx_attn_kernel, ctx_len=ctx_len),
        out_shape=jax.ShapeDtypeStruct((batch * ctx_len, A_Q_HEADS * HEAD_DIM), BF16),
        grid=(batch, A_KV_HEADS),
        in_specs=[pl.BlockSpec(memory_space=pltpu.SMEM),
                  pl.BlockSpec((ctx_len, gw), lambda b, h: (b, h)),
                  pl.BlockSpec((ctx_len, HEAD_DIM), lambda b, h: (b, h)),
                  pl.BlockSpec((ctx_len, HEAD_DIM), lambda b, h: (b, v_blk + h))],
        out_specs=pl.BlockSpec((ctx_len, gw), lambda b, h: (b, h)),
        compiler_params=_cparams(("parallel", "parallel")),
        name="ctx_attn",
    )(sink, qx_r, kx_r, proj_c)


def _sigmoid(x):
    return 0.5 * jnp.tanh(0.5 * x) + 0.5


def _gelu_tanh(x):
    return 0.5 * x * (1.0 + jnp.tanh(0.7978845608028654 * (x + 0.044715 * (x * x * x))))


def _lru_sequence(x_ref, xg_ref, y_ref, xp_ref, a_ref, b_ref, w, init, rows, jb):
    conv_w, conv_b, wa, ba, wx, bx, c_logsig = w
    width = x_ref.shape[-1]
    sub = lax.broadcasted_iota(jnp.int32, (1, SUBLANES, width), 1)

    def block_diag(ub, wd):
        return jnp.concatenate([_dot(ub[:, _head_cols(n)], wd[n]) for n in range(width // LANES)], axis=1)

    def fill(r, carry):
        rr = pl.ds(pl.multiple_of(r * jb, jb), jb)
        xp_ref[pl.ds(pl.multiple_of(r * jb, jb) + CONV_LEFT, jb)] = x_ref[rr].astype(F32)
        return carry

    lax.fori_loop(0, rows // jb, fill, 0)
    tail = x_ref[rows - CONV_LEFT:rows].astype(F32)
    xp_ref[0:CONV_LEFT] = jnp.where(sub == 0, 0.0, pltpu.roll(tail, 1, 1))
    head = x_ref[0:1].astype(F32)
    xp_ref[rows + CONV_LEFT:rows + CONV_LEFT + 1] = jnp.where(
        sub == SUBLANES - 1, 0.0, pltpu.roll(head, SUBLANES - 1, 1))

    def gates(r, carry):
        j0 = pl.multiple_of(r * jb, jb)
        u = conv_b
        for k in range(CONV_W):
            u = u + conv_w[k] * xp_ref[pl.ds(j0 + k, jb)]
        u2 = u.reshape(jb * SUBLANES, width)
        ub = u2.astype(BF16)
        for d in range(2):
            r_gate = _sigmoid(block_diag(ub, wa[d]) + ba[d])
            i_gate = _sigmoid(block_diag(ub, wx[d]) + bx[d])
            log_a = c_logsig[d] * r_gate
            a = jnp.exp(log_a)
            b = jnp.sqrt(1.0 - a * a) * (i_gate * u2)
            a_ref[d, pl.ds(j0, jb)] = a.reshape(jb, SUBLANES, width)
            b_ref[d, pl.ds(j0, jb)] = b.reshape(jb, SUBLANES, width)
        return carry

    lax.fori_loop(0, rows // jb, gates, 0)

    def scan(j, carry):
        hf, pf, hb, pb = carry
        jr = rows - 1 - j
        af = a_ref[0, j]
        hf = af * hf + b_ref[0, j]
        pf = pf * af
        b_ref[0, j] = hf
        a_ref[0, j] = pf
        ab = a_ref[1, jr]
        hb = ab * hb + b_ref[1, jr]
        pb = pb * ab
        b_ref[1, jr] = hb
        a_ref[1, jr] = pb
        return hf, pf, hb, pb

    z = jnp.zeros((SUBLANES, width), F32)
    o = jnp.ones((SUBLANES, width), F32)
    lax.fori_loop(0, rows, scan, (z, o, z, o), unroll=8)

    hf_last, pf_last = b_ref[0, rows - 1], a_ref[0, rows - 1]
    hb_last, pb_last = b_ref[1, 0], a_ref[1, 0]
    s = init[0]
    carry_f = []
    for c in range(SUBLANES):
        carry_f.append(s)
        s = hf_last[c:c + 1] + pf_last[c:c + 1] * s
    out_f = s
    s = init[1]
    carry_b = [None] * SUBLANES
    for c in reversed(range(SUBLANES)):
        carry_b[c] = s
        s = hb_last[c:c + 1] + pb_last[c:c + 1] * s
    out_b = s
    cf = jnp.concatenate(carry_f, axis=0)
    cb = jnp.concatenate(carry_b, axis=0)

    def emit(r, carry):
        rr = pl.ds(pl.multiple_of(r * jb, jb), jb)
        h = (b_ref[0, rr] + a_ref[0, rr] * cf) + (b_ref[1, rr] + a_ref[1, rr] * cb)
        y_ref[rr] = (h * _gelu_tanh(xg_ref[rr].astype(F32))).astype(y_ref.dtype)
        return carry

    lax.fori_loop(0, rows // jb, emit, 0)
    return out_f, out_b


def _lru_kernel(xr_ref, xg_ref, xrc_ref, xgc_ref, cw_ref, cb_ref, wa_ref, ba_ref, wx_ref, bx_ref,
                lam_ref, y_ref, yc_ref, xp_ref, a_ref, b_ref, *, rows, rows_c):
    c_logsig = [LRU_C * jax.nn.log_sigmoid(lam_ref[d]) for d in range(2)]
    w = ([cw_ref[k] for k in range(CONV_W)], cb_ref[0],
         [wa_ref[d] for d in range(2)], [ba_ref[d] for d in range(2)],
         [wx_ref[d] for d in range(2)], [bx_ref[d] for d in range(2)], c_logsig)
    zero = jnp.zeros((1, xr_ref.shape[-1]), F32)
    sf, sb = _lru_sequence(xrc_ref, xgc_ref, yc_ref, xp_ref, a_ref, b_ref, w, (zero, zero),
                           rows_c, min(LRU_JB, rows_c))
    _lru_sequence(xr_ref, xg_ref, y_ref, xp_ref, a_ref, b_ref, w, (sf, sb), rows, LRU_JB)


def _lru(xr, xg, xrc, xgc, conv_w, conv_b, wa, ba, wx, bx, lam):
    batch, rows, _, width = xr.shape
    rows_c = xrc.shape[1]
    cw = LRU_LANE_BLOCKS * LANES
    seq_spec = pl.BlockSpec((None, rows, SUBLANES, cw), lambda b, n: (b, 0, 0, n))
    ctx_spec = pl.BlockSpec((None, rows_c, SUBLANES, cw), lambda b, n: (b, 0, 0, n))
    vec2 = pl.BlockSpec((2, 1, cw), lambda b, n: (0, 0, n))
    mat2 = pl.BlockSpec((2, LRU_LANE_BLOCKS, LANES, LANES), lambda b, n: (0, n, 0, 0))
    return pl.pallas_call(
        functools.partial(_lru_kernel, rows=rows, rows_c=rows_c),
        out_shape=(jax.ShapeDtypeStruct(xr.shape, BF16), jax.ShapeDtypeStruct(xrc.shape, BF16)),
        grid=(batch, width // cw),
        in_specs=[seq_spec, seq_spec, ctx_spec, ctx_spec,
                  pl.BlockSpec((CONV_W, 1, cw), lambda b, n: (0, 0, n)),
                  pl.BlockSpec((1, 1, cw), lambda b, n: (0, 0, n)),
                  mat2, vec2, mat2, vec2, vec2],
        out_specs=(seq_spec, ctx_spec),
        scratch_shapes=[pltpu.VMEM((rows + CONV_W - 1, SUBLANES, cw), F32),
                        pltpu.VMEM((2, rows, SUBLANES, cw), F32),
                        pltpu.VMEM((2, rows, SUBLANES, cw), F32)],
        compiler_params=_cparams(("parallel", "parallel")),
        name="rglru",
    )(xr, xg, xrc, xgc, conv_w.reshape(CONV_W, 1, width), conv_b.reshape(1, 1, width),
      wa.astype(BF16), ba.reshape(2, 1, width), wx.astype(BF16), bx.reshape(2, 1, width),
      lam.reshape(2, 1, width))


def _to_chunked(a, batch):
    t = a.shape[0] // batch
    return a.reshape(batch, LRU_CHUNKS, t // LRU_CHUNKS, a.shape[1]).transpose(0, 2, 1, 3)


def _from_chunked(a):
    b, r, c, w = a.shape
    return a.transpose(0, 2, 1, 3).reshape(b * r * c, w)


def _out_proj_kernel(a1_ref, a2_ref, w1_ref, w2_ref, x_ref, g_ref, o_ref):
    y = _dot(a1_ref[...], w1_ref[...]) + _dot(a2_ref[...], w2_ref[...])
    o_ref[...] = x_ref[...] + g_ref[...] * y


def _out_proj(a1, a1_blk, a2, a2_blk, w, x, mod3, gate_blk, rows_per_mod, mod_base, tm=2048, tn=512):
    n, d = x.shape
    kh = w.shape[0] // 2
    tm = min(tm, n)
    per = rows_per_mod // tm
    gpb = d // tn
    return pl.pallas_call(
        _out_proj_kernel,
        out_shape=jax.ShapeDtypeStruct((n, d), F32),
        grid=(n // tm, d // tn),
        in_specs=[pl.BlockSpec((tm, kh), lambda i, j: (i, a1_blk)),
                  pl.BlockSpec((tm, kh), lambda i, j: (i, a2_blk)),
                  pl.BlockSpec((kh, tn), lambda i, j: (0, j)),
                  pl.BlockSpec((kh, tn), lambda i, j: (1, j)),
                  pl.BlockSpec((tm, tn), lambda i, j: (i, j)),
                  pl.BlockSpec((None, 1, tn), lambda i, j: (mod_base + i // per, 0, gate_blk * gpb + j))],
        out_specs=pl.BlockSpec((tm, tn), lambda i, j: (i, j)),
        compiler_params=_cparams(("parallel", "parallel")),
        name="out_proj",
    )(a1, a2, w, w, x, mod3)


def _lane_block_max(s):
    mm = s[:, 0:LANES]
    for t in range(1, s.shape[1] // LANES):
        mm = jnp.maximum(mm, s[:, t * LANES:(t + 1) * LANES])
    return mm


def _exp_blocks(s, mrep):
    return jnp.concatenate(
        [jnp.exp((s[:, t * LANES:(t + 1) * LANES] - mrep).astype(BF16)) for t in range(s.shape[1] // LANES)],
        axis=1)


def _dense_attn_kernel(q_ref, k_ref, v_ref, kx_ref, vx_ref, o_ref,
                       s_ref, sx_ref, m_ref, acc_ref, va_ref, vax_ref, *, tq, seq, kc):
    n_chunks = seq // kc

    @pl.when(pl.program_id(2) == 0)
    def _():
        for c in range(n_chunks):
            va_ref[c, :, 0:HEAD_DIM] = v_ref[c * kc:(c + 1) * kc, :]
            va_ref[c, :, HEAD_DIM:] = jnp.ones((kc, HEAD_DIM), BF16)
        vax_ref[:, 0:HEAD_DIM] = vx_ref[...]
        vax_ref[:, HEAD_DIM:] = jnp.ones((vx_ref.shape[0], HEAD_DIM), BF16)

    q4 = jnp.concatenate([q_ref[:, _head_cols(g)] for g in range(GQA_GROUP)], axis=0)
    sx = _dot_nt(q4, kx_ref[...])
    sx_ref[...] = sx
    m_ref[...] = _lane_block_max(sx)

    def sweep1(c, carry):
        s = _dot_nt(q4, k_ref[pl.ds(pl.multiple_of(c * kc, kc), kc), :])
        s_ref[c] = s
        m_ref[...] = jnp.maximum(m_ref[...], _lane_block_max(s))
        return carry

    lax.fori_loop(0, n_chunks, sweep1, 0)
    mrep = jnp.broadcast_to(jnp.max(m_ref[...], axis=-1, keepdims=True), m_ref.shape)
    m_ref[...] = mrep
    acc_ref[...] = _dot(_exp_blocks(sx_ref[...], mrep), vax_ref[...])

    def sweep2(c, carry):
        acc_ref[...] += _dot(_exp_blocks(s_ref[c], m_ref[...]), va_ref[c])
        return carry

    lax.fori_loop(0, n_chunks, sweep2, 0)
    o = acc_ref[:, 0:HEAD_DIM] / acc_ref[:, HEAD_DIM:]
    for g in range(GQA_GROUP):
        o_ref[:, _head_cols(g)] = o[g * tq:(g + 1) * tq].astype(o_ref.dtype)


def _dense_attn(q_r, k_r, proj, kx_r, proj_c, batch, seq, ctx_len, tq=256, kc=1024):
    gw = GQA_GROUP * HEAD_DIM
    nq = seq // tq
    rows = GQA_GROUP * tq
    v_blk = C_Q_HEADS + C_KV_HEADS
    vx_blk = C_KV_HEADS
    return pl.pallas_call(
        functools.partial(_dense_attn_kernel, tq=tq, seq=seq, kc=kc),
        out_shape=jax.ShapeDtypeStruct((batch * seq, C_Q_HEADS * HEAD_DIM), BF16),
        grid=(batch, C_KV_HEADS, nq),
        in_specs=[pl.BlockSpec((tq, gw), lambda b, h, i: (b * nq + i, h)),
                  pl.BlockSpec((seq, HEAD_DIM), lambda b, h, i: (b, h)),
                  pl.BlockSpec((seq, HEAD_DIM), lambda b, h, i: (b, v_blk + h)),
                  pl.BlockSpec((ctx_len, HEAD_DIM), lambda b, h, i: (b, h)),
                  pl.BlockSpec((ctx_len, HEAD_DIM), lambda b, h, i: (b, vx_blk + h))],
        out_specs=pl.BlockSpec((tq, gw), lambda b, h, i: (b * nq + i, h)),
        scratch_shapes=[pltpu.VMEM((seq // kc, rows, kc), F32),
                        pltpu.VMEM((rows, ctx_len), F32),
                        pltpu.VMEM((rows, LANES), F32),
                        pltpu.VMEM((rows, 2 * HEAD_DIM), F32),
                        pltpu.VMEM((seq // kc, kc, 2 * HEAD_DIM), BF16),
                        pltpu.VMEM((ctx_len, 2 * HEAD_DIM), BF16)],
        compiler_params=_cparams(("parallel", "parallel", "arbitrary")),
        name="dense_attn",
    )(q_r, k_r, proj, kx_r, proj_c)


def _router_rows(biased, scores):
    v = [biased[e:e + 1, :] for e in range(N_EXPERTS)]
    s = [scores[e:e + 1, :] for e in range(N_EXPERTS)]

    def top2_sum(vals):
        best = vals[0] + vals[1]
        for i in range(len(vals)):
            for j in range(i + 1, len(vals)):
                if (i, j) != (0, 1):
                    best = jnp.maximum(best, vals[i] + vals[j])
        return best

    gsum = [top2_sum(v[g * EXPERTS_PER_GROUP:(g + 1) * EXPERTS_PER_GROUP]) for g in range(N_GROUPS)]
    sel = jnp.zeros_like(gsum[0], dtype=jnp.int32)
    best = gsum[0]
    for g in range(1, N_GROUPS):
        take = gsum[g] > best
        sel = jnp.where(take, g, sel)
        best = jnp.where(take, gsum[g], best)

    def pick_group(rows, i):
        out = rows[i]
        for g in range(1, N_GROUPS):
            out = jnp.where(sel == g, rows[g * EXPERTS_PER_GROUP + i], out)
        return out

    cand = [pick_group(v, i) for i in range(EXPERTS_PER_GROUP)]
    cand_s = [pick_group(s, i) for i in range(EXPERTS_PER_GROUP)]
    i1 = jnp.zeros_like(sel)
    b1 = cand[0]
    for i in range(1, EXPERTS_PER_GROUP):
        take = cand[i] > b1
        i1 = jnp.where(take, i, i1)
        b1 = jnp.where(take, cand[i], b1)
    i2 = jnp.full_like(sel, -1)
    b2 = jnp.full_like(b1, -jnp.inf)
    for i in range(EXPERTS_PER_GROUP):
        take = (i1 != i) & ((cand[i] > b2) | (i2 < 0))
        i2 = jnp.where(take, i, i2)
        b2 = jnp.where(take, cand[i], b2)

    def pick_idx(rows, idx):
        out = rows[0]
        for i in range(1, EXPERTS_PER_GROUP):
            out = jnp.where(idx == i, rows[i], out)
        return out

    s0 = pick_idx(cand_s, i1)
    s1 = pick_idx(cand_s, i2)
    tot = s0 + s1
    e0 = (sel * EXPERTS_PER_GROUP + i1).astype(F32)
    e1 = (sel * EXPERTS_PER_GROUP + i2).astype(F32)
    return e0, e1, s0 / tot, s1 / tot


def _norm_router_kernel(*refs, tm, n_lat_tiles, has_ctx):
    if has_ctx:
        x_ref, xc_ref, g_ref, sh_ref, sc_ref, rw_ref, rb_ref, hp_ref, r_ref, h_ref = refs
        i = pl.program_id(0)

        @pl.when(i < n_lat_tiles)
        def _():
            _norm_mod_rows(x_ref, g_ref, sh_ref, sc_ref, h_ref, tm)

        @pl.when(i >= n_lat_tiles)
        def _():
            _norm_mod_rows(xc_ref, g_ref, sh_ref, sc_ref, h_ref, tm)
    else:
        x_ref, g_ref, sh_ref, sc_ref, rw_ref, rb_ref, hp_ref, r_ref, h_ref = refs
        _norm_mod_rows(x_ref, g_ref, sh_ref, sc_ref, h_ref, tm)

    def pack(r, carry):
        rows = pl.ds(pl.multiple_of(r * 128, 128), 128)
        hp_ref[rows, :] = _pack_halves(h_ref[rows, :])
        return carry

    lax.fori_loop(0, tm // 128, pack, 0)
    logits = _dot_nt(rw_ref[...], h_ref[...])
    scores = jax.nn.sigmoid(logits)
    e0, e1, w0, w1 = _router_rows(scores + rb_ref[...], scores)
    zero = jnp.zeros_like(w0)
    r_ref[...] = jnp.concatenate([e0, e1, w0, w1, zero, zero, zero, zero], axis=0)


def _pack_halves(v):
    c = v.shape[1] // 2
    lo = lax.bitcast_convert_type(v[:, :c].astype(BF16).astype(F32), jnp.uint32)
    hi = lax.bitcast_convert_type(v[:, c:].astype(BF16).astype(F32), jnp.uint32)
    return (lo >> 16) | (hi & jnp.uint32(0xFFFF0000))


def _unpack_halves(p):
    lo = lax.bitcast_convert_type(p << 16, F32)
    hi = lax.bitcast_convert_type(p & jnp.uint32(0xFFFF0000), F32)
    return lo, hi


def _sc_gather_rows(table, idx):
    n, w = idx.shape[0], table.shape[1]
    win = SC_GATHER_WINDOW
    per = n // (win * SC_WORKERS)
    assert per * win * SC_WORKERS == n and per >= 1
    mesh = plsc.VectorSubcoreMesh(core_axis_name="core", subcore_axis_name="subcore")

    @functools.partial(
        pl.kernel, out_type=jax.ShapeDtypeStruct((n, w), table.dtype), mesh=mesh, name="sc_gather_rows",
        scratch_types=[pltpu.VMEM((per * win,), jnp.int32),
                       pltpu.VMEM((win, w), table.dtype), pltpu.VMEM((win, w), table.dtype),
                       pltpu.SemaphoreType.DMA, pltpu.SemaphoreType.DMA,
                       pltpu.SemaphoreType.DMA, pltpu.SemaphoreType.DMA])
    def gather(x_hbm, i_hbm, o_hbm, i_v, buf0, buf1, gsem0, gsem1, wsem0, wsem1):
        wid = lax.axis_index("subcore") * SC_CORES + lax.axis_index("core")
        base = wid * (per * win)
        bufs, gsems, wsems = (buf0, buf1), (gsem0, gsem1), (wsem0, wsem1)
        pltpu.sync_copy(i_hbm.at[pl.ds(base, per * win)], i_v)

        def gather_copy(t, b):
            return pltpu.make_async_copy(x_hbm.at[i_v.at[pl.ds(t * win, win)]], bufs[b], gsems[b])

        def write_copy(t, b):
            return pltpu.make_async_copy(bufs[b], o_hbm.at[pl.ds(base + t * win, win)], wsems[b])

        def step(t, b):
            gather_copy(t, b).wait()

            @pl.when(t > 0)
            def _():
                write_copy(t - 1, 1 - b).wait()

            @pl.when(t + 1 < per)
            def _():
                gather_copy(t + 1, 1 - b).start()

            write_copy(t, b).start()

        gather_copy(0, 0).start()

        @pl.loop(0, per // 2)
        def _(p):
            step(2 * p, 0)
            step(2 * p + 1, 1)

        if per % 2:
            step(per - 1, 0)
        write_copy(per - 1, (per - 1) % 2).wait()

    return gather(table, idx)


def _lat_ctx_maps(n_lat, per, ctx_mod_row):
    def mod_row(i):
        return jnp.where(i < n_lat, i // per, ctx_mod_row)

    def lat(i):
        return (jnp.minimum(i, n_lat - 1), 0)

    def ctx(i):
        return (jnp.maximum(i - n_lat, 0), 0)

    return mod_row, lat, ctx


def _norm_router(x, cx, g, mod3, shift_blk, scale_blk, rw_t, rb, rows_per_mod, ctx_mod_row, tm=512):
    n, d = x.shape
    n_lat = n // tm
    has_ctx = cx is not None
    ntot = n_lat + (cx.shape[0] // tm if has_ctx else 0)
    mod_row, lat, ctx = _lat_ctx_maps(n_lat, rows_per_mod // tm, ctx_mod_row)
    row_specs = [pl.BlockSpec((tm, d), lat)] + ([pl.BlockSpec((tm, d), ctx)] if has_ctx else [])
    row_args = [x] + ([cx] if has_ctx else [])
    return pl.pallas_call(
        functools.partial(_norm_router_kernel, tm=tm, n_lat_tiles=n_lat, has_ctx=has_ctx),
        out_shape=(jax.ShapeDtypeStruct((ntot * tm, d // 2), jnp.uint32),
                   jax.ShapeDtypeStruct((SUBLANES, ntot * tm), F32)),
        grid=(ntot,),
        in_specs=row_specs + [
            pl.BlockSpec((1, d), lambda i: (0, 0)),
            pl.BlockSpec((None, 1, d), lambda i: (mod_row(i), 0, shift_blk)),
            pl.BlockSpec((None, 1, d), lambda i: (mod_row(i), 0, scale_blk)),
            pl.BlockSpec((N_EXPERTS, d), lambda i: (0, 0)),
            pl.BlockSpec((N_EXPERTS, 1), lambda i: (0, 0))],
        out_specs=(pl.BlockSpec((tm, d // 2), lambda i: (i, 0)),
                   pl.BlockSpec((SUBLANES, tm), lambda i: (0, i))),
        scratch_shapes=[pltpu.VMEM((tm, d), BF16)],
        compiler_params=_cparams(("arbitrary",)),
        name="norm_router",
    )(*row_args, g.reshape(1, d), mod3, mod3, rw_t, rb)


def _gmm_kernel(te_ref, tv_ref, xs_ref, wg_ref, wu_ref, wd_ref, *rest, tile0):
    o_ref = rest[-1]
    t = tile0 + pl.program_id(0)

    @pl.when(tv_ref[t] > 0)
    def _():
        lo, hi = _unpack_halves(xs_ref[...])
        x = jnp.concatenate([lo.astype(BF16), hi.astype(BF16)], axis=1)
        gate = _dot(x, wg_ref[0])
        up = _dot(x, wu_ref[0])
        h1 = ((gate * jax.nn.sigmoid(gate)) * up).astype(BF16)
        o_ref[...] = _pack_halves(_dot(h1, wd_ref[0]))

    @pl.when(tv_ref[t] == 0)
    def _():
        o_ref[...] = jnp.zeros(o_ref.shape, o_ref.dtype)


def _grouped_mlp(tile_expert, tile_valid, xs_part, tile0, ys_prev, wg, wu, wd, tm=MOE_TM):
    dpk = xs_part.shape[1]
    d, dff = wg.shape[1], wg.shape[2]
    n_tiles = tile_expert.shape[0]
    in_specs = [pl.BlockSpec((tm, dpk), lambda i, te, tv: (i, 0)),
                pl.BlockSpec((1, d, dff), lambda i, te, tv: (te[tile0 + i], 0, 0)),
                pl.BlockSpec((1, d, dff), lambda i, te, tv: (te[tile0 + i], 0, 0)),
                pl.BlockSpec((1, dff, d), lambda i, te, tv: (te[tile0 + i], 0, 0))]
    args = [tile_expert, tile_valid, xs_part, wg, wu, wd]
    aliases = {}
    if ys_prev is not None:
        in_specs.append(pl.BlockSpec(memory_space=pl.ANY))
        args.append(ys_prev)
        aliases = {len(args) - 1: 0}
    grid_spec = pltpu.PrefetchScalarGridSpec(
        num_scalar_prefetch=2,
        grid=(xs_part.shape[0] // tm,),
        in_specs=in_specs,
        out_specs=pl.BlockSpec((tm, dpk), lambda i, te, tv: (tile0 + i, 0)),
    )
    return pl.pallas_call(
        functools.partial(_gmm_kernel, tile0=tile0),
        out_shape=jax.ShapeDtypeStruct((n_tiles * tm, dpk), jnp.uint32),
        grid_spec=grid_spec,
        input_output_aliases=aliases,
        compiler_params=_cparams(("arbitrary",)),
        name="grouped_mlp",
    )(*args)


def _combine_kernel(*refs, n_lat_tiles, has_ctx):
    if has_ctx:
        x_ref, xc_ref, y0_ref, y1_ref, r_ref, g_ref, o_ref, oc_ref = refs
    else:
        x_ref, y0_ref, y1_ref, r_ref, g_ref, o_ref = refs
    w0 = r_ref[:, 2:3]
    w1 = r_ref[:, 3:4]
    lo0, hi0 = _unpack_halves(y0_ref[...])
    lo1, hi1 = _unpack_halves(y1_ref[...])
    f = g_ref[...] * jnp.concatenate([w0 * lo0 + w1 * lo1, w0 * hi0 + w1 * hi1], axis=1)
    if not has_ctx:
        o_ref[...] = x_ref[...] + f
        return
    i = pl.program_id(0)

    @pl.when(i < n_lat_tiles)
    def _():
        o_ref[...] = x_ref[...] + f

    @pl.when(i >= n_lat_tiles)
    def _():
        oc_ref[...] = xc_ref[...] + f


def _combine(x, cx, yg, route_cols, mod3, gate_blk, rows_per_mod, ctx_mod_row, tm=512):
    n, d = x.shape
    n_lat = n // tm
    has_ctx = cx is not None
    ntot = n_lat + (cx.shape[0] // tm if has_ctx else 0)
    mod_row, lat, ctx = _lat_ctx_maps(n_lat, rows_per_mod // tm, ctx_mod_row)
    row_specs = [pl.BlockSpec((tm, d), lat)] + ([pl.BlockSpec((tm, d), ctx)] if has_ctx else [])
    row_args = [x] + ([cx] if has_ctx else [])
    out_shape = [jax.ShapeDtypeStruct(x.shape, F32)] + ([jax.ShapeDtypeStruct(cx.shape, F32)] if has_ctx else [])
    out = pl.pallas_call(
        functools.partial(_combine_kernel, n_lat_tiles=n_lat, has_ctx=has_ctx),
        out_shape=tuple(out_shape),
        grid=(ntot,),
        in_specs=row_specs + [
            pl.BlockSpec((tm, d // 2), lambda i: (i, 0)),
            pl.BlockSpec((tm, d // 2), lambda i: (ntot + i, 0)),
            pl.BlockSpec((tm, SUBLANES), lambda i: (i, 0)),
            pl.BlockSpec((None, 1, d), lambda i: (mod_row(i), 0, gate_blk))],
        out_specs=tuple(row_specs),
        compiler_params=_cparams(("arbitrary",)),
        name="moe_combine",
    )(*row_args, yg, yg, route_cols, mod3)
    return out if has_ctx else (out[0], None)


def _dispatch_plan(route, tm):
    n = route.shape[1]
    e_flat = jnp.concatenate([route[0], route[1]]).astype(jnp.int32)
    n_assign = 2 * n
    n_tiles = n_assign // tm + N_EXPERTS
    experts = jnp.arange(N_EXPERTS, dtype=jnp.int32)[:, None]
    onehot = (experts == e_flat[None, :]).astype(jnp.int32)
    csum = jnp.cumsum(onehot, axis=1)
    counts = csum[:, -1]
    padded = ((counts + tm - 1) // tm) * tm
    ends_p = jnp.cumsum(padded)
    starts_p = ends_p - padded
    starts_c = jnp.cumsum(counts) - counts
    dest = jnp.sum(onehot * (csum - 1 + starts_p[:, None]), axis=0)
    order = jnp.argsort(e_flat, stable=True).astype(jnp.int32)
    p = jnp.arange(n_tiles * tm, dtype=jnp.int32)[None, :]
    owner = ((p >= starts_p[:, None]) & (p < ends_p[:, None])).astype(jnp.int32)
    within = jnp.sum(owner * (p - starts_p[:, None]), axis=0)
    live = jnp.sum(owner * (p - starts_p[:, None] < counts[:, None]), axis=0) > 0
    compact = jnp.sum(owner * starts_c[:, None], axis=0) + within
    src_tok = jnp.where(live, order[jnp.clip(compact, 0, n_assign - 1)] % n, 0)
    tile_start = jnp.arange(n_tiles, dtype=jnp.int32) * tm
    tile_valid = (tile_start < ends_p[-1]).astype(jnp.int32)
    last_tile = jnp.maximum(ends_p[-1] // tm - 1, 0) * tm
    tile_expert = jnp.searchsorted(ends_p, jnp.minimum(tile_start, last_tile), side="right").astype(jnp.int32)
    tile_expert = jnp.minimum(tile_expert, N_EXPERTS - 1)
    return src_tok, dest, tile_expert, tile_valid


def _moe(x, cx, g, mod3, rw_t, rb, wg, wu, wd, rows_per_mod, ctx_mod_row):
    h, route = _norm_router(x, cx, g, mod3, 3, 4, rw_t, rb, rows_per_mod, ctx_mod_row)
    src_tok, dest, tile_expert, tile_valid = _dispatch_plan(route, MOE_TM)
    n_tiles = tile_expert.shape[0]
    ys = None
    for t0, t1 in ((0, n_tiles // 2), (n_tiles // 2, n_tiles)):
        xs = _sc_gather_rows(h, src_tok[t0 * MOE_TM:t1 * MOE_TM])
        ys = _grouped_mlp(tile_expert, tile_valid, xs, t0, ys, wg, wu, wd)
    yg = _sc_gather_rows(ys, dest)
    route_cols = route.T
    return _combine(x, cx, yg, route_cols, mod3, 5, rows_per_mod, ctx_mod_row)


def _rope_tables(seq):
    rows = seq // GRID_W
    row = jnp.repeat(jnp.arange(rows, dtype=F32), GRID_W)
    col = jnp.tile(jnp.arange(GRID_W, dtype=F32), rows)
    n_freq = HEAD_DIM // 4
    inv_freq = ROPE_BASE ** (-jnp.arange(n_freq, dtype=F32) / n_freq)
    ang = jnp.concatenate([row[:, None] * inv_freq, col[:, None] * inv_freq], axis=-1)
    cos, sin = jnp.cos(ang), jnp.sin(ang)
    return jnp.concatenate([cos, cos], axis=-1), jnp.concatenate([-sin, sin], axis=-1)


def kernel(x, c, ctx, c_ctx, ada_w, ada_b, norm_mix, norm_ffn, ab_w_in, ab_q_gain, ab_k_gain, ab_sink, ab_conv_w, ab_conv_b, ab_gate_a_w, ab_gate_a_b, ab_gate_x_w, ab_gate_x_b, ab_lru_lambda, ab_w_out, gqa_w_in, gqa_q_gain, gqa_k_gain, gqa_w_out, router_w, router_bias, moe_w_gate, moe_w_up, moe_w_down):
    batch, seq, d = x.shape
    ctx_len = ctx.shape[1]
    depth = ada_w.shape[0]
    assert depth == 2 and batch < SUBLANES
    n_lat = batch * seq
    n_ctx = batch * ctx_len
    ctx_row = batch

    xl = x.reshape(n_lat, d)
    xc = ctx.reshape(n_ctx, d)
    cc = jnp.zeros((SUBLANES, d), F32).at[:batch].set(c).at[ctx_row].set(c_ctx)
    mod = _ada(cc, ada_w, ada_b)
    cos2, sin2 = _rope_tables(seq)
    rw_t = router_w.T.astype(BF16)
    rb = router_bias.reshape(N_EXPERTS, 1).astype(F32)
    experts = [tuple(_cast_bf16(w, l) for w in (moe_w_gate, moe_w_up, moe_w_down)) for l in range(depth)]

    mod3 = mod[0].reshape(SUBLANES, 1, 6 * d)
    w_in = ab_w_in[0].astype(BF16)
    proj = _norm_mod_matmul(xl, norm_mix[0], mod3, 0, 1, w_in, seq, 0)
    proj_c = _norm_mod_matmul(xc, norm_mix[0], mod3, 0, 1, w_in, n_ctx, ctx_row)
    q_r = _qknorm(proj, 0, A_Q_HEADS, ab_q_gain[0], cos2, sin2, out_scale=ATTN_SCALE)
    k_r = _qknorm(proj, A_Q_HEADS // A_KV_HEADS, A_KV_HEADS, ab_k_gain[0], cos2, sin2)
    qx_r = _qknorm(proj_c, 0, A_Q_HEADS, ab_q_gain[0], out_scale=ATTN_SCALE)
    kx_r = _qknorm(proj_c, A_Q_HEADS // A_KV_HEADS, A_KV_HEADS, ab_k_gain[0])
    att = _win_attn(ab_sink[0], q_r, k_r, proj, kx_r, proj_c, batch, seq, ctx_len)
    att_c = _ctx_attn(ab_sink[0], qx_r, kx_r, proj_c, batch, ctx_len)

    lru_w = ab_conv_w.shape[2]
    c0 = (A_Q_HEADS + 2 * A_KV_HEADS) * HEAD_DIM
    y_p, yc_p = _lru(_to_chunked(proj[:, c0:c0 + lru_w], batch),
                     _to_chunked(proj[:, c0 + lru_w:c0 + 2 * lru_w], batch),
                     _to_chunked(proj_c[:, c0:c0 + lru_w], batch),
                     _to_chunked(proj_c[:, c0 + lru_w:c0 + 2 * lru_w], batch),
                     ab_conv_w[0], ab_conv_b[0], ab_gate_a_w[0], ab_gate_a_b[0],
                     ab_gate_x_w[0], ab_gate_x_b[0], ab_lru_lambda[0])
    w_out = ab_w_out[0].astype(BF16)
    xl = _out_proj(att, 0, _from_chunked(y_p), 0, w_out, xl, mod3, 2, seq, 0)
    xc = _out_proj(att_c, 0, _from_chunked(yc_p), 0, w_out, xc, mod3, 2, n_ctx, ctx_row)
    xl, xc = _moe(xl, xc, norm_ffn[0], mod3, rw_t, rb, *experts[0], seq, ctx_row)

    mod3 = mod[1].reshape(SUBLANES, 1, 6 * d)
    w_in = gqa_w_in[0].astype(BF16)
    cw = C_Q_HEADS * HEAD_DIM
    proj = _norm_mod_matmul(xl, norm_mix[1], mod3, 0, 1, w_in, seq, 0)
    proj_c = _norm_mod_matmul(xc, norm_mix[1], mod3, 0, 1, w_in[:, cw:], n_ctx, ctx_row)
    q_r = _qknorm(proj, 0, C_Q_HEADS, gqa_q_gain[0], cos2, sin2, out_scale=ATTN_SCALE)
    k_r = _qknorm(proj, C_Q_HEADS // C_KV_HEADS, C_KV_HEADS, gqa_k_gain[0], cos2, sin2)
    kx_r = _qknorm(proj_c, 0, C_KV_HEADS, gqa_k_gain[0])
    att = _dense_attn(q_r, k_r, proj, kx_r, proj_c, batch, seq, ctx_len)
    xl = _out_proj(att, 0, att, 1, gqa_w_out[0].astype(BF16), xl, mod3, 2, seq, 0)
    xl, _ = _moe(xl, None, norm_ffn[1], mod3, rw_t, rb, *experts[1], seq, ctx_row)
    return xl.reshape(batch, seq, d)
```

```python
import functools

import jax
import jax.numpy as jnp
from jax import lax
from jax.experimental import pallas as pl
from jax.experimental.pallas import tpu as pltpu
from jax.experimental.pallas import tpu_sc as plsc

F32 = jnp.float32
BF16 = jnp.bfloat16

LANES = 128
SUBLANES = 8
VMEM_LIMIT = 56 * 1024 * 1024

HEAD_DIM = 128
GRID_W = 64
WINDOW = 128
BLOCK = 128
ROPE_BASE = 10000.0
EPS = 1e-6
ATTN_SCALE = HEAD_DIM ** -0.5
A_Q_HEADS, A_KV_HEADS = 8, 2
C_Q_HEADS, C_KV_HEADS = 16, 4
GQA_GROUP = 4
LRU_C = 8.0
CONV_W = 4
CONV_LEFT = 2
N_EXPERTS = 16
N_GROUPS = 4
EXPERTS_PER_GROUP = 4
NEG_BIG = -1e30

LRU_CHUNKS = SUBLANES
LRU_JB = 8
LRU_LANE_BLOCKS = 2
MOE_TM = 256
MOE_RANGES = 2
SC_CORES = 2
SC_WORKERS = 32
SC_GATHER_WINDOW = 32


def _cparams(sem, vmem=VMEM_LIMIT):
    return pltpu.CompilerParams(dimension_semantics=sem, vmem_limit_bytes=vmem)


def _dot(a, b):
    return jnp.dot(a, b, preferred_element_type=F32)


def _dot_nt(a, b):
    return lax.dot_general(a, b, (((1,), (1,)), ((), ())), preferred_element_type=F32)


def _ada_kernel(c_ref, w_ref, b_ref, o_ref):
    c = c_ref[...]
    s = (c * jax.nn.sigmoid(c)).astype(BF16)
    o_ref[0] = _dot(s, w_ref[0].astype(BF16)) + b_ref[0]


def _ada(cc, ada_w, ada_b):
    depth, d, n = ada_w.shape
    tn = 1024
    return pl.pallas_call(
        _ada_kernel,
        out_shape=jax.ShapeDtypeStruct((depth, SUBLANES, n), F32),
        grid=(depth, n // tn),
        in_specs=[pl.BlockSpec((SUBLANES, d), lambda l, j: (0, 0)),
                  pl.BlockSpec((1, d, tn), lambda l, j: (l, 0, j)),
                  pl.BlockSpec((1, 1, tn), lambda l, j: (l, 0, j))],
        out_specs=pl.BlockSpec((1, SUBLANES, tn), lambda l, j: (l, 0, j)),
        compiler_params=_cparams(("arbitrary", "arbitrary")),
        name="ada",
    )(cc, ada_w, ada_b.reshape(depth, 1, n))


def _norm_mod_rows(x_ref, g_ref, sh_ref, sc_ref, dst_ref, tm, rc=128):
    g = g_ref[...]
    sc1 = 1.0 + sc_ref[...]
    sh = sh_ref[...]

    def body(r, carry):
        rows = pl.ds(pl.multiple_of(r * rc, rc), rc)
        xf = x_ref[rows, :]
        ms = jnp.mean(xf * xf, axis=-1, keepdims=True)
        xn = (xf * lax.rsqrt(ms + EPS)) * g
        dst_ref[rows, :] = (xn * sc1 + sh).astype(dst_ref.dtype)
        return carry

    lax.fori_loop(0, tm // rc, body, 0)


def _nm_mm_kernel(*refs, tm, tn, n_q, n_k, rope):
    if rope:
        x_ref, g_ref, sh_ref, sc_ref, w_ref, qg_ref, kg_ref, cos_ref, sin_ref, o_ref, hn_ref = refs
    else:
        x_ref, g_ref, sh_ref, sc_ref, w_ref, qg_ref, kg_ref, o_ref, hn_ref = refs
    _norm_mod_rows(x_ref, g_ref, sh_ref, sc_ref, hn_ref, tm)
    h = hn_ref[...]
    heads_per_chunk = tn // HEAD_DIM
    for j in range(w_ref.shape[1] // tn):
        cols = slice(j * tn, (j + 1) * tn)
        y = _dot(h, w_ref[:, cols])
        parts = []
        for hh in range(heads_per_chunk):
            head = j * heads_per_chunk + hh
            yh = y[:, _head_cols(hh)]
            if head < n_q + n_k:
                gain = qg_ref[...] if head < n_q else kg_ref[...]
                ms = jnp.mean(yh * yh, axis=-1, keepdims=True)
                yh = (yh * lax.rsqrt(ms + EPS)) * gain
                if rope:
                    yh = yh * cos_ref[...] + pltpu.roll(yh, HEAD_DIM // 2, 1) * sin_ref[...]
                if head < n_q:
                    yh = yh * ATTN_SCALE
            parts.append(yh.astype(o_ref.dtype))
        o_ref[:, cols] = jnp.concatenate(parts, axis=1)


def _norm_mod_matmul(x, g, mod3, shift_blk, scale_blk, w, rows_per_mod, mod_base,
                     n_q, n_k, q_gain, k_gain, cos2=None, sin2=None, tm=512, tn=512):
    n, d = x.shape
    nout = w.shape[1]
    tm = min(tm, n)
    tn = min(tn, nout)
    per = rows_per_mod // tm
    rope = cos2 is not None

    def mod_row(i):
        return mod_base + i // per

    head_vec = pl.BlockSpec((1, HEAD_DIM), lambda i: (0, 0))
    in_specs = [pl.BlockSpec((tm, d), lambda i: (i, 0)),
                pl.BlockSpec((1, d), lambda i: (0, 0)),
                pl.BlockSpec((None, 1, d), lambda i: (mod_row(i), 0, shift_blk)),
                pl.BlockSpec((None, 1, d), lambda i: (mod_row(i), 0, scale_blk)),
                pl.BlockSpec((d, nout), lambda i: (0, 0), pipeline_mode=pl.Buffered(1)),
                head_vec, head_vec]
    args = [x, g.reshape(1, d), mod3, mod3, w, q_gain.reshape(1, HEAD_DIM), k_gain.reshape(1, HEAD_DIM)]
    if rope:
        tiles_per_seq = cos2.shape[0] // tm
        table = pl.BlockSpec((tm, HEAD_DIM), lambda i: (i % tiles_per_seq, 0))
        in_specs += [table, table]
        args += [cos2, sin2]
    return pl.pallas_call(
        functools.partial(_nm_mm_kernel, tm=tm, tn=tn, n_q=n_q, n_k=n_k, rope=rope),
        out_shape=jax.ShapeDtypeStruct((n, nout), BF16),
        grid=(n // tm,),
        in_specs=in_specs,
        out_specs=pl.BlockSpec((tm, nout), lambda i: (i, 0)),
        scratch_shapes=[pltpu.VMEM((tm, d), BF16)],
        compiler_params=_cparams(("parallel",)),
        name="norm_mod_matmul",
    )(*args)


def _head_cols(h):
    return slice(h * HEAD_DIM, (h + 1) * HEAD_DIM)


def _stack_group(q_ref, kvh):
    return jnp.concatenate([q_ref[:, _head_cols(kvh * GQA_GROUP + g)] for g in range(GQA_GROUP)], axis=0)


def _sink_col(sink_ref, kvh, rows):
    return jnp.concatenate([jnp.full((rows, 1), sink_ref[kvh * GQA_GROUP + g], F32)
                            for g in range(GQA_GROUP)], axis=0)


def _win_attn_kernel(sink_ref, q_ref, kp_ref, kc_ref, kn_ref, vp_ref, vc_ref, vn_ref,
                     kx_ref, vx_ref, o_ref, *, seq, ctx_len):
    n = pl.program_id(1)
    rows = GQA_GROUP * BLOCK
    nk = 3 * BLOCK + ctx_len
    qi = lax.broadcasted_iota(jnp.int32, (rows, nk), 0) % BLOCK
    kj = lax.broadcasted_iota(jnp.int32, (rows, nk), 1)
    kpos = n * BLOCK - BLOCK + kj
    in_band = (jnp.abs(kj - BLOCK - qi) <= WINDOW) & (kpos >= 0) & (kpos < seq)
    valid = (kj >= 3 * BLOCK) | in_band
    ones = jnp.ones((nk, HEAD_DIM), BF16)
    for kvh in range(A_KV_HEADS):
        cols = _head_cols(kvh)
        q4 = _stack_group(q_ref, kvh)
        ka = jnp.concatenate([kp_ref[:, cols], kc_ref[:, cols], kn_ref[:, cols], kx_ref[:, cols]], axis=0)
        va = jnp.concatenate([vp_ref[:, cols], vc_ref[:, cols], vn_ref[:, cols], vx_ref[:, cols]], axis=0)
        s = jnp.where(valid, _dot_nt(q4, ka), NEG_BIG)
        sk = _sink_col(sink_ref, kvh, BLOCK)
        m = jnp.maximum(jnp.max(s, axis=-1, keepdims=True), sk)
        p = jnp.exp((s - m).astype(BF16))
        acc = _dot(p, jnp.concatenate([va, ones], axis=1))
        o = acc[:, 0:HEAD_DIM] / (acc[:, HEAD_DIM:] + jnp.exp(sk - m))
        for g in range(GQA_GROUP):
            o_ref[:, _head_cols(kvh * GQA_GROUP + g)] = o[g * BLOCK:(g + 1) * BLOCK].astype(o_ref.dtype)


def _win_attn(sink, proj, proj_c, batch, seq, ctx_len):
    nb = seq // BLOCK
    kvw = A_KV_HEADS * HEAD_DIM
    k_blk = A_Q_HEADS * HEAD_DIM // kvw
    v_blk = k_blk + 1

    def prev(b, n):
        return b * nb + jnp.maximum(n - 1, 0)

    def cur(b, n):
        return b * nb + n

    def nxt(b, n):
        return b * nb + jnp.minimum(n + 1, nb - 1)

    return pl.pallas_call(
        functools.partial(_win_attn_kernel, seq=seq, ctx_len=ctx_len),
        out_shape=jax.ShapeDtypeStruct((batch * seq, A_Q_HEADS * HEAD_DIM), BF16),
        grid=(batch, nb),
        in_specs=[pl.BlockSpec(memory_space=pltpu.SMEM),
                  pl.BlockSpec((BLOCK, A_Q_HEADS * HEAD_DIM), lambda b, n: (cur(b, n), 0)),
                  pl.BlockSpec((BLOCK, kvw), lambda b, n: (prev(b, n), k_blk)),
                  pl.BlockSpec((BLOCK, kvw), lambda b, n: (cur(b, n), k_blk)),
                  pl.BlockSpec((BLOCK, kvw), lambda b, n: (nxt(b, n), k_blk)),
                  pl.BlockSpec((BLOCK, kvw), lambda b, n: (prev(b, n), v_blk)),
                  pl.BlockSpec((BLOCK, kvw), lambda b, n: (cur(b, n), v_blk)),
                  pl.BlockSpec((BLOCK, kvw), lambda b, n: (nxt(b, n), v_blk)),
                  pl.BlockSpec((ctx_len, kvw), lambda b, n: (b, k_blk)),
                  pl.BlockSpec((ctx_len, kvw), lambda b, n: (b, v_blk))],
        out_specs=pl.BlockSpec((BLOCK, A_Q_HEADS * HEAD_DIM), lambda b, n: (cur(b, n), 0)),
        compiler_params=_cparams(("parallel", "parallel")),
        name="win_attn",
    )(sink, proj, proj, proj, proj, proj, proj, proj, proj_c, proj_c)


def _ctx_attn_kernel(sink_ref, q_ref, k_ref, v_ref, o_ref, *, ctx_len):
    kvh = pl.program_id(1)
    q4 = jnp.concatenate([q_ref[:, _head_cols(g)] for g in range(GQA_GROUP)], axis=0)
    s = _dot_nt(q4, k_ref[...])
    sk = jnp.concatenate([jnp.full((ctx_len, 1), sink_ref[kvh * GQA_GROUP + g], F32)
                          for g in range(GQA_GROUP)], axis=0)
    m = jnp.maximum(jnp.max(s, axis=-1, keepdims=True), sk)
    p = jnp.exp(s - m)
    den = jnp.sum(p, axis=-1, keepdims=True) + jnp.exp(sk - m)
    o = _dot(p.astype(BF16), v_ref[...]) / den
    for g in range(GQA_GROUP):
        o_ref[:, _head_cols(g)] = o[g * ctx_len:(g + 1) * ctx_len].astype(o_ref.dtype)


def _ctx_attn(sink, proj_c, batch, ctx_len):
    k_blk = A_Q_HEADS
    v_blk = A_Q_HEADS + A_KV_HEADS
    gw = GQA_GROUP * HEAD_DIM
    return pl.pallas_call(
        functools.partial(_ctx_attn_kernel, ctx_len=ctx_len),
        out_shape=jax.ShapeDtypeStruct((batch * ctx_len, A_Q_HEADS * HEAD_DIM), BF16),
        grid=(batch, A_KV_HEADS),
        in_specs=[pl.BlockSpec(memory_space=pltpu.SMEM),
                  pl.BlockSpec((ctx_len, gw), lambda b, h: (b, h)),
                  pl.BlockSpec((ctx_len, HEAD_DIM), lambda b, h: (b, k_blk + h)),
                  pl.BlockSpec((ctx_len, HEAD_DIM), lambda b, h: (b, v_blk + h))],
        out_specs=pl.BlockSpec((ctx_len, gw), lambda b, h: (b, h)),
        compiler_params=_cparams(("parallel", "parallel")),
        name="ctx_attn",
    )(sink, proj_c, proj_c, proj_c)


def _sigmoid(x):
    return 0.5 * jnp.tanh(0.5 * x) + 0.5


def _gelu_tanh(x):
    return 0.5 * x * (1.0 + jnp.tanh(0.7978845608028654 * (x + 0.044715 * (x * x * x))))


def _lru_sequence(x_ref, xg_ref, y_ref, xp_ref, a_ref, b_ref, w, init, rows, jb):
    conv_w, conv_b, wa, ba, wx, bx, c_logsig = w
    width = x_ref.shape[-1]
    sub = lax.broadcasted_iota(jnp.int32, (1, SUBLANES, width), 1)

    def block_diag(ub, wd):
        return jnp.concatenate([_dot(ub[:, _head_cols(n)], wd[n]) for n in range(width // LANES)], axis=1)

    def fill(r, carry):
        rr = pl.ds(pl.multiple_of(r * jb, jb), jb)
        xp_ref[pl.ds(pl.multiple_of(r * jb, jb) + CONV_LEFT, jb)] = x_ref[rr].astype(F32)
        return carry

    lax.fori_loop(0, rows // jb, fill, 0)
    tail = x_ref[rows - CONV_LEFT:rows].astype(F32)
    xp_ref[0:CONV_LEFT] = jnp.where(sub == 0, 0.0, pltpu.roll(tail, 1, 1))
    head = x_ref[0:1].astype(F32)
    xp_ref[rows + CONV_LEFT:rows + CONV_LEFT + 1] = jnp.where(
        sub == SUBLANES - 1, 0.0, pltpu.roll(head, SUBLANES - 1, 1))

    def gates(r, carry):
        j0 = pl.multiple_of(r * jb, jb)
        u = conv_b
        for k in range(CONV_W):
            u = u + conv_w[k] * xp_ref[pl.ds(j0 + k, jb)]
        u2 = u.reshape(jb * SUBLANES, width)
        ub = u2.astype(BF16)
        for d in range(2):
            r_gate = _sigmoid(block_diag(ub, wa[d]) + ba[d])
            i_gate = _sigmoid(block_diag(ub, wx[d]) + bx[d])
            log_a = c_logsig[d] * r_gate
            a = jnp.exp(log_a)
            b = jnp.sqrt(1.0 - a * a) * (i_gate * u2)
            a_ref[d, pl.ds(j0, jb)] = a.reshape(jb, SUBLANES, width)
            b_ref[d, pl.ds(j0, jb)] = b.reshape(jb, SUBLANES, width)
        return carry

    lax.fori_loop(0, rows // jb, gates, 0)

    def scan(j, carry):
        hf, pf, hb, pb = carry
        jr = rows - 1 - j
        af = a_ref[0, j]
        hf = af * hf + b_ref[0, j]
        pf = pf * af
        b_ref[0, j] = hf
        a_ref[0, j] = pf
        ab = a_ref[1, jr]
        hb = ab * hb + b_ref[1, jr]
        pb = pb * ab
        b_ref[1, jr] = hb
        a_ref[1, jr] = pb
        return hf, pf, hb, pb

    z = jnp.zeros((SUBLANES, width), F32)
    o = jnp.ones((SUBLANES, width), F32)
    lax.fori_loop(0, rows, scan, (z, o, z, o), unroll=8)

    hf_last, pf_last = b_ref[0, rows - 1], a_ref[0, rows - 1]
    hb_last, pb_last = b_ref[1, 0], a_ref[1, 0]
    s = init[0]
    carry_f = []
    for c in range(SUBLANES):
        carry_f.append(s)
        s = hf_last[c:c + 1] + pf_last[c:c + 1] * s
    out_f = s
    s = init[1]
    carry_b = [None] * SUBLANES
    for c in reversed(range(SUBLANES)):
        carry_b[c] = s
        s = hb_last[c:c + 1] + pb_last[c:c + 1] * s
    out_b = s
    cf = jnp.concatenate(carry_f, axis=0)
    cb = jnp.concatenate(carry_b, axis=0)

    def emit(r, carry):
        rr = pl.ds(pl.multiple_of(r * jb, jb), jb)
        h = (b_ref[0, rr] + a_ref[0, rr] * cf) + (b_ref[1, rr] + a_ref[1, rr] * cb)
        y_ref[rr] = (h * _gelu_tanh(xg_ref[rr].astype(F32))).astype(y_ref.dtype)
        return carry

    lax.fori_loop(0, rows // jb, emit, 0)
    return out_f, out_b


def _lru_kernel(xr_ref, xg_ref, xrc_ref, xgc_ref, cw_ref, cb_ref, wa_ref, ba_ref, wx_ref, bx_ref,
                lam_ref, y_ref, yc_ref, xp_ref, a_ref, b_ref, *, rows, rows_c):
    c_logsig = [LRU_C * jax.nn.log_sigmoid(lam_ref[d]) for d in range(2)]
    w = ([cw_ref[k] for k in range(CONV_W)], cb_ref[0],
         [wa_ref[d] for d in range(2)], [ba_ref[d] for d in range(2)],
         [wx_ref[d] for d in range(2)], [bx_ref[d] for d in range(2)], c_logsig)
    zero = jnp.zeros((1, xr_ref.shape[-1]), F32)
    sf, sb = _lru_sequence(xrc_ref, xgc_ref, yc_ref, xp_ref, a_ref, b_ref, w, (zero, zero),
                           rows_c, min(LRU_JB, rows_c))
    _lru_sequence(xr_ref, xg_ref, y_ref, xp_ref, a_ref, b_ref, w, (sf, sb), rows, LRU_JB)


def _lru(xr, xg, xrc, xgc, conv_w, conv_b, wa, ba, wx, bx, lam):
    batch, rows, _, width = xr.shape
    rows_c = xrc.shape[1]
    cw = LRU_LANE_BLOCKS * LANES
    seq_spec = pl.BlockSpec((None, rows, SUBLANES, cw), lambda b, n: (b, 0, 0, n))
    ctx_spec = pl.BlockSpec((None, rows_c, SUBLANES, cw), lambda b, n: (b, 0, 0, n))
    vec2 = pl.BlockSpec((2, 1, cw), lambda b, n: (0, 0, n))
    mat2 = pl.BlockSpec((2, LRU_LANE_BLOCKS, LANES, LANES), lambda b, n: (0, n, 0, 0))
    return pl.pallas_call(
        functools.partial(_lru_kernel, rows=rows, rows_c=rows_c),
        out_shape=(jax.ShapeDtypeStruct(xr.shape, BF16), jax.ShapeDtypeStruct(xrc.shape, BF16)),
        grid=(batch, width // cw),
        in_specs=[seq_spec, seq_spec, ctx_spec, ctx_spec,
                  pl.BlockSpec((CONV_W, 1, cw), lambda b, n: (0, 0, n)),
                  pl.BlockSpec((1, 1, cw), lambda b, n: (0, 0, n)),
                  mat2, vec2, mat2, vec2, vec2],
        out_specs=(seq_spec, ctx_spec),
        scratch_shapes=[pltpu.VMEM((rows + CONV_W - 1, SUBLANES, cw), F32),
                        pltpu.VMEM((2, rows, SUBLANES, cw), F32),
                        pltpu.VMEM((2, rows, SUBLANES, cw), F32)],
        compiler_params=_cparams(("parallel", "parallel")),
        name="rglru",
    )(xr, xg, xrc, xgc, conv_w.reshape(CONV_W, 1, width), conv_b.reshape(1, 1, width),
      wa.astype(BF16), ba.reshape(2, 1, width), wx.astype(BF16), bx.reshape(2, 1, width),
      lam.reshape(2, 1, width))


def _to_chunked(a, batch):
    t = a.shape[0] // batch
    return a.reshape(batch, LRU_CHUNKS, t // LRU_CHUNKS, a.shape[1]).transpose(0, 2, 1, 3)


def _from_chunked(a):
    b, r, c, w = a.shape
    return a.transpose(0, 2, 1, 3).reshape(b * r * c, w)


def _out_proj_kernel(a1_ref, a2_ref, w1_ref, w2_ref, x_ref, g_ref, o_ref):
    y = _dot(a1_ref[...], w1_ref[...]) + _dot(a2_ref[...], w2_ref[...])
    o_ref[...] = x_ref[...] + g_ref[...] * y


def _out_proj(a1, a1_blk, a2, a2_blk, w, x, mod3, gate_blk, rows_per_mod, mod_base, tm=2048, tn=512):
    n, d = x.shape
    kh = w.shape[0] // 2
    tm = min(tm, n)
    per = rows_per_mod // tm
    gpb = d // tn
    return pl.pallas_call(
        _out_proj_kernel,
        out_shape=jax.ShapeDtypeStruct((n, d), F32),
        grid=(n // tm, d // tn),
        in_specs=[pl.BlockSpec((tm, kh), lambda i, j: (i, a1_blk)),
                  pl.BlockSpec((tm, kh), lambda i, j: (i, a2_blk)),
                  pl.BlockSpec((kh, tn), lambda i, j: (0, j)),
                  pl.BlockSpec((kh, tn), lambda i, j: (1, j)),
                  pl.BlockSpec((tm, tn), lambda i, j: (i, j)),
                  pl.BlockSpec((None, 1, tn), lambda i, j: (mod_base + i // per, 0, gate_blk * gpb + j))],
        out_specs=pl.BlockSpec((tm, tn), lambda i, j: (i, j)),
        compiler_params=_cparams(("parallel", "parallel")),
        name="out_proj",
    )(a1, a2, w, w, x, mod3)


def _lane_block_max(s):
    mm = s[:, 0:LANES]
    for t in range(1, s.shape[1] // LANES):
        mm = jnp.maximum(mm, s[:, t * LANES:(t + 1) * LANES])
    return mm


def _exp_blocks(s, mrep):
    return jnp.concatenate(
        [jnp.exp((s[:, t * LANES:(t + 1) * LANES] - mrep).astype(BF16)) for t in range(s.shape[1] // LANES)],
        axis=1)


def _dense_attn_kernel(q_ref, k_ref, v_ref, kx_ref, vx_ref, o_ref,
                       s_ref, sx_ref, m_ref, acc_ref, va_ref, vax_ref, *, tq, seq, kc):
    n_chunks = seq // kc

    @pl.when(pl.program_id(2) == 0)
    def _():
        for c in range(n_chunks):
            va_ref[c, :, 0:HEAD_DIM] = v_ref[c * kc:(c + 1) * kc, :]
            va_ref[c, :, HEAD_DIM:] = jnp.ones((kc, HEAD_DIM), BF16)
        vax_ref[:, 0:HEAD_DIM] = vx_ref[...]
        vax_ref[:, HEAD_DIM:] = jnp.ones((vx_ref.shape[0], HEAD_DIM), BF16)

    q4 = jnp.concatenate([q_ref[:, _head_cols(g)] for g in range(GQA_GROUP)], axis=0)
    sx = _dot_nt(q4, kx_ref[...])
    sx_ref[...] = sx
    m_ref[...] = _lane_block_max(sx)

    def sweep1(c, carry):
        s = _dot_nt(q4, k_ref[pl.ds(pl.multiple_of(c * kc, kc), kc), :])
        s_ref[c] = s
        m_ref[...] = jnp.maximum(m_ref[...], _lane_block_max(s))
        return carry

    lax.fori_loop(0, n_chunks, sweep1, 0)
    mrep = jnp.broadcast_to(jnp.max(m_ref[...], axis=-1, keepdims=True), m_ref.shape)
    m_ref[...] = mrep
    acc_ref[...] = _dot(_exp_blocks(sx_ref[...], mrep), vax_ref[...])

    def sweep2(c, carry):
        acc_ref[...] += _dot(_exp_blocks(s_ref[c], m_ref[...]), va_ref[c])
        return carry

    lax.fori_loop(0, n_chunks, sweep2, 0)
    o = acc_ref[:, 0:HEAD_DIM] / acc_ref[:, HEAD_DIM:]
    for g in range(GQA_GROUP):
        o_ref[:, _head_cols(g)] = o[g * tq:(g + 1) * tq].astype(o_ref.dtype)


def _dense_attn(proj, proj_c, batch, seq, ctx_len, tq=256, kc=1024):
    gw = GQA_GROUP * HEAD_DIM
    nq = seq // tq
    rows = GQA_GROUP * tq
    k_blk = C_Q_HEADS
    v_blk = C_Q_HEADS + C_KV_HEADS
    vx_blk = C_KV_HEADS
    return pl.pallas_call(
        functools.partial(_dense_attn_kernel, tq=tq, seq=seq, kc=kc),
        out_shape=jax.ShapeDtypeStruct((batch * seq, C_Q_HEADS * HEAD_DIM), BF16),
        grid=(batch, C_KV_HEADS, nq),
        in_specs=[pl.BlockSpec((tq, gw), lambda b, h, i: (b * nq + i, h)),
                  pl.BlockSpec((seq, HEAD_DIM), lambda b, h, i: (b, k_blk + h)),
                  pl.BlockSpec((seq, HEAD_DIM), lambda b, h, i: (b, v_blk + h)),
                  pl.BlockSpec((ctx_len, HEAD_DIM), lambda b, h, i: (b, h)),
                  pl.BlockSpec((ctx_len, HEAD_DIM), lambda b, h, i: (b, vx_blk + h))],
        out_specs=pl.BlockSpec((tq, gw), lambda b, h, i: (b * nq + i, h)),
        scratch_shapes=[pltpu.VMEM((seq // kc, rows, kc), F32),
                        pltpu.VMEM((rows, ctx_len), F32),
                        pltpu.VMEM((rows, LANES), F32),
                        pltpu.VMEM((rows, 2 * HEAD_DIM), F32),
                        pltpu.VMEM((seq // kc, kc, 2 * HEAD_DIM), BF16),
                        pltpu.VMEM((ctx_len, 2 * HEAD_DIM), BF16)],
        compiler_params=_cparams(("parallel", "parallel", "arbitrary")),
        name="dense_attn",
    )(proj, proj, proj, proj_c, proj_c)


def _router_rows(biased, scores):
    v = [biased[e:e + 1, :] for e in range(N_EXPERTS)]
    s = [scores[e:e + 1, :] for e in range(N_EXPERTS)]

    def top2_sum(vals):
        best = vals[0] + vals[1]
        for i in range(len(vals)):
            for j in range(i + 1, len(vals)):
                if (i, j) != (0, 1):
                    best = jnp.maximum(best, vals[i] + vals[j])
        return best

    gsum = [top2_sum(v[g * EXPERTS_PER_GROUP:(g + 1) * EXPERTS_PER_GROUP]) for g in range(N_GROUPS)]
    sel = jnp.zeros_like(gsum[0], dtype=jnp.int32)
    best = gsum[0]
    for g in range(1, N_GROUPS):
        take = gsum[g] > best
        sel = jnp.where(take, g, sel)
        best = jnp.where(take, gsum[g], best)

    def pick_group(rows, i):
        out = rows[i]
        for g in range(1, N_GROUPS):
            out = jnp.where(sel == g, rows[g * EXPERTS_PER_GROUP + i], out)
        return out

    cand = [pick_group(v, i) for i in range(EXPERTS_PER_GROUP)]
    cand_s = [pick_group(s, i) for i in range(EXPERTS_PER_GROUP)]
    i1 = jnp.zeros_like(sel)
    b1 = cand[0]
    for i in range(1, EXPERTS_PER_GROUP):
        take = cand[i] > b1
        i1 = jnp.where(take, i, i1)
        b1 = jnp.where(take, cand[i], b1)
    i2 = jnp.full_like(sel, -1)
    b2 = jnp.full_like(b1, -jnp.inf)
    for i in range(EXPERTS_PER_GROUP):
        take = (i1 != i) & ((cand[i] > b2) | (i2 < 0))
        i2 = jnp.where(take, i, i2)
        b2 = jnp.where(take, cand[i], b2)

    def pick_idx(rows, idx):
        out = rows[0]
        for i in range(1, EXPERTS_PER_GROUP):
            out = jnp.where(idx == i, rows[i], out)
        return out

    s0 = pick_idx(cand_s, i1)
    s1 = pick_idx(cand_s, i2)
    tot = s0 + s1
    e0 = (sel * EXPERTS_PER_GROUP + i1).astype(F32)
    e1 = (sel * EXPERTS_PER_GROUP + i2).astype(F32)
    return e0, e1, s0 / tot, s1 / tot


def _norm_router_kernel(*refs, tm, n_lat_tiles, has_ctx):
    if has_ctx:
        x_ref, xc_ref, g_ref, sh_ref, sc_ref, rw_ref, rb_ref, hp_ref, r_ref, h_ref = refs
        i = pl.program_id(0)

        @pl.when(i < n_lat_tiles)
        def _():
            _norm_mod_rows(x_ref, g_ref, sh_ref, sc_ref, h_ref, tm)

        @pl.when(i >= n_lat_tiles)
        def _():
            _norm_mod_rows(xc_ref, g_ref, sh_ref, sc_ref, h_ref, tm)
    else:
        x_ref, g_ref, sh_ref, sc_ref, rw_ref, rb_ref, hp_ref, r_ref, h_ref = refs
        _norm_mod_rows(x_ref, g_ref, sh_ref, sc_ref, h_ref, tm)

    def pack(r, carry):
        rows = pl.ds(pl.multiple_of(r * 128, 128), 128)
        hp_ref[rows, :] = _pack_halves(h_ref[rows, :])
        return carry

    lax.fori_loop(0, tm // 128, pack, 0)
    logits = _dot_nt(rw_ref[...], h_ref[...])
    scores = jax.nn.sigmoid(logits)
    e0, e1, w0, w1 = _router_rows(scores + rb_ref[...], scores)
    zero = jnp.zeros_like(w0)
    r_ref[...] = jnp.concatenate([e0, e1, w0, w1, zero, zero, zero, zero], axis=0)


def _pack_halves(v):
    c = v.shape[1] // 2
    lo = lax.bitcast_convert_type(v[:, :c].astype(BF16).astype(F32), jnp.uint32)
    hi = lax.bitcast_convert_type(v[:, c:].astype(BF16).astype(F32), jnp.uint32)
    return (lo >> 16) | (hi & jnp.uint32(0xFFFF0000))


def _unpack_halves(p):
    lo = lax.bitcast_convert_type(p << 16, F32)
    hi = lax.bitcast_convert_type(p & jnp.uint32(0xFFFF0000), F32)
    return lo, hi


def _sc_gather_rows(table, idx):
    n, w = idx.shape[0], table.shape[1]
    win = SC_GATHER_WINDOW
    per = n // (win * SC_WORKERS)
    assert per * win * SC_WORKERS == n and per >= 1
    mesh = plsc.VectorSubcoreMesh(core_axis_name="core", subcore_axis_name="subcore")

    @functools.partial(
        pl.kernel, out_type=jax.ShapeDtypeStruct((n, w), table.dtype), mesh=mesh, name="sc_gather_rows",
        scratch_types=[pltpu.VMEM((per * win,), jnp.int32),
                       pltpu.VMEM((win, w), table.dtype), pltpu.VMEM((win, w), table.dtype),
                       pltpu.SemaphoreType.DMA, pltpu.SemaphoreType.DMA,
                       pltpu.SemaphoreType.DMA, pltpu.SemaphoreType.DMA])
    def gather(x_hbm, i_hbm, o_hbm, i_v, buf0, buf1, gsem0, gsem1, wsem0, wsem1):
        wid = lax.axis_index("subcore") * SC_CORES + lax.axis_index("core")
        base = wid * (per * win)
        bufs, gsems, wsems = (buf0, buf1), (gsem0, gsem1), (wsem0, wsem1)
        pltpu.sync_copy(i_hbm.at[pl.ds(base, per * win)], i_v)

        def gather_copy(t, b):
            return pltpu.make_async_copy(x_hbm.at[i_v.at[pl.ds(t * win, win)]], bufs[b], gsems[b])

        def write_copy(t, b):
            return pltpu.make_async_copy(bufs[b], o_hbm.at[pl.ds(base + t * win, win)], wsems[b])

        def step(t, b):
            gather_copy(t, b).wait()

            @pl.when(t > 0)
            def _():
                write_copy(t - 1, 1 - b).wait()

            @pl.when(t + 1 < per)
            def _():
                gather_copy(t + 1, 1 - b).start()

            write_copy(t, b).start()

        gather_copy(0, 0).start()

        @pl.loop(0, per // 2)
        def _(p):
            step(2 * p, 0)
            step(2 * p + 1, 1)

        if per % 2:
            step(per - 1, 0)
        write_copy(per - 1, (per - 1) % 2).wait()

    return gather(table, idx)


def _lat_ctx_maps(n_lat, per, ctx_mod_row):
    def mod_row(i):
        return jnp.where(i < n_lat, i // per, ctx_mod_row)

    def lat(i):
        return (jnp.minimum(i, n_lat - 1), 0)

    def ctx(i):
        return (jnp.maximum(i - n_lat, 0), 0)

    return mod_row, lat, ctx


def _norm_router(x, cx, g, mod3, shift_blk, scale_blk, rw_t, rb, rows_per_mod, ctx_mod_row, tm=512):
    n, d = x.shape
    n_lat = n // tm
    has_ctx = cx is not None
    ntot = n_lat + (cx.shape[0] // tm if has_ctx else 0)
    mod_row, lat, ctx = _lat_ctx_maps(n_lat, rows_per_mod // tm, ctx_mod_row)
    row_specs = [pl.BlockSpec((tm, d), lat)] + ([pl.BlockSpec((tm, d), ctx)] if has_ctx else [])
    row_args = [x] + ([cx] if has_ctx else [])
    return pl.pallas_call(
        functools.partial(_norm_router_kernel, tm=tm, n_lat_tiles=n_lat, has_ctx=has_ctx),
        out_shape=(jax.ShapeDtypeStruct((ntot * tm, d // 2), jnp.uint32),
                   jax.ShapeDtypeStruct((SUBLANES, ntot * tm), F32)),
        grid=(ntot,),
        in_specs=row_specs + [
            pl.BlockSpec((1, d), lambda i: (0, 0)),
            pl.BlockSpec((None, 1, d), lambda i: (mod_row(i), 0, shift_blk)),
            pl.BlockSpec((None, 1, d), lambda i: (mod_row(i), 0, scale_blk)),
            pl.BlockSpec((N_EXPERTS, d), lambda i: (0, 0)),
            pl.BlockSpec((N_EXPERTS, 1), lambda i: (0, 0))],
        out_specs=(pl.BlockSpec((tm, d // 2), lambda i: (i, 0)),
                   pl.BlockSpec((SUBLANES, tm), lambda i: (0, i))),
        scratch_shapes=[pltpu.VMEM((tm, d), BF16)],
        compiler_params=_cparams(("arbitrary",)),
        name="norm_router",
    )(*row_args, g.reshape(1, d), mod3, mod3, rw_t, rb)


def _cast_rows(src_ref, dst_ref, rb=256):
    def body(r, carry):
        rows = pl.ds(pl.multiple_of(r * rb, rb), rb)
        dst_ref[rows, :] = src_ref[0, rows, :].astype(dst_ref.dtype)
        return carry

    lax.fori_loop(0, dst_ref.shape[0] // rb, body, 0)


def _gmm_kernel(te_ref, tv_ref, xs_ref, wg_ref, wu_ref, wd_ref, *rest, tile0):
    o_ref, wg_b, wu_b, wd_b = rest[-4:]
    i = pl.program_id(0)
    t = tile0 + i
    live = tv_ref[t] > 0
    new_expert = (i == 0) | (te_ref[t] != te_ref[jnp.maximum(t - 1, 0)])

    @pl.when(live & new_expert)
    def _():
        _cast_rows(wg_ref, wg_b)
        _cast_rows(wu_ref, wu_b)
        _cast_rows(wd_ref, wd_b)

    @pl.when(live)
    def _():
        lo, hi = _unpack_halves(xs_ref[...])
        x = jnp.concatenate([lo.astype(BF16), hi.astype(BF16)], axis=1)
        gate = _dot(x, wg_b[...])
        up = _dot(x, wu_b[...])
        h1 = ((gate * jax.nn.sigmoid(gate)) * up).astype(BF16)
        o_ref[...] = _pack_halves(_dot(h1, wd_b[...]))

    @pl.when(tv_ref[t] == 0)
    def _():
        o_ref[...] = jnp.zeros(o_ref.shape, o_ref.dtype)


def _grouped_mlp(tile_expert, tile_valid, xs_part, tile0, ys_prev, wg, wu, wd, layer, tm=MOE_TM):
    dpk = xs_part.shape[1]
    d, dff = wg.shape[2], wg.shape[3]
    n_tiles = tile_expert.shape[0]

    def expert_block(rows, cols):
        return pl.BlockSpec((None, 1, rows, cols), lambda i, te, tv: (layer, te[tile0 + i], 0, 0),
                            pipeline_mode=pl.Buffered(1))

    in_specs = [pl.BlockSpec((tm, dpk), lambda i, te, tv: (i, 0)),
                expert_block(d, dff), expert_block(d, dff), expert_block(dff, d)]
    args = [tile_expert, tile_valid, xs_part, wg, wu, wd]
    aliases = {}
    if ys_prev is not None:
        in_specs.append(pl.BlockSpec(memory_space=pl.ANY))
        args.append(ys_prev)
        aliases = {len(args) - 1: 0}
    grid_spec = pltpu.PrefetchScalarGridSpec(
        num_scalar_prefetch=2,
        grid=(xs_part.shape[0] // tm,),
        in_specs=in_specs,
        out_specs=pl.BlockSpec((tm, dpk), lambda i, te, tv: (tile0 + i, 0)),
        scratch_shapes=[pltpu.VMEM((d, dff), BF16), pltpu.VMEM((d, dff), BF16), pltpu.VMEM((dff, d), BF16)],
    )
    return pl.pallas_call(
        functools.partial(_gmm_kernel, tile0=tile0),
        out_shape=jax.ShapeDtypeStruct((n_tiles * tm, dpk), jnp.uint32),
        grid_spec=grid_spec,
        input_output_aliases=aliases,
        compiler_params=_cparams(("arbitrary",)),
        name="grouped_mlp",
    )(*args)


def _combine_kernel(*refs, n_lat_tiles, has_ctx):
    if has_ctx:
        x_ref, xc_ref, y0_ref, y1_ref, r_ref, g_ref, o_ref, oc_ref = refs
    else:
        x_ref, y0_ref, y1_ref, r_ref, g_ref, o_ref = refs
    w0 = r_ref[:, 2:3]
    w1 = r_ref[:, 3:4]
    lo0, hi0 = _unpack_halves(y0_ref[...])
    lo1, hi1 = _unpack_halves(y1_ref[...])
    f = g_ref[...] * jnp.concatenate([w0 * lo0 + w1 * lo1, w0 * hi0 + w1 * hi1], axis=1)
    if not has_ctx:
        o_ref[...] = x_ref[...] + f
        return
    i = pl.program_id(0)

    @pl.when(i < n_lat_tiles)
    def _():
        o_ref[...] = x_ref[...] + f

    @pl.when(i >= n_lat_tiles)
    def _():
        oc_ref[...] = xc_ref[...] + f


def _combine(x, cx, yg, route_cols, mod3, gate_blk, rows_per_mod, ctx_mod_row, tm=512):
    n, d = x.shape
    n_lat = n // tm
    has_ctx = cx is not None
    ntot = n_lat + (cx.shape[0] // tm if has_ctx else 0)
    mod_row, lat, ctx = _lat_ctx_maps(n_lat, rows_per_mod // tm, ctx_mod_row)
    row_specs = [pl.BlockSpec((tm, d), lat)] + ([pl.BlockSpec((tm, d), ctx)] if has_ctx else [])
    row_args = [x] + ([cx] if has_ctx else [])
    out_shape = [jax.ShapeDtypeStruct(x.shape, F32)] + ([jax.ShapeDtypeStruct(cx.shape, F32)] if has_ctx else [])
    out = pl.pallas_call(
        functools.partial(_combine_kernel, n_lat_tiles=n_lat, has_ctx=has_ctx),
        out_shape=tuple(out_shape),
        grid=(ntot,),
        in_specs=row_specs + [
            pl.BlockSpec((tm, d // 2), lambda i: (i, 0)),
            pl.BlockSpec((tm, d // 2), lambda i: (ntot + i, 0)),
            pl.BlockSpec((tm, SUBLANES), lambda i: (i, 0)),
            pl.BlockSpec((None, 1, d), lambda i: (mod_row(i), 0, gate_blk))],
        out_specs=tuple(row_specs),
        compiler_params=_cparams(("arbitrary",)),
        name="moe_combine",
    )(*row_args, yg, yg, route_cols, mod3)
    return out if has_ctx else (out[0], None)


def _dispatch_plan(route, tm):
    n = route.shape[1]
    e_flat = jnp.concatenate([route[0], route[1]]).astype(jnp.int32)
    n_assign = 2 * n
    n_tiles = n_assign // tm + N_EXPERTS
    experts = jnp.arange(N_EXPERTS, dtype=jnp.int32)[:, None]
    onehot = (experts == e_flat[None, :]).astype(jnp.int32)
    csum = jnp.cumsum(onehot, axis=1)
    counts = csum[:, -1]
    padded = ((counts + tm - 1) // tm) * tm
    ends_p = jnp.cumsum(padded)
    starts_p = ends_p - padded
    starts_c = jnp.cumsum(counts) - counts
    dest = jnp.sum(onehot * (csum - 1 + starts_p[:, None]), axis=0)
    order = jnp.argsort(e_flat, stable=True).astype(jnp.int32)
    p = jnp.arange(n_tiles * tm, dtype=jnp.int32)[None, :]
    owner = ((p >= starts_p[:, None]) & (p < ends_p[:, None])).astype(jnp.int32)
    within = jnp.sum(owner * (p - starts_p[:, None]), axis=0)
    live = jnp.sum(owner * (p - starts_p[:, None] < counts[:, None]), axis=0) > 0
    compact = jnp.sum(owner * starts_c[:, None], axis=0) + within
    src_tok = jnp.where(live, order[jnp.clip(compact, 0, n_assign - 1)] % n, 0)
    tile_start = jnp.arange(n_tiles, dtype=jnp.int32) * tm
    tile_valid = (tile_start < ends_p[-1]).astype(jnp.int32)
    last_tile = jnp.maximum(ends_p[-1] // tm - 1, 0) * tm
    tile_expert = jnp.searchsorted(ends_p, jnp.minimum(tile_start, last_tile), side="right").astype(jnp.int32)
    tile_expert = jnp.minimum(tile_expert, N_EXPERTS - 1)
    return src_tok, dest, tile_expert, tile_valid


def _moe(x, cx, g, mod3, rw_t, rb, wg, wu, wd, layer, rows_per_mod, ctx_mod_row):
    h, route = _norm_router(x, cx, g, mod3, 3, 4, rw_t, rb, rows_per_mod, ctx_mod_row)
    src_tok, dest, tile_expert, tile_valid = _dispatch_plan(route, MOE_TM)
    n_tiles = tile_expert.shape[0]
    bounds = [n_tiles * k // MOE_RANGES for k in range(MOE_RANGES + 1)]
    ys = None
    for t0, t1 in zip(bounds[:-1], bounds[1:]):
        xs = _sc_gather_rows(h, src_tok[t0 * MOE_TM:t1 * MOE_TM])
        ys = _grouped_mlp(tile_expert, tile_valid, xs, t0, ys, wg, wu, wd, layer)
    yg = _sc_gather_rows(ys, dest)
    route_cols = route.T
    return _combine(x, cx, yg, route_cols, mod3, 5, rows_per_mod, ctx_mod_row)


def _rope_tables(seq):
    rows = seq // GRID_W
    row = jnp.repeat(jnp.arange(rows, dtype=F32), GRID_W)
    col = jnp.tile(jnp.arange(GRID_W, dtype=F32), rows)
    n_freq = HEAD_DIM // 4
    inv_freq = ROPE_BASE ** (-jnp.arange(n_freq, dtype=F32) / n_freq)
    ang = jnp.concatenate([row[:, None] * inv_freq, col[:, None] * inv_freq], axis=-1)
    cos, sin = jnp.cos(ang), jnp.sin(ang)
    return jnp.concatenate([cos, cos], axis=-1), jnp.concatenate([-sin, sin], axis=-1)


def kernel(x, c, ctx, c_ctx, ada_w, ada_b, norm_mix, norm_ffn, ab_w_in, ab_q_gain, ab_k_gain, ab_sink, ab_conv_w, ab_conv_b, ab_gate_a_w, ab_gate_a_b, ab_gate_x_w, ab_gate_x_b, ab_lru_lambda, ab_w_out, gqa_w_in, gqa_q_gain, gqa_k_gain, gqa_w_out, router_w, router_bias, moe_w_gate, moe_w_up, moe_w_down):
    batch, seq, d = x.shape
    ctx_len = ctx.shape[1]
    depth = ada_w.shape[0]
    assert depth == 2 and batch < SUBLANES
    n_lat = batch * seq
    n_ctx = batch * ctx_len
    ctx_row = batch

    xl = x.reshape(n_lat, d)
    xc = ctx.reshape(n_ctx, d)
    cc = jnp.zeros((SUBLANES, d), F32).at[:batch].set(c).at[ctx_row].set(c_ctx)
    mod = _ada(cc, ada_w, ada_b)
    cos2, sin2 = _rope_tables(seq)
    rw_t = router_w.T.astype(BF16)
    rb = router_bias.reshape(N_EXPERTS, 1).astype(F32)
    experts = (moe_w_gate, moe_w_up, moe_w_down)

    mod3 = mod[0].reshape(SUBLANES, 1, 6 * d)
    w_in = ab_w_in[0].astype(BF16)
    proj = _norm_mod_matmul(xl, norm_mix[0], mod3, 0, 1, w_in, seq, 0,
                            A_Q_HEADS, A_KV_HEADS, ab_q_gain[0], ab_k_gain[0], cos2, sin2)
    proj_c = _norm_mod_matmul(xc, norm_mix[0], mod3, 0, 1, w_in, n_ctx, ctx_row,
                              A_Q_HEADS, A_KV_HEADS, ab_q_gain[0], ab_k_gain[0])
    att = _win_attn(ab_sink[0], proj, proj_c, batch, seq, ctx_len)
    att_c = _ctx_attn(ab_sink[0], proj_c, batch, ctx_len)

    lru_w = ab_conv_w.shape[2]
    c0 = (A_Q_HEADS + 2 * A_KV_HEADS) * HEAD_DIM
    y_p, yc_p = _lru(_to_chunked(proj[:, c0:c0 + lru_w], batch),
                     _to_chunked(proj[:, c0 + lru_w:c0 + 2 * lru_w], batch),
                     _to_chunked(proj_c[:, c0:c0 + lru_w], batch),
                     _to_chunked(proj_c[:, c0 + lru_w:c0 + 2 * lru_w], batch),
                     ab_conv_w[0], ab_conv_b[0], ab_gate_a_w[0], ab_gate_a_b[0],
                     ab_gate_x_w[0], ab_gate_x_b[0], ab_lru_lambda[0])
    w_out = ab_w_out[0].astype(BF16)
    xl = _out_proj(att, 0, _from_chunked(y_p), 0, w_out, xl, mod3, 2, seq, 0)
    xc = _out_proj(att_c, 0, _from_chunked(yc_p), 0, w_out, xc, mod3, 2, n_ctx, ctx_row)
    xl, xc = _moe(xl, xc, norm_ffn[0], mod3, rw_t, rb, *experts, 0, seq, ctx_row)

    mod3 = mod[1].reshape(SUBLANES, 1, 6 * d)
    w_in = gqa_w_in[0].astype(BF16)
    cw = C_Q_HEADS * HEAD_DIM
    proj = _norm_mod_matmul(xl, norm_mix[1], mod3, 0, 1, w_in, seq, 0,
                            C_Q_HEADS, C_KV_HEADS, gqa_q_gain[0], gqa_k_gain[0], cos2, sin2)
    proj_c = _norm_mod_matmul(xc, norm_mix[1], mod3, 0, 1, w_in[:, cw:], n_ctx, ctx_row,
                              0, C_KV_HEADS, gqa_q_gain[0], gqa_k_gain[0])
    att = _dense_attn(proj, proj_c, batch, seq, ctx_len)
    xl = _out_proj(att, 0, att, 1, gqa_w_out[0].astype(BF16), xl, mod3, 2, seq, 0)
    xl, _ = _moe(xl, None, norm_ffn[1], mod3, rw_t, rb, *experts, 1, seq, ctx_row)
    return xl.reshape(batch, seq, d)
```

```python
import functools

import jax
import jax.numpy as jnp
from jax import lax
from jax.experimental import pallas as pl
from jax.experimental.pallas import tpu as pltpu
from jax.experimental.pallas import tpu_sc as plsc

F32 = jnp.float32
BF16 = jnp.bfloat16

LANES = 128
SUBLANES = 8
VMEM_LIMIT = 56 * 1024 * 1024

HEAD_DIM = 128
GRID_W = 64
WINDOW = 128
BLOCK = 128
ROPE_BASE = 10000.0
EPS = 1e-6
ATTN_SCALE = HEAD_DIM ** -0.5
A_Q_HEADS, A_KV_HEADS = 8, 2
C_Q_HEADS, C_KV_HEADS = 16, 4
GQA_GROUP = 4
LRU_C = 8.0
CONV_W = 4
CONV_LEFT = 2
N_EXPERTS = 16
N_GROUPS = 4
EXPERTS_PER_GROUP = 4
NEG_BIG = -1e30

LRU_CHUNKS = SUBLANES
LRU_JB = 8
LRU_LANE_BLOCKS = 2
MOE_TM = 256
MOE_RANGES = 2
SC_CORES = 2
SC_WORKERS = 32
SC_GATHER_WINDOW = 32


def _cparams(sem, vmem=VMEM_LIMIT):
    return pltpu.CompilerParams(dimension_semantics=sem, vmem_limit_bytes=vmem)


def _dot(a, b):
    return jnp.dot(a, b, preferred_element_type=F32)


def _dot_nt(a, b):
    return lax.dot_general(a, b, (((1,), (1,)), ((), ())), preferred_element_type=F32)


def _ada_kernel(c_ref, w_ref, b_ref, o_ref):
    c = c_ref[...]
    s = (c * jax.nn.sigmoid(c)).astype(BF16)
    o_ref[0] = _dot(s, w_ref[0].astype(BF16)) + b_ref[0]


def _ada(cc, ada_w, ada_b):
    depth, d, n = ada_w.shape
    tn = 1024
    return pl.pallas_call(
        _ada_kernel,
        out_shape=jax.ShapeDtypeStruct((depth, SUBLANES, n), F32),
        grid=(depth, n // tn),
        in_specs=[pl.BlockSpec((SUBLANES, d), lambda l, j: (0, 0)),
                  pl.BlockSpec((1, d, tn), lambda l, j: (l, 0, j)),
                  pl.BlockSpec((1, 1, tn), lambda l, j: (l, 0, j))],
        out_specs=pl.BlockSpec((1, SUBLANES, tn), lambda l, j: (l, 0, j)),
        compiler_params=_cparams(("arbitrary", "arbitrary")),
        name="ada",
    )(cc, ada_w, ada_b.reshape(depth, 1, n))


def _norm_mod_rows(x_ref, g_ref, sh_ref, sc_ref, dst_ref, tm, rc=128):
    g = g_ref[...]
    sc1 = 1.0 + sc_ref[...]
    sh = sh_ref[...]

    def body(r, carry):
        rows = pl.ds(pl.multiple_of(r * rc, rc), rc)
        xf = x_ref[rows, :]
        ms = jnp.mean(xf * xf, axis=-1, keepdims=True)
        xn = (xf * lax.rsqrt(ms + EPS)) * g
        dst_ref[rows, :] = (xn * sc1 + sh).astype(dst_ref.dtype)
        return carry

    lax.fori_loop(0, tm // rc, body, 0)


def _nm_mm_kernel(*refs, tm, tn, n_q, n_k, rope):
    if rope:
        x_ref, g_ref, sh_ref, sc_ref, w_ref, qg_ref, kg_ref, cos_ref, sin_ref, o_ref, hn_ref = refs
    else:
        x_ref, g_ref, sh_ref, sc_ref, w_ref, qg_ref, kg_ref, o_ref, hn_ref = refs
    _norm_mod_rows(x_ref, g_ref, sh_ref, sc_ref, hn_ref, tm)
    h = hn_ref[...]
    heads_per_chunk = tn // HEAD_DIM
    for j in range(w_ref.shape[1] // tn):
        cols = slice(j * tn, (j + 1) * tn)
        y = _dot(h, w_ref[:, cols])
        parts = []
        for hh in range(heads_per_chunk):
            head = j * heads_per_chunk + hh
            yh = y[:, _head_cols(hh)]
            if head < n_q + n_k:
                gain = qg_ref[...] if head < n_q else kg_ref[...]
                ms = jnp.mean(yh * yh, axis=-1, keepdims=True)
                yh = (yh * lax.rsqrt(ms + EPS)) * gain
                if rope:
                    yh = yh * cos_ref[...] + pltpu.roll(yh, HEAD_DIM // 2, 1) * sin_ref[...]
                if head < n_q:
                    yh = yh * ATTN_SCALE
            parts.append(yh.astype(o_ref.dtype))
        o_ref[:, cols] = jnp.concatenate(parts, axis=1)


def _norm_mod_matmul(x, g, mod3, shift_blk, scale_blk, w, rows_per_mod, mod_base,
                     n_q, n_k, q_gain, k_gain, cos2=None, sin2=None, tm=512, tn=512):
    n, d = x.shape
    nout = w.shape[1]
    tm = min(tm, n)
    tn = min(tn, nout)
    per = rows_per_mod // tm
    rope = cos2 is not None

    def mod_row(i):
        return mod_base + i // per

    head_vec = pl.BlockSpec((1, HEAD_DIM), lambda i: (0, 0))
    in_specs = [pl.BlockSpec((tm, d), lambda i: (i, 0)),
                pl.BlockSpec((1, d), lambda i: (0, 0)),
                pl.BlockSpec((None, 1, d), lambda i: (mod_row(i), 0, shift_blk)),
                pl.BlockSpec((None, 1, d), lambda i: (mod_row(i), 0, scale_blk)),
                pl.BlockSpec((d, nout), lambda i: (0, 0), pipeline_mode=pl.Buffered(1)),
                head_vec, head_vec]
    args = [x, g.reshape(1, d), mod3, mod3, w, q_gain.reshape(1, HEAD_DIM), k_gain.reshape(1, HEAD_DIM)]
    if rope:
        tiles_per_seq = cos2.shape[0] // tm
        table = pl.BlockSpec((tm, HEAD_DIM), lambda i: (i % tiles_per_seq, 0))
        in_specs += [table, table]
        args += [cos2, sin2]
    return pl.pallas_call(
        functools.partial(_nm_mm_kernel, tm=tm, tn=tn, n_q=n_q, n_k=n_k, rope=rope),
        out_shape=jax.ShapeDtypeStruct((n, nout), BF16),
        grid=(n // tm,),
        in_specs=in_specs,
        out_specs=pl.BlockSpec((tm, nout), lambda i: (i, 0)),
        scratch_shapes=[pltpu.VMEM((tm, d), BF16)],
        compiler_params=_cparams(("parallel",)),
        name="norm_mod_matmul",
    )(*args)


def _head_cols(h):
    return slice(h * HEAD_DIM, (h + 1) * HEAD_DIM)


def _stack_group(q_ref, kvh):
    return jnp.concatenate([q_ref[:, _head_cols(kvh * GQA_GROUP + g)] for g in range(GQA_GROUP)], axis=0)


def _sink_col(sink_ref, kvh, rows):
    return jnp.concatenate([jnp.full((rows, 1), sink_ref[kvh * GQA_GROUP + g], F32)
                            for g in range(GQA_GROUP)], axis=0)


def _win_attn_kernel(sink_ref, q_ref, kp_ref, kc_ref, kn_ref, vp_ref, vc_ref, vn_ref,
                     kx_ref, vx_ref, o_ref, *, seq, ctx_len):
    n = pl.program_id(1)
    rows = GQA_GROUP * BLOCK
    nk = 3 * BLOCK + ctx_len
    qi = lax.broadcasted_iota(jnp.int32, (rows, nk), 0) % BLOCK
    kj = lax.broadcasted_iota(jnp.int32, (rows, nk), 1)
    kpos = n * BLOCK - BLOCK + kj
    in_band = (jnp.abs(kj - BLOCK - qi) <= WINDOW) & (kpos >= 0) & (kpos < seq)
    valid = (kj >= 3 * BLOCK) | in_band
    ones = jnp.ones((nk, HEAD_DIM), BF16)
    for kvh in range(A_KV_HEADS):
        cols = _head_cols(kvh)
        q4 = _stack_group(q_ref, kvh)
        ka = jnp.concatenate([kp_ref[:, cols], kc_ref[:, cols], kn_ref[:, cols], kx_ref[:, cols]], axis=0)
        va = jnp.concatenate([vp_ref[:, cols], vc_ref[:, cols], vn_ref[:, cols], vx_ref[:, cols]], axis=0)
        s = jnp.where(valid, _dot_nt(q4, ka), NEG_BIG)
        sk = _sink_col(sink_ref, kvh, BLOCK)
        m = jnp.maximum(jnp.max(s, axis=-1, keepdims=True), sk)
        p = jnp.exp((s - m).astype(BF16))
        acc = _dot(p, jnp.concatenate([va, ones], axis=1))
        o = acc[:, 0:HEAD_DIM] / (acc[:, HEAD_DIM:] + jnp.exp(sk - m))
        for g in range(GQA_GROUP):
            o_ref[:, _head_cols(kvh * GQA_GROUP + g)] = o[g * BLOCK:(g + 1) * BLOCK].astype(o_ref.dtype)


def _win_attn(sink, proj, proj_c, batch, seq, ctx_len):
    nb = seq // BLOCK
    kvw = A_KV_HEADS * HEAD_DIM
    k_blk = A_Q_HEADS * HEAD_DIM // kvw
    v_blk = k_blk + 1

    def prev(b, n):
        return b * nb + jnp.maximum(n - 1, 0)

    def cur(b, n):
        return b * nb + n

    def nxt(b, n):
        return b * nb + jnp.minimum(n + 1, nb - 1)

    return pl.pallas_call(
        functools.partial(_win_attn_kernel, seq=seq, ctx_len=ctx_len),
        out_shape=jax.ShapeDtypeStruct((batch * seq, A_Q_HEADS * HEAD_DIM), BF16),
        grid=(batch, nb),
        in_specs=[pl.BlockSpec(memory_space=pltpu.SMEM),
                  pl.BlockSpec((BLOCK, A_Q_HEADS * HEAD_DIM), lambda b, n: (cur(b, n), 0)),
                  pl.BlockSpec((BLOCK, kvw), lambda b, n: (prev(b, n), k_blk)),
                  pl.BlockSpec((BLOCK, kvw), lambda b, n: (cur(b, n), k_blk)),
                  pl.BlockSpec((BLOCK, kvw), lambda b, n: (nxt(b, n), k_blk)),
                  pl.BlockSpec((BLOCK, kvw), lambda b, n: (prev(b, n), v_blk)),
                  pl.BlockSpec((BLOCK, kvw), lambda b, n: (cur(b, n), v_blk)),
                  pl.BlockSpec((BLOCK, kvw), lambda b, n: (nxt(b, n), v_blk)),
                  pl.BlockSpec((ctx_len, kvw), lambda b, n: (b, k_blk)),
                  pl.BlockSpec((ctx_len, kvw), lambda b, n: (b, v_blk))],
        out_specs=pl.BlockSpec((BLOCK, A_Q_HEADS * HEAD_DIM), lambda b, n: (cur(b, n), 0)),
        compiler_params=_cparams(("parallel", "parallel")),
        name="win_attn",
    )(sink, proj, proj, proj, proj, proj, proj, proj, proj_c, proj_c)


def _ctx_attn_kernel(sink_ref, q_ref, k_ref, v_ref, o_ref, *, ctx_len):
    kvh = pl.program_id(1)
    q4 = jnp.concatenate([q_ref[:, _head_cols(g)] for g in range(GQA_GROUP)], axis=0)
    s = _dot_nt(q4, k_ref[...])
    sk = jnp.concatenate([jnp.full((ctx_len, 1), sink_ref[kvh * GQA_GROUP + g], F32)
                          for g in range(GQA_GROUP)], axis=0)
    m = jnp.maximum(jnp.max(s, axis=-1, keepdims=True), sk)
    p = jnp.exp(s - m)
    den = jnp.sum(p, axis=-1, keepdims=True) + jnp.exp(sk - m)
    o = _dot(p.astype(BF16), v_ref[...]) / den
    for g in range(GQA_GROUP):
        o_ref[:, _head_cols(g)] = o[g * ctx_len:(g + 1) * ctx_len].astype(o_ref.dtype)


def _ctx_attn(sink, proj_c, batch, ctx_len):
    k_blk = A_Q_HEADS
    v_blk = A_Q_HEADS + A_KV_HEADS
    gw = GQA_GROUP * HEAD_DIM
    return pl.pallas_call(
        functools.partial(_ctx_attn_kernel, ctx_len=ctx_len),
        out_shape=jax.ShapeDtypeStruct((batch * ctx_len, A_Q_HEADS * HEAD_DIM), BF16),
        grid=(batch, A_KV_HEADS),
        in_specs=[pl.BlockSpec(memory_space=pltpu.SMEM),
                  pl.BlockSpec((ctx_len, gw), lambda b, h: (b, h)),
                  pl.BlockSpec((ctx_len, HEAD_DIM), lambda b, h: (b, k_blk + h)),
                  pl.BlockSpec((ctx_len, HEAD_DIM), lambda b, h: (b, v_blk + h))],
        out_specs=pl.BlockSpec((ctx_len, gw), lambda b, h: (b, h)),
        compiler_params=_cparams(("parallel", "parallel")),
        name="ctx_attn",
    )(sink, proj_c, proj_c, proj_c)


def _sigmoid(x):
    return 0.5 * jnp.tanh(0.5 * x) + 0.5


def _gelu_tanh(x):
    return 0.5 * x * (1.0 + jnp.tanh(0.7978845608028654 * (x + 0.044715 * (x * x * x))))


def _lru_sequence(x_ref, xg_ref, y_ref, xp_ref, a_ref, b_ref, w, init, rows, jb):
    conv_w, conv_b, wa, ba, wx, bx, c_logsig = w
    width = x_ref.shape[-1]
    sub = lax.broadcasted_iota(jnp.int32, (1, SUBLANES, width), 1)

    def block_diag(ub, wd):
        return jnp.concatenate([_dot(ub[:, _head_cols(n)], wd[n]) for n in range(width // LANES)], axis=1)

    def fill(r, carry):
        rr = pl.ds(pl.multiple_of(r * jb, jb), jb)
        xp_ref[pl.ds(pl.multiple_of(r * jb, jb) + CONV_LEFT, jb)] = x_ref[rr].astype(F32)
        return carry

    lax.fori_loop(0, rows // jb, fill, 0)
    tail = x_ref[rows - CONV_LEFT:rows].astype(F32)
    xp_ref[0:CONV_LEFT] = jnp.where(sub == 0, 0.0, pltpu.roll(tail, 1, 1))
    head = x_ref[0:1].astype(F32)
    xp_ref[rows + CONV_LEFT:rows + CONV_LEFT + 1] = jnp.where(
        sub == SUBLANES - 1, 0.0, pltpu.roll(head, SUBLANES - 1, 1))

    def gates(r, carry):
        j0 = pl.multiple_of(r * jb, jb)
        u = conv_b
        for k in range(CONV_W):
            u = u + conv_w[k] * xp_ref[pl.ds(j0 + k, jb)]
        u2 = u.reshape(jb * SUBLANES, width)
        ub = u2.astype(BF16)
        for d in range(2):
            r_gate = _sigmoid(block_diag(ub, wa[d]) + ba[d])
            i_gate = _sigmoid(block_diag(ub, wx[d]) + bx[d])
            log_a = c_logsig[d] * r_gate
            a = jnp.exp(log_a)
            b = jnp.sqrt(1.0 - a * a) * (i_gate * u2)
            a_ref[d, pl.ds(j0, jb)] = a.reshape(jb, SUBLANES, width)
            b_ref[d, pl.ds(j0, jb)] = b.reshape(jb, SUBLANES, width)
        return carry

    lax.fori_loop(0, rows // jb, gates, 0)

    def scan(j, carry):
        hf, pf, hb, pb = carry
        jr = rows - 1 - j
        af = a_ref[0, j]
        hf = af * hf + b_ref[0, j]
        pf = pf * af
        b_ref[0, j] = hf
        a_ref[0, j] = pf
        ab = a_ref[1, jr]
        hb = ab * hb + b_ref[1, jr]
        pb = pb * ab
        b_ref[1, jr] = hb
        a_ref[1, jr] = pb
        return hf, pf, hb, pb

    z = jnp.zeros((SUBLANES, width), F32)
    o = jnp.ones((SUBLANES, width), F32)
    lax.fori_loop(0, rows, scan, (z, o, z, o), unroll=8)

    hf_last, pf_last = b_ref[0, rows - 1], a_ref[0, rows - 1]
    hb_last, pb_last = b_ref[1, 0], a_ref[1, 0]
    s = init[0]
    carry_f = []
    for c in range(SUBLANES):
        carry_f.append(s)
        s = hf_last[c:c + 1] + pf_last[c:c + 1] * s
    out_f = s
    s = init[1]
    carry_b = [None] * SUBLANES
    for c in reversed(range(SUBLANES)):
        carry_b[c] = s
        s = hb_last[c:c + 1] + pb_last[c:c + 1] * s
    out_b = s
    cf = jnp.concatenate(carry_f, axis=0)
    cb = jnp.concatenate(carry_b, axis=0)

    def emit(r, carry):
        rr = pl.ds(pl.multiple_of(r * jb, jb), jb)
        h = (b_ref[0, rr] + a_ref[0, rr] * cf) + (b_ref[1, rr] + a_ref[1, rr] * cb)
        y_ref[rr] = (h * _gelu_tanh(xg_ref[rr].astype(F32))).astype(y_ref.dtype)
        return carry

    lax.fori_loop(0, rows // jb, emit, 0)
    return out_f, out_b


def _lru_kernel(xr_ref, xg_ref, xrc_ref, xgc_ref, cw_ref, cb_ref, wa_ref, ba_ref, wx_ref, bx_ref,
                lam_ref, y_ref, yc_ref, xp_ref, a_ref, b_ref, *, rows, rows_c):
    c_logsig = [LRU_C * jax.nn.log_sigmoid(lam_ref[d]) for d in range(2)]
    w = ([cw_ref[k] for k in range(CONV_W)], cb_ref[0],
         [wa_ref[d] for d in range(2)], [ba_ref[d] for d in range(2)],
         [wx_ref[d] for d in range(2)], [bx_ref[d] for d in range(2)], c_logsig)
    zero = jnp.zeros((1, xr_ref.shape[-1]), F32)
    sf, sb = _lru_sequence(xrc_ref, xgc_ref, yc_ref, xp_ref, a_ref, b_ref, w, (zero, zero),
                           rows_c, min(LRU_JB, rows_c))
    _lru_sequence(xr_ref, xg_ref, y_ref, xp_ref, a_ref, b_ref, w, (sf, sb), rows, LRU_JB)


def _lru(xr, xg, xrc, xgc, conv_w, conv_b, wa, ba, wx, bx, lam):
    batch, rows, _, width = xr.shape
    rows_c = xrc.shape[1]
    cw = LRU_LANE_BLOCKS * LANES
    seq_spec = pl.BlockSpec((None, rows, SUBLANES, cw), lambda b, n: (b, 0, 0, n))
    ctx_spec = pl.BlockSpec((None, rows_c, SUBLANES, cw), lambda b, n: (b, 0, 0, n))
    vec2 = pl.BlockSpec((2, 1, cw), lambda b, n: (0, 0, n))
    mat2 = pl.BlockSpec((2, LRU_LANE_BLOCKS, LANES, LANES), lambda b, n: (0, n, 0, 0))
    return pl.pallas_call(
        functools.partial(_lru_kernel, rows=rows, rows_c=rows_c),
        out_shape=(jax.ShapeDtypeStruct(xr.shape, BF16), jax.ShapeDtypeStruct(xrc.shape, BF16)),
        grid=(batch, width // cw),
        in_specs=[seq_spec, seq_spec, ctx_spec, ctx_spec,
                  pl.BlockSpec((CONV_W, 1, cw), lambda b, n: (0, 0, n)),
                  pl.BlockSpec((1, 1, cw), lambda b, n: (0, 0, n)),
                  mat2, vec2, mat2, vec2, vec2],
        out_specs=(seq_spec, ctx_spec),
        scratch_shapes=[pltpu.VMEM((rows + CONV_W - 1, SUBLANES, cw), F32),
                        pltpu.VMEM((2, rows, SUBLANES, cw), F32),
                        pltpu.VMEM((2, rows, SUBLANES, cw), F32)],
        compiler_params=_cparams(("parallel", "parallel")),
        name="rglru",
    )(xr, xg, xrc, xgc, conv_w.reshape(CONV_W, 1, width), conv_b.reshape(1, 1, width),
      wa.astype(BF16), ba.reshape(2, 1, width), wx.astype(BF16), bx.reshape(2, 1, width),
      lam.reshape(2, 1, width))


def _to_chunked(a, batch):
    t = a.shape[0] // batch
    return a.reshape(batch, LRU_CHUNKS, t // LRU_CHUNKS, a.shape[1]).transpose(0, 2, 1, 3)


def _from_chunked(a):
    b, r, c, w = a.shape
    return a.transpose(0, 2, 1, 3).reshape(b * r * c, w)


def _out_proj_kernel(a1_ref, a2_ref, w1_ref, w2_ref, x_ref, g_ref, o_ref):
    y = _dot(a1_ref[...], w1_ref[...]) + _dot(a2_ref[...], w2_ref[...])
    o_ref[...] = x_ref[...] + g_ref[...] * y


def _out_proj(a1, a1_blk, a2, a2_blk, w, x, mod3, gate_blk, rows_per_mod, mod_base, tm=2048, tn=512):
    n, d = x.shape
    kh = w.shape[0] // 2
    tm = min(tm, n)
    per = rows_per_mod // tm
    gpb = d // tn
    return pl.pallas_call(
        _out_proj_kernel,
        out_shape=jax.ShapeDtypeStruct((n, d), F32),
        grid=(n // tm, d // tn),
        in_specs=[pl.BlockSpec((tm, kh), lambda i, j: (i, a1_blk)),
                  pl.BlockSpec((tm, kh), lambda i, j: (i, a2_blk)),
                  pl.BlockSpec((kh, tn), lambda i, j: (0, j)),
                  pl.BlockSpec((kh, tn), lambda i, j: (1, j)),
                  pl.BlockSpec((tm, tn), lambda i, j: (i, j)),
                  pl.BlockSpec((None, 1, tn), lambda i, j: (mod_base + i // per, 0, gate_blk * gpb + j))],
        out_specs=pl.BlockSpec((tm, tn), lambda i, j: (i, j)),
        compiler_params=_cparams(("parallel", "parallel")),
        name="out_proj",
    )(a1, a2, w, w, x, mod3)


def _lane_block_max(s):
    mm = s[:, 0:LANES]
    for t in range(1, s.shape[1] // LANES):
        mm = jnp.maximum(mm, s[:, t * LANES:(t + 1) * LANES])
    return mm


def _exp_blocks(s, mrep):
    return jnp.concatenate(
        [jnp.exp((s[:, t * LANES:(t + 1) * LANES] - mrep).astype(BF16)) for t in range(s.shape[1] // LANES)],
        axis=1)


def _dense_attn_kernel(q_ref, k_ref, v_ref, kx_ref, vx_ref, o_ref,
                       s_ref, sx_ref, m_ref, acc_ref, va_ref, vax_ref, *, tq, seq, kc):
    n_chunks = seq // kc

    @pl.when(pl.program_id(2) == 0)
    def _():
        for c in range(n_chunks):
            va_ref[c, :, 0:HEAD_DIM] = v_ref[c * kc:(c + 1) * kc, :]
            va_ref[c, :, HEAD_DIM:] = jnp.ones((kc, HEAD_DIM), BF16)
        vax_ref[:, 0:HEAD_DIM] = vx_ref[...]
        vax_ref[:, HEAD_DIM:] = jnp.ones((vx_ref.shape[0], HEAD_DIM), BF16)

    q4 = jnp.concatenate([q_ref[:, _head_cols(g)] for g in range(GQA_GROUP)], axis=0)
    sx = _dot_nt(q4, kx_ref[...])
    sx_ref[...] = sx
    m_ref[...] = _lane_block_max(sx)

    def sweep1(c, carry):
        s = _dot_nt(q4, k_ref[pl.ds(pl.multiple_of(c * kc, kc), kc), :])
        s_ref[c] = s
        m_ref[...] = jnp.maximum(m_ref[...], _lane_block_max(s))
        return carry

    lax.fori_loop(0, n_chunks, sweep1, 0)
    mrep = jnp.broadcast_to(jnp.max(m_ref[...], axis=-1, keepdims=True), m_ref.shape)
    m_ref[...] = mrep
    acc_ref[...] = _dot(_exp_blocks(sx_ref[...], mrep), vax_ref[...])

    def sweep2(c, carry):
        acc_ref[...] += _dot(_exp_blocks(s_ref[c], m_ref[...]), va_ref[c])
        return carry

    lax.fori_loop(0, n_chunks, sweep2, 0)
    o = acc_ref[:, 0:HEAD_DIM] / acc_ref[:, HEAD_DIM:]
    for g in range(GQA_GROUP):
        o_ref[:, _head_cols(g)] = o[g * tq:(g + 1) * tq].astype(o_ref.dtype)


def _dense_attn(proj, proj_c, batch, seq, ctx_len, tq=256, kc=1024):
    gw = GQA_GROUP * HEAD_DIM
    nq = seq // tq
    rows = GQA_GROUP * tq
    k_blk = C_Q_HEADS
    v_blk = C_Q_HEADS + C_KV_HEADS
    vx_blk = C_KV_HEADS
    return pl.pallas_call(
        functools.partial(_dense_attn_kernel, tq=tq, seq=seq, kc=kc),
        out_shape=jax.ShapeDtypeStruct((batch * seq, C_Q_HEADS * HEAD_DIM), BF16),
        grid=(batch, C_KV_HEADS, nq),
        in_specs=[pl.BlockSpec((tq, gw), lambda b, h, i: (b * nq + i, h)),
                  pl.BlockSpec((seq, HEAD_DIM), lambda b, h, i: (b, k_blk + h)),
                  pl.BlockSpec((seq, HEAD_DIM), lambda b, h, i: (b, v_blk + h)),
                  pl.BlockSpec((ctx_len, HEAD_DIM), lambda b, h, i: (b, h)),
                  pl.BlockSpec((ctx_len, HEAD_DIM), lambda b, h, i: (b, vx_blk + h))],
        out_specs=pl.BlockSpec((tq, gw), lambda b, h, i: (b * nq + i, h)),
        scratch_shapes=[pltpu.VMEM((seq // kc, rows, kc), F32),
                        pltpu.VMEM((rows, ctx_len), F32),
                        pltpu.VMEM((rows, LANES), F32),
                        pltpu.VMEM((rows, 2 * HEAD_DIM), F32),
                        pltpu.VMEM((seq // kc, kc, 2 * HEAD_DIM), BF16),
                        pltpu.VMEM((ctx_len, 2 * HEAD_DIM), BF16)],
        compiler_params=_cparams(("parallel", "parallel", "arbitrary")),
        name="dense_attn",
    )(proj, proj, proj, proj_c, proj_c)


def _router_rows(biased, scores):
    v = [biased[e:e + 1, :] for e in range(N_EXPERTS)]
    s = [scores[e:e + 1, :] for e in range(N_EXPERTS)]

    def top2_sum(vals):
        best = vals[0] + vals[1]
        for i in range(len(vals)):
            for j in range(i + 1, len(vals)):
                if (i, j) != (0, 1):
                    best = jnp.maximum(best, vals[i] + vals[j])
        return best

    gsum = [top2_sum(v[g * EXPERTS_PER_GROUP:(g + 1) * EXPERTS_PER_GROUP]) for g in range(N_GROUPS)]
    sel = jnp.zeros_like(gsum[0], dtype=jnp.int32)
    best = gsum[0]
    for g in range(1, N_GROUPS):
        take = gsum[g] > best
        sel = jnp.where(take, g, sel)
        best = jnp.where(take, gsum[g], best)

    def pick_group(rows, i):
        out = rows[i]
        for g in range(1, N_GROUPS):
            out = jnp.where(sel == g, rows[g * EXPERTS_PER_GROUP + i], out)
        return out

    cand = [pick_group(v, i) for i in range(EXPERTS_PER_GROUP)]
    cand_s = [pick_group(s, i) for i in range(EXPERTS_PER_GROUP)]
    i1 = jnp.zeros_like(sel)
    b1 = cand[0]
    for i in range(1, EXPERTS_PER_GROUP):
        take = cand[i] > b1
        i1 = jnp.where(take, i, i1)
        b1 = jnp.where(take, cand[i], b1)
    i2 = jnp.full_like(sel, -1)
    b2 = jnp.full_like(b1, -jnp.inf)
    for i in range(EXPERTS_PER_GROUP):
        take = (i1 != i) & ((cand[i] > b2) | (i2 < 0))
        i2 = jnp.where(take, i, i2)
        b2 = jnp.where(take, cand[i], b2)

    def pick_idx(rows, idx):
        out = rows[0]
        for i in range(1, EXPERTS_PER_GROUP):
            out = jnp.where(idx == i, rows[i], out)
        return out

    s0 = pick_idx(cand_s, i1)
    s1 = pick_idx(cand_s, i2)
    tot = s0 + s1
    e0 = (sel * EXPERTS_PER_GROUP + i1).astype(F32)
    e1 = (sel * EXPERTS_PER_GROUP + i2).astype(F32)
    return e0, e1, s0 / tot, s1 / tot


def _norm_router_kernel(*refs, tm, n_lat_tiles, has_ctx):
    if has_ctx:
        x_ref, xc_ref, g_ref, sh_ref, sc_ref, rw_ref, rb_ref, hp_ref, r_ref, h_ref = refs
        i = pl.program_id(0)

        @pl.when(i < n_lat_tiles)
        def _():
            _norm_mod_rows(x_ref, g_ref, sh_ref, sc_ref, h_ref, tm)

        @pl.when(i >= n_lat_tiles)
        def _():
            _norm_mod_rows(xc_ref, g_ref, sh_ref, sc_ref, h_ref, tm)
    else:
        x_ref, g_ref, sh_ref, sc_ref, rw_ref, rb_ref, hp_ref, r_ref, h_ref = refs
        _norm_mod_rows(x_ref, g_ref, sh_ref, sc_ref, h_ref, tm)

    def pack(r, carry):
        rows = pl.ds(pl.multiple_of(r * 128, 128), 128)
        hp_ref[rows, :] = _pack_halves(h_ref[rows, :])
        return carry

    lax.fori_loop(0, tm // 128, pack, 0)
    logits = _dot_nt(rw_ref[...], h_ref[...])
    scores = jax.nn.sigmoid(logits)
    e0, e1, w0, w1 = _router_rows(scores + rb_ref[...], scores)
    zero = jnp.zeros_like(w0)
    r_ref[...] = jnp.concatenate([e0, e1, w0, w1, zero, zero, zero, zero], axis=0)


def _pack_halves(v):
    c = v.shape[1] // 2
    lo = lax.bitcast_convert_type(v[:, :c].astype(BF16).astype(F32), jnp.uint32)
    hi = lax.bitcast_convert_type(v[:, c:].astype(BF16).astype(F32), jnp.uint32)
    return (lo >> 16) | (hi & jnp.uint32(0xFFFF0000))


def _unpack_halves(p):
    lo = lax.bitcast_convert_type(p << 16, F32)
    hi = lax.bitcast_convert_type(p & jnp.uint32(0xFFFF0000), F32)
    return lo, hi


def _sc_gather_rows(table, idx):
    n, w = idx.shape[0], table.shape[1]
    win = SC_GATHER_WINDOW
    per = n // (win * SC_WORKERS)
    assert per * win * SC_WORKERS == n and per >= 1
    mesh = plsc.VectorSubcoreMesh(core_axis_name="core", subcore_axis_name="subcore")

    @functools.partial(
        pl.kernel, out_type=jax.ShapeDtypeStruct((n, w), table.dtype), mesh=mesh, name="sc_gather_rows",
        scratch_types=[pltpu.VMEM((per * win,), jnp.int32),
                       pltpu.VMEM((win, w), table.dtype), pltpu.VMEM((win, w), table.dtype),
                       pltpu.SemaphoreType.DMA, pltpu.SemaphoreType.DMA,
                       pltpu.SemaphoreType.DMA, pltpu.SemaphoreType.DMA])
    def gather(x_hbm, i_hbm, o_hbm, i_v, buf0, buf1, gsem0, gsem1, wsem0, wsem1):
        wid = lax.axis_index("subcore") * SC_CORES + lax.axis_index("core")
        base = wid * (per * win)
        bufs, gsems, wsems = (buf0, buf1), (gsem0, gsem1), (wsem0, wsem1)
        pltpu.sync_copy(i_hbm.at[pl.ds(base, per * win)], i_v)

        def gather_copy(t, b):
            return pltpu.make_async_copy(x_hbm.at[i_v.at[pl.ds(t * win, win)]], bufs[b], gsems[b])

        def write_copy(t, b):
            return pltpu.make_async_copy(bufs[b], o_hbm.at[pl.ds(base + t * win, win)], wsems[b])

        def step(t, b):
            gather_copy(t, b).wait()

            @pl.when(t > 0)
            def _():
                write_copy(t - 1, 1 - b).wait()

            @pl.when(t + 1 < per)
            def _():
                gather_copy(t + 1, 1 - b).start()

            write_copy(t, b).start()

        gather_copy(0, 0).start()

        @pl.loop(0, per // 2)
        def _(p):
            step(2 * p, 0)
            step(2 * p + 1, 1)

        if per % 2:
            step(per - 1, 0)
        write_copy(per - 1, (per - 1) % 2).wait()

    return gather(table, idx)


def _lat_ctx_maps(n_lat, per, ctx_mod_row):
    def mod_row(i):
        return jnp.where(i < n_lat, i // per, ctx_mod_row)

    def lat(i):
        return (jnp.minimum(i, n_lat - 1), 0)

    def ctx(i):
        return (jnp.maximum(i - n_lat, 0), 0)

    return mod_row, lat, ctx


def _norm_router(x, cx, g, mod3, shift_blk, scale_blk, rw_t, rb, rows_per_mod, ctx_mod_row, tm=512):
    n, d = x.shape
    n_lat = n // tm
    has_ctx = cx is not None
    ntot = n_lat + (cx.shape[0] // tm if has_ctx else 0)
    mod_row, lat, ctx = _lat_ctx_maps(n_lat, rows_per_mod // tm, ctx_mod_row)
    row_specs = [pl.BlockSpec((tm, d), lat)] + ([pl.BlockSpec((tm, d), ctx)] if has_ctx else [])
    row_args = [x] + ([cx] if has_ctx else [])
    return pl.pallas_call(
        functools.partial(_norm_router_kernel, tm=tm, n_lat_tiles=n_lat, has_ctx=has_ctx),
        out_shape=(jax.ShapeDtypeStruct((ntot * tm, d // 2), jnp.uint32),
                   jax.ShapeDtypeStruct((SUBLANES, ntot * tm), F32)),
        grid=(ntot,),
        in_specs=row_specs + [
            pl.BlockSpec((1, d), lambda i: (0, 0)),
            pl.BlockSpec((None, 1, d), lambda i: (mod_row(i), 0, shift_blk)),
            pl.BlockSpec((None, 1, d), lambda i: (mod_row(i), 0, scale_blk)),
            pl.BlockSpec((N_EXPERTS, d), lambda i: (0, 0)),
            pl.BlockSpec((N_EXPERTS, 1), lambda i: (0, 0))],
        out_specs=(pl.BlockSpec((tm, d // 2), lambda i: (i, 0)),
                   pl.BlockSpec((SUBLANES, tm), lambda i: (0, i))),
        scratch_shapes=[pltpu.VMEM((tm, d), BF16)],
        compiler_params=_cparams(("arbitrary",)),
        name="norm_router",
    )(*row_args, g.reshape(1, d), mod3, mod3, rw_t, rb)


def _cast_rows(src_ref, dst_ref, rb=256):
    def body(r, carry):
        rows = pl.ds(pl.multiple_of(r * rb, rb), rb)
        dst_ref[rows, :] = src_ref[0, rows, :].astype(dst_ref.dtype)
        return carry

    lax.fori_loop(0, dst_ref.shape[0] // rb, body, 0)


def _tile_state(te_ref, tv_ref, tile0):
    i = pl.program_id(0)
    t = tile0 + i
    live = tv_ref[t] > 0
    new_expert = (i == 0) | (te_ref[t] != te_ref[jnp.maximum(t - 1, 0)])
    return live, new_expert


def _gmm_up_kernel(te_ref, tv_ref, xs_ref, wg_ref, wu_ref, h_ref, wg_b, wu_b, *, tile0):
    live, new_expert = _tile_state(te_ref, tv_ref, tile0)

    @pl.when(live & new_expert)
    def _():
        _cast_rows(wg_ref, wg_b)
        _cast_rows(wu_ref, wu_b)

    @pl.when(live)
    def _():
        lo, hi = _unpack_halves(xs_ref[...])
        x = jnp.concatenate([lo.astype(BF16), hi.astype(BF16)], axis=1)
        gate = _dot(x, wg_b[...])
        up = _dot(x, wu_b[...])
        h_ref[...] = ((gate * jax.nn.sigmoid(gate)) * up).astype(h_ref.dtype)

    @pl.when(jnp.logical_not(live))
    def _():
        h_ref[...] = jnp.zeros(h_ref.shape, h_ref.dtype)


def _gmm_down_kernel(te_ref, tv_ref, h_ref, wd_ref, *rest, tile0):
    o_ref, wd_b = rest[-2:]
    live, new_expert = _tile_state(te_ref, tv_ref, tile0)

    @pl.when(live & new_expert)
    def _():
        _cast_rows(wd_ref, wd_b)

    @pl.when(live)
    def _():
        o_ref[...] = _pack_halves(_dot(h_ref[...], wd_b[...]))

    @pl.when(jnp.logical_not(live))
    def _():
        o_ref[...] = jnp.zeros(o_ref.shape, o_ref.dtype)


def _grouped_mlp(tile_expert, tile_valid, xs_part, tile0, ys_prev, wg, wu, wd, layer, tm=MOE_TM):
    dpk = xs_part.shape[1]
    d, dff = wg.shape[2], wg.shape[3]
    n_tiles = tile_expert.shape[0]
    n_part = xs_part.shape[0] // tm

    def expert_block(rows, cols):
        return pl.BlockSpec((None, 1, rows, cols), lambda i, te, tv: (layer, te[tile0 + i], 0, 0))

    def row_block(cols):
        return pl.BlockSpec((tm, cols), lambda i, te, tv: (i, 0))

    hid = pl.pallas_call(
        functools.partial(_gmm_up_kernel, tile0=tile0),
        out_shape=jax.ShapeDtypeStruct((n_part * tm, dff), BF16),
        grid_spec=pltpu.PrefetchScalarGridSpec(
            num_scalar_prefetch=2,
            grid=(n_part,),
            in_specs=[row_block(dpk), expert_block(d, dff), expert_block(d, dff)],
            out_specs=row_block(dff),
            scratch_shapes=[pltpu.VMEM((d, dff), BF16), pltpu.VMEM((d, dff), BF16)]),
        compiler_params=_cparams(("arbitrary",)),
        name="grouped_mlp_up",
    )(tile_expert, tile_valid, xs_part, wg, wu)

    in_specs = [row_block(dff), expert_block(dff, d)]
    args = [tile_expert, tile_valid, hid, wd]
    aliases = {}
    if ys_prev is not None:
        in_specs.append(pl.BlockSpec(memory_space=pl.ANY))
        args.append(ys_prev)
        aliases = {len(args) - 1: 0}
    return pl.pallas_call(
        functools.partial(_gmm_down_kernel, tile0=tile0),
        out_shape=jax.ShapeDtypeStruct((n_tiles * tm, dpk), jnp.uint32),
        grid_spec=pltpu.PrefetchScalarGridSpec(
            num_scalar_prefetch=2,
            grid=(n_part,),
            in_specs=in_specs,
            out_specs=pl.BlockSpec((tm, dpk), lambda i, te, tv: (tile0 + i, 0)),
            scratch_shapes=[pltpu.VMEM((dff, d), BF16)]),
        input_output_aliases=aliases,
        compiler_params=_cparams(("arbitrary",)),
        name="grouped_mlp_down",
    )(*args)


def _combine_kernel(*refs, n_lat_tiles, has_ctx):
    if has_ctx:
        x_ref, xc_ref, y0_ref, y1_ref, r_ref, g_ref, o_ref, oc_ref = refs
    else:
        x_ref, y0_ref, y1_ref, r_ref, g_ref, o_ref = refs
    w0 = r_ref[:, 2:3]
    w1 = r_ref[:, 3:4]
    lo0, hi0 = _unpack_halves(y0_ref[...])
    lo1, hi1 = _unpack_halves(y1_ref[...])
    f = g_ref[...] * jnp.concatenate([w0 * lo0 + w1 * lo1, w0 * hi0 + w1 * hi1], axis=1)
    if not has_ctx:
        o_ref[...] = x_ref[...] + f
        return
    i = pl.program_id(0)

    @pl.when(i < n_lat_tiles)
    def _():
        o_ref[...] = x_ref[...] + f

    @pl.when(i >= n_lat_tiles)
    def _():
        oc_ref[...] = xc_ref[...] + f


def _combine(x, cx, yg, route_cols, mod3, gate_blk, rows_per_mod, ctx_mod_row, tm=512):
    n, d = x.shape
    n_lat = n // tm
    has_ctx = cx is not None
    ntot = n_lat + (cx.shape[0] // tm if has_ctx else 0)
    mod_row, lat, ctx = _lat_ctx_maps(n_lat, rows_per_mod // tm, ctx_mod_row)
    row_specs = [pl.BlockSpec((tm, d), lat)] + ([pl.BlockSpec((tm, d), ctx)] if has_ctx else [])
    row_args = [x] + ([cx] if has_ctx else [])
    out_shape = [jax.ShapeDtypeStruct(x.shape, F32)] + ([jax.ShapeDtypeStruct(cx.shape, F32)] if has_ctx else [])
    out = pl.pallas_call(
        functools.partial(_combine_kernel, n_lat_tiles=n_lat, has_ctx=has_ctx),
        out_shape=tuple(out_shape),
        grid=(ntot,),
        in_specs=row_specs + [
            pl.BlockSpec((tm, d // 2), lambda i: (i, 0)),
            pl.BlockSpec((tm, d // 2), lambda i: (ntot + i, 0)),
            pl.BlockSpec((tm, SUBLANES), lambda i: (i, 0)),
            pl.BlockSpec((None, 1, d), lambda i: (mod_row(i), 0, gate_blk))],
        out_specs=tuple(row_specs),
        compiler_params=_cparams(("arbitrary",)),
        name="moe_combine",
    )(*row_args, yg, yg, route_cols, mod3)
    return out if has_ctx else (out[0], None)


def _dispatch_plan(route, tm):
    n = route.shape[1]
    e_flat = jnp.concatenate([route[0], route[1]]).astype(jnp.int32)
    n_assign = 2 * n
    n_tiles = n_assign // tm + N_EXPERTS
    experts = jnp.arange(N_EXPERTS, dtype=jnp.int32)[:, None]
    onehot = (experts == e_flat[None, :]).astype(jnp.int32)
    csum = jnp.cumsum(onehot, axis=1)
    counts = csum[:, -1]
    padded = ((counts + tm - 1) // tm) * tm
    ends_p = jnp.cumsum(padded)
    starts_p = ends_p - padded
    starts_c = jnp.cumsum(counts) - counts
    dest = jnp.sum(onehot * (csum - 1 + starts_p[:, None]), axis=0)
    order = jnp.argsort(e_flat, stable=True).astype(jnp.int32)
    p = jnp.arange(n_tiles * tm, dtype=jnp.int32)[None, :]
    owner = ((p >= starts_p[:, None]) & (p < ends_p[:, None])).astype(jnp.int32)
    within = jnp.sum(owner * (p - starts_p[:, None]), axis=0)
    live = jnp.sum(owner * (p - starts_p[:, None] < counts[:, None]), axis=0) > 0
    compact = jnp.sum(owner * starts_c[:, None], axis=0) + within
    src_tok = jnp.where(live, order[jnp.clip(compact, 0, n_assign - 1)] % n, 0)
    tile_start = jnp.arange(n_tiles, dtype=jnp.int32) * tm
    tile_valid = (tile_start < ends_p[-1]).astype(jnp.int32)
    last_tile = jnp.maximum(ends_p[-1] // tm - 1, 0) * tm
    tile_expert = jnp.searchsorted(ends_p, jnp.minimum(tile_start, last_tile), side="right").astype(jnp.int32)
    tile_expert = jnp.minimum(tile_expert, N_EXPERTS - 1)
    return src_tok, dest, tile_expert, tile_valid


def _moe(x, cx, g, mod3, rw_t, rb, wg, wu, wd, layer, rows_per_mod, ctx_mod_row):
    h, route = _norm_router(x, cx, g, mod3, 3, 4, rw_t, rb, rows_per_mod, ctx_mod_row)
    src_tok, dest, tile_expert, tile_valid = _dispatch_plan(route, MOE_TM)
    n_tiles = tile_expert.shape[0]
    bounds = [n_tiles * k // MOE_RANGES for k in range(MOE_RANGES + 1)]
    ys = None
    for t0, t1 in zip(bounds[:-1], bounds[1:]):
        xs = _sc_gather_rows(h, src_tok[t0 * MOE_TM:t1 * MOE_TM])
        ys = _grouped_mlp(tile_expert, tile_valid, xs, t0, ys, wg, wu, wd, layer)
    yg = _sc_gather_rows(ys, dest)
    route_cols = route.T
    return _combine(x, cx, yg, route_cols, mod3, 5, rows_per_mod, ctx_mod_row)


def _rope_tables(seq):
    rows = seq // GRID_W
    row = jnp.repeat(jnp.arange(rows, dtype=F32), GRID_W)
    col = jnp.tile(jnp.arange(GRID_W, dtype=F32), rows)
    n_freq = HEAD_DIM // 4
    inv_freq = ROPE_BASE ** (-jnp.arange(n_freq, dtype=F32) / n_freq)
    ang = jnp.concatenate([row[:, None] * inv_freq, col[:, None] * inv_freq], axis=-1)
    cos, sin = jnp.cos(ang), jnp.sin(ang)
    return jnp.concatenate([cos, cos], axis=-1), jnp.concatenate([-sin, sin], axis=-1)


def kernel(x, c, ctx, c_ctx, ada_w, ada_b, norm_mix, norm_ffn, ab_w_in, ab_q_gain, ab_k_gain, ab_sink, ab_conv_w, ab_conv_b, ab_gate_a_w, ab_gate_a_b, ab_gate_x_w, ab_gate_x_b, ab_lru_lambda, ab_w_out, gqa_w_in, gqa_q_gain, gqa_k_gain, gqa_w_out, router_w, router_bias, moe_w_gate, moe_w_up, moe_w_down):
    batch, seq, d = x.shape
    ctx_len = ctx.shape[1]
    depth = ada_w.shape[0]
    assert depth == 2 and batch < SUBLANES
    n_lat = batch * seq
    n_ctx = batch * ctx_len
    ctx_row = batch

    xl = x.reshape(n_lat, d)
    xc = ctx.reshape(n_ctx, d)
    cc = jnp.zeros((SUBLANES, d), F32).at[:batch].set(c).at[ctx_row].set(c_ctx)
    mod = _ada(cc, ada_w, ada_b)
    cos2, sin2 = _rope_tables(seq)
    rw_t = router_w.T.astype(BF16)
    rb = router_bias.reshape(N_EXPERTS, 1).astype(F32)
    experts = (moe_w_gate, moe_w_up, moe_w_down)

    mod3 = mod[0].reshape(SUBLANES, 1, 6 * d)
    w_in = ab_w_in[0].astype(BF16)
    proj = _norm_mod_matmul(xl, norm_mix[0], mod3, 0, 1, w_in, seq, 0,
                            A_Q_HEADS, A_KV_HEADS, ab_q_gain[0], ab_k_gain[0], cos2, sin2)
    proj_c = _norm_mod_matmul(xc, norm_mix[0], mod3, 0, 1, w_in, n_ctx, ctx_row,
                              A_Q_HEADS, A_KV_HEADS, ab_q_gain[0], ab_k_gain[0])
    att = _win_attn(ab_sink[0], proj, proj_c, batch, seq, ctx_len)
    att_c = _ctx_attn(ab_sink[0], proj_c, batch, ctx_len)

    lru_w = ab_conv_w.shape[2]
    c0 = (A_Q_HEADS + 2 * A_KV_HEADS) * HEAD_DIM
    y_p, yc_p = _lru(_to_chunked(proj[:, c0:c0 + lru_w], batch),
                     _to_chunked(proj[:, c0 + lru_w:c0 + 2 * lru_w], batch),
                     _to_chunked(proj_c[:, c0:c0 + lru_w], batch),
                     _to_chunked(proj_c[:, c0 + lru_w:c0 + 2 * lru_w], batch),
                     ab_conv_w[0], ab_conv_b[0], ab_gate_a_w[0], ab_gate_a_b[0],
                     ab_gate_x_w[0], ab_gate_x_b[0], ab_lru_lambda[0])
    w_out = ab_w_out[0].astype(BF16)
    xl = _out_proj(att, 0, _from_chunked(y_p), 0, w_out, xl, mod3, 2, seq, 0)
    xc = _out_proj(att_c, 0, _from_chunked(yc_p), 0, w_out, xc, mod3, 2, n_ctx, ctx_row)
    xl, xc = _moe(xl, xc, norm_ffn[0], mod3, rw_t, rb, *experts, 0, seq, ctx_row)

    mod3 = mod[1].reshape(SUBLANES, 1, 6 * d)
    w_in = gqa_w_in[0].astype(BF16)
    cw = C_Q_HEADS * HEAD_DIM
    proj = _norm_mod_matmul(xl, norm_mix[1], mod3, 0, 1, w_in, seq, 0,
                            C_Q_HEADS, C_KV_HEADS, gqa_q_gain[0], gqa_k_gain[0], cos2, sin2)
    proj_c = _norm_mod_matmul(xc, norm_mix[1], mod3, 0, 1, w_in[:, cw:], n_ctx, ctx_row,
                              0, C_KV_HEADS, gqa_q_gain[0], gqa_k_gain[0])
    att = _dense_attn(proj, proj_c, batch, seq, ctx_len)
    xl = _out_proj(att, 0, att, 1, gqa_w_out[0].astype(BF16), xl, mod3, 2, seq, 0)
    xl, _ = _moe(xl, None, norm_ffn[1], mod3, rw_t, rb, *experts, 1, seq, ctx_row)
    return xl.reshape(batch, seq, d)
```

```python
import functools

import jax
import jax.numpy as jnp
from jax import lax
from jax.experimental import pallas as pl
from jax.experimental.pallas import tpu as pltpu
from jax.experimental.pallas import tpu_sc as plsc

F32 = jnp.float32
BF16 = jnp.bfloat16

LANES = 128
SUBLANES = 8
VMEM_LIMIT = 56 * 1024 * 1024

HEAD_DIM = 128
GRID_W = 64
WINDOW = 128
BLOCK = 128
ROPE_BASE = 10000.0
EPS = 1e-6
ATTN_SCALE = HEAD_DIM ** -0.5
A_Q_HEADS, A_KV_HEADS = 8, 2
C_Q_HEADS, C_KV_HEADS = 16, 4
GQA_GROUP = 4
LRU_C = 8.0
CONV_W = 4
CONV_LEFT = 2
N_EXPERTS = 16
N_GROUPS = 4
EXPERTS_PER_GROUP = 4
NEG_BIG = -1e30

LRU_CHUNKS = SUBLANES
LRU_JB = 16
LRU_LANE_BLOCKS = 2
MOE_TM = 256
MOE_RANGES = 1
SC_CORES = 2
SC_WORKERS = 32
SC_GATHER_WINDOW = 32


def _cparams(sem, vmem=VMEM_LIMIT):
    return pltpu.CompilerParams(dimension_semantics=sem, vmem_limit_bytes=vmem)


def _dot(a, b):
    return jnp.dot(a, b, preferred_element_type=F32)


def _dot_nt(a, b):
    return lax.dot_general(a, b, (((1,), (1,)), ((), ())), preferred_element_type=F32)


def _ada_kernel(c_ref, w_ref, b_ref, o_ref):
    c = c_ref[...]
    s = (c * jax.nn.sigmoid(c)).astype(BF16)
    o_ref[0] = _dot(s, w_ref[0].astype(BF16)) + b_ref[0]


def _ada(cc, ada_w, ada_b):
    depth, d, n = ada_w.shape
    tn = 1024
    return pl.pallas_call(
        _ada_kernel,
        out_shape=jax.ShapeDtypeStruct((depth, SUBLANES, n), F32),
        grid=(depth, n // tn),
        in_specs=[pl.BlockSpec((SUBLANES, d), lambda l, j: (0, 0)),
                  pl.BlockSpec((1, d, tn), lambda l, j: (l, 0, j)),
                  pl.BlockSpec((1, 1, tn), lambda l, j: (l, 0, j))],
        out_specs=pl.BlockSpec((1, SUBLANES, tn), lambda l, j: (l, 0, j)),
        compiler_params=_cparams(("arbitrary", "arbitrary")),
        name="ada",
    )(cc, ada_w, ada_b.reshape(depth, 1, n))


def _norm_mod_rows(x_ref, g_ref, sh_ref, sc_ref, dst_ref, tm, rc=128):
    g = g_ref[...]
    sc1 = 1.0 + sc_ref[...]
    sh = sh_ref[...]

    def body(r, carry):
        rows = pl.ds(pl.multiple_of(r * rc, rc), rc)
        xf = x_ref[rows, :]
        ms = jnp.mean(xf * xf, axis=-1, keepdims=True)
        xn = (xf * lax.rsqrt(ms + EPS)) * g
        dst_ref[rows, :] = (xn * sc1 + sh).astype(dst_ref.dtype)
        return carry

    lax.fori_loop(0, tm // rc, body, 0)


def _nm_mm_kernel(*refs, tm, tn, n_q, n_k, rope):
    if rope:
        x_ref, g_ref, sh_ref, sc_ref, w_ref, qg_ref, kg_ref, cos_ref, sin_ref, o_ref, hn_ref = refs
    else:
        x_ref, g_ref, sh_ref, sc_ref, w_ref, qg_ref, kg_ref, o_ref, hn_ref = refs
    _norm_mod_rows(x_ref, g_ref, sh_ref, sc_ref, hn_ref, tm)
    h = hn_ref[...]
    heads_per_chunk = tn // HEAD_DIM
    for j in range(w_ref.shape[1] // tn):
        cols = slice(j * tn, (j + 1) * tn)
        y = _dot(h, w_ref[:, cols])
        parts = []
        for hh in range(heads_per_chunk):
            head = j * heads_per_chunk + hh
            yh = y[:, _head_cols(hh)]
            if head < n_q + n_k:
                gain = qg_ref[...] if head < n_q else kg_ref[...]
                ms = jnp.mean(yh * yh, axis=-1, keepdims=True)
                yh = (yh * lax.rsqrt(ms + EPS)) * gain
                if rope:
                    yh = yh * cos_ref[...] + pltpu.roll(yh, HEAD_DIM // 2, 1) * sin_ref[...]
                if head < n_q:
                    yh = yh * ATTN_SCALE
            parts.append(yh.astype(o_ref.dtype))
        o_ref[:, cols] = jnp.concatenate(parts, axis=1)


def _norm_mod_matmul(x, g, mod3, shift_blk, scale_blk, w, rows_per_mod, mod_base,
                     n_q, n_k, q_gain, k_gain, cos2=None, sin2=None, tm=512, tn=512):
    n, d = x.shape
    nout = w.shape[1]
    tm = min(tm, n)
    tn = min(tn, nout)
    per = rows_per_mod // tm
    rope = cos2 is not None

    def mod_row(i):
        return mod_base + i // per

    head_vec = pl.BlockSpec((1, HEAD_DIM), lambda i: (0, 0))
    in_specs = [pl.BlockSpec((tm, d), lambda i: (i, 0)),
                pl.BlockSpec((1, d), lambda i: (0, 0)),
                pl.BlockSpec((None, 1, d), lambda i: (mod_row(i), 0, shift_blk)),
                pl.BlockSpec((None, 1, d), lambda i: (mod_row(i), 0, scale_blk)),
                pl.BlockSpec((d, nout), lambda i: (0, 0), pipeline_mode=pl.Buffered(1)),
                head_vec, head_vec]
    args = [x, g.reshape(1, d), mod3, mod3, w, q_gain.reshape(1, HEAD_DIM), k_gain.reshape(1, HEAD_DIM)]
    if rope:
        tiles_per_seq = cos2.shape[0] // tm
        table = pl.BlockSpec((tm, HEAD_DIM), lambda i: (i % tiles_per_seq, 0))
        in_specs += [table, table]
        args += [cos2, sin2]
    return pl.pallas_call(
        functools.partial(_nm_mm_kernel, tm=tm, tn=tn, n_q=n_q, n_k=n_k, rope=rope),
        out_shape=jax.ShapeDtypeStruct((n, nout), BF16),
        grid=(n // tm,),
        in_specs=in_specs,
        out_specs=pl.BlockSpec((tm, nout), lambda i: (i, 0)),
        scratch_shapes=[pltpu.VMEM((tm, d), BF16)],
        compiler_params=_cparams(("parallel",)),
        name="norm_mod_matmul",
    )(*args)


def _head_cols(h):
    return slice(h * HEAD_DIM, (h + 1) * HEAD_DIM)


def _stack_group(q_ref, kvh):
    return jnp.concatenate([q_ref[:, _head_cols(kvh * GQA_GROUP + g)] for g in range(GQA_GROUP)], axis=0)


def _sink_col(sink_ref, kvh, rows):
    return jnp.concatenate([jnp.full((rows, 1), sink_ref[kvh * GQA_GROUP + g], F32)
                            for g in range(GQA_GROUP)], axis=0)


def _win_attn_kernel(sink_ref, q_ref, kp_ref, kc_ref, kn_ref, vp_ref, vc_ref, vn_ref,
                     kx_ref, vx_ref, o_ref, *, seq, ctx_len):
    n = pl.program_id(1)
    rows = GQA_GROUP * BLOCK
    nk = 3 * BLOCK + ctx_len
    qi = lax.broadcasted_iota(jnp.int32, (rows, nk), 0) % BLOCK
    kj = lax.broadcasted_iota(jnp.int32, (rows, nk), 1)
    kpos = n * BLOCK - BLOCK + kj
    in_band = (jnp.abs(kj - BLOCK - qi) <= WINDOW) & (kpos >= 0) & (kpos < seq)
    valid = (kj >= 3 * BLOCK) | in_band
    ones = jnp.ones((nk, HEAD_DIM), BF16)
    for kvh in range(A_KV_HEADS):
        cols = _head_cols(kvh)
        q4 = _stack_group(q_ref, kvh)
        ka = jnp.concatenate([kp_ref[:, cols], kc_ref[:, cols], kn_ref[:, cols], kx_ref[:, cols]], axis=0)
        va = jnp.concatenate([vp_ref[:, cols], vc_ref[:, cols], vn_ref[:, cols], vx_ref[:, cols]], axis=0)
        s = jnp.where(valid, _dot_nt(q4, ka), NEG_BIG)
        sk = _sink_col(sink_ref, kvh, BLOCK)
        m = jnp.maximum(jnp.max(s, axis=-1, keepdims=True), sk)
        p = jnp.exp((s - m).astype(BF16))
        acc = _dot(p, jnp.concatenate([va, ones], axis=1))
        o = acc[:, 0:HEAD_DIM] / (acc[:, HEAD_DIM:] + jnp.exp(sk - m))
        for g in range(GQA_GROUP):
            o_ref[:, _head_cols(kvh * GQA_GROUP + g)] = o[g * BLOCK:(g + 1) * BLOCK].astype(o_ref.dtype)


def _win_attn(sink, proj, proj_c, batch, seq, ctx_len):
    nb = seq // BLOCK
    kvw = A_KV_HEADS * HEAD_DIM
    k_blk = A_Q_HEADS * HEAD_DIM // kvw
    v_blk = k_blk + 1

    def prev(b, n):
        return b * nb + jnp.maximum(n - 1, 0)

    def cur(b, n):
        return b * nb + n

    def nxt(b, n):
        return b * nb + jnp.minimum(n + 1, nb - 1)

    return pl.pallas_call(
        functools.partial(_win_attn_kernel, seq=seq, ctx_len=ctx_len),
        out_shape=jax.ShapeDtypeStruct((batch * seq, A_Q_HEADS * HEAD_DIM), BF16),
        grid=(batch, nb),
        in_specs=[pl.BlockSpec(memory_space=pltpu.SMEM),
                  pl.BlockSpec((BLOCK, A_Q_HEADS * HEAD_DIM), lambda b, n: (cur(b, n), 0)),
                  pl.BlockSpec((BLOCK, kvw), lambda b, n: (prev(b, n), k_blk)),
                  pl.BlockSpec((BLOCK, kvw), lambda b, n: (cur(b, n), k_blk)),
                  pl.BlockSpec((BLOCK, kvw), lambda b, n: (nxt(b, n), k_blk)),
                  pl.BlockSpec((BLOCK, kvw), lambda b, n: (prev(b, n), v_blk)),
                  pl.BlockSpec((BLOCK, kvw), lambda b, n: (cur(b, n), v_blk)),
                  pl.BlockSpec((BLOCK, kvw), lambda b, n: (nxt(b, n), v_blk)),
                  pl.BlockSpec((ctx_len, kvw), lambda b, n: (b, k_blk)),
                  pl.BlockSpec((ctx_len, kvw), lambda b, n: (b, v_blk))],
        out_specs=pl.BlockSpec((BLOCK, A_Q_HEADS * HEAD_DIM), lambda b, n: (cur(b, n), 0)),
        compiler_params=_cparams(("parallel", "parallel")),
        name="win_attn",
    )(sink, proj, proj, proj, proj, proj, proj, proj, proj_c, proj_c)


def _ctx_attn_kernel(sink_ref, q_ref, k_ref, v_ref, o_ref, *, ctx_len):
    kvh = pl.program_id(1)
    q4 = jnp.concatenate([q_ref[:, _head_cols(g)] for g in range(GQA_GROUP)], axis=0)
    s = _dot_nt(q4, k_ref[...])
    sk = jnp.concatenate([jnp.full((ctx_len, 1), sink_ref[kvh * GQA_GROUP + g], F32)
                          for g in range(GQA_GROUP)], axis=0)
    m = jnp.maximum(jnp.max(s, axis=-1, keepdims=True), sk)
    p = jnp.exp(s - m)
    den = jnp.sum(p, axis=-1, keepdims=True) + jnp.exp(sk - m)
    o = _dot(p.astype(BF16), v_ref[...]) / den
    for g in range(GQA_GROUP):
        o_ref[:, _head_cols(g)] = o[g * ctx_len:(g + 1) * ctx_len].astype(o_ref.dtype)


def _ctx_attn(sink, proj_c, batch, ctx_len):
    k_blk = A_Q_HEADS
    v_blk = A_Q_HEADS + A_KV_HEADS
    gw = GQA_GROUP * HEAD_DIM
    return pl.pallas_call(
        functools.partial(_ctx_attn_kernel, ctx_len=ctx_len),
        out_shape=jax.ShapeDtypeStruct((batch * ctx_len, A_Q_HEADS * HEAD_DIM), BF16),
        grid=(batch, A_KV_HEADS),
        in_specs=[pl.BlockSpec(memory_space=pltpu.SMEM),
                  pl.BlockSpec((ctx_len, gw), lambda b, h: (b, h)),
                  pl.BlockSpec((ctx_len, HEAD_DIM), lambda b, h: (b, k_blk + h)),
                  pl.BlockSpec((ctx_len, HEAD_DIM), lambda b, h: (b, v_blk + h))],
        out_specs=pl.BlockSpec((ctx_len, gw), lambda b, h: (b, h)),
        compiler_params=_cparams(("parallel", "parallel")),
        name="ctx_attn",
    )(sink, proj_c, proj_c, proj_c)


def _sigmoid(x):
    return 0.5 * jnp.tanh(0.5 * x) + 0.5


def _gelu_tanh(x):
    return 0.5 * x * (1.0 + jnp.tanh(0.7978845608028654 * (x + 0.044715 * (x * x * x))))


def _lru_sequence(x_ref, xg_ref, y_ref, xp_ref, a_ref, b_ref, w, init, rows, jb):
    conv_w, conv_b, wa, ba, wx, bx, c_logsig = w
    width = x_ref.shape[-1]
    sub = lax.broadcasted_iota(jnp.int32, (1, SUBLANES, width), 1)

    def block_diag(ub, wd):
        return jnp.concatenate([_dot(ub[:, _head_cols(n)], wd[n]) for n in range(width // LANES)], axis=1)

    def fill(r, carry):
        rr = pl.ds(pl.multiple_of(r * jb, jb), jb)
        xp_ref[pl.ds(pl.multiple_of(r * jb, jb) + CONV_LEFT, jb)] = x_ref[rr].astype(F32)
        return carry

    lax.fori_loop(0, rows // jb, fill, 0)
    tail = x_ref[rows - CONV_LEFT:rows].astype(F32)
    xp_ref[0:CONV_LEFT] = jnp.where(sub == 0, 0.0, pltpu.roll(tail, 1, 1))
    head = x_ref[0:1].astype(F32)
    xp_ref[rows + CONV_LEFT:rows + CONV_LEFT + 1] = jnp.where(
        sub == SUBLANES - 1, 0.0, pltpu.roll(head, SUBLANES - 1, 1))

    def gates(r, carry):
        j0 = pl.multiple_of(r * jb, jb)
        u = conv_b
        for k in range(CONV_W):
            u = u + conv_w[k] * xp_ref[pl.ds(j0 + k, jb)]
        u2 = u.reshape(jb * SUBLANES, width)
        ub = u2.astype(BF16)
        for d in range(2):
            r_gate = _sigmoid(block_diag(ub, wa[d]) + ba[d])
            i_gate = _sigmoid(block_diag(ub, wx[d]) + bx[d])
            log_a = c_logsig[d] * r_gate
            a = jnp.exp(log_a)
            b = jnp.sqrt(1.0 - a * a) * (i_gate * u2)
            a_ref[d, pl.ds(j0, jb)] = a.reshape(jb, SUBLANES, width)
            b_ref[d, pl.ds(j0, jb)] = b.reshape(jb, SUBLANES, width)
        return carry

    lax.fori_loop(0, rows // jb, gates, 0, unroll=2)

    def scan(j, carry):
        hf, pf, hb, pb = carry
        jr = rows - 1 - j
        af = a_ref[0, j]
        hf = af * hf + b_ref[0, j]
        pf = pf * af
        b_ref[0, j] = hf
        a_ref[0, j] = pf
        ab = a_ref[1, jr]
        hb = ab * hb + b_ref[1, jr]
        pb = pb * ab
        b_ref[1, jr] = hb
        a_ref[1, jr] = pb
        return hf, pf, hb, pb

    z = jnp.zeros((SUBLANES, width), F32)
    o = jnp.ones((SUBLANES, width), F32)
    lax.fori_loop(0, rows, scan, (z, o, z, o), unroll=8)

    hf_last, pf_last = b_ref[0, rows - 1], a_ref[0, rows - 1]
    hb_last, pb_last = b_ref[1, 0], a_ref[1, 0]
    s = init[0]
    carry_f = []
    for c in range(SUBLANES):
        carry_f.append(s)
        s = hf_last[c:c + 1] + pf_last[c:c + 1] * s
    out_f = s
    s = init[1]
    carry_b = [None] * SUBLANES
    for c in reversed(range(SUBLANES)):
        carry_b[c] = s
        s = hb_last[c:c + 1] + pb_last[c:c + 1] * s
    out_b = s
    cf = jnp.concatenate(carry_f, axis=0)
    cb = jnp.concatenate(carry_b, axis=0)

    def emit(r, carry):
        rr = pl.ds(pl.multiple_of(r * jb, jb), jb)
        h = (b_ref[0, rr] + a_ref[0, rr] * cf) + (b_ref[1, rr] + a_ref[1, rr] * cb)
        y_ref[rr] = (h * _gelu_tanh(xg_ref[rr].astype(F32))).astype(y_ref.dtype)
        return carry

    lax.fori_loop(0, rows // jb, emit, 0)
    return out_f, out_b


def _lru_kernel(xr_ref, xg_ref, xrc_ref, xgc_ref, cw_ref, cb_ref, wa_ref, ba_ref, wx_ref, bx_ref,
                lam_ref, y_ref, yc_ref, xp_ref, a_ref, b_ref, *, rows, rows_c):
    c_logsig = [LRU_C * jax.nn.log_sigmoid(lam_ref[d]) for d in range(2)]
    w = ([cw_ref[k] for k in range(CONV_W)], cb_ref[0],
         [wa_ref[d] for d in range(2)], [ba_ref[d] for d in range(2)],
         [wx_ref[d] for d in range(2)], [bx_ref[d] for d in range(2)], c_logsig)
    zero = jnp.zeros((1, xr_ref.shape[-1]), F32)
    sf, sb = _lru_sequence(xrc_ref, xgc_ref, yc_ref, xp_ref, a_ref, b_ref, w, (zero, zero),
                           rows_c, min(LRU_JB, rows_c))
    _lru_sequence(xr_ref, xg_ref, y_ref, xp_ref, a_ref, b_ref, w, (sf, sb), rows, LRU_JB)


def _lru(xr, xg, xrc, xgc, conv_w, conv_b, wa, ba, wx, bx, lam):
    batch, rows, _, width = xr.shape
    rows_c = xrc.shape[1]
    cw = LRU_LANE_BLOCKS * LANES
    seq_spec = pl.BlockSpec((None, rows, SUBLANES, cw), lambda b, n: (b, 0, 0, n))
    ctx_spec = pl.BlockSpec((None, rows_c, SUBLANES, cw), lambda b, n: (b, 0, 0, n))
    vec2 = pl.BlockSpec((2, 1, cw), lambda b, n: (0, 0, n))
    mat2 = pl.BlockSpec((2, LRU_LANE_BLOCKS, LANES, LANES), lambda b, n: (0, n, 0, 0))
    return pl.pallas_call(
        functools.partial(_lru_kernel, rows=rows, rows_c=rows_c),
        out_shape=(jax.ShapeDtypeStruct(xr.shape, BF16), jax.ShapeDtypeStruct(xrc.shape, BF16)),
        grid=(batch, width // cw),
        in_specs=[seq_spec, seq_spec, ctx_spec, ctx_spec,
                  pl.BlockSpec((CONV_W, 1, cw), lambda b, n: (0, 0, n)),
                  pl.BlockSpec((1, 1, cw), lambda b, n: (0, 0, n)),
                  mat2, vec2, mat2, vec2, vec2],
        out_specs=(seq_spec, ctx_spec),
        scratch_shapes=[pltpu.VMEM((rows + CONV_W - 1, SUBLANES, cw), F32),
                        pltpu.VMEM((2, rows, SUBLANES, cw), F32),
                        pltpu.VMEM((2, rows, SUBLANES, cw), F32)],
        compiler_params=_cparams(("parallel", "parallel")),
        name="rglru",
    )(xr, xg, xrc, xgc, conv_w.reshape(CONV_W, 1, width), conv_b.reshape(1, 1, width),
      wa.astype(BF16), ba.reshape(2, 1, width), wx.astype(BF16), bx.reshape(2, 1, width),
      lam.reshape(2, 1, width))


def _to_chunked(a, batch):
    t = a.shape[0] // batch
    return a.reshape(batch, LRU_CHUNKS, t // LRU_CHUNKS, a.shape[1]).transpose(0, 2, 1, 3)


def _from_chunked(a):
    b, r, c, w = a.shape
    return a.transpose(0, 2, 1, 3).reshape(b * r * c, w)


def _out_proj_kernel(a1_ref, a2_ref, w1_ref, w2_ref, x_ref, g_ref, o_ref):
    y = _dot(a1_ref[...], w1_ref[...]) + _dot(a2_ref[...], w2_ref[...])
    o_ref[...] = x_ref[...] + g_ref[...] * y


def _out_proj(a1, a1_blk, a2, a2_blk, w, x, mod3, gate_blk, rows_per_mod, mod_base, tm=2048, tn=512):
    n, d = x.shape
    kh = w.shape[0] // 2
    tm = min(tm, n)
    per = rows_per_mod // tm
    gpb = d // tn
    return pl.pallas_call(
        _out_proj_kernel,
        out_shape=jax.ShapeDtypeStruct((n, d), F32),
        grid=(n // tm, d // tn),
        in_specs=[pl.BlockSpec((tm, kh), lambda i, j: (i, a1_blk)),
                  pl.BlockSpec((tm, kh), lambda i, j: (i, a2_blk)),
                  pl.BlockSpec((kh, tn), lambda i, j: (0, j)),
                  pl.BlockSpec((kh, tn), lambda i, j: (1, j)),
                  pl.BlockSpec((tm, tn), lambda i, j: (i, j)),
                  pl.BlockSpec((None, 1, tn), lambda i, j: (mod_base + i // per, 0, gate_blk * gpb + j))],
        out_specs=pl.BlockSpec((tm, tn), lambda i, j: (i, j)),
        compiler_params=_cparams(("parallel", "parallel")),
        name="out_proj",
    )(a1, a2, w, w, x, mod3)


def _lane_block_max(s):
    mm = s[:, 0:LANES]
    for t in range(1, s.shape[1] // LANES):
        mm = jnp.maximum(mm, s[:, t * LANES:(t + 1) * LANES])
    return mm


def _exp_blocks(s, mrep):
    return jnp.concatenate(
        [jnp.exp((s[:, t * LANES:(t + 1) * LANES] - mrep).astype(BF16)) for t in range(s.shape[1] // LANES)],
        axis=1)


def _copy_key_rows(dst_ref, c, kc, lat_ref, ctx_ref, cols):
    seq = lat_ref.shape[0]
    lo, hi = c * kc, (c + 1) * kc
    if lo < seq:
        n = min(hi, seq) - lo
        dst_ref[c, 0:n, cols] = lat_ref[lo:lo + n, :]
    if hi > seq:
        start = max(lo, seq)
        dst_ref[c, start - lo:kc, cols] = ctx_ref[start - seq:hi - seq, :]


def _dense_attn_kernel(q_ref, k_ref, v_ref, kx_ref, vx_ref, o_ref,
                       s_ref, m_ref, acc_ref, ka_ref, va_ref, *, tq, n_chunks, kc):
    @pl.when(pl.program_id(2) == 0)
    def _():
        for c in range(n_chunks):
            _copy_key_rows(ka_ref, c, kc, k_ref, kx_ref, slice(0, HEAD_DIM))
            _copy_key_rows(va_ref, c, kc, v_ref, vx_ref, slice(0, HEAD_DIM))
            va_ref[c, :, HEAD_DIM:] = jnp.ones((kc, HEAD_DIM), BF16)

    q4 = jnp.concatenate([q_ref[:, _head_cols(g)] for g in range(GQA_GROUP)], axis=0)
    m_ref[...] = jnp.full(m_ref.shape, NEG_BIG, F32)

    def sweep1(c, carry):
        s = _dot_nt(q4, ka_ref[c])
        s_ref[c] = s
        m_ref[...] = jnp.maximum(m_ref[...], _lane_block_max(s))
        return carry

    lax.fori_loop(0, n_chunks, sweep1, 0)
    m_ref[...] = jnp.broadcast_to(jnp.max(m_ref[...], axis=-1, keepdims=True), m_ref.shape)
    acc_ref[...] = jnp.zeros(acc_ref.shape, F32)

    def sweep2(c, carry):
        acc_ref[...] += _dot(_exp_blocks(s_ref[c], m_ref[...]), va_ref[c])
        return carry

    lax.fori_loop(0, n_chunks, sweep2, 0)
    o = acc_ref[:, 0:HEAD_DIM] / acc_ref[:, HEAD_DIM:]
    for g in range(GQA_GROUP):
        o_ref[:, _head_cols(g)] = o[g * tq:(g + 1) * tq].astype(o_ref.dtype)


def _dense_attn(proj, proj_c, batch, seq, ctx_len, tq=256, n_chunks=2):
    gw = GQA_GROUP * HEAD_DIM
    nq = seq // tq
    rows = GQA_GROUP * tq
    kc = (seq + ctx_len) // n_chunks
    assert kc * n_chunks == seq + ctx_len and kc % LANES == 0
    k_blk = C_Q_HEADS
    v_blk = C_Q_HEADS + C_KV_HEADS
    vx_blk = C_KV_HEADS
    return pl.pallas_call(
        functools.partial(_dense_attn_kernel, tq=tq, n_chunks=n_chunks, kc=kc),
        out_shape=jax.ShapeDtypeStruct((batch * seq, C_Q_HEADS * HEAD_DIM), BF16),
        grid=(batch, C_KV_HEADS, nq),
        in_specs=[pl.BlockSpec((tq, gw), lambda b, h, i: (b * nq + i, h)),
                  pl.BlockSpec((seq, HEAD_DIM), lambda b, h, i: (b, k_blk + h)),
                  pl.BlockSpec((seq, HEAD_DIM), lambda b, h, i: (b, v_blk + h)),
                  pl.BlockSpec((ctx_len, HEAD_DIM), lambda b, h, i: (b, h)),
                  pl.BlockSpec((ctx_len, HEAD_DIM), lambda b, h, i: (b, vx_blk + h))],
        out_specs=pl.BlockSpec((tq, gw), lambda b, h, i: (b * nq + i, h)),
        scratch_shapes=[pltpu.VMEM((n_chunks, rows, kc), F32),
                        pltpu.VMEM((rows, LANES), F32),
                        pltpu.VMEM((rows, 2 * HEAD_DIM), F32),
                        pltpu.VMEM((n_chunks, kc, HEAD_DIM), BF16),
                        pltpu.VMEM((n_chunks, kc, 2 * HEAD_DIM), BF16)],
        compiler_params=_cparams(("parallel", "parallel", "arbitrary")),
        name="dense_attn",
    )(proj, proj, proj, proj_c, proj_c)


def _router_rows(biased, scores):
    v = [biased[e:e + 1, :] for e in range(N_EXPERTS)]
    s = [scores[e:e + 1, :] for e in range(N_EXPERTS)]

    def top2_sum(vals):
        best = vals[0] + vals[1]
        for i in range(len(vals)):
            for j in range(i + 1, len(vals)):
                if (i, j) != (0, 1):
                    best = jnp.maximum(best, vals[i] + vals[j])
        return best

    gsum = [top2_sum(v[g * EXPERTS_PER_GROUP:(g + 1) * EXPERTS_PER_GROUP]) for g in range(N_GROUPS)]
    sel = jnp.zeros_like(gsum[0], dtype=jnp.int32)
    best = gsum[0]
    for g in range(1, N_GROUPS):
        take = gsum[g] > best
        sel = jnp.where(take, g, sel)
        best = jnp.where(take, gsum[g], best)

    def pick_group(rows, i):
        out = rows[i]
        for g in range(1, N_GROUPS):
            out = jnp.where(sel == g, rows[g * EXPERTS_PER_GROUP + i], out)
        return out

    cand = [pick_group(v, i) for i in range(EXPERTS_PER_GROUP)]
    cand_s = [pick_group(s, i) for i in range(EXPERTS_PER_GROUP)]
    i1 = jnp.zeros_like(sel)
    b1 = cand[0]
    for i in range(1, EXPERTS_PER_GROUP):
        take = cand[i] > b1
        i1 = jnp.where(take, i, i1)
        b1 = jnp.where(take, cand[i], b1)
    i2 = jnp.full_like(sel, -1)
    b2 = jnp.full_like(b1, -jnp.inf)
    for i in range(EXPERTS_PER_GROUP):
        take = (i1 != i) & ((cand[i] > b2) | (i2 < 0))
        i2 = jnp.where(take, i, i2)
        b2 = jnp.where(take, cand[i], b2)

    def pick_idx(rows, idx):
        out = rows[0]
        for i in range(1, EXPERTS_PER_GROUP):
            out = jnp.where(idx == i, rows[i], out)
        return out

    s0 = pick_idx(cand_s, i1)
    s1 = pick_idx(cand_s, i2)
    tot = s0 + s1
    e0 = (sel * EXPERTS_PER_GROUP + i1).astype(F32)
    e1 = (sel * EXPERTS_PER_GROUP + i2).astype(F32)
    return e0, e1, s0 / tot, s1 / tot


def _norm_router_kernel(*refs, tm, n_lat_tiles, has_ctx):
    if has_ctx:
        x_ref, xc_ref, g_ref, sh_ref, sc_ref, rw_ref, rb_ref, hp_ref, r_ref, h_ref = refs
        i = pl.program_id(0)

        @pl.when(i < n_lat_tiles)
        def _():
            _norm_mod_rows(x_ref, g_ref, sh_ref, sc_ref, h_ref, tm)

        @pl.when(i >= n_lat_tiles)
        def _():
            _norm_mod_rows(xc_ref, g_ref, sh_ref, sc_ref, h_ref, tm)
    else:
        x_ref, g_ref, sh_ref, sc_ref, rw_ref, rb_ref, hp_ref, r_ref, h_ref = refs
        _norm_mod_rows(x_ref, g_ref, sh_ref, sc_ref, h_ref, tm)

    def pack(r, carry):
        rows = pl.ds(pl.multiple_of(r * 128, 128), 128)
        hp_ref[rows, :] = _pack_halves(h_ref[rows, :])
        return carry

    lax.fori_loop(0, tm // 128, pack, 0)
    logits = _dot_nt(rw_ref[...], h_ref[...])
    scores = jax.nn.sigmoid(logits)
    e0, e1, w0, w1 = _router_rows(scores + rb_ref[...], scores)
    zero = jnp.zeros_like(w0)
    r_ref[...] = jnp.concatenate([e0, e1, w0, w1, zero, zero, zero, zero], axis=0)


def _pack_halves(v):
    c = v.shape[1] // 2
    lo = lax.bitcast_convert_type(v[:, :c].astype(BF16).astype(F32), jnp.uint32)
    hi = lax.bitcast_convert_type(v[:, c:].astype(BF16).astype(F32), jnp.uint32)
    return (lo >> 16) | (hi & jnp.uint32(0xFFFF0000))


def _unpack_halves(p):
    lo = lax.bitcast_convert_type(p << 16, F32)
    hi = lax.bitcast_convert_type(p & jnp.uint32(0xFFFF0000), F32)
    return lo, hi


def _sc_gather_rows(table, idx):
    n, w = idx.shape[0], table.shape[1]
    win = SC_GATHER_WINDOW
    per = n // (win * SC_WORKERS)
    assert per * win * SC_WORKERS == n and per >= 1
    mesh = plsc.VectorSubcoreMesh(core_axis_name="core", subcore_axis_name="subcore")

    @functools.partial(
        pl.kernel, out_type=jax.ShapeDtypeStruct((n, w), table.dtype), mesh=mesh, name="sc_gather_rows",
        scratch_types=[pltpu.VMEM((per * win,), jnp.int32),
                       pltpu.VMEM((win, w), table.dtype), pltpu.VMEM((win, w), table.dtype),
                       pltpu.SemaphoreType.DMA, pltpu.SemaphoreType.DMA,
                       pltpu.SemaphoreType.DMA, pltpu.SemaphoreType.DMA])
    def gather(x_hbm, i_hbm, o_hbm, i_v, buf0, buf1, gsem0, gsem1, wsem0, wsem1):
        wid = lax.axis_index("subcore") * SC_CORES + lax.axis_index("core")
        base = wid * (per * win)
        bufs, gsems, wsems = (buf0, buf1), (gsem0, gsem1), (wsem0, wsem1)
        pltpu.sync_copy(i_hbm.at[pl.ds(base, per * win)], i_v)

        def gather_copy(t, b):
            return pltpu.make_async_copy(x_hbm.at[i_v.at[pl.ds(t * win, win)]], bufs[b], gsems[b])

        def write_copy(t, b):
            return pltpu.make_async_copy(bufs[b], o_hbm.at[pl.ds(base + t * win, win)], wsems[b])

        def step(t, b):
            gather_copy(t, b).wait()

            @pl.when(t > 0)
            def _():
                write_copy(t - 1, 1 - b).wait()

            @pl.when(t + 1 < per)
            def _():
                gather_copy(t + 1, 1 - b).start()

            write_copy(t, b).start()

        gather_copy(0, 0).start()

        @pl.loop(0, per // 2)
        def _(p):
            step(2 * p, 0)
            step(2 * p + 1, 1)

        if per % 2:
            step(per - 1, 0)
        write_copy(per - 1, (per - 1) % 2).wait()

    return gather(table, idx)


def _lat_ctx_maps(n_lat, per, ctx_mod_row):
    def mod_row(i):
        return jnp.where(i < n_lat, i // per, ctx_mod_row)

    def lat(i):
        return (jnp.minimum(i, n_lat - 1), 0)

    def ctx(i):
        return (jnp.maximum(i - n_lat, 0), 0)

    return mod_row, lat, ctx


def _norm_router(x, cx, g, mod3, shift_blk, scale_blk, rw_t, rb, rows_per_mod, ctx_mod_row, tm=512):
    n, d = x.shape
    n_lat = n // tm
    has_ctx = cx is not None
    ntot = n_lat + (cx.shape[0] // tm if has_ctx else 0)
    mod_row, lat, ctx = _lat_ctx_maps(n_lat, rows_per_mod // tm, ctx_mod_row)
    row_specs = [pl.BlockSpec((tm, d), lat)] + ([pl.BlockSpec((tm, d), ctx)] if has_ctx else [])
    row_args = [x] + ([cx] if has_ctx else [])
    return pl.pallas_call(
        functools.partial(_norm_router_kernel, tm=tm, n_lat_tiles=n_lat, has_ctx=has_ctx),
        out_shape=(jax.ShapeDtypeStruct((ntot * tm, d // 2), jnp.uint32),
                   jax.ShapeDtypeStruct((SUBLANES, ntot * tm), F32)),
        grid=(ntot,),
        in_specs=row_specs + [
            pl.BlockSpec((1, d), lambda i: (0, 0)),
            pl.BlockSpec((None, 1, d), lambda i: (mod_row(i), 0, shift_blk)),
            pl.BlockSpec((None, 1, d), lambda i: (mod_row(i), 0, scale_blk)),
            pl.BlockSpec((N_EXPERTS, d), lambda i: (0, 0)),
            pl.BlockSpec((N_EXPERTS, 1), lambda i: (0, 0))],
        out_specs=(pl.BlockSpec((tm, d // 2), lambda i: (i, 0)),
                   pl.BlockSpec((SUBLANES, tm), lambda i: (0, i))),
        scratch_shapes=[pltpu.VMEM((tm, d), BF16)],
        compiler_params=_cparams(("arbitrary",)),
        name="norm_router",
    )(*row_args, g.reshape(1, d), mod3, mod3, rw_t, rb)


def _cast_rows(src_ref, dst_ref, rb=256):
    def body(r, carry):
        rows = pl.ds(pl.multiple_of(r * rb, rb), rb)
        dst_ref[rows, :] = src_ref[0, rows, :].astype(dst_ref.dtype)
        return carry

    lax.fori_loop(0, dst_ref.shape[0] // rb, body, 0)


def _tile_state(te_ref, tv_ref, tile0):
    i = pl.program_id(0)
    t = tile0 + i
    live = tv_ref[t] > 0
    new_expert = (i == 0) | (te_ref[t] != te_ref[jnp.maximum(t - 1, 0)])
    return live, new_expert


def _gmm_kernel(te_ref, tv_ref, xs_ref, wg_ref, wu_ref, wd_ref, *rest, tile0):
    o_ref, wg_b, wu_b, wd_b = rest[-4:]
    live, new_expert = _tile_state(te_ref, tv_ref, tile0)

    @pl.when(live & new_expert)
    def _():
        _cast_rows(wg_ref, wg_b)
        _cast_rows(wu_ref, wu_b)
        _cast_rows(wd_ref, wd_b)

    @pl.when(live)
    def _():
        lo, hi = _unpack_halves(xs_ref[...])
        x = jnp.concatenate([lo.astype(BF16), hi.astype(BF16)], axis=1)
        gate = _dot(x, wg_b[...])
        up = _dot(x, wu_b[...])
        h1 = ((gate * jax.nn.sigmoid(gate)) * up).astype(BF16)
        o_ref[...] = _pack_halves(_dot(h1, wd_b[...]))

    @pl.when(jnp.logical_not(live))
    def _():
        o_ref[...] = jnp.zeros(o_ref.shape, o_ref.dtype)


def _grouped_mlp(tile_expert, tile_valid, xs_part, tile0, ys_prev, wg, wu, wd, layer, tm=MOE_TM):
    dpk = xs_part.shape[1]
    d, dff = wg.shape[2], wg.shape[3]
    n_tiles = tile_expert.shape[0]

    def expert_block(rows, cols):
        return pl.BlockSpec((None, 1, rows, cols), lambda i, te, tv: (layer, te[tile0 + i], 0, 0),
                            pipeline_mode=pl.Buffered(1))

    in_specs = [pl.BlockSpec((tm, dpk), lambda i, te, tv: (i, 0)),
                expert_block(d, dff), expert_block(d, dff), expert_block(dff, d)]
    args = [tile_expert, tile_valid, xs_part, wg, wu, wd]
    aliases = {}
    if ys_prev is not None:
        in_specs.append(pl.BlockSpec(memory_space=pl.ANY))
        args.append(ys_prev)
        aliases = {len(args) - 1: 0}
    grid_spec = pltpu.PrefetchScalarGridSpec(
        num_scalar_prefetch=2,
        grid=(xs_part.shape[0] // tm,),
        in_specs=in_specs,
        out_specs=pl.BlockSpec((tm, dpk), lambda i, te, tv: (tile0 + i, 0)),
        scratch_shapes=[pltpu.VMEM((d, dff), BF16), pltpu.VMEM((d, dff), BF16), pltpu.VMEM((dff, d), BF16)],
    )
    return pl.pallas_call(
        functools.partial(_gmm_kernel, tile0=tile0),
        out_shape=jax.ShapeDtypeStruct((n_tiles * tm, dpk), jnp.uint32),
        grid_spec=grid_spec,
        input_output_aliases=aliases,
        compiler_params=_cparams(("arbitrary",)),
        name="grouped_mlp",
    )(*args)


def _combine_kernel(*refs, n_lat_tiles, has_ctx):
    if has_ctx:
        x_ref, xc_ref, y0_ref, y1_ref, r_ref, g_ref, o_ref, oc_ref = refs
    else:
        x_ref, y0_ref, y1_ref, r_ref, g_ref, o_ref = refs
    w0 = r_ref[:, 2:3]
    w1 = r_ref[:, 3:4]
    lo0, hi0 = _unpack_halves(y0_ref[...])
    lo1, hi1 = _unpack_halves(y1_ref[...])
    f = g_ref[...] * jnp.concatenate([w0 * lo0 + w1 * lo1, w0 * hi0 + w1 * hi1], axis=1)
    if not has_ctx:
        o_ref[...] = x_ref[...] + f
        return
    i = pl.program_id(0)

    @pl.when(i < n_lat_tiles)
    def _():
        o_ref[...] = x_ref[...] + f

    @pl.when(i >= n_lat_tiles)
    def _():
        oc_ref[...] = xc_ref[...] + f


def _combine(x, cx, yg, route_cols, mod3, gate_blk, rows_per_mod, ctx_mod_row, tm=512):
    n, d = x.shape
    n_lat = n // tm
    has_ctx = cx is not None
    ntot = n_lat + (cx.shape[0] // tm if has_ctx else 0)
    mod_row, lat, ctx = _lat_ctx_maps(n_lat, rows_per_mod // tm, ctx_mod_row)
    row_specs = [pl.BlockSpec((tm, d), lat)] + ([pl.BlockSpec((tm, d), ctx)] if has_ctx else [])
    row_args = [x] + ([cx] if has_ctx else [])
    out_shape = [jax.ShapeDtypeStruct(x.shape, F32)] + ([jax.ShapeDtypeStruct(cx.shape, F32)] if has_ctx else [])
    out = pl.pallas_call(
        functools.partial(_combine_kernel, n_lat_tiles=n_lat, has_ctx=has_ctx),
        out_shape=tuple(out_shape),
        grid=(ntot,),
        in_specs=row_specs + [
            pl.BlockSpec((tm, d // 2), lambda i: (i, 0)),
            pl.BlockSpec((tm, d // 2), lambda i: (ntot + i, 0)),
            pl.BlockSpec((tm, SUBLANES), lambda i: (i, 0)),
            pl.BlockSpec((None, 1, d), lambda i: (mod_row(i), 0, gate_blk))],
        out_specs=tuple(row_specs),
        compiler_params=_cparams(("arbitrary",)),
        name="moe_combine",
    )(*row_args, yg, yg, route_cols, mod3)
    return out if has_ctx else (out[0], None)


def _dispatch_plan(route, tm):
    n = route.shape[1]
    e_flat = jnp.concatenate([route[0], route[1]]).astype(jnp.int32)
    n_assign = 2 * n
    n_tiles = n_assign // tm + N_EXPERTS
    experts = jnp.arange(N_EXPERTS, dtype=jnp.int32)[:, None]
    onehot = (experts == e_flat[None, :]).astype(jnp.int32)
    csum = jnp.cumsum(onehot, axis=1)
    counts = csum[:, -1]
    padded = ((counts + tm - 1) // tm) * tm
    ends_p = jnp.cumsum(padded)
    starts_p = ends_p - padded
    starts_c = jnp.cumsum(counts) - counts
    dest = jnp.sum(onehot * (csum - 1 + starts_p[:, None]), axis=0)
    order = jnp.argsort(e_flat, stable=True).astype(jnp.int32)
    p = jnp.arange(n_tiles * tm, dtype=jnp.int32)[None, :]
    owner = ((p >= starts_p[:, None]) & (p < ends_p[:, None])).astype(jnp.int32)
    within = jnp.sum(owner * (p - starts_p[:, None]), axis=0)
    live = jnp.sum(owner * (p - starts_p[:, None] < counts[:, None]), axis=0) > 0
    compact = jnp.sum(owner * starts_c[:, None], axis=0) + within
    src_tok = jnp.where(live, order[jnp.clip(compact, 0, n_assign - 1)] % n, 0)
    tile_start = jnp.arange(n_tiles, dtype=jnp.int32) * tm
    tile_valid = (tile_start < ends_p[-1]).astype(jnp.int32)
    last_tile = jnp.maximum(ends_p[-1] // tm - 1, 0) * tm
    tile_expert = jnp.searchsorted(ends_p, jnp.minimum(tile_start, last_tile), side="right").astype(jnp.int32)
    tile_expert = jnp.minimum(tile_expert, N_EXPERTS - 1)
    return src_tok, dest, tile_expert, tile_valid


def _moe(x, cx, g, mod3, rw_t, rb, wg, wu, wd, layer, rows_per_mod, ctx_mod_row):
    h, route = _norm_router(x, cx, g, mod3, 3, 4, rw_t, rb, rows_per_mod, ctx_mod_row)
    src_tok, dest, tile_expert, tile_valid = _dispatch_plan(route, MOE_TM)
    n_tiles = tile_expert.shape[0]
    bounds = [n_tiles * k // MOE_RANGES for k in range(MOE_RANGES + 1)]
    ys = None
    for t0, t1 in zip(bounds[:-1], bounds[1:]):
        xs = _sc_gather_rows(h, src_tok[t0 * MOE_TM:t1 * MOE_TM])
        ys = _grouped_mlp(tile_expert, tile_valid, xs, t0, ys, wg, wu, wd, layer)
    yg = _sc_gather_rows(ys, dest)
    route_cols = route.T
    return _combine(x, cx, yg, route_cols, mod3, 5, rows_per_mod, ctx_mod_row)


def _rope_tables(seq):
    rows = seq // GRID_W
    row = jnp.repeat(jnp.arange(rows, dtype=F32), GRID_W)
    col = jnp.tile(jnp.arange(GRID_W, dtype=F32), rows)
    n_freq = HEAD_DIM // 4
    inv_freq = ROPE_BASE ** (-jnp.arange(n_freq, dtype=F32) / n_freq)
    ang = jnp.concatenate([row[:, None] * inv_freq, col[:, None] * inv_freq], axis=-1)
    cos, sin = jnp.cos(ang), jnp.sin(ang)
    return jnp.concatenate([cos, cos], axis=-1), jnp.concatenate([-sin, sin], axis=-1)


def kernel(x, c, ctx, c_ctx, ada_w, ada_b, norm_mix, norm_ffn, ab_w_in, ab_q_gain, ab_k_gain, ab_sink, ab_conv_w, ab_conv_b, ab_gate_a_w, ab_gate_a_b, ab_gate_x_w, ab_gate_x_b, ab_lru_lambda, ab_w_out, gqa_w_in, gqa_q_gain, gqa_k_gain, gqa_w_out, router_w, router_bias, moe_w_gate, moe_w_up, moe_w_down):
    batch, seq, d = x.shape
    ctx_len = ctx.shape[1]
    depth = ada_w.shape[0]
    assert depth == 2 and batch < SUBLANES
    n_lat = batch * seq
    n_ctx = batch * ctx_len
    ctx_row = batch

    xl = x.reshape(n_lat, d)
    xc = ctx.reshape(n_ctx, d)
    cc = jnp.zeros((SUBLANES, d), F32).at[:batch].set(c).at[ctx_row].set(c_ctx)
    mod = _ada(cc, ada_w, ada_b)
    cos2, sin2 = _rope_tables(seq)
    rw_t = router_w.T.astype(BF16)
    rb = router_bias.reshape(N_EXPERTS, 1).astype(F32)
    experts = (moe_w_gate, moe_w_up, moe_w_down)

    mod3 = mod[0].reshape(SUBLANES, 1, 6 * d)
    w_in = ab_w_in[0].astype(BF16)
    proj = _norm_mod_matmul(xl, norm_mix[0], mod3, 0, 1, w_in, seq, 0,
                            A_Q_HEADS, A_KV_HEADS, ab_q_gain[0], ab_k_gain[0], cos2, sin2)
    proj_c = _norm_mod_matmul(xc, norm_mix[0], mod3, 0, 1, w_in, n_ctx, ctx_row,
                              A_Q_HEADS, A_KV_HEADS, ab_q_gain[0], ab_k_gain[0])
    att = _win_attn(ab_sink[0], proj, proj_c, batch, seq, ctx_len)
    att_c = _ctx_attn(ab_sink[0], proj_c, batch, ctx_len)

    lru_w = ab_conv_w.shape[2]
    c0 = (A_Q_HEADS + 2 * A_KV_HEADS) * HEAD_DIM
    y_p, yc_p = _lru(_to_chunked(proj[:, c0:c0 + lru_w], batch),
                     _to_chunked(proj[:, c0 + lru_w:c0 + 2 * lru_w], batch),
                     _to_chunked(proj_c[:, c0:c0 + lru_w], batch),
                     _to_chunked(proj_c[:, c0 + lru_w:c0 + 2 * lru_w], batch),
                     ab_conv_w[0], ab_conv_b[0], ab_gate_a_w[0], ab_gate_a_b[0],
                     ab_gate_x_w[0], ab_gate_x_b[0], ab_lru_lambda[0])
    w_out = ab_w_out[0].astype(BF16)
    xl = _out_proj(att, 0, _from_chunked(y_p), 0, w_out, xl, mod3, 2, seq, 0)
    xc = _out_proj(att_c, 0, _from_chunked(yc_p), 0, w_out, xc, mod3, 2, n_ctx, ctx_row)
    xl, xc = _moe(xl, xc, norm_ffn[0], mod3, rw_t, rb, *experts, 0, seq, ctx_row)

    mod3 = mod[1].reshape(SUBLANES, 1, 6 * d)
    w_in = gqa_w_in[0].astype(BF16)
    cw = C_Q_HEADS * HEAD_DIM
    proj = _norm_mod_matmul(xl, norm_mix[1], mod3, 0, 1, w_in, seq, 0,
                            C_Q_HEADS, C_KV_HEADS, gqa_q_gain[0], gqa_k_gain[0], cos2, sin2)
    proj_c = _norm_mod_matmul(xc, norm_mix[1], mod3, 0, 1, w_in[:, cw:], n_ctx, ctx_row,
                              0, C_KV_HEADS, gqa_q_gain[0], gqa_k_gain[0])
    att = _dense_attn(proj, proj_c, batch, seq, ctx_len)
    xl = _out_proj(att, 0, att, 1, gqa_w_out[0].astype(BF16), xl, mod3, 2, seq, 0)
    xl, _ = _moe(xl, None, norm_ffn[1], mod3, rw_t, rb, *experts, 1, seq, ctx_row)
    return xl.reshape(batch, seq, d)
```

```python
import functools

import jax
import jax.numpy as jnp
from jax import lax
from jax.experimental import pallas as pl
from jax.experimental.pallas import tpu as pltpu
from jax.experimental.pallas import tpu_sc as plsc

F32 = jnp.float32
BF16 = jnp.bfloat16

LANES = 128
SUBLANES = 8
VMEM_LIMIT = 56 * 1024 * 1024

HEAD_DIM = 128
GRID_W = 64
WINDOW = 128
BLOCK = 128
ROPE_BASE = 10000.0
EPS = 1e-6
ATTN_SCALE = HEAD_DIM ** -0.5
A_Q_HEADS, A_KV_HEADS = 8, 2
C_Q_HEADS, C_KV_HEADS = 16, 4
GQA_GROUP = 4
LRU_C = 8.0
CONV_W = 4
CONV_LEFT = 2
N_EXPERTS = 16
N_GROUPS = 4
EXPERTS_PER_GROUP = 4
NEG_BIG = -1e30

LRU_CHUNKS = SUBLANES
LRU_JB = 16
LRU_LANE_BLOCKS = 2
MOE_TM = 256
MOE_RANGES = 1
SC_CORES = 2
SC_WORKERS = 32
SC_GATHER_WINDOW = 32
SC_GATHER_BUFFERS = 3
SC_GATHER_SUB = 8


def _cparams(sem, vmem=VMEM_LIMIT):
    return pltpu.CompilerParams(dimension_semantics=sem, vmem_limit_bytes=vmem)


def _dot(a, b):
    return jnp.dot(a, b, preferred_element_type=F32)


def _dot_nt(a, b):
    return lax.dot_general(a, b, (((1,), (1,)), ((), ())), preferred_element_type=F32)


def _ada_kernel(c_ref, w_ref, b_ref, o_ref):
    c = c_ref[...]
    s = (c * jax.nn.sigmoid(c)).astype(BF16)
    o_ref[0] = _dot(s, w_ref[0].astype(BF16)) + b_ref[0]


def _ada(cc, ada_w, ada_b):
    depth, d, n = ada_w.shape
    tn = 1024
    return pl.pallas_call(
        _ada_kernel,
        out_shape=jax.ShapeDtypeStruct((depth, SUBLANES, n), F32),
        grid=(depth, n // tn),
        in_specs=[pl.BlockSpec((SUBLANES, d), lambda l, j: (0, 0)),
                  pl.BlockSpec((1, d, tn), lambda l, j: (l, 0, j)),
                  pl.BlockSpec((1, 1, tn), lambda l, j: (l, 0, j))],
        out_specs=pl.BlockSpec((1, SUBLANES, tn), lambda l, j: (l, 0, j)),
        compiler_params=_cparams(("arbitrary", "arbitrary")),
        name="ada",
    )(cc, ada_w, ada_b.reshape(depth, 1, n))


def _norm_mod_rows(x_ref, g_ref, sh_ref, sc_ref, dst_ref, tm, rc=128):
    g = g_ref[...]
    sc1 = 1.0 + sc_ref[...]
    sh = sh_ref[...]

    def body(r, carry):
        rows = pl.ds(pl.multiple_of(r * rc, rc), rc)
        xf = x_ref[rows, :]
        ms = jnp.mean(xf * xf, axis=-1, keepdims=True)
        xn = (xf * lax.rsqrt(ms + EPS)) * g
        dst_ref[rows, :] = (xn * sc1 + sh).astype(dst_ref.dtype)
        return carry

    lax.fori_loop(0, tm // rc, body, 0)


def _nm_mm_kernel(*refs, tm, tn, n_q, n_k, rope):
    if rope:
        x_ref, g_ref, sh_ref, sc_ref, w_ref, qg_ref, kg_ref, cos_ref, sin_ref, o_ref, hn_ref = refs
    else:
        x_ref, g_ref, sh_ref, sc_ref, w_ref, qg_ref, kg_ref, o_ref, hn_ref = refs
    _norm_mod_rows(x_ref, g_ref, sh_ref, sc_ref, hn_ref, tm)
    h = hn_ref[...]
    heads_per_chunk = tn // HEAD_DIM
    for j in range(w_ref.shape[1] // tn):
        cols = slice(j * tn, (j + 1) * tn)
        y = _dot(h, w_ref[:, cols])
        parts = []
        for hh in range(heads_per_chunk):
            head = j * heads_per_chunk + hh
            yh = y[:, _head_cols(hh)]
            if head < n_q + n_k:
                gain = qg_ref[...] if head < n_q else kg_ref[...]
                ms = jnp.mean(yh * yh, axis=-1, keepdims=True)
                yh = (yh * lax.rsqrt(ms + EPS)) * gain
                if rope:
                    yh = yh * cos_ref[...] + pltpu.roll(yh, HEAD_DIM // 2, 1) * sin_ref[...]
                if head < n_q:
                    yh = yh * ATTN_SCALE
            parts.append(yh.astype(o_ref.dtype))
        o_ref[:, cols] = jnp.concatenate(parts, axis=1)


def _norm_mod_matmul(x, g, mod3, shift_blk, scale_blk, w, rows_per_mod, mod_base,
                     n_q, n_k, q_gain, k_gain, cos2=None, sin2=None, tm=512, tn=512):
    n, d = x.shape
    nout = w.shape[1]
    tm = min(tm, n)
    tn = min(tn, nout)
    per = rows_per_mod // tm
    rope = cos2 is not None

    def mod_row(i):
        return mod_base + i // per

    head_vec = pl.BlockSpec((1, HEAD_DIM), lambda i: (0, 0))
    in_specs = [pl.BlockSpec((tm, d), lambda i: (i, 0)),
                pl.BlockSpec((1, d), lambda i: (0, 0)),
                pl.BlockSpec((None, 1, d), lambda i: (mod_row(i), 0, shift_blk)),
                pl.BlockSpec((None, 1, d), lambda i: (mod_row(i), 0, scale_blk)),
                pl.BlockSpec((d, nout), lambda i: (0, 0), pipeline_mode=pl.Buffered(1)),
                head_vec, head_vec]
    args = [x, g.reshape(1, d), mod3, mod3, w, q_gain.reshape(1, HEAD_DIM), k_gain.reshape(1, HEAD_DIM)]
    if rope:
        tiles_per_seq = cos2.shape[0] // tm
        table = pl.BlockSpec((tm, HEAD_DIM), lambda i: (i % tiles_per_seq, 0))
        in_specs += [table, table]
        args += [cos2, sin2]
    return pl.pallas_call(
        functools.partial(_nm_mm_kernel, tm=tm, tn=tn, n_q=n_q, n_k=n_k, rope=rope),
        out_shape=jax.ShapeDtypeStruct((n, nout), BF16),
        grid=(n // tm,),
        in_specs=in_specs,
        out_specs=pl.BlockSpec((tm, nout), lambda i: (i, 0)),
        scratch_shapes=[pltpu.VMEM((tm, d), BF16)],
        compiler_params=_cparams(("parallel",)),
        name="norm_mod_matmul",
    )(*args)


def _head_cols(h):
    return slice(h * HEAD_DIM, (h + 1) * HEAD_DIM)


def _stack_group(q_ref, kvh):
    return jnp.concatenate([q_ref[:, _head_cols(kvh * GQA_GROUP + g)] for g in range(GQA_GROUP)], axis=0)


def _sink_col(sink_ref, kvh, rows):
    return jnp.concatenate([jnp.full((rows, 1), sink_ref[kvh * GQA_GROUP + g], F32)
                            for g in range(GQA_GROUP)], axis=0)


def _win_attn_kernel(sink_ref, q_ref, kp_ref, kc_ref, kn_ref, vp_ref, vc_ref, vn_ref,
                     kx_ref, vx_ref, o_ref, *, seq, ctx_len):
    n = pl.program_id(1)
    rows = GQA_GROUP * BLOCK
    nk = 3 * BLOCK + ctx_len
    qi = lax.broadcasted_iota(jnp.int32, (rows, nk), 0) % BLOCK
    kj = lax.broadcasted_iota(jnp.int32, (rows, nk), 1)
    kpos = n * BLOCK - BLOCK + kj
    in_band = (jnp.abs(kj - BLOCK - qi) <= WINDOW) & (kpos >= 0) & (kpos < seq)
    valid = (kj >= 3 * BLOCK) | in_band
    ones = jnp.ones((nk, HEAD_DIM), BF16)
    for kvh in range(A_KV_HEADS):
        cols = _head_cols(kvh)
        q4 = _stack_group(q_ref, kvh)
        ka = jnp.concatenate([kp_ref[:, cols], kc_ref[:, cols], kn_ref[:, cols], kx_ref[:, cols]], axis=0)
        va = jnp.concatenate([vp_ref[:, cols], vc_ref[:, cols], vn_ref[:, cols], vx_ref[:, cols]], axis=0)
        s = jnp.where(valid, _dot_nt(q4, ka), NEG_BIG)
        sk = _sink_col(sink_ref, kvh, BLOCK)
        m = jnp.maximum(jnp.max(s, axis=-1, keepdims=True), sk)
        p = jnp.exp((s - m).astype(BF16))
        acc = _dot(p, jnp.concatenate([va, ones], axis=1))
        o = acc[:, 0:HEAD_DIM] / (acc[:, HEAD_DIM:] + jnp.exp(sk - m))
        for g in range(GQA_GROUP):
            o_ref[:, _head_cols(kvh * GQA_GROUP + g)] = o[g * BLOCK:(g + 1) * BLOCK].astype(o_ref.dtype)


def _win_attn(sink, proj, proj_c, batch, seq, ctx_len):
    nb = seq // BLOCK
    kvw = A_KV_HEADS * HEAD_DIM
    k_blk = A_Q_HEADS * HEAD_DIM // kvw
    v_blk = k_blk + 1

    def prev(b, n):
        return b * nb + jnp.maximum(n - 1, 0)

    def cur(b, n):
        return b * nb + n

    def nxt(b, n):
        return b * nb + jnp.minimum(n + 1, nb - 1)

    return pl.pallas_call(
        functools.partial(_win_attn_kernel, seq=seq, ctx_len=ctx_len),
        out_shape=jax.ShapeDtypeStruct((batch * seq, A_Q_HEADS * HEAD_DIM), BF16),
        grid=(batch, nb),
        in_specs=[pl.BlockSpec(memory_space=pltpu.SMEM),
                  pl.BlockSpec((BLOCK, A_Q_HEADS * HEAD_DIM), lambda b, n: (cur(b, n), 0)),
                  pl.BlockSpec((BLOCK, kvw), lambda b, n: (prev(b, n), k_blk)),
                  pl.BlockSpec((BLOCK, kvw), lambda b, n: (cur(b, n), k_blk)),
                  pl.BlockSpec((BLOCK, kvw), lambda b, n: (nxt(b, n), k_blk)),
                  pl.BlockSpec((BLOCK, kvw), lambda b, n: (prev(b, n), v_blk)),
                  pl.BlockSpec((BLOCK, kvw), lambda b, n: (cur(b, n), v_blk)),
                  pl.BlockSpec((BLOCK, kvw), lambda b, n: (nxt(b, n), v_blk)),
                  pl.BlockSpec((ctx_len, kvw), lambda b, n: (b, k_blk)),
                  pl.BlockSpec((ctx_len, kvw), lambda b, n: (b, v_blk))],
        out_specs=pl.BlockSpec((BLOCK, A_Q_HEADS * HEAD_DIM), lambda b, n: (cur(b, n), 0)),
        compiler_params=_cparams(("parallel", "parallel")),
        name="win_attn",
    )(sink, proj, proj, proj, proj, proj, proj, proj, proj_c, proj_c)


def _ctx_attn_kernel(sink_ref, q_ref, k_ref, v_ref, o_ref, *, ctx_len):
    kvh = pl.program_id(1)
    q4 = jnp.concatenate([q_ref[:, _head_cols(g)] for g in range(GQA_GROUP)], axis=0)
    s = _dot_nt(q4, k_ref[...])
    sk = jnp.concatenate([jnp.full((ctx_len, 1), sink_ref[kvh * GQA_GROUP + g], F32)
                          for g in range(GQA_GROUP)], axis=0)
    m = jnp.maximum(jnp.max(s, axis=-1, keepdims=True), sk)
    p = jnp.exp(s - m)
    den = jnp.sum(p, axis=-1, keepdims=True) + jnp.exp(sk - m)
    o = _dot(p.astype(BF16), v_ref[...]) / den
    for g in range(GQA_GROUP):
        o_ref[:, _head_cols(g)] = o[g * ctx_len:(g + 1) * ctx_len].astype(o_ref.dtype)


def _ctx_attn(sink, proj_c, batch, ctx_len):
    k_blk = A_Q_HEADS
    v_blk = A_Q_HEADS + A_KV_HEADS
    gw = GQA_GROUP * HEAD_DIM
    return pl.pallas_call(
        functools.partial(_ctx_attn_kernel, ctx_len=ctx_len),
        out_shape=jax.ShapeDtypeStruct((batch * ctx_len, A_Q_HEADS * HEAD_DIM), BF16),
        grid=(batch, A_KV_HEADS),
        in_specs=[pl.BlockSpec(memory_space=pltpu.SMEM),
                  pl.BlockSpec((ctx_len, gw), lambda b, h: (b, h)),
                  pl.BlockSpec((ctx_len, HEAD_DIM), lambda b, h: (b, k_blk + h)),
                  pl.BlockSpec((ctx_len, HEAD_DIM), lambda b, h: (b, v_blk + h))],
        out_specs=pl.BlockSpec((ctx_len, gw), lambda b, h: (b, h)),
        compiler_params=_cparams(("parallel", "parallel")),
        name="ctx_attn",
    )(sink, proj_c, proj_c, proj_c)


def _sigmoid(x):
    return 0.5 * jnp.tanh(0.5 * x) + 0.5


def _gelu_tanh(x):
    return 0.5 * x * (1.0 + jnp.tanh(0.7978845608028654 * (x + 0.044715 * (x * x * x))))


def _lru_sequence(x_ref, xg_ref, y_ref, xp_ref, a_ref, b_ref, w, init, rows, jb):
    conv_w, conv_b, wa, ba, wx, bx, c_logsig = w
    width = x_ref.shape[-1]
    sub = lax.broadcasted_iota(jnp.int32, (1, SUBLANES, width), 1)

    def block_diag(ub, wd):
        return jnp.concatenate([_dot(ub[:, _head_cols(n)], wd[n]) for n in range(width // LANES)], axis=1)

    def fill(r, carry):
        rr = pl.ds(pl.multiple_of(r * jb, jb), jb)
        xp_ref[pl.ds(pl.multiple_of(r * jb, jb) + CONV_LEFT, jb)] = x_ref[rr].astype(F32)
        return carry

    lax.fori_loop(0, rows // jb, fill, 0)
    tail = x_ref[rows - CONV_LEFT:rows].astype(F32)
    xp_ref[0:CONV_LEFT] = jnp.where(sub == 0, 0.0, pltpu.roll(tail, 1, 1))
    head = x_ref[0:1].astype(F32)
    xp_ref[rows + CONV_LEFT:rows + CONV_LEFT + 1] = jnp.where(
        sub == SUBLANES - 1, 0.0, pltpu.roll(head, SUBLANES - 1, 1))

    def gates(r, carry):
        j0 = pl.multiple_of(r * jb, jb)
        u = conv_b
        for k in range(CONV_W):
            u = u + conv_w[k] * xp_ref[pl.ds(j0 + k, jb)]
        u2 = u.reshape(jb * SUBLANES, width)
        ub = u2.astype(BF16)
        for d in range(2):
            r_gate = _sigmoid(block_diag(ub, wa[d]) + ba[d])
            i_gate = _sigmoid(block_diag(ub, wx[d]) + bx[d])
            log_a = c_logsig[d] * r_gate
            a = jnp.exp(log_a)
            b = jnp.sqrt(1.0 - a * a) * (i_gate * u2)
            a_ref[d, pl.ds(j0, jb)] = a.reshape(jb, SUBLANES, width)
            b_ref[d, pl.ds(j0, jb)] = b.reshape(jb, SUBLANES, width)
        return carry

    lax.fori_loop(0, rows // jb, gates, 0, unroll=2)

    def scan(j, carry):
        hf, pf, hb, pb = carry
        jr = rows - 1 - j
        af = a_ref[0, j]
        hf = af * hf + b_ref[0, j]
        pf = pf * af
        b_ref[0, j] = hf
        a_ref[0, j] = pf
        ab = a_ref[1, jr]
        hb = ab * hb + b_ref[1, jr]
        pb = pb * ab
        b_ref[1, jr] = hb
        a_ref[1, jr] = pb
        return hf, pf, hb, pb

    z = jnp.zeros((SUBLANES, width), F32)
    o = jnp.ones((SUBLANES, width), F32)
    lax.fori_loop(0, rows, scan, (z, o, z, o), unroll=8)

    hf_last, pf_last = b_ref[0, rows - 1], a_ref[0, rows - 1]
    hb_last, pb_last = b_ref[1, 0], a_ref[1, 0]
    s = init[0]
    carry_f = []
    for c in range(SUBLANES):
        carry_f.append(s)
        s = hf_last[c:c + 1] + pf_last[c:c + 1] * s
    out_f = s
    s = init[1]
    carry_b = [None] * SUBLANES
    for c in reversed(range(SUBLANES)):
        carry_b[c] = s
        s = hb_last[c:c + 1] + pb_last[c:c + 1] * s
    out_b = s
    cf = jnp.concatenate(carry_f, axis=0)
    cb = jnp.concatenate(carry_b, axis=0)

    def emit(r, carry):
        rr = pl.ds(pl.multiple_of(r * jb, jb), jb)
        h = (b_ref[0, rr] + a_ref[0, rr] * cf) + (b_ref[1, rr] + a_ref[1, rr] * cb)
        y_ref[rr] = (h * _gelu_tanh(xg_ref[rr].astype(F32))).astype(y_ref.dtype)
        return carry

    lax.fori_loop(0, rows // jb, emit, 0)
    return out_f, out_b


def _lru_kernel(xr_ref, xg_ref, xrc_ref, xgc_ref, cw_ref, cb_ref, wa_ref, ba_ref, wx_ref, bx_ref,
                lam_ref, y_ref, yc_ref, xp_ref, a_ref, b_ref, *, rows, rows_c):
    c_logsig = [LRU_C * jax.nn.log_sigmoid(lam_ref[d]) for d in range(2)]
    w = ([cw_ref[k] for k in range(CONV_W)], cb_ref[0],
         [wa_ref[d] for d in range(2)], [ba_ref[d] for d in range(2)],
         [wx_ref[d] for d in range(2)], [bx_ref[d] for d in range(2)], c_logsig)
    zero = jnp.zeros((1, xr_ref.shape[-1]), F32)
    sf, sb = _lru_sequence(xrc_ref, xgc_ref, yc_ref, xp_ref, a_ref, b_ref, w, (zero, zero),
                           rows_c, min(LRU_JB, rows_c))
    _lru_sequence(xr_ref, xg_ref, y_ref, xp_ref, a_ref, b_ref, w, (sf, sb), rows, LRU_JB)


def _lru(xr, xg, xrc, xgc, conv_w, conv_b, wa, ba, wx, bx, lam):
    batch, rows, _, width = xr.shape
    rows_c = xrc.shape[1]
    cw = LRU_LANE_BLOCKS * LANES
    seq_spec = pl.BlockSpec((None, rows, SUBLANES, cw), lambda b, n: (b, 0, 0, n))
    ctx_spec = pl.BlockSpec((None, rows_c, SUBLANES, cw), lambda b, n: (b, 0, 0, n))
    vec2 = pl.BlockSpec((2, 1, cw), lambda b, n: (0, 0, n))
    mat2 = pl.BlockSpec((2, LRU_LANE_BLOCKS, LANES, LANES), lambda b, n: (0, n, 0, 0))
    return pl.pallas_call(
        functools.partial(_lru_kernel, rows=rows, rows_c=rows_c),
        out_shape=(jax.ShapeDtypeStruct(xr.shape, BF16), jax.ShapeDtypeStruct(xrc.shape, BF16)),
        grid=(batch, width // cw),
        in_specs=[seq_spec, seq_spec, ctx_spec, ctx_spec,
                  pl.BlockSpec((CONV_W, 1, cw), lambda b, n: (0, 0, n)),
                  pl.BlockSpec((1, 1, cw), lambda b, n: (0, 0, n)),
                  mat2, vec2, mat2, vec2, vec2],
        out_specs=(seq_spec, ctx_spec),
        scratch_shapes=[pltpu.VMEM((rows + CONV_W - 1, SUBLANES, cw), F32),
                        pltpu.VMEM((2, rows, SUBLANES, cw), F32),
                        pltpu.VMEM((2, rows, SUBLANES, cw), F32)],
        compiler_params=_cparams(("parallel", "parallel")),
        name="rglru",
    )(xr, xg, xrc, xgc, conv_w.reshape(CONV_W, 1, width), conv_b.reshape(1, 1, width),
      wa.astype(BF16), ba.reshape(2, 1, width), wx.astype(BF16), bx.reshape(2, 1, width),
      lam.reshape(2, 1, width))


def _to_chunked(a, batch):
    t = a.shape[0] // batch
    return a.reshape(batch, LRU_CHUNKS, t // LRU_CHUNKS, a.shape[1]).transpose(0, 2, 1, 3)


def _from_chunked(a):
    b, r, c, w = a.shape
    return a.transpose(0, 2, 1, 3).reshape(b * r * c, w)


def _out_proj_kernel(a1_ref, a2_ref, w1_ref, w2_ref, x_ref, g_ref, o_ref):
    y = _dot(a1_ref[...], w1_ref[...]) + _dot(a2_ref[...], w2_ref[...])
    o_ref[...] = x_ref[...] + g_ref[...] * y


def _out_proj(a1, a1_blk, a2, a2_blk, w, x, mod3, gate_blk, rows_per_mod, mod_base, tm=2048, tn=512):
    n, d = x.shape
    kh = w.shape[0] // 2
    tm = min(tm, n)
    per = rows_per_mod // tm
    gpb = d // tn
    return pl.pallas_call(
        _out_proj_kernel,
        out_shape=jax.ShapeDtypeStruct((n, d), F32),
        grid=(n // tm, d // tn),
        in_specs=[pl.BlockSpec((tm, kh), lambda i, j: (i, a1_blk)),
                  pl.BlockSpec((tm, kh), lambda i, j: (i, a2_blk)),
                  pl.BlockSpec((kh, tn), lambda i, j: (0, j)),
                  pl.BlockSpec((kh, tn), lambda i, j: (1, j)),
                  pl.BlockSpec((tm, tn), lambda i, j: (i, j)),
                  pl.BlockSpec((None, 1, tn), lambda i, j: (mod_base + i // per, 0, gate_blk * gpb + j))],
        out_specs=pl.BlockSpec((tm, tn), lambda i, j: (i, j)),
        compiler_params=_cparams(("parallel", "parallel")),
        name="out_proj",
    )(a1, a2, w, w, x, mod3)


def _lane_block_max(s):
    mm = s[:, 0:LANES]
    for t in range(1, s.shape[1] // LANES):
        mm = jnp.maximum(mm, s[:, t * LANES:(t + 1) * LANES])
    return mm


def _exp_blocks(s, mrep):
    return jnp.concatenate(
        [jnp.exp((s[:, t * LANES:(t + 1) * LANES] - mrep).astype(BF16)) for t in range(s.shape[1] // LANES)],
        axis=1)


def _copy_key_rows(dst_ref, c, kc, lat_ref, ctx_ref, cols):
    seq = lat_ref.shape[0]
    lo, hi = c * kc, (c + 1) * kc
    if lo < seq:
        n = min(hi, seq) - lo
        dst_ref[c, 0:n, cols] = lat_ref[lo:lo + n, :]
    if hi > seq:
        start = max(lo, seq)
        dst_ref[c, start - lo:kc, cols] = ctx_ref[start - seq:hi - seq, :]


def _dense_attn_kernel(q_ref, k_ref, v_ref, kx_ref, vx_ref, o_ref,
                       s_ref, m_ref, acc_ref, ka_ref, va_ref, *, tq, n_chunks, kc):
    @pl.when(pl.program_id(2) == 0)
    def _():
        for c in range(n_chunks):
            _copy_key_rows(ka_ref, c, kc, k_ref, kx_ref, slice(0, HEAD_DIM))
            _copy_key_rows(va_ref, c, kc, v_ref, vx_ref, slice(0, HEAD_DIM))
            va_ref[c, :, HEAD_DIM:] = jnp.ones((kc, HEAD_DIM), BF16)

    q4 = jnp.concatenate([q_ref[:, _head_cols(g)] for g in range(GQA_GROUP)], axis=0)
    m_ref[...] = jnp.full(m_ref.shape, NEG_BIG, F32)

    def sweep1(c, carry):
        s = _dot_nt(q4, ka_ref[c])
        s_ref[c] = s
        m_ref[...] = jnp.maximum(m_ref[...], _lane_block_max(s))
        return carry

    lax.fori_loop(0, n_chunks, sweep1, 0)
    m_ref[...] = jnp.broadcast_to(jnp.max(m_ref[...], axis=-1, keepdims=True), m_ref.shape)
    acc_ref[...] = jnp.zeros(acc_ref.shape, F32)

    def sweep2(c, carry):
        acc_ref[...] += _dot(_exp_blocks(s_ref[c], m_ref[...]), va_ref[c])
        return carry

    lax.fori_loop(0, n_chunks, sweep2, 0)
    o = acc_ref[:, 0:HEAD_DIM] / acc_ref[:, HEAD_DIM:]
    for g in range(GQA_GROUP):
        o_ref[:, _head_cols(g)] = o[g * tq:(g + 1) * tq].astype(o_ref.dtype)


def _dense_attn(proj, proj_c, batch, seq, ctx_len, tq=256, n_chunks=2):
    gw = GQA_GROUP * HEAD_DIM
    nq = seq // tq
    rows = GQA_GROUP * tq
    kc = (seq + ctx_len) // n_chunks
    assert kc * n_chunks == seq + ctx_len and kc % LANES == 0
    k_blk = C_Q_HEADS
    v_blk = C_Q_HEADS + C_KV_HEADS
    vx_blk = C_KV_HEADS
    return pl.pallas_call(
        functools.partial(_dense_attn_kernel, tq=tq, n_chunks=n_chunks, kc=kc),
        out_shape=jax.ShapeDtypeStruct((batch * seq, C_Q_HEADS * HEAD_DIM), BF16),
        grid=(batch, C_KV_HEADS, nq),
        in_specs=[pl.BlockSpec((tq, gw), lambda b, h, i: (b * nq + i, h)),
                  pl.BlockSpec((seq, HEAD_DIM), lambda b, h, i: (b, k_blk + h)),
                  pl.BlockSpec((seq, HEAD_DIM), lambda b, h, i: (b, v_blk + h)),
                  pl.BlockSpec((ctx_len, HEAD_DIM), lambda b, h, i: (b, h)),
                  pl.BlockSpec((ctx_len, HEAD_DIM), lambda b, h, i: (b, vx_blk + h))],
        out_specs=pl.BlockSpec((tq, gw), lambda b, h, i: (b * nq + i, h)),
        scratch_shapes=[pltpu.VMEM((n_chunks, rows, kc), F32),
                        pltpu.VMEM((rows, LANES), F32),
                        pltpu.VMEM((rows, 2 * HEAD_DIM), F32),
                        pltpu.VMEM((n_chunks, kc, HEAD_DIM), BF16),
                        pltpu.VMEM((n_chunks, kc, 2 * HEAD_DIM), BF16)],
        compiler_params=_cparams(("parallel", "parallel", "arbitrary")),
        name="dense_attn",
    )(proj, proj, proj, proj_c, proj_c)


def _router_rows(biased, scores):
    v = [biased[e:e + 1, :] for e in range(N_EXPERTS)]
    s = [scores[e:e + 1, :] for e in range(N_EXPERTS)]

    def top2_sum(vals):
        best = vals[0] + vals[1]
        for i in range(len(vals)):
            for j in range(i + 1, len(vals)):
                if (i, j) != (0, 1):
                    best = jnp.maximum(best, vals[i] + vals[j])
        return best

    gsum = [top2_sum(v[g * EXPERTS_PER_GROUP:(g + 1) * EXPERTS_PER_GROUP]) for g in range(N_GROUPS)]
    sel = jnp.zeros_like(gsum[0], dtype=jnp.int32)
    best = gsum[0]
    for g in range(1, N_GROUPS):
        take = gsum[g] > best
        sel = jnp.where(take, g, sel)
        best = jnp.where(take, gsum[g], best)

    def pick_group(rows, i):
        out = rows[i]
        for g in range(1, N_GROUPS):
            out = jnp.where(sel == g, rows[g * EXPERTS_PER_GROUP + i], out)
        return out

    cand = [pick_group(v, i) for i in range(EXPERTS_PER_GROUP)]
    cand_s = [pick_group(s, i) for i in range(EXPERTS_PER_GROUP)]
    i1 = jnp.zeros_like(sel)
    b1 = cand[0]
    for i in range(1, EXPERTS_PER_GROUP):
        take = cand[i] > b1
        i1 = jnp.where(take, i, i1)
        b1 = jnp.where(take, cand[i], b1)
    i2 = jnp.full_like(sel, -1)
    b2 = jnp.full_like(b1, -jnp.inf)
    for i in range(EXPERTS_PER_GROUP):
        take = (i1 != i) & ((cand[i] > b2) | (i2 < 0))
        i2 = jnp.where(take, i, i2)
        b2 = jnp.where(take, cand[i], b2)

    def pick_idx(rows, idx):
        out = rows[0]
        for i in range(1, EXPERTS_PER_GROUP):
            out = jnp.where(idx == i, rows[i], out)
        return out

    s0 = pick_idx(cand_s, i1)
    s1 = pick_idx(cand_s, i2)
    tot = s0 + s1
    e0 = (sel * EXPERTS_PER_GROUP + i1).astype(F32)
    e1 = (sel * EXPERTS_PER_GROUP + i2).astype(F32)
    return e0, e1, s0 / tot, s1 / tot


def _norm_router_kernel(*refs, tm, n_lat_tiles, has_ctx):
    if has_ctx:
        x_ref, xc_ref, g_ref, sh_ref, sc_ref, rw_ref, rb_ref, hp_ref, r_ref, h_ref = refs
        i = pl.program_id(0)

        @pl.when(i < n_lat_tiles)
        def _():
            _norm_mod_rows(x_ref, g_ref, sh_ref, sc_ref, h_ref, tm)

        @pl.when(i >= n_lat_tiles)
        def _():
            _norm_mod_rows(xc_ref, g_ref, sh_ref, sc_ref, h_ref, tm)
    else:
        x_ref, g_ref, sh_ref, sc_ref, rw_ref, rb_ref, hp_ref, r_ref, h_ref = refs
        _norm_mod_rows(x_ref, g_ref, sh_ref, sc_ref, h_ref, tm)

    def pack(r, carry):
        rows = pl.ds(pl.multiple_of(r * 128, 128), 128)
        hp_ref[rows, :] = _pack_halves(h_ref[rows, :])
        return carry

    lax.fori_loop(0, tm // 128, pack, 0)
    logits = _dot_nt(rw_ref[...], h_ref[...])
    scores = jax.nn.sigmoid(logits)
    e0, e1, w0, w1 = _router_rows(scores + rb_ref[...], scores)
    zero = jnp.zeros_like(w0)
    r_ref[...] = jnp.concatenate([e0, e1, w0, w1, zero, zero, zero, zero], axis=0)


def _pack_halves(v):
    c = v.shape[1] // 2
    lo = lax.bitcast_convert_type(v[:, :c].astype(BF16).astype(F32), jnp.uint32)
    hi = lax.bitcast_convert_type(v[:, c:].astype(BF16).astype(F32), jnp.uint32)
    return (lo >> 16) | (hi & jnp.uint32(0xFFFF0000))


def _unpack_halves(p):
    lo = lax.bitcast_convert_type(p << 16, F32)
    hi = lax.bitcast_convert_type(p & jnp.uint32(0xFFFF0000), F32)
    return lo, hi


def _sc_gather_rows(table, idx):
    n, w = idx.shape[0], table.shape[1]
    win, nb, sub = SC_GATHER_WINDOW, SC_GATHER_BUFFERS, SC_GATHER_SUB
    per = n // (win * SC_WORKERS)
    assert per * win * SC_WORKERS == n and per >= 1
    mesh = plsc.VectorSubcoreMesh(core_axis_name="core", subcore_axis_name="subcore")

    @functools.partial(
        pl.kernel, out_type=jax.ShapeDtypeStruct((n, w), table.dtype), mesh=mesh, name="sc_gather_rows",
        scratch_types=([pltpu.VMEM((per * win,), jnp.int32)] + [pltpu.VMEM((win, w), table.dtype)] * nb
                       + [pltpu.SemaphoreType.DMA] * (2 * nb)))
    def gather(x_hbm, i_hbm, o_hbm, i_v, *rest):
        bufs, gsems, wsems = rest[:nb], rest[nb:2 * nb], rest[2 * nb:]
        wid = lax.axis_index("subcore") * SC_CORES + lax.axis_index("core")
        base = wid * (per * win)
        pltpu.sync_copy(i_hbm.at[pl.ds(base, per * win)], i_v)

        def gather_copy(t, b, s):
            return pltpu.make_async_copy(x_hbm.at[i_v.at[pl.ds(t * win + s * sub, sub)]],
                                         bufs[b].at[pl.ds(s * sub, sub)], gsems[b])

        def write_copy(t, b):
            return pltpu.make_async_copy(bufs[b], o_hbm.at[pl.ds(base + t * win, win)], wsems[b])

        def start_gathers(t, b):
            for s in range(win // sub):
                gather_copy(t, b, s).start()

        def step(t, b):
            for s in range(win // sub):
                gather_copy(t, b, s).wait()
            write_copy(t, b).start()

            @pl.when(t >= 1)
            def _():
                write_copy(t - 1, (b - 1) % nb).wait()

            @pl.when(t + nb - 1 < per)
            def _():
                start_gathers(t + nb - 1, (b + nb - 1) % nb)

        for t in range(min(nb - 1, per)):
            start_gathers(t, t)

        @pl.loop(0, per // nb)
        def _(p):
            for j in range(nb):
                step(nb * p + j, j)

        for t in range(per - per % nb, per):
            step(t, t % nb)
        write_copy(per - 1, (per - 1) % nb).wait()

    return gather(table, idx)


def _lat_ctx_maps(n_lat, per, ctx_mod_row):
    def mod_row(i):
        return jnp.where(i < n_lat, i // per, ctx_mod_row)

    def lat(i):
        return (jnp.minimum(i, n_lat - 1), 0)

    def ctx(i):
        return (jnp.maximum(i - n_lat, 0), 0)

    return mod_row, lat, ctx


def _norm_router(x, cx, g, mod3, shift_blk, scale_blk, rw_t, rb, rows_per_mod, ctx_mod_row, tm=512):
    n, d = x.shape
    n_lat = n // tm
    has_ctx = cx is not None
    ntot = n_lat + (cx.shape[0] // tm if has_ctx else 0)
    mod_row, lat, ctx = _lat_ctx_maps(n_lat, rows_per_mod // tm, ctx_mod_row)
    row_specs = [pl.BlockSpec((tm, d), lat)] + ([pl.BlockSpec((tm, d), ctx)] if has_ctx else [])
    row_args = [x] + ([cx] if has_ctx else [])
    return pl.pallas_call(
        functools.partial(_norm_router_kernel, tm=tm, n_lat_tiles=n_lat, has_ctx=has_ctx),
        out_shape=(jax.ShapeDtypeStruct((ntot * tm, d // 2), jnp.uint32),
                   jax.ShapeDtypeStruct((SUBLANES, ntot * tm), F32)),
        grid=(ntot,),
        in_specs=row_specs + [
            pl.BlockSpec((1, d), lambda i: (0, 0)),
            pl.BlockSpec((None, 1, d), lambda i: (mod_row(i), 0, shift_blk)),
            pl.BlockSpec((None, 1, d), lambda i: (mod_row(i), 0, scale_blk)),
            pl.BlockSpec((N_EXPERTS, d), lambda i: (0, 0)),
            pl.BlockSpec((N_EXPERTS, 1), lambda i: (0, 0))],
        out_specs=(pl.BlockSpec((tm, d // 2), lambda i: (i, 0)),
                   pl.BlockSpec((SUBLANES, tm), lambda i: (0, i))),
        scratch_shapes=[pltpu.VMEM((tm, d), BF16)],
        compiler_params=_cparams(("arbitrary",)),
        name="norm_router",
    )(*row_args, g.reshape(1, d), mod3, mod3, rw_t, rb)


def _cast_rows(src_ref, dst_ref, rb=256):
    def body(r, carry):
        rows = pl.ds(pl.multiple_of(r * rb, rb), rb)
        dst_ref[rows, :] = src_ref[0, rows, :].astype(dst_ref.dtype)
        return carry

    lax.fori_loop(0, dst_ref.shape[0] // rb, body, 0)


def _tile_state(te_ref, tv_ref, tile0):
    i = pl.program_id(0)
    t = tile0 + i
    live = tv_ref[t] > 0
    new_expert = (i == 0) | (te_ref[t] != te_ref[jnp.maximum(t - 1, 0)])
    return live, new_expert


def _gmm_kernel(te_ref, tv_ref, xs_ref, wg_ref, wu_ref, wd_ref, *rest, tile0):
    o_ref, wg_b, wu_b, wd_b = rest[-4:]
    live, new_expert = _tile_state(te_ref, tv_ref, tile0)

    @pl.when(live & new_expert)
    def _():
        _cast_rows(wg_ref, wg_b)
        _cast_rows(wu_ref, wu_b)
        _cast_rows(wd_ref, wd_b)

    @pl.when(live)
    def _():
        lo, hi = _unpack_halves(xs_ref[...])
        x = jnp.concatenate([lo.astype(BF16), hi.astype(BF16)], axis=1)
        gate = _dot(x, wg_b[...])
        up = _dot(x, wu_b[...])
        h1 = ((gate * jax.nn.sigmoid(gate)) * up).astype(BF16)
        o_ref[...] = _pack_halves(_dot(h1, wd_b[...]))

    @pl.when(jnp.logical_not(live))
    def _():
        o_ref[...] = jnp.zeros(o_ref.shape, o_ref.dtype)


def _grouped_mlp(tile_expert, tile_valid, xs_part, tile0, ys_prev, wg, wu, wd, layer, tm=MOE_TM):
    dpk = xs_part.shape[1]
    d, dff = wg.shape[2], wg.shape[3]
    n_tiles = tile_expert.shape[0]

    def expert_block(rows, cols):
        return pl.BlockSpec((None, 1, rows, cols), lambda i, te, tv: (layer, te[tile0 + i], 0, 0),
                            pipeline_mode=pl.Buffered(1))

    in_specs = [pl.BlockSpec((tm, dpk), lambda i, te, tv: (i, 0)),
                expert_block(d, dff), expert_block(d, dff), expert_block(dff, d)]
    args = [tile_expert, tile_valid, xs_part, wg, wu, wd]
    aliases = {}
    if ys_prev is not None:
        in_specs.append(pl.BlockSpec(memory_space=pl.ANY))
        args.append(ys_prev)
        aliases = {len(args) - 1: 0}
    grid_spec = pltpu.PrefetchScalarGridSpec(
        num_scalar_prefetch=2,
        grid=(xs_part.shape[0] // tm,),
        in_specs=in_specs,
        out_specs=pl.BlockSpec((tm, dpk), lambda i, te, tv: (tile0 + i, 0)),
        scratch_shapes=[pltpu.VMEM((d, dff), BF16), pltpu.VMEM((d, dff), BF16), pltpu.VMEM((dff, d), BF16)],
    )
    return pl.pallas_call(
        functools.partial(_gmm_kernel, tile0=tile0),
        out_shape=jax.ShapeDtypeStruct((n_tiles * tm, dpk), jnp.uint32),
        grid_spec=grid_spec,
        input_output_aliases=aliases,
        compiler_params=_cparams(("arbitrary",)),
        name="grouped_mlp",
    )(*args)


def _combine_kernel(*refs, n_lat_tiles, has_ctx):
    if has_ctx:
        x_ref, xc_ref, y0_ref, y1_ref, r_ref, g_ref, o_ref, oc_ref = refs
    else:
        x_ref, y0_ref, y1_ref, r_ref, g_ref, o_ref = refs
    w0 = r_ref[:, 2:3]
    w1 = r_ref[:, 3:4]
    lo0, hi0 = _unpack_halves(y0_ref[...])
    lo1, hi1 = _unpack_halves(y1_ref[...])
    f = g_ref[...] * jnp.concatenate([w0 * lo0 + w1 * lo1, w0 * hi0 + w1 * hi1], axis=1)
    if not has_ctx:
        o_ref[...] = x_ref[...] + f
        return
    i = pl.program_id(0)

    @pl.when(i < n_lat_tiles)
    def _():
        o_ref[...] = x_ref[...] + f

    @pl.when(i >= n_lat_tiles)
    def _():
        oc_ref[...] = xc_ref[...] + f


def _combine(x, cx, yg, route_cols, mod3, gate_blk, rows_per_mod, ctx_mod_row, tm=512):
    n, d = x.shape
    n_lat = n // tm
    has_ctx = cx is not None
    ntot = n_lat + (cx.shape[0] // tm if has_ctx else 0)
    mod_row, lat, ctx = _lat_ctx_maps(n_lat, rows_per_mod // tm, ctx_mod_row)
    row_specs = [pl.BlockSpec((tm, d), lat)] + ([pl.BlockSpec((tm, d), ctx)] if has_ctx else [])
    row_args = [x] + ([cx] if has_ctx else [])
    out_shape = [jax.ShapeDtypeStruct(x.shape, F32)] + ([jax.ShapeDtypeStruct(cx.shape, F32)] if has_ctx else [])
    out = pl.pallas_call(
        functools.partial(_combine_kernel, n_lat_tiles=n_lat, has_ctx=has_ctx),
        out_shape=tuple(out_shape),
        grid=(ntot,),
        in_specs=row_specs + [
            pl.BlockSpec((tm, d // 2), lambda i: (i, 0)),
            pl.BlockSpec((tm, d // 2), lambda i: (ntot + i, 0)),
            pl.BlockSpec((tm, SUBLANES), lambda i: (i, 0)),
            pl.BlockSpec((None, 1, d), lambda i: (mod_row(i), 0, gate_blk))],
        out_specs=tuple(row_specs),
        compiler_params=_cparams(("arbitrary",)),
        name="moe_combine",
    )(*row_args, yg, yg, route_cols, mod3)
    return out if has_ctx else (out[0], None)


def _dispatch_plan(route, tm):
    n = route.shape[1]
    e_flat = jnp.concatenate([route[0], route[1]]).astype(jnp.int32)
    n_assign = 2 * n
    n_tiles = n_assign // tm + N_EXPERTS
    experts = jnp.arange(N_EXPERTS, dtype=jnp.int32)[:, None]
    onehot = (experts == e_flat[None, :]).astype(jnp.int32)
    csum = jnp.cumsum(onehot, axis=1)
    counts = csum[:, -1]
    padded = ((counts + tm - 1) // tm) * tm
    ends_p = jnp.cumsum(padded)
    starts_p = ends_p - padded
    starts_c = jnp.cumsum(counts) - counts
    dest = jnp.sum(onehot * (csum - 1 + starts_p[:, None]), axis=0)
    order = jnp.argsort(e_flat, stable=True).astype(jnp.int32)
    p = jnp.arange(n_tiles * tm, dtype=jnp.int32)[None, :]
    owner = ((p >= starts_p[:, None]) & (p < ends_p[:, None])).astype(jnp.int32)
    within = jnp.sum(owner * (p - starts_p[:, None]), axis=0)
    live = jnp.sum(owner * (p - starts_p[:, None] < counts[:, None]), axis=0) > 0
    compact = jnp.sum(owner * starts_c[:, None], axis=0) + within
    src_tok = jnp.where(live, order[jnp.clip(compact, 0, n_assign - 1)] % n, 0)
    tile_start = jnp.arange(n_tiles, dtype=jnp.int32) * tm
    tile_valid = (tile_start < ends_p[-1]).astype(jnp.int32)
    last_tile = jnp.maximum(ends_p[-1] // tm - 1, 0) * tm
    tile_expert = jnp.searchsorted(ends_p, jnp.minimum(tile_start, last_tile), side="right").astype(jnp.int32)
    tile_expert = jnp.minimum(tile_expert, N_EXPERTS - 1)
    return src_tok, dest, tile_expert, tile_valid


def _moe(x, cx, g, mod3, rw_t, rb, wg, wu, wd, layer, rows_per_mod, ctx_mod_row):
    h, route = _norm_router(x, cx, g, mod3, 3, 4, rw_t, rb, rows_per_mod, ctx_mod_row)
    src_tok, dest, tile_expert, tile_valid = _dispatch_plan(route, MOE_TM)
    n_tiles = tile_expert.shape[0]
    bounds = [n_tiles * k // MOE_RANGES for k in range(MOE_RANGES + 1)]
    ys = None
    for t0, t1 in zip(bounds[:-1], bounds[1:]):
        xs = _sc_gather_rows(h, src_tok[t0 * MOE_TM:t1 * MOE_TM])
        ys = _grouped_mlp(tile_expert, tile_valid, xs, t0, ys, wg, wu, wd, layer)
    yg = _sc_gather_rows(ys, dest)
    route_cols = route.T
    return _combine(x, cx, yg, route_cols, mod3, 5, rows_per_mod, ctx_mod_row)


def _rope_tables(seq):
    rows = seq // GRID_W
    row = jnp.repeat(jnp.arange(rows, dtype=F32), GRID_W)
    col = jnp.tile(jnp.arange(GRID_W, dtype=F32), rows)
    n_freq = HEAD_DIM // 4
    inv_freq = ROPE_BASE ** (-jnp.arange(n_freq, dtype=F32) / n_freq)
    ang = jnp.concatenate([row[:, None] * inv_freq, col[:, None] * inv_freq], axis=-1)
    cos, sin = jnp.cos(ang), jnp.sin(ang)
    return jnp.concatenate([cos, cos], axis=-1), jnp.concatenate([-sin, sin], axis=-1)


def kernel(x, c, ctx, c_ctx, ada_w, ada_b, norm_mix, norm_ffn, ab_w_in, ab_q_gain, ab_k_gain, ab_sink, ab_conv_w, ab_conv_b, ab_gate_a_w, ab_gate_a_b, ab_gate_x_w, ab_gate_x_b, ab_lru_lambda, ab_w_out, gqa_w_in, gqa_q_gain, gqa_k_gain, gqa_w_out, router_w, router_bias, moe_w_gate, moe_w_up, moe_w_down):
    batch, seq, d = x.shape
    ctx_len = ctx.shape[1]
    depth = ada_w.shape[0]
    assert depth == 2 and batch < SUBLANES
    n_lat = batch * seq
    n_ctx = batch * ctx_len
    ctx_row = batch

    xl = x.reshape(n_lat, d)
    xc = ctx.reshape(n_ctx, d)
    cc = jnp.zeros((SUBLANES, d), F32).at[:batch].set(c).at[ctx_row].set(c_ctx)
    mod = _ada(cc, ada_w, ada_b)
    cos2, sin2 = _rope_tables(seq)
    rw_t = router_w.T.astype(BF16)
    rb = router_bias.reshape(N_EXPERTS, 1).astype(F32)
    experts = (moe_w_gate, moe_w_up, moe_w_down)

    mod3 = mod[0].reshape(SUBLANES, 1, 6 * d)
    w_in = ab_w_in[0].astype(BF16)
    proj = _norm_mod_matmul(xl, norm_mix[0], mod3, 0, 1, w_in, seq, 0,
                            A_Q_HEADS, A_KV_HEADS, ab_q_gain[0], ab_k_gain[0], cos2, sin2)
    proj_c = _norm_mod_matmul(xc, norm_mix[0], mod3, 0, 1, w_in, n_ctx, ctx_row,
                              A_Q_HEADS, A_KV_HEADS, ab_q_gain[0], ab_k_gain[0])
    att = _win_attn(ab_sink[0], proj, proj_c, batch, seq, ctx_len)
    att_c = _ctx_attn(ab_sink[0], proj_c, batch, ctx_len)

    lru_w = ab_conv_w.shape[2]
    c0 = (A_Q_HEADS + 2 * A_KV_HEADS) * HEAD_DIM
    y_p, yc_p = _lru(_to_chunked(proj[:, c0:c0 + lru_w], batch),
                     _to_chunked(proj[:, c0 + lru_w:c0 + 2 * lru_w], batch),
                     _to_chunked(proj_c[:, c0:c0 + lru_w], batch),
                     _to_chunked(proj_c[:, c0 + lru_w:c0 + 2 * lru_w], batch),
                     ab_conv_w[0], ab_conv_b[0], ab_gate_a_w[0], ab_gate_a_b[0],
                     ab_gate_x_w[0], ab_gate_x_b[0], ab_lru_lambda[0])
    w_out = ab_w_out[0].astype(BF16)
    xl = _out_proj(att, 0, _from_chunked(y_p), 0, w_out, xl, mod3, 2, seq, 0)
    xc = _out_proj(att_c, 0, _from_chunked(yc_p), 0, w_out, xc, mod3, 2, n_ctx, ctx_row)
    xl, xc = _moe(xl, xc, norm_ffn[0], mod3, rw_t, rb, *experts, 0, seq, ctx_row)

    mod3 = mod[1].reshape(SUBLANES, 1, 6 * d)
    w_in = gqa_w_in[0].astype(BF16)
    cw = C_Q_HEADS * HEAD_DIM
    proj = _norm_mod_matmul(xl, norm_mix[1], mod3, 0, 1, w_in, seq, 0,
                            C_Q_HEADS, C_KV_HEADS, gqa_q_gain[0], gqa_k_gain[0], cos2, sin2)
    proj_c = _norm_mod_matmul(xc, norm_mix[1], mod3, 0, 1, w_in[:, cw:], n_ctx, ctx_row,
                              0, C_KV_HEADS, gqa_q_gain[0], gqa_k_gain[0])
    att = _dense_attn(proj, proj_c, batch, seq, ctx_len)
    xl = _out_proj(att, 0, att, 1, gqa_w_out[0].astype(BF16), xl, mod3, 2, seq, 0)
    xl, _ = _moe(xl, None, norm_ffn[1], mod3, rw_t, rb, *experts, 1, seq, ctx_row)
    return xl.reshape(batch, seq, d)
```

```python
import functools

import jax
import jax.numpy as jnp
import numpy as np
from jax import lax
from jax.experimental import pallas as pl
from jax.experimental.pallas import tpu as pltpu
from jax.experimental.pallas import tpu_sc as plsc

F32 = jnp.float32
BF16 = jnp.bfloat16

LANES = 128
SUBLANES = 8
VMEM_LIMIT = 56 * 1024 * 1024

HEAD_DIM = 128
GRID_W = 64
WINDOW = 128
BLOCK = 128
ROPE_BASE = 10000.0
EPS = 1e-6
ATTN_SCALE = HEAD_DIM ** -0.5
A_Q_HEADS, A_KV_HEADS = 8, 2
C_Q_HEADS, C_KV_HEADS = 16, 4
GQA_GROUP = 4
LRU_C = 8.0
CONV_W = 4
CONV_LEFT = 2
N_EXPERTS = 16
N_GROUPS = 4
EXPERTS_PER_GROUP = 4
NEG_BIG = -1e30

LRU_CHUNKS = SUBLANES
LRU_JB = 16
LRU_LANE_BLOCKS = 2
MOE_TM = 256
MOE_RANGES = 1
SC_CORES = 2
SC_WORKERS = 32
SC_GATHER_WINDOW = 32
SC_GATHER_BUFFERS = 3
SC_GATHER_SUB = 8


def _cparams(sem, vmem=VMEM_LIMIT):
    return pltpu.CompilerParams(dimension_semantics=sem, vmem_limit_bytes=vmem)


def _dot(a, b):
    return jnp.dot(a, b, preferred_element_type=F32)


def _dot_nt(a, b):
    return lax.dot_general(a, b, (((1,), (1,)), ((), ())), preferred_element_type=F32)


def _ada_kernel(c_ref, w_ref, b_ref, o_ref):
    c = c_ref[...]
    s = (c * jax.nn.sigmoid(c)).astype(BF16)
    o_ref[0] = _dot(s, w_ref[0].astype(BF16)) + b_ref[0]


def _ada(cc, ada_w, ada_b):
    depth, d, n = ada_w.shape
    tn = 1024
    return pl.pallas_call(
        _ada_kernel,
        out_shape=jax.ShapeDtypeStruct((depth, SUBLANES, n), F32),
        grid=(depth, n // tn),
        in_specs=[pl.BlockSpec((SUBLANES, d), lambda l, j: (0, 0)),
                  pl.BlockSpec((1, d, tn), lambda l, j: (l, 0, j)),
                  pl.BlockSpec((1, 1, tn), lambda l, j: (l, 0, j))],
        out_specs=pl.BlockSpec((1, SUBLANES, tn), lambda l, j: (l, 0, j)),
        compiler_params=_cparams(("arbitrary", "arbitrary")),
        name="ada",
    )(cc, ada_w, ada_b.reshape(depth, 1, n))


def _norm_mod_rows(x_ref, g_ref, sh_ref, sc_ref, dst_ref, tm, rc=128):
    g = g_ref[...]
    sc1 = 1.0 + sc_ref[...]
    sh = sh_ref[...]

    def body(r, carry):
        rows = pl.ds(pl.multiple_of(r * rc, rc), rc)
        xf = x_ref[rows, :]
        ms = jnp.mean(xf * xf, axis=-1, keepdims=True)
        xn = (xf * lax.rsqrt(ms + EPS)) * g
        dst_ref[rows, :] = (xn * sc1 + sh).astype(dst_ref.dtype)
        return carry

    lax.fori_loop(0, tm // rc, body, 0)


def _nm_mm_kernel(*refs, tm, tn, n_q, n_k, rope):
    if rope:
        x_ref, g_ref, sh_ref, sc_ref, w_ref, qg_ref, kg_ref, cos_ref, sin_ref, o_ref, hn_ref = refs
    else:
        x_ref, g_ref, sh_ref, sc_ref, w_ref, qg_ref, kg_ref, o_ref, hn_ref = refs
    _norm_mod_rows(x_ref, g_ref, sh_ref, sc_ref, hn_ref, tm)
    h = hn_ref[...]
    heads_per_chunk = tn // HEAD_DIM
    for j in range(w_ref.shape[1] // tn):
        cols = slice(j * tn, (j + 1) * tn)
        y = _dot(h, w_ref[:, cols])
        parts = []
        for hh in range(heads_per_chunk):
            head = j * heads_per_chunk + hh
            yh = y[:, _head_cols(hh)]
            if head < n_q + n_k:
                gain = qg_ref[...] if head < n_q else kg_ref[...]
                ms = jnp.mean(yh * yh, axis=-1, keepdims=True)
                yh = (yh * lax.rsqrt(ms + EPS)) * gain
                if rope:
                    yh = yh * cos_ref[...] + pltpu.roll(yh, HEAD_DIM // 2, 1) * sin_ref[...]
                if head < n_q:
                    yh = yh * ATTN_SCALE
            parts.append(yh.astype(o_ref.dtype))
        o_ref[:, cols] = jnp.concatenate(parts, axis=1)


def _norm_mod_matmul(x, g, mod3, shift_blk, scale_blk, w, rows_per_mod, mod_base,
                     n_q, n_k, q_gain, k_gain, cos2=None, sin2=None, tm=512, tn=512):
    n, d = x.shape
    nout = w.shape[1]
    tm = min(tm, n)
    tn = min(tn, nout)
    per = rows_per_mod // tm
    rope = cos2 is not None

    def mod_row(i):
        return mod_base + i // per

    head_vec = pl.BlockSpec((1, HEAD_DIM), lambda i: (0, 0))
    in_specs = [pl.BlockSpec((tm, d), lambda i: (i, 0)),
                pl.BlockSpec((1, d), lambda i: (0, 0)),
                pl.BlockSpec((None, 1, d), lambda i: (mod_row(i), 0, shift_blk)),
                pl.BlockSpec((None, 1, d), lambda i: (mod_row(i), 0, scale_blk)),
                pl.BlockSpec((d, nout), lambda i: (0, 0), pipeline_mode=pl.Buffered(1)),
                head_vec, head_vec]
    args = [x, g.reshape(1, d), mod3, mod3, w, q_gain.reshape(1, HEAD_DIM), k_gain.reshape(1, HEAD_DIM)]
    if rope:
        tiles_per_seq = cos2.shape[0] // tm
        table = pl.BlockSpec((tm, HEAD_DIM), lambda i: (i % tiles_per_seq, 0))
        in_specs += [table, table]
        args += [cos2, sin2]
    return pl.pallas_call(
        functools.partial(_nm_mm_kernel, tm=tm, tn=tn, n_q=n_q, n_k=n_k, rope=rope),
        out_shape=jax.ShapeDtypeStruct((n, nout), BF16),
        grid=(n // tm,),
        in_specs=in_specs,
        out_specs=pl.BlockSpec((tm, nout), lambda i: (i, 0)),
        scratch_shapes=[pltpu.VMEM((tm, d), BF16)],
        compiler_params=_cparams(("parallel",)),
        name="norm_mod_matmul",
    )(*args)


def _head_cols(h):
    return slice(h * HEAD_DIM, (h + 1) * HEAD_DIM)


def _stack_group(q_ref, kvh):
    return jnp.concatenate([q_ref[:, _head_cols(kvh * GQA_GROUP + g)] for g in range(GQA_GROUP)], axis=0)


def _sink_col(sink_ref, kvh, rows):
    return jnp.concatenate([jnp.full((rows, 1), sink_ref[kvh * GQA_GROUP + g], F32)
                            for g in range(GQA_GROUP)], axis=0)


def _band_bias(ctx_len):
    rows, nk = GQA_GROUP * BLOCK, 3 * BLOCK + ctx_len
    qi = np.arange(rows)[:, None] % BLOCK
    kj = np.arange(nk)[None, :]
    inner = (kj >= 3 * BLOCK) | (np.abs(kj - BLOCK - qi) <= WINDOW)
    first = inner & ~(kj < BLOCK)
    last = inner & ~((kj >= 2 * BLOCK) & (kj < 3 * BLOCK))
    return np.where(np.stack([first, inner, last]), 0.0, NEG_BIG).astype(np.float32)


def _win_attn_kernel(sink_ref, bias_ref, q_ref, kp_ref, kc_ref, kn_ref, vp_ref, vc_ref, vn_ref,
                     kx_ref, vx_ref, o_ref, *, ctx_len):
    nk = 3 * BLOCK + ctx_len
    bias = bias_ref[...]
    ones = jnp.ones((nk, HEAD_DIM), BF16)
    for kvh in range(A_KV_HEADS):
        cols = _head_cols(kvh)
        q4 = _stack_group(q_ref, kvh)
        ka = jnp.concatenate([kp_ref[:, cols], kc_ref[:, cols], kn_ref[:, cols], kx_ref[:, cols]], axis=0)
        va = jnp.concatenate([vp_ref[:, cols], vc_ref[:, cols], vn_ref[:, cols], vx_ref[:, cols]], axis=0)
        s = _dot_nt(q4, ka) + bias
        sk = _sink_col(sink_ref, kvh, BLOCK)
        m = jnp.maximum(jnp.max(s, axis=-1, keepdims=True), sk)
        p = jnp.exp((s - m).astype(BF16))
        acc = _dot(p, jnp.concatenate([va, ones], axis=1))
        o = acc[:, 0:HEAD_DIM] / (acc[:, HEAD_DIM:] + jnp.exp(sk - m))
        for g in range(GQA_GROUP):
            o_ref[:, _head_cols(kvh * GQA_GROUP + g)] = o[g * BLOCK:(g + 1) * BLOCK].astype(o_ref.dtype)


def _win_attn(sink, proj, proj_c, batch, seq, ctx_len):
    nb = seq // BLOCK
    assert nb >= 2
    kvw = A_KV_HEADS * HEAD_DIM
    k_blk = A_Q_HEADS * HEAD_DIM // kvw
    v_blk = k_blk + 1
    bias = jnp.asarray(_band_bias(ctx_len))

    def which_bias(b, n):
        return (jnp.where(n == 0, 0, jnp.where(n == nb - 1, 2, 1)), 0, 0)

    def prev(b, n):
        return b * nb + jnp.maximum(n - 1, 0)

    def cur(b, n):
        return b * nb + n

    def nxt(b, n):
        return b * nb + jnp.minimum(n + 1, nb - 1)

    return pl.pallas_call(
        functools.partial(_win_attn_kernel, ctx_len=ctx_len),
        out_shape=jax.ShapeDtypeStruct((batch * seq, A_Q_HEADS * HEAD_DIM), BF16),
        grid=(batch, nb),
        in_specs=[pl.BlockSpec(memory_space=pltpu.SMEM),
                  pl.BlockSpec((None,) + bias.shape[1:], which_bias),
                  pl.BlockSpec((BLOCK, A_Q_HEADS * HEAD_DIM), lambda b, n: (cur(b, n), 0)),
                  pl.BlockSpec((BLOCK, kvw), lambda b, n: (prev(b, n), k_blk)),
                  pl.BlockSpec((BLOCK, kvw), lambda b, n: (cur(b, n), k_blk)),
                  pl.BlockSpec((BLOCK, kvw), lambda b, n: (nxt(b, n), k_blk)),
                  pl.BlockSpec((BLOCK, kvw), lambda b, n: (prev(b, n), v_blk)),
                  pl.BlockSpec((BLOCK, kvw), lambda b, n: (cur(b, n), v_blk)),
                  pl.BlockSpec((BLOCK, kvw), lambda b, n: (nxt(b, n), v_blk)),
                  pl.BlockSpec((ctx_len, kvw), lambda b, n: (b, k_blk)),
                  pl.BlockSpec((ctx_len, kvw), lambda b, n: (b, v_blk))],
        out_specs=pl.BlockSpec((BLOCK, A_Q_HEADS * HEAD_DIM), lambda b, n: (cur(b, n), 0)),
        compiler_params=_cparams(("parallel", "parallel")),
        name="win_attn",
    )(sink, bias, proj, proj, proj, proj, proj, proj, proj, proj_c, proj_c)


def _ctx_attn_kernel(sink_ref, q_ref, k_ref, v_ref, o_ref, *, ctx_len):
    kvh = pl.program_id(1)
    q4 = jnp.concatenate([q_ref[:, _head_cols(g)] for g in range(GQA_GROUP)], axis=0)
    s = _dot_nt(q4, k_ref[...])
    sk = jnp.concatenate([jnp.full((ctx_len, 1), sink_ref[kvh * GQA_GROUP + g], F32)
                          for g in range(GQA_GROUP)], axis=0)
    m = jnp.maximum(jnp.max(s, axis=-1, keepdims=True), sk)
    p = jnp.exp(s - m)
    den = jnp.sum(p, axis=-1, keepdims=True) + jnp.exp(sk - m)
    o = _dot(p.astype(BF16), v_ref[...]) / den
    for g in range(GQA_GROUP):
        o_ref[:, _head_cols(g)] = o[g * ctx_len:(g + 1) * ctx_len].astype(o_ref.dtype)


def _ctx_attn(sink, proj_c, batch, ctx_len):
    k_blk = A_Q_HEADS
    v_blk = A_Q_HEADS + A_KV_HEADS
    gw = GQA_GROUP * HEAD_DIM
    return pl.pallas_call(
        functools.partial(_ctx_attn_kernel, ctx_len=ctx_len),
        out_shape=jax.ShapeDtypeStruct((batch * ctx_len, A_Q_HEADS * HEAD_DIM), BF16),
        grid=(batch, A_KV_HEADS),
        in_specs=[pl.BlockSpec(memory_space=pltpu.SMEM),
                  pl.BlockSpec((ctx_len, gw), lambda b, h: (b, h)),
                  pl.BlockSpec((ctx_len, HEAD_DIM), lambda b, h: (b, k_blk + h)),
                  pl.BlockSpec((ctx_len, HEAD_DIM), lambda b, h: (b, v_blk + h))],
        out_specs=pl.BlockSpec((ctx_len, gw), lambda b, h: (b, h)),
        compiler_params=_cparams(("parallel", "parallel")),
        name="ctx_attn",
    )(sink, proj_c, proj_c, proj_c)


def _sigmoid(x):
    return 0.5 * jnp.tanh(0.5 * x) + 0.5


def _gelu_tanh(x):
    return 0.5 * x * (1.0 + jnp.tanh(0.7978845608028654 * (x + 0.044715 * (x * x * x))))


def _lru_sequence(x_ref, xg_ref, y_ref, xp_ref, a_ref, b_ref, w, init, rows, jb):
    conv_w, conv_b, wa, ba, wx, bx, c_logsig = w
    width = x_ref.shape[-1]
    sub = lax.broadcasted_iota(jnp.int32, (1, SUBLANES, width), 1)

    def block_diag(ub, wd):
        return jnp.concatenate([_dot(ub[:, _head_cols(n)], wd[n]) for n in range(width // LANES)], axis=1)

    def fill(r, carry):
        rr = pl.ds(pl.multiple_of(r * jb, jb), jb)
        xp_ref[pl.ds(pl.multiple_of(r * jb, jb) + CONV_LEFT, jb)] = x_ref[rr].astype(F32)
        return carry

    lax.fori_loop(0, rows // jb, fill, 0)
    tail = x_ref[rows - CONV_LEFT:rows].astype(F32)
    xp_ref[0:CONV_LEFT] = jnp.where(sub == 0, 0.0, pltpu.roll(tail, 1, 1))
    head = x_ref[0:1].astype(F32)
    xp_ref[rows + CONV_LEFT:rows + CONV_LEFT + 1] = jnp.where(
        sub == SUBLANES - 1, 0.0, pltpu.roll(head, SUBLANES - 1, 1))

    def gates(r, carry):
        j0 = pl.multiple_of(r * jb, jb)
        u = conv_b
        for k in range(CONV_W):
            u = u + conv_w[k] * xp_ref[pl.ds(j0 + k, jb)]
        u2 = u.reshape(jb * SUBLANES, width)
        ub = u2.astype(BF16)
        for d in range(2):
            r_gate = _sigmoid(block_diag(ub, wa[d]) + ba[d])
            i_gate = _sigmoid(block_diag(ub, wx[d]) + bx[d])
            log_a = c_logsig[d] * r_gate
            a = jnp.exp(log_a)
            b = jnp.sqrt(1.0 - a * a) * (i_gate * u2)
            a_ref[d, pl.ds(j0, jb)] = a.reshape(jb, SUBLANES, width)
            b_ref[d, pl.ds(j0, jb)] = b.reshape(jb, SUBLANES, width)
        return carry

    lax.fori_loop(0, rows // jb, gates, 0, unroll=2)

    def scan(j, carry):
        hf, pf, hb, pb = carry
        jr = rows - 1 - j
        af = a_ref[0, j]
        hf = af * hf + b_ref[0, j]
        pf = pf * af
        b_ref[0, j] = hf
        a_ref[0, j] = pf
        ab = a_ref[1, jr]
        hb = ab * hb + b_ref[1, jr]
        pb = pb * ab
        b_ref[1, jr] = hb
        a_ref[1, jr] = pb
        return hf, pf, hb, pb

    z = jnp.zeros((SUBLANES, width), F32)
    o = jnp.ones((SUBLANES, width), F32)
    lax.fori_loop(0, rows, scan, (z, o, z, o), unroll=8)

    hf_last, pf_last = b_ref[0, rows - 1], a_ref[0, rows - 1]
    hb_last, pb_last = b_ref[1, 0], a_ref[1, 0]
    s = init[0]
    carry_f = []
    for c in range(SUBLANES):
        carry_f.append(s)
        s = hf_last[c:c + 1] + pf_last[c:c + 1] * s
    out_f = s
    s = init[1]
    carry_b = [None] * SUBLANES
    for c in reversed(range(SUBLANES)):
        carry_b[c] = s
        s = hb_last[c:c + 1] + pb_last[c:c + 1] * s
    out_b = s
    cf = jnp.concatenate(carry_f, axis=0)
    cb = jnp.concatenate(carry_b, axis=0)

    def emit(r, carry):
        rr = pl.ds(pl.multiple_of(r * jb, jb), jb)
        h = (b_ref[0, rr] + a_ref[0, rr] * cf) + (b_ref[1, rr] + a_ref[1, rr] * cb)
        y_ref[rr] = (h * _gelu_tanh(xg_ref[rr].astype(F32))).astype(y_ref.dtype)
        return carry

    lax.fori_loop(0, rows // jb, emit, 0)
    return out_f, out_b


def _lru_kernel(xr_ref, xg_ref, xrc_ref, xgc_ref, cw_ref, cb_ref, wa_ref, ba_ref, wx_ref, bx_ref,
                lam_ref, y_ref, yc_ref, xp_ref, a_ref, b_ref, *, rows, rows_c):
    c_logsig = [LRU_C * jax.nn.log_sigmoid(lam_ref[d]) for d in range(2)]
    w = ([cw_ref[k] for k in range(CONV_W)], cb_ref[0],
         [wa_ref[d] for d in range(2)], [ba_ref[d] for d in range(2)],
         [wx_ref[d] for d in range(2)], [bx_ref[d] for d in range(2)], c_logsig)
    zero = jnp.zeros((1, xr_ref.shape[-1]), F32)
    sf, sb = _lru_sequence(xrc_ref, xgc_ref, yc_ref, xp_ref, a_ref, b_ref, w, (zero, zero),
                           rows_c, min(LRU_JB, rows_c))
    _lru_sequence(xr_ref, xg_ref, y_ref, xp_ref, a_ref, b_ref, w, (sf, sb), rows, LRU_JB)


def _lru(xr, xg, xrc, xgc, conv_w, conv_b, wa, ba, wx, bx, lam):
    batch, rows, _, width = xr.shape
    rows_c = xrc.shape[1]
    cw = LRU_LANE_BLOCKS * LANES
    seq_spec = pl.BlockSpec((None, rows, SUBLANES, cw), lambda b, n: (b, 0, 0, n))
    ctx_spec = pl.BlockSpec((None, rows_c, SUBLANES, cw), lambda b, n: (b, 0, 0, n))
    vec2 = pl.BlockSpec((2, 1, cw), lambda b, n: (0, 0, n))
    mat2 = pl.BlockSpec((2, LRU_LANE_BLOCKS, LANES, LANES), lambda b, n: (0, n, 0, 0))
    return pl.pallas_call(
        functools.partial(_lru_kernel, rows=rows, rows_c=rows_c),
        out_shape=(jax.ShapeDtypeStruct(xr.shape, BF16), jax.ShapeDtypeStruct(xrc.shape, BF16)),
        grid=(batch, width // cw),
        in_specs=[seq_spec, seq_spec, ctx_spec, ctx_spec,
                  pl.BlockSpec((CONV_W, 1, cw), lambda b, n: (0, 0, n)),
                  pl.BlockSpec((1, 1, cw), lambda b, n: (0, 0, n)),
                  mat2, vec2, mat2, vec2, vec2],
        out_specs=(seq_spec, ctx_spec),
        scratch_shapes=[pltpu.VMEM((rows + CONV_W - 1, SUBLANES, cw), F32),
                        pltpu.VMEM((2, rows, SUBLANES, cw), F32),
                        pltpu.VMEM((2, rows, SUBLANES, cw), F32)],
        compiler_params=_cparams(("parallel", "parallel")),
        name="rglru",
    )(xr, xg, xrc, xgc, conv_w.reshape(CONV_W, 1, width), conv_b.reshape(1, 1, width),
      wa.astype(BF16), ba.reshape(2, 1, width), wx.astype(BF16), bx.reshape(2, 1, width),
      lam.reshape(2, 1, width))


def _to_chunked(a, batch):
    t = a.shape[0] // batch
    return a.reshape(batch, LRU_CHUNKS, t // LRU_CHUNKS, a.shape[1]).transpose(0, 2, 1, 3)


def _from_chunked(a):
    b, r, c, w = a.shape
    return a.transpose(0, 2, 1, 3).reshape(b * r * c, w)


def _out_proj_kernel(a1_ref, a2_ref, w1_ref, w2_ref, x_ref, g_ref, o_ref):
    y = _dot(a1_ref[...], w1_ref[...]) + _dot(a2_ref[...], w2_ref[...])
    o_ref[...] = x_ref[...] + g_ref[...] * y


def _out_proj(a1, a1_blk, a2, a2_blk, w, x, mod3, gate_blk, rows_per_mod, mod_base, tm=2048, tn=512):
    n, d = x.shape
    kh = w.shape[0] // 2
    tm = min(tm, n)
    per = rows_per_mod // tm
    gpb = d // tn
    return pl.pallas_call(
        _out_proj_kernel,
        out_shape=jax.ShapeDtypeStruct((n, d), F32),
        grid=(n // tm, d // tn),
        in_specs=[pl.BlockSpec((tm, kh), lambda i, j: (i, a1_blk)),
                  pl.BlockSpec((tm, kh), lambda i, j: (i, a2_blk)),
                  pl.BlockSpec((kh, tn), lambda i, j: (0, j)),
                  pl.BlockSpec((kh, tn), lambda i, j: (1, j)),
                  pl.BlockSpec((tm, tn), lambda i, j: (i, j)),
                  pl.BlockSpec((None, 1, tn), lambda i, j: (mod_base + i // per, 0, gate_blk * gpb + j))],
        out_specs=pl.BlockSpec((tm, tn), lambda i, j: (i, j)),
        compiler_params=_cparams(("parallel", "parallel")),
        name="out_proj",
    )(a1, a2, w, w, x, mod3)


def _lane_block_max(s):
    mm = s[:, 0:LANES]
    for t in range(1, s.shape[1] // LANES):
        mm = jnp.maximum(mm, s[:, t * LANES:(t + 1) * LANES])
    return mm


def _exp_blocks(s, mrep):
    return jnp.concatenate(
        [jnp.exp((s[:, t * LANES:(t + 1) * LANES] - mrep).astype(BF16)) for t in range(s.shape[1] // LANES)],
        axis=1)


def _copy_key_rows(dst_ref, c, kc, lat_ref, ctx_ref, cols):
    seq = lat_ref.shape[0]
    lo, hi = c * kc, (c + 1) * kc
    if lo < seq:
        n = min(hi, seq) - lo
        dst_ref[c, 0:n, cols] = lat_ref[lo:lo + n, :]
    if hi > seq:
        start = max(lo, seq)
        dst_ref[c, start - lo:kc, cols] = ctx_ref[start - seq:hi - seq, :]


def _dense_attn_kernel(q_ref, k_ref, v_ref, kx_ref, vx_ref, o_ref,
                       s_ref, m_ref, acc_ref, ka_ref, va_ref, *, tq, n_chunks, kc):
    @pl.when(pl.program_id(2) == 0)
    def _():
        for c in range(n_chunks):
            _copy_key_rows(ka_ref, c, kc, k_ref, kx_ref, slice(0, HEAD_DIM))
            _copy_key_rows(va_ref, c, kc, v_ref, vx_ref, slice(0, HEAD_DIM))
            va_ref[c, :, HEAD_DIM:] = jnp.ones((kc, HEAD_DIM), BF16)

    q4 = jnp.concatenate([q_ref[:, _head_cols(g)] for g in range(GQA_GROUP)], axis=0)
    m_ref[...] = jnp.full(m_ref.shape, NEG_BIG, F32)

    def sweep1(c, carry):
        s = _dot_nt(q4, ka_ref[c])
        s_ref[c] = s
        m_ref[...] = jnp.maximum(m_ref[...], _lane_block_max(s))
        return carry

    lax.fori_loop(0, n_chunks, sweep1, 0)
    m_ref[...] = jnp.broadcast_to(jnp.max(m_ref[...], axis=-1, keepdims=True), m_ref.shape)
    acc_ref[...] = jnp.zeros(acc_ref.shape, F32)

    def sweep2(c, carry):
        acc_ref[...] += _dot(_exp_blocks(s_ref[c], m_ref[...]), va_ref[c])
        return carry

    lax.fori_loop(0, n_chunks, sweep2, 0)
    o = acc_ref[:, 0:HEAD_DIM] / acc_ref[:, HEAD_DIM:]
    for g in range(GQA_GROUP):
        o_ref[:, _head_cols(g)] = o[g * tq:(g + 1) * tq].astype(o_ref.dtype)


def _dense_attn(proj, proj_c, batch, seq, ctx_len, tq=256, n_chunks=2):
    gw = GQA_GROUP * HEAD_DIM
    nq = seq // tq
    rows = GQA_GROUP * tq
    kc = (seq + ctx_len) // n_chunks
    assert kc * n_chunks == seq + ctx_len and kc % LANES == 0
    k_blk = C_Q_HEADS
    v_blk = C_Q_HEADS + C_KV_HEADS
    vx_blk = C_KV_HEADS
    return pl.pallas_call(
        functools.partial(_dense_attn_kernel, tq=tq, n_chunks=n_chunks, kc=kc),
        out_shape=jax.ShapeDtypeStruct((batch * seq, C_Q_HEADS * HEAD_DIM), BF16),
        grid=(batch, C_KV_HEADS, nq),
        in_specs=[pl.BlockSpec((tq, gw), lambda b, h, i: (b * nq + i, h)),
                  pl.BlockSpec((seq, HEAD_DIM), lambda b, h, i: (b, k_blk + h)),
                  pl.BlockSpec((seq, HEAD_DIM), lambda b, h, i: (b, v_blk + h)),
                  pl.BlockSpec((ctx_len, HEAD_DIM), lambda b, h, i: (b, h)),
                  pl.BlockSpec((ctx_len, HEAD_DIM), lambda b, h, i: (b, vx_blk + h))],
        out_specs=pl.BlockSpec((tq, gw), lambda b, h, i: (b * nq + i, h)),
        scratch_shapes=[pltpu.VMEM((n_chunks, rows, kc), F32),
                        pltpu.VMEM((rows, LANES), F32),
                        pltpu.VMEM((rows, 2 * HEAD_DIM), F32),
                        pltpu.VMEM((n_chunks, kc, HEAD_DIM), BF16),
                        pltpu.VMEM((n_chunks, kc, 2 * HEAD_DIM), BF16)],
        compiler_params=_cparams(("parallel", "parallel", "arbitrary")),
        name="dense_attn",
    )(proj, proj, proj, proj_c, proj_c)


def _router_rows(biased, scores):
    v = [biased[e:e + 1, :] for e in range(N_EXPERTS)]
    s = [scores[e:e + 1, :] for e in range(N_EXPERTS)]

    def top2_sum(vals):
        best = vals[0] + vals[1]
        for i in range(len(vals)):
            for j in range(i + 1, len(vals)):
                if (i, j) != (0, 1):
                    best = jnp.maximum(best, vals[i] + vals[j])
        return best

    gsum = [top2_sum(v[g * EXPERTS_PER_GROUP:(g + 1) * EXPERTS_PER_GROUP]) for g in range(N_GROUPS)]
    sel = jnp.zeros_like(gsum[0], dtype=jnp.int32)
    best = gsum[0]
    for g in range(1, N_GROUPS):
        take = gsum[g] > best
        sel = jnp.where(take, g, sel)
        best = jnp.where(take, gsum[g], best)

    def pick_group(rows, i):
        out = rows[i]
        for g in range(1, N_GROUPS):
            out = jnp.where(sel == g, rows[g * EXPERTS_PER_GROUP + i], out)
        return out

    cand = [pick_group(v, i) for i in range(EXPERTS_PER_GROUP)]
    cand_s = [pick_group(s, i) for i in range(EXPERTS_PER_GROUP)]
    i1 = jnp.zeros_like(sel)
    b1 = cand[0]
    for i in range(1, EXPERTS_PER_GROUP):
        take = cand[i] > b1
        i1 = jnp.where(take, i, i1)
        b1 = jnp.where(take, cand[i], b1)
    i2 = jnp.full_like(sel, -1)
    b2 = jnp.full_like(b1, -jnp.inf)
    for i in range(EXPERTS_PER_GROUP):
        take = (i1 != i) & ((cand[i] > b2) | (i2 < 0))
        i2 = jnp.where(take, i, i2)
        b2 = jnp.where(take, cand[i], b2)

    def pick_idx(rows, idx):
        out = rows[0]
        for i in range(1, EXPERTS_PER_GROUP):
            out = jnp.where(idx == i, rows[i], out)
        return out

    s0 = pick_idx(cand_s, i1)
    s1 = pick_idx(cand_s, i2)
    tot = s0 + s1
    e0 = (sel * EXPERTS_PER_GROUP + i1).astype(F32)
    e1 = (sel * EXPERTS_PER_GROUP + i2).astype(F32)
    return e0, e1, s0 / tot, s1 / tot


def _norm_router_kernel(*refs, tm, n_lat_tiles, has_ctx):
    if has_ctx:
        x_ref, xc_ref, g_ref, sh_ref, sc_ref, rw_ref, rb_ref, hp_ref, r_ref, h_ref = refs
        i = pl.program_id(0)

        @pl.when(i < n_lat_tiles)
        def _():
            _norm_mod_rows(x_ref, g_ref, sh_ref, sc_ref, h_ref, tm)

        @pl.when(i >= n_lat_tiles)
        def _():
            _norm_mod_rows(xc_ref, g_ref, sh_ref, sc_ref, h_ref, tm)
    else:
        x_ref, g_ref, sh_ref, sc_ref, rw_ref, rb_ref, hp_ref, r_ref, h_ref = refs
        _norm_mod_rows(x_ref, g_ref, sh_ref, sc_ref, h_ref, tm)

    def pack(r, carry):
        rows = pl.ds(pl.multiple_of(r * 128, 128), 128)
        hp_ref[rows, :] = _pack_halves(h_ref[rows, :])
        return carry

    lax.fori_loop(0, tm // 128, pack, 0)
    logits = _dot_nt(rw_ref[...], h_ref[...])
    scores = jax.nn.sigmoid(logits)
    e0, e1, w0, w1 = _router_rows(scores + rb_ref[...], scores)
    zero = jnp.zeros_like(w0)
    r_ref[...] = jnp.concatenate([e0, e1, w0, w1, zero, zero, zero, zero], axis=0)


def _pack_halves(v):
    c = v.shape[1] // 2
    lo = lax.bitcast_convert_type(v[:, :c].astype(BF16).astype(F32), jnp.uint32)
    hi = lax.bitcast_convert_type(v[:, c:].astype(BF16).astype(F32), jnp.uint32)
    return (lo >> 16) | (hi & jnp.uint32(0xFFFF0000))


def _unpack_halves(p):
    lo = lax.bitcast_convert_type(p << 16, F32)
    hi = lax.bitcast_convert_type(p & jnp.uint32(0xFFFF0000), F32)
    return lo, hi


def _sc_gather_rows(table, idx):
    n, w = idx.shape[0], table.shape[1]
    win, nb, sub = SC_GATHER_WINDOW, SC_GATHER_BUFFERS, SC_GATHER_SUB
    per = n // (win * SC_WORKERS)
    assert per * win * SC_WORKERS == n and per >= 1
    mesh = plsc.VectorSubcoreMesh(core_axis_name="core", subcore_axis_name="subcore")

    @functools.partial(
        pl.kernel, out_type=jax.ShapeDtypeStruct((n, w), table.dtype), mesh=mesh, name="sc_gather_rows",
        scratch_types=([pltpu.VMEM((per * win,), jnp.int32)] + [pltpu.VMEM((win, w), table.dtype)] * nb
                       + [pltpu.SemaphoreType.DMA] * (2 * nb)))
    def gather(x_hbm, i_hbm, o_hbm, i_v, *rest):
        bufs, gsems, wsems = rest[:nb], rest[nb:2 * nb], rest[2 * nb:]
        wid = lax.axis_index("subcore") * SC_CORES + lax.axis_index("core")
        base = wid * (per * win)
        pltpu.sync_copy(i_hbm.at[pl.ds(base, per * win)], i_v)
        rot = (wid * per) // SC_WORKERS

        def row0(t):
            u = t + rot
            u = u - per * (u >= per).astype(jnp.int32)
            return pl.multiple_of(u * win, win)

        def gather_copy(t, b, s):
            return pltpu.make_async_copy(x_hbm.at[i_v.at[pl.ds(row0(t) + s * sub, sub)]],
                                         bufs[b].at[pl.ds(s * sub, sub)], gsems[b])

        def write_copy(t, b):
            return pltpu.make_async_copy(bufs[b], o_hbm.at[pl.ds(base + row0(t), win)], wsems[b])

        def start_gathers(t, b):
            for s in range(win // sub):
                gather_copy(t, b, s).start()

        def step(t, b):
            for s in range(win // sub):
                gather_copy(t, b, s).wait()
            write_copy(t, b).start()

            @pl.when(t >= 1)
            def _():
                write_copy(t - 1, (b - 1) % nb).wait()

            @pl.when(t + nb - 1 < per)
            def _():
                start_gathers(t + nb - 1, (b + nb - 1) % nb)

        for t in range(min(nb - 1, per)):
            start_gathers(t, t)

        @pl.loop(0, per // nb)
        def _(p):
            for j in range(nb):
                step(nb * p + j, j)

        for t in range(per - per % nb, per):
            step(t, t % nb)
        write_copy(per - 1, (per - 1) % nb).wait()

    return gather(table, idx)


def _lat_ctx_maps(n_lat, per, ctx_mod_row):
    def mod_row(i):
        return jnp.where(i < n_lat, i // per, ctx_mod_row)

    def lat(i):
        return (jnp.minimum(i, n_lat - 1), 0)

    def ctx(i):
        return (jnp.maximum(i - n_lat, 0), 0)

    return mod_row, lat, ctx


def _norm_router(x, cx, g, mod3, shift_blk, scale_blk, rw_t, rb, rows_per_mod, ctx_mod_row, tm=512):
    n, d = x.shape
    n_lat = n // tm
    has_ctx = cx is not None
    ntot = n_lat + (cx.shape[0] // tm if has_ctx else 0)
    mod_row, lat, ctx = _lat_ctx_maps(n_lat, rows_per_mod // tm, ctx_mod_row)
    row_specs = [pl.BlockSpec((tm, d), lat)] + ([pl.BlockSpec((tm, d), ctx)] if has_ctx else [])
    row_args = [x] + ([cx] if has_ctx else [])
    return pl.pallas_call(
        functools.partial(_norm_router_kernel, tm=tm, n_lat_tiles=n_lat, has_ctx=has_ctx),
        out_shape=(jax.ShapeDtypeStruct((ntot * tm, d // 2), jnp.uint32),
                   jax.ShapeDtypeStruct((SUBLANES, ntot * tm), F32)),
        grid=(ntot,),
        in_specs=row_specs + [
            pl.BlockSpec((1, d), lambda i: (0, 0)),
            pl.BlockSpec((None, 1, d), lambda i: (mod_row(i), 0, shift_blk)),
            pl.BlockSpec((None, 1, d), lambda i: (mod_row(i), 0, scale_blk)),
            pl.BlockSpec((N_EXPERTS, d), lambda i: (0, 0)),
            pl.BlockSpec((N_EXPERTS, 1), lambda i: (0, 0))],
        out_specs=(pl.BlockSpec((tm, d // 2), lambda i: (i, 0)),
                   pl.BlockSpec((SUBLANES, tm), lambda i: (0, i))),
        scratch_shapes=[pltpu.VMEM((tm, d), BF16)],
        compiler_params=_cparams(("arbitrary",)),
        name="norm_router",
    )(*row_args, g.reshape(1, d), mod3, mod3, rw_t, rb)


def _cast_rows(src_ref, dst_ref, rb=256):
    def body(r, carry):
        rows = pl.ds(pl.multiple_of(r * rb, rb), rb)
        dst_ref[rows, :] = src_ref[0, rows, :].astype(dst_ref.dtype)
        return carry

    lax.fori_loop(0, dst_ref.shape[0] // rb, body, 0)


def _tile_state(te_ref, tv_ref, tile0):
    i = pl.program_id(0)
    t = tile0 + i
    live = tv_ref[t] > 0
    new_expert = (i == 0) | (te_ref[t] != te_ref[jnp.maximum(t - 1, 0)])
    return live, new_expert


def _gmm_kernel(te_ref, tv_ref, xs_ref, wg_ref, wu_ref, wd_ref, *rest, tile0):
    o_ref, wg_b, wu_b, wd_b = rest[-4:]
    live, new_expert = _tile_state(te_ref, tv_ref, tile0)

    @pl.when(live & new_expert)
    def _():
        _cast_rows(wg_ref, wg_b)
        _cast_rows(wu_ref, wu_b)
        _cast_rows(wd_ref, wd_b)

    @pl.when(live)
    def _():
        lo, hi = _unpack_halves(xs_ref[...])
        x = jnp.concatenate([lo.astype(BF16), hi.astype(BF16)], axis=1)
        gate = _dot(x, wg_b[...])
        up = _dot(x, wu_b[...])
        h1 = ((gate * jax.nn.sigmoid(gate)) * up).astype(BF16)
        o_ref[...] = _pack_halves(_dot(h1, wd_b[...]))

    @pl.when(jnp.logical_not(live))
    def _():
        o_ref[...] = jnp.zeros(o_ref.shape, o_ref.dtype)


def _grouped_mlp(tile_expert, tile_valid, xs_part, tile0, ys_prev, wg, wu, wd, layer, tm=MOE_TM):
    dpk = xs_part.shape[1]
    d, dff = wg.shape[2], wg.shape[3]
    n_tiles = tile_expert.shape[0]

    def expert_block(rows, cols):
        return pl.BlockSpec((None, 1, rows, cols), lambda i, te, tv: (layer, te[tile0 + i], 0, 0),
                            pipeline_mode=pl.Buffered(1))

    in_specs = [pl.BlockSpec((tm, dpk), lambda i, te, tv: (i, 0)),
                expert_block(d, dff), expert_block(d, dff), expert_block(dff, d)]
    args = [tile_expert, tile_valid, xs_part, wg, wu, wd]
    aliases = {}
    if ys_prev is not None:
        in_specs.append(pl.BlockSpec(memory_space=pl.ANY))
        args.append(ys_prev)
        aliases = {len(args) - 1: 0}
    grid_spec = pltpu.PrefetchScalarGridSpec(
        num_scalar_prefetch=2,
        grid=(xs_part.shape[0] // tm,),
        in_specs=in_specs,
        out_specs=pl.BlockSpec((tm, dpk), lambda i, te, tv: (tile0 + i, 0)),
        scratch_shapes=[pltpu.VMEM((d, dff), BF16), pltpu.VMEM((d, dff), BF16), pltpu.VMEM((dff, d), BF16)],
    )
    return pl.pallas_call(
        functools.partial(_gmm_kernel, tile0=tile0),
        out_shape=jax.ShapeDtypeStruct((n_tiles * tm, dpk), jnp.uint32),
        grid_spec=grid_spec,
        input_output_aliases=aliases,
        compiler_params=_cparams(("arbitrary",)),
        name="grouped_mlp",
    )(*args)


def _combine_kernel(*refs, n_lat_tiles, has_ctx):
    if has_ctx:
        x_ref, xc_ref, y0_ref, y1_ref, r_ref, g_ref, o_ref, oc_ref = refs
    else:
        x_ref, y0_ref, y1_ref, r_ref, g_ref, o_ref = refs
    w0 = r_ref[:, 2:3]
    w1 = r_ref[:, 3:4]
    lo0, hi0 = _unpack_halves(y0_ref[...])
    lo1, hi1 = _unpack_halves(y1_ref[...])
    f = g_ref[...] * jnp.concatenate([w0 * lo0 + w1 * lo1, w0 * hi0 + w1 * hi1], axis=1)
    if not has_ctx:
        o_ref[...] = x_ref[...] + f
        return
    i = pl.program_id(0)

    @pl.when(i < n_lat_tiles)
    def _():
        o_ref[...] = x_ref[...] + f

    @pl.when(i >= n_lat_tiles)
    def _():
        oc_ref[...] = xc_ref[...] + f


def _combine(x, cx, yg, route_cols, mod3, gate_blk, rows_per_mod, ctx_mod_row, tm=512):
    n, d = x.shape
    n_lat = n // tm
    has_ctx = cx is not None
    ntot = n_lat + (cx.shape[0] // tm if has_ctx else 0)
    mod_row, lat, ctx = _lat_ctx_maps(n_lat, rows_per_mod // tm, ctx_mod_row)
    row_specs = [pl.BlockSpec((tm, d), lat)] + ([pl.BlockSpec((tm, d), ctx)] if has_ctx else [])
    row_args = [x] + ([cx] if has_ctx else [])
    out_shape = [jax.ShapeDtypeStruct(x.shape, F32)] + ([jax.ShapeDtypeStruct(cx.shape, F32)] if has_ctx else [])
    out = pl.pallas_call(
        functools.partial(_combine_kernel, n_lat_tiles=n_lat, has_ctx=has_ctx),
        out_shape=tuple(out_shape),
        grid=(ntot,),
        in_specs=row_specs + [
            pl.BlockSpec((tm, d // 2), lambda i: (i, 0)),
            pl.BlockSpec((tm, d // 2), lambda i: (ntot + i, 0)),
            pl.BlockSpec((tm, SUBLANES), lambda i: (i, 0)),
            pl.BlockSpec((None, 1, d), lambda i: (mod_row(i), 0, gate_blk))],
        out_specs=tuple(row_specs),
        compiler_params=_cparams(("arbitrary",)),
        name="moe_combine",
    )(*row_args, yg, yg, route_cols, mod3)
    return out if has_ctx else (out[0], None)


def _dispatch_plan(route, tm):
    n = route.shape[1]
    e_flat = jnp.concatenate([route[0], route[1]]).astype(jnp.int32)
    n_assign = 2 * n
    n_tiles = n_assign // tm + N_EXPERTS
    experts = jnp.arange(N_EXPERTS, dtype=jnp.int32)[:, None]
    onehot = (experts == e_flat[None, :]).astype(jnp.int32)
    csum = jnp.cumsum(onehot, axis=1)
    counts = csum[:, -1]
    padded = ((counts + tm - 1) // tm) * tm
    ends_p = jnp.cumsum(padded)
    starts_p = ends_p - padded
    starts_c = jnp.cumsum(counts) - counts
    dest = jnp.sum(onehot * (csum - 1 + starts_p[:, None]), axis=0)
    order = jnp.argsort(e_flat, stable=True).astype(jnp.int32)
    p = jnp.arange(n_tiles * tm, dtype=jnp.int32)[None, :]
    owner = ((p >= starts_p[:, None]) & (p < ends_p[:, None])).astype(jnp.int32)
    within = jnp.sum(owner * (p - starts_p[:, None]), axis=0)
    live = jnp.sum(owner * (p - starts_p[:, None] < counts[:, None]), axis=0) > 0
    compact = jnp.sum(owner * starts_c[:, None], axis=0) + within
    src_tok = jnp.where(live, order[jnp.clip(compact, 0, n_assign - 1)] % n, 0)
    tile_start = jnp.arange(n_tiles, dtype=jnp.int32) * tm
    tile_valid = (tile_start < ends_p[-1]).astype(jnp.int32)
    last_tile = jnp.maximum(ends_p[-1] // tm - 1, 0) * tm
    tile_expert = jnp.searchsorted(ends_p, jnp.minimum(tile_start, last_tile), side="right").astype(jnp.int32)
    tile_expert = jnp.minimum(tile_expert, N_EXPERTS - 1)
    return src_tok, dest, tile_expert, tile_valid


def _moe(x, cx, g, mod3, rw_t, rb, wg, wu, wd, layer, rows_per_mod, ctx_mod_row):
    h, route = _norm_router(x, cx, g, mod3, 3, 4, rw_t, rb, rows_per_mod, ctx_mod_row)
    src_tok, dest, tile_expert, tile_valid = _dispatch_plan(route, MOE_TM)
    n_tiles = tile_expert.shape[0]
    bounds = [n_tiles * k // MOE_RANGES for k in range(MOE_RANGES + 1)]
    ys = None
    for t0, t1 in zip(bounds[:-1], bounds[1:]):
        xs = _sc_gather_rows(h, src_tok[t0 * MOE_TM:t1 * MOE_TM])
        ys = _grouped_mlp(tile_expert, tile_valid, xs, t0, ys, wg, wu, wd, layer)
    yg = _sc_gather_rows(ys, dest)
    route_cols = route.T
    return _combine(x, cx, yg, route_cols, mod3, 5, rows_per_mod, ctx_mod_row)


def _rope_tables(seq):
    rows = seq // GRID_W
    row = jnp.repeat(jnp.arange(rows, dtype=F32), GRID_W)
    col = jnp.tile(jnp.arange(GRID_W, dtype=F32), rows)
    n_freq = HEAD_DIM // 4
    inv_freq = ROPE_BASE ** (-jnp.arange(n_freq, dtype=F32) / n_freq)
    ang = jnp.concatenate([row[:, None] * inv_freq, col[:, None] * inv_freq], axis=-1)
    cos, sin = jnp.cos(ang), jnp.sin(ang)
    return jnp.concatenate([cos, cos], axis=-1), jnp.concatenate([-sin, sin], axis=-1)


def kernel(x, c, ctx, c_ctx, ada_w, ada_b, norm_mix, norm_ffn, ab_w_in, ab_q_gain, ab_k_gain, ab_sink, ab_conv_w, ab_conv_b, ab_gate_a_w, ab_gate_a_b, ab_gate_x_w, ab_gate_x_b, ab_lru_lambda, ab_w_out, gqa_w_in, gqa_q_gain, gqa_k_gain, gqa_w_out, router_w, router_bias, moe_w_gate, moe_w_up, moe_w_down):
    batch, seq, d = x.shape
    ctx_len = ctx.shape[1]
    depth = ada_w.shape[0]
    assert depth == 2 and batch < SUBLANES
    n_lat = batch * seq
    n_ctx = batch * ctx_len
    ctx_row = batch

    xl = x.reshape(n_lat, d)
    xc = ctx.reshape(n_ctx, d)
    cc = jnp.zeros((SUBLANES, d), F32).at[:batch].set(c).at[ctx_row].set(c_ctx)
    mod = _ada(cc, ada_w, ada_b)
    cos2, sin2 = _rope_tables(seq)
    rw_t = router_w.T.astype(BF16)
    rb = router_bias.reshape(N_EXPERTS, 1).astype(F32)
    experts = (moe_w_gate, moe_w_up, moe_w_down)

    mod3 = mod[0].reshape(SUBLANES, 1, 6 * d)
    w_in = ab_w_in[0].astype(BF16)
    proj = _norm_mod_matmul(xl, norm_mix[0], mod3, 0, 1, w_in, seq, 0,
                            A_Q_HEADS, A_KV_HEADS, ab_q_gain[0], ab_k_gain[0], cos2, sin2)
    proj_c = _norm_mod_matmul(xc, norm_mix[0], mod3, 0, 1, w_in, n_ctx, ctx_row,
                              A_Q_HEADS, A_KV_HEADS, ab_q_gain[0], ab_k_gain[0])
    att = _win_attn(ab_sink[0], proj, proj_c, batch, seq, ctx_len)
    att_c = _ctx_attn(ab_sink[0], proj_c, batch, ctx_len)

    lru_w = ab_conv_w.shape[2]
    c0 = (A_Q_HEADS + 2 * A_KV_HEADS) * HEAD_DIM
    y_p, yc_p = _lru(_to_chunked(proj[:, c0:c0 + lru_w], batch),
                     _to_chunked(proj[:, c0 + lru_w:c0 + 2 * lru_w], batch),
                     _to_chunked(proj_c[:, c0:c0 + lru_w], batch),
                     _to_chunked(proj_c[:, c0 + lru_w:c0 + 2 * lru_w], batch),
                     ab_conv_w[0], ab_conv_b[0], ab_gate_a_w[0], ab_gate_a_b[0],
                     ab_gate_x_w[0], ab_gate_x_b[0], ab_lru_lambda[0])
    w_out = ab_w_out[0].astype(BF16)
    xl = _out_proj(att, 0, _from_chunked(y_p), 0, w_out, xl, mod3, 2, seq, 0)
    xc = _out_proj(att_c, 0, _from_chunked(yc_p), 0, w_out, xc, mod3, 2, n_ctx, ctx_row)
    xl, xc = _moe(xl, xc, norm_ffn[0], mod3, rw_t, rb, *experts, 0, seq, ctx_row)

    mod3 = mod[1].reshape(SUBLANES, 1, 6 * d)
    w_in = gqa_w_in[0].astype(BF16)
    cw = C_Q_HEADS * HEAD_DIM
    proj = _norm_mod_matmul(xl, norm_mix[1], mod3, 0, 1, w_in, seq, 0,
                            C_Q_HEADS, C_KV_HEADS, gqa_q_gain[0], gqa_k_gain[0], cos2, sin2)
    proj_c = _norm_mod_matmul(xc, norm_mix[1], mod3, 0, 1, w_in[:, cw:], n_ctx, ctx_row,
                              0, C_KV_HEADS, gqa_q_gain[0], gqa_k_gain[0])
    att = _dense_attn(proj, proj_c, batch, seq, ctx_len)
    xl = _out_proj(att, 0, att, 1, gqa_w_out[0].astype(BF16), xl, mod3, 2, seq, 0)
    xl, _ = _moe(xl, None, norm_ffn[1], mod3, rw_t, rb, *experts, 1, seq, ctx_row)
    return xl.reshape(batch, seq, d)
```

```python
import functools

import jax
import jax.numpy as jnp
import numpy as np
from jax import lax
from jax.experimental import pallas as pl
from jax.experimental.pallas import tpu as pltpu
from jax.experimental.pallas import tpu_sc as plsc

F32 = jnp.float32
BF16 = jnp.bfloat16

LANES = 128
SUBLANES = 8
VMEM_LIMIT = 56 * 1024 * 1024

HEAD_DIM = 128
GRID_W = 64
WINDOW = 128
BLOCK = 128
ROPE_BASE = 10000.0
EPS = 1e-6
ATTN_SCALE = HEAD_DIM ** -0.5
A_Q_HEADS, A_KV_HEADS = 8, 2
C_Q_HEADS, C_KV_HEADS = 16, 4
GQA_GROUP = 4
LRU_C = 8.0
CONV_W = 4
CONV_LEFT = 2
N_EXPERTS = 16
N_GROUPS = 4
EXPERTS_PER_GROUP = 4
NEG_BIG = -1e30

LRU_CHUNKS = SUBLANES
LRU_JB = 16
LRU_LANE_BLOCKS = 2
MOE_TM = 256
MOE_RANGES = 1
SC_CORES = 2
SC_WORKERS = 32
SC_GATHER_WINDOW = 32
SC_GATHER_BUFFERS = 3
SC_GATHER_SUB = 8


def _cparams(sem, vmem=VMEM_LIMIT):
    return pltpu.CompilerParams(dimension_semantics=sem, vmem_limit_bytes=vmem)


def _dot(a, b):
    return jnp.dot(a, b, preferred_element_type=F32)


def _dot_nt(a, b):
    return lax.dot_general(a, b, (((1,), (1,)), ((), ())), preferred_element_type=F32)


def _ada_kernel(c_ref, w_ref, b_ref, o_ref):
    c = c_ref[...]
    s = (c * jax.nn.sigmoid(c)).astype(BF16)
    o_ref[0] = _dot(s, w_ref[0].astype(BF16)) + b_ref[0]


def _ada(cc, ada_w, ada_b):
    depth, d, n = ada_w.shape
    tn = 1024
    return pl.pallas_call(
        _ada_kernel,
        out_shape=jax.ShapeDtypeStruct((depth, SUBLANES, n), F32),
        grid=(depth, n // tn),
        in_specs=[pl.BlockSpec((SUBLANES, d), lambda l, j: (0, 0)),
                  pl.BlockSpec((1, d, tn), lambda l, j: (l, 0, j)),
                  pl.BlockSpec((1, 1, tn), lambda l, j: (l, 0, j))],
        out_specs=pl.BlockSpec((1, SUBLANES, tn), lambda l, j: (l, 0, j)),
        compiler_params=_cparams(("arbitrary", "arbitrary")),
        name="ada",
    )(cc, ada_w, ada_b.reshape(depth, 1, n))


def _norm_mod_rows(x_ref, g_ref, sh_ref, sc_ref, dst_ref, tm, rc=128):
    g = g_ref[...]
    sc1 = 1.0 + sc_ref[...]
    sh = sh_ref[...]

    def body(r, carry):
        rows = pl.ds(pl.multiple_of(r * rc, rc), rc)
        xf = x_ref[rows, :]
        ms = jnp.mean(xf * xf, axis=-1, keepdims=True)
        xn = (xf * lax.rsqrt(ms + EPS)) * g
        dst_ref[rows, :] = (xn * sc1 + sh).astype(dst_ref.dtype)
        return carry

    lax.fori_loop(0, tm // rc, body, 0)


def _nm_mm_kernel(*refs, tm, tn, n_q, n_k, rope):
    if rope:
        x_ref, g_ref, sh_ref, sc_ref, w_ref, qg_ref, kg_ref, cos_ref, sin_ref, o_ref, hn_ref = refs
    else:
        x_ref, g_ref, sh_ref, sc_ref, w_ref, qg_ref, kg_ref, o_ref, hn_ref = refs
    _norm_mod_rows(x_ref, g_ref, sh_ref, sc_ref, hn_ref, tm)
    h = hn_ref[...]
    heads_per_chunk = tn // HEAD_DIM
    for j in range(w_ref.shape[1] // tn):
        cols = slice(j * tn, (j + 1) * tn)
        y = _dot(h, w_ref[:, cols])
        parts = []
        for hh in range(heads_per_chunk):
            head = j * heads_per_chunk + hh
            yh = y[:, _head_cols(hh)]
            if head < n_q + n_k:
                gain = qg_ref[...] if head < n_q else kg_ref[...]
                ms = jnp.mean(yh * yh, axis=-1, keepdims=True)
                yh = (yh * lax.rsqrt(ms + EPS)) * gain
                if rope:
                    yh = yh * cos_ref[...] + pltpu.roll(yh, HEAD_DIM // 2, 1) * sin_ref[...]
                if head < n_q:
                    yh = yh * ATTN_SCALE
            parts.append(yh.astype(o_ref.dtype))
        o_ref[:, cols] = jnp.concatenate(parts, axis=1)


def _norm_mod_matmul(x, g, mod3, shift_blk, scale_blk, w, rows_per_mod, mod_base,
                     n_q, n_k, q_gain, k_gain, cos2=None, sin2=None, tm=512, tn=512):
    n, d = x.shape
    nout = w.shape[1]
    tm = min(tm, n)
    tn = min(tn, nout)
    per = rows_per_mod // tm
    rope = cos2 is not None

    def mod_row(i):
        return mod_base + i // per

    head_vec = pl.BlockSpec((1, HEAD_DIM), lambda i: (0, 0))
    in_specs = [pl.BlockSpec((tm, d), lambda i: (i, 0)),
                pl.BlockSpec((1, d), lambda i: (0, 0)),
                pl.BlockSpec((None, 1, d), lambda i: (mod_row(i), 0, shift_blk)),
                pl.BlockSpec((None, 1, d), lambda i: (mod_row(i), 0, scale_blk)),
                pl.BlockSpec((d, nout), lambda i: (0, 0), pipeline_mode=pl.Buffered(1)),
                head_vec, head_vec]
    args = [x, g.reshape(1, d), mod3, mod3, w, q_gain.reshape(1, HEAD_DIM), k_gain.reshape(1, HEAD_DIM)]
    if rope:
        tiles_per_seq = cos2.shape[0] // tm
        table = pl.BlockSpec((tm, HEAD_DIM), lambda i: (i % tiles_per_seq, 0))
        in_specs += [table, table]
        args += [cos2, sin2]
    return pl.pallas_call(
        functools.partial(_nm_mm_kernel, tm=tm, tn=tn, n_q=n_q, n_k=n_k, rope=rope),
        out_shape=jax.ShapeDtypeStruct((n, nout), BF16),
        grid=(n // tm,),
        in_specs=in_specs,
        out_specs=pl.BlockSpec((tm, nout), lambda i: (i, 0)),
        scratch_shapes=[pltpu.VMEM((tm, d), BF16)],
        compiler_params=_cparams(("parallel",)),
        name="norm_mod_matmul",
    )(*args)


def _head_cols(h):
    return slice(h * HEAD_DIM, (h + 1) * HEAD_DIM)


def _stack_group(q_ref, kvh):
    return jnp.concatenate([q_ref[:, _head_cols(kvh * GQA_GROUP + g)] for g in range(GQA_GROUP)], axis=0)


def _sink_col(sink_ref, kvh, rows):
    return jnp.concatenate([jnp.full((rows, 1), sink_ref[kvh * GQA_GROUP + g], F32)
                            for g in range(GQA_GROUP)], axis=0)


def _band_bias(ctx_len):
    rows, nk = GQA_GROUP * BLOCK, 3 * BLOCK + ctx_len
    qi = np.arange(rows)[:, None] % BLOCK
    kj = np.arange(nk)[None, :]
    inner = (kj >= 3 * BLOCK) | (np.abs(kj - BLOCK - qi) <= WINDOW)
    first = inner & ~(kj < BLOCK)
    last = inner & ~((kj >= 2 * BLOCK) & (kj < 3 * BLOCK))
    return np.where(np.stack([first, inner, last]), 0.0, NEG_BIG).astype(np.float32)


def _win_attn_kernel(sink_ref, bias_ref, q_ref, kp_ref, kc_ref, kn_ref, vp_ref, vc_ref, vn_ref,
                     kx_ref, vx_ref, o_ref, *, ctx_len):
    nk = 3 * BLOCK + ctx_len
    bias = bias_ref[...]
    ones = jnp.ones((nk, HEAD_DIM), BF16)
    for kvh in range(A_KV_HEADS):
        cols = _head_cols(kvh)
        q4 = _stack_group(q_ref, kvh)
        ka = jnp.concatenate([kp_ref[:, cols], kc_ref[:, cols], kn_ref[:, cols], kx_ref[:, cols]], axis=0)
        va = jnp.concatenate([vp_ref[:, cols], vc_ref[:, cols], vn_ref[:, cols], vx_ref[:, cols]], axis=0)
        s = _dot_nt(q4, ka) + bias
        sk = _sink_col(sink_ref, kvh, BLOCK)
        m = jnp.maximum(jnp.max(s, axis=-1, keepdims=True), sk)
        p = jnp.exp((s - m).astype(BF16))
        acc = _dot(p, jnp.concatenate([va, ones], axis=1))
        o = acc[:, 0:HEAD_DIM] / (acc[:, HEAD_DIM:] + jnp.exp(sk - m))
        for g in range(GQA_GROUP):
            o_ref[:, _head_cols(kvh * GQA_GROUP + g)] = o[g * BLOCK:(g + 1) * BLOCK].astype(o_ref.dtype)


def _win_attn(sink, proj, proj_c, batch, seq, ctx_len):
    nb = seq // BLOCK
    assert nb >= 2
    kvw = A_KV_HEADS * HEAD_DIM
    k_blk = A_Q_HEADS * HEAD_DIM // kvw
    v_blk = k_blk + 1
    bias = jnp.asarray(_band_bias(ctx_len))

    def which_bias(b, n):
        return (jnp.where(n == 0, 0, jnp.where(n == nb - 1, 2, 1)), 0, 0)

    def prev(b, n):
        return b * nb + jnp.maximum(n - 1, 0)

    def cur(b, n):
        return b * nb + n

    def nxt(b, n):
        return b * nb + jnp.minimum(n + 1, nb - 1)

    return pl.pallas_call(
        functools.partial(_win_attn_kernel, ctx_len=ctx_len),
        out_shape=jax.ShapeDtypeStruct((batch * seq, A_Q_HEADS * HEAD_DIM), BF16),
        grid=(batch, nb),
        in_specs=[pl.BlockSpec(memory_space=pltpu.SMEM),
                  pl.BlockSpec((None,) + bias.shape[1:], which_bias),
                  pl.BlockSpec((BLOCK, A_Q_HEADS * HEAD_DIM), lambda b, n: (cur(b, n), 0)),
                  pl.BlockSpec((BLOCK, kvw), lambda b, n: (prev(b, n), k_blk)),
                  pl.BlockSpec((BLOCK, kvw), lambda b, n: (cur(b, n), k_blk)),
                  pl.BlockSpec((BLOCK, kvw), lambda b, n: (nxt(b, n), k_blk)),
                  pl.BlockSpec((BLOCK, kvw), lambda b, n: (prev(b, n), v_blk)),
                  pl.BlockSpec((BLOCK, kvw), lambda b, n: (cur(b, n), v_blk)),
                  pl.BlockSpec((BLOCK, kvw), lambda b, n: (nxt(b, n), v_blk)),
                  pl.BlockSpec((ctx_len, kvw), lambda b, n: (b, k_blk)),
                  pl.BlockSpec((ctx_len, kvw), lambda b, n: (b, v_blk))],
        out_specs=pl.BlockSpec((BLOCK, A_Q_HEADS * HEAD_DIM), lambda b, n: (cur(b, n), 0)),
        compiler_params=_cparams(("parallel", "parallel")),
        name="win_attn",
    )(sink, bias, proj, proj, proj, proj, proj, proj, proj, proj_c, proj_c)


def _ctx_attn_kernel(sink_ref, q_ref, k_ref, v_ref, o_ref, *, ctx_len):
    kvh = pl.program_id(1)
    q4 = jnp.concatenate([q_ref[:, _head_cols(g)] for g in range(GQA_GROUP)], axis=0)
    s = _dot_nt(q4, k_ref[...])
    sk = jnp.concatenate([jnp.full((ctx_len, 1), sink_ref[kvh * GQA_GROUP + g], F32)
                          for g in range(GQA_GROUP)], axis=0)
    m = jnp.maximum(jnp.max(s, axis=-1, keepdims=True), sk)
    p = jnp.exp(s - m)
    den = jnp.sum(p, axis=-1, keepdims=True) + jnp.exp(sk - m)
    o = _dot(p.astype(BF16), v_ref[...]) / den
    for g in range(GQA_GROUP):
        o_ref[:, _head_cols(g)] = o[g * ctx_len:(g + 1) * ctx_len].astype(o_ref.dtype)


def _ctx_attn(sink, proj_c, batch, ctx_len):
    k_blk = A_Q_HEADS
    v_blk = A_Q_HEADS + A_KV_HEADS
    gw = GQA_GROUP * HEAD_DIM
    return pl.pallas_call(
        functools.partial(_ctx_attn_kernel, ctx_len=ctx_len),
        out_shape=jax.ShapeDtypeStruct((batch * ctx_len, A_Q_HEADS * HEAD_DIM), BF16),
        grid=(batch, A_KV_HEADS),
        in_specs=[pl.BlockSpec(memory_space=pltpu.SMEM),
                  pl.BlockSpec((ctx_len, gw), lambda b, h: (b, h)),
                  pl.BlockSpec((ctx_len, HEAD_DIM), lambda b, h: (b, k_blk + h)),
                  pl.BlockSpec((ctx_len, HEAD_DIM), lambda b, h: (b, v_blk + h))],
        out_specs=pl.BlockSpec((ctx_len, gw), lambda b, h: (b, h)),
        compiler_params=_cparams(("parallel", "parallel")),
        name="ctx_attn",
    )(sink, proj_c, proj_c, proj_c)


def _sigmoid(x):
    return 0.5 * jnp.tanh(0.5 * x) + 0.5


def _gelu_tanh(x):
    return 0.5 * x * (1.0 + jnp.tanh(0.7978845608028654 * (x + 0.044715 * (x * x * x))))


def _lru_sequence(x_ref, xg_ref, y_ref, xp_ref, a_ref, b_ref, w, init, rows, jb):
    conv_w, conv_b, wa, ba, wx, bx, c_logsig = w
    width = x_ref.shape[-1]
    sub = lax.broadcasted_iota(jnp.int32, (1, SUBLANES, width), 1)

    def block_diag(ub, wd):
        return jnp.concatenate([_dot(ub[:, _head_cols(n)], wd[n]) for n in range(width // LANES)], axis=1)

    def fill(r, carry):
        rr = pl.ds(pl.multiple_of(r * jb, jb), jb)
        xp_ref[pl.ds(pl.multiple_of(r * jb, jb) + CONV_LEFT, jb)] = x_ref[rr].astype(F32)
        return carry

    lax.fori_loop(0, rows // jb, fill, 0)
    tail = x_ref[rows - CONV_LEFT:rows].astype(F32)
    xp_ref[0:CONV_LEFT] = jnp.where(sub == 0, 0.0, pltpu.roll(tail, 1, 1))
    head = x_ref[0:1].astype(F32)
    xp_ref[rows + CONV_LEFT:rows + CONV_LEFT + 1] = jnp.where(
        sub == SUBLANES - 1, 0.0, pltpu.roll(head, SUBLANES - 1, 1))

    def gates(r, carry):
        j0 = pl.multiple_of(r * jb, jb)
        u = conv_b
        for k in range(CONV_W):
            u = u + conv_w[k] * xp_ref[pl.ds(j0 + k, jb)]
        u2 = u.reshape(jb * SUBLANES, width)
        ub = u2.astype(BF16)
        for d in range(2):
            r_gate = _sigmoid(block_diag(ub, wa[d]) + ba[d])
            i_gate = _sigmoid(block_diag(ub, wx[d]) + bx[d])
            log_a = c_logsig[d] * r_gate
            a = jnp.exp(log_a)
            b = jnp.sqrt(1.0 - a * a) * (i_gate * u2)
            a_ref[d, pl.ds(j0, jb)] = a.reshape(jb, SUBLANES, width)
            b_ref[d, pl.ds(j0, jb)] = b.reshape(jb, SUBLANES, width)
        return carry

    lax.fori_loop(0, rows // jb, gates, 0, unroll=2)

    def scan(j, carry):
        hf, pf, hb, pb = carry
        jr = rows - 1 - j
        af = a_ref[0, j]
        hf = af * hf + b_ref[0, j]
        pf = pf * af
        b_ref[0, j] = hf
        a_ref[0, j] = pf
        ab = a_ref[1, jr]
        hb = ab * hb + b_ref[1, jr]
        pb = pb * ab
        b_ref[1, jr] = hb
        a_ref[1, jr] = pb
        return hf, pf, hb, pb

    z = jnp.zeros((SUBLANES, width), F32)
    o = jnp.ones((SUBLANES, width), F32)
    lax.fori_loop(0, rows, scan, (z, o, z, o), unroll=8)

    hf_last, pf_last = b_ref[0, rows - 1], a_ref[0, rows - 1]
    hb_last, pb_last = b_ref[1, 0], a_ref[1, 0]
    s = init[0]
    carry_f = []
    for c in range(SUBLANES):
        carry_f.append(s)
        s = hf_last[c:c + 1] + pf_last[c:c + 1] * s
    out_f = s
    s = init[1]
    carry_b = [None] * SUBLANES
    for c in reversed(range(SUBLANES)):
        carry_b[c] = s
        s = hb_last[c:c + 1] + pb_last[c:c + 1] * s
    out_b = s
    cf = jnp.concatenate(carry_f, axis=0)
    cb = jnp.concatenate(carry_b, axis=0)

    def emit(r, carry):
        rr = pl.ds(pl.multiple_of(r * jb, jb), jb)
        h = (b_ref[0, rr] + a_ref[0, rr] * cf) + (b_ref[1, rr] + a_ref[1, rr] * cb)
        y_ref[rr] = (h * _gelu_tanh(xg_ref[rr].astype(F32))).astype(y_ref.dtype)
        return carry

    lax.fori_loop(0, rows // jb, emit, 0)
    return out_f, out_b


def _lru_kernel(xr_ref, xg_ref, xrc_ref, xgc_ref, cw_ref, cb_ref, wa_ref, ba_ref, wx_ref, bx_ref,
                lam_ref, y_ref, yc_ref, xp_ref, a_ref, b_ref, *, rows, rows_c):
    c_logsig = [LRU_C * jax.nn.log_sigmoid(lam_ref[d]) for d in range(2)]
    w = ([cw_ref[k] for k in range(CONV_W)], cb_ref[0],
         [wa_ref[d] for d in range(2)], [ba_ref[d] for d in range(2)],
         [wx_ref[d] for d in range(2)], [bx_ref[d] for d in range(2)], c_logsig)
    zero = jnp.zeros((1, xr_ref.shape[-1]), F32)
    sf, sb = _lru_sequence(xrc_ref, xgc_ref, yc_ref, xp_ref, a_ref, b_ref, w, (zero, zero),
                           rows_c, min(LRU_JB, rows_c))
    _lru_sequence(xr_ref, xg_ref, y_ref, xp_ref, a_ref, b_ref, w, (sf, sb), rows, LRU_JB)


def _lru(xr, xg, xrc, xgc, conv_w, conv_b, wa, ba, wx, bx, lam):
    batch, rows, _, width = xr.shape
    rows_c = xrc.shape[1]
    cw = LRU_LANE_BLOCKS * LANES
    seq_spec = pl.BlockSpec((None, rows, SUBLANES, cw), lambda b, n: (b, 0, 0, n))
    ctx_spec = pl.BlockSpec((None, rows_c, SUBLANES, cw), lambda b, n: (b, 0, 0, n))
    vec2 = pl.BlockSpec((2, 1, cw), lambda b, n: (0, 0, n))
    mat2 = pl.BlockSpec((2, LRU_LANE_BLOCKS, LANES, LANES), lambda b, n: (0, n, 0, 0))
    return pl.pallas_call(
        functools.partial(_lru_kernel, rows=rows, rows_c=rows_c),
        out_shape=(jax.ShapeDtypeStruct(xr.shape, BF16), jax.ShapeDtypeStruct(xrc.shape, BF16)),
        grid=(batch, width // cw),
        in_specs=[seq_spec, seq_spec, ctx_spec, ctx_spec,
                  pl.BlockSpec((CONV_W, 1, cw), lambda b, n: (0, 0, n)),
                  pl.BlockSpec((1, 1, cw), lambda b, n: (0, 0, n)),
                  mat2, vec2, mat2, vec2, vec2],
        out_specs=(seq_spec, ctx_spec),
        scratch_shapes=[pltpu.VMEM((rows + CONV_W - 1, SUBLANES, cw), F32),
                        pltpu.VMEM((2, rows, SUBLANES, cw), F32),
                        pltpu.VMEM((2, rows, SUBLANES, cw), F32)],
        compiler_params=_cparams(("parallel", "parallel")),
        name="rglru",
    )(xr, xg, xrc, xgc, conv_w.reshape(CONV_W, 1, width), conv_b.reshape(1, 1, width),
      wa.astype(BF16), ba.reshape(2, 1, width), wx.astype(BF16), bx.reshape(2, 1, width),
      lam.reshape(2, 1, width))


def _to_chunked(a, batch):
    t = a.shape[0] // batch
    return a.reshape(batch, LRU_CHUNKS, t // LRU_CHUNKS, a.shape[1]).transpose(0, 2, 1, 3)


def _from_chunked(a):
    b, r, c, w = a.shape
    return a.transpose(0, 2, 1, 3).reshape(b * r * c, w)


def _out_proj_kernel(a1_ref, a2_ref, w1_ref, w2_ref, x_ref, g_ref, o_ref):
    y = _dot(a1_ref[...], w1_ref[...]) + _dot(a2_ref[...], w2_ref[...])
    o_ref[...] = x_ref[...] + g_ref[...] * y


def _out_proj(a1, a1_blk, a2, a2_blk, w, x, mod3, gate_blk, rows_per_mod, mod_base, tm=2048, tn=512):
    n, d = x.shape
    kh = w.shape[0] // 2
    tm = min(tm, n)
    per = rows_per_mod // tm
    gpb = d // tn
    return pl.pallas_call(
        _out_proj_kernel,
        out_shape=jax.ShapeDtypeStruct((n, d), F32),
        grid=(n // tm, d // tn),
        in_specs=[pl.BlockSpec((tm, kh), lambda i, j: (i, a1_blk)),
                  pl.BlockSpec((tm, kh), lambda i, j: (i, a2_blk)),
                  pl.BlockSpec((kh, tn), lambda i, j: (0, j)),
                  pl.BlockSpec((kh, tn), lambda i, j: (1, j)),
                  pl.BlockSpec((tm, tn), lambda i, j: (i, j)),
                  pl.BlockSpec((None, 1, tn), lambda i, j: (mod_base + i // per, 0, gate_blk * gpb + j))],
        out_specs=pl.BlockSpec((tm, tn), lambda i, j: (i, j)),
        compiler_params=_cparams(("parallel", "parallel")),
        name="out_proj",
    )(a1, a2, w, w, x, mod3)


def _lane_block_max(s):
    mm = s[:, 0:LANES]
    for t in range(1, s.shape[1] // LANES):
        mm = jnp.maximum(mm, s[:, t * LANES:(t + 1) * LANES])
    return mm


def _exp_blocks(s, mrep):
    return jnp.concatenate(
        [jnp.exp((s[:, t * LANES:(t + 1) * LANES] - mrep).astype(BF16)) for t in range(s.shape[1] // LANES)],
        axis=1)


def _copy_key_rows(dst_ref, c, kc, lat_ref, ctx_ref, cols):
    seq = lat_ref.shape[0]
    lo, hi = c * kc, (c + 1) * kc
    if lo < seq:
        n = min(hi, seq) - lo
        dst_ref[c, 0:n, cols] = lat_ref[lo:lo + n, :]
    if hi > seq:
        start = max(lo, seq)
        dst_ref[c, start - lo:kc, cols] = ctx_ref[start - seq:hi - seq, :]


def _dense_attn_kernel(q_ref, k_ref, v_ref, kx_ref, vx_ref, o_ref,
                       s_ref, m_ref, acc_ref, ka_ref, va_ref, *, tq, n_chunks, kc):
    @pl.when(pl.program_id(2) == 0)
    def _():
        for c in range(n_chunks):
            _copy_key_rows(ka_ref, c, kc, k_ref, kx_ref, slice(0, HEAD_DIM))
            _copy_key_rows(va_ref, c, kc, v_ref, vx_ref, slice(0, HEAD_DIM))
            va_ref[c, :, HEAD_DIM:] = jnp.ones((kc, HEAD_DIM), BF16)

    q4 = jnp.concatenate([q_ref[:, _head_cols(g)] for g in range(GQA_GROUP)], axis=0)
    m_ref[...] = jnp.full(m_ref.shape, NEG_BIG, F32)

    def sweep1(c, carry):
        s = _dot_nt(q4, ka_ref[c])
        s_ref[c] = s
        m_ref[...] = jnp.maximum(m_ref[...], _lane_block_max(s))
        return carry

    lax.fori_loop(0, n_chunks, sweep1, 0)
    m_ref[...] = jnp.broadcast_to(jnp.max(m_ref[...], axis=-1, keepdims=True), m_ref.shape)
    acc_ref[...] = jnp.zeros(acc_ref.shape, F32)

    def sweep2(c, carry):
        acc_ref[...] += _dot(_exp_blocks(s_ref[c], m_ref[...]), va_ref[c])
        return carry

    lax.fori_loop(0, n_chunks, sweep2, 0)
    o = acc_ref[:, 0:HEAD_DIM] / acc_ref[:, HEAD_DIM:]
    for g in range(GQA_GROUP):
        o_ref[:, _head_cols(g)] = o[g * tq:(g + 1) * tq].astype(o_ref.dtype)


def _dense_attn(proj, proj_c, batch, seq, ctx_len, tq=256, n_chunks=2):
    gw = GQA_GROUP * HEAD_DIM
    nq = seq // tq
    rows = GQA_GROUP * tq
    kc = (seq + ctx_len) // n_chunks
    assert kc * n_chunks == seq + ctx_len and kc % LANES == 0
    k_blk = C_Q_HEADS
    v_blk = C_Q_HEADS + C_KV_HEADS
    vx_blk = C_KV_HEADS
    return pl.pallas_call(
        functools.partial(_dense_attn_kernel, tq=tq, n_chunks=n_chunks, kc=kc),
        out_shape=jax.ShapeDtypeStruct((batch * seq, C_Q_HEADS * HEAD_DIM), BF16),
        grid=(batch, C_KV_HEADS, nq),
        in_specs=[pl.BlockSpec((tq, gw), lambda b, h, i: (b * nq + i, h)),
                  pl.BlockSpec((seq, HEAD_DIM), lambda b, h, i: (b, k_blk + h)),
                  pl.BlockSpec((seq, HEAD_DIM), lambda b, h, i: (b, v_blk + h)),
                  pl.BlockSpec((ctx_len, HEAD_DIM), lambda b, h, i: (b, h)),
                  pl.BlockSpec((ctx_len, HEAD_DIM), lambda b, h, i: (b, vx_blk + h))],
        out_specs=pl.BlockSpec((tq, gw), lambda b, h, i: (b * nq + i, h)),
        scratch_shapes=[pltpu.VMEM((n_chunks, rows, kc), F32),
                        pltpu.VMEM((rows, LANES), F32),
                        pltpu.VMEM((rows, 2 * HEAD_DIM), F32),
                        pltpu.VMEM((n_chunks, kc, HEAD_DIM), BF16),
                        pltpu.VMEM((n_chunks, kc, 2 * HEAD_DIM), BF16)],
        compiler_params=_cparams(("parallel", "parallel", "arbitrary")),
        name="dense_attn",
    )(proj, proj, proj, proj_c, proj_c)


def _router_rows(biased, scores):
    v = [biased[e:e + 1, :] for e in range(N_EXPERTS)]
    s = [scores[e:e + 1, :] for e in range(N_EXPERTS)]

    def top2_sum(vals):
        best = vals[0] + vals[1]
        for i in range(len(vals)):
            for j in range(i + 1, len(vals)):
                if (i, j) != (0, 1):
                    best = jnp.maximum(best, vals[i] + vals[j])
        return best

    gsum = [top2_sum(v[g * EXPERTS_PER_GROUP:(g + 1) * EXPERTS_PER_GROUP]) for g in range(N_GROUPS)]
    sel = jnp.zeros_like(gsum[0], dtype=jnp.int32)
    best = gsum[0]
    for g in range(1, N_GROUPS):
        take = gsum[g] > best
        sel = jnp.where(take, g, sel)
        best = jnp.where(take, gsum[g], best)

    def pick_group(rows, i):
        out = rows[i]
        for g in range(1, N_GROUPS):
            out = jnp.where(sel == g, rows[g * EXPERTS_PER_GROUP + i], out)
        return out

    cand = [pick_group(v, i) for i in range(EXPERTS_PER_GROUP)]
    cand_s = [pick_group(s, i) for i in range(EXPERTS_PER_GROUP)]
    i1 = jnp.zeros_like(sel)
    b1 = cand[0]
    for i in range(1, EXPERTS_PER_GROUP):
        take = cand[i] > b1
        i1 = jnp.where(take, i, i1)
        b1 = jnp.where(take, cand[i], b1)
    i2 = jnp.full_like(sel, -1)
    b2 = jnp.full_like(b1, -jnp.inf)
    for i in range(EXPERTS_PER_GROUP):
        take = (i1 != i) & ((cand[i] > b2) | (i2 < 0))
        i2 = jnp.where(take, i, i2)
        b2 = jnp.where(take, cand[i], b2)

    def pick_idx(rows, idx):
        out = rows[0]
        for i in range(1, EXPERTS_PER_GROUP):
            out = jnp.where(idx == i, rows[i], out)
        return out

    s0 = pick_idx(cand_s, i1)
    s1 = pick_idx(cand_s, i2)
    tot = s0 + s1
    e0 = (sel * EXPERTS_PER_GROUP + i1).astype(F32)
    e1 = (sel * EXPERTS_PER_GROUP + i2).astype(F32)
    return e0, e1, s0 / tot, s1 / tot


def _norm_router_kernel(*refs, tm, n_lat_tiles, has_ctx):
    i = pl.program_id(0)
    if has_ctx:
        x_ref, xc_ref, g_ref, sh_ref, sc_ref, rw_ref, rb_ref, hp_ref, r_ref, h_ref, tri_ref, run_ref = refs

        @pl.when(i < n_lat_tiles)
        def _():
            _norm_mod_rows(x_ref, g_ref, sh_ref, sc_ref, h_ref, tm)

        @pl.when(i >= n_lat_tiles)
        def _():
            _norm_mod_rows(xc_ref, g_ref, sh_ref, sc_ref, h_ref, tm)
    else:
        x_ref, g_ref, sh_ref, sc_ref, rw_ref, rb_ref, hp_ref, r_ref, h_ref, tri_ref, run_ref = refs
        _norm_mod_rows(x_ref, g_ref, sh_ref, sc_ref, h_ref, tm)

    @pl.when(i == 0)
    def _():
        run_ref[...] = jnp.zeros(run_ref.shape, F32)
        before = lax.broadcasted_iota(jnp.int32, (tm, tm), 0) <= lax.broadcasted_iota(jnp.int32, (tm, tm), 1)
        tri_ref[...] = jnp.where(before, 1.0, 0.0).astype(BF16)

    def pack(r, carry):
        rows = pl.ds(pl.multiple_of(r * 128, 128), 128)
        hp_ref[rows, :] = _pack_halves(h_ref[rows, :])
        return carry

    lax.fori_loop(0, tm // 128, pack, 0)
    logits = _dot_nt(rw_ref[...], h_ref[...])
    scores = jax.nn.sigmoid(logits)
    e0, e1, w0, w1 = _router_rows(scores + rb_ref[...], scores)
    expert_ids = lax.broadcasted_iota(jnp.int32, (N_EXPERTS, tm), 0).astype(F32)
    ranks = []
    for e_row in (e0, e1):
        hit = expert_ids == e_row
        seen = _dot(jnp.where(hit, 1.0, 0.0).astype(BF16), tri_ref[...])
        ranks.append(jnp.sum(jnp.where(hit, seen - 1.0 + run_ref[...], 0.0), axis=0, keepdims=True))
        run_ref[...] = run_ref[...] + seen[:, tm - 1:tm]
    zero = jnp.zeros_like(w0)
    r_ref[...] = jnp.concatenate([e0, e1, w0, w1, ranks[0], ranks[1], zero, zero], axis=0)


def _pack_halves(v):
    c = v.shape[1] // 2
    lo = lax.bitcast_convert_type(v[:, :c].astype(BF16).astype(F32), jnp.uint32)
    hi = lax.bitcast_convert_type(v[:, c:].astype(BF16).astype(F32), jnp.uint32)
    return (lo >> 16) | (hi & jnp.uint32(0xFFFF0000))


def _unpack_halves(p):
    lo = lax.bitcast_convert_type(p << 16, F32)
    hi = lax.bitcast_convert_type(p & jnp.uint32(0xFFFF0000), F32)
    return lo, hi


def _sc_gather_rows(table, idx):
    n, w = idx.shape[0], table.shape[1]
    win, nb, sub = SC_GATHER_WINDOW, SC_GATHER_BUFFERS, SC_GATHER_SUB
    per = n // (win * SC_WORKERS)
    assert per * win * SC_WORKERS == n and per >= 1
    mesh = plsc.VectorSubcoreMesh(core_axis_name="core", subcore_axis_name="subcore")

    @functools.partial(
        pl.kernel, out_type=jax.ShapeDtypeStruct((n, w), table.dtype), mesh=mesh, name="sc_gather_rows",
        scratch_types=([pltpu.VMEM((per * win,), jnp.int32)] + [pltpu.VMEM((win, w), table.dtype)] * nb
                       + [pltpu.SemaphoreType.DMA] * (2 * nb)))
    def gather(x_hbm, i_hbm, o_hbm, i_v, *rest):
        bufs, gsems, wsems = rest[:nb], rest[nb:2 * nb], rest[2 * nb:]
        wid = lax.axis_index("subcore") * SC_CORES + lax.axis_index("core")
        base = wid * (per * win)
        pltpu.sync_copy(i_hbm.at[pl.ds(base, per * win)], i_v)
        rot = (wid * per) // SC_WORKERS

        def row0(t):
            u = t + rot
            u = u - per * (u >= per).astype(jnp.int32)
            return pl.multiple_of(u * win, win)

        def gather_copy(t, b, s):
            return pltpu.make_async_copy(x_hbm.at[i_v.at[pl.ds(row0(t) + s * sub, sub)]],
                                         bufs[b].at[pl.ds(s * sub, sub)], gsems[b])

        def write_copy(t, b):
            return pltpu.make_async_copy(bufs[b], o_hbm.at[pl.ds(base + row0(t), win)], wsems[b])

        def start_gathers(t, b):
            for s in range(win // sub):
                gather_copy(t, b, s).start()

        def step(t, b):
            for s in range(win // sub):
                gather_copy(t, b, s).wait()
            write_copy(t, b).start()

            @pl.when(t >= 1)
            def _():
                write_copy(t - 1, (b - 1) % nb).wait()

            @pl.when(t + nb - 1 < per)
            def _():
                start_gathers(t + nb - 1, (b + nb - 1) % nb)

        for t in range(min(nb - 1, per)):
            start_gathers(t, t)

        @pl.loop(0, per // nb)
        def _(p):
            for j in range(nb):
                step(nb * p + j, j)

        for t in range(per - per % nb, per):
            step(t, t % nb)
        write_copy(per - 1, (per - 1) % nb).wait()

    return gather(table, idx)


def _lat_ctx_maps(n_lat, per, ctx_mod_row):
    def mod_row(i):
        return jnp.where(i < n_lat, i // per, ctx_mod_row)

    def lat(i):
        return (jnp.minimum(i, n_lat - 1), 0)

    def ctx(i):
        return (jnp.maximum(i - n_lat, 0), 0)

    return mod_row, lat, ctx


def _norm_router(x, cx, g, mod3, shift_blk, scale_blk, rw_t, rb, rows_per_mod, ctx_mod_row, tm=512):
    n, d = x.shape
    n_lat = n // tm
    has_ctx = cx is not None
    ntot = n_lat + (cx.shape[0] // tm if has_ctx else 0)
    mod_row, lat, ctx = _lat_ctx_maps(n_lat, rows_per_mod // tm, ctx_mod_row)
    row_specs = [pl.BlockSpec((tm, d), lat)] + ([pl.BlockSpec((tm, d), ctx)] if has_ctx else [])
    row_args = [x] + ([cx] if has_ctx else [])
    return pl.pallas_call(
        functools.partial(_norm_router_kernel, tm=tm, n_lat_tiles=n_lat, has_ctx=has_ctx),
        out_shape=(jax.ShapeDtypeStruct((ntot * tm, d // 2), jnp.uint32),
                   jax.ShapeDtypeStruct((SUBLANES, ntot * tm), F32)),
        grid=(ntot,),
        in_specs=row_specs + [
            pl.BlockSpec((1, d), lambda i: (0, 0)),
            pl.BlockSpec((None, 1, d), lambda i: (mod_row(i), 0, shift_blk)),
            pl.BlockSpec((None, 1, d), lambda i: (mod_row(i), 0, scale_blk)),
            pl.BlockSpec((N_EXPERTS, d), lambda i: (0, 0)),
            pl.BlockSpec((N_EXPERTS, 1), lambda i: (0, 0))],
        out_specs=(pl.BlockSpec((tm, d // 2), lambda i: (i, 0)),
                   pl.BlockSpec((SUBLANES, tm), lambda i: (0, i))),
        scratch_shapes=[pltpu.VMEM((tm, d), BF16), pltpu.VMEM((tm, tm), BF16), pltpu.VMEM((N_EXPERTS, 1), F32)],
        compiler_params=_cparams(("arbitrary",)),
        name="norm_router",
    )(*row_args, g.reshape(1, d), mod3, mod3, rw_t, rb)


def _cast_rows(src_ref, dst_ref, rb=256):
    def body(r, carry):
        rows = pl.ds(pl.multiple_of(r * rb, rb), rb)
        dst_ref[rows, :] = src_ref[0, rows, :].astype(dst_ref.dtype)
        return carry

    lax.fori_loop(0, dst_ref.shape[0] // rb, body, 0)


def _tile_state(te_ref, tv_ref, tile0):
    i = pl.program_id(0)
    t = tile0 + i
    live = tv_ref[t] > 0
    new_expert = (i == 0) | (te_ref[t] != te_ref[jnp.maximum(t - 1, 0)])
    return live, new_expert


def _gmm_kernel(te_ref, tv_ref, xs_ref, wg_ref, wu_ref, wd_ref, *rest, tile0):
    o_ref, wg_b, wu_b, wd_b = rest[-4:]
    live, new_expert = _tile_state(te_ref, tv_ref, tile0)

    @pl.when(live & new_expert)
    def _():
        _cast_rows(wg_ref, wg_b)
        _cast_rows(wu_ref, wu_b)
        _cast_rows(wd_ref, wd_b)

    @pl.when(live)
    def _():
        lo, hi = _unpack_halves(xs_ref[...])
        x = jnp.concatenate([lo.astype(BF16), hi.astype(BF16)], axis=1)
        gate = _dot(x, wg_b[...])
        up = _dot(x, wu_b[...])
        h1 = ((gate * jax.nn.sigmoid(gate)) * up).astype(BF16)
        o_ref[...] = _pack_halves(_dot(h1, wd_b[...]))

    @pl.when(jnp.logical_not(live))
    def _():
        o_ref[...] = jnp.zeros(o_ref.shape, o_ref.dtype)


def _grouped_mlp(tile_expert, tile_valid, xs_part, tile0, ys_prev, wg, wu, wd, layer, tm=MOE_TM):
    dpk = xs_part.shape[1]
    d, dff = wg.shape[2], wg.shape[3]
    n_tiles = tile_expert.shape[0]

    def expert_block(rows, cols):
        return pl.BlockSpec((None, 1, rows, cols), lambda i, te, tv: (layer, te[tile0 + i], 0, 0),
                            pipeline_mode=pl.Buffered(1))

    in_specs = [pl.BlockSpec((tm, dpk), lambda i, te, tv: (i, 0)),
                expert_block(d, dff), expert_block(d, dff), expert_block(dff, d)]
    args = [tile_expert, tile_valid, xs_part, wg, wu, wd]
    aliases = {}
    if ys_prev is not None:
        in_specs.append(pl.BlockSpec(memory_space=pl.ANY))
        args.append(ys_prev)
        aliases = {len(args) - 1: 0}
    grid_spec = pltpu.PrefetchScalarGridSpec(
        num_scalar_prefetch=2,
        grid=(xs_part.shape[0] // tm,),
        in_specs=in_specs,
        out_specs=pl.BlockSpec((tm, dpk), lambda i, te, tv: (tile0 + i, 0)),
        scratch_shapes=[pltpu.VMEM((d, dff), BF16), pltpu.VMEM((d, dff), BF16), pltpu.VMEM((dff, d), BF16)],
    )
    return pl.pallas_call(
        functools.partial(_gmm_kernel, tile0=tile0),
        out_shape=jax.ShapeDtypeStruct((n_tiles * tm, dpk), jnp.uint32),
        grid_spec=grid_spec,
        input_output_aliases=aliases,
        compiler_params=_cparams(("arbitrary",)),
        name="grouped_mlp",
    )(*args)


def _combine_kernel(*refs, n_lat_tiles, has_ctx):
    if has_ctx:
        x_ref, xc_ref, y0_ref, y1_ref, r_ref, g_ref, o_ref, oc_ref = refs
    else:
        x_ref, y0_ref, y1_ref, r_ref, g_ref, o_ref = refs
    w0 = r_ref[:, 2:3]
    w1 = r_ref[:, 3:4]
    lo0, hi0 = _unpack_halves(y0_ref[...])
    lo1, hi1 = _unpack_halves(y1_ref[...])
    f = g_ref[...] * jnp.concatenate([w0 * lo0 + w1 * lo1, w0 * hi0 + w1 * hi1], axis=1)
    if not has_ctx:
        o_ref[...] = x_ref[...] + f
        return
    i = pl.program_id(0)

    @pl.when(i < n_lat_tiles)
    def _():
        o_ref[...] = x_ref[...] + f

    @pl.when(i >= n_lat_tiles)
    def _():
        oc_ref[...] = xc_ref[...] + f


def _combine(x, cx, yg, route_cols, mod3, gate_blk, rows_per_mod, ctx_mod_row, tm=512):
    n, d = x.shape
    n_lat = n // tm
    has_ctx = cx is not None
    ntot = n_lat + (cx.shape[0] // tm if has_ctx else 0)
    mod_row, lat, ctx = _lat_ctx_maps(n_lat, rows_per_mod // tm, ctx_mod_row)
    row_specs = [pl.BlockSpec((tm, d), lat)] + ([pl.BlockSpec((tm, d), ctx)] if has_ctx else [])
    row_args = [x] + ([cx] if has_ctx else [])
    out_shape = [jax.ShapeDtypeStruct(x.shape, F32)] + ([jax.ShapeDtypeStruct(cx.shape, F32)] if has_ctx else [])
    out = pl.pallas_call(
        functools.partial(_combine_kernel, n_lat_tiles=n_lat, has_ctx=has_ctx),
        out_shape=tuple(out_shape),
        grid=(ntot,),
        in_specs=row_specs + [
            pl.BlockSpec((tm, d // 2), lambda i: (i, 0)),
            pl.BlockSpec((tm, d // 2), lambda i: (ntot + i, 0)),
            pl.BlockSpec((tm, SUBLANES), lambda i: (i, 0)),
            pl.BlockSpec((None, 1, d), lambda i: (mod_row(i), 0, gate_blk))],
        out_specs=tuple(row_specs),
        compiler_params=_cparams(("arbitrary",)),
        name="moe_combine",
    )(*row_args, yg, yg, route_cols, mod3)
    return out if has_ctx else (out[0], None)


def _dispatch_plan(route, tm):
    n = route.shape[1]
    e_flat = jnp.concatenate([route[0], route[1]]).astype(jnp.int32)
    rank = jnp.concatenate([route[4], route[5]]).astype(jnp.int32)
    n_assign = 2 * n
    n_tiles = n_assign // tm + N_EXPERTS
    experts = jnp.arange(N_EXPERTS, dtype=jnp.int32)[:, None]
    onehot = (experts == e_flat[None, :]).astype(jnp.int32)
    counts = jnp.sum(onehot, axis=1)
    padded = ((counts + tm - 1) // tm) * tm
    ends_p = jnp.cumsum(padded)
    starts_p = ends_p - padded
    starts_c = jnp.cumsum(counts) - counts
    dest = jnp.sum(onehot * starts_p[:, None], axis=0) + rank
    by_expert = jnp.sum(onehot * starts_c[:, None], axis=0) + rank
    order = jnp.argsort(by_expert).astype(jnp.int32)
    p = jnp.arange(n_tiles * tm, dtype=jnp.int32)[None, :]
    owner = ((p >= starts_p[:, None]) & (p < ends_p[:, None])).astype(jnp.int32)
    within = jnp.sum(owner * (p - starts_p[:, None]), axis=0)
    live = jnp.sum(owner * (p - starts_p[:, None] < counts[:, None]), axis=0) > 0
    compact = jnp.sum(owner * starts_c[:, None], axis=0) + within
    src_tok = jnp.where(live, order[jnp.clip(compact, 0, n_assign - 1)] % n, p[0] % n)
    tile_start = jnp.arange(n_tiles, dtype=jnp.int32) * tm
    tile_valid = (tile_start < ends_p[-1]).astype(jnp.int32)
    last_tile = jnp.maximum(ends_p[-1] // tm - 1, 0) * tm
    tile_expert = jnp.searchsorted(ends_p, jnp.minimum(tile_start, last_tile), side="right").astype(jnp.int32)
    tile_expert = jnp.minimum(tile_expert, N_EXPERTS - 1)
    return src_tok, dest, tile_expert, tile_valid


def _moe(x, cx, g, mod3, rw_t, rb, wg, wu, wd, layer, rows_per_mod, ctx_mod_row):
    h, route = _norm_router(x, cx, g, mod3, 3, 4, rw_t, rb, rows_per_mod, ctx_mod_row)
    src_tok, dest, tile_expert, tile_valid = _dispatch_plan(route, MOE_TM)
    n_tiles = tile_expert.shape[0]
    bounds = [n_tiles * k // MOE_RANGES for k in range(MOE_RANGES + 1)]
    ys = None
    for t0, t1 in zip(bounds[:-1], bounds[1:]):
        xs = _sc_gather_rows(h, src_tok[t0 * MOE_TM:t1 * MOE_TM])
        ys = _grouped_mlp(tile_expert, tile_valid, xs, t0, ys, wg, wu, wd, layer)
    yg = _sc_gather_rows(ys, dest)
    route_cols = route.T
    return _combine(x, cx, yg, route_cols, mod3, 5, rows_per_mod, ctx_mod_row)


def _rope_tables(seq):
    rows = seq // GRID_W
    row = jnp.repeat(jnp.arange(rows, dtype=F32), GRID_W)
    col = jnp.tile(jnp.arange(GRID_W, dtype=F32), rows)
    n_freq = HEAD_DIM // 4
    inv_freq = ROPE_BASE ** (-jnp.arange(n_freq, dtype=F32) / n_freq)
    ang = jnp.concatenate([row[:, None] * inv_freq, col[:, None] * inv_freq], axis=-1)
    cos, sin = jnp.cos(ang), jnp.sin(ang)
    return jnp.concatenate([cos, cos], axis=-1), jnp.concatenate([-sin, sin], axis=-1)


def kernel(x, c, ctx, c_ctx, ada_w, ada_b, norm_mix, norm_ffn, ab_w_in, ab_q_gain, ab_k_gain, ab_sink, ab_conv_w, ab_conv_b, ab_gate_a_w, ab_gate_a_b, ab_gate_x_w, ab_gate_x_b, ab_lru_lambda, ab_w_out, gqa_w_in, gqa_q_gain, gqa_k_gain, gqa_w_out, router_w, router_bias, moe_w_gate, moe_w_up, moe_w_down):
    batch, seq, d = x.shape
    ctx_len = ctx.shape[1]
    depth = ada_w.shape[0]
    assert depth == 2 and batch < SUBLANES
    n_lat = batch * seq
    n_ctx = batch * ctx_len
    ctx_row = batch

    xl = x.reshape(n_lat, d)
    xc = ctx.reshape(n_ctx, d)
    cc = jnp.zeros((SUBLANES, d), F32).at[:batch].set(c).at[ctx_row].set(c_ctx)
    mod = _ada(cc, ada_w, ada_b)
    cos2, sin2 = _rope_tables(seq)
    rw_t = router_w.T.astype(BF16)
    rb = router_bias.reshape(N_EXPERTS, 1).astype(F32)
    experts = (moe_w_gate, moe_w_up, moe_w_down)

    mod3 = mod[0].reshape(SUBLANES, 1, 6 * d)
    w_in = ab_w_in[0].astype(BF16)
    proj = _norm_mod_matmul(xl, norm_mix[0], mod3, 0, 1, w_in, seq, 0,
                            A_Q_HEADS, A_KV_HEADS, ab_q_gain[0], ab_k_gain[0], cos2, sin2)
    proj_c = _norm_mod_matmul(xc, norm_mix[0], mod3, 0, 1, w_in, n_ctx, ctx_row,
                              A_Q_HEADS, A_KV_HEADS, ab_q_gain[0], ab_k_gain[0])
    att = _win_attn(ab_sink[0], proj, proj_c, batch, seq, ctx_len)
    att_c = _ctx_attn(ab_sink[0], proj_c, batch, ctx_len)

    lru_w = ab_conv_w.shape[2]
    c0 = (A_Q_HEADS + 2 * A_KV_HEADS) * HEAD_DIM
    y_p, yc_p = _lru(_to_chunked(proj[:, c0:c0 + lru_w], batch),
                     _to_chunked(proj[:, c0 + lru_w:c0 + 2 * lru_w], batch),
                     _to_chunked(proj_c[:, c0:c0 + lru_w], batch),
                     _to_chunked(proj_c[:, c0 + lru_w:c0 + 2 * lru_w], batch),
                     ab_conv_w[0], ab_conv_b[0], ab_gate_a_w[0], ab_gate_a_b[0],
                     ab_gate_x_w[0], ab_gate_x_b[0], ab_lru_lambda[0])
    w_out = ab_w_out[0].astype(BF16)
    xl = _out_proj(att, 0, _from_chunked(y_p), 0, w_out, xl, mod3, 2, seq, 0)
    xc = _out_proj(att_c, 0, _from_chunked(yc_p), 0, w_out, xc, mod3, 2, n_ctx, ctx_row)
    xl, xc = _moe(xl, xc, norm_ffn[0], mod3, rw_t, rb, *experts, 0, seq, ctx_row)

    mod3 = mod[1].reshape(SUBLANES, 1, 6 * d)
    w_in = gqa_w_in[0].astype(BF16)
    cw = C_Q_HEADS * HEAD_DIM
    proj = _norm_mod_matmul(xl, norm_mix[1], mod3, 0, 1, w_in, seq, 0,
                            C_Q_HEADS, C_KV_HEADS, gqa_q_gain[0], gqa_k_gain[0], cos2, sin2)
    proj_c = _norm_mod_matmul(xc, norm_mix[1], mod3, 0, 1, w_in[:, cw:], n_ctx, ctx_row,
                              0, C_KV_HEADS, gqa_q_gain[0], gqa_k_gain[0])
    att = _dense_attn(proj, proj_c, batch, seq, ctx_len)
    xl = _out_proj(att, 0, att, 1, gqa_w_out[0].astype(BF16), xl, mod3, 2, seq, 0)
    xl, _ = _moe(xl, None, norm_ffn[1], mod3, rw_t, rb, *experts, 1, seq, ctx_row)
    return xl.reshape(batch, seq, d)
```

```python
import functools

import jax
import jax.numpy as jnp
import numpy as np
from jax import lax
from jax.experimental import pallas as pl
from jax.experimental.pallas import tpu as pltpu
from jax.experimental.pallas import tpu_sc as plsc

F32 = jnp.float32
BF16 = jnp.bfloat16

LANES = 128
SUBLANES = 8
VMEM_LIMIT = 56 * 1024 * 1024

HEAD_DIM = 128
GRID_W = 64
WINDOW = 128
BLOCK = 128
ROPE_BASE = 10000.0
EPS = 1e-6
ATTN_SCALE = HEAD_DIM ** -0.5
A_Q_HEADS, A_KV_HEADS = 8, 2
C_Q_HEADS, C_KV_HEADS = 16, 4
GQA_GROUP = 4
LRU_C = 8.0
CONV_W = 4
CONV_LEFT = 2
N_EXPERTS = 16
N_GROUPS = 4
EXPERTS_PER_GROUP = 4
NEG_BIG = -1e30

LRU_CHUNKS = SUBLANES
LRU_JB = 16
LRU_LANE_BLOCKS = 2
MOE_TM = 256
MOE_RANGES = 2
SC_CORES = 2
SC_WORKERS = 32
SC_GATHER_WINDOW = 32
SC_GATHER_BUFFERS = 3
SC_GATHER_SUB = 8


def _cparams(sem, vmem=VMEM_LIMIT):
    return pltpu.CompilerParams(dimension_semantics=sem, vmem_limit_bytes=vmem)


def _dot(a, b):
    return jnp.dot(a, b, preferred_element_type=F32)


def _dot_nt(a, b):
    return lax.dot_general(a, b, (((1,), (1,)), ((), ())), preferred_element_type=F32)


def _ada_kernel(c_ref, w_ref, b_ref, o_ref):
    c = c_ref[...]
    s = (c * jax.nn.sigmoid(c)).astype(BF16)
    o_ref[0] = _dot(s, w_ref[0].astype(BF16)) + b_ref[0]


def _ada(cc, ada_w, ada_b):
    depth, d, n = ada_w.shape
    tn = 1024
    return pl.pallas_call(
        _ada_kernel,
        out_shape=jax.ShapeDtypeStruct((depth, SUBLANES, n), F32),
        grid=(depth, n // tn),
        in_specs=[pl.BlockSpec((SUBLANES, d), lambda l, j: (0, 0)),
                  pl.BlockSpec((1, d, tn), lambda l, j: (l, 0, j)),
                  pl.BlockSpec((1, 1, tn), lambda l, j: (l, 0, j))],
        out_specs=pl.BlockSpec((1, SUBLANES, tn), lambda l, j: (l, 0, j)),
        compiler_params=_cparams(("arbitrary", "arbitrary")),
        name="ada",
    )(cc, ada_w, ada_b.reshape(depth, 1, n))


def _norm_mod_rows(x_ref, g_ref, sh_ref, sc_ref, dst_ref, tm, rc=128):
    g = g_ref[...]
    sc1 = 1.0 + sc_ref[...]
    sh = sh_ref[...]

    def body(r, carry):
        rows = pl.ds(pl.multiple_of(r * rc, rc), rc)
        xf = x_ref[rows, :]
        ms = jnp.mean(xf * xf, axis=-1, keepdims=True)
        xn = (xf * lax.rsqrt(ms + EPS)) * g
        dst_ref[rows, :] = (xn * sc1 + sh).astype(dst_ref.dtype)
        return carry

    lax.fori_loop(0, tm // rc, body, 0)


def _nm_mm_kernel(*refs, tm, tn, n_q, n_k, rope):
    if rope:
        x_ref, g_ref, sh_ref, sc_ref, w_ref, qg_ref, kg_ref, cos_ref, sin_ref, o_ref, hn_ref = refs
    else:
        x_ref, g_ref, sh_ref, sc_ref, w_ref, qg_ref, kg_ref, o_ref, hn_ref = refs
    _norm_mod_rows(x_ref, g_ref, sh_ref, sc_ref, hn_ref, tm)
    h = hn_ref[...]
    heads_per_chunk = tn // HEAD_DIM
    for j in range(w_ref.shape[1] // tn):
        cols = slice(j * tn, (j + 1) * tn)
        y = _dot(h, w_ref[:, cols])
        parts = []
        for hh in range(heads_per_chunk):
            head = j * heads_per_chunk + hh
            yh = y[:, _head_cols(hh)]
            if head < n_q + n_k:
                gain = qg_ref[...] if head < n_q else kg_ref[...]
                ms = jnp.mean(yh * yh, axis=-1, keepdims=True)
                yh = (yh * lax.rsqrt(ms + EPS)) * gain
                if rope:
                    yh = yh * cos_ref[...] + pltpu.roll(yh, HEAD_DIM // 2, 1) * sin_ref[...]
                if head < n_q:
                    yh = yh * ATTN_SCALE
            parts.append(yh.astype(o_ref.dtype))
        o_ref[:, cols] = jnp.concatenate(parts, axis=1)


def _norm_mod_matmul(x, g, mod3, shift_blk, scale_blk, w, rows_per_mod, mod_base,
                     n_q, n_k, q_gain, k_gain, cos2=None, sin2=None, tm=512, tn=512):
    n, d = x.shape
    nout = w.shape[1]
    tm = min(tm, n)
    tn = min(tn, nout)
    per = rows_per_mod // tm
    rope = cos2 is not None

    def mod_row(i):
        return mod_base + i // per

    head_vec = pl.BlockSpec((1, HEAD_DIM), lambda i: (0, 0))
    in_specs = [pl.BlockSpec((tm, d), lambda i: (i, 0)),
                pl.BlockSpec((1, d), lambda i: (0, 0)),
                pl.BlockSpec((None, 1, d), lambda i: (mod_row(i), 0, shift_blk)),
                pl.BlockSpec((None, 1, d), lambda i: (mod_row(i), 0, scale_blk)),
                pl.BlockSpec((d, nout), lambda i: (0, 0), pipeline_mode=pl.Buffered(1)),
                head_vec, head_vec]
    args = [x, g.reshape(1, d), mod3, mod3, w, q_gain.reshape(1, HEAD_DIM), k_gain.reshape(1, HEAD_DIM)]
    if rope:
        tiles_per_seq = cos2.shape[0] // tm
        table = pl.BlockSpec((tm, HEAD_DIM), lambda i: (i % tiles_per_seq, 0))
        in_specs += [table, table]
        args += [cos2, sin2]
    return pl.pallas_call(
        functools.partial(_nm_mm_kernel, tm=tm, tn=tn, n_q=n_q, n_k=n_k, rope=rope),
        out_shape=jax.ShapeDtypeStruct((n, nout), BF16),
        grid=(n // tm,),
        in_specs=in_specs,
        out_specs=pl.BlockSpec((tm, nout), lambda i: (i, 0)),
        scratch_shapes=[pltpu.VMEM((tm, d), BF16)],
        compiler_params=_cparams(("parallel",)),
        name="norm_mod_matmul",
    )(*args)


def _head_cols(h):
    return slice(h * HEAD_DIM, (h + 1) * HEAD_DIM)


def _stack_group(q_ref, kvh):
    return jnp.concatenate([q_ref[:, _head_cols(kvh * GQA_GROUP + g)] for g in range(GQA_GROUP)], axis=0)


def _sink_col(sink_ref, kvh, rows):
    return jnp.concatenate([jnp.full((rows, 1), sink_ref[kvh * GQA_GROUP + g], F32)
                            for g in range(GQA_GROUP)], axis=0)


def _band_bias(ctx_len):
    rows, nk = GQA_GROUP * BLOCK, 3 * BLOCK + ctx_len
    qi = np.arange(rows)[:, None] % BLOCK
    kj = np.arange(nk)[None, :]
    inner = (kj >= 3 * BLOCK) | (np.abs(kj - BLOCK - qi) <= WINDOW)
    first = inner & ~(kj < BLOCK)
    last = inner & ~((kj >= 2 * BLOCK) & (kj < 3 * BLOCK))
    return np.where(np.stack([first, inner, last]), 0.0, NEG_BIG).astype(np.float32)


def _win_attn_kernel(sink_ref, bias_ref, q_ref, kp_ref, kc_ref, kn_ref, vp_ref, vc_ref, vn_ref,
                     kx_ref, vx_ref, o_ref, *, ctx_len):
    nk = 3 * BLOCK + ctx_len
    bias = bias_ref[...]
    ones = jnp.ones((nk, HEAD_DIM), BF16)
    for kvh in range(A_KV_HEADS):
        cols = _head_cols(kvh)
        q4 = _stack_group(q_ref, kvh)
        ka = jnp.concatenate([kp_ref[:, cols], kc_ref[:, cols], kn_ref[:, cols], kx_ref[:, cols]], axis=0)
        va = jnp.concatenate([vp_ref[:, cols], vc_ref[:, cols], vn_ref[:, cols], vx_ref[:, cols]], axis=0)
        s = _dot_nt(q4, ka) + bias
        sk = _sink_col(sink_ref, kvh, BLOCK)
        m = jnp.maximum(jnp.max(s, axis=-1, keepdims=True), sk)
        p = jnp.exp((s - m).astype(BF16))
        acc = _dot(p, jnp.concatenate([va, ones], axis=1))
        o = acc[:, 0:HEAD_DIM] / (acc[:, HEAD_DIM:] + jnp.exp(sk - m))
        for g in range(GQA_GROUP):
            o_ref[:, _head_cols(kvh * GQA_GROUP + g)] = o[g * BLOCK:(g + 1) * BLOCK].astype(o_ref.dtype)


def _win_attn(sink, proj, proj_c, batch, seq, ctx_len):
    nb = seq // BLOCK
    assert nb >= 2
    kvw = A_KV_HEADS * HEAD_DIM
    k_blk = A_Q_HEADS * HEAD_DIM // kvw
    v_blk = k_blk + 1
    bias = jnp.asarray(_band_bias(ctx_len))

    def which_bias(b, n):
        return (jnp.where(n == 0, 0, jnp.where(n == nb - 1, 2, 1)), 0, 0)

    def prev(b, n):
        return b * nb + jnp.maximum(n - 1, 0)

    def cur(b, n):
        return b * nb + n

    def nxt(b, n):
        return b * nb + jnp.minimum(n + 1, nb - 1)

    return pl.pallas_call(
        functools.partial(_win_attn_kernel, ctx_len=ctx_len),
        out_shape=jax.ShapeDtypeStruct((batch * seq, A_Q_HEADS * HEAD_DIM), BF16),
        grid=(batch, nb),
        in_specs=[pl.BlockSpec(memory_space=pltpu.SMEM),
                  pl.BlockSpec((None,) + bias.shape[1:], which_bias),
                  pl.BlockSpec((BLOCK, A_Q_HEADS * HEAD_DIM), lambda b, n: (cur(b, n), 0)),
                  pl.BlockSpec((BLOCK, kvw), lambda b, n: (prev(b, n), k_blk)),
                  pl.BlockSpec((BLOCK, kvw), lambda b, n: (cur(b, n), k_blk)),
                  pl.BlockSpec((BLOCK, kvw), lambda b, n: (nxt(b, n), k_blk)),
                  pl.BlockSpec((BLOCK, kvw), lambda b, n: (prev(b, n), v_blk)),
                  pl.BlockSpec((BLOCK, kvw), lambda b, n: (cur(b, n), v_blk)),
                  pl.BlockSpec((BLOCK, kvw), lambda b, n: (nxt(b, n), v_blk)),
                  pl.BlockSpec((ctx_len, kvw), lambda b, n: (b, k_blk)),
                  pl.BlockSpec((ctx_len, kvw), lambda b, n: (b, v_blk))],
        out_specs=pl.BlockSpec((BLOCK, A_Q_HEADS * HEAD_DIM), lambda b, n: (cur(b, n), 0)),
        compiler_params=_cparams(("parallel", "parallel")),
        name="win_attn",
    )(sink, bias, proj, proj, proj, proj, proj, proj, proj, proj_c, proj_c)


def _ctx_attn_kernel(sink_ref, q_ref, k_ref, v_ref, o_ref, *, ctx_len):
    kvh = pl.program_id(1)
    q4 = jnp.concatenate([q_ref[:, _head_cols(g)] for g in range(GQA_GROUP)], axis=0)
    s = _dot_nt(q4, k_ref[...])
    sk = jnp.concatenate([jnp.full((ctx_len, 1), sink_ref[kvh * GQA_GROUP + g], F32)
                          for g in range(GQA_GROUP)], axis=0)
    m = jnp.maximum(jnp.max(s, axis=-1, keepdims=True), sk)
    p = jnp.exp(s - m)
    den = jnp.sum(p, axis=-1, keepdims=True) + jnp.exp(sk - m)
    o = _dot(p.astype(BF16), v_ref[...]) / den
    for g in range(GQA_GROUP):
        o_ref[:, _head_cols(g)] = o[g * ctx_len:(g + 1) * ctx_len].astype(o_ref.dtype)


def _ctx_attn(sink, proj_c, batch, ctx_len):
    k_blk = A_Q_HEADS
    v_blk = A_Q_HEADS + A_KV_HEADS
    gw = GQA_GROUP * HEAD_DIM
    return pl.pallas_call(
        functools.partial(_ctx_attn_kernel, ctx_len=ctx_len),
        out_shape=jax.ShapeDtypeStruct((batch * ctx_len, A_Q_HEADS * HEAD_DIM), BF16),
        grid=(batch, A_KV_HEADS),
        in_specs=[pl.BlockSpec(memory_space=pltpu.SMEM),
                  pl.BlockSpec((ctx_len, gw), lambda b, h: (b, h)),
                  pl.BlockSpec((ctx_len, HEAD_DIM), lambda b, h: (b, k_blk + h)),
                  pl.BlockSpec((ctx_len, HEAD_DIM), lambda b, h: (b, v_blk + h))],
        out_specs=pl.BlockSpec((ctx_len, gw), lambda b, h: (b, h)),
        compiler_params=_cparams(("parallel", "parallel")),
        name="ctx_attn",
    )(sink, proj_c, proj_c, proj_c)


def _sigmoid(x):
    return 0.5 * jnp.tanh(0.5 * x) + 0.5


def _gelu_tanh(x):
    return 0.5 * x * (1.0 + jnp.tanh(0.7978845608028654 * (x + 0.044715 * (x * x * x))))


def _lru_sequence(x_ref, xg_ref, y_ref, xp_ref, a_ref, b_ref, w, init, rows, jb):
    conv_w, conv_b, wa, ba, wx, bx, c_logsig = w
    width = x_ref.shape[-1]
    sub = lax.broadcasted_iota(jnp.int32, (1, SUBLANES, width), 1)

    def block_diag(ub, wd):
        return jnp.concatenate([_dot(ub[:, _head_cols(n)], wd[n]) for n in range(width // LANES)], axis=1)

    def fill(r, carry):
        rr = pl.ds(pl.multiple_of(r * jb, jb), jb)
        xp_ref[pl.ds(pl.multiple_of(r * jb, jb) + CONV_LEFT, jb)] = x_ref[rr].astype(F32)
        return carry

    lax.fori_loop(0, rows // jb, fill, 0)
    tail = x_ref[rows - CONV_LEFT:rows].astype(F32)
    xp_ref[0:CONV_LEFT] = jnp.where(sub == 0, 0.0, pltpu.roll(tail, 1, 1))
    head = x_ref[0:1].astype(F32)
    xp_ref[rows + CONV_LEFT:rows + CONV_LEFT + 1] = jnp.where(
        sub == SUBLANES - 1, 0.0, pltpu.roll(head, SUBLANES - 1, 1))

    def gates(r, carry):
        j0 = pl.multiple_of(r * jb, jb)
        u = conv_b
        for k in range(CONV_W):
            u = u + conv_w[k] * xp_ref[pl.ds(j0 + k, jb)]
        u2 = u.reshape(jb * SUBLANES, width)
        ub = u2.astype(BF16)
        for d in range(2):
            r_gate = _sigmoid(block_diag(ub, wa[d]) + ba[d])
            i_gate = _sigmoid(block_diag(ub, wx[d]) + bx[d])
            log_a = c_logsig[d] * r_gate
            a = jnp.exp(log_a)
            b = jnp.sqrt(1.0 - a * a) * (i_gate * u2)
            a_ref[d, pl.ds(j0, jb)] = a.reshape(jb, SUBLANES, width)
            b_ref[d, pl.ds(j0, jb)] = b.reshape(jb, SUBLANES, width)
        return carry

    lax.fori_loop(0, rows // jb, gates, 0, unroll=2)

    def scan(j, carry):
        hf, pf, hb, pb = carry
        jr = rows - 1 - j
        af = a_ref[0, j]
        hf = af * hf + b_ref[0, j]
        pf = pf * af
        b_ref[0, j] = hf
        a_ref[0, j] = pf
        ab = a_ref[1, jr]
        hb = ab * hb + b_ref[1, jr]
        pb = pb * ab
        b_ref[1, jr] = hb
        a_ref[1, jr] = pb
        return hf, pf, hb, pb

    z = jnp.zeros((SUBLANES, width), F32)
    o = jnp.ones((SUBLANES, width), F32)
    lax.fori_loop(0, rows, scan, (z, o, z, o), unroll=8)

    hf_last, pf_last = b_ref[0, rows - 1], a_ref[0, rows - 1]
    hb_last, pb_last = b_ref[1, 0], a_ref[1, 0]
    s = init[0]
    carry_f = []
    for c in range(SUBLANES):
        carry_f.append(s)
        s = hf_last[c:c + 1] + pf_last[c:c + 1] * s
    out_f = s
    s = init[1]
    carry_b = [None] * SUBLANES
    for c in reversed(range(SUBLANES)):
        carry_b[c] = s
        s = hb_last[c:c + 1] + pb_last[c:c + 1] * s
    out_b = s
    cf = jnp.concatenate(carry_f, axis=0)
    cb = jnp.concatenate(carry_b, axis=0)

    def emit(r, carry):
        rr = pl.ds(pl.multiple_of(r * jb, jb), jb)
        h = (b_ref[0, rr] + a_ref[0, rr] * cf) + (b_ref[1, rr] + a_ref[1, rr] * cb)
        y_ref[rr] = (h * _gelu_tanh(xg_ref[rr].astype(F32))).astype(y_ref.dtype)
        return carry

    lax.fori_loop(0, rows // jb, emit, 0)
    return out_f, out_b


def _lru_kernel(xr_ref, xg_ref, xrc_ref, xgc_ref, cw_ref, cb_ref, wa_ref, ba_ref, wx_ref, bx_ref,
                lam_ref, y_ref, yc_ref, xp_ref, a_ref, b_ref, *, rows, rows_c):
    c_logsig = [LRU_C * jax.nn.log_sigmoid(lam_ref[d]) for d in range(2)]
    w = ([cw_ref[k] for k in range(CONV_W)], cb_ref[0],
         [wa_ref[d] for d in range(2)], [ba_ref[d] for d in range(2)],
         [wx_ref[d] for d in range(2)], [bx_ref[d] for d in range(2)], c_logsig)
    zero = jnp.zeros((1, xr_ref.shape[-1]), F32)
    sf, sb = _lru_sequence(xrc_ref, xgc_ref, yc_ref, xp_ref, a_ref, b_ref, w, (zero, zero),
                           rows_c, min(LRU_JB, rows_c))
    _lru_sequence(xr_ref, xg_ref, y_ref, xp_ref, a_ref, b_ref, w, (sf, sb), rows, LRU_JB)


def _lru(xr, xg, xrc, xgc, conv_w, conv_b, wa, ba, wx, bx, lam):
    batch, rows, _, width = xr.shape
    rows_c = xrc.shape[1]
    cw = LRU_LANE_BLOCKS * LANES
    seq_spec = pl.BlockSpec((None, rows, SUBLANES, cw), lambda b, n: (b, 0, 0, n))
    ctx_spec = pl.BlockSpec((None, rows_c, SUBLANES, cw), lambda b, n: (b, 0, 0, n))
    vec2 = pl.BlockSpec((2, 1, cw), lambda b, n: (0, 0, n))
    mat2 = pl.BlockSpec((2, LRU_LANE_BLOCKS, LANES, LANES), lambda b, n: (0, n, 0, 0))
    return pl.pallas_call(
        functools.partial(_lru_kernel, rows=rows, rows_c=rows_c),
        out_shape=(jax.ShapeDtypeStruct(xr.shape, BF16), jax.ShapeDtypeStruct(xrc.shape, BF16)),
        grid=(batch, width // cw),
        in_specs=[seq_spec, seq_spec, ctx_spec, ctx_spec,
                  pl.BlockSpec((CONV_W, 1, cw), lambda b, n: (0, 0, n)),
                  pl.BlockSpec((1, 1, cw), lambda b, n: (0, 0, n)),
                  mat2, vec2, mat2, vec2, vec2],
        out_specs=(seq_spec, ctx_spec),
        scratch_shapes=[pltpu.VMEM((rows + CONV_W - 1, SUBLANES, cw), F32),
                        pltpu.VMEM((2, rows, SUBLANES, cw), F32),
                        pltpu.VMEM((2, rows, SUBLANES, cw), F32)],
        compiler_params=_cparams(("parallel", "parallel")),
        name="rglru",
    )(xr, xg, xrc, xgc, conv_w.reshape(CONV_W, 1, width), conv_b.reshape(1, 1, width),
      wa.astype(BF16), ba.reshape(2, 1, width), wx.astype(BF16), bx.reshape(2, 1, width),
      lam.reshape(2, 1, width))


def _to_chunked(a, batch):
    t = a.shape[0] // batch
    return a.reshape(batch, LRU_CHUNKS, t // LRU_CHUNKS, a.shape[1]).transpose(0, 2, 1, 3)


def _from_chunked(a):
    b, r, c, w = a.shape
    return a.transpose(0, 2, 1, 3).reshape(b * r * c, w)


def _out_proj_kernel(a1_ref, a2_ref, w1_ref, w2_ref, x_ref, g_ref, o_ref):
    y = _dot(a1_ref[...], w1_ref[...]) + _dot(a2_ref[...], w2_ref[...])
    o_ref[...] = x_ref[...] + g_ref[...] * y


def _out_proj(a1, a1_blk, a2, a2_blk, w, x, mod3, gate_blk, rows_per_mod, mod_base, tm=2048, tn=512):
    n, d = x.shape
    kh = w.shape[0] // 2
    tm = min(tm, n)
    per = rows_per_mod // tm
    gpb = d // tn
    return pl.pallas_call(
        _out_proj_kernel,
        out_shape=jax.ShapeDtypeStruct((n, d), F32),
        grid=(n // tm, d // tn),
        in_specs=[pl.BlockSpec((tm, kh), lambda i, j: (i, a1_blk)),
                  pl.BlockSpec((tm, kh), lambda i, j: (i, a2_blk)),
                  pl.BlockSpec((kh, tn), lambda i, j: (0, j)),
                  pl.BlockSpec((kh, tn), lambda i, j: (1, j)),
                  pl.BlockSpec((tm, tn), lambda i, j: (i, j)),
                  pl.BlockSpec((None, 1, tn), lambda i, j: (mod_base + i // per, 0, gate_blk * gpb + j))],
        out_specs=pl.BlockSpec((tm, tn), lambda i, j: (i, j)),
        compiler_params=_cparams(("parallel", "parallel")),
        name="out_proj",
    )(a1, a2, w, w, x, mod3)


def _lane_block_max(s):
    mm = s[:, 0:LANES]
    for t in range(1, s.shape[1] // LANES):
        mm = jnp.maximum(mm, s[:, t * LANES:(t + 1) * LANES])
    return mm


def _exp_blocks(s, mrep):
    return jnp.concatenate(
        [jnp.exp((s[:, t * LANES:(t + 1) * LANES] - mrep).astype(BF16)) for t in range(s.shape[1] // LANES)],
        axis=1)


def _copy_key_rows(dst_ref, c, kc, lat_ref, ctx_ref, cols):
    seq = lat_ref.shape[0]
    lo, hi = c * kc, (c + 1) * kc
    if lo < seq:
        n = min(hi, seq) - lo
        dst_ref[c, 0:n, cols] = lat_ref[lo:lo + n, :]
    if hi > seq:
        start = max(lo, seq)
        dst_ref[c, start - lo:kc, cols] = ctx_ref[start - seq:hi - seq, :]


def _dense_attn_kernel(q_ref, k_ref, v_ref, kx_ref, vx_ref, o_ref,
                       s_ref, m_ref, acc_ref, ka_ref, va_ref, *, tq, n_chunks, kc):
    @pl.when(pl.program_id(2) == 0)
    def _():
        for c in range(n_chunks):
            _copy_key_rows(ka_ref, c, kc, k_ref, kx_ref, slice(0, HEAD_DIM))
            _copy_key_rows(va_ref, c, kc, v_ref, vx_ref, slice(0, HEAD_DIM))
            va_ref[c, :, HEAD_DIM:] = jnp.ones((kc, HEAD_DIM), BF16)

    q4 = jnp.concatenate([q_ref[:, _head_cols(g)] for g in range(GQA_GROUP)], axis=0)
    m_ref[...] = jnp.full(m_ref.shape, NEG_BIG, F32)

    def sweep1(c, carry):
        s = _dot_nt(q4, ka_ref[c])
        s_ref[c] = s
        m_ref[...] = jnp.maximum(m_ref[...], _lane_block_max(s))
        return carry

    lax.fori_loop(0, n_chunks, sweep1, 0)
    m_ref[...] = jnp.broadcast_to(jnp.max(m_ref[...], axis=-1, keepdims=True), m_ref.shape)
    acc_ref[...] = jnp.zeros(acc_ref.shape, F32)

    def sweep2(c, carry):
        acc_ref[...] += _dot(_exp_blocks(s_ref[c], m_ref[...]), va_ref[c])
        return carry

    lax.fori_loop(0, n_chunks, sweep2, 0)
    o = acc_ref[:, 0:HEAD_DIM] / acc_ref[:, HEAD_DIM:]
    for g in range(GQA_GROUP):
        o_ref[:, _head_cols(g)] = o[g * tq:(g + 1) * tq].astype(o_ref.dtype)


def _dense_attn(proj, proj_c, batch, seq, ctx_len, tq=256, n_chunks=2):
    gw = GQA_GROUP * HEAD_DIM
    nq = seq // tq
    rows = GQA_GROUP * tq
    kc = (seq + ctx_len) // n_chunks
    assert kc * n_chunks == seq + ctx_len and kc % LANES == 0
    k_blk = C_Q_HEADS
    v_blk = C_Q_HEADS + C_KV_HEADS
    vx_blk = C_KV_HEADS
    return pl.pallas_call(
        functools.partial(_dense_attn_kernel, tq=tq, n_chunks=n_chunks, kc=kc),
        out_shape=jax.ShapeDtypeStruct((batch * seq, C_Q_HEADS * HEAD_DIM), BF16),
        grid=(batch, C_KV_HEADS, nq),
        in_specs=[pl.BlockSpec((tq, gw), lambda b, h, i: (b * nq + i, h)),
                  pl.BlockSpec((seq, HEAD_DIM), lambda b, h, i: (b, k_blk + h)),
                  pl.BlockSpec((seq, HEAD_DIM), lambda b, h, i: (b, v_blk + h)),
                  pl.BlockSpec((ctx_len, HEAD_DIM), lambda b, h, i: (b, h)),
                  pl.BlockSpec((ctx_len, HEAD_DIM), lambda b, h, i: (b, vx_blk + h))],
        out_specs=pl.BlockSpec((tq, gw), lambda b, h, i: (b * nq + i, h)),
        scratch_shapes=[pltpu.VMEM((n_chunks, rows, kc), F32),
                        pltpu.VMEM((rows, LANES), F32),
                        pltpu.VMEM((rows, 2 * HEAD_DIM), F32),
                        pltpu.VMEM((n_chunks, kc, HEAD_DIM), BF16),
                        pltpu.VMEM((n_chunks, kc, 2 * HEAD_DIM), BF16)],
        compiler_params=_cparams(("parallel", "parallel", "arbitrary")),
        name="dense_attn",
    )(proj, proj, proj, proj_c, proj_c)


def _router_rows(biased, scores):
    v = [biased[e:e + 1, :] for e in range(N_EXPERTS)]
    s = [scores[e:e + 1, :] for e in range(N_EXPERTS)]

    def top2_sum(vals):
        best = vals[0] + vals[1]
        for i in range(len(vals)):
            for j in range(i + 1, len(vals)):
                if (i, j) != (0, 1):
                    best = jnp.maximum(best, vals[i] + vals[j])
        return best

    gsum = [top2_sum(v[g * EXPERTS_PER_GROUP:(g + 1) * EXPERTS_PER_GROUP]) for g in range(N_GROUPS)]
    sel = jnp.zeros_like(gsum[0], dtype=jnp.int32)
    best = gsum[0]
    for g in range(1, N_GROUPS):
        take = gsum[g] > best
        sel = jnp.where(take, g, sel)
        best = jnp.where(take, gsum[g], best)

    def pick_group(rows, i):
        out = rows[i]
        for g in range(1, N_GROUPS):
            out = jnp.where(sel == g, rows[g * EXPERTS_PER_GROUP + i], out)
        return out

    cand = [pick_group(v, i) for i in range(EXPERTS_PER_GROUP)]
    cand_s = [pick_group(s, i) for i in range(EXPERTS_PER_GROUP)]
    i1 = jnp.zeros_like(sel)
    b1 = cand[0]
    for i in range(1, EXPERTS_PER_GROUP):
        take = cand[i] > b1
        i1 = jnp.where(take, i, i1)
        b1 = jnp.where(take, cand[i], b1)
    i2 = jnp.full_like(sel, -1)
    b2 = jnp.full_like(b1, -jnp.inf)
    for i in range(EXPERTS_PER_GROUP):
        take = (i1 != i) & ((cand[i] > b2) | (i2 < 0))
        i2 = jnp.where(take, i, i2)
        b2 = jnp.where(take, cand[i], b2)

    def pick_idx(rows, idx):
        out = rows[0]
        for i in range(1, EXPERTS_PER_GROUP):
            out = jnp.where(idx == i, rows[i], out)
        return out

    s0 = pick_idx(cand_s, i1)
    s1 = pick_idx(cand_s, i2)
    tot = s0 + s1
    e0 = (sel * EXPERTS_PER_GROUP + i1).astype(F32)
    e1 = (sel * EXPERTS_PER_GROUP + i2).astype(F32)
    return e0, e1, s0 / tot, s1 / tot


def _norm_router_kernel(*refs, tm, n_lat_tiles, has_ctx):
    i = pl.program_id(0)
    if has_ctx:
        x_ref, xc_ref, g_ref, sh_ref, sc_ref, rw_ref, rb_ref, hp_ref, r_ref, h_ref, tri_ref, run_ref = refs

        @pl.when(i < n_lat_tiles)
        def _():
            _norm_mod_rows(x_ref, g_ref, sh_ref, sc_ref, h_ref, tm)

        @pl.when(i >= n_lat_tiles)
        def _():
            _norm_mod_rows(xc_ref, g_ref, sh_ref, sc_ref, h_ref, tm)
    else:
        x_ref, g_ref, sh_ref, sc_ref, rw_ref, rb_ref, hp_ref, r_ref, h_ref, tri_ref, run_ref = refs
        _norm_mod_rows(x_ref, g_ref, sh_ref, sc_ref, h_ref, tm)

    @pl.when(i == 0)
    def _():
        run_ref[...] = jnp.zeros(run_ref.shape, F32)
        before = lax.broadcasted_iota(jnp.int32, (tm, tm), 0) <= lax.broadcasted_iota(jnp.int32, (tm, tm), 1)
        tri_ref[...] = jnp.where(before, 1.0, 0.0).astype(BF16)

    def pack(r, carry):
        rows = pl.ds(pl.multiple_of(r * 128, 128), 128)
        hp_ref[rows, :] = _pack_halves(h_ref[rows, :])
        return carry

    lax.fori_loop(0, tm // 128, pack, 0)
    logits = _dot_nt(rw_ref[...], h_ref[...])
    scores = jax.nn.sigmoid(logits)
    e0, e1, w0, w1 = _router_rows(scores + rb_ref[...], scores)
    expert_ids = lax.broadcasted_iota(jnp.int32, (N_EXPERTS, tm), 0).astype(F32)
    ranks = []
    for e_row in (e0, e1):
        hit = expert_ids == e_row
        seen = _dot(jnp.where(hit, 1.0, 0.0).astype(BF16), tri_ref[...])
        ranks.append(jnp.sum(jnp.where(hit, seen - 1.0 + run_ref[...], 0.0), axis=0, keepdims=True))
        run_ref[...] = run_ref[...] + seen[:, tm - 1:tm]
    zero = jnp.zeros_like(w0)
    r_ref[...] = jnp.concatenate([e0, e1, w0, w1, ranks[0], ranks[1], zero, zero], axis=0)


def _pack_halves(v):
    c = v.shape[1] // 2
    lo = lax.bitcast_convert_type(v[:, :c].astype(BF16).astype(F32), jnp.uint32)
    hi = lax.bitcast_convert_type(v[:, c:].astype(BF16).astype(F32), jnp.uint32)
    return (lo >> 16) | (hi & jnp.uint32(0xFFFF0000))


def _unpack_halves(p):
    lo = lax.bitcast_convert_type(p << 16, F32)
    hi = lax.bitcast_convert_type(p & jnp.uint32(0xFFFF0000), F32)
    return lo, hi


def _sc_gather_rows(table, idx):
    n, w = idx.shape[0], table.shape[1]
    win, nb, sub = SC_GATHER_WINDOW, SC_GATHER_BUFFERS, SC_GATHER_SUB
    per = n // (win * SC_WORKERS)
    assert per * win * SC_WORKERS == n and per >= 1
    mesh = plsc.VectorSubcoreMesh(core_axis_name="core", subcore_axis_name="subcore")

    @functools.partial(
        pl.kernel, out_type=jax.ShapeDtypeStruct((n, w), table.dtype), mesh=mesh, name="sc_gather_rows",
        scratch_types=([pltpu.VMEM((per * win,), jnp.int32)] + [pltpu.VMEM((win, w), table.dtype)] * nb
                       + [pltpu.SemaphoreType.DMA] * (2 * nb)))
    def gather(x_hbm, i_hbm, o_hbm, i_v, *rest):
        bufs, gsems, wsems = rest[:nb], rest[nb:2 * nb], rest[2 * nb:]
        wid = lax.axis_index("subcore") * SC_CORES + lax.axis_index("core")
        base = wid * (per * win)
        pltpu.sync_copy(i_hbm.at[pl.ds(base, per * win)], i_v)
        rot = (wid * per) // SC_WORKERS

        def row0(t):
            u = t + rot
            u = u - per * (u >= per).astype(jnp.int32)
            return pl.multiple_of(u * win, win)

        def gather_copy(t, b, s):
            return pltpu.make_async_copy(x_hbm.at[i_v.at[pl.ds(row0(t) + s * sub, sub)]],
                                         bufs[b].at[pl.ds(s * sub, sub)], gsems[b])

        def write_copy(t, b):
            return pltpu.make_async_copy(bufs[b], o_hbm.at[pl.ds(base + row0(t), win)], wsems[b])

        def start_gathers(t, b):
            for s in range(win // sub):
                gather_copy(t, b, s).start()

        def step(t, b):
            for s in range(win // sub):
                gather_copy(t, b, s).wait()
            write_copy(t, b).start()

            @pl.when(t >= 1)
            def _():
                write_copy(t - 1, (b - 1) % nb).wait()

            @pl.when(t + nb - 1 < per)
            def _():
                start_gathers(t + nb - 1, (b + nb - 1) % nb)

        for t in range(min(nb - 1, per)):
            start_gathers(t, t)

        @pl.loop(0, per // nb)
        def _(p):
            for j in range(nb):
                step(nb * p + j, j)

        for t in range(per - per % nb, per):
            step(t, t % nb)
        write_copy(per - 1, (per - 1) % nb).wait()

    return gather(table, idx)


def _lat_ctx_maps(n_lat, per, ctx_mod_row):
    def mod_row(i):
        return jnp.where(i < n_lat, i // per, ctx_mod_row)

    def lat(i):
        return (jnp.minimum(i, n_lat - 1), 0)

    def ctx(i):
        return (jnp.maximum(i - n_lat, 0), 0)

    return mod_row, lat, ctx


def _norm_router(x, cx, g, mod3, shift_blk, scale_blk, rw_t, rb, rows_per_mod, ctx_mod_row, tm=512):
    n, d = x.shape
    n_lat = n // tm
    has_ctx = cx is not None
    ntot = n_lat + (cx.shape[0] // tm if has_ctx else 0)
    mod_row, lat, ctx = _lat_ctx_maps(n_lat, rows_per_mod // tm, ctx_mod_row)
    row_specs = [pl.BlockSpec((tm, d), lat)] + ([pl.BlockSpec((tm, d), ctx)] if has_ctx else [])
    row_args = [x] + ([cx] if has_ctx else [])
    return pl.pallas_call(
        functools.partial(_norm_router_kernel, tm=tm, n_lat_tiles=n_lat, has_ctx=has_ctx),
        out_shape=(jax.ShapeDtypeStruct((ntot * tm, d // 2), jnp.uint32),
                   jax.ShapeDtypeStruct((SUBLANES, ntot * tm), F32)),
        grid=(ntot,),
        in_specs=row_specs + [
            pl.BlockSpec((1, d), lambda i: (0, 0)),
            pl.BlockSpec((None, 1, d), lambda i: (mod_row(i), 0, shift_blk)),
            pl.BlockSpec((None, 1, d), lambda i: (mod_row(i), 0, scale_blk)),
            pl.BlockSpec((N_EXPERTS, d), lambda i: (0, 0)),
            pl.BlockSpec((N_EXPERTS, 1), lambda i: (0, 0))],
        out_specs=(pl.BlockSpec((tm, d // 2), lambda i: (i, 0)),
                   pl.BlockSpec((SUBLANES, tm), lambda i: (0, i))),
        scratch_shapes=[pltpu.VMEM((tm, d), BF16), pltpu.VMEM((tm, tm), BF16), pltpu.VMEM((N_EXPERTS, 1), F32)],
        compiler_params=_cparams(("arbitrary",)),
        name="norm_router",
    )(*row_args, g.reshape(1, d), mod3, mod3, rw_t, rb)


def _cast_rows(src_ref, dst_ref, rb=256):
    def body(r, carry):
        rows = pl.ds(pl.multiple_of(r * rb, rb), rb)
        dst_ref[rows, :] = src_ref[0, rows, :].astype(dst_ref.dtype)
        return carry

    lax.fori_loop(0, dst_ref.shape[0] // rb, body, 0)


def _tile_state(te_ref, tv_ref, tile0):
    i = pl.program_id(0)
    t = tile0 + i
    live = tv_ref[t] > 0
    new_expert = (i == 0) | (te_ref[t] != te_ref[jnp.maximum(t - 1, 0)])
    return live, new_expert


def _gmm_kernel(te_ref, tv_ref, xs_ref, wg_ref, wu_ref, wd_ref, *rest, tile0):
    o_ref, wg_b, wu_b, wd_b = rest[-4:]
    live, new_expert = _tile_state(te_ref, tv_ref, tile0)

    @pl.when(live & new_expert)
    def _():
        _cast_rows(wg_ref, wg_b)
        _cast_rows(wu_ref, wu_b)
        _cast_rows(wd_ref, wd_b)

    @pl.when(live)
    def _():
        lo, hi = _unpack_halves(xs_ref[...])
        x = jnp.concatenate([lo.astype(BF16), hi.astype(BF16)], axis=1)
        gate = _dot(x, wg_b[...])
        up = _dot(x, wu_b[...])
        h1 = ((gate * jax.nn.sigmoid(gate)) * up).astype(BF16)
        o_ref[...] = _pack_halves(_dot(h1, wd_b[...]))

    @pl.when(jnp.logical_not(live))
    def _():
        o_ref[...] = jnp.zeros(o_ref.shape, o_ref.dtype)


def _grouped_mlp(tile_expert, tile_valid, xs_part, tile0, ys_prev, wg, wu, wd, layer, tm=MOE_TM):
    dpk = xs_part.shape[1]
    d, dff = wg.shape[2], wg.shape[3]
    n_tiles = tile_expert.shape[0]

    def expert_block(rows, cols):
        return pl.BlockSpec((None, 1, rows, cols), lambda i, te, tv: (layer, te[tile0 + i], 0, 0),
                            pipeline_mode=pl.Buffered(1))

    in_specs = [pl.BlockSpec((tm, dpk), lambda i, te, tv: (i, 0)),
                expert_block(d, dff), expert_block(d, dff), expert_block(dff, d)]
    args = [tile_expert, tile_valid, xs_part, wg, wu, wd]
    aliases = {}
    if ys_prev is not None:
        in_specs.append(pl.BlockSpec(memory_space=pl.ANY))
        args.append(ys_prev)
        aliases = {len(args) - 1: 0}
    grid_spec = pltpu.PrefetchScalarGridSpec(
        num_scalar_prefetch=2,
        grid=(xs_part.shape[0] // tm,),
        in_specs=in_specs,
        out_specs=pl.BlockSpec((tm, dpk), lambda i, te, tv: (tile0 + i, 0)),
        scratch_shapes=[pltpu.VMEM((d, dff), BF16), pltpu.VMEM((d, dff), BF16), pltpu.VMEM((dff, d), BF16)],
    )
    return pl.pallas_call(
        functools.partial(_gmm_kernel, tile0=tile0),
        out_shape=jax.ShapeDtypeStruct((n_tiles * tm, dpk), jnp.uint32),
        grid_spec=grid_spec,
        input_output_aliases=aliases,
        compiler_params=_cparams(("arbitrary",)),
        name="grouped_mlp",
    )(*args)


def _combine_kernel(*refs, n_lat_tiles, has_ctx):
    if has_ctx:
        x_ref, xc_ref, y0_ref, y1_ref, r_ref, g_ref, o_ref, oc_ref = refs
    else:
        x_ref, y0_ref, y1_ref, r_ref, g_ref, o_ref = refs
    w0 = r_ref[:, 2:3]
    w1 = r_ref[:, 3:4]
    lo0, hi0 = _unpack_halves(y0_ref[...])
    lo1, hi1 = _unpack_halves(y1_ref[...])
    f = g_ref[...] * jnp.concatenate([w0 * lo0 + w1 * lo1, w0 * hi0 + w1 * hi1], axis=1)
    if not has_ctx:
        o_ref[...] = x_ref[...] + f
        return
    i = pl.program_id(0)

    @pl.when(i < n_lat_tiles)
    def _():
        o_ref[...] = x_ref[...] + f

    @pl.when(i >= n_lat_tiles)
    def _():
        oc_ref[...] = xc_ref[...] + f


def _combine(x, cx, yg, route_cols, mod3, gate_blk, rows_per_mod, ctx_mod_row, tm=512):
    n, d = x.shape
    n_lat = n // tm
    has_ctx = cx is not None
    ntot = n_lat + (cx.shape[0] // tm if has_ctx else 0)
    mod_row, lat, ctx = _lat_ctx_maps(n_lat, rows_per_mod // tm, ctx_mod_row)
    row_specs = [pl.BlockSpec((tm, d), lat)] + ([pl.BlockSpec((tm, d), ctx)] if has_ctx else [])
    row_args = [x] + ([cx] if has_ctx else [])
    out_shape = [jax.ShapeDtypeStruct(x.shape, F32)] + ([jax.ShapeDtypeStruct(cx.shape, F32)] if has_ctx else [])
    out = pl.pallas_call(
        functools.partial(_combine_kernel, n_lat_tiles=n_lat, has_ctx=has_ctx),
        out_shape=tuple(out_shape),
        grid=(ntot,),
        in_specs=row_specs + [
            pl.BlockSpec((tm, d // 2), lambda i: (i, 0)),
            pl.BlockSpec((tm, d // 2), lambda i: (ntot + i, 0)),
            pl.BlockSpec((tm, SUBLANES), lambda i: (i, 0)),
            pl.BlockSpec((None, 1, d), lambda i: (mod_row(i), 0, gate_blk))],
        out_specs=tuple(row_specs),
        compiler_params=_cparams(("arbitrary",)),
        name="moe_combine",
    )(*row_args, yg, yg, route_cols, mod3)
    return out if has_ctx else (out[0], None)


def _dispatch_plan(route, tm):
    n = route.shape[1]
    e_flat = jnp.concatenate([route[0], route[1]]).astype(jnp.int32)
    rank = jnp.concatenate([route[4], route[5]]).astype(jnp.int32)
    n_assign = 2 * n
    n_tiles = n_assign // tm + N_EXPERTS
    experts = jnp.arange(N_EXPERTS, dtype=jnp.int32)[:, None]
    onehot = (experts == e_flat[None, :]).astype(jnp.int32)
    counts = jnp.sum(onehot, axis=1)
    padded = ((counts + tm - 1) // tm) * tm
    ends_p = jnp.cumsum(padded)
    starts_p = ends_p - padded
    starts_c = jnp.cumsum(counts) - counts
    dest = jnp.sum(onehot * starts_p[:, None], axis=0) + rank
    by_expert = jnp.sum(onehot * starts_c[:, None], axis=0) + rank
    order = jnp.argsort(by_expert).astype(jnp.int32)
    p = jnp.arange(n_tiles * tm, dtype=jnp.int32)[None, :]
    owner = ((p >= starts_p[:, None]) & (p < ends_p[:, None])).astype(jnp.int32)
    within = jnp.sum(owner * (p - starts_p[:, None]), axis=0)
    live = jnp.sum(owner * (p - starts_p[:, None] < counts[:, None]), axis=0) > 0
    compact = jnp.sum(owner * starts_c[:, None], axis=0) + within
    src_tok = jnp.where(live, order[jnp.clip(compact, 0, n_assign - 1)] % n, p[0] % n)
    tile_start = jnp.arange(n_tiles, dtype=jnp.int32) * tm
    tile_valid = (tile_start < ends_p[-1]).astype(jnp.int32)
    last_tile = jnp.maximum(ends_p[-1] // tm - 1, 0) * tm
    tile_row = jnp.minimum(tile_start, last_tile)[:, None]
    tile_expert = jnp.sum((ends_p[None, :] <= tile_row).astype(jnp.int32), axis=1)
    tile_expert = jnp.minimum(tile_expert, N_EXPERTS - 1)
    return src_tok, dest, tile_expert, tile_valid


def _moe(x, cx, g, mod3, rw_t, rb, wg, wu, wd, layer, rows_per_mod, ctx_mod_row):
    h, route = _norm_router(x, cx, g, mod3, 3, 4, rw_t, rb, rows_per_mod, ctx_mod_row)
    src_tok, dest, tile_expert, tile_valid = _dispatch_plan(route, MOE_TM)
    n_tiles = tile_expert.shape[0]
    bounds = [n_tiles * k // MOE_RANGES for k in range(MOE_RANGES + 1)]
    ys = None
    for t0, t1 in zip(bounds[:-1], bounds[1:]):
        xs = _sc_gather_rows(h, src_tok[t0 * MOE_TM:t1 * MOE_TM])
        ys = _grouped_mlp(tile_expert, tile_valid, xs, t0, ys, wg, wu, wd, layer)
    yg = _sc_gather_rows(ys, dest)
    route_cols = route.T
    return _combine(x, cx, yg, route_cols, mod3, 5, rows_per_mod, ctx_mod_row)


def _rope_tables(seq):
    rows = seq // GRID_W
    row = jnp.repeat(jnp.arange(rows, dtype=F32), GRID_W)
    col = jnp.tile(jnp.arange(GRID_W, dtype=F32), rows)
    n_freq = HEAD_DIM // 4
    inv_freq = ROPE_BASE ** (-jnp.arange(n_freq, dtype=F32) / n_freq)
    ang = jnp.concatenate([row[:, None] * inv_freq, col[:, None] * inv_freq], axis=-1)
    cos, sin = jnp.cos(ang), jnp.sin(ang)
    return jnp.concatenate([cos, cos], axis=-1), jnp.concatenate([-sin, sin], axis=-1)


def kernel(x, c, ctx, c_ctx, ada_w, ada_b, norm_mix, norm_ffn, ab_w_in, ab_q_gain, ab_k_gain, ab_sink, ab_conv_w, ab_conv_b, ab_gate_a_w, ab_gate_a_b, ab_gate_x_w, ab_gate_x_b, ab_lru_lambda, ab_w_out, gqa_w_in, gqa_q_gain, gqa_k_gain, gqa_w_out, router_w, router_bias, moe_w_gate, moe_w_up, moe_w_down):
    batch, seq, d = x.shape
    ctx_len = ctx.shape[1]
    depth = ada_w.shape[0]
    assert depth == 2 and batch < SUBLANES
    n_lat = batch * seq
    n_ctx = batch * ctx_len
    ctx_row = batch

    xl = x.reshape(n_lat, d)
    xc = ctx.reshape(n_ctx, d)
    cc = jnp.zeros((SUBLANES, d), F32).at[:batch].set(c).at[ctx_row].set(c_ctx)
    mod = _ada(cc, ada_w, ada_b)
    cos2, sin2 = _rope_tables(seq)
    rw_t = router_w.T.astype(BF16)
    rb = router_bias.reshape(N_EXPERTS, 1).astype(F32)
    experts = (moe_w_gate, moe_w_up, moe_w_down)

    mod3 = mod[0].reshape(SUBLANES, 1, 6 * d)
    w_in = ab_w_in[0].astype(BF16)
    proj = _norm_mod_matmul(xl, norm_mix[0], mod3, 0, 1, w_in, seq, 0,
                            A_Q_HEADS, A_KV_HEADS, ab_q_gain[0], ab_k_gain[0], cos2, sin2)
    proj_c = _norm_mod_matmul(xc, norm_mix[0], mod3, 0, 1, w_in, n_ctx, ctx_row,
                              A_Q_HEADS, A_KV_HEADS, ab_q_gain[0], ab_k_gain[0])
    att = _win_attn(ab_sink[0], proj, proj_c, batch, seq, ctx_len)
    att_c = _ctx_attn(ab_sink[0], proj_c, batch, ctx_len)

    lru_w = ab_conv_w.shape[2]
    c0 = (A_Q_HEADS + 2 * A_KV_HEADS) * HEAD_DIM
    y_p, yc_p = _lru(_to_chunked(proj[:, c0:c0 + lru_w], batch),
                     _to_chunked(proj[:, c0 + lru_w:c0 + 2 * lru_w], batch),
                     _to_chunked(proj_c[:, c0:c0 + lru_w], batch),
                     _to_chunked(proj_c[:, c0 + lru_w:c0 + 2 * lru_w], batch),
                     ab_conv_w[0], ab_conv_b[0], ab_gate_a_w[0], ab_gate_a_b[0],
                     ab_gate_x_w[0], ab_gate_x_b[0], ab_lru_lambda[0])
    w_out = ab_w_out[0].astype(BF16)
    xl = _out_proj(att, 0, _from_chunked(y_p), 0, w_out, xl, mod3, 2, seq, 0)
    xc = _out_proj(att_c, 0, _from_chunked(yc_p), 0, w_out, xc, mod3, 2, n_ctx, ctx_row)
    xl, xc = _moe(xl, xc, norm_ffn[0], mod3, rw_t, rb, *experts, 0, seq, ctx_row)

    mod3 = mod[1].reshape(SUBLANES, 1, 6 * d)
    w_in = gqa_w_in[0].astype(BF16)
    cw = C_Q_HEADS * HEAD_DIM
    proj = _norm_mod_matmul(xl, norm_mix[1], mod3, 0, 1, w_in, seq, 0,
                            C_Q_HEADS, C_KV_HEADS, gqa_q_gain[0], gqa_k_gain[0], cos2, sin2)
    proj_c = _norm_mod_matmul(xc, norm_mix[1], mod3, 0, 1, w_in[:, cw:], n_ctx, ctx_row,
                              0, C_KV_HEADS, gqa_q_gain[0], gqa_k_gain[0])
    att = _dense_attn(proj, proj_c, batch, seq, ctx_len)
    xl = _out_proj(att, 0, att, 1, gqa_w_out[0].astype(BF16), xl, mod3, 2, seq, 0)
    xl, _ = _moe(xl, None, norm_ffn[1], mod3, rw_t, rb, *experts, 1, seq, ctx_row)
    return xl.reshape(batch, seq, d)
```

```python
import functools

import jax
import jax.numpy as jnp
import numpy as np
from jax import lax
from jax.experimental import pallas as pl
from jax.experimental.pallas import tpu as pltpu
from jax.experimental.pallas import tpu_sc as plsc

F32 = jnp.float32
BF16 = jnp.bfloat16

LANES = 128
SUBLANES = 8
VMEM_LIMIT = 56 * 1024 * 1024

HEAD_DIM = 128
GRID_W = 64
WINDOW = 128
BLOCK = 128
ROPE_BASE = 10000.0
EPS = 1e-6
ATTN_SCALE = HEAD_DIM ** -0.5
A_Q_HEADS, A_KV_HEADS = 8, 2
C_Q_HEADS, C_KV_HEADS = 16, 4
GQA_GROUP = 4
LRU_C = 8.0
CONV_W = 4
CONV_LEFT = 2
N_EXPERTS = 16
N_GROUPS = 4
EXPERTS_PER_GROUP = 4
NEG_BIG = -1e30

LRU_CHUNKS = SUBLANES
LRU_JB = 16
LRU_LANE_BLOCKS = 2
MOE_TM = 256
MOE_RANGES = 2
SC_CORES = 2
SC_WORKERS = 32
SC_GATHER_WINDOW = 32
SC_GATHER_BUFFERS = 3
SC_GATHER_SUB = 8


def _cparams(sem, vmem=VMEM_LIMIT):
    return pltpu.CompilerParams(dimension_semantics=sem, vmem_limit_bytes=vmem)


def _dot(a, b):
    return jnp.dot(a, b, preferred_element_type=F32)


def _dot_nt(a, b):
    return lax.dot_general(a, b, (((1,), (1,)), ((), ())), preferred_element_type=F32)


def _ada_kernel(c_ref, w_ref, b_ref, o_ref):
    c = c_ref[...]
    s = (c * jax.nn.sigmoid(c)).astype(BF16)
    o_ref[0] = _dot(s, w_ref[0].astype(BF16)) + b_ref[0]


def _ada(cc, ada_w, ada_b):
    depth, d, n = ada_w.shape
    tn = 1024
    return pl.pallas_call(
        _ada_kernel,
        out_shape=jax.ShapeDtypeStruct((depth, SUBLANES, n), F32),
        grid=(depth, n // tn),
        in_specs=[pl.BlockSpec((SUBLANES, d), lambda l, j: (0, 0)),
                  pl.BlockSpec((1, d, tn), lambda l, j: (l, 0, j)),
                  pl.BlockSpec((1, 1, tn), lambda l, j: (l, 0, j))],
        out_specs=pl.BlockSpec((1, SUBLANES, tn), lambda l, j: (l, 0, j)),
        compiler_params=_cparams(("arbitrary", "arbitrary")),
        name="ada",
    )(cc, ada_w, ada_b.reshape(depth, 1, n))


def _norm_mod_rows(x_ref, g_ref, sh_ref, sc_ref, dst_ref, tm, rc=128, load_rows=None):
    g = g_ref[...]
    sc1 = 1.0 + sc_ref[...]
    sh = sh_ref[...]

    def body(r, carry):
        rows = pl.ds(pl.multiple_of(r * rc, rc), rc)
        xf = x_ref[rows, :] if load_rows is None else load_rows(rows)
        ms = jnp.mean(xf * xf, axis=-1, keepdims=True)
        xn = (xf * lax.rsqrt(ms + EPS)) * g
        dst_ref[rows, :] = (xn * sc1 + sh).astype(dst_ref.dtype)
        return carry

    lax.fori_loop(0, tm // rc, body, 0)


def _moe_residual_rows(x_ref, y0_ref, y1_ref, r_ref, gate_ref, rows):
    w0 = r_ref[rows, 2:3]
    w1 = r_ref[rows, 3:4]
    lo0, hi0 = _unpack_halves(y0_ref[rows, :])
    lo1, hi1 = _unpack_halves(y1_ref[rows, :])
    f = jnp.concatenate([w0 * lo0 + w1 * lo1, w0 * hi0 + w1 * hi1], axis=1)
    return x_ref[rows, :] + gate_ref[...] * f


def _nm_mm_kernel(*refs, tm, tn, n_q, n_k, rope, pending_moe):
    refs = list(refs)
    x_ref = refs.pop(0)
    if pending_moe:
        y0_ref, y1_ref, r_ref, gate_ref = (refs.pop(0) for _ in range(4))
    g_ref, sh_ref, sc_ref, w_ref, qg_ref, kg_ref = (refs.pop(0) for _ in range(6))
    if rope:
        cos_ref, sin_ref = refs.pop(0), refs.pop(0)
    o_ref = refs.pop(0)
    if pending_moe:
        xnew_ref = refs.pop(0)

        def load_rows(rows):
            xf = _moe_residual_rows(x_ref, y0_ref, y1_ref, r_ref, gate_ref, rows)
            xnew_ref[rows, :] = xf
            return xf
    else:
        load_rows = None
    hn_ref = refs.pop(0)
    _norm_mod_rows(x_ref, g_ref, sh_ref, sc_ref, hn_ref, tm, load_rows=load_rows)
    h = hn_ref[...]
    heads_per_chunk = tn // HEAD_DIM
    for j in range(w_ref.shape[1] // tn):
        cols = slice(j * tn, (j + 1) * tn)
        y = _dot(h, w_ref[:, cols])
        parts = []
        for hh in range(heads_per_chunk):
            head = j * heads_per_chunk + hh
            yh = y[:, _head_cols(hh)]
            if head < n_q + n_k:
                gain = qg_ref[...] if head < n_q else kg_ref[...]
                ms = jnp.mean(yh * yh, axis=-1, keepdims=True)
                yh = (yh * lax.rsqrt(ms + EPS)) * gain
                if rope:
                    yh = yh * cos_ref[...] + pltpu.roll(yh, HEAD_DIM // 2, 1) * sin_ref[...]
                if head < n_q:
                    yh = yh * ATTN_SCALE
            parts.append(yh.astype(o_ref.dtype))
        o_ref[:, cols] = jnp.concatenate(parts, axis=1)


def _norm_mod_matmul(x, g, mod3, shift_blk, scale_blk, w, rows_per_mod, mod_base,
                     n_q, n_k, q_gain, k_gain, cos2=None, sin2=None, pending_moe=None, tm=512, tn=512):
    n, d = x.shape
    nout = w.shape[1]
    tm = min(tm, n)
    tn = min(tn, nout)
    per = rows_per_mod // tm
    rope = cos2 is not None

    def mod_row(i):
        return mod_base + i // per

    head_vec = pl.BlockSpec((1, HEAD_DIM), lambda i: (0, 0))
    in_specs = [pl.BlockSpec((tm, d), lambda i: (i, 0))]
    args = [x]
    out_shape = [jax.ShapeDtypeStruct((n, nout), BF16)]
    out_specs = [pl.BlockSpec((tm, nout), lambda i: (i, 0))]
    if pending_moe is not None:
        yg, route_cols, first_row, mod3_prev, gate_blk = pending_moe
        t0 = first_row // tm
        slot1 = yg.shape[0] // 2 // tm
        in_specs += [pl.BlockSpec((tm, d // 2), lambda i: (t0 + i, 0)),
                     pl.BlockSpec((tm, d // 2), lambda i: (slot1 + t0 + i, 0)),
                     pl.BlockSpec((tm, SUBLANES), lambda i: (t0 + i, 0)),
                     pl.BlockSpec((None, 1, d), lambda i: (mod_row(i), 0, gate_blk))]
        args += [yg, yg, route_cols, mod3_prev]
        out_shape.append(jax.ShapeDtypeStruct((n, d), F32))
        out_specs.append(pl.BlockSpec((tm, d), lambda i: (i, 0)))
    in_specs += [pl.BlockSpec((1, d), lambda i: (0, 0)),
                 pl.BlockSpec((None, 1, d), lambda i: (mod_row(i), 0, shift_blk)),
                 pl.BlockSpec((None, 1, d), lambda i: (mod_row(i), 0, scale_blk)),
                 pl.BlockSpec((d, nout), lambda i: (0, 0), pipeline_mode=pl.Buffered(1)),
                 head_vec, head_vec]
    args += [g.reshape(1, d), mod3, mod3, w, q_gain.reshape(1, HEAD_DIM), k_gain.reshape(1, HEAD_DIM)]
    if rope:
        tiles_per_seq = cos2.shape[0] // tm
        table = pl.BlockSpec((tm, HEAD_DIM), lambda i: (i % tiles_per_seq, 0))
        in_specs += [table, table]
        args += [cos2, sin2]
    out = pl.pallas_call(
        functools.partial(_nm_mm_kernel, tm=tm, tn=tn, n_q=n_q, n_k=n_k, rope=rope,
                          pending_moe=pending_moe is not None),
        out_shape=tuple(out_shape),
        grid=(n // tm,),
        in_specs=in_specs,
        out_specs=tuple(out_specs),
        scratch_shapes=[pltpu.VMEM((tm, d), BF16)],
        compiler_params=_cparams(("parallel",)),
        name="norm_mod_matmul",
    )(*args)
    return out if pending_moe is not None else out[0]


def _head_cols(h):
    return slice(h * HEAD_DIM, (h + 1) * HEAD_DIM)


def _stack_group(q_ref, kvh):
    return jnp.concatenate([q_ref[:, _head_cols(kvh * GQA_GROUP + g)] for g in range(GQA_GROUP)], axis=0)


def _sink_col(sink_ref, kvh, rows):
    return jnp.concatenate([jnp.full((rows, 1), sink_ref[kvh * GQA_GROUP + g], F32)
                            for g in range(GQA_GROUP)], axis=0)


def _band_bias(ctx_len):
    rows, nk = GQA_GROUP * BLOCK, 3 * BLOCK + ctx_len
    qi = np.arange(rows)[:, None] % BLOCK
    kj = np.arange(nk)[None, :]
    inner = (kj >= 3 * BLOCK) | (np.abs(kj - BLOCK - qi) <= WINDOW)
    first = inner & ~(kj < BLOCK)
    last = inner & ~((kj >= 2 * BLOCK) & (kj < 3 * BLOCK))
    return np.where(np.stack([first, inner, last]), 0.0, NEG_BIG).astype(np.float32)


def _win_attn_kernel(sink_ref, bias_ref, q_ref, kp_ref, kc_ref, kn_ref, vp_ref, vc_ref, vn_ref,
                     kx_ref, vx_ref, o_ref, *, ctx_len):
    nk = 3 * BLOCK + ctx_len
    bias = bias_ref[...]
    ones = jnp.ones((nk, HEAD_DIM), BF16)
    for kvh in range(A_KV_HEADS):
        cols = _head_cols(kvh)
        q4 = _stack_group(q_ref, kvh)
        ka = jnp.concatenate([kp_ref[:, cols], kc_ref[:, cols], kn_ref[:, cols], kx_ref[:, cols]], axis=0)
        va = jnp.concatenate([vp_ref[:, cols], vc_ref[:, cols], vn_ref[:, cols], vx_ref[:, cols]], axis=0)
        s = _dot_nt(q4, ka) + bias
        sk = _sink_col(sink_ref, kvh, BLOCK)
        m = jnp.maximum(jnp.max(s, axis=-1, keepdims=True), sk)
        p = jnp.exp((s - m).astype(BF16))
        acc = _dot(p, jnp.concatenate([va, ones], axis=1))
        o = acc[:, 0:HEAD_DIM] / (acc[:, HEAD_DIM:] + jnp.exp(sk - m))
        for g in range(GQA_GROUP):
            o_ref[:, _head_cols(kvh * GQA_GROUP + g)] = o[g * BLOCK:(g + 1) * BLOCK].astype(o_ref.dtype)


def _win_attn(sink, proj, proj_c, batch, seq, ctx_len):
    nb = seq // BLOCK
    assert nb >= 2
    kvw = A_KV_HEADS * HEAD_DIM
    k_blk = A_Q_HEADS * HEAD_DIM // kvw
    v_blk = k_blk + 1
    bias = jnp.asarray(_band_bias(ctx_len))

    def which_bias(b, n):
        return (jnp.where(n == 0, 0, jnp.where(n == nb - 1, 2, 1)), 0, 0)

    def prev(b, n):
        return b * nb + jnp.maximum(n - 1, 0)

    def cur(b, n):
        return b * nb + n

    def nxt(b, n):
        return b * nb + jnp.minimum(n + 1, nb - 1)

    return pl.pallas_call(
        functools.partial(_win_attn_kernel, ctx_len=ctx_len),
        out_shape=jax.ShapeDtypeStruct((batch * seq, A_Q_HEADS * HEAD_DIM), BF16),
        grid=(batch, nb),
        in_specs=[pl.BlockSpec(memory_space=pltpu.SMEM),
                  pl.BlockSpec((None,) + bias.shape[1:], which_bias),
                  pl.BlockSpec((BLOCK, A_Q_HEADS * HEAD_DIM), lambda b, n: (cur(b, n), 0)),
                  pl.BlockSpec((BLOCK, kvw), lambda b, n: (prev(b, n), k_blk)),
                  pl.BlockSpec((BLOCK, kvw), lambda b, n: (cur(b, n), k_blk)),
                  pl.BlockSpec((BLOCK, kvw), lambda b, n: (nxt(b, n), k_blk)),
                  pl.BlockSpec((BLOCK, kvw), lambda b, n: (prev(b, n), v_blk)),
                  pl.BlockSpec((BLOCK, kvw), lambda b, n: (cur(b, n), v_blk)),
                  pl.BlockSpec((BLOCK, kvw), lambda b, n: (nxt(b, n), v_blk)),
                  pl.BlockSpec((ctx_len, kvw), lambda b, n: (b, k_blk)),
                  pl.BlockSpec((ctx_len, kvw), lambda b, n: (b, v_blk))],
        out_specs=pl.BlockSpec((BLOCK, A_Q_HEADS * HEAD_DIM), lambda b, n: (cur(b, n), 0)),
        compiler_params=_cparams(("parallel", "parallel")),
        name="win_attn",
    )(sink, bias, proj, proj, proj, proj, proj, proj, proj, proj_c, proj_c)


def _ctx_attn_kernel(sink_ref, q_ref, k_ref, v_ref, o_ref, *, ctx_len):
    kvh = pl.program_id(1)
    q4 = jnp.concatenate([q_ref[:, _head_cols(g)] for g in range(GQA_GROUP)], axis=0)
    s = _dot_nt(q4, k_ref[...])
    sk = jnp.concatenate([jnp.full((ctx_len, 1), sink_ref[kvh * GQA_GROUP + g], F32)
                          for g in range(GQA_GROUP)], axis=0)
    m = jnp.maximum(jnp.max(s, axis=-1, keepdims=True), sk)
    p = jnp.exp(s - m)
    den = jnp.sum(p, axis=-1, keepdims=True) + jnp.exp(sk - m)
    o = _dot(p.astype(BF16), v_ref[...]) / den
    for g in range(GQA_GROUP):
        o_ref[:, _head_cols(g)] = o[g * ctx_len:(g + 1) * ctx_len].astype(o_ref.dtype)


def _ctx_attn(sink, proj_c, batch, ctx_len):
    k_blk = A_Q_HEADS
    v_blk = A_Q_HEADS + A_KV_HEADS
    gw = GQA_GROUP * HEAD_DIM
    return pl.pallas_call(
        functools.partial(_ctx_attn_kernel, ctx_len=ctx_len),
        out_shape=jax.ShapeDtypeStruct((batch * ctx_len, A_Q_HEADS * HEAD_DIM), BF16),
        grid=(batch, A_KV_HEADS),
        in_specs=[pl.BlockSpec(memory_space=pltpu.SMEM),
                  pl.BlockSpec((ctx_len, gw), lambda b, h: (b, h)),
                  pl.BlockSpec((ctx_len, HEAD_DIM), lambda b, h: (b, k_blk + h)),
                  pl.BlockSpec((ctx_len, HEAD_DIM), lambda b, h: (b, v_blk + h))],
        out_specs=pl.BlockSpec((ctx_len, gw), lambda b, h: (b, h)),
        compiler_params=_cparams(("parallel", "parallel")),
        name="ctx_attn",
    )(sink, proj_c, proj_c, proj_c)


def _sigmoid(x):
    return 0.5 * jnp.tanh(0.5 * x) + 0.5


def _gelu_tanh(x):
    return 0.5 * x * (1.0 + jnp.tanh(0.7978845608028654 * (x + 0.044715 * (x * x * x))))


def _lru_sequence(x_ref, xg_ref, y_ref, xp_ref, a_ref, b_ref, w, init, rows, jb):
    conv_w, conv_b, wa, ba, wx, bx, c_logsig = w
    width = x_ref.shape[-1]
    sub = lax.broadcasted_iota(jnp.int32, (1, SUBLANES, width), 1)

    def block_diag(ub, wd):
        return jnp.concatenate([_dot(ub[:, _head_cols(n)], wd[n]) for n in range(width // LANES)], axis=1)

    def fill(r, carry):
        rr = pl.ds(pl.multiple_of(r * jb, jb), jb)
        xp_ref[pl.ds(pl.multiple_of(r * jb, jb) + CONV_LEFT, jb)] = x_ref[rr].astype(F32)
        return carry

    lax.fori_loop(0, rows // jb, fill, 0)
    tail = x_ref[rows - CONV_LEFT:rows].astype(F32)
    xp_ref[0:CONV_LEFT] = jnp.where(sub == 0, 0.0, pltpu.roll(tail, 1, 1))
    head = x_ref[0:1].astype(F32)
    xp_ref[rows + CONV_LEFT:rows + CONV_LEFT + 1] = jnp.where(
        sub == SUBLANES - 1, 0.0, pltpu.roll(head, SUBLANES - 1, 1))

    def gates(r, carry):
        j0 = pl.multiple_of(r * jb, jb)
        u = conv_b
        for k in range(CONV_W):
            u = u + conv_w[k] * xp_ref[pl.ds(j0 + k, jb)]
        u2 = u.reshape(jb * SUBLANES, width)
        ub = u2.astype(BF16)
        for d in range(2):
            r_gate = _sigmoid(block_diag(ub, wa[d]) + ba[d])
            i_gate = _sigmoid(block_diag(ub, wx[d]) + bx[d])
            log_a = c_logsig[d] * r_gate
            a = jnp.exp(log_a)
            v = 1.0 - a * a
            root = jnp.where(v > 0.0, v * lax.rsqrt(v), 0.0)
            b = root * (i_gate * u2)
            a_ref[d, pl.ds(j0, jb)] = a.reshape(jb, SUBLANES, width)
            b_ref[d, pl.ds(j0, jb)] = b.reshape(jb, SUBLANES, width)
        return carry

    lax.fori_loop(0, rows // jb, gates, 0, unroll=2)

    def scan(j, carry):
        hf, pf, hb, pb = carry
        jr = rows - 1 - j
        af = a_ref[0, j]
        hf = af * hf + b_ref[0, j]
        pf = pf * af
        b_ref[0, j] = hf
        a_ref[0, j] = pf
        ab = a_ref[1, jr]
        hb = ab * hb + b_ref[1, jr]
        pb = pb * ab
        b_ref[1, jr] = hb
        a_ref[1, jr] = pb
        return hf, pf, hb, pb

    z = jnp.zeros((SUBLANES, width), F32)
    o = jnp.ones((SUBLANES, width), F32)
    lax.fori_loop(0, rows, scan, (z, o, z, o), unroll=8)

    hf_last, pf_last = b_ref[0, rows - 1], a_ref[0, rows - 1]
    hb_last, pb_last = b_ref[1, 0], a_ref[1, 0]
    s = init[0]
    carry_f = []
    for c in range(SUBLANES):
        carry_f.append(s)
        s = hf_last[c:c + 1] + pf_last[c:c + 1] * s
    out_f = s
    s = init[1]
    carry_b = [None] * SUBLANES
    for c in reversed(range(SUBLANES)):
        carry_b[c] = s
        s = hb_last[c:c + 1] + pb_last[c:c + 1] * s
    out_b = s
    cf = jnp.concatenate(carry_f, axis=0)
    cb = jnp.concatenate(carry_b, axis=0)

    def emit(r, carry):
        rr = pl.ds(pl.multiple_of(r * jb, jb), jb)
        h = (b_ref[0, rr] + a_ref[0, rr] * cf) + (b_ref[1, rr] + a_ref[1, rr] * cb)
        y_ref[rr] = (h * _gelu_tanh(xg_ref[rr].astype(F32))).astype(y_ref.dtype)
        return carry

    lax.fori_loop(0, rows // jb, emit, 0)
    return out_f, out_b


def _lru_kernel(xr_ref, xg_ref, xrc_ref, xgc_ref, cw_ref, cb_ref, wa_ref, ba_ref, wx_ref, bx_ref,
                lam_ref, y_ref, yc_ref, xp_ref, a_ref, b_ref, *, rows, rows_c):
    c_logsig = [LRU_C * jax.nn.log_sigmoid(lam_ref[d]) for d in range(2)]
    w = ([cw_ref[k] for k in range(CONV_W)], cb_ref[0],
         [wa_ref[d] for d in range(2)], [ba_ref[d] for d in range(2)],
         [wx_ref[d] for d in range(2)], [bx_ref[d] for d in range(2)], c_logsig)
    zero = jnp.zeros((1, xr_ref.shape[-1]), F32)
    sf, sb = _lru_sequence(xrc_ref, xgc_ref, yc_ref, xp_ref, a_ref, b_ref, w, (zero, zero),
                           rows_c, min(LRU_JB, rows_c))
    _lru_sequence(xr_ref, xg_ref, y_ref, xp_ref, a_ref, b_ref, w, (sf, sb), rows, LRU_JB)


def _lru(xr, xg, xrc, xgc, conv_w, conv_b, wa, ba, wx, bx, lam):
    batch, rows, _, width = xr.shape
    rows_c = xrc.shape[1]
    cw = LRU_LANE_BLOCKS * LANES
    seq_spec = pl.BlockSpec((None, rows, SUBLANES, cw), lambda b, n: (b, 0, 0, n))
    ctx_spec = pl.BlockSpec((None, rows_c, SUBLANES, cw), lambda b, n: (b, 0, 0, n))
    vec2 = pl.BlockSpec((2, 1, cw), lambda b, n: (0, 0, n))
    mat2 = pl.BlockSpec((2, LRU_LANE_BLOCKS, LANES, LANES), lambda b, n: (0, n, 0, 0))
    return pl.pallas_call(
        functools.partial(_lru_kernel, rows=rows, rows_c=rows_c),
        out_shape=(jax.ShapeDtypeStruct(xr.shape, BF16), jax.ShapeDtypeStruct(xrc.shape, BF16)),
        grid=(batch, width // cw),
        in_specs=[seq_spec, seq_spec, ctx_spec, ctx_spec,
                  pl.BlockSpec((CONV_W, 1, cw), lambda b, n: (0, 0, n)),
                  pl.BlockSpec((1, 1, cw), lambda b, n: (0, 0, n)),
                  mat2, vec2, mat2, vec2, vec2],
        out_specs=(seq_spec, ctx_spec),
        scratch_shapes=[pltpu.VMEM((rows + CONV_W - 1, SUBLANES, cw), F32),
                        pltpu.VMEM((2, rows, SUBLANES, cw), F32),
                        pltpu.VMEM((2, rows, SUBLANES, cw), F32)],
        compiler_params=_cparams(("parallel", "parallel")),
        name="rglru",
    )(xr, xg, xrc, xgc, conv_w.reshape(CONV_W, 1, width), conv_b.reshape(1, 1, width),
      wa.astype(BF16), ba.reshape(2, 1, width), wx.astype(BF16), bx.reshape(2, 1, width),
      lam.reshape(2, 1, width))


def _to_chunked(a, batch):
    t = a.shape[0] // batch
    return a.reshape(batch, LRU_CHUNKS, t // LRU_CHUNKS, a.shape[1]).transpose(0, 2, 1, 3)


def _from_chunked(a):
    b, r, c, w = a.shape
    return a.transpose(0, 2, 1, 3).reshape(b * r * c, w)


def _out_proj_kernel(a1_ref, a2_ref, w1_ref, w2_ref, x_ref, g_ref, o_ref):
    y = _dot(a1_ref[...], w1_ref[...]) + _dot(a2_ref[...], w2_ref[...])
    o_ref[...] = x_ref[...] + g_ref[...] * y


def _out_proj(a1, a1_blk, a2, a2_blk, w, x, mod3, gate_blk, rows_per_mod, mod_base, tm=2048, tn=512):
    n, d = x.shape
    kh = w.shape[0] // 2
    tm = min(tm, n)
    per = rows_per_mod // tm
    gpb = d // tn
    return pl.pallas_call(
        _out_proj_kernel,
        out_shape=jax.ShapeDtypeStruct((n, d), F32),
        grid=(n // tm, d // tn),
        in_specs=[pl.BlockSpec((tm, kh), lambda i, j: (i, a1_blk)),
                  pl.BlockSpec((tm, kh), lambda i, j: (i, a2_blk)),
                  pl.BlockSpec((kh, tn), lambda i, j: (0, j)),
                  pl.BlockSpec((kh, tn), lambda i, j: (1, j)),
                  pl.BlockSpec((tm, tn), lambda i, j: (i, j)),
                  pl.BlockSpec((None, 1, tn), lambda i, j: (mod_base + i // per, 0, gate_blk * gpb + j))],
        out_specs=pl.BlockSpec((tm, tn), lambda i, j: (i, j)),
        compiler_params=_cparams(("parallel", "parallel")),
        name="out_proj",
    )(a1, a2, w, w, x, mod3)


def _lane_block_max(s):
    mm = s[:, 0:LANES]
    for t in range(1, s.shape[1] // LANES):
        mm = jnp.maximum(mm, s[:, t * LANES:(t + 1) * LANES])
    return mm


def _exp_blocks(s, mrep):
    return jnp.concatenate(
        [jnp.exp((s[:, t * LANES:(t + 1) * LANES] - mrep).astype(BF16)) for t in range(s.shape[1] // LANES)],
        axis=1)


def _copy_key_rows(dst_ref, c, kc, lat_ref, ctx_ref, cols):
    seq = lat_ref.shape[0]
    lo, hi = c * kc, (c + 1) * kc
    if lo < seq:
        n = min(hi, seq) - lo
        dst_ref[c, 0:n, cols] = lat_ref[lo:lo + n, :]
    if hi > seq:
        start = max(lo, seq)
        dst_ref[c, start - lo:kc, cols] = ctx_ref[start - seq:hi - seq, :]


def _dense_attn_kernel(q_ref, k_ref, v_ref, kx_ref, vx_ref, o_ref,
                       s_ref, m_ref, acc_ref, ka_ref, va_ref, *, tq, n_chunks, kc):
    @pl.when(pl.program_id(2) == 0)
    def _():
        for c in range(n_chunks):
            _copy_key_rows(ka_ref, c, kc, k_ref, kx_ref, slice(0, HEAD_DIM))
            _copy_key_rows(va_ref, c, kc, v_ref, vx_ref, slice(0, HEAD_DIM))
            va_ref[c, :, HEAD_DIM:] = jnp.ones((kc, HEAD_DIM), BF16)

    q4 = jnp.concatenate([q_ref[:, _head_cols(g)] for g in range(GQA_GROUP)], axis=0)
    for c in range(n_chunks):
        s = _dot_nt(q4, ka_ref[c])
        s_ref[c] = s
        block_max = _lane_block_max(s)
        m_ref[...] = block_max if c == 0 else jnp.maximum(m_ref[...], block_max)
    m_ref[...] = jnp.broadcast_to(jnp.max(m_ref[...], axis=-1, keepdims=True), m_ref.shape)
    for c in range(n_chunks):
        part = _dot(_exp_blocks(s_ref[c], m_ref[...]), va_ref[c])
        if c == 0:
            acc_ref[...] = part
        else:
            acc_ref[...] += part
    o = acc_ref[:, 0:HEAD_DIM] / acc_ref[:, HEAD_DIM:]
    for g in range(GQA_GROUP):
        o_ref[:, _head_cols(g)] = o[g * tq:(g + 1) * tq].astype(o_ref.dtype)


def _dense_attn(proj, proj_c, batch, seq, ctx_len, tq=256, n_chunks=2):
    gw = GQA_GROUP * HEAD_DIM
    nq = seq // tq
    rows = GQA_GROUP * tq
    kc = (seq + ctx_len) // n_chunks
    assert kc * n_chunks == seq + ctx_len and kc % LANES == 0
    k_blk = C_Q_HEADS
    v_blk = C_Q_HEADS + C_KV_HEADS
    vx_blk = C_KV_HEADS
    return pl.pallas_call(
        functools.partial(_dense_attn_kernel, tq=tq, n_chunks=n_chunks, kc=kc),
        out_shape=jax.ShapeDtypeStruct((batch * seq, C_Q_HEADS * HEAD_DIM), BF16),
        grid=(batch, C_KV_HEADS, nq),
        in_specs=[pl.BlockSpec((tq, gw), lambda b, h, i: (b * nq + i, h)),
                  pl.BlockSpec((seq, HEAD_DIM), lambda b, h, i: (b, k_blk + h)),
                  pl.BlockSpec((seq, HEAD_DIM), lambda b, h, i: (b, v_blk + h)),
                  pl.BlockSpec((ctx_len, HEAD_DIM), lambda b, h, i: (b, h)),
                  pl.BlockSpec((ctx_len, HEAD_DIM), lambda b, h, i: (b, vx_blk + h))],
        out_specs=pl.BlockSpec((tq, gw), lambda b, h, i: (b * nq + i, h)),
        scratch_shapes=[pltpu.VMEM((n_chunks, rows, kc), F32),
                        pltpu.VMEM((rows, LANES), F32),
                        pltpu.VMEM((rows, 2 * HEAD_DIM), F32),
                        pltpu.VMEM((n_chunks, kc, HEAD_DIM), BF16),
                        pltpu.VMEM((n_chunks, kc, 2 * HEAD_DIM), BF16)],
        compiler_params=_cparams(("parallel", "parallel", "arbitrary")),
        name="dense_attn",
    )(proj, proj, proj, proj_c, proj_c)


def _router_rows(biased, scores):
    v = [biased[e:e + 1, :] for e in range(N_EXPERTS)]
    s = [scores[e:e + 1, :] for e in range(N_EXPERTS)]

    def top2_sum(vals):
        best = vals[0] + vals[1]
        for i in range(len(vals)):
            for j in range(i + 1, len(vals)):
                if (i, j) != (0, 1):
                    best = jnp.maximum(best, vals[i] + vals[j])
        return best

    gsum = [top2_sum(v[g * EXPERTS_PER_GROUP:(g + 1) * EXPERTS_PER_GROUP]) for g in range(N_GROUPS)]
    sel = jnp.zeros_like(gsum[0], dtype=jnp.int32)
    best = gsum[0]
    for g in range(1, N_GROUPS):
        take = gsum[g] > best
        sel = jnp.where(take, g, sel)
        best = jnp.where(take, gsum[g], best)

    def pick_group(rows, i):
        out = rows[i]
        for g in range(1, N_GROUPS):
            out = jnp.where(sel == g, rows[g * EXPERTS_PER_GROUP + i], out)
        return out

    cand = [pick_group(v, i) for i in range(EXPERTS_PER_GROUP)]
    cand_s = [pick_group(s, i) for i in range(EXPERTS_PER_GROUP)]
    i1 = jnp.zeros_like(sel)
    b1 = cand[0]
    for i in range(1, EXPERTS_PER_GROUP):
        take = cand[i] > b1
        i1 = jnp.where(take, i, i1)
        b1 = jnp.where(take, cand[i], b1)
    i2 = jnp.full_like(sel, -1)
    b2 = jnp.full_like(b1, -jnp.inf)
    for i in range(EXPERTS_PER_GROUP):
        take = (i1 != i) & ((cand[i] > b2) | (i2 < 0))
        i2 = jnp.where(take, i, i2)
        b2 = jnp.where(take, cand[i], b2)

    def pick_idx(rows, idx):
        out = rows[0]
        for i in range(1, EXPERTS_PER_GROUP):
            out = jnp.where(idx == i, rows[i], out)
        return out

    s0 = pick_idx(cand_s, i1)
    s1 = pick_idx(cand_s, i2)
    tot = s0 + s1
    e0 = (sel * EXPERTS_PER_GROUP + i1).astype(F32)
    e1 = (sel * EXPERTS_PER_GROUP + i2).astype(F32)
    return e0, e1, s0 / tot, s1 / tot


def _norm_router_kernel(*refs, tm, n_lat_tiles, has_ctx):
    i = pl.program_id(0)
    if has_ctx:
        x_ref, xc_ref, g_ref, sh_ref, sc_ref, rw_ref, rb_ref, hp_ref, r_ref, h_ref, tri_ref, run_ref = refs

        @pl.when(i < n_lat_tiles)
        def _():
            _norm_mod_rows(x_ref, g_ref, sh_ref, sc_ref, h_ref, tm)

        @pl.when(i >= n_lat_tiles)
        def _():
            _norm_mod_rows(xc_ref, g_ref, sh_ref, sc_ref, h_ref, tm)
    else:
        x_ref, g_ref, sh_ref, sc_ref, rw_ref, rb_ref, hp_ref, r_ref, h_ref, tri_ref, run_ref = refs
        _norm_mod_rows(x_ref, g_ref, sh_ref, sc_ref, h_ref, tm)

    @pl.when(i == 0)
    def _():
        run_ref[...] = jnp.zeros(run_ref.shape, F32)
        before = lax.broadcasted_iota(jnp.int32, (tm, tm), 0) <= lax.broadcasted_iota(jnp.int32, (tm, tm), 1)
        tri_ref[...] = jnp.where(before, 1.0, 0.0).astype(BF16)

    def pack(r, carry):
        rows = pl.ds(pl.multiple_of(r * 128, 128), 128)
        hp_ref[rows, :] = _pack_halves(h_ref[rows, :])
        return carry

    lax.fori_loop(0, tm // 128, pack, 0)
    logits = _dot_nt(rw_ref[...], h_ref[...])
    scores = jax.nn.sigmoid(logits)
    e0, e1, w0, w1 = _router_rows(scores + rb_ref[...], scores)
    expert_ids = lax.broadcasted_iota(jnp.int32, (N_EXPERTS, tm), 0).astype(F32)
    ranks = []
    for e_row in (e0, e1):
        hit = expert_ids == e_row
        seen = _dot(jnp.where(hit, 1.0, 0.0).astype(BF16), tri_ref[...])
        ranks.append(jnp.sum(jnp.where(hit, seen - 1.0 + run_ref[...], 0.0), axis=0, keepdims=True))
        run_ref[...] = run_ref[...] + seen[:, tm - 1:tm]
    zero = jnp.zeros_like(w0)
    r_ref[...] = jnp.concatenate([e0, e1, w0, w1, ranks[0], ranks[1], zero, zero], axis=0)


def _pack_halves(v):
    c = v.shape[1] // 2
    lo = lax.bitcast_convert_type(v[:, :c].astype(BF16).astype(F32), jnp.uint32)
    hi = lax.bitcast_convert_type(v[:, c:].astype(BF16).astype(F32), jnp.uint32)
    return (lo >> 16) | (hi & jnp.uint32(0xFFFF0000))


def _unpack_halves(p):
    lo = lax.bitcast_convert_type(p << 16, F32)
    hi = lax.bitcast_convert_type(p & jnp.uint32(0xFFFF0000), F32)
    return lo, hi


def _sc_gather_rows(table, idx):
    n, w = idx.shape[0], table.shape[1]
    win, nb, sub = SC_GATHER_WINDOW, SC_GATHER_BUFFERS, SC_GATHER_SUB
    per = n // (win * SC_WORKERS)
    assert per * win * SC_WORKERS == n and per >= 1
    mesh = plsc.VectorSubcoreMesh(core_axis_name="core", subcore_axis_name="subcore")

    @functools.partial(
        pl.kernel, out_type=jax.ShapeDtypeStruct((n, w), table.dtype), mesh=mesh, name="sc_gather_rows",
        scratch_types=([pltpu.VMEM((per * win,), jnp.int32)] + [pltpu.VMEM((win, w), table.dtype)] * nb
                       + [pltpu.SemaphoreType.DMA] * (2 * nb)))
    def gather(x_hbm, i_hbm, o_hbm, i_v, *rest):
        bufs, gsems, wsems = rest[:nb], rest[nb:2 * nb], rest[2 * nb:]
        wid = lax.axis_index("subcore") * SC_CORES + lax.axis_index("core")
        base = wid * (per * win)
        pltpu.sync_copy(i_hbm.at[pl.ds(base, per * win)], i_v)
        rot = (wid * per) // SC_WORKERS

        def row0(t):
            u = t + rot
            u = u - per * (u >= per).astype(jnp.int32)
            return pl.multiple_of(u * win, win)

        def gather_copy(t, b, s):
            return pltpu.make_async_copy(x_hbm.at[i_v.at[pl.ds(row0(t) + s * sub, sub)]],
                                         bufs[b].at[pl.ds(s * sub, sub)], gsems[b])

        def write_copy(t, b):
            return pltpu.make_async_copy(bufs[b], o_hbm.at[pl.ds(base + row0(t), win)], wsems[b])

        def start_gathers(t, b):
            for s in range(win // sub):
                gather_copy(t, b, s).start()

        def step(t, b):
            for s in range(win // sub):
                gather_copy(t, b, s).wait()
            write_copy(t, b).start()

            @pl.when(t >= 1)
            def _():
                write_copy(t - 1, (b - 1) % nb).wait()

            @pl.when(t + nb - 1 < per)
            def _():
                start_gathers(t + nb - 1, (b + nb - 1) % nb)

        for t in range(min(nb - 1, per)):
            start_gathers(t, t)

        @pl.loop(0, per // nb)
        def _(p):
            for j in range(nb):
                step(nb * p + j, j)

        for t in range(per - per % nb, per):
            step(t, t % nb)
        write_copy(per - 1, (per - 1) % nb).wait()

    return gather(table, idx)


def _lat_ctx_maps(n_lat, per, ctx_mod_row):
    def mod_row(i):
        return jnp.where(i < n_lat, i // per, ctx_mod_row)

    def lat(i):
        return (jnp.minimum(i, n_lat - 1), 0)

    def ctx(i):
        return (jnp.maximum(i - n_lat, 0), 0)

    return mod_row, lat, ctx


def _norm_router(x, cx, g, mod3, shift_blk, scale_blk, rw_t, rb, rows_per_mod, ctx_mod_row, tm=512):
    n, d = x.shape
    n_lat = n // tm
    has_ctx = cx is not None
    ntot = n_lat + (cx.shape[0] // tm if has_ctx else 0)
    mod_row, lat, ctx = _lat_ctx_maps(n_lat, rows_per_mod // tm, ctx_mod_row)
    row_specs = [pl.BlockSpec((tm, d), lat)] + ([pl.BlockSpec((tm, d), ctx)] if has_ctx else [])
    row_args = [x] + ([cx] if has_ctx else [])
    return pl.pallas_call(
        functools.partial(_norm_router_kernel, tm=tm, n_lat_tiles=n_lat, has_ctx=has_ctx),
        out_shape=(jax.ShapeDtypeStruct((ntot * tm, d // 2), jnp.uint32),
                   jax.ShapeDtypeStruct((SUBLANES, ntot * tm), F32)),
        grid=(ntot,),
        in_specs=row_specs + [
            pl.BlockSpec((1, d), lambda i: (0, 0)),
            pl.BlockSpec((None, 1, d), lambda i: (mod_row(i), 0, shift_blk)),
            pl.BlockSpec((None, 1, d), lambda i: (mod_row(i), 0, scale_blk)),
            pl.BlockSpec((N_EXPERTS, d), lambda i: (0, 0)),
            pl.BlockSpec((N_EXPERTS, 1), lambda i: (0, 0))],
        out_specs=(pl.BlockSpec((tm, d // 2), lambda i: (i, 0)),
                   pl.BlockSpec((SUBLANES, tm), lambda i: (0, i))),
        scratch_shapes=[pltpu.VMEM((tm, d), BF16), pltpu.VMEM((tm, tm), BF16), pltpu.VMEM((N_EXPERTS, 1), F32)],
        compiler_params=_cparams(("arbitrary",)),
        name="norm_router",
    )(*row_args, g.reshape(1, d), mod3, mod3, rw_t, rb)


def _cast_rows(src_ref, dst_ref, rb=256):
    def body(r, carry):
        rows = pl.ds(pl.multiple_of(r * rb, rb), rb)
        dst_ref[rows, :] = src_ref[0, rows, :].astype(dst_ref.dtype)
        return carry

    lax.fori_loop(0, dst_ref.shape[0] // rb, body, 0)


def _tile_state(te_ref, tv_ref, tile0):
    i = pl.program_id(0)
    t = tile0 + i
    live = tv_ref[t] > 0
    new_expert = (i == 0) | (te_ref[t] != te_ref[jnp.maximum(t - 1, 0)])
    return live, new_expert


def _gmm_kernel(te_ref, tv_ref, xs_ref, wg_ref, wu_ref, wd_ref, *rest, tile0):
    o_ref, wg_b, wu_b, wd_b = rest[-4:]
    live, new_expert = _tile_state(te_ref, tv_ref, tile0)

    @pl.when(live & new_expert)
    def _():
        _cast_rows(wg_ref, wg_b)
        _cast_rows(wu_ref, wu_b)
        _cast_rows(wd_ref, wd_b)

    @pl.when(live)
    def _():
        lo, hi = _unpack_halves(xs_ref[...])
        x = jnp.concatenate([lo.astype(BF16), hi.astype(BF16)], axis=1)
        gate = _dot(x, wg_b[...])
        up = _dot(x, wu_b[...])
        h1 = ((gate * jax.nn.sigmoid(gate)) * up).astype(BF16)
        o_ref[...] = _pack_halves(_dot(h1, wd_b[...]))

    @pl.when(jnp.logical_not(live))
    def _():
        o_ref[...] = jnp.zeros(o_ref.shape, o_ref.dtype)


def _grouped_mlp(tile_expert, tile_valid, xs_part, tile0, ys_prev, wg, wu, wd, layer, tm=MOE_TM):
    dpk = xs_part.shape[1]
    d, dff = wg.shape[2], wg.shape[3]
    n_tiles = tile_expert.shape[0]

    def expert_block(rows, cols):
        return pl.BlockSpec((None, 1, rows, cols), lambda i, te, tv: (layer, te[tile0 + i], 0, 0),
                            pipeline_mode=pl.Buffered(1))

    in_specs = [pl.BlockSpec((tm, dpk), lambda i, te, tv: (i, 0)),
                expert_block(d, dff), expert_block(d, dff), expert_block(dff, d)]
    args = [tile_expert, tile_valid, xs_part, wg, wu, wd]
    aliases = {}
    if ys_prev is not None:
        in_specs.append(pl.BlockSpec(memory_space=pl.ANY))
        args.append(ys_prev)
        aliases = {len(args) - 1: 0}
    grid_spec = pltpu.PrefetchScalarGridSpec(
        num_scalar_prefetch=2,
        grid=(xs_part.shape[0] // tm,),
        in_specs=in_specs,
        out_specs=pl.BlockSpec((tm, dpk), lambda i, te, tv: (tile0 + i, 0)),
        scratch_shapes=[pltpu.VMEM((d, dff), BF16), pltpu.VMEM((d, dff), BF16), pltpu.VMEM((dff, d), BF16)],
    )
    return pl.pallas_call(
        functools.partial(_gmm_kernel, tile0=tile0),
        out_shape=jax.ShapeDtypeStruct((n_tiles * tm, dpk), jnp.uint32),
        grid_spec=grid_spec,
        input_output_aliases=aliases,
        compiler_params=_cparams(("arbitrary",)),
        name="grouped_mlp",
    )(*args)


def _combine_kernel(*refs, n_lat_tiles, has_ctx):
    if has_ctx:
        x_ref, xc_ref, y0_ref, y1_ref, r_ref, g_ref, o_ref, oc_ref = refs
    else:
        x_ref, y0_ref, y1_ref, r_ref, g_ref, o_ref = refs
    w0 = r_ref[:, 2:3]
    w1 = r_ref[:, 3:4]
    lo0, hi0 = _unpack_halves(y0_ref[...])
    lo1, hi1 = _unpack_halves(y1_ref[...])
    f = g_ref[...] * jnp.concatenate([w0 * lo0 + w1 * lo1, w0 * hi0 + w1 * hi1], axis=1)
    if not has_ctx:
        o_ref[...] = x_ref[...] + f
        return
    i = pl.program_id(0)

    @pl.when(i < n_lat_tiles)
    def _():
        o_ref[...] = x_ref[...] + f

    @pl.when(i >= n_lat_tiles)
    def _():
        oc_ref[...] = xc_ref[...] + f


def _combine(x, cx, yg, route_cols, mod3, gate_blk, rows_per_mod, ctx_mod_row, tm=512):
    n, d = x.shape
    n_lat = n // tm
    has_ctx = cx is not None
    ntot = n_lat + (cx.shape[0] // tm if has_ctx else 0)
    mod_row, lat, ctx = _lat_ctx_maps(n_lat, rows_per_mod // tm, ctx_mod_row)
    row_specs = [pl.BlockSpec((tm, d), lat)] + ([pl.BlockSpec((tm, d), ctx)] if has_ctx else [])
    row_args = [x] + ([cx] if has_ctx else [])
    out_shape = [jax.ShapeDtypeStruct(x.shape, F32)] + ([jax.ShapeDtypeStruct(cx.shape, F32)] if has_ctx else [])
    out = pl.pallas_call(
        functools.partial(_combine_kernel, n_lat_tiles=n_lat, has_ctx=has_ctx),
        out_shape=tuple(out_shape),
        grid=(ntot,),
        in_specs=row_specs + [
            pl.BlockSpec((tm, d // 2), lambda i: (i, 0)),
            pl.BlockSpec((tm, d // 2), lambda i: (ntot + i, 0)),
            pl.BlockSpec((tm, SUBLANES), lambda i: (i, 0)),
            pl.BlockSpec((None, 1, d), lambda i: (mod_row(i), 0, gate_blk))],
        out_specs=tuple(row_specs),
        compiler_params=_cparams(("arbitrary",)),
        name="moe_combine",
    )(*row_args, yg, yg, route_cols, mod3)
    return out if has_ctx else (out[0], None)


def _dispatch_plan(route, tm):
    n = route.shape[1]
    e_flat = jnp.concatenate([route[0], route[1]]).astype(jnp.int32)
    rank = jnp.concatenate([route[4], route[5]]).astype(jnp.int32)
    n_assign = 2 * n
    n_tiles = n_assign // tm + N_EXPERTS
    experts = jnp.arange(N_EXPERTS, dtype=jnp.int32)[:, None]
    onehot = (experts == e_flat[None, :]).astype(jnp.int32)
    counts = jnp.sum(onehot, axis=1)
    padded = ((counts + tm - 1) // tm) * tm
    ends_p = jnp.cumsum(padded)
    starts_p = ends_p - padded
    starts_c = jnp.cumsum(counts) - counts
    dest = jnp.sum(onehot * starts_p[:, None], axis=0) + rank
    by_expert = jnp.sum(onehot * starts_c[:, None], axis=0) + rank
    order = jnp.argsort(by_expert).astype(jnp.int32)
    p = jnp.arange(n_tiles * tm, dtype=jnp.int32)[None, :]
    owner = ((p >= starts_p[:, None]) & (p < ends_p[:, None])).astype(jnp.int32)
    within = jnp.sum(owner * (p - starts_p[:, None]), axis=0)
    live = jnp.sum(owner * (p - starts_p[:, None] < counts[:, None]), axis=0) > 0
    compact = jnp.sum(owner * starts_c[:, None], axis=0) + within
    src_tok = jnp.where(live, order[jnp.clip(compact, 0, n_assign - 1)] % n, p[0] % n)
    tile_start = jnp.arange(n_tiles, dtype=jnp.int32) * tm
    tile_valid = (tile_start < ends_p[-1]).astype(jnp.int32)
    last_tile = jnp.maximum(ends_p[-1] // tm - 1, 0) * tm
    tile_row = jnp.minimum(tile_start, last_tile)[:, None]
    tile_expert = jnp.sum((ends_p[None, :] <= tile_row).astype(jnp.int32), axis=1)
    tile_expert = jnp.minimum(tile_expert, N_EXPERTS - 1)
    return src_tok, dest, tile_expert, tile_valid


def _moe(x, cx, g, mod3, rw_t, rb, wg, wu, wd, layer, rows_per_mod, ctx_mod_row, combine=True):
    h, route = _norm_router(x, cx, g, mod3, 3, 4, rw_t, rb, rows_per_mod, ctx_mod_row)
    src_tok, dest, tile_expert, tile_valid = _dispatch_plan(route, MOE_TM)
    n_tiles = tile_expert.shape[0]
    bounds = [n_tiles * k // MOE_RANGES for k in range(MOE_RANGES + 1)]
    ys = None
    for t0, t1 in zip(bounds[:-1], bounds[1:]):
        xs = _sc_gather_rows(h, src_tok[t0 * MOE_TM:t1 * MOE_TM])
        ys = _grouped_mlp(tile_expert, tile_valid, xs, t0, ys, wg, wu, wd, layer)
    yg = _sc_gather_rows(ys, dest)
    route_cols = route.T
    if not combine:
        return yg, route_cols
    return _combine(x, cx, yg, route_cols, mod3, 5, rows_per_mod, ctx_mod_row)


def _rope_tables(seq):
    rows = seq // GRID_W
    row = jnp.repeat(jnp.arange(rows, dtype=F32), GRID_W)
    col = jnp.tile(jnp.arange(GRID_W, dtype=F32), rows)
    n_freq = HEAD_DIM // 4
    inv_freq = ROPE_BASE ** (-jnp.arange(n_freq, dtype=F32) / n_freq)
    ang = jnp.concatenate([row[:, None] * inv_freq, col[:, None] * inv_freq], axis=-1)
    cos, sin = jnp.cos(ang), jnp.sin(ang)
    return jnp.concatenate([cos, cos], axis=-1), jnp.concatenate([-sin, sin], axis=-1)


def kernel(x, c, ctx, c_ctx, ada_w, ada_b, norm_mix, norm_ffn, ab_w_in, ab_q_gain, ab_k_gain, ab_sink, ab_conv_w, ab_conv_b, ab_gate_a_w, ab_gate_a_b, ab_gate_x_w, ab_gate_x_b, ab_lru_lambda, ab_w_out, gqa_w_in, gqa_q_gain, gqa_k_gain, gqa_w_out, router_w, router_bias, moe_w_gate, moe_w_up, moe_w_down):
    batch, seq, d = x.shape
    ctx_len = ctx.shape[1]
    depth = ada_w.shape[0]
    assert depth == 2 and batch < SUBLANES
    n_lat = batch * seq
    n_ctx = batch * ctx_len
    ctx_row = batch

    xl = x.reshape(n_lat, d)
    xc = ctx.reshape(n_ctx, d)
    cc = jnp.zeros((SUBLANES, d), F32).at[:batch].set(c).at[ctx_row].set(c_ctx)
    mod = _ada(cc, ada_w, ada_b)
    cos2, sin2 = _rope_tables(seq)
    rw_t = router_w.T.astype(BF16)
    rb = router_bias.reshape(N_EXPERTS, 1).astype(F32)
    experts = (moe_w_gate, moe_w_up, moe_w_down)

    mod3 = mod[0].reshape(SUBLANES, 1, 6 * d)
    w_in = ab_w_in[0].astype(BF16)
    proj = _norm_mod_matmul(xl, norm_mix[0], mod3, 0, 1, w_in, seq, 0,
                            A_Q_HEADS, A_KV_HEADS, ab_q_gain[0], ab_k_gain[0], cos2, sin2)
    proj_c = _norm_mod_matmul(xc, norm_mix[0], mod3, 0, 1, w_in, n_ctx, ctx_row,
                              A_Q_HEADS, A_KV_HEADS, ab_q_gain[0], ab_k_gain[0])
    att = _win_attn(ab_sink[0], proj, proj_c, batch, seq, ctx_len)
    att_c = _ctx_attn(ab_sink[0], proj_c, batch, ctx_len)

    lru_w = ab_conv_w.shape[2]
    c0 = (A_Q_HEADS + 2 * A_KV_HEADS) * HEAD_DIM
    y_p, yc_p = _lru(_to_chunked(proj[:, c0:c0 + lru_w], batch),
                     _to_chunked(proj[:, c0 + lru_w:c0 + 2 * lru_w], batch),
                     _to_chunked(proj_c[:, c0:c0 + lru_w], batch),
                     _to_chunked(proj_c[:, c0 + lru_w:c0 + 2 * lru_w], batch),
                     ab_conv_w[0], ab_conv_b[0], ab_gate_a_w[0], ab_gate_a_b[0],
                     ab_gate_x_w[0], ab_gate_x_b[0], ab_lru_lambda[0])
    w_out = ab_w_out[0].astype(BF16)
    xl = _out_proj(att, 0, _from_chunked(y_p), 0, w_out, xl, mod3, 2, seq, 0)
    xc = _out_proj(att_c, 0, _from_chunked(yc_p), 0, w_out, xc, mod3, 2, n_ctx, ctx_row)
    yg, route_cols = _moe(xl, xc, norm_ffn[0], mod3, rw_t, rb, *experts, 0, seq, ctx_row, combine=False)

    mod3_prev, mod3 = mod3, mod[1].reshape(SUBLANES, 1, 6 * d)
    w_in = gqa_w_in[0].astype(BF16)
    cw = C_Q_HEADS * HEAD_DIM
    proj, xl = _norm_mod_matmul(xl, norm_mix[1], mod3, 0, 1, w_in, seq, 0,
                                C_Q_HEADS, C_KV_HEADS, gqa_q_gain[0], gqa_k_gain[0], cos2, sin2,
                                pending_moe=(yg, route_cols, 0, mod3_prev, 5))
    proj_c, _ = _norm_mod_matmul(xc, norm_mix[1], mod3, 0, 1, w_in[:, cw:], n_ctx, ctx_row,
                                 0, C_KV_HEADS, gqa_q_gain[0], gqa_k_gain[0],
                                 pending_moe=(yg, route_cols, n_lat, mod3_prev, 5))
    att = _dense_attn(proj, proj_c, batch, seq, ctx_len)
    xl = _out_proj(att, 0, att, 1, gqa_w_out[0].astype(BF16), xl, mod3, 2, seq, 0)
    xl, _ = _moe(xl, None, norm_ffn[1], mod3, rw_t, rb, *experts, 1, seq, ctx_row)
    return xl.reshape(batch, seq, d)
```

```python
import functools

import jax
import jax.numpy as jnp
import numpy as np
from jax import lax
from jax.experimental import pallas as pl
from jax.experimental.pallas import tpu as pltpu
from jax.experimental.pallas import tpu_sc as plsc

F32 = jnp.float32
BF16 = jnp.bfloat16

LANES = 128
SUBLANES = 8
VMEM_LIMIT = 56 * 1024 * 1024

HEAD_DIM = 128
GRID_W = 64
WINDOW = 128
BLOCK = 128
ROPE_BASE = 10000.0
EPS = 1e-6
ATTN_SCALE = HEAD_DIM ** -0.5
A_Q_HEADS, A_KV_HEADS = 8, 2
C_Q_HEADS, C_KV_HEADS = 16, 4
GQA_GROUP = 4
LRU_C = 8.0
CONV_W = 4
CONV_LEFT = 2
N_EXPERTS = 16
N_GROUPS = 4
EXPERTS_PER_GROUP = 4
NEG_BIG = -1e30

LRU_CHUNKS = SUBLANES
LRU_JB = 16
LRU_LANE_BLOCKS = 2
MOE_TM = 256
MOE_RANGES = 2
SC_CORES = 2
SC_WORKERS = 32
SC_GATHER_WINDOW = 32
SC_GATHER_BUFFERS = 3
SC_GATHER_SUB = 8


def _cparams(sem, vmem=VMEM_LIMIT):
    return pltpu.CompilerParams(dimension_semantics=sem, vmem_limit_bytes=vmem)


def _dot(a, b):
    return jnp.dot(a, b, preferred_element_type=F32)


def _dot_nt(a, b):
    return lax.dot_general(a, b, (((1,), (1,)), ((), ())), preferred_element_type=F32)


def _ada_kernel(c_ref, w_ref, b_ref, o_ref):
    c = c_ref[...]
    s = (c * jax.nn.sigmoid(c)).astype(BF16)
    o_ref[0] = _dot(s, w_ref[0].astype(BF16)) + b_ref[0]


def _ada(cc, ada_w, ada_b):
    depth, d, n = ada_w.shape
    tn = 1024
    return pl.pallas_call(
        _ada_kernel,
        out_shape=jax.ShapeDtypeStruct((depth, SUBLANES, n), F32),
        grid=(depth, n // tn),
        in_specs=[pl.BlockSpec((SUBLANES, d), lambda l, j: (0, 0)),
                  pl.BlockSpec((1, d, tn), lambda l, j: (l, 0, j)),
                  pl.BlockSpec((1, 1, tn), lambda l, j: (l, 0, j))],
        out_specs=pl.BlockSpec((1, SUBLANES, tn), lambda l, j: (l, 0, j)),
        compiler_params=_cparams(("arbitrary", "arbitrary")),
        name="ada",
    )(cc, ada_w, ada_b.reshape(depth, 1, n))


def _norm_mod_rows(x_ref, g_ref, sh_ref, sc_ref, dst_ref, tm, rc=128, load_rows=None):
    g = g_ref[...]
    sc1 = 1.0 + sc_ref[...]
    sh = sh_ref[...]

    def body(r, carry):
        rows = pl.ds(pl.multiple_of(r * rc, rc), rc)
        xf = x_ref[rows, :] if load_rows is None else load_rows(rows)
        ms = jnp.mean(xf * xf, axis=-1, keepdims=True)
        xn = (xf * lax.rsqrt(ms + EPS)) * g
        dst_ref[rows, :] = (xn * sc1 + sh).astype(dst_ref.dtype)
        return carry

    lax.fori_loop(0, tm // rc, body, 0)


def _moe_residual_rows(x_ref, y0_ref, y1_ref, r_ref, gate_ref, rows):
    w0 = r_ref[rows, 2:3]
    w1 = r_ref[rows, 3:4]
    lo0, hi0 = _unpack_halves(y0_ref[rows, :])
    lo1, hi1 = _unpack_halves(y1_ref[rows, :])
    f = jnp.concatenate([w0 * lo0 + w1 * lo1, w0 * hi0 + w1 * hi1], axis=1)
    return x_ref[rows, :] + gate_ref[...] * f


def _nm_mm_kernel(*refs, tm, tn, n_q, n_k, rope, pending_moe):
    refs = list(refs)
    x_ref = refs.pop(0)
    if pending_moe:
        y0_ref, y1_ref, r_ref, gate_ref = (refs.pop(0) for _ in range(4))
    g_ref, sh_ref, sc_ref, w_ref, qg_ref, kg_ref = (refs.pop(0) for _ in range(6))
    if rope:
        cos_ref, sin_ref = refs.pop(0), refs.pop(0)
    o_ref = refs.pop(0)
    if pending_moe:
        xnew_ref = refs.pop(0)

        def load_rows(rows):
            xf = _moe_residual_rows(x_ref, y0_ref, y1_ref, r_ref, gate_ref, rows)
            xnew_ref[rows, :] = xf
            return xf
    else:
        load_rows = None
    hn_ref = refs.pop(0)
    _norm_mod_rows(x_ref, g_ref, sh_ref, sc_ref, hn_ref, tm, load_rows=load_rows)
    h = hn_ref[...]
    heads_per_chunk = tn // HEAD_DIM
    for j in range(w_ref.shape[1] // tn):
        cols = slice(j * tn, (j + 1) * tn)
        y = _dot(h, w_ref[:, cols])
        parts = []
        for hh in range(heads_per_chunk):
            head = j * heads_per_chunk + hh
            yh = y[:, _head_cols(hh)]
            if head < n_q + n_k:
                gain = qg_ref[...] if head < n_q else kg_ref[...]
                ms = jnp.mean(yh * yh, axis=-1, keepdims=True)
                yh = (yh * lax.rsqrt(ms + EPS)) * gain
                if rope:
                    yh = yh * cos_ref[...] + pltpu.roll(yh, HEAD_DIM // 2, 1) * sin_ref[...]
                if head < n_q:
                    yh = yh * ATTN_SCALE
            parts.append(yh.astype(o_ref.dtype))
        o_ref[:, cols] = jnp.concatenate(parts, axis=1)


def _norm_mod_matmul(x, g, mod3, shift_blk, scale_blk, w, rows_per_mod, mod_base,
                     n_q, n_k, q_gain, k_gain, cos2=None, sin2=None, pending_moe=None, tm=512, tn=512):
    n, d = x.shape
    nout = w.shape[1]
    tm = min(tm, n)
    tn = min(tn, nout)
    per = rows_per_mod // tm
    rope = cos2 is not None

    def mod_row(i):
        return mod_base + i // per

    head_vec = pl.BlockSpec((1, HEAD_DIM), lambda i: (0, 0))
    in_specs = [pl.BlockSpec((tm, d), lambda i: (i, 0))]
    args = [x]
    out_shape = [jax.ShapeDtypeStruct((n, nout), BF16)]
    out_specs = [pl.BlockSpec((tm, nout), lambda i: (i, 0))]
    if pending_moe is not None:
        yg, route_cols, first_row, mod3_prev, gate_blk = pending_moe
        t0 = first_row // tm
        slot1 = yg.shape[0] // 2 // tm
        in_specs += [pl.BlockSpec((tm, d // 2), lambda i: (t0 + i, 0)),
                     pl.BlockSpec((tm, d // 2), lambda i: (slot1 + t0 + i, 0)),
                     pl.BlockSpec((tm, SUBLANES), lambda i: (t0 + i, 0)),
                     pl.BlockSpec((None, 1, d), lambda i: (mod_row(i), 0, gate_blk))]
        args += [yg, yg, route_cols, mod3_prev]
        out_shape.append(jax.ShapeDtypeStruct((n, d), F32))
        out_specs.append(pl.BlockSpec((tm, d), lambda i: (i, 0)))
    in_specs += [pl.BlockSpec((1, d), lambda i: (0, 0)),
                 pl.BlockSpec((None, 1, d), lambda i: (mod_row(i), 0, shift_blk)),
                 pl.BlockSpec((None, 1, d), lambda i: (mod_row(i), 0, scale_blk)),
                 pl.BlockSpec((d, nout), lambda i: (0, 0), pipeline_mode=pl.Buffered(1)),
                 head_vec, head_vec]
    args += [g.reshape(1, d), mod3, mod3, w, q_gain.reshape(1, HEAD_DIM), k_gain.reshape(1, HEAD_DIM)]
    if rope:
        tiles_per_seq = cos2.shape[0] // tm
        table = pl.BlockSpec((tm, HEAD_DIM), lambda i: (i % tiles_per_seq, 0))
        in_specs += [table, table]
        args += [cos2, sin2]
    out = pl.pallas_call(
        functools.partial(_nm_mm_kernel, tm=tm, tn=tn, n_q=n_q, n_k=n_k, rope=rope,
                          pending_moe=pending_moe is not None),
        out_shape=tuple(out_shape),
        grid=(n // tm,),
        in_specs=in_specs,
        out_specs=tuple(out_specs),
        scratch_shapes=[pltpu.VMEM((tm, d), BF16)],
        compiler_params=_cparams(("parallel",)),
        name="norm_mod_matmul",
    )(*args)
    return out if pending_moe is not None else out[0]


def _head_cols(h):
    return slice(h * HEAD_DIM, (h + 1) * HEAD_DIM)


def _stack_group(q_ref, kvh):
    return jnp.concatenate([q_ref[:, _head_cols(kvh * GQA_GROUP + g)] for g in range(GQA_GROUP)], axis=0)


def _sink_col(sink_ref, kvh, rows):
    return jnp.concatenate([jnp.full((rows, 1), sink_ref[kvh * GQA_GROUP + g], F32)
                            for g in range(GQA_GROUP)], axis=0)


def _band_bias(ctx_len):
    rows, nk = GQA_GROUP * BLOCK, 3 * BLOCK + ctx_len
    qi = np.arange(rows)[:, None] % BLOCK
    kj = np.arange(nk)[None, :]
    inner = (kj >= 3 * BLOCK) | (np.abs(kj - BLOCK - qi) <= WINDOW)
    first = inner & ~(kj < BLOCK)
    last = inner & ~((kj >= 2 * BLOCK) & (kj < 3 * BLOCK))
    return np.where(np.stack([first, inner, last]), 0.0, NEG_BIG).astype(np.float32)


def _win_attn_kernel(sink_ref, bias_ref, q_ref, kp_ref, kc_ref, kn_ref, vp_ref, vc_ref, vn_ref,
                     kx_ref, vx_ref, o_ref, *, ctx_len):
    nk = 3 * BLOCK + ctx_len
    bias = bias_ref[...]
    ones = jnp.ones((nk, HEAD_DIM), BF16)
    for kvh in range(A_KV_HEADS):
        cols = _head_cols(kvh)
        q4 = _stack_group(q_ref, kvh)
        ka = jnp.concatenate([kp_ref[:, cols], kc_ref[:, cols], kn_ref[:, cols], kx_ref[:, cols]], axis=0)
        va = jnp.concatenate([vp_ref[:, cols], vc_ref[:, cols], vn_ref[:, cols], vx_ref[:, cols]], axis=0)
        s = _dot_nt(q4, ka) + bias
        sk = _sink_col(sink_ref, kvh, BLOCK)
        m = jnp.maximum(jnp.max(s, axis=-1, keepdims=True), sk)
        p = jnp.exp((s - m).astype(BF16))
        acc = _dot(p, jnp.concatenate([va, ones], axis=1))
        o = acc[:, 0:HEAD_DIM] / (acc[:, HEAD_DIM:] + jnp.exp(sk - m))
        for g in range(GQA_GROUP):
            o_ref[:, _head_cols(kvh * GQA_GROUP + g)] = o[g * BLOCK:(g + 1) * BLOCK].astype(o_ref.dtype)


def _win_attn(sink, proj, proj_c, batch, seq, ctx_len):
    nb = seq // BLOCK
    assert nb >= 2
    kvw = A_KV_HEADS * HEAD_DIM
    k_blk = A_Q_HEADS * HEAD_DIM // kvw
    v_blk = k_blk + 1
    bias = jnp.asarray(_band_bias(ctx_len))

    def which_bias(b, n):
        return (jnp.where(n == 0, 0, jnp.where(n == nb - 1, 2, 1)), 0, 0)

    def prev(b, n):
        return b * nb + jnp.maximum(n - 1, 0)

    def cur(b, n):
        return b * nb + n

    def nxt(b, n):
        return b * nb + jnp.minimum(n + 1, nb - 1)

    return pl.pallas_call(
        functools.partial(_win_attn_kernel, ctx_len=ctx_len),
        out_shape=jax.ShapeDtypeStruct((batch * seq, A_Q_HEADS * HEAD_DIM), BF16),
        grid=(batch, nb),
        in_specs=[pl.BlockSpec(memory_space=pltpu.SMEM),
                  pl.BlockSpec((None,) + bias.shape[1:], which_bias),
                  pl.BlockSpec((BLOCK, A_Q_HEADS * HEAD_DIM), lambda b, n: (cur(b, n), 0)),
                  pl.BlockSpec((BLOCK, kvw), lambda b, n: (prev(b, n), k_blk)),
                  pl.BlockSpec((BLOCK, kvw), lambda b, n: (cur(b, n), k_blk)),
                  pl.BlockSpec((BLOCK, kvw), lambda b, n: (nxt(b, n), k_blk)),
                  pl.BlockSpec((BLOCK, kvw), lambda b, n: (prev(b, n), v_blk)),
                  pl.BlockSpec((BLOCK, kvw), lambda b, n: (cur(b, n), v_blk)),
                  pl.BlockSpec((BLOCK, kvw), lambda b, n: (nxt(b, n), v_blk)),
                  pl.BlockSpec((ctx_len, kvw), lambda b, n: (b, k_blk)),
                  pl.BlockSpec((ctx_len, kvw), lambda b, n: (b, v_blk))],
        out_specs=pl.BlockSpec((BLOCK, A_Q_HEADS * HEAD_DIM), lambda b, n: (cur(b, n), 0)),
        compiler_params=_cparams(("parallel", "parallel")),
        name="win_attn",
    )(sink, bias, proj, proj, proj, proj, proj, proj, proj, proj_c, proj_c)


def _ctx_attn_kernel(sink_ref, q_ref, k_ref, v_ref, o_ref, *, ctx_len):
    kvh = pl.program_id(1)
    q4 = jnp.concatenate([q_ref[:, _head_cols(g)] for g in range(GQA_GROUP)], axis=0)
    s = _dot_nt(q4, k_ref[...])
    sk = jnp.concatenate([jnp.full((ctx_len, 1), sink_ref[kvh * GQA_GROUP + g], F32)
                          for g in range(GQA_GROUP)], axis=0)
    m = jnp.maximum(jnp.max(s, axis=-1, keepdims=True), sk)
    p = jnp.exp(s - m)
    den = jnp.sum(p, axis=-1, keepdims=True) + jnp.exp(sk - m)
    o = _dot(p.astype(BF16), v_ref[...]) / den
    for g in range(GQA_GROUP):
        o_ref[:, _head_cols(g)] = o[g * ctx_len:(g + 1) * ctx_len].astype(o_ref.dtype)


def _ctx_attn(sink, proj_c, batch, ctx_len):
    k_blk = A_Q_HEADS
    v_blk = A_Q_HEADS + A_KV_HEADS
    gw = GQA_GROUP * HEAD_DIM
    return pl.pallas_call(
        functools.partial(_ctx_attn_kernel, ctx_len=ctx_len),
        out_shape=jax.ShapeDtypeStruct((batch * ctx_len, A_Q_HEADS * HEAD_DIM), BF16),
        grid=(batch, A_KV_HEADS),
        in_specs=[pl.BlockSpec(memory_space=pltpu.SMEM),
                  pl.BlockSpec((ctx_len, gw), lambda b, h: (b, h)),
                  pl.BlockSpec((ctx_len, HEAD_DIM), lambda b, h: (b, k_blk + h)),
                  pl.BlockSpec((ctx_len, HEAD_DIM), lambda b, h: (b, v_blk + h))],
        out_specs=pl.BlockSpec((ctx_len, gw), lambda b, h: (b, h)),
        compiler_params=_cparams(("parallel", "parallel")),
        name="ctx_attn",
    )(sink, proj_c, proj_c, proj_c)


def _sigmoid(x):
    return 0.5 * jnp.tanh(0.5 * x) + 0.5


def _gelu_tanh(x):
    return 0.5 * x * (1.0 + jnp.tanh(0.7978845608028654 * (x + 0.044715 * (x * x * x))))


def _lru_sequence(x_ref, xg_ref, y_ref, xp_ref, a_ref, b_ref, w, init, rows, jb):
    conv_w, conv_b, wa, ba, wx, bx, c_logsig = w
    width = x_ref.shape[-1]
    sub = lax.broadcasted_iota(jnp.int32, (1, SUBLANES, width), 1)

    def block_diag(ub, wd):
        return jnp.concatenate([_dot(ub[:, _head_cols(n)], wd[n]) for n in range(width // LANES)], axis=1)

    def fill(r, carry):
        rr = pl.ds(pl.multiple_of(r * jb, jb), jb)
        xp_ref[pl.ds(pl.multiple_of(r * jb, jb) + CONV_LEFT, jb)] = x_ref[rr].astype(F32)
        return carry

    lax.fori_loop(0, rows // jb, fill, 0)
    tail = x_ref[rows - CONV_LEFT:rows].astype(F32)
    xp_ref[0:CONV_LEFT] = jnp.where(sub == 0, 0.0, pltpu.roll(tail, 1, 1))
    head = x_ref[0:1].astype(F32)
    xp_ref[rows + CONV_LEFT:rows + CONV_LEFT + 1] = jnp.where(
        sub == SUBLANES - 1, 0.0, pltpu.roll(head, SUBLANES - 1, 1))

    def gates(r, carry):
        j0 = pl.multiple_of(r * jb, jb)
        u = conv_b
        for k in range(CONV_W):
            u = u + conv_w[k] * xp_ref[pl.ds(j0 + k, jb)]
        u2 = u.reshape(jb * SUBLANES, width)
        ub = u2.astype(BF16)
        for d in range(2):
            r_gate = _sigmoid(block_diag(ub, wa[d]) + ba[d])
            i_gate = _sigmoid(block_diag(ub, wx[d]) + bx[d])
            log_a = c_logsig[d] * r_gate
            a = jnp.exp(log_a)
            v = 1.0 - a * a
            root = jnp.where(v > 0.0, v * lax.rsqrt(v), 0.0)
            b = root * (i_gate * u2)
            a_ref[d, pl.ds(j0, jb)] = a.reshape(jb, SUBLANES, width)
            b_ref[d, pl.ds(j0, jb)] = b.reshape(jb, SUBLANES, width)
        return carry

    lax.fori_loop(0, rows // jb, gates, 0, unroll=2)

    def scan(j, carry):
        hf, pf, hb, pb = carry
        jr = rows - 1 - j
        af = a_ref[0, j]
        hf = af * hf + b_ref[0, j]
        pf = pf * af
        b_ref[0, j] = hf
        a_ref[0, j] = pf
        ab = a_ref[1, jr]
        hb = ab * hb + b_ref[1, jr]
        pb = pb * ab
        b_ref[1, jr] = hb
        a_ref[1, jr] = pb
        return hf, pf, hb, pb

    z = jnp.zeros((SUBLANES, width), F32)
    o = jnp.ones((SUBLANES, width), F32)
    lax.fori_loop(0, rows, scan, (z, o, z, o), unroll=8)

    hf_last, pf_last = b_ref[0, rows - 1], a_ref[0, rows - 1]
    hb_last, pb_last = b_ref[1, 0], a_ref[1, 0]
    s = init[0]
    carry_f = []
    for c in range(SUBLANES):
        carry_f.append(s)
        s = hf_last[c:c + 1] + pf_last[c:c + 1] * s
    out_f = s
    s = init[1]
    carry_b = [None] * SUBLANES
    for c in reversed(range(SUBLANES)):
        carry_b[c] = s
        s = hb_last[c:c + 1] + pb_last[c:c + 1] * s
    out_b = s
    cf = jnp.concatenate(carry_f, axis=0)
    cb = jnp.concatenate(carry_b, axis=0)

    def emit(r, carry):
        rr = pl.ds(pl.multiple_of(r * jb, jb), jb)
        h = (b_ref[0, rr] + a_ref[0, rr] * cf) + (b_ref[1, rr] + a_ref[1, rr] * cb)
        y_ref[rr] = (h * _gelu_tanh(xg_ref[rr].astype(F32))).astype(y_ref.dtype)
        return carry

    lax.fori_loop(0, rows // jb, emit, 0)
    return out_f, out_b


def _lru_kernel(xr_ref, xg_ref, xrc_ref, xgc_ref, cw_ref, cb_ref, wa_ref, ba_ref, wx_ref, bx_ref,
                lam_ref, y_ref, yc_ref, xp_ref, a_ref, b_ref, *, rows, rows_c):
    c_logsig = [LRU_C * jax.nn.log_sigmoid(lam_ref[d]) for d in range(2)]
    w = ([cw_ref[k] for k in range(CONV_W)], cb_ref[0],
         [wa_ref[d] for d in range(2)], [ba_ref[d] for d in range(2)],
         [wx_ref[d] for d in range(2)], [bx_ref[d] for d in range(2)], c_logsig)
    zero = jnp.zeros((1, xr_ref.shape[-1]), F32)
    sf, sb = _lru_sequence(xrc_ref, xgc_ref, yc_ref, xp_ref, a_ref, b_ref, w, (zero, zero),
                           rows_c, min(LRU_JB, rows_c))
    _lru_sequence(xr_ref, xg_ref, y_ref, xp_ref, a_ref, b_ref, w, (sf, sb), rows, LRU_JB)


def _lru(xr, xg, xrc, xgc, conv_w, conv_b, wa, ba, wx, bx, lam):
    batch, rows, _, width = xr.shape
    rows_c = xrc.shape[1]
    cw = LRU_LANE_BLOCKS * LANES
    seq_spec = pl.BlockSpec((None, rows, SUBLANES, cw), lambda b, n: (b, 0, 0, n))
    ctx_spec = pl.BlockSpec((None, rows_c, SUBLANES, cw), lambda b, n: (b, 0, 0, n))
    vec2 = pl.BlockSpec((2, 1, cw), lambda b, n: (0, 0, n))
    mat2 = pl.BlockSpec((2, LRU_LANE_BLOCKS, LANES, LANES), lambda b, n: (0, n, 0, 0))
    return pl.pallas_call(
        functools.partial(_lru_kernel, rows=rows, rows_c=rows_c),
        out_shape=(jax.ShapeDtypeStruct(xr.shape, BF16), jax.ShapeDtypeStruct(xrc.shape, BF16)),
        grid=(batch, width // cw),
        in_specs=[seq_spec, seq_spec, ctx_spec, ctx_spec,
                  pl.BlockSpec((CONV_W, 1, cw), lambda b, n: (0, 0, n)),
                  pl.BlockSpec((1, 1, cw), lambda b, n: (0, 0, n)),
                  mat2, vec2, mat2, vec2, vec2],
        out_specs=(seq_spec, ctx_spec),
        scratch_shapes=[pltpu.VMEM((rows + CONV_W - 1, SUBLANES, cw), F32),
                        pltpu.VMEM((2, rows, SUBLANES, cw), F32),
                        pltpu.VMEM((2, rows, SUBLANES, cw), F32)],
        compiler_params=_cparams(("parallel", "parallel")),
        name="rglru",
    )(xr, xg, xrc, xgc, conv_w.reshape(CONV_W, 1, width), conv_b.reshape(1, 1, width),
      wa.astype(BF16), ba.reshape(2, 1, width), wx.astype(BF16), bx.reshape(2, 1, width),
      lam.reshape(2, 1, width))


def _to_chunked(a, batch):
    t = a.shape[0] // batch
    return a.reshape(batch, LRU_CHUNKS, t // LRU_CHUNKS, a.shape[1]).transpose(0, 2, 1, 3)


def _from_chunked(a):
    b, r, c, w = a.shape
    return a.transpose(0, 2, 1, 3).reshape(b * r * c, w)


def _out_proj_kernel(a1_ref, a2_ref, w1_ref, w2_ref, x_ref, g_ref, o_ref):
    y = _dot(a1_ref[...], w1_ref[...]) + _dot(a2_ref[...], w2_ref[...])
    o_ref[...] = x_ref[...] + g_ref[...] * y


def _out_proj(a1, a1_blk, a2, a2_blk, w, x, mod3, gate_blk, rows_per_mod, mod_base, tm=2048, tn=512):
    n, d = x.shape
    kh = w.shape[0] // 2
    tm = min(tm, n)
    per = rows_per_mod // tm
    gpb = d // tn
    return pl.pallas_call(
        _out_proj_kernel,
        out_shape=jax.ShapeDtypeStruct((n, d), F32),
        grid=(n // tm, d // tn),
        in_specs=[pl.BlockSpec((tm, kh), lambda i, j: (i, a1_blk)),
                  pl.BlockSpec((tm, kh), lambda i, j: (i, a2_blk)),
                  pl.BlockSpec((kh, tn), lambda i, j: (0, j)),
                  pl.BlockSpec((kh, tn), lambda i, j: (1, j)),
                  pl.BlockSpec((tm, tn), lambda i, j: (i, j)),
                  pl.BlockSpec((None, 1, tn), lambda i, j: (mod_base + i // per, 0, gate_blk * gpb + j))],
        out_specs=pl.BlockSpec((tm, tn), lambda i, j: (i, j)),
        compiler_params=_cparams(("parallel", "parallel")),
        name="out_proj",
    )(a1, a2, w, w, x, mod3)


def _lane_block_max(s):
    mm = s[:, 0:LANES]
    for t in range(1, s.shape[1] // LANES):
        mm = jnp.maximum(mm, s[:, t * LANES:(t + 1) * LANES])
    return mm


def _exp_blocks(s, mrep):
    return jnp.concatenate(
        [jnp.exp((s[:, t * LANES:(t + 1) * LANES] - mrep).astype(BF16)) for t in range(s.shape[1] // LANES)],
        axis=1)


def _copy_key_rows(dst_ref, c, kc, lat_ref, ctx_ref, cols):
    seq = lat_ref.shape[0]
    lo, hi = c * kc, (c + 1) * kc
    if lo < seq:
        n = min(hi, seq) - lo
        dst_ref[c, 0:n, cols] = lat_ref[lo:lo + n, :]
    if hi > seq:
        start = max(lo, seq)
        dst_ref[c, start - lo:kc, cols] = ctx_ref[start - seq:hi - seq, :]


def _dense_attn_kernel(q_ref, k_ref, v_ref, kx_ref, vx_ref, o_ref,
                       s_ref, m_ref, acc_ref, ka_ref, va_ref, *, tq, n_chunks, kc):
    @pl.when(pl.program_id(2) == 0)
    def _():
        for c in range(n_chunks):
            _copy_key_rows(ka_ref, c, kc, k_ref, kx_ref, slice(0, HEAD_DIM))
            _copy_key_rows(va_ref, c, kc, v_ref, vx_ref, slice(0, HEAD_DIM))
            va_ref[c, :, HEAD_DIM:] = jnp.ones((kc, HEAD_DIM), BF16)

    q4 = jnp.concatenate([q_ref[:, _head_cols(g)] for g in range(GQA_GROUP)], axis=0)
    m_ref[...] = jnp.full(m_ref.shape, NEG_BIG, F32)

    def sweep1(c, carry):
        s = _dot_nt(q4, ka_ref[c])
        s_ref[c] = s
        m_ref[...] = jnp.maximum(m_ref[...], _lane_block_max(s))
        return carry

    lax.fori_loop(0, n_chunks, sweep1, 0)
    m_ref[...] = jnp.broadcast_to(jnp.max(m_ref[...], axis=-1, keepdims=True), m_ref.shape)
    acc_ref[...] = jnp.zeros(acc_ref.shape, F32)

    def sweep2(c, carry):
        acc_ref[...] += _dot(_exp_blocks(s_ref[c], m_ref[...]), va_ref[c])
        return carry

    lax.fori_loop(0, n_chunks, sweep2, 0)
    o = acc_ref[:, 0:HEAD_DIM] / acc_ref[:, HEAD_DIM:]
    for g in range(GQA_GROUP):
        o_ref[:, _head_cols(g)] = o[g * tq:(g + 1) * tq].astype(o_ref.dtype)


def _dense_attn(proj, proj_c, batch, seq, ctx_len, tq=256, n_chunks=2):
    gw = GQA_GROUP * HEAD_DIM
    nq = seq // tq
    rows = GQA_GROUP * tq
    kc = (seq + ctx_len) // n_chunks
    assert kc * n_chunks == seq + ctx_len and kc % LANES == 0
    k_blk = C_Q_HEADS
    v_blk = C_Q_HEADS + C_KV_HEADS
    vx_blk = C_KV_HEADS
    return pl.pallas_call(
        functools.partial(_dense_attn_kernel, tq=tq, n_chunks=n_chunks, kc=kc),
        out_shape=jax.ShapeDtypeStruct((batch * seq, C_Q_HEADS * HEAD_DIM), BF16),
        grid=(batch, C_KV_HEADS, nq),
        in_specs=[pl.BlockSpec((tq, gw), lambda b, h, i: (b * nq + i, h)),
                  pl.BlockSpec((seq, HEAD_DIM), lambda b, h, i: (b, k_blk + h)),
                  pl.BlockSpec((seq, HEAD_DIM), lambda b, h, i: (b, v_blk + h)),
                  pl.BlockSpec((ctx_len, HEAD_DIM), lambda b, h, i: (b, h)),
                  pl.BlockSpec((ctx_len, HEAD_DIM), lambda b, h, i: (b, vx_blk + h))],
        out_specs=pl.BlockSpec((tq, gw), lambda b, h, i: (b * nq + i, h)),
        scratch_shapes=[pltpu.VMEM((n_chunks, rows, kc), F32),
                        pltpu.VMEM((rows, LANES), F32),
                        pltpu.VMEM((rows, 2 * HEAD_DIM), F32),
                        pltpu.VMEM((n_chunks, kc, HEAD_DIM), BF16),
                        pltpu.VMEM((n_chunks, kc, 2 * HEAD_DIM), BF16)],
        compiler_params=_cparams(("parallel", "parallel", "arbitrary")),
        name="dense_attn",
    )(proj, proj, proj, proj_c, proj_c)


def _router_rows(biased, scores):
    v = [biased[e:e + 1, :] for e in range(N_EXPERTS)]
    s = [scores[e:e + 1, :] for e in range(N_EXPERTS)]

    def top2_sum(vals):
        best = vals[0] + vals[1]
        for i in range(len(vals)):
            for j in range(i + 1, len(vals)):
                if (i, j) != (0, 1):
                    best = jnp.maximum(best, vals[i] + vals[j])
        return best

    gsum = [top2_sum(v[g * EXPERTS_PER_GROUP:(g + 1) * EXPERTS_PER_GROUP]) for g in range(N_GROUPS)]
    sel = jnp.zeros_like(gsum[0], dtype=jnp.int32)
    best = gsum[0]
    for g in range(1, N_GROUPS):
        take = gsum[g] > best
        sel = jnp.where(take, g, sel)
        best = jnp.where(take, gsum[g], best)

    def pick_group(rows, i):
        out = rows[i]
        for g in range(1, N_GROUPS):
            out = jnp.where(sel == g, rows[g * EXPERTS_PER_GROUP + i], out)
        return out

    cand = [pick_group(v, i) for i in range(EXPERTS_PER_GROUP)]
    cand_s = [pick_group(s, i) for i in range(EXPERTS_PER_GROUP)]
    i1 = jnp.zeros_like(sel)
    b1 = cand[0]
    for i in range(1, EXPERTS_PER_GROUP):
        take = cand[i] > b1
        i1 = jnp.where(take, i, i1)
        b1 = jnp.where(take, cand[i], b1)
    i2 = jnp.full_like(sel, -1)
    b2 = jnp.full_like(b1, -jnp.inf)
    for i in range(EXPERTS_PER_GROUP):
        take = (i1 != i) & ((cand[i] > b2) | (i2 < 0))
        i2 = jnp.where(take, i, i2)
        b2 = jnp.where(take, cand[i], b2)

    def pick_idx(rows, idx):
        out = rows[0]
        for i in range(1, EXPERTS_PER_GROUP):
            out = jnp.where(idx == i, rows[i], out)
        return out

    s0 = pick_idx(cand_s, i1)
    s1 = pick_idx(cand_s, i2)
    tot = s0 + s1
    e0 = (sel * EXPERTS_PER_GROUP + i1).astype(F32)
    e1 = (sel * EXPERTS_PER_GROUP + i2).astype(F32)
    return e0, e1, s0 / tot, s1 / tot


def _norm_router_kernel(*refs, tm, n_lat_tiles, has_ctx):
    i = pl.program_id(0)
    if has_ctx:
        x_ref, xc_ref, g_ref, sh_ref, sc_ref, rw_ref, rb_ref, hp_ref, r_ref, h_ref, tri_ref, run_ref = refs

        @pl.when(i < n_lat_tiles)
        def _():
            _norm_mod_rows(x_ref, g_ref, sh_ref, sc_ref, h_ref, tm)

        @pl.when(i >= n_lat_tiles)
        def _():
            _norm_mod_rows(xc_ref, g_ref, sh_ref, sc_ref, h_ref, tm)
    else:
        x_ref, g_ref, sh_ref, sc_ref, rw_ref, rb_ref, hp_ref, r_ref, h_ref, tri_ref, run_ref = refs
        _norm_mod_rows(x_ref, g_ref, sh_ref, sc_ref, h_ref, tm)

    @pl.when(i == 0)
    def _():
        run_ref[...] = jnp.zeros(run_ref.shape, F32)
        before = lax.broadcasted_iota(jnp.int32, (tm, tm), 0) <= lax.broadcasted_iota(jnp.int32, (tm, tm), 1)
        tri_ref[...] = jnp.where(before, 1.0, 0.0).astype(BF16)

    def pack(r, carry):
        rows = pl.ds(pl.multiple_of(r * 128, 128), 128)
        hp_ref[rows, :] = _pack_halves(h_ref[rows, :])
        return carry

    lax.fori_loop(0, tm // 128, pack, 0)
    logits = _dot_nt(rw_ref[...], h_ref[...])
    scores = jax.nn.sigmoid(logits)
    e0, e1, w0, w1 = _router_rows(scores + rb_ref[...], scores)
    expert_ids = lax.broadcasted_iota(jnp.int32, (N_EXPERTS, tm), 0).astype(F32)
    ranks = []
    for e_row in (e0, e1):
        hit = expert_ids == e_row
        seen = _dot(jnp.where(hit, 1.0, 0.0).astype(BF16), tri_ref[...])
        ranks.append(jnp.sum(jnp.where(hit, seen - 1.0 + run_ref[...], 0.0), axis=0, keepdims=True))
        run_ref[...] = run_ref[...] + seen[:, tm - 1:tm]
    zero = jnp.zeros_like(w0)
    r_ref[...] = jnp.concatenate([e0, e1, w0, w1, ranks[0], ranks[1], zero, zero], axis=0)


def _pack_halves(v):
    c = v.shape[1] // 2
    lo = lax.bitcast_convert_type(v[:, :c].astype(BF16).astype(F32), jnp.uint32)
    hi = lax.bitcast_convert_type(v[:, c:].astype(BF16).astype(F32), jnp.uint32)
    return (lo >> 16) | (hi & jnp.uint32(0xFFFF0000))


def _unpack_halves(p):
    lo = lax.bitcast_convert_type(p << 16, F32)
    hi = lax.bitcast_convert_type(p & jnp.uint32(0xFFFF0000), F32)
    return lo, hi


def _sc_gather_rows(table, idx):
    n, w = idx.shape[0], table.shape[1]
    win, nb, sub = SC_GATHER_WINDOW, SC_GATHER_BUFFERS, SC_GATHER_SUB
    per = n // (win * SC_WORKERS)
    assert per * win * SC_WORKERS == n and per >= 1
    mesh = plsc.VectorSubcoreMesh(core_axis_name="core", subcore_axis_name="subcore")

    @functools.partial(
        pl.kernel, out_type=jax.ShapeDtypeStruct((n, w), table.dtype), mesh=mesh, name="sc_gather_rows",
        scratch_types=([pltpu.VMEM((per * win,), jnp.int32)] + [pltpu.VMEM((win, w), table.dtype)] * nb
                       + [pltpu.SemaphoreType.DMA] * (2 * nb)))
    def gather(x_hbm, i_hbm, o_hbm, i_v, *rest):
        bufs, gsems, wsems = rest[:nb], rest[nb:2 * nb], rest[2 * nb:]
        wid = lax.axis_index("subcore") * SC_CORES + lax.axis_index("core")
        base = wid * (per * win)
        pltpu.sync_copy(i_hbm.at[pl.ds(base, per * win)], i_v)
        rot = (wid * per) // SC_WORKERS

        def row0(t):
            u = t + rot
            u = u - per * (u >= per).astype(jnp.int32)
            return pl.multiple_of(u * win, win)

        def gather_copy(t, b, s):
            return pltpu.make_async_copy(x_hbm.at[i_v.at[pl.ds(row0(t) + s * sub, sub)]],
                                         bufs[b].at[pl.ds(s * sub, sub)], gsems[b])

        def write_copy(t, b):
            return pltpu.make_async_copy(bufs[b], o_hbm.at[pl.ds(base + row0(t), win)], wsems[b])

        def start_gathers(t, b):
            for s in range(win // sub):
                gather_copy(t, b, s).start()

        def step(t, b):
            for s in range(win // sub):
                gather_copy(t, b, s).wait()
            write_copy(t, b).start()

            @pl.when(t >= 1)
            def _():
                write_copy(t - 1, (b - 1) % nb).wait()

            @pl.when(t + nb - 1 < per)
            def _():
                start_gathers(t + nb - 1, (b + nb - 1) % nb)

        for t in range(min(nb - 1, per)):
            start_gathers(t, t)

        @pl.loop(0, per // nb)
        def _(p):
            for j in range(nb):
                step(nb * p + j, j)

        for t in range(per - per % nb, per):
            step(t, t % nb)
        write_copy(per - 1, (per - 1) % nb).wait()

    return gather(table, idx)


def _lat_ctx_maps(n_lat, per, ctx_mod_row):
    def mod_row(i):
        return jnp.where(i < n_lat, i // per, ctx_mod_row)

    def lat(i):
        return (jnp.minimum(i, n_lat - 1), 0)

    def ctx(i):
        return (jnp.maximum(i - n_lat, 0), 0)

    return mod_row, lat, ctx


def _norm_router(x, cx, g, mod3, shift_blk, scale_blk, rw_t, rb, rows_per_mod, ctx_mod_row, tm=512):
    n, d = x.shape
    n_lat = n // tm
    has_ctx = cx is not None
    ntot = n_lat + (cx.shape[0] // tm if has_ctx else 0)
    mod_row, lat, ctx = _lat_ctx_maps(n_lat, rows_per_mod // tm, ctx_mod_row)
    row_specs = [pl.BlockSpec((tm, d), lat)] + ([pl.BlockSpec((tm, d), ctx)] if has_ctx else [])
    row_args = [x] + ([cx] if has_ctx else [])
    return pl.pallas_call(
        functools.partial(_norm_router_kernel, tm=tm, n_lat_tiles=n_lat, has_ctx=has_ctx),
        out_shape=(jax.ShapeDtypeStruct((ntot * tm, d // 2), jnp.uint32),
                   jax.ShapeDtypeStruct((SUBLANES, ntot * tm), F32)),
        grid=(ntot,),
        in_specs=row_specs + [
            pl.BlockSpec((1, d), lambda i: (0, 0)),
            pl.BlockSpec((None, 1, d), lambda i: (mod_row(i), 0, shift_blk)),
            pl.BlockSpec((None, 1, d), lambda i: (mod_row(i), 0, scale_blk)),
            pl.BlockSpec((N_EXPERTS, d), lambda i: (0, 0)),
            pl.BlockSpec((N_EXPERTS, 1), lambda i: (0, 0))],
        out_specs=(pl.BlockSpec((tm, d // 2), lambda i: (i, 0)),
                   pl.BlockSpec((SUBLANES, tm), lambda i: (0, i))),
        scratch_shapes=[pltpu.VMEM((tm, d), BF16), pltpu.VMEM((tm, tm), BF16), pltpu.VMEM((N_EXPERTS, 1), F32)],
        compiler_params=_cparams(("arbitrary",)),
        name="norm_router",
    )(*row_args, g.reshape(1, d), mod3, mod3, rw_t, rb)


def _cast_rows(src_ref, dst_ref, rb=256):
    def body(r, carry):
        rows = pl.ds(pl.multiple_of(r * rb, rb), rb)
        dst_ref[rows, :] = src_ref[0, rows, :].astype(dst_ref.dtype)
        return carry

    lax.fori_loop(0, dst_ref.shape[0] // rb, body, 0)


def _tile_state(te_ref, tv_ref, tile0):
    i = pl.program_id(0)
    t = tile0 + i
    live = tv_ref[t] > 0
    new_expert = (i == 0) | (te_ref[t] != te_ref[jnp.maximum(t - 1, 0)])
    return live, new_expert


def _gmm_kernel(te_ref, tv_ref, xs_ref, wg_ref, wu_ref, wd_ref, *rest, tile0):
    o_ref, wg_b, wu_b, wd_b = rest[-4:]
    live, new_expert = _tile_state(te_ref, tv_ref, tile0)

    @pl.when(live & new_expert)
    def _():
        _cast_rows(wg_ref, wg_b)
        _cast_rows(wu_ref, wu_b)
        _cast_rows(wd_ref, wd_b)

    @pl.when(live)
    def _():
        lo, hi = _unpack_halves(xs_ref[...])
        x = jnp.concatenate([lo.astype(BF16), hi.astype(BF16)], axis=1)
        gate = _dot(x, wg_b[...])
        up = _dot(x, wu_b[...])
        h1 = ((gate * jax.nn.sigmoid(gate)) * up).astype(BF16)
        o_ref[...] = _pack_halves(_dot(h1, wd_b[...]))

    @pl.when(jnp.logical_not(live))
    def _():
        o_ref[...] = jnp.zeros(o_ref.shape, o_ref.dtype)


def _grouped_mlp(tile_expert, tile_valid, xs_part, tile0, ys_prev, wg, wu, wd, layer, tm=MOE_TM):
    dpk = xs_part.shape[1]
    d, dff = wg.shape[2], wg.shape[3]
    n_tiles = tile_expert.shape[0]

    def expert_block(rows, cols):
        return pl.BlockSpec((None, 1, rows, cols), lambda i, te, tv: (layer, te[tile0 + i], 0, 0),
                            pipeline_mode=pl.Buffered(1))

    in_specs = [pl.BlockSpec((tm, dpk), lambda i, te, tv: (i, 0)),
                expert_block(d, dff), expert_block(d, dff), expert_block(dff, d)]
    args = [tile_expert, tile_valid, xs_part, wg, wu, wd]
    aliases = {}
    if ys_prev is not None:
        in_specs.append(pl.BlockSpec(memory_space=pl.ANY))
        args.append(ys_prev)
        aliases = {len(args) - 1: 0}
    grid_spec = pltpu.PrefetchScalarGridSpec(
        num_scalar_prefetch=2,
        grid=(xs_part.shape[0] // tm,),
        in_specs=in_specs,
        out_specs=pl.BlockSpec((tm, dpk), lambda i, te, tv: (tile0 + i, 0)),
        scratch_shapes=[pltpu.VMEM((d, dff), BF16), pltpu.VMEM((d, dff), BF16), pltpu.VMEM((dff, d), BF16)],
    )
    return pl.pallas_call(
        functools.partial(_gmm_kernel, tile0=tile0),
        out_shape=jax.ShapeDtypeStruct((n_tiles * tm, dpk), jnp.uint32),
        grid_spec=grid_spec,
        input_output_aliases=aliases,
        compiler_params=_cparams(("arbitrary",)),
        name="grouped_mlp",
    )(*args)


def _combine_kernel(*refs, n_lat_tiles, has_ctx):
    if has_ctx:
        x_ref, xc_ref, y0_ref, y1_ref, r_ref, g_ref, o_ref, oc_ref = refs
    else:
        x_ref, y0_ref, y1_ref, r_ref, g_ref, o_ref = refs
    w0 = r_ref[:, 2:3]
    w1 = r_ref[:, 3:4]
    lo0, hi0 = _unpack_halves(y0_ref[...])
    lo1, hi1 = _unpack_halves(y1_ref[...])
    f = g_ref[...] * jnp.concatenate([w0 * lo0 + w1 * lo1, w0 * hi0 + w1 * hi1], axis=1)
    if not has_ctx:
        o_ref[...] = x_ref[...] + f
        return
    i = pl.program_id(0)

    @pl.when(i < n_lat_tiles)
    def _():
        o_ref[...] = x_ref[...] + f

    @pl.when(i >= n_lat_tiles)
    def _():
        oc_ref[...] = xc_ref[...] + f


def _combine(x, cx, yg, route_cols, mod3, gate_blk, rows_per_mod, ctx_mod_row, tm=512):
    n, d = x.shape
    n_lat = n // tm
    has_ctx = cx is not None
    ntot = n_lat + (cx.shape[0] // tm if has_ctx else 0)
    mod_row, lat, ctx = _lat_ctx_maps(n_lat, rows_per_mod // tm, ctx_mod_row)
    row_specs = [pl.BlockSpec((tm, d), lat)] + ([pl.BlockSpec((tm, d), ctx)] if has_ctx else [])
    row_args = [x] + ([cx] if has_ctx else [])
    out_shape = [jax.ShapeDtypeStruct(x.shape, F32)] + ([jax.ShapeDtypeStruct(cx.shape, F32)] if has_ctx else [])
    out = pl.pallas_call(
        functools.partial(_combine_kernel, n_lat_tiles=n_lat, has_ctx=has_ctx),
        out_shape=tuple(out_shape),
        grid=(ntot,),
        in_specs=row_specs + [
            pl.BlockSpec((tm, d // 2), lambda i: (i, 0)),
            pl.BlockSpec((tm, d // 2), lambda i: (ntot + i, 0)),
            pl.BlockSpec((tm, SUBLANES), lambda i: (i, 0)),
            pl.BlockSpec((None, 1, d), lambda i: (mod_row(i), 0, gate_blk))],
        out_specs=tuple(row_specs),
        compiler_params=_cparams(("arbitrary",)),
        name="moe_combine",
    )(*row_args, yg, yg, route_cols, mod3)
    return out if has_ctx else (out[0], None)


def _dispatch_plan(route, tm):
    n = route.shape[1]
    e_flat = jnp.concatenate([route[0], route[1]]).astype(jnp.int32)
    rank = jnp.concatenate([route[4], route[5]]).astype(jnp.int32)
    n_assign = 2 * n
    n_tiles = n_assign // tm + N_EXPERTS
    experts = jnp.arange(N_EXPERTS, dtype=jnp.int32)[:, None]
    onehot = (experts == e_flat[None, :]).astype(jnp.int32)
    counts = jnp.sum(onehot, axis=1)
    padded = ((counts + tm - 1) // tm) * tm
    ends_p = jnp.cumsum(padded)
    starts_p = ends_p - padded
    starts_c = jnp.cumsum(counts) - counts
    dest = jnp.sum(onehot * starts_p[:, None], axis=0) + rank
    by_expert = jnp.sum(onehot * starts_c[:, None], axis=0) + rank
    order = jnp.argsort(by_expert).astype(jnp.int32)
    p = jnp.arange(n_tiles * tm, dtype=jnp.int32)[None, :]
    owner = ((p >= starts_p[:, None]) & (p < ends_p[:, None])).astype(jnp.int32)
    within = jnp.sum(owner * (p - starts_p[:, None]), axis=0)
    live = jnp.sum(owner * (p - starts_p[:, None] < counts[:, None]), axis=0) > 0
    compact = jnp.sum(owner * starts_c[:, None], axis=0) + within
    src_tok = jnp.where(live, order[jnp.clip(compact, 0, n_assign - 1)] % n, p[0] % n)
    tile_start = jnp.arange(n_tiles, dtype=jnp.int32) * tm
    tile_valid = (tile_start < ends_p[-1]).astype(jnp.int32)
    last_tile = jnp.maximum(ends_p[-1] // tm - 1, 0) * tm
    tile_row = jnp.minimum(tile_start, last_tile)[:, None]
    tile_expert = jnp.sum((ends_p[None, :] <= tile_row).astype(jnp.int32), axis=1)
    tile_expert = jnp.minimum(tile_expert, N_EXPERTS - 1)
    return src_tok, dest, tile_expert, tile_valid


def _moe(x, cx, g, mod3, rw_t, rb, wg, wu, wd, layer, rows_per_mod, ctx_mod_row, combine=True):
    h, route = _norm_router(x, cx, g, mod3, 3, 4, rw_t, rb, rows_per_mod, ctx_mod_row)
    src_tok, dest, tile_expert, tile_valid = _dispatch_plan(route, MOE_TM)
    n_tiles = tile_expert.shape[0]
    bounds = [n_tiles * k // MOE_RANGES for k in range(MOE_RANGES + 1)]
    ys = None
    for t0, t1 in zip(bounds[:-1], bounds[1:]):
        xs = _sc_gather_rows(h, src_tok[t0 * MOE_TM:t1 * MOE_TM])
        ys = _grouped_mlp(tile_expert, tile_valid, xs, t0, ys, wg, wu, wd, layer)
    yg = _sc_gather_rows(ys, dest)
    route_cols = route.T
    if not combine:
        return yg, route_cols
    return _combine(x, cx, yg, route_cols, mod3, 5, rows_per_mod, ctx_mod_row)


def _rope_tables(seq):
    rows = seq // GRID_W
    row = jnp.repeat(jnp.arange(rows, dtype=F32), GRID_W)
    col = jnp.tile(jnp.arange(GRID_W, dtype=F32), rows)
    n_freq = HEAD_DIM // 4
    inv_freq = ROPE_BASE ** (-jnp.arange(n_freq, dtype=F32) / n_freq)
    ang = jnp.concatenate([row[:, None] * inv_freq, col[:, None] * inv_freq], axis=-1)
    cos, sin = jnp.cos(ang), jnp.sin(ang)
    return jnp.concatenate([cos, cos], axis=-1), jnp.concatenate([-sin, sin], axis=-1)


def kernel(x, c, ctx, c_ctx, ada_w, ada_b, norm_mix, norm_ffn, ab_w_in, ab_q_gain, ab_k_gain, ab_sink, ab_conv_w, ab_conv_b, ab_gate_a_w, ab_gate_a_b, ab_gate_x_w, ab_gate_x_b, ab_lru_lambda, ab_w_out, gqa_w_in, gqa_q_gain, gqa_k_gain, gqa_w_out, router_w, router_bias, moe_w_gate, moe_w_up, moe_w_down):
    batch, seq, d = x.shape
    ctx_len = ctx.shape[1]
    depth = ada_w.shape[0]
    assert depth == 2 and batch < SUBLANES
    n_lat = batch * seq
    n_ctx = batch * ctx_len
    ctx_row = batch

    xl = x.reshape(n_lat, d)
    xc = ctx.reshape(n_ctx, d)
    cc = jnp.zeros((SUBLANES, d), F32).at[:batch].set(c).at[ctx_row].set(c_ctx)
    mod = _ada(cc, ada_w, ada_b)
    cos2, sin2 = _rope_tables(seq)
    rw_t = router_w.T.astype(BF16)
    rb = router_bias.reshape(N_EXPERTS, 1).astype(F32)
    experts = (moe_w_gate, moe_w_up, moe_w_down)

    mod3 = mod[0].reshape(SUBLANES, 1, 6 * d)
    w_in = ab_w_in[0].astype(BF16)
    proj = _norm_mod_matmul(xl, norm_mix[0], mod3, 0, 1, w_in, seq, 0,
                            A_Q_HEADS, A_KV_HEADS, ab_q_gain[0], ab_k_gain[0], cos2, sin2)
    proj_c = _norm_mod_matmul(xc, norm_mix[0], mod3, 0, 1, w_in, n_ctx, ctx_row,
                              A_Q_HEADS, A_KV_HEADS, ab_q_gain[0], ab_k_gain[0])
    att = _win_attn(ab_sink[0], proj, proj_c, batch, seq, ctx_len)
    att_c = _ctx_attn(ab_sink[0], proj_c, batch, ctx_len)

    lru_w = ab_conv_w.shape[2]
    c0 = (A_Q_HEADS + 2 * A_KV_HEADS) * HEAD_DIM
    y_p, yc_p = _lru(_to_chunked(proj[:, c0:c0 + lru_w], batch),
                     _to_chunked(proj[:, c0 + lru_w:c0 + 2 * lru_w], batch),
                     _to_chunked(proj_c[:, c0:c0 + lru_w], batch),
                     _to_chunked(proj_c[:, c0 + lru_w:c0 + 2 * lru_w], batch),
                     ab_conv_w[0], ab_conv_b[0], ab_gate_a_w[0], ab_gate_a_b[0],
                     ab_gate_x_w[0], ab_gate_x_b[0], ab_lru_lambda[0])
    w_out = ab_w_out[0].astype(BF16)
    xl = _out_proj(att, 0, _from_chunked(y_p), 0, w_out, xl, mod3, 2, seq, 0)
    xc = _out_proj(att_c, 0, _from_chunked(yc_p), 0, w_out, xc, mod3, 2, n_ctx, ctx_row)
    yg, route_cols = _moe(xl, xc, norm_ffn[0], mod3, rw_t, rb, *experts, 0, seq, ctx_row, combine=False)

    mod3_prev, mod3 = mod3, mod[1].reshape(SUBLANES, 1, 6 * d)
    w_in = gqa_w_in[0].astype(BF16)
    cw = C_Q_HEADS * HEAD_DIM
    proj, xl = _norm_mod_matmul(xl, norm_mix[1], mod3, 0, 1, w_in, seq, 0,
                                C_Q_HEADS, C_KV_HEADS, gqa_q_gain[0], gqa_k_gain[0], cos2, sin2,
                                pending_moe=(yg, route_cols, 0, mod3_prev, 5))
    proj_c, _ = _norm_mod_matmul(xc, norm_mix[1], mod3, 0, 1, w_in[:, cw:], n_ctx, ctx_row,
                                 0, C_KV_HEADS, gqa_q_gain[0], gqa_k_gain[0],
                                 pending_moe=(yg, route_cols, n_lat, mod3_prev, 5))
    att = _dense_attn(proj, proj_c, batch, seq, ctx_len)
    xl = _out_proj(att, 0, att, 1, gqa_w_out[0].astype(BF16), xl, mod3, 2, seq, 0)
    xl, _ = _moe(xl, None, norm_ffn[1], mod3, rw_t, rb, *experts, 1, seq, ctx_row)
    return xl.reshape(batch, seq, d)
```

```python
import functools

import jax
import jax.numpy as jnp
import numpy as np
from jax import lax
from jax.experimental import pallas as pl
from jax.experimental.pallas import tpu as pltpu
from jax.experimental.pallas import tpu_sc as plsc

F32 = jnp.float32
BF16 = jnp.bfloat16

LANES = 128
SUBLANES = 8
VMEM_LIMIT = 56 * 1024 * 1024

HEAD_DIM = 128
GRID_W = 64
WINDOW = 128
BLOCK = 128
ROPE_BASE = 10000.0
EPS = 1e-6
ATTN_SCALE = HEAD_DIM ** -0.5
A_Q_HEADS, A_KV_HEADS = 8, 2
C_Q_HEADS, C_KV_HEADS = 16, 4
GQA_GROUP = 4
LRU_C = 8.0
CONV_W = 4
CONV_LEFT = 2
N_EXPERTS = 16
N_GROUPS = 4
EXPERTS_PER_GROUP = 4
NEG_BIG = -1e30

LRU_CHUNKS = SUBLANES
LRU_JB = 16
LRU_LANE_BLOCKS = 2
LRU_SCAN_ROWS = 4
MOE_TM = 256
MOE_RANGES = 2
SC_CORES = 2
SC_WORKERS = 32
SC_GATHER_WINDOW = 32
SC_GATHER_BUFFERS = 3
SC_GATHER_SUB = 8


def _cparams(sem, vmem=VMEM_LIMIT):
    return pltpu.CompilerParams(dimension_semantics=sem, vmem_limit_bytes=vmem)


def _dot(a, b):
    return jnp.dot(a, b, preferred_element_type=F32)


def _dot_nt(a, b):
    return lax.dot_general(a, b, (((1,), (1,)), ((), ())), preferred_element_type=F32)


def _ada_kernel(c_ref, w_ref, b_ref, o_ref):
    c = c_ref[...]
    s = (c * jax.nn.sigmoid(c)).astype(BF16)
    o_ref[0] = _dot(s, w_ref[0].astype(BF16)) + b_ref[0]


def _ada(cc, ada_w, ada_b):
    depth, d, n = ada_w.shape
    tn = 1024
    return pl.pallas_call(
        _ada_kernel,
        out_shape=jax.ShapeDtypeStruct((depth, SUBLANES, n), F32),
        grid=(depth, n // tn),
        in_specs=[pl.BlockSpec((SUBLANES, d), lambda l, j: (0, 0)),
                  pl.BlockSpec((1, d, tn), lambda l, j: (l, 0, j)),
                  pl.BlockSpec((1, 1, tn), lambda l, j: (l, 0, j))],
        out_specs=pl.BlockSpec((1, SUBLANES, tn), lambda l, j: (l, 0, j)),
        compiler_params=_cparams(("arbitrary", "arbitrary")),
        name="ada",
    )(cc, ada_w, ada_b.reshape(depth, 1, n))


def _norm_mod_rows(x_ref, g_ref, sh_ref, sc_ref, dst_ref, tm, rc=128, load_rows=None):
    g = g_ref[...]
    sc1 = 1.0 + sc_ref[...]
    sh = sh_ref[...]

    def body(r, carry):
        rows = pl.ds(pl.multiple_of(r * rc, rc), rc)
        xf = x_ref[rows, :] if load_rows is None else load_rows(rows)
        ms = jnp.mean(xf * xf, axis=-1, keepdims=True)
        xn = (xf * lax.rsqrt(ms + EPS)) * g
        dst_ref[rows, :] = (xn * sc1 + sh).astype(dst_ref.dtype)
        return carry

    lax.fori_loop(0, tm // rc, body, 0)


def _moe_residual_rows(x_ref, y0_ref, y1_ref, r_ref, gate_ref, rows):
    w0 = r_ref[rows, 2:3]
    w1 = r_ref[rows, 3:4]
    lo0, hi0 = _unpack_halves(y0_ref[rows, :])
    lo1, hi1 = _unpack_halves(y1_ref[rows, :])
    f = jnp.concatenate([w0 * lo0 + w1 * lo1, w0 * hi0 + w1 * hi1], axis=1)
    return x_ref[rows, :] + gate_ref[...] * f


def _nm_mm_kernel(*refs, tm, tn, n_q, n_k, rope, pending_moe):
    refs = list(refs)
    x_ref = refs.pop(0)
    if pending_moe:
        y0_ref, y1_ref, r_ref, gate_ref = (refs.pop(0) for _ in range(4))
    g_ref, sh_ref, sc_ref, w_ref, qg_ref, kg_ref = (refs.pop(0) for _ in range(6))
    if rope:
        cos_ref, sin_ref = refs.pop(0), refs.pop(0)
    o_ref = refs.pop(0)
    if pending_moe:
        xnew_ref = refs.pop(0)

        def load_rows(rows):
            xf = _moe_residual_rows(x_ref, y0_ref, y1_ref, r_ref, gate_ref, rows)
            xnew_ref[rows, :] = xf
            return xf
    else:
        load_rows = None
    hn_ref = refs.pop(0)
    _norm_mod_rows(x_ref, g_ref, sh_ref, sc_ref, hn_ref, tm, load_rows=load_rows)
    h = hn_ref[...]
    heads_per_chunk = tn // HEAD_DIM
    for j in range(w_ref.shape[1] // tn):
        cols = slice(j * tn, (j + 1) * tn)
        y = _dot(h, w_ref[:, cols])
        parts = []
        for hh in range(heads_per_chunk):
            head = j * heads_per_chunk + hh
            yh = y[:, _head_cols(hh)]
            if head < n_q + n_k:
                gain = qg_ref[...] if head < n_q else kg_ref[...]
                ms = jnp.mean(yh * yh, axis=-1, keepdims=True)
                yh = (yh * lax.rsqrt(ms + EPS)) * gain
                if rope:
                    yh = yh * cos_ref[...] + pltpu.roll(yh, HEAD_DIM // 2, 1) * sin_ref[...]
                if head < n_q:
                    yh = yh * ATTN_SCALE
            parts.append(yh.astype(o_ref.dtype))
        o_ref[:, cols] = jnp.concatenate(parts, axis=1)


def _norm_mod_matmul(x, g, mod3, shift_blk, scale_blk, w, rows_per_mod, mod_base,
                     n_q, n_k, q_gain, k_gain, cos2=None, sin2=None, pending_moe=None, tm=512, tn=512):
    n, d = x.shape
    nout = w.shape[1]
    tm = min(tm, n)
    tn = min(tn, nout)
    per = rows_per_mod // tm
    rope = cos2 is not None

    def mod_row(i):
        return mod_base + i // per

    head_vec = pl.BlockSpec((1, HEAD_DIM), lambda i: (0, 0))
    in_specs = [pl.BlockSpec((tm, d), lambda i: (i, 0))]
    args = [x]
    out_shape = [jax.ShapeDtypeStruct((n, nout), BF16)]
    out_specs = [pl.BlockSpec((tm, nout), lambda i: (i, 0))]
    if pending_moe is not None:
        yg, route_cols, first_row, mod3_prev, gate_blk = pending_moe
        t0 = first_row // tm
        slot1 = yg.shape[0] // 2 // tm
        in_specs += [pl.BlockSpec((tm, d // 2), lambda i: (t0 + i, 0)),
                     pl.BlockSpec((tm, d // 2), lambda i: (slot1 + t0 + i, 0)),
                     pl.BlockSpec((tm, SUBLANES), lambda i: (t0 + i, 0)),
                     pl.BlockSpec((None, 1, d), lambda i: (mod_row(i), 0, gate_blk))]
        args += [yg, yg, route_cols, mod3_prev]
        out_shape.append(jax.ShapeDtypeStruct((n, d), F32))
        out_specs.append(pl.BlockSpec((tm, d), lambda i: (i, 0)))
    in_specs += [pl.BlockSpec((1, d), lambda i: (0, 0)),
                 pl.BlockSpec((None, 1, d), lambda i: (mod_row(i), 0, shift_blk)),
                 pl.BlockSpec((None, 1, d), lambda i: (mod_row(i), 0, scale_blk)),
                 pl.BlockSpec((d, nout), lambda i: (0, 0), pipeline_mode=pl.Buffered(1)),
                 head_vec, head_vec]
    args += [g.reshape(1, d), mod3, mod3, w, q_gain.reshape(1, HEAD_DIM), k_gain.reshape(1, HEAD_DIM)]
    if rope:
        tiles_per_seq = cos2.shape[0] // tm
        table = pl.BlockSpec((tm, HEAD_DIM), lambda i: (i % tiles_per_seq, 0))
        in_specs += [table, table]
        args += [cos2, sin2]
    out = pl.pallas_call(
        functools.partial(_nm_mm_kernel, tm=tm, tn=tn, n_q=n_q, n_k=n_k, rope=rope,
                          pending_moe=pending_moe is not None),
        out_shape=tuple(out_shape),
        grid=(n // tm,),
        in_specs=in_specs,
        out_specs=tuple(out_specs),
        scratch_shapes=[pltpu.VMEM((tm, d), BF16)],
        compiler_params=_cparams(("parallel",)),
        name="norm_mod_matmul",
    )(*args)
    return out if pending_moe is not None else out[0]


def _head_cols(h):
    return slice(h * HEAD_DIM, (h + 1) * HEAD_DIM)


def _stack_group(q_ref, kvh):
    return jnp.concatenate([q_ref[:, _head_cols(kvh * GQA_GROUP + g)] for g in range(GQA_GROUP)], axis=0)


def _sink_col(sink_ref, kvh, rows):
    return jnp.concatenate([jnp.full((rows, 1), sink_ref[kvh * GQA_GROUP + g], F32)
                            for g in range(GQA_GROUP)], axis=0)


def _band_bias(ctx_len):
    rows, nk = GQA_GROUP * BLOCK, 3 * BLOCK + ctx_len
    qi = np.arange(rows)[:, None] % BLOCK
    kj = np.arange(nk)[None, :]
    inner = (kj >= 3 * BLOCK) | (np.abs(kj - BLOCK - qi) <= WINDOW)
    first = inner & ~(kj < BLOCK)
    last = inner & ~((kj >= 2 * BLOCK) & (kj < 3 * BLOCK))
    return np.where(np.stack([first, inner, last]), 0.0, NEG_BIG).astype(np.float32)


def _win_attn_kernel(sink_ref, bias_ref, q_ref, kp_ref, kc_ref, kn_ref, vp_ref, vc_ref, vn_ref,
                     kx_ref, vx_ref, o_ref, *, ctx_len):
    nk = 3 * BLOCK + ctx_len
    bias = bias_ref[...]
    ones = jnp.ones((nk, HEAD_DIM), BF16)
    for kvh in range(A_KV_HEADS):
        cols = _head_cols(kvh)
        q4 = _stack_group(q_ref, kvh)
        ka = jnp.concatenate([kp_ref[:, cols], kc_ref[:, cols], kn_ref[:, cols], kx_ref[:, cols]], axis=0)
        va = jnp.concatenate([vp_ref[:, cols], vc_ref[:, cols], vn_ref[:, cols], vx_ref[:, cols]], axis=0)
        s = _dot_nt(q4, ka) + bias
        sk = _sink_col(sink_ref, kvh, BLOCK)
        m = jnp.maximum(jnp.max(s, axis=-1, keepdims=True), sk)
        p = jnp.exp((s - m).astype(BF16))
        acc = _dot(p, jnp.concatenate([va, ones], axis=1))
        o = acc[:, 0:HEAD_DIM] / (acc[:, HEAD_DIM:] + jnp.exp(sk - m))
        for g in range(GQA_GROUP):
            o_ref[:, _head_cols(kvh * GQA_GROUP + g)] = o[g * BLOCK:(g + 1) * BLOCK].astype(o_ref.dtype)


def _win_attn(sink, proj, proj_c, batch, seq, ctx_len):
    nb = seq // BLOCK
    assert nb >= 2
    kvw = A_KV_HEADS * HEAD_DIM
    k_blk = A_Q_HEADS * HEAD_DIM // kvw
    v_blk = k_blk + 1
    bias = jnp.asarray(_band_bias(ctx_len))

    def which_bias(b, n):
        return (jnp.where(n == 0, 0, jnp.where(n == nb - 1, 2, 1)), 0, 0)

    def prev(b, n):
        return b * nb + jnp.maximum(n - 1, 0)

    def cur(b, n):
        return b * nb + n

    def nxt(b, n):
        return b * nb + jnp.minimum(n + 1, nb - 1)

    return pl.pallas_call(
        functools.partial(_win_attn_kernel, ctx_len=ctx_len),
        out_shape=jax.ShapeDtypeStruct((batch * seq, A_Q_HEADS * HEAD_DIM), BF16),
        grid=(batch, nb),
        in_specs=[pl.BlockSpec(memory_space=pltpu.SMEM),
                  pl.BlockSpec((None,) + bias.shape[1:], which_bias),
                  pl.BlockSpec((BLOCK, A_Q_HEADS * HEAD_DIM), lambda b, n: (cur(b, n), 0)),
                  pl.BlockSpec((BLOCK, kvw), lambda b, n: (prev(b, n), k_blk)),
                  pl.BlockSpec((BLOCK, kvw), lambda b, n: (cur(b, n), k_blk)),
                  pl.BlockSpec((BLOCK, kvw), lambda b, n: (nxt(b, n), k_blk)),
                  pl.BlockSpec((BLOCK, kvw), lambda b, n: (prev(b, n), v_blk)),
                  pl.BlockSpec((BLOCK, kvw), lambda b, n: (cur(b, n), v_blk)),
                  pl.BlockSpec((BLOCK, kvw), lambda b, n: (nxt(b, n), v_blk)),
                  pl.BlockSpec((ctx_len, kvw), lambda b, n: (b, k_blk)),
                  pl.BlockSpec((ctx_len, kvw), lambda b, n: (b, v_blk))],
        out_specs=pl.BlockSpec((BLOCK, A_Q_HEADS * HEAD_DIM), lambda b, n: (cur(b, n), 0)),
        compiler_params=_cparams(("parallel", "parallel")),
        name="win_attn",
    )(sink, bias, proj, proj, proj, proj, proj, proj, proj, proj_c, proj_c)


def _ctx_attn_kernel(sink_ref, q_ref, k_ref, v_ref, o_ref, *, ctx_len):
    kvh = pl.program_id(1)
    q4 = jnp.concatenate([q_ref[:, _head_cols(g)] for g in range(GQA_GROUP)], axis=0)
    s = _dot_nt(q4, k_ref[...])
    sk = jnp.concatenate([jnp.full((ctx_len, 1), sink_ref[kvh * GQA_GROUP + g], F32)
                          for g in range(GQA_GROUP)], axis=0)
    m = jnp.maximum(jnp.max(s, axis=-1, keepdims=True), sk)
    p = jnp.exp(s - m)
    den = jnp.sum(p, axis=-1, keepdims=True) + jnp.exp(sk - m)
    o = _dot(p.astype(BF16), v_ref[...]) / den
    for g in range(GQA_GROUP):
        o_ref[:, _head_cols(g)] = o[g * ctx_len:(g + 1) * ctx_len].astype(o_ref.dtype)


def _ctx_attn(sink, proj_c, batch, ctx_len):
    k_blk = A_Q_HEADS
    v_blk = A_Q_HEADS + A_KV_HEADS
    gw = GQA_GROUP * HEAD_DIM
    return pl.pallas_call(
        functools.partial(_ctx_attn_kernel, ctx_len=ctx_len),
        out_shape=jax.ShapeDtypeStruct((batch * ctx_len, A_Q_HEADS * HEAD_DIM), BF16),
        grid=(batch, A_KV_HEADS),
        in_specs=[pl.BlockSpec(memory_space=pltpu.SMEM),
                  pl.BlockSpec((ctx_len, gw), lambda b, h: (b, h)),
                  pl.BlockSpec((ctx_len, HEAD_DIM), lambda b, h: (b, k_blk + h)),
                  pl.BlockSpec((ctx_len, HEAD_DIM), lambda b, h: (b, v_blk + h))],
        out_specs=pl.BlockSpec((ctx_len, gw), lambda b, h: (b, h)),
        compiler_params=_cparams(("parallel", "parallel")),
        name="ctx_attn",
    )(sink, proj_c, proj_c, proj_c)


def _sigmoid(x):
    return 0.5 * jnp.tanh(0.5 * x) + 0.5


def _gelu_tanh(x):
    return 0.5 * x * (1.0 + jnp.tanh(0.7978845608028654 * (x + 0.044715 * (x * x * x))))


def _lru_sequence(x_ref, xg_ref, y_ref, xp_ref, a_ref, b_ref, w, init, rows, jb):
    conv_w, conv_b, wa, ba, wx, bx, c_logsig = w
    width = x_ref.shape[-1]
    sub = lax.broadcasted_iota(jnp.int32, (1, SUBLANES, width), 1)

    def block_diag(ub, wd):
        return jnp.concatenate([_dot(ub[:, _head_cols(n)], wd[n]) for n in range(width // LANES)], axis=1)

    def fill(r, carry):
        rr = pl.ds(pl.multiple_of(r * jb, jb), jb)
        xp_ref[pl.ds(pl.multiple_of(r * jb, jb) + CONV_LEFT, jb)] = x_ref[rr].astype(F32)
        return carry

    lax.fori_loop(0, rows // jb, fill, 0)
    tail = x_ref[rows - CONV_LEFT:rows].astype(F32)
    xp_ref[0:CONV_LEFT] = jnp.where(sub == 0, 0.0, pltpu.roll(tail, 1, 1))
    head = x_ref[0:1].astype(F32)
    xp_ref[rows + CONV_LEFT:rows + CONV_LEFT + 1] = jnp.where(
        sub == SUBLANES - 1, 0.0, pltpu.roll(head, SUBLANES - 1, 1))

    def gates(r, carry):
        j0 = pl.multiple_of(r * jb, jb)
        u = conv_b
        for k in range(CONV_W):
            u = u + conv_w[k] * xp_ref[pl.ds(j0 + k, jb)]
        u2 = u.reshape(jb * SUBLANES, width)
        ub = u2.astype(BF16)
        for d in range(2):
            r_gate = _sigmoid(block_diag(ub, wa[d]) + ba[d])
            i_gate = _sigmoid(block_diag(ub, wx[d]) + bx[d])
            log_a = c_logsig[d] * r_gate
            a = jnp.exp(log_a)
            v = 1.0 - a * a
            root = jnp.where(v > 0.0, v * lax.rsqrt(v), 0.0)
            b = root * (i_gate * u2)
            a_ref[d, pl.ds(j0, jb)] = a.reshape(jb, SUBLANES, width)
            b_ref[d, pl.ds(j0, jb)] = b.reshape(jb, SUBLANES, width)
        return carry

    lax.fori_loop(0, rows // jb, gates, 0, unroll=2)

    def scan(it, carry):
        hf, pf, hb, pb = carry
        fwd = pl.ds(pl.multiple_of(it * LRU_SCAN_ROWS, LRU_SCAN_ROWS), LRU_SCAN_ROWS)
        bwd = pl.ds(pl.multiple_of(rows - LRU_SCAN_ROWS - it * LRU_SCAN_ROWS, LRU_SCAN_ROWS), LRU_SCAN_ROWS)
        af, bf = a_ref[0, fwd], b_ref[0, fwd]
        ab, bb = a_ref[1, bwd], b_ref[1, bwd]
        hs_f, ps_f = [], []
        hs_b, ps_b = [None] * LRU_SCAN_ROWS, [None] * LRU_SCAN_ROWS
        for r in range(LRU_SCAN_ROWS):
            hf = af[r] * hf + bf[r]
            pf = pf * af[r]
            hs_f.append(hf)
            ps_f.append(pf)
            rr = LRU_SCAN_ROWS - 1 - r
            hb = ab[rr] * hb + bb[rr]
            pb = pb * ab[rr]
            hs_b[rr] = hb
            ps_b[rr] = pb
        b_ref[0, fwd] = jnp.stack(hs_f)
        a_ref[0, fwd] = jnp.stack(ps_f)
        b_ref[1, bwd] = jnp.stack(hs_b)
        a_ref[1, bwd] = jnp.stack(ps_b)
        return hf, pf, hb, pb

    z = jnp.zeros((SUBLANES, width), F32)
    o = jnp.ones((SUBLANES, width), F32)
    lax.fori_loop(0, rows // LRU_SCAN_ROWS, scan, (z, o, z, o), unroll=2)

    hf_last, pf_last = b_ref[0, rows - 1], a_ref[0, rows - 1]
    hb_last, pb_last = b_ref[1, 0], a_ref[1, 0]
    s = init[0]
    carry_f = []
    for c in range(SUBLANES):
        carry_f.append(s)
        s = hf_last[c:c + 1] + pf_last[c:c + 1] * s
    out_f = s
    s = init[1]
    carry_b = [None] * SUBLANES
    for c in reversed(range(SUBLANES)):
        carry_b[c] = s
        s = hb_last[c:c + 1] + pb_last[c:c + 1] * s
    out_b = s
    cf = jnp.concatenate(carry_f, axis=0)
    cb = jnp.concatenate(carry_b, axis=0)

    def emit(r, carry):
        rr = pl.ds(pl.multiple_of(r * jb, jb), jb)
        h = (b_ref[0, rr] + a_ref[0, rr] * cf) + (b_ref[1, rr] + a_ref[1, rr] * cb)
        y_ref[rr] = (h * _gelu_tanh(xg_ref[rr].astype(F32))).astype(y_ref.dtype)
        return carry

    lax.fori_loop(0, rows // jb, emit, 0)
    return out_f, out_b


def _lru_kernel(xr_ref, xg_ref, xrc_ref, xgc_ref, cw_ref, cb_ref, wa_ref, ba_ref, wx_ref, bx_ref,
                lam_ref, y_ref, yc_ref, xp_ref, a_ref, b_ref, *, rows, rows_c):
    c_logsig = [LRU_C * jax.nn.log_sigmoid(lam_ref[d]) for d in range(2)]
    w = ([cw_ref[k] for k in range(CONV_W)], cb_ref[0],
         [wa_ref[d] for d in range(2)], [ba_ref[d] for d in range(2)],
         [wx_ref[d] for d in range(2)], [bx_ref[d] for d in range(2)], c_logsig)
    zero = jnp.zeros((1, xr_ref.shape[-1]), F32)
    sf, sb = _lru_sequence(xrc_ref, xgc_ref, yc_ref, xp_ref, a_ref, b_ref, w, (zero, zero),
                           rows_c, min(LRU_JB, rows_c))
    _lru_sequence(xr_ref, xg_ref, y_ref, xp_ref, a_ref, b_ref, w, (sf, sb), rows, LRU_JB)


def _lru(xr, xg, xrc, xgc, conv_w, conv_b, wa, ba, wx, bx, lam):
    batch, rows, _, width = xr.shape
    rows_c = xrc.shape[1]
    cw = LRU_LANE_BLOCKS * LANES
    seq_spec = pl.BlockSpec((None, rows, SUBLANES, cw), lambda b, n: (b, 0, 0, n))
    ctx_spec = pl.BlockSpec((None, rows_c, SUBLANES, cw), lambda b, n: (b, 0, 0, n))
    vec2 = pl.BlockSpec((2, 1, cw), lambda b, n: (0, 0, n))
    mat2 = pl.BlockSpec((2, LRU_LANE_BLOCKS, LANES, LANES), lambda b, n: (0, n, 0, 0))
    return pl.pallas_call(
        functools.partial(_lru_kernel, rows=rows, rows_c=rows_c),
        out_shape=(jax.ShapeDtypeStruct(xr.shape, BF16), jax.ShapeDtypeStruct(xrc.shape, BF16)),
        grid=(batch, width // cw),
        in_specs=[seq_spec, seq_spec, ctx_spec, ctx_spec,
                  pl.BlockSpec((CONV_W, 1, cw), lambda b, n: (0, 0, n)),
                  pl.BlockSpec((1, 1, cw), lambda b, n: (0, 0, n)),
                  mat2, vec2, mat2, vec2, vec2],
        out_specs=(seq_spec, ctx_spec),
        scratch_shapes=[pltpu.VMEM((rows + CONV_W - 1, SUBLANES, cw), F32),
                        pltpu.VMEM((2, rows, SUBLANES, cw), F32),
                        pltpu.VMEM((2, rows, SUBLANES, cw), F32)],
        compiler_params=_cparams(("parallel", "parallel")),
        name="rglru",
    )(xr, xg, xrc, xgc, conv_w.reshape(CONV_W, 1, width), conv_b.reshape(1, 1, width),
      wa.astype(BF16), ba.reshape(2, 1, width), wx.astype(BF16), bx.reshape(2, 1, width),
      lam.reshape(2, 1, width))


def _to_chunked(a, batch):
    t = a.shape[0] // batch
    return a.reshape(batch, LRU_CHUNKS, t // LRU_CHUNKS, a.shape[1]).transpose(0, 2, 1, 3)


def _from_chunked(a):
    b, r, c, w = a.shape
    return a.transpose(0, 2, 1, 3).reshape(b * r * c, w)


def _out_proj_kernel(a1_ref, a2_ref, w1_ref, w2_ref, x_ref, g_ref, o_ref):
    y = _dot(a1_ref[...], w1_ref[...]) + _dot(a2_ref[...], w2_ref[...])
    o_ref[...] = x_ref[...] + g_ref[...] * y


def _out_proj(a1, a1_blk, a2, a2_blk, w, x, mod3, gate_blk, rows_per_mod, mod_base, tm=2048, tn=512):
    n, d = x.shape
    kh = w.shape[0] // 2
    tm = min(tm, n)
    per = rows_per_mod // tm
    gpb = d // tn
    return pl.pallas_call(
        _out_proj_kernel,
        out_shape=jax.ShapeDtypeStruct((n, d), F32),
        grid=(n // tm, d // tn),
        in_specs=[pl.BlockSpec((tm, kh), lambda i, j: (i, a1_blk)),
                  pl.BlockSpec((tm, kh), lambda i, j: (i, a2_blk)),
                  pl.BlockSpec((kh, tn), lambda i, j: (0, j)),
                  pl.BlockSpec((kh, tn), lambda i, j: (1, j)),
                  pl.BlockSpec((tm, tn), lambda i, j: (i, j)),
                  pl.BlockSpec((None, 1, tn), lambda i, j: (mod_base + i // per, 0, gate_blk * gpb + j))],
        out_specs=pl.BlockSpec((tm, tn), lambda i, j: (i, j)),
        compiler_params=_cparams(("parallel", "parallel")),
        name="out_proj",
    )(a1, a2, w, w, x, mod3)


def _lane_block_max(s):
    mm = s[:, 0:LANES]
    for t in range(1, s.shape[1] // LANES):
        mm = jnp.maximum(mm, s[:, t * LANES:(t + 1) * LANES])
    return mm


def _exp_blocks(s, mrep):
    return jnp.concatenate(
        [jnp.exp((s[:, t * LANES:(t + 1) * LANES] - mrep).astype(BF16)) for t in range(s.shape[1] // LANES)],
        axis=1)


def _copy_key_rows(dst_ref, c, kc, lat_ref, ctx_ref, cols):
    seq = lat_ref.shape[0]
    lo, hi = c * kc, (c + 1) * kc
    if lo < seq:
        n = min(hi, seq) - lo
        dst_ref[c, 0:n, cols] = lat_ref[lo:lo + n, :]
    if hi > seq:
        start = max(lo, seq)
        dst_ref[c, start - lo:kc, cols] = ctx_ref[start - seq:hi - seq, :]


def _dense_attn_kernel(q_ref, k_ref, v_ref, kx_ref, vx_ref, o_ref,
                       s_ref, m_ref, acc_ref, ka_ref, va_ref, *, tq, n_chunks, kc):
    @pl.when(pl.program_id(2) == 0)
    def _():
        for c in range(n_chunks):
            _copy_key_rows(ka_ref, c, kc, k_ref, kx_ref, slice(0, HEAD_DIM))
            _copy_key_rows(va_ref, c, kc, v_ref, vx_ref, slice(0, HEAD_DIM))
            va_ref[c, :, HEAD_DIM:] = jnp.ones((kc, HEAD_DIM), BF16)

    q4 = jnp.concatenate([q_ref[:, _head_cols(g)] for g in range(GQA_GROUP)], axis=0)
    m_ref[...] = jnp.full(m_ref.shape, NEG_BIG, F32)

    def sweep1(c, carry):
        s = _dot_nt(q4, ka_ref[c])
        s_ref[c] = s
        m_ref[...] = jnp.maximum(m_ref[...], _lane_block_max(s))
        return carry

    lax.fori_loop(0, n_chunks, sweep1, 0)
    m_ref[...] = jnp.broadcast_to(jnp.max(m_ref[...], axis=-1, keepdims=True), m_ref.shape)
    acc_ref[...] = jnp.zeros(acc_ref.shape, F32)

    def sweep2(c, carry):
        acc_ref[...] += _dot(_exp_blocks(s_ref[c], m_ref[...]), va_ref[c])
        return carry

    lax.fori_loop(0, n_chunks, sweep2, 0)
    o = acc_ref[:, 0:HEAD_DIM] / acc_ref[:, HEAD_DIM:]
    for g in range(GQA_GROUP):
        o_ref[:, _head_cols(g)] = o[g * tq:(g + 1) * tq].astype(o_ref.dtype)


def _dense_attn(proj, proj_c, batch, seq, ctx_len, tq=256, n_chunks=2):
    gw = GQA_GROUP * HEAD_DIM
    nq = seq // tq
    rows = GQA_GROUP * tq
    kc = (seq + ctx_len) // n_chunks
    assert kc * n_chunks == seq + ctx_len and kc % LANES == 0
    k_blk = C_Q_HEADS
    v_blk = C_Q_HEADS + C_KV_HEADS
    vx_blk = C_KV_HEADS
    return pl.pallas_call(
        functools.partial(_dense_attn_kernel, tq=tq, n_chunks=n_chunks, kc=kc),
        out_shape=jax.ShapeDtypeStruct((batch * seq, C_Q_HEADS * HEAD_DIM), BF16),
        grid=(batch, C_KV_HEADS, nq),
        in_specs=[pl.BlockSpec((tq, gw), lambda b, h, i: (b * nq + i, h)),
                  pl.BlockSpec((seq, HEAD_DIM), lambda b, h, i: (b, k_blk + h)),
                  pl.BlockSpec((seq, HEAD_DIM), lambda b, h, i: (b, v_blk + h)),
                  pl.BlockSpec((ctx_len, HEAD_DIM), lambda b, h, i: (b, h)),
                  pl.BlockSpec((ctx_len, HEAD_DIM), lambda b, h, i: (b, vx_blk + h))],
        out_specs=pl.BlockSpec((tq, gw), lambda b, h, i: (b * nq + i, h)),
        scratch_shapes=[pltpu.VMEM((n_chunks, rows, kc), F32),
                        pltpu.VMEM((rows, LANES), F32),
                        pltpu.VMEM((rows, 2 * HEAD_DIM), F32),
                        pltpu.VMEM((n_chunks, kc, HEAD_DIM), BF16),
                        pltpu.VMEM((n_chunks, kc, 2 * HEAD_DIM), BF16)],
        compiler_params=_cparams(("parallel", "parallel", "arbitrary")),
        name="dense_attn",
    )(proj, proj, proj, proj_c, proj_c)


def _router_rows(biased, scores):
    v = [biased[e:e + 1, :] for e in range(N_EXPERTS)]
    s = [scores[e:e + 1, :] for e in range(N_EXPERTS)]

    def top2_sum(vals):
        best = vals[0] + vals[1]
        for i in range(len(vals)):
            for j in range(i + 1, len(vals)):
                if (i, j) != (0, 1):
                    best = jnp.maximum(best, vals[i] + vals[j])
        return best

    gsum = [top2_sum(v[g * EXPERTS_PER_GROUP:(g + 1) * EXPERTS_PER_GROUP]) for g in range(N_GROUPS)]
    sel = jnp.zeros_like(gsum[0], dtype=jnp.int32)
    best = gsum[0]
    for g in range(1, N_GROUPS):
        take = gsum[g] > best
        sel = jnp.where(take, g, sel)
        best = jnp.where(take, gsum[g], best)

    def pick_group(rows, i):
        out = rows[i]
        for g in range(1, N_GROUPS):
            out = jnp.where(sel == g, rows[g * EXPERTS_PER_GROUP + i], out)
        return out

    cand = [pick_group(v, i) for i in range(EXPERTS_PER_GROUP)]
    cand_s = [pick_group(s, i) for i in range(EXPERTS_PER_GROUP)]
    i1 = jnp.zeros_like(sel)
    b1 = cand[0]
    for i in range(1, EXPERTS_PER_GROUP):
        take = cand[i] > b1
        i1 = jnp.where(take, i, i1)
        b1 = jnp.where(take, cand[i], b1)
    i2 = jnp.full_like(sel, -1)
    b2 = jnp.full_like(b1, -jnp.inf)
    for i in range(EXPERTS_PER_GROUP):
        take = (i1 != i) & ((cand[i] > b2) | (i2 < 0))
        i2 = jnp.where(take, i, i2)
        b2 = jnp.where(take, cand[i], b2)

    def pick_idx(rows, idx):
        out = rows[0]
        for i in range(1, EXPERTS_PER_GROUP):
            out = jnp.where(idx == i, rows[i], out)
        return out

    s0 = pick_idx(cand_s, i1)
    s1 = pick_idx(cand_s, i2)
    tot = s0 + s1
    e0 = (sel * EXPERTS_PER_GROUP + i1).astype(F32)
    e1 = (sel * EXPERTS_PER_GROUP + i2).astype(F32)
    return e0, e1, s0 / tot, s1 / tot


def _norm_router_kernel(*refs, tm, n_lat_tiles, has_ctx, pending_proj):
    i = pl.program_id(0)
    if pending_proj:
        x_ref, a_ref, w_ref, gate_ref, *refs = refs
        refs = list(refs)
        xnew_ref = refs.pop(7)
        a = a_ref[...]
        tn = 512
        for j in range(w_ref.shape[1] // tn):
            cols = slice(j * tn, (j + 1) * tn)
            xnew_ref[:, cols] = x_ref[:, cols] + gate_ref[:, cols] * _dot(a, w_ref[:, cols])
        refs = [xnew_ref] + refs
    if has_ctx:
        x_ref, xc_ref, g_ref, sh_ref, sc_ref, rw_ref, rb_ref, hp_ref, r_ref, h_ref, tri_ref, run_ref = refs

        @pl.when(i < n_lat_tiles)
        def _():
            _norm_mod_rows(x_ref, g_ref, sh_ref, sc_ref, h_ref, tm)

        @pl.when(i >= n_lat_tiles)
        def _():
            _norm_mod_rows(xc_ref, g_ref, sh_ref, sc_ref, h_ref, tm)
    else:
        x_ref, g_ref, sh_ref, sc_ref, rw_ref, rb_ref, hp_ref, r_ref, h_ref, tri_ref, run_ref = refs
        _norm_mod_rows(x_ref, g_ref, sh_ref, sc_ref, h_ref, tm)

    @pl.when(i == 0)
    def _():
        run_ref[...] = jnp.zeros(run_ref.shape, F32)
        before = lax.broadcasted_iota(jnp.int32, (tm, tm), 0) <= lax.broadcasted_iota(jnp.int32, (tm, tm), 1)
        tri_ref[...] = jnp.where(before, 1.0, 0.0).astype(BF16)

    def pack(r, carry):
        rows = pl.ds(pl.multiple_of(r * 128, 128), 128)
        hp_ref[rows, :] = _pack_halves(h_ref[rows, :])
        return carry

    lax.fori_loop(0, tm // 128, pack, 0)
    logits = _dot_nt(rw_ref[...], h_ref[...])
    scores = jax.nn.sigmoid(logits)
    e0, e1, w0, w1 = _router_rows(scores + rb_ref[...], scores)
    expert_ids = lax.broadcasted_iota(jnp.int32, (N_EXPERTS, tm), 0).astype(F32)
    ranks = []
    for e_row in (e0, e1):
        hit = expert_ids == e_row
        seen = _dot(jnp.where(hit, 1.0, 0.0).astype(BF16), tri_ref[...])
        ranks.append(jnp.sum(jnp.where(hit, seen - 1.0 + run_ref[...], 0.0), axis=0, keepdims=True))
        run_ref[...] = run_ref[...] + seen[:, tm - 1:tm]
    zero = jnp.zeros_like(w0)
    r_ref[...] = jnp.concatenate([e0, e1, w0, w1, ranks[0], ranks[1], zero, zero], axis=0)


def _pack_halves(v):
    c = v.shape[1] // 2
    lo = lax.bitcast_convert_type(v[:, :c].astype(BF16).astype(F32), jnp.uint32)
    hi = lax.bitcast_convert_type(v[:, c:].astype(BF16).astype(F32), jnp.uint32)
    return (lo >> 16) | (hi & jnp.uint32(0xFFFF0000))


def _unpack_halves(p):
    lo = lax.bitcast_convert_type(p << 16, F32)
    hi = lax.bitcast_convert_type(p & jnp.uint32(0xFFFF0000), F32)
    return lo, hi


def _sc_gather_rows(table, idx):
    n, w = idx.shape[0], table.shape[1]
    win, nb, sub = SC_GATHER_WINDOW, SC_GATHER_BUFFERS, SC_GATHER_SUB
    per = n // (win * SC_WORKERS)
    assert per * win * SC_WORKERS == n and per >= 1
    mesh = plsc.VectorSubcoreMesh(core_axis_name="core", subcore_axis_name="subcore")

    @functools.partial(
        pl.kernel, out_type=jax.ShapeDtypeStruct((n, w), table.dtype), mesh=mesh, name="sc_gather_rows",
        scratch_types=([pltpu.VMEM((per * win,), jnp.int32)] + [pltpu.VMEM((win, w), table.dtype)] * nb
                       + [pltpu.SemaphoreType.DMA] * (2 * nb)))
    def gather(x_hbm, i_hbm, o_hbm, i_v, *rest):
        bufs, gsems, wsems = rest[:nb], rest[nb:2 * nb], rest[2 * nb:]
        wid = lax.axis_index("subcore") * SC_CORES + lax.axis_index("core")
        base = wid * (per * win)
        pltpu.sync_copy(i_hbm.at[pl.ds(base, per * win)], i_v)
        rot = (wid * per) // SC_WORKERS

        def row0(t):
            u = t + rot
            u = u - per * (u >= per).astype(jnp.int32)
            return pl.multiple_of(u * win, win)

        def gather_copy(t, b, s):
            return pltpu.make_async_copy(x_hbm.at[i_v.at[pl.ds(row0(t) + s * sub, sub)]],
                                         bufs[b].at[pl.ds(s * sub, sub)], gsems[b])

        def write_copy(t, b):
            return pltpu.make_async_copy(bufs[b], o_hbm.at[pl.ds(base + row0(t), win)], wsems[b])

        def start_gathers(t, b):
            for s in range(win // sub):
                gather_copy(t, b, s).start()

        def step(t, b):
            for s in range(win // sub):
                gather_copy(t, b, s).wait()
            write_copy(t, b).start()

            @pl.when(t >= 1)
            def _():
                write_copy(t - 1, (b - 1) % nb).wait()

            @pl.when(t + nb - 1 < per)
            def _():
                start_gathers(t + nb - 1, (b + nb - 1) % nb)

        for t in range(min(nb - 1, per)):
            start_gathers(t, t)

        @pl.loop(0, per // nb)
        def _(p):
            for j in range(nb):
                step(nb * p + j, j)

        for t in range(per - per % nb, per):
            step(t, t % nb)
        write_copy(per - 1, (per - 1) % nb).wait()

    return gather(table, idx)


def _lat_ctx_maps(n_lat, per, ctx_mod_row):
    def mod_row(i):
        return jnp.where(i < n_lat, i // per, ctx_mod_row)

    def lat(i):
        return (jnp.minimum(i, n_lat - 1), 0)

    def ctx(i):
        return (jnp.maximum(i - n_lat, 0), 0)

    return mod_row, lat, ctx


def _norm_router(x, cx, g, mod3, shift_blk, scale_blk, rw_t, rb, rows_per_mod, ctx_mod_row,
                 pending_proj=None, tm=512):
    n, d = x.shape
    n_lat = n // tm
    has_ctx = cx is not None
    assert not (has_ctx and pending_proj is not None)
    ntot = n_lat + (cx.shape[0] // tm if has_ctx else 0)
    mod_row, lat, ctx = _lat_ctx_maps(n_lat, rows_per_mod // tm, ctx_mod_row)
    row_specs = [pl.BlockSpec((tm, d), lat)] + ([pl.BlockSpec((tm, d), ctx)] if has_ctx else [])
    row_args = [x] + ([cx] if has_ctx else [])
    out_shape = [jax.ShapeDtypeStruct((ntot * tm, d // 2), jnp.uint32),
                 jax.ShapeDtypeStruct((SUBLANES, ntot * tm), F32)]
    out_specs = [pl.BlockSpec((tm, d // 2), lambda i: (i, 0)),
                 pl.BlockSpec((SUBLANES, tm), lambda i: (0, i))]
    if pending_proj is not None:
        a, w, gate_blk = pending_proj
        row_specs += [pl.BlockSpec((tm, a.shape[1]), lat),
                      pl.BlockSpec(w.shape, lambda i: (0, 0), pipeline_mode=pl.Buffered(1)),
                      pl.BlockSpec((None, 1, d), lambda i: (mod_row(i), 0, gate_blk))]
        row_args += [a, w, mod3]
        out_shape.append(jax.ShapeDtypeStruct((n, d), F32))
        out_specs.append(pl.BlockSpec((tm, d), lat))
    return pl.pallas_call(
        functools.partial(_norm_router_kernel, tm=tm, n_lat_tiles=n_lat, has_ctx=has_ctx,
                          pending_proj=pending_proj is not None),
        out_shape=tuple(out_shape),
        grid=(ntot,),
        in_specs=row_specs + [
            pl.BlockSpec((1, d), lambda i: (0, 0)),
            pl.BlockSpec((None, 1, d), lambda i: (mod_row(i), 0, shift_blk)),
            pl.BlockSpec((None, 1, d), lambda i: (mod_row(i), 0, scale_blk)),
            pl.BlockSpec((N_EXPERTS, d), lambda i: (0, 0)),
            pl.BlockSpec((N_EXPERTS, 1), lambda i: (0, 0))],
        out_specs=tuple(out_specs),
        scratch_shapes=[pltpu.VMEM((tm, d), BF16), pltpu.VMEM((tm, tm), BF16), pltpu.VMEM((N_EXPERTS, 1), F32)],
        compiler_params=_cparams(("arbitrary",)),
        name="norm_router",
    )(*row_args, g.reshape(1, d), mod3, mod3, rw_t, rb)


def _cast_rows(src_ref, dst_ref, rb=256):
    def body(r, carry):
        rows = pl.ds(pl.multiple_of(r * rb, rb), rb)
        dst_ref[rows, :] = src_ref[0, rows, :].astype(dst_ref.dtype)
        return carry

    lax.fori_loop(0, dst_ref.shape[0] // rb, body, 0)


def _tile_state(te_ref, tv_ref, tile0):
    i = pl.program_id(0)
    t = tile0 + i
    live = tv_ref[t] > 0
    new_expert = (i == 0) | (te_ref[t] != te_ref[jnp.maximum(t - 1, 0)])
    return live, new_expert


def _gmm_kernel(te_ref, tv_ref, xs_ref, wg_ref, wu_ref, wd_ref, *rest, tile0):
    o_ref, wg_b, wu_b, wd_b = rest[-4:]
    live, new_expert = _tile_state(te_ref, tv_ref, tile0)

    @pl.when(live & new_expert)
    def _():
        _cast_rows(wg_ref, wg_b)
        _cast_rows(wu_ref, wu_b)
        _cast_rows(wd_ref, wd_b)

    @pl.when(live)
    def _():
        lo, hi = _unpack_halves(xs_ref[...])
        x = jnp.concatenate([lo.astype(BF16), hi.astype(BF16)], axis=1)
        gate = _dot(x, wg_b[...])
        up = _dot(x, wu_b[...])
        h1 = ((gate * jax.nn.sigmoid(gate)) * up).astype(BF16)
        o_ref[...] = _pack_halves(_dot(h1, wd_b[...]))

    @pl.when(jnp.logical_not(live))
    def _():
        o_ref[...] = jnp.zeros(o_ref.shape, o_ref.dtype)


def _grouped_mlp(tile_expert, tile_valid, xs_part, tile0, ys_prev, wg, wu, wd, layer, tm=MOE_TM):
    dpk = xs_part.shape[1]
    d, dff = wg.shape[2], wg.shape[3]
    n_tiles = tile_expert.shape[0]

    def expert_block(rows, cols):
        return pl.BlockSpec((None, 1, rows, cols), lambda i, te, tv: (layer, te[tile0 + i], 0, 0),
                            pipeline_mode=pl.Buffered(1))

    in_specs = [pl.BlockSpec((tm, dpk), lambda i, te, tv: (i, 0)),
                expert_block(d, dff), expert_block(d, dff), expert_block(dff, d)]
    args = [tile_expert, tile_valid, xs_part, wg, wu, wd]
    aliases = {}
    if ys_prev is not None:
        in_specs.append(pl.BlockSpec(memory_space=pl.ANY))
        args.append(ys_prev)
        aliases = {len(args) - 1: 0}
    grid_spec = pltpu.PrefetchScalarGridSpec(
        num_scalar_prefetch=2,
        grid=(xs_part.shape[0] // tm,),
        in_specs=in_specs,
        out_specs=pl.BlockSpec((tm, dpk), lambda i, te, tv: (tile0 + i, 0)),
        scratch_shapes=[pltpu.VMEM((d, dff), BF16), pltpu.VMEM((d, dff), BF16), pltpu.VMEM((dff, d), BF16)],
    )
    return pl.pallas_call(
        functools.partial(_gmm_kernel, tile0=tile0),
        out_shape=jax.ShapeDtypeStruct((n_tiles * tm, dpk), jnp.uint32),
        grid_spec=grid_spec,
        input_output_aliases=aliases,
        compiler_params=_cparams(("arbitrary",)),
        name="grouped_mlp",
    )(*args)


def _combine_kernel(*refs, n_lat_tiles, has_ctx):
    if has_ctx:
        x_ref, xc_ref, y0_ref, y1_ref, r_ref, g_ref, o_ref, oc_ref = refs
    else:
        x_ref, y0_ref, y1_ref, r_ref, g_ref, o_ref = refs
    w0 = r_ref[:, 2:3]
    w1 = r_ref[:, 3:4]
    lo0, hi0 = _unpack_halves(y0_ref[...])
    lo1, hi1 = _unpack_halves(y1_ref[...])
    f = g_ref[...] * jnp.concatenate([w0 * lo0 + w1 * lo1, w0 * hi0 + w1 * hi1], axis=1)
    if not has_ctx:
        o_ref[...] = x_ref[...] + f
        return
    i = pl.program_id(0)

    @pl.when(i < n_lat_tiles)
    def _():
        o_ref[...] = x_ref[...] + f

    @pl.when(i >= n_lat_tiles)
    def _():
        oc_ref[...] = xc_ref[...] + f


def _combine(x, cx, yg, route_cols, mod3, gate_blk, rows_per_mod, ctx_mod_row, tm=512):
    n, d = x.shape
    n_lat = n // tm
    has_ctx = cx is not None
    ntot = n_lat + (cx.shape[0] // tm if has_ctx else 0)
    mod_row, lat, ctx = _lat_ctx_maps(n_lat, rows_per_mod // tm, ctx_mod_row)
    row_specs = [pl.BlockSpec((tm, d), lat)] + ([pl.BlockSpec((tm, d), ctx)] if has_ctx else [])
    row_args = [x] + ([cx] if has_ctx else [])
    out_shape = [jax.ShapeDtypeStruct(x.shape, F32)] + ([jax.ShapeDtypeStruct(cx.shape, F32)] if has_ctx else [])
    out = pl.pallas_call(
        functools.partial(_combine_kernel, n_lat_tiles=n_lat, has_ctx=has_ctx),
        out_shape=tuple(out_shape),
        grid=(ntot,),
        in_specs=row_specs + [
            pl.BlockSpec((tm, d // 2), lambda i: (i, 0)),
            pl.BlockSpec((tm, d // 2), lambda i: (ntot + i, 0)),
            pl.BlockSpec((tm, SUBLANES), lambda i: (i, 0)),
            pl.BlockSpec((None, 1, d), lambda i: (mod_row(i), 0, gate_blk))],
        out_specs=tuple(row_specs),
        compiler_params=_cparams(("arbitrary",)),
        name="moe_combine",
    )(*row_args, yg, yg, route_cols, mod3)
    return out if has_ctx else (out[0], None)


def _dispatch_plan(route, tm):
    n = route.shape[1]
    e_flat = jnp.concatenate([route[0], route[1]]).astype(jnp.int32)
    rank = jnp.concatenate([route[4], route[5]]).astype(jnp.int32)
    n_assign = 2 * n
    n_tiles = n_assign // tm + N_EXPERTS
    experts = jnp.arange(N_EXPERTS, dtype=jnp.int32)[:, None]
    onehot = (experts == e_flat[None, :]).astype(jnp.int32)
    counts = jnp.sum(onehot, axis=1)
    padded = ((counts + tm - 1) // tm) * tm
    ends_p = jnp.cumsum(padded)
    starts_p = ends_p - padded
    starts_c = jnp.cumsum(counts) - counts
    dest = jnp.sum(onehot * starts_p[:, None], axis=0) + rank
    by_expert = jnp.sum(onehot * starts_c[:, None], axis=0) + rank
    order = jnp.argsort(by_expert).astype(jnp.int32)
    p = jnp.arange(n_tiles * tm, dtype=jnp.int32)[None, :]
    owner = ((p >= starts_p[:, None]) & (p < ends_p[:, None])).astype(jnp.int32)
    within = jnp.sum(owner * (p - starts_p[:, None]), axis=0)
    live = jnp.sum(owner * (p - starts_p[:, None] < counts[:, None]), axis=0) > 0
    compact = jnp.sum(owner * starts_c[:, None], axis=0) + within
    src_tok = jnp.where(live, order[jnp.clip(compact, 0, n_assign - 1)] % n, p[0] % n)
    tile_start = jnp.arange(n_tiles, dtype=jnp.int32) * tm
    tile_valid = (tile_start < ends_p[-1]).astype(jnp.int32)
    last_tile = jnp.maximum(ends_p[-1] // tm - 1, 0) * tm
    tile_row = jnp.minimum(tile_start, last_tile)[:, None]
    tile_expert = jnp.sum((ends_p[None, :] <= tile_row).astype(jnp.int32), axis=1)
    tile_expert = jnp.minimum(tile_expert, N_EXPERTS - 1)
    return src_tok, dest, tile_expert, tile_valid


def _moe(x, cx, g, mod3, rw_t, rb, wg, wu, wd, layer, rows_per_mod, ctx_mod_row, combine=True,
         pending_proj=None):
    if pending_proj is None:
        h, route = _norm_router(x, cx, g, mod3, 3, 4, rw_t, rb, rows_per_mod, ctx_mod_row)
    else:
        h, route, x = _norm_router(x, cx, g, mod3, 3, 4, rw_t, rb, rows_per_mod, ctx_mod_row, pending_proj)
    src_tok, dest, tile_expert, tile_valid = _dispatch_plan(route, MOE_TM)
    n_tiles = tile_expert.shape[0]
    bounds = [n_tiles * k // MOE_RANGES for k in range(MOE_RANGES + 1)]
    ys = None
    for t0, t1 in zip(bounds[:-1], bounds[1:]):
        xs = _sc_gather_rows(h, src_tok[t0 * MOE_TM:t1 * MOE_TM])
        ys = _grouped_mlp(tile_expert, tile_valid, xs, t0, ys, wg, wu, wd, layer)
    yg = _sc_gather_rows(ys, dest)
    route_cols = route.T
    if not combine:
        return yg, route_cols
    return _combine(x, cx, yg, route_cols, mod3, 5, rows_per_mod, ctx_mod_row)


def _rope_tables(seq):
    rows = seq // GRID_W
    row = jnp.repeat(jnp.arange(rows, dtype=F32), GRID_W)
    col = jnp.tile(jnp.arange(GRID_W, dtype=F32), rows)
    n_freq = HEAD_DIM // 4
    inv_freq = ROPE_BASE ** (-jnp.arange(n_freq, dtype=F32) / n_freq)
    ang = jnp.concatenate([row[:, None] * inv_freq, col[:, None] * inv_freq], axis=-1)
    cos, sin = jnp.cos(ang), jnp.sin(ang)
    return jnp.concatenate([cos, cos], axis=-1), jnp.concatenate([-sin, sin], axis=-1)


def kernel(x, c, ctx, c_ctx, ada_w, ada_b, norm_mix, norm_ffn, ab_w_in, ab_q_gain, ab_k_gain, ab_sink, ab_conv_w, ab_conv_b, ab_gate_a_w, ab_gate_a_b, ab_gate_x_w, ab_gate_x_b, ab_lru_lambda, ab_w_out, gqa_w_in, gqa_q_gain, gqa_k_gain, gqa_w_out, router_w, router_bias, moe_w_gate, moe_w_up, moe_w_down):
    batch, seq, d = x.shape
    ctx_len = ctx.shape[1]
    depth = ada_w.shape[0]
    assert depth == 2 and batch < SUBLANES
    n_lat = batch * seq
    n_ctx = batch * ctx_len
    ctx_row = batch

    xl = x.reshape(n_lat, d)
    xc = ctx.reshape(n_ctx, d)
    cc = jnp.zeros((SUBLANES, d), F32).at[:batch].set(c).at[ctx_row].set(c_ctx)
    mod = _ada(cc, ada_w, ada_b)
    cos2, sin2 = _rope_tables(seq)
    rw_t = router_w.T.astype(BF16)
    rb = router_bias.reshape(N_EXPERTS, 1).astype(F32)
    experts = (moe_w_gate, moe_w_up, moe_w_down)

    mod3 = mod[0].reshape(SUBLANES, 1, 6 * d)
    w_in = ab_w_in[0].astype(BF16)
    proj = _norm_mod_matmul(xl, norm_mix[0], mod3, 0, 1, w_in, seq, 0,
                            A_Q_HEADS, A_KV_HEADS, ab_q_gain[0], ab_k_gain[0], cos2, sin2)
    proj_c = _norm_mod_matmul(xc, norm_mix[0], mod3, 0, 1, w_in, n_ctx, ctx_row,
                              A_Q_HEADS, A_KV_HEADS, ab_q_gain[0], ab_k_gain[0])
    att = _win_attn(ab_sink[0], proj, proj_c, batch, seq, ctx_len)
    att_c = _ctx_attn(ab_sink[0], proj_c, batch, ctx_len)

    lru_w = ab_conv_w.shape[2]
    c0 = (A_Q_HEADS + 2 * A_KV_HEADS) * HEAD_DIM
    y_p, yc_p = _lru(_to_chunked(proj[:, c0:c0 + lru_w], batch),
                     _to_chunked(proj[:, c0 + lru_w:c0 + 2 * lru_w], batch),
                     _to_chunked(proj_c[:, c0:c0 + lru_w], batch),
                     _to_chunked(proj_c[:, c0 + lru_w:c0 + 2 * lru_w], batch),
                     ab_conv_w[0], ab_conv_b[0], ab_gate_a_w[0], ab_gate_a_b[0],
                     ab_gate_x_w[0], ab_gate_x_b[0], ab_lru_lambda[0])
    w_out = ab_w_out[0].astype(BF16)
    xl = _out_proj(att, 0, _from_chunked(y_p), 0, w_out, xl, mod3, 2, seq, 0)
    xc = _out_proj(att_c, 0, _from_chunked(yc_p), 0, w_out, xc, mod3, 2, n_ctx, ctx_row)
    yg, route_cols = _moe(xl, xc, norm_ffn[0], mod3, rw_t, rb, *experts, 0, seq, ctx_row, combine=False)

    mod3_prev, mod3 = mod3, mod[1].reshape(SUBLANES, 1, 6 * d)
    w_in = gqa_w_in[0].astype(BF16)
    cw = C_Q_HEADS * HEAD_DIM
    proj, xl = _norm_mod_matmul(xl, norm_mix[1], mod3, 0, 1, w_in, seq, 0,
                                C_Q_HEADS, C_KV_HEADS, gqa_q_gain[0], gqa_k_gain[0], cos2, sin2,
                                pending_moe=(yg, route_cols, 0, mod3_prev, 5))
    proj_c, _ = _norm_mod_matmul(xc, norm_mix[1], mod3, 0, 1, w_in[:, cw:], n_ctx, ctx_row,
                                 0, C_KV_HEADS, gqa_q_gain[0], gqa_k_gain[0],
                                 pending_moe=(yg, route_cols, n_lat, mod3_prev, 5))
    att = _dense_attn(proj, proj_c, batch, seq, ctx_len)
    xl, _ = _moe(xl, None, norm_ffn[1], mod3, rw_t, rb, *experts, 1, seq, ctx_row,
                 pending_proj=(att, gqa_w_out[0].astype(BF16), 2))
    return xl.reshape(batch, seq, d)
```

```python
import functools

import jax
import jax.numpy as jnp
import numpy as np
from jax import lax
from jax.experimental import pallas as pl
from jax.experimental.pallas import tpu as pltpu
from jax.experimental.pallas import tpu_sc as plsc

F32 = jnp.float32
BF16 = jnp.bfloat16

LANES = 128
SUBLANES = 8
VMEM_LIMIT = 56 * 1024 * 1024

HEAD_DIM = 128
GRID_W = 64
WINDOW = 128
BLOCK = 128
ROPE_BASE = 10000.0
EPS = 1e-6
ATTN_SCALE = HEAD_DIM ** -0.5
A_Q_HEADS, A_KV_HEADS = 8, 2
C_Q_HEADS, C_KV_HEADS = 16, 4
GQA_GROUP = 4
LRU_C = 8.0
CONV_W = 4
CONV_LEFT = 2
N_EXPERTS = 16
N_GROUPS = 4
EXPERTS_PER_GROUP = 4
NEG_BIG = -1e30

LRU_CHUNKS = SUBLANES
LRU_JB = 16
LRU_LANE_BLOCKS = 2
LRU_SCAN_ROWS = 4
MOE_TM = 256
MOE_RANGES = 2
SC_CORES = 2
SC_WORKERS = 32
SC_GATHER_WINDOW = 32
SC_GATHER_BUFFERS = 3
SC_GATHER_SUB = 8


def _cparams(sem, vmem=VMEM_LIMIT):
    return pltpu.CompilerParams(dimension_semantics=sem, vmem_limit_bytes=vmem)


def _dot(a, b):
    return jnp.dot(a, b, preferred_element_type=F32)


def _dot_nt(a, b):
    return lax.dot_general(a, b, (((1,), (1,)), ((), ())), preferred_element_type=F32)


def _ada_kernel(c_ref, w_ref, b_ref, o_ref):
    c = c_ref[...]
    s = (c * jax.nn.sigmoid(c)).astype(BF16)
    o_ref[0] = _dot(s, w_ref[0].astype(BF16)) + b_ref[0]


def _ada(cc, ada_w, ada_b):
    depth, d, n = ada_w.shape
    tn = 1024
    return pl.pallas_call(
        _ada_kernel,
        out_shape=jax.ShapeDtypeStruct((depth, SUBLANES, n), F32),
        grid=(depth, n // tn),
        in_specs=[pl.BlockSpec((SUBLANES, d), lambda l, j: (0, 0)),
                  pl.BlockSpec((1, d, tn), lambda l, j: (l, 0, j)),
                  pl.BlockSpec((1, 1, tn), lambda l, j: (l, 0, j))],
        out_specs=pl.BlockSpec((1, SUBLANES, tn), lambda l, j: (l, 0, j)),
        compiler_params=_cparams(("arbitrary", "arbitrary")),
        name="ada",
    )(cc, ada_w, ada_b.reshape(depth, 1, n))


def _norm_mod_rows(x_ref, g_ref, sh_ref, sc_ref, dst_ref, tm, rc=128, load_rows=None):
    g = g_ref[...]
    sc1 = 1.0 + sc_ref[...]
    sh = sh_ref[...]

    def body(r, carry):
        rows = pl.ds(pl.multiple_of(r * rc, rc), rc)
        xf = x_ref[rows, :] if load_rows is None else load_rows(rows)
        ms = jnp.mean(xf * xf, axis=-1, keepdims=True)
        xn = (xf * lax.rsqrt(ms + EPS)) * g
        dst_ref[rows, :] = (xn * sc1 + sh).astype(dst_ref.dtype)
        return carry

    lax.fori_loop(0, tm // rc, body, 0)


def _moe_residual_rows(x_ref, y0_ref, y1_ref, r_ref, gate_ref, rows):
    w0 = r_ref[rows, 2:3]
    w1 = r_ref[rows, 3:4]
    lo0, hi0 = _unpack_halves(y0_ref[rows, :])
    lo1, hi1 = _unpack_halves(y1_ref[rows, :])
    f = jnp.concatenate([w0 * lo0 + w1 * lo1, w0 * hi0 + w1 * hi1], axis=1)
    return x_ref[rows, :] + gate_ref[...] * f


def _nm_mm_kernel(*refs, tm, tn, n_q, n_k, rope, pending_moe):
    refs = list(refs)
    x_ref = refs.pop(0)
    if pending_moe:
        y0_ref, y1_ref, r_ref, gate_ref = (refs.pop(0) for _ in range(4))
    g_ref, sh_ref, sc_ref, w_ref, qg_ref, kg_ref = (refs.pop(0) for _ in range(6))
    if rope:
        cos_ref, sin_ref = refs.pop(0), refs.pop(0)
    o_ref = refs.pop(0)
    if pending_moe:
        xnew_ref = refs.pop(0)

        def load_rows(rows):
            xf = _moe_residual_rows(x_ref, y0_ref, y1_ref, r_ref, gate_ref, rows)
            xnew_ref[rows, :] = xf
            return xf
    else:
        load_rows = None
    hn_ref = refs.pop(0)
    _norm_mod_rows(x_ref, g_ref, sh_ref, sc_ref, hn_ref, tm, load_rows=load_rows)
    h = hn_ref[...]
    heads_per_chunk = tn // HEAD_DIM
    for j in range(w_ref.shape[1] // tn):
        cols = slice(j * tn, (j + 1) * tn)
        y = _dot(h, w_ref[:, cols])
        parts = []
        for hh in range(heads_per_chunk):
            head = j * heads_per_chunk + hh
            yh = y[:, _head_cols(hh)]
            if head < n_q + n_k:
                gain = qg_ref[...] if head < n_q else kg_ref[...]
                ms = jnp.mean(yh * yh, axis=-1, keepdims=True)
                yh = (yh * lax.rsqrt(ms + EPS)) * gain
                if rope:
                    yh = yh * cos_ref[...] + pltpu.roll(yh, HEAD_DIM // 2, 1) * sin_ref[...]
                if head < n_q:
                    yh = yh * ATTN_SCALE
            parts.append(yh.astype(o_ref.dtype))
        o_ref[:, cols] = jnp.concatenate(parts, axis=1)


def _norm_mod_matmul(x, g, mod3, shift_blk, scale_blk, w, rows_per_mod, mod_base,
                     n_q, n_k, q_gain, k_gain, cos2=None, sin2=None, pending_moe=None, tm=512, tn=512):
    n, d = x.shape
    nout = w.shape[1]
    tm = min(tm, n)
    tn = min(tn, nout)
    per = rows_per_mod // tm
    rope = cos2 is not None

    def mod_row(i):
        return mod_base + i // per

    head_vec = pl.BlockSpec((1, HEAD_DIM), lambda i: (0, 0))
    in_specs = [pl.BlockSpec((tm, d), lambda i: (i, 0))]
    args = [x]
    out_shape = [jax.ShapeDtypeStruct((n, nout), BF16)]
    out_specs = [pl.BlockSpec((tm, nout), lambda i: (i, 0))]
    if pending_moe is not None:
        yg, route_cols, first_row, mod3_prev, gate_blk = pending_moe
        t0 = first_row // tm
        slot1 = yg.shape[0] // 2 // tm
        in_specs += [pl.BlockSpec((tm, d // 2), lambda i: (t0 + i, 0)),
                     pl.BlockSpec((tm, d // 2), lambda i: (slot1 + t0 + i, 0)),
                     pl.BlockSpec((tm, SUBLANES), lambda i: (t0 + i, 0)),
                     pl.BlockSpec((None, 1, d), lambda i: (mod_row(i), 0, gate_blk))]
        args += [yg, yg, route_cols, mod3_prev]
        out_shape.append(jax.ShapeDtypeStruct((n, d), F32))
        out_specs.append(pl.BlockSpec((tm, d), lambda i: (i, 0)))
    in_specs += [pl.BlockSpec((1, d), lambda i: (0, 0)),
                 pl.BlockSpec((None, 1, d), lambda i: (mod_row(i), 0, shift_blk)),
                 pl.BlockSpec((None, 1, d), lambda i: (mod_row(i), 0, scale_blk)),
                 pl.BlockSpec((d, nout), lambda i: (0, 0), pipeline_mode=pl.Buffered(1)),
                 head_vec, head_vec]
    args += [g.reshape(1, d), mod3, mod3, w, q_gain.reshape(1, HEAD_DIM), k_gain.reshape(1, HEAD_DIM)]
    if rope:
        tiles_per_seq = cos2.shape[0] // tm
        table = pl.BlockSpec((tm, HEAD_DIM), lambda i: (i % tiles_per_seq, 0))
        in_specs += [table, table]
        args += [cos2, sin2]
    out = pl.pallas_call(
        functools.partial(_nm_mm_kernel, tm=tm, tn=tn, n_q=n_q, n_k=n_k, rope=rope,
                          pending_moe=pending_moe is not None),
        out_shape=tuple(out_shape),
        grid=(n // tm,),
        in_specs=in_specs,
        out_specs=tuple(out_specs),
        scratch_shapes=[pltpu.VMEM((tm, d), BF16)],
        compiler_params=_cparams(("parallel",)),
        name="norm_mod_matmul",
    )(*args)
    return out if pending_moe is not None else out[0]


def _head_cols(h):
    return slice(h * HEAD_DIM, (h + 1) * HEAD_DIM)


def _stack_group(q_ref, kvh):
    return jnp.concatenate([q_ref[:, _head_cols(kvh * GQA_GROUP + g)] for g in range(GQA_GROUP)], axis=0)


def _sink_col(sink_ref, kvh, rows):
    return jnp.concatenate([jnp.full((rows, 1), sink_ref[kvh * GQA_GROUP + g], F32)
                            for g in range(GQA_GROUP)], axis=0)


def _band_bias(ctx_len):
    rows, nk = GQA_GROUP * BLOCK, 3 * BLOCK + ctx_len
    qi = np.arange(rows)[:, None] % BLOCK
    kj = np.arange(nk)[None, :]
    inner = (kj >= 3 * BLOCK) | (np.abs(kj - BLOCK - qi) <= WINDOW)
    first = inner & ~(kj < BLOCK)
    last = inner & ~((kj >= 2 * BLOCK) & (kj < 3 * BLOCK))
    return np.where(np.stack([first, inner, last]), 0.0, NEG_BIG).astype(np.float32)


def _win_attn_kernel(sink_ref, bias_ref, q_ref, kp_ref, kc_ref, kn_ref, vp_ref, vc_ref, vn_ref,
                     kx_ref, vx_ref, o_ref, *, ctx_len):
    nk = 3 * BLOCK + ctx_len
    bias = bias_ref[...]
    ones = jnp.ones((nk, HEAD_DIM), BF16)
    for kvh in range(A_KV_HEADS):
        cols = _head_cols(kvh)
        q4 = _stack_group(q_ref, kvh)
        ka = jnp.concatenate([kp_ref[:, cols], kc_ref[:, cols], kn_ref[:, cols], kx_ref[:, cols]], axis=0)
        va = jnp.concatenate([vp_ref[:, cols], vc_ref[:, cols], vn_ref[:, cols], vx_ref[:, cols]], axis=0)
        s = _dot_nt(q4, ka) + bias
        sk = _sink_col(sink_ref, kvh, BLOCK)
        m = jnp.maximum(jnp.max(s, axis=-1, keepdims=True), sk)
        p = jnp.exp((s - m).astype(BF16))
        acc = _dot(p, jnp.concatenate([va, ones], axis=1))
        o = acc[:, 0:HEAD_DIM] / (acc[:, HEAD_DIM:] + jnp.exp(sk - m))
        for g in range(GQA_GROUP):
            o_ref[:, _head_cols(kvh * GQA_GROUP + g)] = o[g * BLOCK:(g + 1) * BLOCK].astype(o_ref.dtype)


def _win_attn(sink, proj, proj_c, batch, seq, ctx_len):
    nb = seq // BLOCK
    assert nb >= 2
    kvw = A_KV_HEADS * HEAD_DIM
    k_blk = A_Q_HEADS * HEAD_DIM // kvw
    v_blk = k_blk + 1
    bias = jnp.asarray(_band_bias(ctx_len))

    def which_bias(b, n):
        return (jnp.where(n == 0, 0, jnp.where(n == nb - 1, 2, 1)), 0, 0)

    def prev(b, n):
        return b * nb + jnp.maximum(n - 1, 0)

    def cur(b, n):
        return b * nb + n

    def nxt(b, n):
        return b * nb + jnp.minimum(n + 1, nb - 1)

    return pl.pallas_call(
        functools.partial(_win_attn_kernel, ctx_len=ctx_len),
        out_shape=jax.ShapeDtypeStruct((batch * seq, A_Q_HEADS * HEAD_DIM), BF16),
        grid=(batch, nb),
        in_specs=[pl.BlockSpec(memory_space=pltpu.SMEM),
                  pl.BlockSpec((None,) + bias.shape[1:], which_bias),
                  pl.BlockSpec((BLOCK, A_Q_HEADS * HEAD_DIM), lambda b, n: (cur(b, n), 0)),
                  pl.BlockSpec((BLOCK, kvw), lambda b, n: (prev(b, n), k_blk)),
                  pl.BlockSpec((BLOCK, kvw), lambda b, n: (cur(b, n), k_blk)),
                  pl.BlockSpec((BLOCK, kvw), lambda b, n: (nxt(b, n), k_blk)),
                  pl.BlockSpec((BLOCK, kvw), lambda b, n: (prev(b, n), v_blk)),
                  pl.BlockSpec((BLOCK, kvw), lambda b, n: (cur(b, n), v_blk)),
                  pl.BlockSpec((BLOCK, kvw), lambda b, n: (nxt(b, n), v_blk)),
                  pl.BlockSpec((ctx_len, kvw), lambda b, n: (b, k_blk)),
                  pl.BlockSpec((ctx_len, kvw), lambda b, n: (b, v_blk))],
        out_specs=pl.BlockSpec((BLOCK, A_Q_HEADS * HEAD_DIM), lambda b, n: (cur(b, n), 0)),
        compiler_params=_cparams(("parallel", "parallel")),
        name="win_attn",
    )(sink, bias, proj, proj, proj, proj, proj, proj, proj, proj_c, proj_c)


def _ctx_attn_kernel(sink_ref, q_ref, k_ref, v_ref, o_ref, *, ctx_len):
    kvh = pl.program_id(1)
    q4 = jnp.concatenate([q_ref[:, _head_cols(g)] for g in range(GQA_GROUP)], axis=0)
    s = _dot_nt(q4, k_ref[...])
    sk = jnp.concatenate([jnp.full((ctx_len, 1), sink_ref[kvh * GQA_GROUP + g], F32)
                          for g in range(GQA_GROUP)], axis=0)
    m = jnp.maximum(jnp.max(s, axis=-1, keepdims=True), sk)
    p = jnp.exp(s - m)
    den = jnp.sum(p, axis=-1, keepdims=True) + jnp.exp(sk - m)
    o = _dot(p.astype(BF16), v_ref[...]) / den
    for g in range(GQA_GROUP):
        o_ref[:, _head_cols(g)] = o[g * ctx_len:(g + 1) * ctx_len].astype(o_ref.dtype)


def _ctx_attn(sink, proj_c, batch, ctx_len):
    k_blk = A_Q_HEADS
    v_blk = A_Q_HEADS + A_KV_HEADS
    gw = GQA_GROUP * HEAD_DIM
    return pl.pallas_call(
        functools.partial(_ctx_attn_kernel, ctx_len=ctx_len),
        out_shape=jax.ShapeDtypeStruct((batch * ctx_len, A_Q_HEADS * HEAD_DIM), BF16),
        grid=(batch, A_KV_HEADS),
        in_specs=[pl.BlockSpec(memory_space=pltpu.SMEM),
                  pl.BlockSpec((ctx_len, gw), lambda b, h: (b, h)),
                  pl.BlockSpec((ctx_len, HEAD_DIM), lambda b, h: (b, k_blk + h)),
                  pl.BlockSpec((ctx_len, HEAD_DIM), lambda b, h: (b, v_blk + h))],
        out_specs=pl.BlockSpec((ctx_len, gw), lambda b, h: (b, h)),
        compiler_params=_cparams(("parallel", "parallel")),
        name="ctx_attn",
    )(sink, proj_c, proj_c, proj_c)


def _sigmoid(x):
    return 0.5 * jnp.tanh(0.5 * x) + 0.5


def _gelu_tanh(x):
    return 0.5 * x * (1.0 + jnp.tanh(0.7978845608028654 * (x + 0.044715 * (x * x * x))))


def _lru_sequence(x_ref, xg_ref, y_ref, xp_ref, a_ref, b_ref, w, init, rows, jb):
    conv_w, conv_b, wa, ba, wx, bx, c_logsig = w
    width = x_ref.shape[-1]
    sub = lax.broadcasted_iota(jnp.int32, (1, SUBLANES, width), 1)

    def block_diag(ub, wd):
        return jnp.concatenate([_dot(ub[:, _head_cols(n)], wd[n]) for n in range(width // LANES)], axis=1)

    def fill(r, carry):
        rr = pl.ds(pl.multiple_of(r * jb, jb), jb)
        xp_ref[pl.ds(pl.multiple_of(r * jb, jb) + CONV_LEFT, jb)] = x_ref[rr].astype(F32)
        return carry

    lax.fori_loop(0, rows // jb, fill, 0)
    tail = x_ref[rows - CONV_LEFT:rows].astype(F32)
    xp_ref[0:CONV_LEFT] = jnp.where(sub == 0, 0.0, pltpu.roll(tail, 1, 1))
    head = x_ref[0:1].astype(F32)
    xp_ref[rows + CONV_LEFT:rows + CONV_LEFT + 1] = jnp.where(
        sub == SUBLANES - 1, 0.0, pltpu.roll(head, SUBLANES - 1, 1))

    def gates(r, carry):
        j0 = pl.multiple_of(r * jb, jb)
        u = conv_b
        for k in range(CONV_W):
            u = u + conv_w[k] * xp_ref[pl.ds(j0 + k, jb)]
        u2 = u.reshape(jb * SUBLANES, width)
        ub = u2.astype(BF16)
        for d in range(2):
            r_gate = _sigmoid(block_diag(ub, wa[d]) + ba[d])
            i_gate = _sigmoid(block_diag(ub, wx[d]) + bx[d])
            log_a = c_logsig[d] * r_gate
            a = jnp.exp(log_a)
            v = 1.0 - a * a
            root = jnp.where(v > 0.0, v * lax.rsqrt(v), 0.0)
            b = root * (i_gate * u2)
            a_ref[d, pl.ds(j0, jb)] = a.reshape(jb, SUBLANES, width)
            b_ref[d, pl.ds(j0, jb)] = b.reshape(jb, SUBLANES, width)
        return carry

    lax.fori_loop(0, rows // jb, gates, 0, unroll=2)

    def scan(it, carry):
        hf, pf, hb, pb = carry
        fwd = pl.ds(pl.multiple_of(it * LRU_SCAN_ROWS, LRU_SCAN_ROWS), LRU_SCAN_ROWS)
        bwd = pl.ds(pl.multiple_of(rows - LRU_SCAN_ROWS - it * LRU_SCAN_ROWS, LRU_SCAN_ROWS), LRU_SCAN_ROWS)
        af, bf = a_ref[0, fwd], b_ref[0, fwd]
        ab, bb = a_ref[1, bwd], b_ref[1, bwd]
        hs_f, ps_f = [], []
        hs_b, ps_b = [None] * LRU_SCAN_ROWS, [None] * LRU_SCAN_ROWS
        for r in range(LRU_SCAN_ROWS):
            hf = af[r] * hf + bf[r]
            pf = pf * af[r]
            hs_f.append(hf)
            ps_f.append(pf)
            rr = LRU_SCAN_ROWS - 1 - r
            hb = ab[rr] * hb + bb[rr]
            pb = pb * ab[rr]
            hs_b[rr] = hb
            ps_b[rr] = pb
        b_ref[0, fwd] = jnp.stack(hs_f)
        a_ref[0, fwd] = jnp.stack(ps_f)
        b_ref[1, bwd] = jnp.stack(hs_b)
        a_ref[1, bwd] = jnp.stack(ps_b)
        return hf, pf, hb, pb

    z = jnp.zeros((SUBLANES, width), F32)
    o = jnp.ones((SUBLANES, width), F32)
    lax.fori_loop(0, rows // LRU_SCAN_ROWS, scan, (z, o, z, o), unroll=2)

    hf_last, pf_last = b_ref[0, rows - 1], a_ref[0, rows - 1]
    hb_last, pb_last = b_ref[1, 0], a_ref[1, 0]
    s = init[0]
    carry_f = []
    for c in range(SUBLANES):
        carry_f.append(s)
        s = hf_last[c:c + 1] + pf_last[c:c + 1] * s
    out_f = s
    s = init[1]
    carry_b = [None] * SUBLANES
    for c in reversed(range(SUBLANES)):
        carry_b[c] = s
        s = hb_last[c:c + 1] + pb_last[c:c + 1] * s
    out_b = s
    cf = jnp.concatenate(carry_f, axis=0)
    cb = jnp.concatenate(carry_b, axis=0)

    def emit(r, carry):
        rr = pl.ds(pl.multiple_of(r * jb, jb), jb)
        h = (b_ref[0, rr] + a_ref[0, rr] * cf) + (b_ref[1, rr] + a_ref[1, rr] * cb)
        y_ref[rr] = (h * _gelu_tanh(xg_ref[rr].astype(F32))).astype(y_ref.dtype)
        return carry

    lax.fori_loop(0, rows // jb, emit, 0)
    return out_f, out_b


def _lru_kernel(xr_ref, xg_ref, xrc_ref, xgc_ref, cw_ref, cb_ref, wa_ref, ba_ref, wx_ref, bx_ref,
                lam_ref, y_ref, yc_ref, xp_ref, a_ref, b_ref, *, rows, rows_c):
    c_logsig = [LRU_C * jax.nn.log_sigmoid(lam_ref[d]) for d in range(2)]
    w = ([cw_ref[k] for k in range(CONV_W)], cb_ref[0],
         [wa_ref[d] for d in range(2)], [ba_ref[d] for d in range(2)],
         [wx_ref[d] for d in range(2)], [bx_ref[d] for d in range(2)], c_logsig)
    zero = jnp.zeros((1, xr_ref.shape[-1]), F32)
    sf, sb = _lru_sequence(xrc_ref, xgc_ref, yc_ref, xp_ref, a_ref, b_ref, w, (zero, zero),
                           rows_c, min(LRU_JB, rows_c))
    _lru_sequence(xr_ref, xg_ref, y_ref, xp_ref, a_ref, b_ref, w, (sf, sb), rows, LRU_JB)


def _lru(xr, xg, xrc, xgc, conv_w, conv_b, wa, ba, wx, bx, lam):
    batch, rows, _, width = xr.shape
    rows_c = xrc.shape[1]
    cw = LRU_LANE_BLOCKS * LANES
    seq_spec = pl.BlockSpec((None, rows, SUBLANES, cw), lambda b, n: (b, 0, 0, n))
    ctx_spec = pl.BlockSpec((None, rows_c, SUBLANES, cw), lambda b, n: (b, 0, 0, n))
    vec2 = pl.BlockSpec((2, 1, cw), lambda b, n: (0, 0, n))
    mat2 = pl.BlockSpec((2, LRU_LANE_BLOCKS, LANES, LANES), lambda b, n: (0, n, 0, 0))
    return pl.pallas_call(
        functools.partial(_lru_kernel, rows=rows, rows_c=rows_c),
        out_shape=(jax.ShapeDtypeStruct(xr.shape, BF16), jax.ShapeDtypeStruct(xrc.shape, BF16)),
        grid=(batch, width // cw),
        in_specs=[seq_spec, seq_spec, ctx_spec, ctx_spec,
                  pl.BlockSpec((CONV_W, 1, cw), lambda b, n: (0, 0, n)),
                  pl.BlockSpec((1, 1, cw), lambda b, n: (0, 0, n)),
                  mat2, vec2, mat2, vec2, vec2],
        out_specs=(seq_spec, ctx_spec),
        scratch_shapes=[pltpu.VMEM((rows + CONV_W - 1, SUBLANES, cw), F32),
                        pltpu.VMEM((2, rows, SUBLANES, cw), F32),
                        pltpu.VMEM((2, rows, SUBLANES, cw), F32)],
        compiler_params=_cparams(("parallel", "parallel")),
        name="rglru",
    )(xr, xg, xrc, xgc, conv_w.reshape(CONV_W, 1, width), conv_b.reshape(1, 1, width),
      wa.astype(BF16), ba.reshape(2, 1, width), wx.astype(BF16), bx.reshape(2, 1, width),
      lam.reshape(2, 1, width))


def _to_chunked(a, batch):
    t = a.shape[0] // batch
    return a.reshape(batch, LRU_CHUNKS, t // LRU_CHUNKS, a.shape[1]).transpose(0, 2, 1, 3)


def _from_chunked(a):
    b, r, c, w = a.shape
    return a.transpose(0, 2, 1, 3).reshape(b * r * c, w)


def _out_proj_kernel(a1_ref, a2_ref, w1_ref, w2_ref, x_ref, g_ref, o_ref):
    y = _dot(a1_ref[...], w1_ref[...]) + _dot(a2_ref[...], w2_ref[...])
    o_ref[...] = x_ref[...] + g_ref[...] * y


def _out_proj(a1, a1_blk, a2, a2_blk, w, x, mod3, gate_blk, rows_per_mod, mod_base, tm=2048, tn=512):
    n, d = x.shape
    kh = w.shape[0] // 2
    tm = min(tm, n)
    per = rows_per_mod // tm
    gpb = d // tn
    return pl.pallas_call(
        _out_proj_kernel,
        out_shape=jax.ShapeDtypeStruct((n, d), F32),
        grid=(n // tm, d // tn),
        in_specs=[pl.BlockSpec((tm, kh), lambda i, j: (i, a1_blk)),
                  pl.BlockSpec((tm, kh), lambda i, j: (i, a2_blk)),
                  pl.BlockSpec((kh, tn), lambda i, j: (0, j)),
                  pl.BlockSpec((kh, tn), lambda i, j: (1, j)),
                  pl.BlockSpec((tm, tn), lambda i, j: (i, j)),
                  pl.BlockSpec((None, 1, tn), lambda i, j: (mod_base + i // per, 0, gate_blk * gpb + j))],
        out_specs=pl.BlockSpec((tm, tn), lambda i, j: (i, j)),
        compiler_params=_cparams(("parallel", "parallel")),
        name="out_proj",
    )(a1, a2, w, w, x, mod3)


def _lane_block_max(s):
    mm = s[:, 0:LANES]
    for t in range(1, s.shape[1] // LANES):
        mm = jnp.maximum(mm, s[:, t * LANES:(t + 1) * LANES])
    return mm


def _exp_blocks(s, mrep):
    return jnp.concatenate(
        [jnp.exp((s[:, t * LANES:(t + 1) * LANES] - mrep).astype(BF16)) for t in range(s.shape[1] // LANES)],
        axis=1)


def _copy_key_rows(dst_ref, c, kc, lat_ref, ctx_ref, cols):
    seq = lat_ref.shape[0]
    lo, hi = c * kc, (c + 1) * kc
    if lo < seq:
        n = min(hi, seq) - lo
        dst_ref[c, 0:n, cols] = lat_ref[lo:lo + n, :]
    if hi > seq:
        start = max(lo, seq)
        dst_ref[c, start - lo:kc, cols] = ctx_ref[start - seq:hi - seq, :]


def _dense_attn_kernel(q_ref, k_ref, v_ref, kx_ref, vx_ref, o_ref,
                       s_ref, m_ref, acc_ref, ka_ref, va_ref, *, tq, n_chunks, kc):
    @pl.when(pl.program_id(2) == 0)
    def _():
        for c in range(n_chunks):
            _copy_key_rows(ka_ref, c, kc, k_ref, kx_ref, slice(0, HEAD_DIM))
            _copy_key_rows(va_ref, c, kc, v_ref, vx_ref, slice(0, HEAD_DIM))
            va_ref[c, :, HEAD_DIM:] = jnp.ones((kc, HEAD_DIM), BF16)

    q4 = jnp.concatenate([q_ref[:, _head_cols(g)] for g in range(GQA_GROUP)], axis=0)
    m_ref[...] = jnp.full(m_ref.shape, NEG_BIG, F32)

    def sweep1(c, carry):
        s = _dot_nt(q4, ka_ref[c])
        s_ref[c] = s
        m_ref[...] = jnp.maximum(m_ref[...], _lane_block_max(s))
        return carry

    lax.fori_loop(0, n_chunks, sweep1, 0)
    m_ref[...] = jnp.broadcast_to(jnp.max(m_ref[...], axis=-1, keepdims=True), m_ref.shape)
    acc_ref[...] = jnp.zeros(acc_ref.shape, F32)

    def sweep2(c, carry):
        acc_ref[...] += _dot(_exp_blocks(s_ref[c], m_ref[...]), va_ref[c])
        return carry

    lax.fori_loop(0, n_chunks, sweep2, 0)
    o = acc_ref[:, 0:HEAD_DIM] / acc_ref[:, HEAD_DIM:]
    for g in range(GQA_GROUP):
        o_ref[:, _head_cols(g)] = o[g * tq:(g + 1) * tq].astype(o_ref.dtype)


def _dense_attn(proj, proj_c, batch, seq, ctx_len, tq=256, n_chunks=2):
    gw = GQA_GROUP * HEAD_DIM
    nq = seq // tq
    rows = GQA_GROUP * tq
    kc = (seq + ctx_len) // n_chunks
    assert kc * n_chunks == seq + ctx_len and kc % LANES == 0
    k_blk = C_Q_HEADS
    v_blk = C_Q_HEADS + C_KV_HEADS
    vx_blk = C_KV_HEADS
    return pl.pallas_call(
        functools.partial(_dense_attn_kernel, tq=tq, n_chunks=n_chunks, kc=kc),
        out_shape=jax.ShapeDtypeStruct((batch * seq, C_Q_HEADS * HEAD_DIM), BF16),
        grid=(batch, C_KV_HEADS, nq),
        in_specs=[pl.BlockSpec((tq, gw), lambda b, h, i: (b * nq + i, h)),
                  pl.BlockSpec((seq, HEAD_DIM), lambda b, h, i: (b, k_blk + h)),
                  pl.BlockSpec((seq, HEAD_DIM), lambda b, h, i: (b, v_blk + h)),
                  pl.BlockSpec((ctx_len, HEAD_DIM), lambda b, h, i: (b, h)),
                  pl.BlockSpec((ctx_len, HEAD_DIM), lambda b, h, i: (b, vx_blk + h))],
        out_specs=pl.BlockSpec((tq, gw), lambda b, h, i: (b * nq + i, h)),
        scratch_shapes=[pltpu.VMEM((n_chunks, rows, kc), F32),
                        pltpu.VMEM((rows, LANES), F32),
                        pltpu.VMEM((rows, 2 * HEAD_DIM), F32),
                        pltpu.VMEM((n_chunks, kc, HEAD_DIM), BF16),
                        pltpu.VMEM((n_chunks, kc, 2 * HEAD_DIM), BF16)],
        compiler_params=_cparams(("parallel", "parallel", "arbitrary")),
        name="dense_attn",
    )(proj, proj, proj, proj_c, proj_c)


def _router_rows(biased, scores):
    v = [biased[e:e + 1, :] for e in range(N_EXPERTS)]
    s = [scores[e:e + 1, :] for e in range(N_EXPERTS)]

    def top2_sum(vals):
        best = vals[0] + vals[1]
        for i in range(len(vals)):
            for j in range(i + 1, len(vals)):
                if (i, j) != (0, 1):
                    best = jnp.maximum(best, vals[i] + vals[j])
        return best

    gsum = [top2_sum(v[g * EXPERTS_PER_GROUP:(g + 1) * EXPERTS_PER_GROUP]) for g in range(N_GROUPS)]
    sel = jnp.zeros_like(gsum[0], dtype=jnp.int32)
    best = gsum[0]
    for g in range(1, N_GROUPS):
        take = gsum[g] > best
        sel = jnp.where(take, g, sel)
        best = jnp.where(take, gsum[g], best)

    def pick_group(rows, i):
        out = rows[i]
        for g in range(1, N_GROUPS):
            out = jnp.where(sel == g, rows[g * EXPERTS_PER_GROUP + i], out)
        return out

    cand = [pick_group(v, i) for i in range(EXPERTS_PER_GROUP)]
    cand_s = [pick_group(s, i) for i in range(EXPERTS_PER_GROUP)]
    i1 = jnp.zeros_like(sel)
    b1 = cand[0]
    for i in range(1, EXPERTS_PER_GROUP):
        take = cand[i] > b1
        i1 = jnp.where(take, i, i1)
        b1 = jnp.where(take, cand[i], b1)
    i2 = jnp.full_like(sel, -1)
    b2 = jnp.full_like(b1, -jnp.inf)
    for i in range(EXPERTS_PER_GROUP):
        take = (i1 != i) & ((cand[i] > b2) | (i2 < 0))
        i2 = jnp.where(take, i, i2)
        b2 = jnp.where(take, cand[i], b2)

    def pick_idx(rows, idx):
        out = rows[0]
        for i in range(1, EXPERTS_PER_GROUP):
            out = jnp.where(idx == i, rows[i], out)
        return out

    s0 = pick_idx(cand_s, i1)
    s1 = pick_idx(cand_s, i2)
    tot = s0 + s1
    e0 = (sel * EXPERTS_PER_GROUP + i1).astype(F32)
    e1 = (sel * EXPERTS_PER_GROUP + i2).astype(F32)
    return e0, e1, s0 / tot, s1 / tot


def _norm_router_kernel(*refs, tm, n_lat_tiles, has_ctx, n_pending):
    i = pl.program_id(0)
    refs = list(refs)
    x_ref = refs.pop(0)
    xc_ref = refs.pop(0) if has_ctx else None
    pending = [(refs.pop(0), refs.pop(0)) for _ in range(n_pending)]
    gate_ref = refs.pop(0) if n_pending else None
    g_ref, sh_ref, sc_ref, rw_ref, rb_ref, hp_ref, r_ref = (refs.pop(0) for _ in range(7))
    xnew_ref = refs.pop(0) if n_pending else None
    h_ref, tri_ref, run_ref = refs

    def latent_rows():
        src_ref = x_ref
        if n_pending:
            tn = 512
            for j in range(x_ref.shape[1] // tn):
                cols = slice(j * tn, (j + 1) * tn)
                y = _dot(pending[0][0][...], pending[0][1][:, cols])
                for a_ref, w_ref in pending[1:]:
                    y = y + _dot(a_ref[...], w_ref[:, cols])
                xnew_ref[:, cols] = x_ref[:, cols] + gate_ref[:, cols] * y
            src_ref = xnew_ref
        _norm_mod_rows(src_ref, g_ref, sh_ref, sc_ref, h_ref, tm)

    if has_ctx:
        pl.when(i < n_lat_tiles)(latent_rows)

        @pl.when(i >= n_lat_tiles)
        def _():
            _norm_mod_rows(xc_ref, g_ref, sh_ref, sc_ref, h_ref, tm)
    else:
        latent_rows()

    @pl.when(i == 0)
    def _():
        run_ref[...] = jnp.zeros(run_ref.shape, F32)
        before = lax.broadcasted_iota(jnp.int32, (tm, tm), 0) <= lax.broadcasted_iota(jnp.int32, (tm, tm), 1)
        tri_ref[...] = jnp.where(before, 1.0, 0.0).astype(BF16)

    def pack(r, carry):
        rows = pl.ds(pl.multiple_of(r * 128, 128), 128)
        hp_ref[rows, :] = _pack_halves(h_ref[rows, :])
        return carry

    lax.fori_loop(0, tm // 128, pack, 0)
    logits = _dot_nt(rw_ref[...], h_ref[...])
    scores = jax.nn.sigmoid(logits)
    e0, e1, w0, w1 = _router_rows(scores + rb_ref[...], scores)
    expert_ids = lax.broadcasted_iota(jnp.int32, (N_EXPERTS, tm), 0).astype(F32)
    ranks = []
    for e_row in (e0, e1):
        hit = expert_ids == e_row
        seen = _dot(jnp.where(hit, 1.0, 0.0).astype(BF16), tri_ref[...])
        ranks.append(jnp.sum(jnp.where(hit, seen - 1.0 + run_ref[...], 0.0), axis=0, keepdims=True))
        run_ref[...] = run_ref[...] + seen[:, tm - 1:tm]
    zero = jnp.zeros_like(w0)
    r_ref[...] = jnp.concatenate([e0, e1, w0, w1, ranks[0], ranks[1], zero, zero], axis=0)


def _pack_halves(v):
    c = v.shape[1] // 2
    lo = lax.bitcast_convert_type(v[:, :c].astype(BF16).astype(F32), jnp.uint32)
    hi = lax.bitcast_convert_type(v[:, c:].astype(BF16).astype(F32), jnp.uint32)
    return (lo >> 16) | (hi & jnp.uint32(0xFFFF0000))


def _unpack_halves(p):
    lo = lax.bitcast_convert_type(p << 16, F32)
    hi = lax.bitcast_convert_type(p & jnp.uint32(0xFFFF0000), F32)
    return lo, hi


def _sc_gather_rows(table, idx):
    n, w = idx.shape[0], table.shape[1]
    win, nb, sub = SC_GATHER_WINDOW, SC_GATHER_BUFFERS, SC_GATHER_SUB
    per = n // (win * SC_WORKERS)
    assert per * win * SC_WORKERS == n and per >= 1
    mesh = plsc.VectorSubcoreMesh(core_axis_name="core", subcore_axis_name="subcore")

    @functools.partial(
        pl.kernel, out_type=jax.ShapeDtypeStruct((n, w), table.dtype), mesh=mesh, name="sc_gather_rows",
        scratch_types=([pltpu.VMEM((per * win,), jnp.int32)] + [pltpu.VMEM((win, w), table.dtype)] * nb
                       + [pltpu.SemaphoreType.DMA] * (2 * nb)))
    def gather(x_hbm, i_hbm, o_hbm, i_v, *rest):
        bufs, gsems, wsems = rest[:nb], rest[nb:2 * nb], rest[2 * nb:]
        wid = lax.axis_index("subcore") * SC_CORES + lax.axis_index("core")
        base = wid * (per * win)
        pltpu.sync_copy(i_hbm.at[pl.ds(base, per * win)], i_v)
        rot = (wid * per) // SC_WORKERS

        def row0(t):
            u = t + rot
            u = u - per * (u >= per).astype(jnp.int32)
            return pl.multiple_of(u * win, win)

        def gather_copy(t, b, s):
            return pltpu.make_async_copy(x_hbm.at[i_v.at[pl.ds(row0(t) + s * sub, sub)]],
                                         bufs[b].at[pl.ds(s * sub, sub)], gsems[b])

        def write_copy(t, b):
            return pltpu.make_async_copy(bufs[b], o_hbm.at[pl.ds(base + row0(t), win)], wsems[b])

        def start_gathers(t, b):
            for s in range(win // sub):
                gather_copy(t, b, s).start()

        def step(t, b):
            for s in range(win // sub):
                gather_copy(t, b, s).wait()
            write_copy(t, b).start()

            @pl.when(t >= 1)
            def _():
                write_copy(t - 1, (b - 1) % nb).wait()

            @pl.when(t + nb - 1 < per)
            def _():
                start_gathers(t + nb - 1, (b + nb - 1) % nb)

        for t in range(min(nb - 1, per)):
            start_gathers(t, t)

        @pl.loop(0, per // nb)
        def _(p):
            for j in range(nb):
                step(nb * p + j, j)

        for t in range(per - per % nb, per):
            step(t, t % nb)
        write_copy(per - 1, (per - 1) % nb).wait()

    return gather(table, idx)


def _lat_ctx_maps(n_lat, per, ctx_mod_row):
    def mod_row(i):
        return jnp.where(i < n_lat, i // per, ctx_mod_row)

    def lat(i):
        return (jnp.minimum(i, n_lat - 1), 0)

    def ctx(i):
        return (jnp.maximum(i - n_lat, 0), 0)

    return mod_row, lat, ctx


def _norm_router(x, cx, g, mod3, shift_blk, scale_blk, rw_t, rb, rows_per_mod, ctx_mod_row,
                 pending_proj=None, tm=512):
    n, d = x.shape
    n_lat = n // tm
    has_ctx = cx is not None
    ntot = n_lat + (cx.shape[0] // tm if has_ctx else 0)
    mod_row, lat, ctx = _lat_ctx_maps(n_lat, rows_per_mod // tm, ctx_mod_row)
    row_specs = [pl.BlockSpec((tm, d), lat)] + ([pl.BlockSpec((tm, d), ctx)] if has_ctx else [])
    row_args = [x] + ([cx] if has_ctx else [])
    out_shape = [jax.ShapeDtypeStruct((ntot * tm, d // 2), jnp.uint32),
                 jax.ShapeDtypeStruct((SUBLANES, ntot * tm), F32)]
    out_specs = [pl.BlockSpec((tm, d // 2), lambda i: (i, 0)),
                 pl.BlockSpec((SUBLANES, tm), lambda i: (0, i))]
    n_pending = 0
    if pending_proj is not None:
        pairs, gate_blk = pending_proj
        n_pending = len(pairs)
        for a, w in pairs:
            row_specs += [pl.BlockSpec((tm, a.shape[1]), lat),
                          pl.BlockSpec(w.shape, lambda i: (0, 0), pipeline_mode=pl.Buffered(1))]
            row_args += [a, w]
        row_specs.append(pl.BlockSpec((None, 1, d), lambda i: (jnp.minimum(i, n_lat - 1) // (rows_per_mod // tm),
                                                                0, gate_blk)))
        row_args.append(mod3)
        out_shape.append(jax.ShapeDtypeStruct((n, d), F32))
        out_specs.append(pl.BlockSpec((tm, d), lat))
    return pl.pallas_call(
        functools.partial(_norm_router_kernel, tm=tm, n_lat_tiles=n_lat, has_ctx=has_ctx,
                          n_pending=n_pending),
        out_shape=tuple(out_shape),
        grid=(ntot,),
        in_specs=row_specs + [
            pl.BlockSpec((1, d), lambda i: (0, 0)),
            pl.BlockSpec((None, 1, d), lambda i: (mod_row(i), 0, shift_blk)),
            pl.BlockSpec((None, 1, d), lambda i: (mod_row(i), 0, scale_blk)),
            pl.BlockSpec((N_EXPERTS, d), lambda i: (0, 0)),
            pl.BlockSpec((N_EXPERTS, 1), lambda i: (0, 0))],
        out_specs=tuple(out_specs),
        scratch_shapes=[pltpu.VMEM((tm, d), BF16), pltpu.VMEM((tm, tm), BF16), pltpu.VMEM((N_EXPERTS, 1), F32)],
        compiler_params=_cparams(("arbitrary",)),
        name="norm_router",
    )(*row_args, g.reshape(1, d), mod3, mod3, rw_t, rb)


def _cast_rows(src_ref, dst_ref, rb=256):
    def body(r, carry):
        rows = pl.ds(pl.multiple_of(r * rb, rb), rb)
        dst_ref[rows, :] = src_ref[0, rows, :].astype(dst_ref.dtype)
        return carry

    lax.fori_loop(0, dst_ref.shape[0] // rb, body, 0)


def _tile_state(te_ref, tv_ref, tile0):
    i = pl.program_id(0)
    t = tile0 + i
    live = tv_ref[t] > 0
    new_expert = (i == 0) | (te_ref[t] != te_ref[jnp.maximum(t - 1, 0)])
    return live, new_expert


def _gmm_kernel(te_ref, tv_ref, xs_ref, wg_ref, wu_ref, wd_ref, *rest, tile0):
    o_ref, wg_b, wu_b, wd_b = rest[-4:]
    live, new_expert = _tile_state(te_ref, tv_ref, tile0)

    @pl.when(live & new_expert)
    def _():
        _cast_rows(wg_ref, wg_b)
        _cast_rows(wu_ref, wu_b)
        _cast_rows(wd_ref, wd_b)

    @pl.when(live)
    def _():
        lo, hi = _unpack_halves(xs_ref[...])
        x = jnp.concatenate([lo.astype(BF16), hi.astype(BF16)], axis=1)
        gate = _dot(x, wg_b[...])
        up = _dot(x, wu_b[...])
        h1 = ((gate * jax.nn.sigmoid(gate)) * up).astype(BF16)
        o_ref[...] = _pack_halves(_dot(h1, wd_b[...]))

    @pl.when(jnp.logical_not(live))
    def _():
        o_ref[...] = jnp.zeros(o_ref.shape, o_ref.dtype)


def _grouped_mlp(tile_expert, tile_valid, xs_part, tile0, ys_prev, wg, wu, wd, layer, tm=MOE_TM):
    dpk = xs_part.shape[1]
    d, dff = wg.shape[2], wg.shape[3]
    n_tiles = tile_expert.shape[0]

    def expert_block(rows, cols):
        return pl.BlockSpec((None, 1, rows, cols), lambda i, te, tv: (layer, te[tile0 + i], 0, 0),
                            pipeline_mode=pl.Buffered(1))

    in_specs = [pl.BlockSpec((tm, dpk), lambda i, te, tv: (i, 0)),
                expert_block(d, dff), expert_block(d, dff), expert_block(dff, d)]
    args = [tile_expert, tile_valid, xs_part, wg, wu, wd]
    aliases = {}
    if ys_prev is not None:
        in_specs.append(pl.BlockSpec(memory_space=pl.ANY))
        args.append(ys_prev)
        aliases = {len(args) - 1: 0}
    grid_spec = pltpu.PrefetchScalarGridSpec(
        num_scalar_prefetch=2,
        grid=(xs_part.shape[0] // tm,),
        in_specs=in_specs,
        out_specs=pl.BlockSpec((tm, dpk), lambda i, te, tv: (tile0 + i, 0)),
        scratch_shapes=[pltpu.VMEM((d, dff), BF16), pltpu.VMEM((d, dff), BF16), pltpu.VMEM((dff, d), BF16)],
    )
    return pl.pallas_call(
        functools.partial(_gmm_kernel, tile0=tile0),
        out_shape=jax.ShapeDtypeStruct((n_tiles * tm, dpk), jnp.uint32),
        grid_spec=grid_spec,
        input_output_aliases=aliases,
        compiler_params=_cparams(("arbitrary",)),
        name="grouped_mlp",
    )(*args)


def _combine_kernel(*refs, n_lat_tiles, has_ctx):
    if has_ctx:
        x_ref, xc_ref, y0_ref, y1_ref, r_ref, g_ref, o_ref, oc_ref = refs
    else:
        x_ref, y0_ref, y1_ref, r_ref, g_ref, o_ref = refs
    w0 = r_ref[:, 2:3]
    w1 = r_ref[:, 3:4]
    lo0, hi0 = _unpack_halves(y0_ref[...])
    lo1, hi1 = _unpack_halves(y1_ref[...])
    f = g_ref[...] * jnp.concatenate([w0 * lo0 + w1 * lo1, w0 * hi0 + w1 * hi1], axis=1)
    if not has_ctx:
        o_ref[...] = x_ref[...] + f
        return
    i = pl.program_id(0)

    @pl.when(i < n_lat_tiles)
    def _():
        o_ref[...] = x_ref[...] + f

    @pl.when(i >= n_lat_tiles)
    def _():
        oc_ref[...] = xc_ref[...] + f


def _combine(x, cx, yg, route_cols, mod3, gate_blk, rows_per_mod, ctx_mod_row, tm=512):
    n, d = x.shape
    n_lat = n // tm
    has_ctx = cx is not None
    ntot = n_lat + (cx.shape[0] // tm if has_ctx else 0)
    mod_row, lat, ctx = _lat_ctx_maps(n_lat, rows_per_mod // tm, ctx_mod_row)
    row_specs = [pl.BlockSpec((tm, d), lat)] + ([pl.BlockSpec((tm, d), ctx)] if has_ctx else [])
    row_args = [x] + ([cx] if has_ctx else [])
    out_shape = [jax.ShapeDtypeStruct(x.shape, F32)] + ([jax.ShapeDtypeStruct(cx.shape, F32)] if has_ctx else [])
    out = pl.pallas_call(
        functools.partial(_combine_kernel, n_lat_tiles=n_lat, has_ctx=has_ctx),
        out_shape=tuple(out_shape),
        grid=(ntot,),
        in_specs=row_specs + [
            pl.BlockSpec((tm, d // 2), lambda i: (i, 0)),
            pl.BlockSpec((tm, d // 2), lambda i: (ntot + i, 0)),
            pl.BlockSpec((tm, SUBLANES), lambda i: (i, 0)),
            pl.BlockSpec((None, 1, d), lambda i: (mod_row(i), 0, gate_blk))],
        out_specs=tuple(row_specs),
        compiler_params=_cparams(("arbitrary",)),
        name="moe_combine",
    )(*row_args, yg, yg, route_cols, mod3)
    return out if has_ctx else (out[0], None)


def _dispatch_plan(route, tm):
    n = route.shape[1]
    e_flat = jnp.concatenate([route[0], route[1]]).astype(jnp.int32)
    rank = jnp.concatenate([route[4], route[5]]).astype(jnp.int32)
    n_assign = 2 * n
    n_tiles = n_assign // tm + N_EXPERTS
    experts = jnp.arange(N_EXPERTS, dtype=jnp.int32)[:, None]
    onehot = (experts == e_flat[None, :]).astype(jnp.int32)
    counts = jnp.sum(onehot, axis=1)
    padded = ((counts + tm - 1) // tm) * tm
    ends_p = jnp.cumsum(padded)
    starts_p = ends_p - padded
    starts_c = jnp.cumsum(counts) - counts
    dest = jnp.sum(onehot * starts_p[:, None], axis=0) + rank
    by_expert = jnp.sum(onehot * starts_c[:, None], axis=0) + rank
    order = jnp.argsort(by_expert).astype(jnp.int32)
    p = jnp.arange(n_tiles * tm, dtype=jnp.int32)[None, :]
    owner = ((p >= starts_p[:, None]) & (p < ends_p[:, None])).astype(jnp.int32)
    within = jnp.sum(owner * (p - starts_p[:, None]), axis=0)
    live = jnp.sum(owner * (p - starts_p[:, None] < counts[:, None]), axis=0) > 0
    compact = jnp.sum(owner * starts_c[:, None], axis=0) + within
    src_tok = jnp.where(live, order[jnp.clip(compact, 0, n_assign - 1)] % n, p[0] % n)
    tile_start = jnp.arange(n_tiles, dtype=jnp.int32) * tm
    tile_valid = (tile_start < ends_p[-1]).astype(jnp.int32)
    last_tile = jnp.maximum(ends_p[-1] // tm - 1, 0) * tm
    tile_row = jnp.minimum(tile_start, last_tile)[:, None]
    tile_expert = jnp.sum((ends_p[None, :] <= tile_row).astype(jnp.int32), axis=1)
    tile_expert = jnp.minimum(tile_expert, N_EXPERTS - 1)
    return src_tok, dest, tile_expert, tile_valid


def _moe(x, cx, g, mod3, rw_t, rb, wg, wu, wd, layer, rows_per_mod, ctx_mod_row, combine=True,
         pending_proj=None):
    if pending_proj is None:
        h, route = _norm_router(x, cx, g, mod3, 3, 4, rw_t, rb, rows_per_mod, ctx_mod_row)
    else:
        h, route, x = _norm_router(x, cx, g, mod3, 3, 4, rw_t, rb, rows_per_mod, ctx_mod_row, pending_proj)
    src_tok, dest, tile_expert, tile_valid = _dispatch_plan(route, MOE_TM)
    n_tiles = tile_expert.shape[0]
    bounds = [n_tiles * k // MOE_RANGES for k in range(MOE_RANGES + 1)]
    ys = None
    for t0, t1 in zip(bounds[:-1], bounds[1:]):
        xs = _sc_gather_rows(h, src_tok[t0 * MOE_TM:t1 * MOE_TM])
        ys = _grouped_mlp(tile_expert, tile_valid, xs, t0, ys, wg, wu, wd, layer)
    yg = _sc_gather_rows(ys, dest)
    route_cols = route.T
    if not combine:
        return yg, route_cols, x
    return _combine(x, cx, yg, route_cols, mod3, 5, rows_per_mod, ctx_mod_row)


def _rope_tables(seq):
    rows = seq // GRID_W
    row = jnp.repeat(jnp.arange(rows, dtype=F32), GRID_W)
    col = jnp.tile(jnp.arange(GRID_W, dtype=F32), rows)
    n_freq = HEAD_DIM // 4
    inv_freq = ROPE_BASE ** (-jnp.arange(n_freq, dtype=F32) / n_freq)
    ang = jnp.concatenate([row[:, None] * inv_freq, col[:, None] * inv_freq], axis=-1)
    cos, sin = jnp.cos(ang), jnp.sin(ang)
    return jnp.concatenate([cos, cos], axis=-1), jnp.concatenate([-sin, sin], axis=-1)


def kernel(x, c, ctx, c_ctx, ada_w, ada_b, norm_mix, norm_ffn, ab_w_in, ab_q_gain, ab_k_gain, ab_sink, ab_conv_w, ab_conv_b, ab_gate_a_w, ab_gate_a_b, ab_gate_x_w, ab_gate_x_b, ab_lru_lambda, ab_w_out, gqa_w_in, gqa_q_gain, gqa_k_gain, gqa_w_out, router_w, router_bias, moe_w_gate, moe_w_up, moe_w_down):
    batch, seq, d = x.shape
    ctx_len = ctx.shape[1]
    depth = ada_w.shape[0]
    assert depth == 2 and batch < SUBLANES
    n_lat = batch * seq
    n_ctx = batch * ctx_len
    ctx_row = batch

    xl = x.reshape(n_lat, d)
    xc = ctx.reshape(n_ctx, d)
    cc = jnp.zeros((SUBLANES, d), F32).at[:batch].set(c).at[ctx_row].set(c_ctx)
    mod = _ada(cc, ada_w, ada_b)
    cos2, sin2 = _rope_tables(seq)
    rw_t = router_w.T.astype(BF16)
    rb = router_bias.reshape(N_EXPERTS, 1).astype(F32)
    experts = (moe_w_gate, moe_w_up, moe_w_down)

    mod3 = mod[0].reshape(SUBLANES, 1, 6 * d)
    w_in = ab_w_in[0].astype(BF16)
    proj = _norm_mod_matmul(xl, norm_mix[0], mod3, 0, 1, w_in, seq, 0,
                            A_Q_HEADS, A_KV_HEADS, ab_q_gain[0], ab_k_gain[0], cos2, sin2)
    proj_c = _norm_mod_matmul(xc, norm_mix[0], mod3, 0, 1, w_in, n_ctx, ctx_row,
                              A_Q_HEADS, A_KV_HEADS, ab_q_gain[0], ab_k_gain[0])
    att = _win_attn(ab_sink[0], proj, proj_c, batch, seq, ctx_len)
    att_c = _ctx_attn(ab_sink[0], proj_c, batch, ctx_len)

    lru_w = ab_conv_w.shape[2]
    c0 = (A_Q_HEADS + 2 * A_KV_HEADS) * HEAD_DIM
    y_p, yc_p = _lru(_to_chunked(proj[:, c0:c0 + lru_w], batch),
                     _to_chunked(proj[:, c0 + lru_w:c0 + 2 * lru_w], batch),
                     _to_chunked(proj_c[:, c0:c0 + lru_w], batch),
                     _to_chunked(proj_c[:, c0 + lru_w:c0 + 2 * lru_w], batch),
                     ab_conv_w[0], ab_conv_b[0], ab_gate_a_w[0], ab_gate_a_b[0],
                     ab_gate_x_w[0], ab_gate_x_b[0], ab_lru_lambda[0])
    w_out = ab_w_out[0].astype(BF16)
    xc = _out_proj(att_c, 0, _from_chunked(yc_p), 0, w_out, xc, mod3, 2, n_ctx, ctx_row)
    half = w_out.shape[0] // 2
    yg, route_cols, xl = _moe(
        xl, xc, norm_ffn[0], mod3, rw_t, rb, *experts, 0, seq, ctx_row, combine=False,
        pending_proj=([(att, w_out[:half]), (_from_chunked(y_p), w_out[half:])], 2))

    mod3_prev, mod3 = mod3, mod[1].reshape(SUBLANES, 1, 6 * d)
    w_in = gqa_w_in[0].astype(BF16)
    cw = C_Q_HEADS * HEAD_DIM
    proj, xl = _norm_mod_matmul(xl, norm_mix[1], mod3, 0, 1, w_in, seq, 0,
                                C_Q_HEADS, C_KV_HEADS, gqa_q_gain[0], gqa_k_gain[0], cos2, sin2,
                                pending_moe=(yg, route_cols, 0, mod3_prev, 5))
    proj_c, _ = _norm_mod_matmul(xc, norm_mix[1], mod3, 0, 1, w_in[:, cw:], n_ctx, ctx_row,
                                 0, C_KV_HEADS, gqa_q_gain[0], gqa_k_gain[0],
                                 pending_moe=(yg, route_cols, n_lat, mod3_prev, 5))
    att = _dense_attn(proj, proj_c, batch, seq, ctx_len)
    xl, _ = _moe(xl, None, norm_ffn[1], mod3, rw_t, rb, *experts, 1, seq, ctx_row,
                 pending_proj=([(att, gqa_w_out[0].astype(BF16))], 2))
    return xl.reshape(batch, seq, d)
```

```python
import functools

import jax
import jax.numpy as jnp
import numpy as np
from jax import lax
from jax.experimental import pallas as pl
from jax.experimental.pallas import tpu as pltpu
from jax.experimental.pallas import tpu_sc as plsc

F32 = jnp.float32
BF16 = jnp.bfloat16

LANES = 128
SUBLANES = 8
VMEM_LIMIT = 56 * 1024 * 1024

HEAD_DIM = 128
GRID_W = 64
WINDOW = 128
BLOCK = 128
ROPE_BASE = 10000.0
EPS = 1e-6
ATTN_SCALE = HEAD_DIM ** -0.5
A_Q_HEADS, A_KV_HEADS = 8, 2
C_Q_HEADS, C_KV_HEADS = 16, 4
GQA_GROUP = 4
LRU_C = 8.0
CONV_W = 4
CONV_LEFT = 2
N_EXPERTS = 16
N_GROUPS = 4
EXPERTS_PER_GROUP = 4
NEG_BIG = -1e30

LRU_CHUNKS = SUBLANES
LRU_JB = 16
LRU_LANE_BLOCKS = 2
LRU_SCAN_ROWS = 4
MOE_TM = 256
MOE_RANGES = 2
SC_CORES = 2
SC_WORKERS = 32
SC_GATHER_WINDOW = 32
SC_GATHER_BUFFERS = 3
SC_GATHER_SUB = 8


def _cparams(sem, vmem=VMEM_LIMIT):
    return pltpu.CompilerParams(dimension_semantics=sem, vmem_limit_bytes=vmem)


def _dot(a, b):
    return jnp.dot(a, b, preferred_element_type=F32)


def _dot_nt(a, b):
    return lax.dot_general(a, b, (((1,), (1,)), ((), ())), preferred_element_type=F32)


def _ada_kernel(c_ref, w_ref, b_ref, o_ref):
    c = c_ref[...]
    s = (c * jax.nn.sigmoid(c)).astype(BF16)
    o_ref[0] = _dot(s, w_ref[0].astype(BF16)) + b_ref[0]


def _ada(cc, ada_w, ada_b):
    depth, d, n = ada_w.shape
    tn = 1024
    return pl.pallas_call(
        _ada_kernel,
        out_shape=jax.ShapeDtypeStruct((depth, SUBLANES, n), F32),
        grid=(depth, n // tn),
        in_specs=[pl.BlockSpec((SUBLANES, d), lambda l, j: (0, 0)),
                  pl.BlockSpec((1, d, tn), lambda l, j: (l, 0, j)),
                  pl.BlockSpec((1, 1, tn), lambda l, j: (l, 0, j))],
        out_specs=pl.BlockSpec((1, SUBLANES, tn), lambda l, j: (l, 0, j)),
        compiler_params=_cparams(("arbitrary", "arbitrary")),
        name="ada",
    )(cc, ada_w, ada_b.reshape(depth, 1, n))


def _norm_mod_rows(x_ref, g_ref, sh_ref, sc_ref, dst_ref, tm, rc=128, load_rows=None):
    g = g_ref[...]
    sc1 = 1.0 + sc_ref[...]
    sh = sh_ref[...]

    def body(r, carry):
        rows = pl.ds(pl.multiple_of(r * rc, rc), rc)
        xf = x_ref[rows, :] if load_rows is None else load_rows(rows)
        ms = jnp.mean(xf * xf, axis=-1, keepdims=True)
        xn = (xf * lax.rsqrt(ms + EPS)) * g
        dst_ref[rows, :] = (xn * sc1 + sh).astype(dst_ref.dtype)
        return carry

    lax.fori_loop(0, tm // rc, body, 0)


def _moe_residual_rows(x_ref, y0_ref, y1_ref, r_ref, gate_ref, rows):
    w0 = r_ref[rows, 2:3]
    w1 = r_ref[rows, 3:4]
    lo0, hi0 = _unpack_halves(y0_ref[rows, :])
    lo1, hi1 = _unpack_halves(y1_ref[rows, :])
    f = jnp.concatenate([w0 * lo0 + w1 * lo1, w0 * hi0 + w1 * hi1], axis=1)
    return x_ref[rows, :] + gate_ref[...] * f


def _nm_mm_kernel(*refs, tm, tn, n_q, n_k, rope, pending_moe):
    refs = list(refs)
    x_ref = refs.pop(0)
    if pending_moe:
        y0_ref, y1_ref, r_ref, gate_ref = (refs.pop(0) for _ in range(4))
    g_ref, sh_ref, sc_ref, w_ref, qg_ref, kg_ref = (refs.pop(0) for _ in range(6))
    if rope:
        cos_ref, sin_ref = refs.pop(0), refs.pop(0)
    o_ref = refs.pop(0)
    if pending_moe:
        xnew_ref = refs.pop(0)

        def load_rows(rows):
            xf = _moe_residual_rows(x_ref, y0_ref, y1_ref, r_ref, gate_ref, rows)
            xnew_ref[rows, :] = xf
            return xf
    else:
        load_rows = None
    hn_ref = refs.pop(0)
    _norm_mod_rows(x_ref, g_ref, sh_ref, sc_ref, hn_ref, tm, load_rows=load_rows)
    h = hn_ref[...]
    heads_per_chunk = tn // HEAD_DIM
    for j in range(w_ref.shape[1] // tn):
        cols = slice(j * tn, (j + 1) * tn)
        y = _dot(h, w_ref[:, cols])
        parts = []
        for hh in range(heads_per_chunk):
            head = j * heads_per_chunk + hh
            yh = y[:, _head_cols(hh)]
            if head < n_q + n_k:
                gain = qg_ref[...] if head < n_q else kg_ref[...]
                ms = jnp.mean(yh * yh, axis=-1, keepdims=True)
                yh = (yh * lax.rsqrt(ms + EPS)) * gain
                if rope:
                    yh = yh * cos_ref[...] + pltpu.roll(yh, HEAD_DIM // 2, 1) * sin_ref[...]
                if head < n_q:
                    yh = yh * ATTN_SCALE
            parts.append(yh.astype(o_ref.dtype))
        o_ref[:, cols] = jnp.concatenate(parts, axis=1)


def _norm_mod_matmul(x, g, mod3, shift_blk, scale_blk, w, rows_per_mod, mod_base,
                     n_q, n_k, q_gain, k_gain, cos2=None, sin2=None, pending_moe=None, tm=512, tn=512):
    n, d = x.shape
    nout = w.shape[1]
    tm = min(tm, n)
    tn = min(tn, nout)
    per = rows_per_mod // tm
    rope = cos2 is not None

    def mod_row(i):
        return mod_base + i // per

    head_vec = pl.BlockSpec((1, HEAD_DIM), lambda i: (0, 0))
    in_specs = [pl.BlockSpec((tm, d), lambda i: (i, 0))]
    args = [x]
    out_shape = [jax.ShapeDtypeStruct((n, nout), BF16)]
    out_specs = [pl.BlockSpec((tm, nout), lambda i: (i, 0))]
    if pending_moe is not None:
        yg, route_cols, first_row, mod3_prev, gate_blk = pending_moe
        t0 = first_row // tm
        slot1 = yg.shape[0] // 2 // tm
        in_specs += [pl.BlockSpec((tm, d // 2), lambda i: (t0 + i, 0)),
                     pl.BlockSpec((tm, d // 2), lambda i: (slot1 + t0 + i, 0)),
                     pl.BlockSpec((tm, SUBLANES), lambda i: (t0 + i, 0)),
                     pl.BlockSpec((None, 1, d), lambda i: (mod_row(i), 0, gate_blk))]
        args += [yg, yg, route_cols, mod3_prev]
        out_shape.append(jax.ShapeDtypeStruct((n, d), F32))
        out_specs.append(pl.BlockSpec((tm, d), lambda i: (i, 0)))
    in_specs += [pl.BlockSpec((1, d), lambda i: (0, 0)),
                 pl.BlockSpec((None, 1, d), lambda i: (mod_row(i), 0, shift_blk)),
                 pl.BlockSpec((None, 1, d), lambda i: (mod_row(i), 0, scale_blk)),
                 pl.BlockSpec((d, nout), lambda i: (0, 0), pipeline_mode=pl.Buffered(1)),
                 head_vec, head_vec]
    args += [g.reshape(1, d), mod3, mod3, w, q_gain.reshape(1, HEAD_DIM), k_gain.reshape(1, HEAD_DIM)]
    if rope:
        tiles_per_seq = cos2.shape[0] // tm
        table = pl.BlockSpec((tm, HEAD_DIM), lambda i: (i % tiles_per_seq, 0))
        in_specs += [table, table]
        args += [cos2, sin2]
    out = pl.pallas_call(
        functools.partial(_nm_mm_kernel, tm=tm, tn=tn, n_q=n_q, n_k=n_k, rope=rope,
                          pending_moe=pending_moe is not None),
        out_shape=tuple(out_shape),
        grid=(n // tm,),
        in_specs=in_specs,
        out_specs=tuple(out_specs),
        scratch_shapes=[pltpu.VMEM((tm, d), BF16)],
        compiler_params=_cparams(("parallel",)),
        name="norm_mod_matmul",
    )(*args)
    return out if pending_moe is not None else out[0]


def _head_cols(h):
    return slice(h * HEAD_DIM, (h + 1) * HEAD_DIM)


def _stack_group(q_ref, kvh):
    return jnp.concatenate([q_ref[:, _head_cols(kvh * GQA_GROUP + g)] for g in range(GQA_GROUP)], axis=0)


def _sink_col(sink_ref, kvh, rows):
    return jnp.concatenate([jnp.full((rows, 1), sink_ref[kvh * GQA_GROUP + g], F32)
                            for g in range(GQA_GROUP)], axis=0)


def _band_bias(ctx_len):
    rows, nk = GQA_GROUP * BLOCK, 3 * BLOCK + ctx_len
    qi = np.arange(rows)[:, None] % BLOCK
    kj = np.arange(nk)[None, :]
    inner = (kj >= 3 * BLOCK) | (np.abs(kj - BLOCK - qi) <= WINDOW)
    first = inner & ~(kj < BLOCK)
    last = inner & ~((kj >= 2 * BLOCK) & (kj < 3 * BLOCK))
    return np.where(np.stack([first, inner, last]), 0.0, NEG_BIG).astype(np.float32)


def _win_attn_kernel(sink_ref, bias_ref, q_ref, kp_ref, kc_ref, kn_ref, vp_ref, vc_ref, vn_ref,
                     kx_ref, vx_ref, o_ref, *, ctx_len):
    nk = 3 * BLOCK + ctx_len
    bias = bias_ref[...]
    ones = jnp.ones((nk, HEAD_DIM), BF16)
    for kvh in range(A_KV_HEADS):
        cols = _head_cols(kvh)
        q4 = _stack_group(q_ref, kvh)
        ka = jnp.concatenate([kp_ref[:, cols], kc_ref[:, cols], kn_ref[:, cols], kx_ref[:, cols]], axis=0)
        va = jnp.concatenate([vp_ref[:, cols], vc_ref[:, cols], vn_ref[:, cols], vx_ref[:, cols]], axis=0)
        s = _dot_nt(q4, ka) + bias
        sk = _sink_col(sink_ref, kvh, BLOCK)
        m = jnp.maximum(jnp.max(s, axis=-1, keepdims=True), sk)
        p = jnp.exp((s - m).astype(BF16))
        acc = _dot(p, jnp.concatenate([va, ones], axis=1))
        o = acc[:, 0:HEAD_DIM] / (acc[:, HEAD_DIM:] + jnp.exp(sk - m))
        for g in range(GQA_GROUP):
            o_ref[:, _head_cols(kvh * GQA_GROUP + g)] = o[g * BLOCK:(g + 1) * BLOCK].astype(o_ref.dtype)


def _win_attn(sink, proj, proj_c, batch, seq, ctx_len):
    nb = seq // BLOCK
    assert nb >= 2
    kvw = A_KV_HEADS * HEAD_DIM
    k_blk = A_Q_HEADS * HEAD_DIM // kvw
    v_blk = k_blk + 1
    bias = jnp.asarray(_band_bias(ctx_len))

    def which_bias(b, n):
        return (jnp.where(n == 0, 0, jnp.where(n == nb - 1, 2, 1)), 0, 0)

    def prev(b, n):
        return b * nb + jnp.maximum(n - 1, 0)

    def cur(b, n):
        return b * nb + n

    def nxt(b, n):
        return b * nb + jnp.minimum(n + 1, nb - 1)

    return pl.pallas_call(
        functools.partial(_win_attn_kernel, ctx_len=ctx_len),
        out_shape=jax.ShapeDtypeStruct((batch * seq, A_Q_HEADS * HEAD_DIM), BF16),
        grid=(batch, nb),
        in_specs=[pl.BlockSpec(memory_space=pltpu.SMEM),
                  pl.BlockSpec((None,) + bias.shape[1:], which_bias),
                  pl.BlockSpec((BLOCK, A_Q_HEADS * HEAD_DIM), lambda b, n: (cur(b, n), 0)),
                  pl.BlockSpec((BLOCK, kvw), lambda b, n: (prev(b, n), k_blk)),
                  pl.BlockSpec((BLOCK, kvw), lambda b, n: (cur(b, n), k_blk)),
                  pl.BlockSpec((BLOCK, kvw), lambda b, n: (nxt(b, n), k_blk)),
                  pl.BlockSpec((BLOCK, kvw), lambda b, n: (prev(b, n), v_blk)),
                  pl.BlockSpec((BLOCK, kvw), lambda b, n: (cur(b, n), v_blk)),
                  pl.BlockSpec((BLOCK, kvw), lambda b, n: (nxt(b, n), v_blk)),
                  pl.BlockSpec((ctx_len, kvw), lambda b, n: (b, k_blk)),
                  pl.BlockSpec((ctx_len, kvw), lambda b, n: (b, v_blk))],
        out_specs=pl.BlockSpec((BLOCK, A_Q_HEADS * HEAD_DIM), lambda b, n: (cur(b, n), 0)),
        compiler_params=_cparams(("parallel", "parallel")),
        name="win_attn",
    )(sink, bias, proj, proj, proj, proj, proj, proj, proj, proj_c, proj_c)


def _ctx_attn_kernel(sink_ref, q_ref, k_ref, v_ref, o_ref, *, ctx_len):
    kvh = pl.program_id(1)
    q4 = jnp.concatenate([q_ref[:, _head_cols(g)] for g in range(GQA_GROUP)], axis=0)
    s = _dot_nt(q4, k_ref[...])
    sk = jnp.concatenate([jnp.full((ctx_len, 1), sink_ref[kvh * GQA_GROUP + g], F32)
                          for g in range(GQA_GROUP)], axis=0)
    m = jnp.maximum(jnp.max(s, axis=-1, keepdims=True), sk)
    p = jnp.exp(s - m)
    den = jnp.sum(p, axis=-1, keepdims=True) + jnp.exp(sk - m)
    o = _dot(p.astype(BF16), v_ref[...]) / den
    for g in range(GQA_GROUP):
        o_ref[:, _head_cols(g)] = o[g * ctx_len:(g + 1) * ctx_len].astype(o_ref.dtype)


def _ctx_attn(sink, proj_c, batch, ctx_len):
    k_blk = A_Q_HEADS
    v_blk = A_Q_HEADS + A_KV_HEADS
    gw = GQA_GROUP * HEAD_DIM
    return pl.pallas_call(
        functools.partial(_ctx_attn_kernel, ctx_len=ctx_len),
        out_shape=jax.ShapeDtypeStruct((batch * ctx_len, A_Q_HEADS * HEAD_DIM), BF16),
        grid=(batch, A_KV_HEADS),
        in_specs=[pl.BlockSpec(memory_space=pltpu.SMEM),
                  pl.BlockSpec((ctx_len, gw), lambda b, h: (b, h)),
                  pl.BlockSpec((ctx_len, HEAD_DIM), lambda b, h: (b, k_blk + h)),
                  pl.BlockSpec((ctx_len, HEAD_DIM), lambda b, h: (b, v_blk + h))],
        out_specs=pl.BlockSpec((ctx_len, gw), lambda b, h: (b, h)),
        compiler_params=_cparams(("parallel", "parallel")),
        name="ctx_attn",
    )(sink, proj_c, proj_c, proj_c)


def _sigmoid(x):
    return 0.5 * jnp.tanh(0.5 * x) + 0.5


def _gelu_tanh(x):
    return 0.5 * x * (1.0 + jnp.tanh(0.7978845608028654 * (x + 0.044715 * (x * x * x))))


def _lru_sequence(x_ref, xg_ref, y_ref, xp_ref, a_ref, b_ref, w, init, rows, jb):
    conv_w, conv_b, wa, ba, wx, bx, c_logsig = w
    width = x_ref.shape[-1]
    sub = lax.broadcasted_iota(jnp.int32, (1, SUBLANES, width), 1)

    def block_diag(ub, wd):
        return jnp.concatenate([_dot(ub[:, _head_cols(n)], wd[n]) for n in range(width // LANES)], axis=1)

    def fill(r, carry):
        rr = pl.ds(pl.multiple_of(r * jb, jb), jb)
        xp_ref[pl.ds(pl.multiple_of(r * jb, jb) + CONV_LEFT, jb)] = x_ref[rr].astype(F32)
        return carry

    lax.fori_loop(0, rows // jb, fill, 0)
    tail = x_ref[rows - CONV_LEFT:rows].astype(F32)
    xp_ref[0:CONV_LEFT] = jnp.where(sub == 0, 0.0, pltpu.roll(tail, 1, 1))
    head = x_ref[0:1].astype(F32)
    xp_ref[rows + CONV_LEFT:rows + CONV_LEFT + 1] = jnp.where(
        sub == SUBLANES - 1, 0.0, pltpu.roll(head, SUBLANES - 1, 1))

    def gates(r, carry):
        j0 = pl.multiple_of(r * jb, jb)
        u = conv_b
        for k in range(CONV_W):
            u = u + conv_w[k] * xp_ref[pl.ds(j0 + k, jb)]
        u2 = u.reshape(jb * SUBLANES, width)
        ub = u2.astype(BF16)
        for d in range(2):
            r_gate = _sigmoid(block_diag(ub, wa[d]) + ba[d])
            i_gate = _sigmoid(block_diag(ub, wx[d]) + bx[d])
            log_a = c_logsig[d] * r_gate
            a = jnp.exp(log_a)
            v = 1.0 - a * a
            root = jnp.where(v > 0.0, v * lax.rsqrt(v), 0.0)
            b = root * (i_gate * u2)
            a_ref[d, pl.ds(j0, jb)] = a.reshape(jb, SUBLANES, width)
            b_ref[d, pl.ds(j0, jb)] = b.reshape(jb, SUBLANES, width)
        return carry

    lax.fori_loop(0, rows // jb, gates, 0, unroll=2)

    def scan(it, carry):
        hf, pf, hb, pb = carry
        fwd = pl.ds(pl.multiple_of(it * LRU_SCAN_ROWS, LRU_SCAN_ROWS), LRU_SCAN_ROWS)
        bwd = pl.ds(pl.multiple_of(rows - LRU_SCAN_ROWS - it * LRU_SCAN_ROWS, LRU_SCAN_ROWS), LRU_SCAN_ROWS)
        af, bf = a_ref[0, fwd], b_ref[0, fwd]
        ab, bb = a_ref[1, bwd], b_ref[1, bwd]
        hs_f, ps_f = [], []
        hs_b, ps_b = [None] * LRU_SCAN_ROWS, [None] * LRU_SCAN_ROWS
        for r in range(LRU_SCAN_ROWS):
            hf = af[r] * hf + bf[r]
            pf = pf * af[r]
            hs_f.append(hf)
            ps_f.append(pf)
            rr = LRU_SCAN_ROWS - 1 - r
            hb = ab[rr] * hb + bb[rr]
            pb = pb * ab[rr]
            hs_b[rr] = hb
            ps_b[rr] = pb
        b_ref[0, fwd] = jnp.stack(hs_f)
        a_ref[0, fwd] = jnp.stack(ps_f)
        b_ref[1, bwd] = jnp.stack(hs_b)
        a_ref[1, bwd] = jnp.stack(ps_b)
        return hf, pf, hb, pb

    z = jnp.zeros((SUBLANES, width), F32)
    o = jnp.ones((SUBLANES, width), F32)
    lax.fori_loop(0, rows // LRU_SCAN_ROWS, scan, (z, o, z, o), unroll=2)

    hf_last, pf_last = b_ref[0, rows - 1], a_ref[0, rows - 1]
    hb_last, pb_last = b_ref[1, 0], a_ref[1, 0]
    s = init[0]
    carry_f = []
    for c in range(SUBLANES):
        carry_f.append(s)
        s = hf_last[c:c + 1] + pf_last[c:c + 1] * s
    out_f = s
    s = init[1]
    carry_b = [None] * SUBLANES
    for c in reversed(range(SUBLANES)):
        carry_b[c] = s
        s = hb_last[c:c + 1] + pb_last[c:c + 1] * s
    out_b = s
    cf = jnp.concatenate(carry_f, axis=0)
    cb = jnp.concatenate(carry_b, axis=0)

    def emit(r, carry):
        rr = pl.ds(pl.multiple_of(r * jb, jb), jb)
        h = (b_ref[0, rr] + a_ref[0, rr] * cf) + (b_ref[1, rr] + a_ref[1, rr] * cb)
        y_ref[rr] = (h * _gelu_tanh(xg_ref[rr].astype(F32))).astype(y_ref.dtype)
        return carry

    lax.fori_loop(0, rows // jb, emit, 0)
    return out_f, out_b


def _lru_kernel(xr_ref, xg_ref, xrc_ref, xgc_ref, cw_ref, cb_ref, wa_ref, ba_ref, wx_ref, bx_ref,
                lam_ref, y_ref, yc_ref, xp_ref, a_ref, b_ref, *, rows, rows_c):
    c_logsig = [LRU_C * jax.nn.log_sigmoid(lam_ref[d]) for d in range(2)]
    w = ([cw_ref[k] for k in range(CONV_W)], cb_ref[0],
         [wa_ref[d] for d in range(2)], [ba_ref[d] for d in range(2)],
         [wx_ref[d] for d in range(2)], [bx_ref[d] for d in range(2)], c_logsig)
    zero = jnp.zeros((1, xr_ref.shape[-1]), F32)
    sf, sb = _lru_sequence(xrc_ref, xgc_ref, yc_ref, xp_ref, a_ref, b_ref, w, (zero, zero),
                           rows_c, min(LRU_JB, rows_c))
    _lru_sequence(xr_ref, xg_ref, y_ref, xp_ref, a_ref, b_ref, w, (sf, sb), rows, LRU_JB)


def _lru(xr, xg, xrc, xgc, conv_w, conv_b, wa, ba, wx, bx, lam):
    batch, rows, _, width = xr.shape
    rows_c = xrc.shape[1]
    cw = LRU_LANE_BLOCKS * LANES
    seq_spec = pl.BlockSpec((None, rows, SUBLANES, cw), lambda b, n: (b, 0, 0, n))
    ctx_spec = pl.BlockSpec((None, rows_c, SUBLANES, cw), lambda b, n: (b, 0, 0, n))
    vec2 = pl.BlockSpec((2, 1, cw), lambda b, n: (0, 0, n))
    mat2 = pl.BlockSpec((2, LRU_LANE_BLOCKS, LANES, LANES), lambda b, n: (0, n, 0, 0))
    return pl.pallas_call(
        functools.partial(_lru_kernel, rows=rows, rows_c=rows_c),
        out_shape=(jax.ShapeDtypeStruct(xr.shape, BF16), jax.ShapeDtypeStruct(xrc.shape, BF16)),
        grid=(batch, width // cw),
        in_specs=[seq_spec, seq_spec, ctx_spec, ctx_spec,
                  pl.BlockSpec((CONV_W, 1, cw), lambda b, n: (0, 0, n)),
                  pl.BlockSpec((1, 1, cw), lambda b, n: (0, 0, n)),
                  mat2, vec2, mat2, vec2, vec2],
        out_specs=(seq_spec, ctx_spec),
        scratch_shapes=[pltpu.VMEM((rows + CONV_W - 1, SUBLANES, cw), F32),
                        pltpu.VMEM((2, rows, SUBLANES, cw), F32),
                        pltpu.VMEM((2, rows, SUBLANES, cw), F32)],
        compiler_params=_cparams(("parallel", "parallel")),
        name="rglru",
    )(xr, xg, xrc, xgc, conv_w.reshape(CONV_W, 1, width), conv_b.reshape(1, 1, width),
      wa.astype(BF16), ba.reshape(2, 1, width), wx.astype(BF16), bx.reshape(2, 1, width),
      lam.reshape(2, 1, width))


def _to_chunked(a, batch):
    t = a.shape[0] // batch
    return a.reshape(batch, LRU_CHUNKS, t // LRU_CHUNKS, a.shape[1]).transpose(0, 2, 1, 3)


def _from_chunked(a):
    b, r, c, w = a.shape
    return a.transpose(0, 2, 1, 3).reshape(b * r * c, w)


def _out_proj_kernel(a1_ref, a2_ref, w1_ref, w2_ref, x_ref, g_ref, o_ref):
    y = _dot(a1_ref[...], w1_ref[...]) + _dot(a2_ref[...], w2_ref[...])
    o_ref[...] = x_ref[...] + g_ref[...] * y


def _out_proj(a1, a1_blk, a2, a2_blk, w, x, mod3, gate_blk, rows_per_mod, mod_base, tm=2048, tn=512):
    n, d = x.shape
    kh = w.shape[0] // 2
    tm = min(tm, n)
    per = rows_per_mod // tm
    gpb = d // tn
    return pl.pallas_call(
        _out_proj_kernel,
        out_shape=jax.ShapeDtypeStruct((n, d), F32),
        grid=(n // tm, d // tn),
        in_specs=[pl.BlockSpec((tm, kh), lambda i, j: (i, a1_blk)),
                  pl.BlockSpec((tm, kh), lambda i, j: (i, a2_blk)),
                  pl.BlockSpec((kh, tn), lambda i, j: (0, j)),
                  pl.BlockSpec((kh, tn), lambda i, j: (1, j)),
                  pl.BlockSpec((tm, tn), lambda i, j: (i, j)),
                  pl.BlockSpec((None, 1, tn), lambda i, j: (mod_base + i // per, 0, gate_blk * gpb + j))],
        out_specs=pl.BlockSpec((tm, tn), lambda i, j: (i, j)),
        compiler_params=_cparams(("parallel", "parallel")),
        name="out_proj",
    )(a1, a2, w, w, x, mod3)


def _lane_block_max(s):
    mm = s[:, 0:LANES]
    for t in range(1, s.shape[1] // LANES):
        mm = jnp.maximum(mm, s[:, t * LANES:(t + 1) * LANES])
    return mm


def _exp_blocks(s, mrep):
    return jnp.concatenate(
        [jnp.exp((s[:, t * LANES:(t + 1) * LANES] - mrep).astype(BF16)) for t in range(s.shape[1] // LANES)],
        axis=1)


def _copy_key_rows(dst_ref, c, kc, lat_ref, ctx_ref, cols):
    seq = lat_ref.shape[0]
    lo, hi = c * kc, (c + 1) * kc
    if lo < seq:
        n = min(hi, seq) - lo
        dst_ref[c, 0:n, cols] = lat_ref[lo:lo + n, :]
    if hi > seq:
        start = max(lo, seq)
        dst_ref[c, start - lo:kc, cols] = ctx_ref[start - seq:hi - seq, :]


def _dense_attn_kernel(q_ref, k_ref, v_ref, kx_ref, vx_ref, o_ref,
                       s_ref, m_ref, acc_ref, ka_ref, va_ref, *, tq, n_chunks, kc):
    @pl.when(pl.program_id(2) == 0)
    def _():
        for c in range(n_chunks):
            _copy_key_rows(ka_ref, c, kc, k_ref, kx_ref, slice(0, HEAD_DIM))
            _copy_key_rows(va_ref, c, kc, v_ref, vx_ref, slice(0, HEAD_DIM))
            va_ref[c, :, HEAD_DIM:] = jnp.ones((kc, HEAD_DIM), BF16)

    q4 = jnp.concatenate([q_ref[:, _head_cols(g)] for g in range(GQA_GROUP)], axis=0)
    m_ref[...] = jnp.full(m_ref.shape, NEG_BIG, F32)

    def sweep1(c, carry):
        s = _dot_nt(q4, ka_ref[c])
        s_ref[c] = s
        m_ref[...] = jnp.maximum(m_ref[...], _lane_block_max(s))
        return carry

    lax.fori_loop(0, n_chunks, sweep1, 0)
    m_ref[...] = jnp.broadcast_to(jnp.max(m_ref[...], axis=-1, keepdims=True), m_ref.shape)
    acc_ref[...] = jnp.zeros(acc_ref.shape, F32)

    def sweep2(c, carry):
        acc_ref[...] += _dot(_exp_blocks(s_ref[c], m_ref[...]), va_ref[c])
        return carry

    lax.fori_loop(0, n_chunks, sweep2, 0)
    o = acc_ref[:, 0:HEAD_DIM] / acc_ref[:, HEAD_DIM:]
    for g in range(GQA_GROUP):
        o_ref[:, _head_cols(g)] = o[g * tq:(g + 1) * tq].astype(o_ref.dtype)


def _dense_attn(proj, proj_c, batch, seq, ctx_len, tq=256, n_chunks=2):
    gw = GQA_GROUP * HEAD_DIM
    nq = seq // tq
    rows = GQA_GROUP * tq
    kc = (seq + ctx_len) // n_chunks
    assert kc * n_chunks == seq + ctx_len and kc % LANES == 0
    k_blk = C_Q_HEADS
    v_blk = C_Q_HEADS + C_KV_HEADS
    vx_blk = C_KV_HEADS
    return pl.pallas_call(
        functools.partial(_dense_attn_kernel, tq=tq, n_chunks=n_chunks, kc=kc),
        out_shape=jax.ShapeDtypeStruct((batch * seq, C_Q_HEADS * HEAD_DIM), BF16),
        grid=(batch, C_KV_HEADS, nq),
        in_specs=[pl.BlockSpec((tq, gw), lambda b, h, i: (b * nq + i, h)),
                  pl.BlockSpec((seq, HEAD_DIM), lambda b, h, i: (b, k_blk + h)),
                  pl.BlockSpec((seq, HEAD_DIM), lambda b, h, i: (b, v_blk + h)),
                  pl.BlockSpec((ctx_len, HEAD_DIM), lambda b, h, i: (b, h)),
                  pl.BlockSpec((ctx_len, HEAD_DIM), lambda b, h, i: (b, vx_blk + h))],
        out_specs=pl.BlockSpec((tq, gw), lambda b, h, i: (b * nq + i, h)),
        scratch_shapes=[pltpu.VMEM((n_chunks, rows, kc), F32),
                        pltpu.VMEM((rows, LANES), F32),
                        pltpu.VMEM((rows, 2 * HEAD_DIM), F32),
                        pltpu.VMEM((n_chunks, kc, HEAD_DIM), BF16),
                        pltpu.VMEM((n_chunks, kc, 2 * HEAD_DIM), BF16)],
        compiler_params=_cparams(("parallel", "parallel", "arbitrary")),
        name="dense_attn",
    )(proj, proj, proj, proj_c, proj_c)


def _router_rows(biased, scores):
    v = [biased[e:e + 1, :] for e in range(N_EXPERTS)]
    s = [scores[e:e + 1, :] for e in range(N_EXPERTS)]

    def top2_sum(vals):
        best = vals[0] + vals[1]
        for i in range(len(vals)):
            for j in range(i + 1, len(vals)):
                if (i, j) != (0, 1):
                    best = jnp.maximum(best, vals[i] + vals[j])
        return best

    gsum = [top2_sum(v[g * EXPERTS_PER_GROUP:(g + 1) * EXPERTS_PER_GROUP]) for g in range(N_GROUPS)]
    sel = jnp.zeros_like(gsum[0], dtype=jnp.int32)
    best = gsum[0]
    for g in range(1, N_GROUPS):
        take = gsum[g] > best
        sel = jnp.where(take, g, sel)
        best = jnp.where(take, gsum[g], best)

    def pick_group(rows, i):
        out = rows[i]
        for g in range(1, N_GROUPS):
            out = jnp.where(sel == g, rows[g * EXPERTS_PER_GROUP + i], out)
        return out

    cand = [pick_group(v, i) for i in range(EXPERTS_PER_GROUP)]
    cand_s = [pick_group(s, i) for i in range(EXPERTS_PER_GROUP)]
    i1 = jnp.zeros_like(sel)
    b1 = cand[0]
    for i in range(1, EXPERTS_PER_GROUP):
        take = cand[i] > b1
        i1 = jnp.where(take, i, i1)
        b1 = jnp.where(take, cand[i], b1)
    i2 = jnp.full_like(sel, -1)
    b2 = jnp.full_like(b1, -jnp.inf)
    for i in range(EXPERTS_PER_GROUP):
        take = (i1 != i) & ((cand[i] > b2) | (i2 < 0))
        i2 = jnp.where(take, i, i2)
        b2 = jnp.where(take, cand[i], b2)

    def pick_idx(rows, idx):
        out = rows[0]
        for i in range(1, EXPERTS_PER_GROUP):
            out = jnp.where(idx == i, rows[i], out)
        return out

    s0 = pick_idx(cand_s, i1)
    s1 = pick_idx(cand_s, i2)
    tot = s0 + s1
    e0 = (sel * EXPERTS_PER_GROUP + i1).astype(F32)
    e1 = (sel * EXPERTS_PER_GROUP + i2).astype(F32)
    return e0, e1, s0 / tot, s1 / tot


def _norm_router_kernel(*refs, tm, n_lat_tiles, has_ctx, n_pending):
    i = pl.program_id(0)
    refs = list(refs)
    x_ref = refs.pop(0)
    xc_ref = refs.pop(0) if has_ctx else None
    pending = [(refs.pop(0), refs.pop(0)) for _ in range(n_pending)]
    gate_ref = refs.pop(0) if n_pending else None
    g_ref, sh_ref, sc_ref, rw_ref, rb_ref, hp_ref, r_ref = (refs.pop(0) for _ in range(7))
    xnew_ref = refs.pop(0) if n_pending else None
    h_ref, tri_ref, run_ref = refs

    def latent_rows():
        src_ref = x_ref
        if n_pending:
            tn = 512
            for j in range(x_ref.shape[1] // tn):
                cols = slice(j * tn, (j + 1) * tn)
                y = _dot(pending[0][0][...], pending[0][1][:, cols])
                for a_ref, w_ref in pending[1:]:
                    y = y + _dot(a_ref[...], w_ref[:, cols])
                xnew_ref[:, cols] = x_ref[:, cols] + gate_ref[:, cols] * y
            src_ref = xnew_ref
        _norm_mod_rows(src_ref, g_ref, sh_ref, sc_ref, h_ref, tm)

    if has_ctx:
        pl.when(i < n_lat_tiles)(latent_rows)

        @pl.when(i >= n_lat_tiles)
        def _():
            _norm_mod_rows(xc_ref, g_ref, sh_ref, sc_ref, h_ref, tm)
    else:
        latent_rows()

    @pl.when(i == 0)
    def _():
        run_ref[...] = jnp.zeros(run_ref.shape, F32)
        before = lax.broadcasted_iota(jnp.int32, (tm, tm), 0) <= lax.broadcasted_iota(jnp.int32, (tm, tm), 1)
        tri_ref[...] = jnp.where(before, 1.0, 0.0).astype(BF16)

    def pack(r, carry):
        rows = pl.ds(pl.multiple_of(r * 128, 128), 128)
        hp_ref[rows, :] = _pack_halves(h_ref[rows, :])
        return carry

    lax.fori_loop(0, tm // 128, pack, 0)
    logits = _dot_nt(rw_ref[...], h_ref[...])
    scores = jax.nn.sigmoid(logits)
    e0, e1, w0, w1 = _router_rows(scores + rb_ref[...], scores)
    expert_ids = lax.broadcasted_iota(jnp.int32, (N_EXPERTS, tm), 0).astype(F32)
    ranks = []
    for e_row in (e0, e1):
        hit = expert_ids == e_row
        seen = _dot(jnp.where(hit, 1.0, 0.0).astype(BF16), tri_ref[...])
        ranks.append(jnp.sum(jnp.where(hit, seen - 1.0 + run_ref[...], 0.0), axis=0, keepdims=True))
        run_ref[...] = run_ref[...] + seen[:, tm - 1:tm]
    zero = jnp.zeros_like(w0)
    r_ref[...] = jnp.concatenate([e0, e1, w0, w1, ranks[0], ranks[1], zero, zero], axis=0)


def _pack_halves(v):
    c = v.shape[1] // 2
    lo = lax.bitcast_convert_type(v[:, :c].astype(BF16).astype(F32), jnp.uint32)
    hi = lax.bitcast_convert_type(v[:, c:].astype(BF16).astype(F32), jnp.uint32)
    return (lo >> 16) | (hi & jnp.uint32(0xFFFF0000))


def _unpack_halves(p):
    lo = lax.bitcast_convert_type(p << 16, F32)
    hi = lax.bitcast_convert_type(p & jnp.uint32(0xFFFF0000), F32)
    return lo, hi


def _sc_gather_rows(table, idx):
    n, w = idx.shape[0], table.shape[1]
    win, nb, sub = SC_GATHER_WINDOW, SC_GATHER_BUFFERS, SC_GATHER_SUB
    per = n // (win * SC_WORKERS)
    assert per * win * SC_WORKERS == n and per >= 1
    mesh = plsc.VectorSubcoreMesh(core_axis_name="core", subcore_axis_name="subcore")

    @functools.partial(
        pl.kernel, out_type=jax.ShapeDtypeStruct((n, w), table.dtype), mesh=mesh, name="sc_gather_rows",
        scratch_types=([pltpu.VMEM((per * win,), jnp.int32)] + [pltpu.VMEM((win, w), table.dtype)] * nb
                       + [pltpu.SemaphoreType.DMA] * (2 * nb)))
    def gather(x_hbm, i_hbm, o_hbm, i_v, *rest):
        bufs, gsems, wsems = rest[:nb], rest[nb:2 * nb], rest[2 * nb:]
        wid = lax.axis_index("subcore") * SC_CORES + lax.axis_index("core")
        base = wid * (per * win)
        pltpu.sync_copy(i_hbm.at[pl.ds(base, per * win)], i_v)
        rot = (wid * per) // SC_WORKERS

        def row0(t):
            u = t + rot
            u = u - per * (u >= per).astype(jnp.int32)
            return pl.multiple_of(u * win, win)

        def gather_copy(t, b, s):
            return pltpu.make_async_copy(x_hbm.at[i_v.at[pl.ds(row0(t) + s * sub, sub)]],
                                         bufs[b].at[pl.ds(s * sub, sub)], gsems[b])

        def write_copy(t, b):
            return pltpu.make_async_copy(bufs[b], o_hbm.at[pl.ds(base + row0(t), win)], wsems[b])

        def start_gathers(t, b):
            for s in range(win // sub):
                gather_copy(t, b, s).start()

        def step(t, b):
            for s in range(win // sub):
                gather_copy(t, b, s).wait()
            write_copy(t, b).start()

            @pl.when(t >= 1)
            def _():
                write_copy(t - 1, (b - 1) % nb).wait()

            @pl.when(t + nb - 1 < per)
            def _():
                start_gathers(t + nb - 1, (b + nb - 1) % nb)

        for t in range(min(nb - 1, per)):
            start_gathers(t, t)

        @pl.loop(0, per // nb)
        def _(p):
            for j in range(nb):
                step(nb * p + j, j)

        for t in range(per - per % nb, per):
            step(t, t % nb)
        write_copy(per - 1, (per - 1) % nb).wait()

    return gather(table, idx)


def _lat_ctx_maps(n_lat, per, ctx_mod_row):
    def mod_row(i):
        return jnp.where(i < n_lat, i // per, ctx_mod_row)

    def lat(i):
        return (jnp.minimum(i, n_lat - 1), 0)

    def ctx(i):
        return (jnp.maximum(i - n_lat, 0), 0)

    return mod_row, lat, ctx


def _norm_router(x, cx, g, mod3, shift_blk, scale_blk, rw_t, rb, rows_per_mod, ctx_mod_row,
                 pending_proj=None, tm=512):
    n, d = x.shape
    n_lat = n // tm
    has_ctx = cx is not None
    ntot = n_lat + (cx.shape[0] // tm if has_ctx else 0)
    mod_row, lat, ctx = _lat_ctx_maps(n_lat, rows_per_mod // tm, ctx_mod_row)
    row_specs = [pl.BlockSpec((tm, d), lat)] + ([pl.BlockSpec((tm, d), ctx)] if has_ctx else [])
    row_args = [x] + ([cx] if has_ctx else [])
    out_shape = [jax.ShapeDtypeStruct((ntot * tm, d // 2), jnp.uint32),
                 jax.ShapeDtypeStruct((SUBLANES, ntot * tm), F32)]
    out_specs = [pl.BlockSpec((tm, d // 2), lambda i: (i, 0)),
                 pl.BlockSpec((SUBLANES, tm), lambda i: (0, i))]
    n_pending = 0
    if pending_proj is not None:
        pairs, gate_blk = pending_proj
        n_pending = len(pairs)
        for a, w in pairs:
            row_specs += [pl.BlockSpec((tm, a.shape[1]), lat),
                          pl.BlockSpec(w.shape, lambda i: (0, 0), pipeline_mode=pl.Buffered(1))]
            row_args += [a, w]
        row_specs.append(pl.BlockSpec((None, 1, d), lambda i: (jnp.minimum(i, n_lat - 1) // (rows_per_mod // tm),
                                                                0, gate_blk)))
        row_args.append(mod3)
        out_shape.append(jax.ShapeDtypeStruct((n, d), F32))
        out_specs.append(pl.BlockSpec((tm, d), lat))
    return pl.pallas_call(
        functools.partial(_norm_router_kernel, tm=tm, n_lat_tiles=n_lat, has_ctx=has_ctx,
                          n_pending=n_pending),
        out_shape=tuple(out_shape),
        grid=(ntot,),
        in_specs=row_specs + [
            pl.BlockSpec((1, d), lambda i: (0, 0)),
            pl.BlockSpec((None, 1, d), lambda i: (mod_row(i), 0, shift_blk)),
            pl.BlockSpec((None, 1, d), lambda i: (mod_row(i), 0, scale_blk)),
            pl.BlockSpec((N_EXPERTS, d), lambda i: (0, 0)),
            pl.BlockSpec((N_EXPERTS, 1), lambda i: (0, 0))],
        out_specs=tuple(out_specs),
        scratch_shapes=[pltpu.VMEM((tm, d), BF16), pltpu.VMEM((tm, tm), BF16), pltpu.VMEM((N_EXPERTS, 1), F32)],
        compiler_params=_cparams(("arbitrary",)),
        name="norm_router",
    )(*row_args, g.reshape(1, d), mod3, mod3, rw_t, rb)


def _cast_rows(src_ref, dst_ref, rb=256):
    def body(r, carry):
        rows = pl.ds(pl.multiple_of(r * rb, rb), rb)
        dst_ref[rows, :] = src_ref[0, rows, :].astype(dst_ref.dtype)
        return carry

    lax.fori_loop(0, dst_ref.shape[0] // rb, body, 0)


def _tile_state(te_ref, tv_ref, tile0):
    i = pl.program_id(0)
    t = tile0 + i
    live = tv_ref[t] > 0
    new_expert = (i == 0) | (te_ref[t] != te_ref[jnp.maximum(t - 1, 0)])
    return live, new_expert


def _gmm_kernel(te_ref, tv_ref, xs_ref, wg_ref, wu_ref, wd_ref, *rest, tile0):
    o_ref, wg_b, wu_b, wd_b = rest[-4:]
    live, new_expert = _tile_state(te_ref, tv_ref, tile0)

    @pl.when(live & new_expert)
    def _():
        _cast_rows(wg_ref, wg_b)
        _cast_rows(wu_ref, wu_b)
        _cast_rows(wd_ref, wd_b)

    @pl.when(live)
    def _():
        lo, hi = _unpack_halves(xs_ref[...])
        x = jnp.concatenate([lo.astype(BF16), hi.astype(BF16)], axis=1)
        gate = _dot(x, wg_b[...])
        up = _dot(x, wu_b[...])
        h1 = ((gate * jax.nn.sigmoid(gate)) * up).astype(BF16)
        o_ref[...] = _pack_halves(_dot(h1, wd_b[...]))

    @pl.when(jnp.logical_not(live))
    def _():
        o_ref[...] = jnp.zeros(o_ref.shape, o_ref.dtype)


def _grouped_mlp(tile_expert, tile_valid, xs_part, tile0, ys_prev, wg, wu, wd, layer, tm=MOE_TM):
    dpk = xs_part.shape[1]
    d, dff = wg.shape[2], wg.shape[3]
    n_tiles = tile_expert.shape[0]

    def expert_block(rows, cols, buffers):
        return pl.BlockSpec((None, 1, rows, cols), lambda i, te, tv: (layer, te[tile0 + i], 0, 0),
                            pipeline_mode=pl.Buffered(buffers))

    in_specs = [pl.BlockSpec((tm, dpk), lambda i, te, tv: (i, 0)),
                expert_block(d, dff, 1), expert_block(d, dff, 1), expert_block(dff, d, 2)]
    args = [tile_expert, tile_valid, xs_part, wg, wu, wd]
    aliases = {}
    if ys_prev is not None:
        in_specs.append(pl.BlockSpec(memory_space=pl.ANY))
        args.append(ys_prev)
        aliases = {len(args) - 1: 0}
    grid_spec = pltpu.PrefetchScalarGridSpec(
        num_scalar_prefetch=2,
        grid=(xs_part.shape[0] // tm,),
        in_specs=in_specs,
        out_specs=pl.BlockSpec((tm, dpk), lambda i, te, tv: (tile0 + i, 0)),
        scratch_shapes=[pltpu.VMEM((d, dff), BF16), pltpu.VMEM((d, dff), BF16), pltpu.VMEM((dff, d), BF16)],
    )
    return pl.pallas_call(
        functools.partial(_gmm_kernel, tile0=tile0),
        out_shape=jax.ShapeDtypeStruct((n_tiles * tm, dpk), jnp.uint32),
        grid_spec=grid_spec,
        input_output_aliases=aliases,
        compiler_params=_cparams(("arbitrary",)),
        name="grouped_mlp",
    )(*args)


def _combine_kernel(*refs, n_lat_tiles, has_ctx):
    if has_ctx:
        x_ref, xc_ref, y0_ref, y1_ref, r_ref, g_ref, o_ref, oc_ref = refs
    else:
        x_ref, y0_ref, y1_ref, r_ref, g_ref, o_ref = refs
    w0 = r_ref[:, 2:3]
    w1 = r_ref[:, 3:4]
    lo0, hi0 = _unpack_halves(y0_ref[...])
    lo1, hi1 = _unpack_halves(y1_ref[...])
    f = g_ref[...] * jnp.concatenate([w0 * lo0 + w1 * lo1, w0 * hi0 + w1 * hi1], axis=1)
    if not has_ctx:
        o_ref[...] = x_ref[...] + f
        return
    i = pl.program_id(0)

    @pl.when(i < n_lat_tiles)
    def _():
        o_ref[...] = x_ref[...] + f

    @pl.when(i >= n_lat_tiles)
    def _():
        oc_ref[...] = xc_ref[...] + f


def _combine(x, cx, yg, route_cols, mod3, gate_blk, rows_per_mod, ctx_mod_row, tm=512):
    n, d = x.shape
    n_lat = n // tm
    has_ctx = cx is not None
    ntot = n_lat + (cx.shape[0] // tm if has_ctx else 0)
    mod_row, lat, ctx = _lat_ctx_maps(n_lat, rows_per_mod // tm, ctx_mod_row)
    row_specs = [pl.BlockSpec((tm, d), lat)] + ([pl.BlockSpec((tm, d), ctx)] if has_ctx else [])
    row_args = [x] + ([cx] if has_ctx else [])
    out_shape = [jax.ShapeDtypeStruct(x.shape, F32)] + ([jax.ShapeDtypeStruct(cx.shape, F32)] if has_ctx else [])
    out = pl.pallas_call(
        functools.partial(_combine_kernel, n_lat_tiles=n_lat, has_ctx=has_ctx),
        out_shape=tuple(out_shape),
        grid=(ntot,),
        in_specs=row_specs + [
            pl.BlockSpec((tm, d // 2), lambda i: (i, 0)),
            pl.BlockSpec((tm, d // 2), lambda i: (ntot + i, 0)),
            pl.BlockSpec((tm, SUBLANES), lambda i: (i, 0)),
            pl.BlockSpec((None, 1, d), lambda i: (mod_row(i), 0, gate_blk))],
        out_specs=tuple(row_specs),
        compiler_params=_cparams(("arbitrary",)),
        name="moe_combine",
    )(*row_args, yg, yg, route_cols, mod3)
    return out if has_ctx else (out[0], None)


def _dispatch_plan(route, tm):
    n = route.shape[1]
    e_flat = jnp.concatenate([route[0], route[1]]).astype(jnp.int32)
    rank = jnp.concatenate([route[4], route[5]]).astype(jnp.int32)
    n_assign = 2 * n
    n_tiles = n_assign // tm + N_EXPERTS
    experts = jnp.arange(N_EXPERTS, dtype=jnp.int32)[:, None]
    onehot = (experts == e_flat[None, :]).astype(jnp.int32)
    counts = jnp.sum(onehot, axis=1)
    padded = ((counts + tm - 1) // tm) * tm
    ends_p = jnp.cumsum(padded)
    starts_p = ends_p - padded
    starts_c = jnp.cumsum(counts) - counts
    dest = jnp.sum(onehot * starts_p[:, None], axis=0) + rank
    by_expert = jnp.sum(onehot * starts_c[:, None], axis=0) + rank
    order = jnp.argsort(by_expert).astype(jnp.int32)
    p = jnp.arange(n_tiles * tm, dtype=jnp.int32)[None, :]
    owner = ((p >= starts_p[:, None]) & (p < ends_p[:, None])).astype(jnp.int32)
    within = jnp.sum(owner * (p - starts_p[:, None]), axis=0)
    live = jnp.sum(owner * (p - starts_p[:, None] < counts[:, None]), axis=0) > 0
    compact = jnp.sum(owner * starts_c[:, None], axis=0) + within
    src_tok = jnp.where(live, order[jnp.clip(compact, 0, n_assign - 1)] % n, p[0] % n)
    tile_start = jnp.arange(n_tiles, dtype=jnp.int32) * tm
    tile_valid = (tile_start < ends_p[-1]).astype(jnp.int32)
    last_tile = jnp.maximum(ends_p[-1] // tm - 1, 0) * tm
    tile_row = jnp.minimum(tile_start, last_tile)[:, None]
    tile_expert = jnp.sum((ends_p[None, :] <= tile_row).astype(jnp.int32), axis=1)
    tile_expert = jnp.minimum(tile_expert, N_EXPERTS - 1)
    return src_tok, dest, tile_expert, tile_valid


def _moe(x, cx, g, mod3, rw_t, rb, wg, wu, wd, layer, rows_per_mod, ctx_mod_row, combine=True,
         pending_proj=None):
    if pending_proj is None:
        h, route = _norm_router(x, cx, g, mod3, 3, 4, rw_t, rb, rows_per_mod, ctx_mod_row)
    else:
        h, route, x = _norm_router(x, cx, g, mod3, 3, 4, rw_t, rb, rows_per_mod, ctx_mod_row, pending_proj)
    src_tok, dest, tile_expert, tile_valid = _dispatch_plan(route, MOE_TM)
    n_tiles = tile_expert.shape[0]
    bounds = [n_tiles * k // MOE_RANGES for k in range(MOE_RANGES + 1)]
    ys = None
    for t0, t1 in zip(bounds[:-1], bounds[1:]):
        xs = _sc_gather_rows(h, src_tok[t0 * MOE_TM:t1 * MOE_TM])
        ys = _grouped_mlp(tile_expert, tile_valid, xs, t0, ys, wg, wu, wd, layer)
    yg = _sc_gather_rows(ys, dest)
    route_cols = route.T
    if not combine:
        return yg, route_cols, x
    return _combine(x, cx, yg, route_cols, mod3, 5, rows_per_mod, ctx_mod_row)


def _rope_tables(seq):
    rows = seq // GRID_W
    row = jnp.repeat(jnp.arange(rows, dtype=F32), GRID_W)
    col = jnp.tile(jnp.arange(GRID_W, dtype=F32), rows)
    n_freq = HEAD_DIM // 4
    inv_freq = ROPE_BASE ** (-jnp.arange(n_freq, dtype=F32) / n_freq)
    ang = jnp.concatenate([row[:, None] * inv_freq, col[:, None] * inv_freq], axis=-1)
    cos, sin = jnp.cos(ang), jnp.sin(ang)
    return jnp.concatenate([cos, cos], axis=-1), jnp.concatenate([-sin, sin], axis=-1)


def kernel(x, c, ctx, c_ctx, ada_w, ada_b, norm_mix, norm_ffn, ab_w_in, ab_q_gain, ab_k_gain, ab_sink, ab_conv_w, ab_conv_b, ab_gate_a_w, ab_gate_a_b, ab_gate_x_w, ab_gate_x_b, ab_lru_lambda, ab_w_out, gqa_w_in, gqa_q_gain, gqa_k_gain, gqa_w_out, router_w, router_bias, moe_w_gate, moe_w_up, moe_w_down):
    batch, seq, d = x.shape
    ctx_len = ctx.shape[1]
    depth = ada_w.shape[0]
    assert depth == 2 and batch < SUBLANES
    n_lat = batch * seq
    n_ctx = batch * ctx_len
    ctx_row = batch

    xl = x.reshape(n_lat, d)
    xc = ctx.reshape(n_ctx, d)
    cc = jnp.zeros((SUBLANES, d), F32).at[:batch].set(c).at[ctx_row].set(c_ctx)
    mod = _ada(cc, ada_w, ada_b)
    cos2, sin2 = _rope_tables(seq)
    rw_t = router_w.T.astype(BF16)
    rb = router_bias.reshape(N_EXPERTS, 1).astype(F32)
    experts = (moe_w_gate, moe_w_up, moe_w_down)

    mod3 = mod[0].reshape(SUBLANES, 1, 6 * d)
    w_in = ab_w_in[0].astype(BF16)
    proj = _norm_mod_matmul(xl, norm_mix[0], mod3, 0, 1, w_in, seq, 0,
                            A_Q_HEADS, A_KV_HEADS, ab_q_gain[0], ab_k_gain[0], cos2, sin2)
    proj_c = _norm_mod_matmul(xc, norm_mix[0], mod3, 0, 1, w_in, n_ctx, ctx_row,
                              A_Q_HEADS, A_KV_HEADS, ab_q_gain[0], ab_k_gain[0])
    att = _win_attn(ab_sink[0], proj, proj_c, batch, seq, ctx_len)
    att_c = _ctx_attn(ab_sink[0], proj_c, batch, ctx_len)

    lru_w = ab_conv_w.shape[2]
    c0 = (A_Q_HEADS + 2 * A_KV_HEADS) * HEAD_DIM
    y_p, yc_p = _lru(_to_chunked(proj[:, c0:c0 + lru_w], batch),
                     _to_chunked(proj[:, c0 + lru_w:c0 + 2 * lru_w], batch),
                     _to_chunked(proj_c[:, c0:c0 + lru_w], batch),
                     _to_chunked(proj_c[:, c0 + lru_w:c0 + 2 * lru_w], batch),
                     ab_conv_w[0], ab_conv_b[0], ab_gate_a_w[0], ab_gate_a_b[0],
                     ab_gate_x_w[0], ab_gate_x_b[0], ab_lru_lambda[0])
    w_out = ab_w_out[0].astype(BF16)
    xc = _out_proj(att_c, 0, _from_chunked(yc_p), 0, w_out, xc, mod3, 2, n_ctx, ctx_row)
    half = w_out.shape[0] // 2
    yg, route_cols, xl = _moe(
        xl, xc, norm_ffn[0], mod3, rw_t, rb, *experts, 0, seq, ctx_row, combine=False,
        pending_proj=([(att, w_out[:half]), (_from_chunked(y_p), w_out[half:])], 2))

    mod3_prev, mod3 = mod3, mod[1].reshape(SUBLANES, 1, 6 * d)
    w_in = gqa_w_in[0].astype(BF16)
    cw = C_Q_HEADS * HEAD_DIM
    proj, xl = _norm_mod_matmul(xl, norm_mix[1], mod3, 0, 1, w_in, seq, 0,
                                C_Q_HEADS, C_KV_HEADS, gqa_q_gain[0], gqa_k_gain[0], cos2, sin2,
                                pending_moe=(yg, route_cols, 0, mod3_prev, 5))
    proj_c, _ = _norm_mod_matmul(xc, norm_mix[1], mod3, 0, 1, w_in[:, cw:], n_ctx, ctx_row,
                                 0, C_KV_HEADS, gqa_q_gain[0], gqa_k_gain[0],
                                 pending_moe=(yg, route_cols, n_lat, mod3_prev, 5))
    att = _dense_attn(proj, proj_c, batch, seq, ctx_len)
    xl, _ = _moe(xl, None, norm_ffn[1], mod3, rw_t, rb, *experts, 1, seq, ctx_row,
                 pending_proj=([(att, gqa_w_out[0].astype(BF16))], 2))
    return xl.reshape(batch, seq, d)
```

```python
import functools

import jax
import jax.numpy as jnp
import numpy as np
from jax import lax
from jax.experimental import pallas as pl
from jax.experimental.pallas import tpu as pltpu
from jax.experimental.pallas import tpu_sc as plsc

F32 = jnp.float32
BF16 = jnp.bfloat16

LANES = 128
SUBLANES = 8
VMEM_LIMIT = 56 * 1024 * 1024

HEAD_DIM = 128
GRID_W = 64
WINDOW = 128
BLOCK = 128
ROPE_BASE = 10000.0
EPS = 1e-6
ATTN_SCALE = HEAD_DIM ** -0.5
A_Q_HEADS, A_KV_HEADS = 8, 2
C_Q_HEADS, C_KV_HEADS = 16, 4
GQA_GROUP = 4
LRU_C = 8.0
CONV_W = 4
CONV_LEFT = 2
N_EXPERTS = 16
N_GROUPS = 4
EXPERTS_PER_GROUP = 4
NEG_BIG = -1e30

LRU_CHUNKS = SUBLANES
LRU_JB = 16
LRU_LANE_BLOCKS = 2
LRU_SCAN_ROWS = 4
MOE_TM = 256
MOE_RANGES = 2
SC_CORES = 2
SC_WORKERS = 32
SC_GATHER_WINDOW = 32
SC_GATHER_BUFFERS = 3
SC_GATHER_SUB = 8


def _cparams(sem, vmem=VMEM_LIMIT):
    return pltpu.CompilerParams(dimension_semantics=sem, vmem_limit_bytes=vmem)


def _dot(a, b):
    return jnp.dot(a, b, preferred_element_type=F32)


def _dot_nt(a, b):
    return lax.dot_general(a, b, (((1,), (1,)), ((), ())), preferred_element_type=F32)


def _ada_kernel(c_ref, w_ref, b_ref, o_ref):
    c = c_ref[...]
    s = (c * jax.nn.sigmoid(c)).astype(BF16)
    o_ref[0] = _dot(s, w_ref[0].astype(BF16)) + b_ref[0]


def _ada(cc, ada_w, ada_b):
    depth, d, n = ada_w.shape
    tn = 1024
    return pl.pallas_call(
        _ada_kernel,
        out_shape=jax.ShapeDtypeStruct((depth, SUBLANES, n), F32),
        grid=(depth, n // tn),
        in_specs=[pl.BlockSpec((SUBLANES, d), lambda l, j: (0, 0)),
                  pl.BlockSpec((1, d, tn), lambda l, j: (l, 0, j)),
                  pl.BlockSpec((1, 1, tn), lambda l, j: (l, 0, j))],
        out_specs=pl.BlockSpec((1, SUBLANES, tn), lambda l, j: (l, 0, j)),
        compiler_params=_cparams(("arbitrary", "arbitrary")),
        name="ada",
    )(cc, ada_w, ada_b.reshape(depth, 1, n))


def _norm_mod_rows(x_ref, g_ref, sh_ref, sc_ref, dst_ref, tm, rc=128, load_rows=None):
    g = g_ref[...]
    sc1 = 1.0 + sc_ref[...]
    sh = sh_ref[...]

    def body(r, carry):
        rows = pl.ds(pl.multiple_of(r * rc, rc), rc)
        xf = x_ref[rows, :] if load_rows is None else load_rows(rows)
        ms = jnp.mean(xf * xf, axis=-1, keepdims=True)
        xn = (xf * lax.rsqrt(ms + EPS)) * g
        dst_ref[rows, :] = (xn * sc1 + sh).astype(dst_ref.dtype)
        return carry

    lax.fori_loop(0, tm // rc, body, 0)


def _moe_residual_rows(x_ref, y0_ref, y1_ref, r_ref, gate_ref, rows):
    w0 = r_ref[rows, 2:3]
    w1 = r_ref[rows, 3:4]
    lo0, hi0 = _unpack_halves(y0_ref[rows, :])
    lo1, hi1 = _unpack_halves(y1_ref[rows, :])
    f = jnp.concatenate([w0 * lo0 + w1 * lo1, w0 * hi0 + w1 * hi1], axis=1)
    return x_ref[rows, :] + gate_ref[...] * f


def _nm_mm_kernel(*refs, tm, tn, n_q, n_k, rope, pending_moe):
    refs = list(refs)
    x_ref = refs.pop(0)
    if pending_moe:
        y0_ref, y1_ref, r_ref, gate_ref = (refs.pop(0) for _ in range(4))
    g_ref, sh_ref, sc_ref, w_ref, qg_ref, kg_ref = (refs.pop(0) for _ in range(6))
    if rope:
        cos_ref, sin_ref = refs.pop(0), refs.pop(0)
    o_ref = refs.pop(0)
    if pending_moe:
        xnew_ref = refs.pop(0)

        def load_rows(rows):
            xf = _moe_residual_rows(x_ref, y0_ref, y1_ref, r_ref, gate_ref, rows)
            xnew_ref[rows, :] = xf
            return xf
    else:
        load_rows = None
    hn_ref = refs.pop(0)
    _norm_mod_rows(x_ref, g_ref, sh_ref, sc_ref, hn_ref, tm, load_rows=load_rows)
    h = hn_ref[...]
    heads_per_chunk = tn // HEAD_DIM
    for j in range(w_ref.shape[1] // tn):
        cols = slice(j * tn, (j + 1) * tn)
        y = _dot(h, w_ref[:, cols])
        parts = []
        for hh in range(heads_per_chunk):
            head = j * heads_per_chunk + hh
            yh = y[:, _head_cols(hh)]
            if head < n_q + n_k:
                gain = qg_ref[...] if head < n_q else kg_ref[...]
                ms = jnp.mean(yh * yh, axis=-1, keepdims=True)
                yh = (yh * lax.rsqrt(ms + EPS)) * gain
                if rope:
                    yh = yh * cos_ref[...] + pltpu.roll(yh, HEAD_DIM // 2, 1) * sin_ref[...]
                if head < n_q:
                    yh = yh * ATTN_SCALE
            parts.append(yh.astype(o_ref.dtype))
        o_ref[:, cols] = jnp.concatenate(parts, axis=1)


def _norm_mod_matmul(x, g, mod3, shift_blk, scale_blk, w, rows_per_mod, mod_base,
                     n_q, n_k, q_gain, k_gain, cos2=None, sin2=None, pending_moe=None, tm=512, tn=512):
    n, d = x.shape
    nout = w.shape[1]
    tm = min(tm, n)
    tn = min(tn, nout)
    per = rows_per_mod // tm
    rope = cos2 is not None

    def mod_row(i):
        return mod_base + i // per

    head_vec = pl.BlockSpec((1, HEAD_DIM), lambda i: (0, 0))
    in_specs = [pl.BlockSpec((tm, d), lambda i: (i, 0))]
    args = [x]
    out_shape = [jax.ShapeDtypeStruct((n, nout), BF16)]
    out_specs = [pl.BlockSpec((tm, nout), lambda i: (i, 0))]
    if pending_moe is not None:
        yg, route_cols, first_row, mod3_prev, gate_blk = pending_moe
        t0 = first_row // tm
        slot1 = yg.shape[0] // 2 // tm
        in_specs += [pl.BlockSpec((tm, d // 2), lambda i: (t0 + i, 0)),
                     pl.BlockSpec((tm, d // 2), lambda i: (slot1 + t0 + i, 0)),
                     pl.BlockSpec((tm, SUBLANES), lambda i: (t0 + i, 0)),
                     pl.BlockSpec((None, 1, d), lambda i: (mod_row(i), 0, gate_blk))]
        args += [yg, yg, route_cols, mod3_prev]
        out_shape.append(jax.ShapeDtypeStruct((n, d), F32))
        out_specs.append(pl.BlockSpec((tm, d), lambda i: (i, 0)))
    in_specs += [pl.BlockSpec((1, d), lambda i: (0, 0)),
                 pl.BlockSpec((None, 1, d), lambda i: (mod_row(i), 0, shift_blk)),
                 pl.BlockSpec((None, 1, d), lambda i: (mod_row(i), 0, scale_blk)),
                 pl.BlockSpec((d, nout), lambda i: (0, 0), pipeline_mode=pl.Buffered(1)),
                 head_vec, head_vec]
    args += [g.reshape(1, d), mod3, mod3, w, q_gain.reshape(1, HEAD_DIM), k_gain.reshape(1, HEAD_DIM)]
    if rope:
        tiles_per_seq = cos2.shape[0] // tm
        table = pl.BlockSpec((tm, HEAD_DIM), lambda i: (i % tiles_per_seq, 0))
        in_specs += [table, table]
        args += [cos2, sin2]
    out = pl.pallas_call(
        functools.partial(_nm_mm_kernel, tm=tm, tn=tn, n_q=n_q, n_k=n_k, rope=rope,
                          pending_moe=pending_moe is not None),
        out_shape=tuple(out_shape),
        grid=(n // tm,),
        in_specs=in_specs,
        out_specs=tuple(out_specs),
        scratch_shapes=[pltpu.VMEM((tm, d), BF16)],
        compiler_params=_cparams(("parallel",)),
        name="norm_mod_matmul",
    )(*args)
    return out if pending_moe is not None else out[0]


def _head_cols(h):
    return slice(h * HEAD_DIM, (h + 1) * HEAD_DIM)


def _stack_group(q_ref, kvh):
    return jnp.concatenate([q_ref[:, _head_cols(kvh * GQA_GROUP + g)] for g in range(GQA_GROUP)], axis=0)


def _sink_col(sink_ref, kvh, rows):
    return jnp.concatenate([jnp.full((rows, 1), sink_ref[kvh * GQA_GROUP + g], F32)
                            for g in range(GQA_GROUP)], axis=0)


def _band_bias(ctx_len):
    rows, nk = GQA_GROUP * BLOCK, 3 * BLOCK + ctx_len
    qi = np.arange(rows)[:, None] % BLOCK
    kj = np.arange(nk)[None, :]
    inner = (kj >= 3 * BLOCK) | (np.abs(kj - BLOCK - qi) <= WINDOW)
    first = inner & ~(kj < BLOCK)
    last = inner & ~((kj >= 2 * BLOCK) & (kj < 3 * BLOCK))
    return np.where(np.stack([first, inner, last]), 0.0, NEG_BIG).astype(np.float32)


def _win_attn_kernel(sink_ref, bias_ref, q_ref, kp_ref, kc_ref, kn_ref, vp_ref, vc_ref, vn_ref,
                     kx_ref, vx_ref, o_ref, *, ctx_len):
    nk = 3 * BLOCK + ctx_len
    bias = bias_ref[...]
    ones = jnp.ones((nk, HEAD_DIM), BF16)
    for kvh in range(A_KV_HEADS):
        cols = _head_cols(kvh)
        q4 = _stack_group(q_ref, kvh)
        ka = jnp.concatenate([kp_ref[:, cols], kc_ref[:, cols], kn_ref[:, cols], kx_ref[:, cols]], axis=0)
        va = jnp.concatenate([vp_ref[:, cols], vc_ref[:, cols], vn_ref[:, cols], vx_ref[:, cols]], axis=0)
        s = _dot_nt(q4, ka) + bias
        sk = _sink_col(sink_ref, kvh, BLOCK)
        m = jnp.maximum(jnp.max(s, axis=-1, keepdims=True), sk)
        p = jnp.exp((s - m).astype(BF16))
        acc = _dot(p, jnp.concatenate([va, ones], axis=1))
        o = acc[:, 0:HEAD_DIM] / (acc[:, HEAD_DIM:] + jnp.exp(sk - m))
        for g in range(GQA_GROUP):
            o_ref[:, _head_cols(kvh * GQA_GROUP + g)] = o[g * BLOCK:(g + 1) * BLOCK].astype(o_ref.dtype)


def _win_attn(sink, proj, proj_c, batch, seq, ctx_len):
    nb = seq // BLOCK
    assert nb >= 2
    kvw = A_KV_HEADS * HEAD_DIM
    k_blk = A_Q_HEADS * HEAD_DIM // kvw
    v_blk = k_blk + 1
    bias = jnp.asarray(_band_bias(ctx_len))

    def which_bias(b, n):
        return (jnp.where(n == 0, 0, jnp.where(n == nb - 1, 2, 1)), 0, 0)

    def prev(b, n):
        return b * nb + jnp.maximum(n - 1, 0)

    def cur(b, n):
        return b * nb + n

    def nxt(b, n):
        return b * nb + jnp.minimum(n + 1, nb - 1)

    return pl.pallas_call(
        functools.partial(_win_attn_kernel, ctx_len=ctx_len),
        out_shape=jax.ShapeDtypeStruct((batch * seq, A_Q_HEADS * HEAD_DIM), BF16),
        grid=(batch, nb),
        in_specs=[pl.BlockSpec(memory_space=pltpu.SMEM),
                  pl.BlockSpec((None,) + bias.shape[1:], which_bias),
                  pl.BlockSpec((BLOCK, A_Q_HEADS * HEAD_DIM), lambda b, n: (cur(b, n), 0)),
                  pl.BlockSpec((BLOCK, kvw), lambda b, n: (prev(b, n), k_blk)),
                  pl.BlockSpec((BLOCK, kvw), lambda b, n: (cur(b, n), k_blk)),
                  pl.BlockSpec((BLOCK, kvw), lambda b, n: (nxt(b, n), k_blk)),
                  pl.BlockSpec((BLOCK, kvw), lambda b, n: (prev(b, n), v_blk)),
                  pl.BlockSpec((BLOCK, kvw), lambda b, n: (cur(b, n), v_blk)),
                  pl.BlockSpec((BLOCK, kvw), lambda b, n: (nxt(b, n), v_blk)),
                  pl.BlockSpec((ctx_len, kvw), lambda b, n: (b, k_blk)),
                  pl.BlockSpec((ctx_len, kvw), lambda b, n: (b, v_blk))],
        out_specs=pl.BlockSpec((BLOCK, A_Q_HEADS * HEAD_DIM), lambda b, n: (cur(b, n), 0)),
        compiler_params=_cparams(("parallel", "parallel")),
        name="win_attn",
    )(sink, bias, proj, proj, proj, proj, proj, proj, proj, proj_c, proj_c)


def _ctx_attn_kernel(sink_ref, q_ref, k_ref, v_ref, o_ref, *, ctx_len):
    kvh = pl.program_id(1)
    q4 = jnp.concatenate([q_ref[:, _head_cols(g)] for g in range(GQA_GROUP)], axis=0)
    s = _dot_nt(q4, k_ref[...])
    sk = jnp.concatenate([jnp.full((ctx_len, 1), sink_ref[kvh * GQA_GROUP + g], F32)
                          for g in range(GQA_GROUP)], axis=0)
    m = jnp.maximum(jnp.max(s, axis=-1, keepdims=True), sk)
    p = jnp.exp(s - m)
    den = jnp.sum(p, axis=-1, keepdims=True) + jnp.exp(sk - m)
    o = _dot(p.astype(BF16), v_ref[...]) / den
    for g in range(GQA_GROUP):
        o_ref[:, _head_cols(g)] = o[g * ctx_len:(g + 1) * ctx_len].astype(o_ref.dtype)


def _ctx_attn(sink, proj_c, batch, ctx_len):
    k_blk = A_Q_HEADS
    v_blk = A_Q_HEADS + A_KV_HEADS
    gw = GQA_GROUP * HEAD_DIM
    return pl.pallas_call(
        functools.partial(_ctx_attn_kernel, ctx_len=ctx_len),
        out_shape=jax.ShapeDtypeStruct((batch * ctx_len, A_Q_HEADS * HEAD_DIM), BF16),
        grid=(batch, A_KV_HEADS),
        in_specs=[pl.BlockSpec(memory_space=pltpu.SMEM),
                  pl.BlockSpec((ctx_len, gw), lambda b, h: (b, h)),
                  pl.BlockSpec((ctx_len, HEAD_DIM), lambda b, h: (b, k_blk + h)),
                  pl.BlockSpec((ctx_len, HEAD_DIM), lambda b, h: (b, v_blk + h))],
        out_specs=pl.BlockSpec((ctx_len, gw), lambda b, h: (b, h)),
        compiler_params=_cparams(("parallel", "parallel")),
        name="ctx_attn",
    )(sink, proj_c, proj_c, proj_c)


def _sigmoid(x):
    return 0.5 * jnp.tanh(0.5 * x) + 0.5


def _gelu_tanh(x):
    return 0.5 * x * (1.0 + jnp.tanh(0.7978845608028654 * (x + 0.044715 * (x * x * x))))


def _lru_sequence(x_ref, xg_ref, y_ref, xp_ref, a_ref, b_ref, w, init, rows, jb):
    conv_w, conv_b, wa, ba, wx, bx, c_logsig = w
    width = x_ref.shape[-1]
    sub = lax.broadcasted_iota(jnp.int32, (1, SUBLANES, width), 1)

    def block_diag(ub, wd):
        return jnp.concatenate([_dot(ub[:, _head_cols(n)], wd[n]) for n in range(width // LANES)], axis=1)

    def fill(r, carry):
        rr = pl.ds(pl.multiple_of(r * jb, jb), jb)
        xp_ref[pl.ds(pl.multiple_of(r * jb, jb) + CONV_LEFT, jb)] = x_ref[rr].astype(F32)
        return carry

    lax.fori_loop(0, rows // jb, fill, 0)
    tail = x_ref[rows - CONV_LEFT:rows].astype(F32)
    xp_ref[0:CONV_LEFT] = jnp.where(sub == 0, 0.0, pltpu.roll(tail, 1, 1))
    head = x_ref[0:1].astype(F32)
    xp_ref[rows + CONV_LEFT:rows + CONV_LEFT + 1] = jnp.where(
        sub == SUBLANES - 1, 0.0, pltpu.roll(head, SUBLANES - 1, 1))

    def gates(r, carry):
        j0 = pl.multiple_of(r * jb, jb)
        u = conv_b
        for k in range(CONV_W):
            u = u + conv_w[k] * xp_ref[pl.ds(j0 + k, jb)]
        u2 = u.reshape(jb * SUBLANES, width)
        ub = u2.astype(BF16)
        for d in range(2):
            r_gate = _sigmoid(block_diag(ub, wa[d]) + ba[d])
            i_gate = _sigmoid(block_diag(ub, wx[d]) + bx[d])
            log_a = c_logsig[d] * r_gate
            a = jnp.exp(log_a)
            v = 1.0 - a * a
            root = jnp.where(v > 0.0, v * lax.rsqrt(v), 0.0)
            b = root * (i_gate * u2)
            a_ref[d, pl.ds(j0, jb)] = a.reshape(jb, SUBLANES, width)
            b_ref[d, pl.ds(j0, jb)] = b.reshape(jb, SUBLANES, width)
        return carry

    lax.fori_loop(0, rows // jb, gates, 0, unroll=2)

    def scan(it, carry):
        hf, pf, hb, pb = carry
        fwd = pl.ds(pl.multiple_of(it * LRU_SCAN_ROWS, LRU_SCAN_ROWS), LRU_SCAN_ROWS)
        bwd = pl.ds(pl.multiple_of(rows - LRU_SCAN_ROWS - it * LRU_SCAN_ROWS, LRU_SCAN_ROWS), LRU_SCAN_ROWS)
        af, bf = a_ref[0, fwd], b_ref[0, fwd]
        ab, bb = a_ref[1, bwd], b_ref[1, bwd]
        hs_f, ps_f = [], []
        hs_b, ps_b = [None] * LRU_SCAN_ROWS, [None] * LRU_SCAN_ROWS
        for r in range(LRU_SCAN_ROWS):
            hf = af[r] * hf + bf[r]
            pf = pf * af[r]
            hs_f.append(hf)
            ps_f.append(pf)
            rr = LRU_SCAN_ROWS - 1 - r
            hb = ab[rr] * hb + bb[rr]
            pb = pb * ab[rr]
            hs_b[rr] = hb
            ps_b[rr] = pb
        b_ref[0, fwd] = jnp.stack(hs_f)
        a_ref[0, fwd] = jnp.stack(ps_f)
        b_ref[1, bwd] = jnp.stack(hs_b)
        a_ref[1, bwd] = jnp.stack(ps_b)
        return hf, pf, hb, pb

    z = jnp.zeros((SUBLANES, width), F32)
    o = jnp.ones((SUBLANES, width), F32)
    lax.fori_loop(0, rows // LRU_SCAN_ROWS, scan, (z, o, z, o), unroll=2)

    hf_last, pf_last = b_ref[0, rows - 1], a_ref[0, rows - 1]
    hb_last, pb_last = b_ref[1, 0], a_ref[1, 0]
    s = init[0]
    carry_f = []
    for c in range(SUBLANES):
        carry_f.append(s)
        s = hf_last[c:c + 1] + pf_last[c:c + 1] * s
    out_f = s
    s = init[1]
    carry_b = [None] * SUBLANES
    for c in reversed(range(SUBLANES)):
        carry_b[c] = s
        s = hb_last[c:c + 1] + pb_last[c:c + 1] * s
    out_b = s
    cf = jnp.concatenate(carry_f, axis=0)
    cb = jnp.concatenate(carry_b, axis=0)

    def emit(r, carry):
        rr = pl.ds(pl.multiple_of(r * jb, jb), jb)
        h = (b_ref[0, rr] + a_ref[0, rr] * cf) + (b_ref[1, rr] + a_ref[1, rr] * cb)
        y_ref[rr] = (h * _gelu_tanh(xg_ref[rr].astype(F32))).astype(y_ref.dtype)
        return carry

    lax.fori_loop(0, rows // jb, emit, 0)
    return out_f, out_b


def _lru_kernel(xr_ref, xg_ref, xrc_ref, xgc_ref, cw_ref, cb_ref, wa_ref, ba_ref, wx_ref, bx_ref,
                lam_ref, y_ref, yc_ref, xp_ref, a_ref, b_ref, *, rows, rows_c):
    c_logsig = [LRU_C * jax.nn.log_sigmoid(lam_ref[d]) for d in range(2)]
    w = ([cw_ref[k] for k in range(CONV_W)], cb_ref[0],
         [wa_ref[d] for d in range(2)], [ba_ref[d] for d in range(2)],
         [wx_ref[d] for d in range(2)], [bx_ref[d] for d in range(2)], c_logsig)
    zero = jnp.zeros((1, xr_ref.shape[-1]), F32)
    sf, sb = _lru_sequence(xrc_ref, xgc_ref, yc_ref, xp_ref, a_ref, b_ref, w, (zero, zero),
                           rows_c, min(LRU_JB, rows_c))
    _lru_sequence(xr_ref, xg_ref, y_ref, xp_ref, a_ref, b_ref, w, (sf, sb), rows, LRU_JB)


def _lru(xr, xg, xrc, xgc, conv_w, conv_b, wa, ba, wx, bx, lam):
    batch, rows, _, width = xr.shape
    rows_c = xrc.shape[1]
    cw = LRU_LANE_BLOCKS * LANES
    seq_spec = pl.BlockSpec((None, rows, SUBLANES, cw), lambda b, n: (b, 0, 0, n))
    ctx_spec = pl.BlockSpec((None, rows_c, SUBLANES, cw), lambda b, n: (b, 0, 0, n))
    vec2 = pl.BlockSpec((2, 1, cw), lambda b, n: (0, 0, n))
    mat2 = pl.BlockSpec((2, LRU_LANE_BLOCKS, LANES, LANES), lambda b, n: (0, n, 0, 0))
    return pl.pallas_call(
        functools.partial(_lru_kernel, rows=rows, rows_c=rows_c),
        out_shape=(jax.ShapeDtypeStruct(xr.shape, BF16), jax.ShapeDtypeStruct(xrc.shape, BF16)),
        grid=(batch, width // cw),
        in_specs=[seq_spec, seq_spec, ctx_spec, ctx_spec,
                  pl.BlockSpec((CONV_W, 1, cw), lambda b, n: (0, 0, n)),
                  pl.BlockSpec((1, 1, cw), lambda b, n: (0, 0, n)),
                  mat2, vec2, mat2, vec2, vec2],
        out_specs=(seq_spec, ctx_spec),
        scratch_shapes=[pltpu.VMEM((rows + CONV_W - 1, SUBLANES, cw), F32),
                        pltpu.VMEM((2, rows, SUBLANES, cw), F32),
                        pltpu.VMEM((2, rows, SUBLANES, cw), F32)],
        compiler_params=_cparams(("parallel", "parallel")),
        name="rglru",
    )(xr, xg, xrc, xgc, conv_w.reshape(CONV_W, 1, width), conv_b.reshape(1, 1, width),
      wa.astype(BF16), ba.reshape(2, 1, width), wx.astype(BF16), bx.reshape(2, 1, width),
      lam.reshape(2, 1, width))


def _to_chunked(a, batch):
    t = a.shape[0] // batch
    return a.reshape(batch, LRU_CHUNKS, t // LRU_CHUNKS, a.shape[1]).transpose(0, 2, 1, 3)


def _from_chunked(a):
    b, r, c, w = a.shape
    return a.transpose(0, 2, 1, 3).reshape(b * r * c, w)


def _out_proj_kernel(a1_ref, a2_ref, w1_ref, w2_ref, x_ref, g_ref, o_ref):
    y = _dot(a1_ref[...], w1_ref[...]) + _dot(a2_ref[...], w2_ref[...])
    o_ref[...] = x_ref[...] + g_ref[...] * y


def _out_proj(a1, a1_blk, a2, a2_blk, w, x, mod3, gate_blk, rows_per_mod, mod_base, tm=2048, tn=512):
    n, d = x.shape
    kh = w.shape[0] // 2
    tm = min(tm, n)
    per = rows_per_mod // tm
    gpb = d // tn
    return pl.pallas_call(
        _out_proj_kernel,
        out_shape=jax.ShapeDtypeStruct((n, d), F32),
        grid=(n // tm, d // tn),
        in_specs=[pl.BlockSpec((tm, kh), lambda i, j: (i, a1_blk)),
                  pl.BlockSpec((tm, kh), lambda i, j: (i, a2_blk)),
                  pl.BlockSpec((kh, tn), lambda i, j: (0, j)),
                  pl.BlockSpec((kh, tn), lambda i, j: (1, j)),
                  pl.BlockSpec((tm, tn), lambda i, j: (i, j)),
                  pl.BlockSpec((None, 1, tn), lambda i, j: (mod_base + i // per, 0, gate_blk * gpb + j))],
        out_specs=pl.BlockSpec((tm, tn), lambda i, j: (i, j)),
        compiler_params=_cparams(("parallel", "parallel")),
        name="out_proj",
    )(a1, a2, w, w, x, mod3)


def _lane_block_max(s):
    mm = s[:, 0:LANES]
    for t in range(1, s.shape[1] // LANES):
        mm = jnp.maximum(mm, s[:, t * LANES:(t + 1) * LANES])
    return mm


def _exp_blocks(s, mrep):
    return jnp.concatenate(
        [jnp.exp((s[:, t * LANES:(t + 1) * LANES] - mrep).astype(BF16)) for t in range(s.shape[1] // LANES)],
        axis=1)


def _copy_key_rows(dst_ref, c, kc, lat_ref, ctx_ref, cols):
    seq = lat_ref.shape[0]
    lo, hi = c * kc, (c + 1) * kc
    if lo < seq:
        n = min(hi, seq) - lo
        dst_ref[c, 0:n, cols] = lat_ref[lo:lo + n, :]
    if hi > seq:
        start = max(lo, seq)
        dst_ref[c, start - lo:kc, cols] = ctx_ref[start - seq:hi - seq, :]


def _dense_attn_kernel(q_ref, k_ref, v_ref, kx_ref, vx_ref, o_ref,
                       s_ref, m_ref, acc_ref, ka_ref, va_ref, *, tq, n_chunks, kc):
    @pl.when(pl.program_id(2) == 0)
    def _():
        for c in range(n_chunks):
            _copy_key_rows(ka_ref, c, kc, k_ref, kx_ref, slice(0, HEAD_DIM))
            _copy_key_rows(va_ref, c, kc, v_ref, vx_ref, slice(0, HEAD_DIM))
            va_ref[c, :, HEAD_DIM:] = jnp.ones((kc, HEAD_DIM), BF16)

    q4 = jnp.concatenate([q_ref[:, _head_cols(g)] for g in range(GQA_GROUP)], axis=0)
    m_ref[...] = jnp.full(m_ref.shape, NEG_BIG, F32)

    def sweep1(c, carry):
        s = _dot_nt(q4, ka_ref[c])
        s_ref[c] = s
        m_ref[...] = jnp.maximum(m_ref[...], _lane_block_max(s))
        return carry

    lax.fori_loop(0, n_chunks, sweep1, 0)
    m_ref[...] = jnp.broadcast_to(jnp.max(m_ref[...], axis=-1, keepdims=True), m_ref.shape)
    acc_ref[...] = jnp.zeros(acc_ref.shape, F32)

    def sweep2(c, carry):
        acc_ref[...] += _dot(_exp_blocks(s_ref[c], m_ref[...]), va_ref[c])
        return carry

    lax.fori_loop(0, n_chunks, sweep2, 0)
    o = acc_ref[:, 0:HEAD_DIM] / acc_ref[:, HEAD_DIM:]
    for g in range(GQA_GROUP):
        o_ref[:, _head_cols(g)] = o[g * tq:(g + 1) * tq].astype(o_ref.dtype)


def _dense_attn(proj, proj_c, batch, seq, ctx_len, tq=256, n_chunks=2):
    gw = GQA_GROUP * HEAD_DIM
    nq = seq // tq
    rows = GQA_GROUP * tq
    kc = (seq + ctx_len) // n_chunks
    assert kc * n_chunks == seq + ctx_len and kc % LANES == 0
    k_blk = C_Q_HEADS
    v_blk = C_Q_HEADS + C_KV_HEADS
    vx_blk = C_KV_HEADS
    return pl.pallas_call(
        functools.partial(_dense_attn_kernel, tq=tq, n_chunks=n_chunks, kc=kc),
        out_shape=jax.ShapeDtypeStruct((batch * seq, C_Q_HEADS * HEAD_DIM), BF16),
        grid=(batch, C_KV_HEADS, nq),
        in_specs=[pl.BlockSpec((tq, gw), lambda b, h, i: (b * nq + i, h)),
                  pl.BlockSpec((seq, HEAD_DIM), lambda b, h, i: (b, k_blk + h)),
                  pl.BlockSpec((seq, HEAD_DIM), lambda b, h, i: (b, v_blk + h)),
                  pl.BlockSpec((ctx_len, HEAD_DIM), lambda b, h, i: (b, h)),
                  pl.BlockSpec((ctx_len, HEAD_DIM), lambda b, h, i: (b, vx_blk + h))],
        out_specs=pl.BlockSpec((tq, gw), lambda b, h, i: (b * nq + i, h)),
        scratch_shapes=[pltpu.VMEM((n_chunks, rows, kc), F32),
                        pltpu.VMEM((rows, LANES), F32),
                        pltpu.VMEM((rows, 2 * HEAD_DIM), F32),
                        pltpu.VMEM((n_chunks, kc, HEAD_DIM), BF16),
                        pltpu.VMEM((n_chunks, kc, 2 * HEAD_DIM), BF16)],
        compiler_params=_cparams(("parallel", "parallel", "arbitrary")),
        name="dense_attn",
    )(proj, proj, proj, proj_c, proj_c)


def _router_rows(biased, scores):
    v = [biased[e:e + 1, :] for e in range(N_EXPERTS)]
    s = [scores[e:e + 1, :] for e in range(N_EXPERTS)]

    def top2_sum(vals):
        best = vals[0] + vals[1]
        for i in range(len(vals)):
            for j in range(i + 1, len(vals)):
                if (i, j) != (0, 1):
                    best = jnp.maximum(best, vals[i] + vals[j])
        return best

    gsum = [top2_sum(v[g * EXPERTS_PER_GROUP:(g + 1) * EXPERTS_PER_GROUP]) for g in range(N_GROUPS)]
    sel = jnp.zeros_like(gsum[0], dtype=jnp.int32)
    best = gsum[0]
    for g in range(1, N_GROUPS):
        take = gsum[g] > best
        sel = jnp.where(take, g, sel)
        best = jnp.where(take, gsum[g], best)

    def pick_group(rows, i):
        out = rows[i]
        for g in range(1, N_GROUPS):
            out = jnp.where(sel == g, rows[g * EXPERTS_PER_GROUP + i], out)
        return out

    cand = [pick_group(v, i) for i in range(EXPERTS_PER_GROUP)]
    cand_s = [pick_group(s, i) for i in range(EXPERTS_PER_GROUP)]
    i1 = jnp.zeros_like(sel)
    b1 = cand[0]
    for i in range(1, EXPERTS_PER_GROUP):
        take = cand[i] > b1
        i1 = jnp.where(take, i, i1)
        b1 = jnp.where(take, cand[i], b1)
    i2 = jnp.full_like(sel, -1)
    b2 = jnp.full_like(b1, -jnp.inf)
    for i in range(EXPERTS_PER_GROUP):
        take = (i1 != i) & ((cand[i] > b2) | (i2 < 0))
        i2 = jnp.where(take, i, i2)
        b2 = jnp.where(take, cand[i], b2)

    def pick_idx(rows, idx):
        out = rows[0]
        for i in range(1, EXPERTS_PER_GROUP):
            out = jnp.where(idx == i, rows[i], out)
        return out

    s0 = pick_idx(cand_s, i1)
    s1 = pick_idx(cand_s, i2)
    tot = s0 + s1
    e0 = (sel * EXPERTS_PER_GROUP + i1).astype(F32)
    e1 = (sel * EXPERTS_PER_GROUP + i2).astype(F32)
    return e0, e1, s0 / tot, s1 / tot


def _norm_router_kernel(*refs, tm, n_lat_tiles, has_ctx, n_pending):
    i = pl.program_id(0)
    refs = list(refs)
    x_ref = refs.pop(0)
    xc_ref = refs.pop(0) if has_ctx else None
    pending = [(refs.pop(0), refs.pop(0)) for _ in range(n_pending)]
    gate_ref = refs.pop(0) if n_pending else None
    g_ref, sh_ref, sc_ref, rw_ref, rb_ref, hp_ref, r_ref = (refs.pop(0) for _ in range(7))
    xnew_ref = refs.pop(0) if n_pending else None
    h_ref, tri_ref, run_ref = refs

    def latent_rows():
        src_ref = x_ref
        if n_pending:
            tn = 512
            for j in range(x_ref.shape[1] // tn):
                cols = slice(j * tn, (j + 1) * tn)
                y = _dot(pending[0][0][...], pending[0][1][:, cols])
                for a_ref, w_ref in pending[1:]:
                    y = y + _dot(a_ref[...], w_ref[:, cols])
                xnew_ref[:, cols] = x_ref[:, cols] + gate_ref[:, cols] * y
            src_ref = xnew_ref
        _norm_mod_rows(src_ref, g_ref, sh_ref, sc_ref, h_ref, tm)

    if has_ctx:
        pl.when(i < n_lat_tiles)(latent_rows)

        @pl.when(i >= n_lat_tiles)
        def _():
            _norm_mod_rows(xc_ref, g_ref, sh_ref, sc_ref, h_ref, tm)
    else:
        latent_rows()

    @pl.when(i == 0)
    def _():
        run_ref[...] = jnp.zeros(run_ref.shape, F32)
        before = lax.broadcasted_iota(jnp.int32, (tm, tm), 0) <= lax.broadcasted_iota(jnp.int32, (tm, tm), 1)
        tri_ref[...] = jnp.where(before, 1.0, 0.0).astype(BF16)

    def pack(r, carry):
        rows = pl.ds(pl.multiple_of(r * 128, 128), 128)
        hp_ref[rows, :] = _pack_halves(h_ref[rows, :])
        return carry

    lax.fori_loop(0, tm // 128, pack, 0)
    logits = _dot_nt(rw_ref[...], h_ref[...])
    scores = jax.nn.sigmoid(logits)
    e0, e1, w0, w1 = _router_rows(scores + rb_ref[...], scores)
    expert_ids = lax.broadcasted_iota(jnp.int32, (N_EXPERTS, tm), 0).astype(F32)
    ranks = []
    for e_row in (e0, e1):
        hit = expert_ids == e_row
        seen = _dot(jnp.where(hit, 1.0, 0.0).astype(BF16), tri_ref[...])
        ranks.append(jnp.sum(jnp.where(hit, seen - 1.0 + run_ref[...], 0.0), axis=0, keepdims=True))
        run_ref[...] = run_ref[...] + seen[:, tm - 1:tm]
    zero = jnp.zeros_like(w0)
    r_ref[...] = jnp.concatenate([e0, e1, w0, w1, ranks[0], ranks[1], zero, zero], axis=0)


def _pack_halves(v):
    c = v.shape[1] // 2
    lo = lax.bitcast_convert_type(v[:, :c].astype(BF16).astype(F32), jnp.uint32)
    hi = lax.bitcast_convert_type(v[:, c:].astype(BF16).astype(F32), jnp.uint32)
    return (lo >> 16) | (hi & jnp.uint32(0xFFFF0000))


def _unpack_halves(p):
    lo = lax.bitcast_convert_type(p << 16, F32)
    hi = lax.bitcast_convert_type(p & jnp.uint32(0xFFFF0000), F32)
    return lo, hi


def _sc_gather_rows(table, idx):
    n, w = idx.shape[0], table.shape[1]
    win, nb, sub = SC_GATHER_WINDOW, SC_GATHER_BUFFERS, SC_GATHER_SUB
    per = n // (win * SC_WORKERS)
    assert per * win * SC_WORKERS == n and per >= 1
    mesh = plsc.VectorSubcoreMesh(core_axis_name="core", subcore_axis_name="subcore")

    @functools.partial(
        pl.kernel, out_type=jax.ShapeDtypeStruct((n, w), table.dtype), mesh=mesh, name="sc_gather_rows",
        scratch_types=([pltpu.VMEM((per * win,), jnp.int32)] + [pltpu.VMEM((win, w), table.dtype)] * nb
                       + [pltpu.SemaphoreType.DMA] * (2 * nb)))
    def gather(x_hbm, i_hbm, o_hbm, i_v, *rest):
        bufs, gsems, wsems = rest[:nb], rest[nb:2 * nb], rest[2 * nb:]
        wid = lax.axis_index("subcore") * SC_CORES + lax.axis_index("core")
        base = wid * (per * win)
        pltpu.sync_copy(i_hbm.at[pl.ds(base, per * win)], i_v)
        rot = (wid * per) // SC_WORKERS

        def row0(t):
            u = t + rot
            u = u - per * (u >= per).astype(jnp.int32)
            return pl.multiple_of(u * win, win)

        def gather_copy(t, b, s):
            return pltpu.make_async_copy(x_hbm.at[i_v.at[pl.ds(row0(t) + s * sub, sub)]],
                                         bufs[b].at[pl.ds(s * sub, sub)], gsems[b])

        def write_copy(t, b):
            return pltpu.make_async_copy(bufs[b], o_hbm.at[pl.ds(base + row0(t), win)], wsems[b])

        def start_gathers(t, b):
            for s in range(win // sub):
                gather_copy(t, b, s).start()

        def step(t, b):
            for s in range(win // sub):
                gather_copy(t, b, s).wait()
            write_copy(t, b).start()

            @pl.when(t >= 1)
            def _():
                write_copy(t - 1, (b - 1) % nb).wait()

            @pl.when(t + nb - 1 < per)
            def _():
                start_gathers(t + nb - 1, (b + nb - 1) % nb)

        for t in range(min(nb - 1, per)):
            start_gathers(t, t)

        @pl.loop(0, per // nb)
        def _(p):
            for j in range(nb):
                step(nb * p + j, j)

        for t in range(per - per % nb, per):
            step(t, t % nb)
        write_copy(per - 1, (per - 1) % nb).wait()

    return gather(table, idx)


def _lat_ctx_maps(n_lat, per, ctx_mod_row):
    def mod_row(i):
        return jnp.where(i < n_lat, i // per, ctx_mod_row)

    def lat(i):
        return (jnp.minimum(i, n_lat - 1), 0)

    def ctx(i):
        return (jnp.maximum(i - n_lat, 0), 0)

    return mod_row, lat, ctx


def _norm_router(x, cx, g, mod3, shift_blk, scale_blk, rw_t, rb, rows_per_mod, ctx_mod_row,
                 pending_proj=None, tm=512):
    n, d = x.shape
    n_lat = n // tm
    has_ctx = cx is not None
    ntot = n_lat + (cx.shape[0] // tm if has_ctx else 0)
    mod_row, lat, ctx = _lat_ctx_maps(n_lat, rows_per_mod // tm, ctx_mod_row)
    row_specs = [pl.BlockSpec((tm, d), lat)] + ([pl.BlockSpec((tm, d), ctx)] if has_ctx else [])
    row_args = [x] + ([cx] if has_ctx else [])
    out_shape = [jax.ShapeDtypeStruct((ntot * tm, d // 2), jnp.uint32),
                 jax.ShapeDtypeStruct((SUBLANES, ntot * tm), F32)]
    out_specs = [pl.BlockSpec((tm, d // 2), lambda i: (i, 0)),
                 pl.BlockSpec((SUBLANES, tm), lambda i: (0, i))]
    n_pending = 0
    if pending_proj is not None:
        pairs, gate_blk = pending_proj
        n_pending = len(pairs)
        for a, w in pairs:
            row_specs += [pl.BlockSpec((tm, a.shape[1]), lat),
                          pl.BlockSpec(w.shape, lambda i: (0, 0), pipeline_mode=pl.Buffered(1))]
            row_args += [a, w]
        row_specs.append(pl.BlockSpec((None, 1, d), lambda i: (jnp.minimum(i, n_lat - 1) // (rows_per_mod // tm),
                                                                0, gate_blk)))
        row_args.append(mod3)
        out_shape.append(jax.ShapeDtypeStruct((n, d), F32))
        out_specs.append(pl.BlockSpec((tm, d), lat))
    return pl.pallas_call(
        functools.partial(_norm_router_kernel, tm=tm, n_lat_tiles=n_lat, has_ctx=has_ctx,
                          n_pending=n_pending),
        out_shape=tuple(out_shape),
        grid=(ntot,),
        in_specs=row_specs + [
            pl.BlockSpec((1, d), lambda i: (0, 0)),
            pl.BlockSpec((None, 1, d), lambda i: (mod_row(i), 0, shift_blk)),
            pl.BlockSpec((None, 1, d), lambda i: (mod_row(i), 0, scale_blk)),
            pl.BlockSpec((N_EXPERTS, d), lambda i: (0, 0)),
            pl.BlockSpec((N_EXPERTS, 1), lambda i: (0, 0))],
        out_specs=tuple(out_specs),
        scratch_shapes=[pltpu.VMEM((tm, d), BF16), pltpu.VMEM((tm, tm), BF16), pltpu.VMEM((N_EXPERTS, 1), F32)],
        compiler_params=_cparams(("arbitrary",)),
        name="norm_router",
    )(*row_args, g.reshape(1, d), mod3, mod3, rw_t, rb)


def _cast_rows(src_ref, dst_ref, rb=256):
    def body(r, carry):
        rows = pl.ds(pl.multiple_of(r * rb, rb), rb)
        dst_ref[rows, :] = src_ref[0, rows, :].astype(dst_ref.dtype)
        return carry

    lax.fori_loop(0, dst_ref.shape[0] // rb, body, 0)


def _tile_state(te_ref, tv_ref, tile0):
    i = pl.program_id(0)
    t = tile0 + i
    live = tv_ref[t] > 0
    new_expert = (i == 0) | (te_ref[t] != te_ref[jnp.maximum(t - 1, 0)])
    return live, new_expert


def _gmm_kernel(te_ref, tv_ref, xs_ref, wg_ref, wu_ref, wd_ref, *rest, tile0):
    o_ref, wg_b, wu_b, wd_b = rest[-4:]
    live, new_expert = _tile_state(te_ref, tv_ref, tile0)

    @pl.when(live & new_expert)
    def _():
        _cast_rows(wg_ref, wg_b)
        _cast_rows(wu_ref, wu_b)
        _cast_rows(wd_ref, wd_b)

    @pl.when(live)
    def _():
        lo, hi = _unpack_halves(xs_ref[...])
        x = jnp.concatenate([lo.astype(BF16), hi.astype(BF16)], axis=1)
        half = wg_b.shape[1] // 2
        y = None
        for c in range(2):
            cols = slice(c * half, (c + 1) * half)
            gate = _dot(x, wg_b[:, cols])
            up = _dot(x, wu_b[:, cols])
            h1 = ((gate * jax.nn.sigmoid(gate)) * up).astype(BF16)
            part = _dot(h1, wd_b[cols, :])
            y = part if y is None else y + part
        o_ref[...] = _pack_halves(y)

    @pl.when(jnp.logical_not(live))
    def _():
        o_ref[...] = jnp.zeros(o_ref.shape, o_ref.dtype)


def _grouped_mlp(tile_expert, tile_valid, xs_part, tile0, ys_prev, wg, wu, wd, layer, tm=MOE_TM):
    dpk = xs_part.shape[1]
    d, dff = wg.shape[2], wg.shape[3]
    n_tiles = tile_expert.shape[0]

    def expert_block(rows, cols, buffers):
        return pl.BlockSpec((None, 1, rows, cols), lambda i, te, tv: (layer, te[tile0 + i], 0, 0),
                            pipeline_mode=pl.Buffered(buffers))

    in_specs = [pl.BlockSpec((tm, dpk), lambda i, te, tv: (i, 0)),
                expert_block(d, dff, 1), expert_block(d, dff, 1), expert_block(dff, d, 2)]
    args = [tile_expert, tile_valid, xs_part, wg, wu, wd]
    aliases = {}
    if ys_prev is not None:
        in_specs.append(pl.BlockSpec(memory_space=pl.ANY))
        args.append(ys_prev)
        aliases = {len(args) - 1: 0}
    grid_spec = pltpu.PrefetchScalarGridSpec(
        num_scalar_prefetch=2,
        grid=(xs_part.shape[0] // tm,),
        in_specs=in_specs,
        out_specs=pl.BlockSpec((tm, dpk), lambda i, te, tv: (tile0 + i, 0)),
        scratch_shapes=[pltpu.VMEM((d, dff), BF16), pltpu.VMEM((d, dff), BF16), pltpu.VMEM((dff, d), BF16)],
    )
    return pl.pallas_call(
        functools.partial(_gmm_kernel, tile0=tile0),
        out_shape=jax.ShapeDtypeStruct((n_tiles * tm, dpk), jnp.uint32),
        grid_spec=grid_spec,
        input_output_aliases=aliases,
        compiler_params=_cparams(("arbitrary",)),
        name="grouped_mlp",
    )(*args)


def _combine_kernel(*refs, n_lat_tiles, has_ctx):
    if has_ctx:
        x_ref, xc_ref, y0_ref, y1_ref, r_ref, g_ref, o_ref, oc_ref = refs
    else:
        x_ref, y0_ref, y1_ref, r_ref, g_ref, o_ref = refs
    w0 = r_ref[:, 2:3]
    w1 = r_ref[:, 3:4]
    lo0, hi0 = _unpack_halves(y0_ref[...])
    lo1, hi1 = _unpack_halves(y1_ref[...])
    f = g_ref[...] * jnp.concatenate([w0 * lo0 + w1 * lo1, w0 * hi0 + w1 * hi1], axis=1)
    if not has_ctx:
        o_ref[...] = x_ref[...] + f
        return
    i = pl.program_id(0)

    @pl.when(i < n_lat_tiles)
    def _():
        o_ref[...] = x_ref[...] + f

    @pl.when(i >= n_lat_tiles)
    def _():
        oc_ref[...] = xc_ref[...] + f


def _combine(x, cx, yg, route_cols, mod3, gate_blk, rows_per_mod, ctx_mod_row, tm=512):
    n, d = x.shape
    n_lat = n // tm
    has_ctx = cx is not None
    ntot = n_lat + (cx.shape[0] // tm if has_ctx else 0)
    mod_row, lat, ctx = _lat_ctx_maps(n_lat, rows_per_mod // tm, ctx_mod_row)
    row_specs = [pl.BlockSpec((tm, d), lat)] + ([pl.BlockSpec((tm, d), ctx)] if has_ctx else [])
    row_args = [x] + ([cx] if has_ctx else [])
    out_shape = [jax.ShapeDtypeStruct(x.shape, F32)] + ([jax.ShapeDtypeStruct(cx.shape, F32)] if has_ctx else [])
    out = pl.pallas_call(
        functools.partial(_combine_kernel, n_lat_tiles=n_lat, has_ctx=has_ctx),
        out_shape=tuple(out_shape),
        grid=(ntot,),
        in_specs=row_specs + [
            pl.BlockSpec((tm, d // 2), lambda i: (i, 0)),
            pl.BlockSpec((tm, d // 2), lambda i: (ntot + i, 0)),
            pl.BlockSpec((tm, SUBLANES), lambda i: (i, 0)),
            pl.BlockSpec((None, 1, d), lambda i: (mod_row(i), 0, gate_blk))],
        out_specs=tuple(row_specs),
        compiler_params=_cparams(("arbitrary",)),
        name="moe_combine",
    )(*row_args, yg, yg, route_cols, mod3)
    return out if has_ctx else (out[0], None)


def _dispatch_plan(route, tm):
    n = route.shape[1]
    e_flat = jnp.concatenate([route[0], route[1]]).astype(jnp.int32)
    rank = jnp.concatenate([route[4], route[5]]).astype(jnp.int32)
    n_assign = 2 * n
    n_tiles = n_assign // tm + N_EXPERTS
    experts = jnp.arange(N_EXPERTS, dtype=jnp.int32)[:, None]
    onehot = (experts == e_flat[None, :]).astype(jnp.int32)
    counts = jnp.sum(onehot, axis=1)
    padded = ((counts + tm - 1) // tm) * tm
    ends_p = jnp.cumsum(padded)
    starts_p = ends_p - padded
    starts_c = jnp.cumsum(counts) - counts
    dest = jnp.sum(onehot * starts_p[:, None], axis=0) + rank
    by_expert = jnp.sum(onehot * starts_c[:, None], axis=0) + rank
    order = jnp.argsort(by_expert).astype(jnp.int32)
    p = jnp.arange(n_tiles * tm, dtype=jnp.int32)[None, :]
    owner = ((p >= starts_p[:, None]) & (p < ends_p[:, None])).astype(jnp.int32)
    within = jnp.sum(owner * (p - starts_p[:, None]), axis=0)
    live = jnp.sum(owner * (p - starts_p[:, None] < counts[:, None]), axis=0) > 0
    compact = jnp.sum(owner * starts_c[:, None], axis=0) + within
    src_tok = jnp.where(live, order[jnp.clip(compact, 0, n_assign - 1)] % n, p[0] % n)
    tile_start = jnp.arange(n_tiles, dtype=jnp.int32) * tm
    tile_valid = (tile_start < ends_p[-1]).astype(jnp.int32)
    last_tile = jnp.maximum(ends_p[-1] // tm - 1, 0) * tm
    tile_row = jnp.minimum(tile_start, last_tile)[:, None]
    tile_expert = jnp.sum((ends_p[None, :] <= tile_row).astype(jnp.int32), axis=1)
    tile_expert = jnp.minimum(tile_expert, N_EXPERTS - 1)
    return src_tok, dest, tile_expert, tile_valid


def _moe(x, cx, g, mod3, rw_t, rb, wg, wu, wd, layer, rows_per_mod, ctx_mod_row, combine=True,
         pending_proj=None):
    if pending_proj is None:
        h, route = _norm_router(x, cx, g, mod3, 3, 4, rw_t, rb, rows_per_mod, ctx_mod_row)
    else:
        h, route, x = _norm_router(x, cx, g, mod3, 3, 4, rw_t, rb, rows_per_mod, ctx_mod_row, pending_proj)
    src_tok, dest, tile_expert, tile_valid = _dispatch_plan(route, MOE_TM)
    n_tiles = tile_expert.shape[0]
    bounds = [n_tiles * k // MOE_RANGES for k in range(MOE_RANGES + 1)]
    ys = None
    for t0, t1 in zip(bounds[:-1], bounds[1:]):
        xs = _sc_gather_rows(h, src_tok[t0 * MOE_TM:t1 * MOE_TM])
        ys = _grouped_mlp(tile_expert, tile_valid, xs, t0, ys, wg, wu, wd, layer)
    yg = _sc_gather_rows(ys, dest)
    route_cols = route.T
    if not combine:
        return yg, route_cols, x
    return _combine(x, cx, yg, route_cols, mod3, 5, rows_per_mod, ctx_mod_row)


def _rope_tables(seq):
    rows = seq // GRID_W
    row = jnp.repeat(jnp.arange(rows, dtype=F32), GRID_W)
    col = jnp.tile(jnp.arange(GRID_W, dtype=F32), rows)
    n_freq = HEAD_DIM // 4
    inv_freq = ROPE_BASE ** (-jnp.arange(n_freq, dtype=F32) / n_freq)
    ang = jnp.concatenate([row[:, None] * inv_freq, col[:, None] * inv_freq], axis=-1)
    cos, sin = jnp.cos(ang), jnp.sin(ang)
    return jnp.concatenate([cos, cos], axis=-1), jnp.concatenate([-sin, sin], axis=-1)


def kernel(x, c, ctx, c_ctx, ada_w, ada_b, norm_mix, norm_ffn, ab_w_in, ab_q_gain, ab_k_gain, ab_sink, ab_conv_w, ab_conv_b, ab_gate_a_w, ab_gate_a_b, ab_gate_x_w, ab_gate_x_b, ab_lru_lambda, ab_w_out, gqa_w_in, gqa_q_gain, gqa_k_gain, gqa_w_out, router_w, router_bias, moe_w_gate, moe_w_up, moe_w_down):
    batch, seq, d = x.shape
    ctx_len = ctx.shape[1]
    depth = ada_w.shape[0]
    assert depth == 2 and batch < SUBLANES
    n_lat = batch * seq
    n_ctx = batch * ctx_len
    ctx_row = batch

    xl = x.reshape(n_lat, d)
    xc = ctx.reshape(n_ctx, d)
    cc = jnp.zeros((SUBLANES, d), F32).at[:batch].set(c).at[ctx_row].set(c_ctx)
    mod = _ada(cc, ada_w, ada_b)
    cos2, sin2 = _rope_tables(seq)
    rw_t = router_w.T.astype(BF16)
    rb = router_bias.reshape(N_EXPERTS, 1).astype(F32)
    experts = (moe_w_gate, moe_w_up, moe_w_down)

    mod3 = mod[0].reshape(SUBLANES, 1, 6 * d)
    w_in = ab_w_in[0].astype(BF16)
    proj = _norm_mod_matmul(xl, norm_mix[0], mod3, 0, 1, w_in, seq, 0,
                            A_Q_HEADS, A_KV_HEADS, ab_q_gain[0], ab_k_gain[0], cos2, sin2)
    proj_c = _norm_mod_matmul(xc, norm_mix[0], mod3, 0, 1, w_in, n_ctx, ctx_row,
                              A_Q_HEADS, A_KV_HEADS, ab_q_gain[0], ab_k_gain[0])
    att = _win_attn(ab_sink[0], proj, proj_c, batch, seq, ctx_len)
    att_c = _ctx_attn(ab_sink[0], proj_c, batch, ctx_len)

    lru_w = ab_conv_w.shape[2]
    c0 = (A_Q_HEADS + 2 * A_KV_HEADS) * HEAD_DIM
    y_p, yc_p = _lru(_to_chunked(proj[:, c0:c0 + lru_w], batch),
                     _to_chunked(proj[:, c0 + lru_w:c0 + 2 * lru_w], batch),
                     _to_chunked(proj_c[:, c0:c0 + lru_w], batch),
                     _to_chunked(proj_c[:, c0 + lru_w:c0 + 2 * lru_w], batch),
                     ab_conv_w[0], ab_conv_b[0], ab_gate_a_w[0], ab_gate_a_b[0],
                     ab_gate_x_w[0], ab_gate_x_b[0], ab_lru_lambda[0])
    w_out = ab_w_out[0].astype(BF16)
    xc = _out_proj(att_c, 0, _from_chunked(yc_p), 0, w_out, xc, mod3, 2, n_ctx, ctx_row)
    half = w_out.shape[0] // 2
    yg, route_cols, xl = _moe(
        xl, xc, norm_ffn[0], mod3, rw_t, rb, *experts, 0, seq, ctx_row, combine=False,
        pending_proj=([(att, w_out[:half]), (_from_chunked(y_p), w_out[half:])], 2))

    mod3_prev, mod3 = mod3, mod[1].reshape(SUBLANES, 1, 6 * d)
    w_in = gqa_w_in[0].astype(BF16)
    cw = C_Q_HEADS * HEAD_DIM
    proj, xl = _norm_mod_matmul(xl, norm_mix[1], mod3, 0, 1, w_in, seq, 0,
                                C_Q_HEADS, C_KV_HEADS, gqa_q_gain[0], gqa_k_gain[0], cos2, sin2,
                                pending_moe=(yg, route_cols, 0, mod3_prev, 5))
    proj_c, _ = _norm_mod_matmul(xc, norm_mix[1], mod3, 0, 1, w_in[:, cw:], n_ctx, ctx_row,
                                 0, C_KV_HEADS, gqa_q_gain[0], gqa_k_gain[0],
                                 pending_moe=(yg, route_cols, n_lat, mod3_prev, 5))
    att = _dense_attn(proj, proj_c, batch, seq, ctx_len)
    xl, _ = _moe(xl, None, norm_ffn[1], mod3, rw_t, rb, *experts, 1, seq, ctx_row,
                 pending_proj=([(att, gqa_w_out[0].astype(BF16))], 2))
    return xl.reshape(batch, seq, d)
```
